```python
import jax, jax.numpy as jnp
from jax import lax
import numpy as np

D_MODEL = 1024
BATCH = 8
SEQ = 2048
DEPTH = 2

GRID_W = 64
CTX_LEN = 256
HEAD_DIM = 64
Q_BLOCK = 128
ROPE_THETA = 10000.0
EPS = 1e-6
NEG = -1e30

A_HEADS = 4
A_KV_HEADS = 2
B_HEADS = 4
B_NOPE = 64
B_ROPE = 32
B_V = 64
B_KV_RANK = 128
C_HEADS = 4
C_KV_HEADS = 2
C_WINDOW = 128
D_HEADS = 4
NA_ROWS = 8
NA_COLS = 16
NA_QCOLS = 16
NA_KCOLS = NA_QCOLS + NA_COLS

N_EXPERTS = 64
TOP_K = 8
N_GROUPS = 8
TOPK_GROUPS = 4
EXPERT_DIM = 256
SHARED_DIM = 256
ROUTED_SCALE = 2.5
MOE_BLOCK = 128

A_Q = A_HEADS * HEAD_DIM
A_KV = A_KV_HEADS * HEAD_DIM
B_Q = B_HEADS * (B_NOPE + B_ROPE)
C_Q = C_HEADS * HEAD_DIM
C_KV = C_KV_HEADS * HEAD_DIM
D_QKV = D_HEADS * HEAD_DIM
IN_SPLITS = (A_Q, A_KV, A_KV, B_Q, B_KV_RANK, B_ROPE, C_Q, C_KV, C_KV, D_QKV, D_QKV, D_QKV)
IN_WIDTH = A_Q + 2 * A_KV + B_Q + B_KV_RANK + B_ROPE + C_Q + 2 * C_KV + 3 * D_QKV
MIX_WIDTH = A_HEADS * HEAD_DIM + B_HEADS * B_V + C_HEADS * HEAD_DIM + D_HEADS * HEAD_DIM

kernel_name = "hybrid_parallel_heads_moe_dit_block"


def _rms_norm(x, g):
    xf = x.astype(jnp.float32)
    y = xf * lax.rsqrt(jnp.mean(xf * xf, axis=-1, keepdims=True) + EPS)
    return (y * g.astype(jnp.float32)).astype(x.dtype)


def _axial_rope(n_tok, dim):
    t = jnp.arange(n_tok)
    row = (t // GRID_W).astype(jnp.float32)
    col = (t % GRID_W).astype(jnp.float32)
    quarter = dim // 4
    freqs = ROPE_THETA ** (-jnp.arange(quarter, dtype=jnp.float32) / quarter)
    ar = row[:, None] * freqs
    ac = col[:, None] * freqs
    ang = jnp.concatenate([ar, ar, ac, ac], axis=-1)
    return jnp.cos(ang), jnp.sin(ang)


def _rope(x, cos, sin):
    xf = x.astype(jnp.float32)
    x1, x2, x3, x4 = jnp.split(xf, 4, axis=-1)
    rot = jnp.concatenate([-x2, x1, -x4, x3], axis=-1)
    return (xf * cos + rot * sin).astype(x.dtype)


def _heads(x, n):
    b, t, _ = x.shape
    return x.reshape(b, t, n, -1).transpose(0, 2, 1, 3)


def _merge(x):
    b, h, t, d = x.shape
    return x.transpose(0, 2, 1, 3).reshape(b, t, h * d)


def _group(x, hk):
    b, h, t, d = x.shape
    return x.reshape(b, hk, h // hk, t, d)


def _ungroup(x):
    b, hk, g, t, d = x.shape
    return x.reshape(b, hk * g, t, d)


def _split_cols(p):
    offs, acc = [], 0
    for w in IN_SPLITS[:-1]:
        acc += w
        offs.append(acc)
    return jnp.split(p, offs, axis=-1)


def _dense_block_attn(q, k, v, scale):
    bsz, hk, g, sq, dk = q.shape
    nb = sq // Q_BLOCK
    qb = jnp.moveaxis(q.reshape(bsz, hk, g, nb, Q_BLOCK, dk), 3, 0)

    def one_block(qi):
        s = jnp.einsum("bhgqd,bhkd->bhgqk", qi, k, preferred_element_type=jnp.float32) * scale
        p = jax.nn.softmax(s, axis=-1).astype(v.dtype)
        return jnp.einsum("bhgqk,bhkd->bhgqd", p, v)

    o = lax.map(one_block, qb)
    return jnp.moveaxis(o, 0, 3).reshape(bsz, hk, g, sq, v.shape[-1])


def _ctx_attn(q, k, v, scale, sink=None):
    s = jnp.einsum("bhgqd,bhkd->bhgqk", q, k, preferred_element_type=jnp.float32) * scale
    if sink is not None:
        s_sink = jnp.broadcast_to(sink.astype(jnp.float32)[None, :, :, None, None], s.shape[:-1] + (1,))
        s = jnp.concatenate([s, s_sink], axis=-1)
    p = jax.nn.softmax(s, axis=-1)[..., :k.shape[2]].astype(v.dtype)
    return jnp.einsum("bhgqk,bhkd->bhgqd", p, v)


def _window_attn(q, k, v, kc, vc, sink, scale):
    bsz, hk, g, s_len, d = q.shape
    halo = -(-C_WINDOW // Q_BLOCK)
    nb = s_len // Q_BLOCK
    nkb = (2 * halo + 1) * Q_BLOCK
    pad = ((0, 0), (0, 0), (halo * Q_BLOCK, halo * Q_BLOCK), (0, 0))

    def band(t):
        tp = jnp.pad(t, pad).reshape(bsz, hk, nb + 2 * halo, Q_BLOCK, t.shape[-1])
        return jnp.concatenate([tp[:, :, j:j + nb] for j in range(2 * halo + 1)], axis=3)

    kb, vb = band(k), band(v)
    qb = q.reshape(bsz, hk, g, nb, Q_BLOCK, d)
    s_band = jnp.einsum("bhgnqd,bhnkd->bhgnqk", qb, kb, preferred_element_type=jnp.float32) * scale
    qpos = (jnp.arange(nb)[:, None] * Q_BLOCK + jnp.arange(Q_BLOCK))[:, :, None]
    kpos = (jnp.arange(nb)[:, None] * Q_BLOCK - halo * Q_BLOCK + jnp.arange(nkb))[:, None, :]
    valid = (kpos >= 0) & (kpos < s_len) & (jnp.abs(qpos - kpos) <= C_WINDOW)
    s_band = jnp.where(valid, s_band, NEG)
    s_ctx = jnp.einsum("bhgnqd,bhld->bhgnql", qb, kc, preferred_element_type=jnp.float32) * scale
    s_sink = jnp.broadcast_to(sink.astype(jnp.float32)[None, :, :, None, None, None], s_band.shape[:-1] + (1,))
    p = jax.nn.softmax(jnp.concatenate([s_band, s_ctx, s_sink], axis=-1), axis=-1)
    n_ctx = kc.shape[2]
    o = (jnp.einsum("bhgnqk,bhnkd->bhgnqd", p[..., :nkb].astype(v.dtype), vb)
         + jnp.einsum("bhgnql,bhld->bhgnqd", p[..., nkb:nkb + n_ctx].astype(vc.dtype), vc))
    return o.reshape(bsz, hk, g, s_len, d)


def _neighbourhood_attn(q, k, v, kc, vc, rpb, scale):
    bsz, nh, s_len, d = q.shape
    rows = s_len // GRID_W
    wr = min(NA_ROWS, rows)
    ncb = GRID_W // NA_QCOLS
    r = jnp.arange(rows)
    rs = jnp.clip(r - wr // 2, 0, rows - wr)
    key_rows = rs[:, None] + jnp.arange(wr)
    cb = jnp.arange(ncb)
    c0 = jnp.clip(cb * NA_QCOLS - NA_COLS // 2, 0, GRID_W - NA_KCOLS)
    key_cols = c0[:, None] + jnp.arange(NA_KCOLS)
    nk = wr * NA_KCOLS
    key_idx = (key_rows[:, None, :, None] * GRID_W + key_cols[None, :, None, :]).reshape(rows, ncb, nk)
    kg = jnp.take(k, key_idx, axis=2)
    vg = jnp.take(v, key_idx, axis=2)
    qb = q.reshape(bsz, nh, rows, ncb, NA_QCOLS, d)
    s_nb = jnp.einsum("bhrcqd,bhrckd->bhrcqk", qb, kg, preferred_element_type=jnp.float32) * scale
    q_cols = cb[:, None] * NA_QCOLS + jnp.arange(NA_QCOLS)
    q_cs = jnp.clip(q_cols - NA_COLS // 2, 0, GRID_W - NA_COLS)
    kcol = key_cols[:, None, :]
    col_ok = (kcol >= q_cs[:, :, None]) & (kcol < q_cs[:, :, None] + NA_COLS)
    mask = jnp.broadcast_to(col_ok[:, :, None, :], (ncb, NA_QCOLS, wr, NA_KCOLS)).reshape(ncb, NA_QCOLS, nk)
    dr = key_rows - r[:, None] + NA_ROWS - 1
    dc = jnp.clip(kcol - q_cols[:, :, None] + NA_COLS - 1, 0, 2 * NA_COLS - 2)
    bias = rpb[:, dr[:, None, None, :, None], dc[None, :, :, None, :]]
    bias = bias.reshape(nh, rows, ncb, NA_QCOLS, nk).astype(jnp.float32)
    s_nb = jnp.where(mask, s_nb + bias, NEG)
    s_ctx = jnp.einsum("bhrcqd,bhld->bhrcql", qb, kc, preferred_element_type=jnp.float32) * scale
    p = jax.nn.softmax(jnp.concatenate([s_nb, s_ctx], axis=-1), axis=-1)
    o = (jnp.einsum("bhrcqk,bhrckd->bhrcqd", p[..., :nk].astype(v.dtype), vg)
         + jnp.einsum("bhrcql,bhld->bhrcqd", p[..., nk:].astype(vc.dtype), vc))
    return o.reshape(bsz, nh, s_len, d)


def _mixer_full_gqa(q, k, v, qc, kc, vc, q_gain, k_gain, cos, sin, with_ctx):
    scale = HEAD_DIM ** -0.5
    bsz, s_len, _ = q.shape
    ql = _rope(_rms_norm(_heads(q, A_HEADS), q_gain), cos, sin)
    kl = _rope(_rms_norm(_heads(k, A_KV_HEADS), k_gain), cos, sin)
    kcx = _rms_norm(_heads(kc, A_KV_HEADS), k_gain)
    vl, vcx = _heads(v, A_KV_HEADS), _heads(vc, A_KV_HEADS)
    o = _dense_block_attn(_group(ql, A_KV_HEADS), jnp.concatenate([kl, kcx], axis=2),
                          jnp.concatenate([vl, vcx], axis=2), scale)
    out = _merge(_ungroup(o))
    out_c = None
    if with_ctx:
        qcx = _rms_norm(_heads(qc, A_HEADS), q_gain)
        out_c = _merge(_ungroup(_ctx_attn(_group(qcx, A_KV_HEADS), kcx, vcx, scale)))
    return out, out_c


def _mixer_mla(q, ckv, kr, qc, ckvc, krc, kv_gain, w_kv_up, cos, sin, with_ctx):
    scale = (B_NOPE + B_ROPE) ** -0.5

    def expand(c_lat, k_rope):
        kv = _heads(_rms_norm(c_lat, kv_gain) @ w_kv_up, B_HEADS)
        kr_h = jnp.broadcast_to(k_rope[:, None], kv.shape[:-1] + (B_ROPE,))
        return jnp.concatenate([kv[..., :B_NOPE], kr_h], axis=-1), kv[..., B_NOPE:]

    kl, vl = expand(ckv, _rope(kr, cos, sin))
    kcx, vcx = expand(ckvc, krc)
    qh = _heads(q, B_HEADS)
    ql = jnp.concatenate([qh[..., :B_NOPE], _rope(qh[..., B_NOPE:], cos, sin)], axis=-1)
    o = _dense_block_attn(ql[:, :, None], jnp.concatenate([kl, kcx], axis=2),
                          jnp.concatenate([vl, vcx], axis=2), scale)
    out = _merge(o[:, :, 0])
    out_c = None
    if with_ctx:
        out_c = _merge(_ctx_attn(_heads(qc, B_HEADS)[:, :, None], kcx, vcx, scale)[:, :, 0])
    return out, out_c


def _mixer_window_gqa(q, k, v, qc, kc, vc, sink, cos, sin, with_ctx):
    scale = HEAD_DIM ** -0.5
    sink_g = sink.reshape(C_KV_HEADS, C_HEADS // C_KV_HEADS)
    ql = _rope(_heads(q, C_HEADS), cos, sin)
    kl = _rope(_heads(k, C_KV_HEADS), cos, sin)
    vl = _heads(v, C_KV_HEADS)
    kcx, vcx = _heads(kc, C_KV_HEADS), _heads(vc, C_KV_HEADS)
    out = _merge(_ungroup(_window_attn(_group(ql, C_KV_HEADS), kl, vl, kcx, vcx, sink_g, scale)))
    out_c = None
    if with_ctx:
        qcx = _group(_heads(qc, C_HEADS), C_KV_HEADS)
        out_c = _merge(_ungroup(_ctx_attn(qcx, kcx, vcx, scale, sink_g)))
    return out, out_c


def _mixer_neighbourhood(q, k, v, qc, kc, vc, rpb, with_ctx):
    scale = HEAD_DIM ** -0.5
    kcx, vcx = _heads(kc, D_HEADS), _heads(vc, D_HEADS)
    out = _merge(_neighbourhood_attn(_heads(q, D_HEADS), _heads(k, D_HEADS), _heads(v, D_HEADS),
                                     kcx, vcx, rpb, scale))
    out_c = None
    if with_ctx:
        out_c = _merge(_ctx_attn(_heads(qc, D_HEADS)[:, :, None], kcx, vcx, scale)[:, :, 0])
    return out, out_c


def _route(h, w_r, b_r):
    s = jax.nn.sigmoid(jnp.matmul(h, w_r, preferred_element_type=jnp.float32))
    sb = s + b_r.astype(jnp.float32)
    n_tok = s.shape[0]
    per = N_EXPERTS // N_GROUPS
    gscore = lax.top_k(sb.reshape(n_tok, N_GROUPS, per), 2)[0].sum(-1)
    _, gidx = lax.top_k(gscore, TOPK_GROUPS)
    gmask = jnp.sum(jax.nn.one_hot(gidx, N_GROUPS, dtype=jnp.float32), axis=1)
    emask = jnp.repeat(gmask, per, axis=1) > 0
    _, idx = lax.top_k(jnp.where(emask, sb, -jnp.inf), TOP_K)
    w = jnp.take_along_axis(s, idx, axis=1)
    w = w / jnp.sum(w, axis=-1, keepdims=True) * ROUTED_SCALE
    return idx, w


def _swiglu(x, wg, wu, wd):
    return (jax.nn.silu(x @ wg) * (x @ wu)) @ wd


def _moe_routed(h, idx, wts, wg, wu, wd):
    n_tok, dm = h.shape
    n_asg = n_tok * TOP_K
    flat_e = idx.reshape(-1)
    order = jnp.argsort(flat_e)
    sorted_e = flat_e[order]
    counts = jnp.bincount(flat_e, length=N_EXPERTS)
    padded = (counts + MOE_BLOCK - 1) // MOE_BLOCK * MOE_BLOCK
    pad_end = jnp.cumsum(padded)
    pad_start = pad_end - padded
    raw_start = jnp.cumsum(counts) - counts
    slot = pad_start[sorted_e] + jnp.arange(n_asg) - raw_start[sorted_e]
    n_blocks = -(-(n_asg + N_EXPERTS * (MOE_BLOCK - 1)) // MOE_BLOCK)
    n_slots = n_blocks * MOE_BLOCK
    slot_tok = jnp.full((n_slots,), n_tok, jnp.int32).at[slot].set((order // TOP_K).astype(jnp.int32))
    slot_w = jnp.zeros((n_slots,), wts.dtype).at[slot].set(wts.reshape(-1)[order])
    block_e = jnp.minimum(jnp.searchsorted(pad_end, jnp.arange(n_blocks) * MOE_BLOCK, side="right"),
                          N_EXPERTS - 1)
    h_pad = jnp.concatenate([h, jnp.zeros((1, dm), h.dtype)], axis=0)

    def step(acc, blk):
        tok, w, e = blk
        xb = h_pad[tok]
        y = (jax.nn.silu(xb @ wg[e]) * (xb @ wu[e])) @ wd[e]
        return acc.at[tok].add(y * w[:, None].astype(y.dtype)), None

    acc, _ = lax.scan(step, jnp.zeros((n_tok + 1, dm), h.dtype),
                      (slot_tok.reshape(n_blocks, MOE_BLOCK), slot_w.reshape(n_blocks, MOE_BLOCK), block_e))
    return acc[:n_tok]


def _moe(h, w_r, b_r, wg, wu, wd, swg, swu, swd):
    idx, wts = _route(h, w_r, b_r)
    return _moe_routed(h, idx, wts, wg, wu, wd) + _swiglu(h, swg, swu, swd)


def setup_inputs(seed: int = 0) -> dict:
    key = jax.random.key(seed)
    ks = jax.random.split(key, 32)
    f32 = jnp.float32
    L = DEPTH

    def nrm(k, shape, scale):
        return jax.random.normal(k, shape, f32) * scale

    return {
        "x": nrm(ks[0], (BATCH, SEQ, D_MODEL), 1.0),
        "c": nrm(ks[1], (BATCH, D_MODEL), 1.0),
        "ctx": nrm(ks[2], (BATCH, CTX_LEN, D_MODEL), 1.0),
        "c_ctx": nrm(ks[3], (D_MODEL,), 1.0),
        "ada_w": nrm(ks[4], (L, D_MODEL, 6 * D_MODEL), 0.5 * D_MODEL ** -0.5),
        "ada_b": nrm(ks[5], (L, 6 * D_MODEL), 0.01),
        "pre_mix_g": 1.0 + nrm(ks[6], (L, D_MODEL), 0.1),
        "post_mix_g": 1.0 + nrm(ks[7], (L, D_MODEL), 0.1),
        "pre_ffn_g": 1.0 + nrm(ks[8], (L, D_MODEL), 0.1),
        "post_ffn_g": 1.0 + nrm(ks[9], (L, D_MODEL), 0.1),
        "w_in": nrm(ks[10], (L, D_MODEL, IN_WIDTH), D_MODEL ** -0.5),
        "a_q_norm": 1.0 + nrm(ks[11], (L, HEAD_DIM), 0.1),
        "a_k_norm": 1.0 + nrm(ks[12], (L, HEAD_DIM), 0.1),
        "b_kv_norm": 1.0 + nrm(ks[13], (L, B_KV_RANK), 0.1),
        "b_w_kv_up": nrm(ks[14], (L, B_KV_RANK, B_HEADS * (B_NOPE + B_V)), B_KV_RANK ** -0.5),
        "c_sink": nrm(ks[15], (L, C_HEADS), 1.0),
        "d_rpb": nrm(ks[16], (L, D_HEADS, 2 * NA_ROWS - 1, 2 * NA_COLS - 1), 0.2),
        "w_out": nrm(ks[17], (L, MIX_WIDTH, D_MODEL), MIX_WIDTH ** -0.5),
        "router_w": nrm(ks[18], (L, D_MODEL, N_EXPERTS), D_MODEL ** -0.5),
        "router_b": nrm(ks[19], (L, N_EXPERTS), 0.01),
        "exp_w_gate": nrm(ks[20], (L, N_EXPERTS, D_MODEL, EXPERT_DIM), D_MODEL ** -0.5),
        "exp_w_up": nrm(ks[21], (L, N_EXPERTS, D_MODEL, EXPERT_DIM), D_MODEL ** -0.5),
        "exp_w_down": nrm(ks[22], (L, N_EXPERTS, EXPERT_DIM, D_MODEL), EXPERT_DIM ** -0.5),
        "sh_w_gate": nrm(ks[23], (L, D_MODEL, SHARED_DIM), D_MODEL ** -0.5),
        "sh_w_up": nrm(ks[24], (L, D_MODEL, SHARED_DIM), D_MODEL ** -0.5),
        "sh_w_down": nrm(ks[25], (L, SHARED_DIM, D_MODEL), SHARED_DIM ** -0.5),
    }


def reference(x, c, ctx, c_ctx, ada_w, ada_b, pre_mix_g, post_mix_g, pre_ffn_g, post_ffn_g,
              w_in, a_q_norm, a_k_norm, b_kv_norm, b_w_kv_up, c_sink, d_rpb, w_out,
              router_w, router_b, exp_w_gate, exp_w_up, exp_w_down, sh_w_gate, sh_w_up, sh_w_down):
    bsz, s_len, dm = x.shape
    cos_h, sin_h = _axial_rope(s_len, HEAD_DIM)
    cos_r, sin_r = _axial_rope(s_len, B_ROPE)
    xc = ctx
    c_act = jax.nn.silu(c)
    cc_act = jax.nn.silu(c_ctx)
    for l in range(DEPTH):
        with_ctx = l < DEPTH - 1
        mod = (c_act @ ada_w[l] + ada_b[l])[:, None, :]
        modc = cc_act @ ada_w[l] + ada_b[l]
        sh1, sc1, g1, sh2, sc2, g2 = jnp.split(mod, 6, axis=-1)
        sh1c, sc1c, g1c, sh2c, sc2c, g2c = jnp.split(modc, 6, axis=-1)

        h = _rms_norm(x, pre_mix_g[l]) * (1.0 + sc1) + sh1
        hc = _rms_norm(xc, pre_mix_g[l]) * (1.0 + sc1c) + sh1c
        p = _split_cols(h @ w_in[l])
        pc = _split_cols(hc @ w_in[l])
        oa, oa_c = _mixer_full_gqa(p[0], p[1], p[2], pc[0], pc[1], pc[2], a_q_norm[l], a_k_norm[l],
                                   cos_h, sin_h, with_ctx)
        ob, ob_c = _mixer_mla(p[3], p[4], p[5], pc[3], pc[4], pc[5], b_kv_norm[l], b_w_kv_up[l],
                              cos_r, sin_r, with_ctx)
        ow, ow_c = _mixer_window_gqa(p[6], p[7], p[8], pc[6], pc[7], pc[8], c_sink[l], cos_h, sin_h, with_ctx)
        on, on_c = _mixer_neighbourhood(p[9], p[10], p[11], pc[9], pc[10], pc[11], d_rpb[l], with_ctx)
        mix = jnp.concatenate([oa, ob, ow, on], axis=-1) @ w_out[l]
        x = x + g1 * _rms_norm(mix, post_mix_g[l])
        if with_ctx:
            mixc = jnp.concatenate([oa_c, ob_c, ow_c, on_c], axis=-1) @ w_out[l]
            xc = xc + g1c * _rms_norm(mixc, post_mix_g[l])

        h2 = (_rms_norm(x, pre_ffn_g[l]) * (1.0 + sc2) + sh2).reshape(-1, dm)
        if with_ctx:
            h2c = (_rms_norm(xc, pre_ffn_g[l]) * (1.0 + sc2c) + sh2c).reshape(-1, dm)
            tokens = jnp.concatenate([h2, h2c], axis=0)
        else:
            tokens = h2
        f = _moe(tokens, router_w[l], router_b[l], exp_w_gate[l], exp_w_up[l], exp_w_down[l],
                 sh_w_gate[l], sh_w_up[l], sh_w_down[l])
        n_lat = bsz * s_len
        x = x + g2 * _rms_norm(f[:n_lat].reshape(bsz, s_len, dm), post_ffn_g[l])
        if with_ctx:
            xc = xc + g2c * _rms_norm(f[n_lat:].reshape(bsz, -1, dm), post_ffn_g[l])
    return x
```

```python
import functools

import numpy as np
import jax
import jax.numpy as jnp
from jax import lax
from jax.experimental import pallas as pl
from jax.experimental.pallas import tpu as pltpu

F32 = jnp.float32
BF16 = jnp.bfloat16

D_MODEL = 1024
SEQ = 2048
CTX_LEN = 256
TOK = SEQ + CTX_LEN
GRID_W = 64
HEAD_DIM = 64
ROPE_THETA = 10000.0
EPS = 1e-6
NEG = -1e30

A_HEADS, A_KV_HEADS = 4, 2
B_HEADS, B_NOPE, B_ROPE, B_V, B_KV_RANK = 4, 64, 32, 64, 128
C_HEADS, C_KV_HEADS, C_WINDOW = 4, 2, 128
D_HEADS, NA_ROWS, NA_COLS = 4, 8, 16

N_EXPERTS, TOP_K, N_GROUPS, TOPK_GROUPS = 64, 8, 8, 4
EXPERT_DIM, SHARED_DIM = 256, 256
ROUTED_SCALE = 2.5

LANES = 128
QBLK = 128
ROW_BLK = 256
MOE_BLK = 256
CMB_BLK = 128
NQ = SEQ // QBLK
NA_WIN = 10 * GRID_W

_AQ, _AK, _AV, _BQ, _BC, _BR = 0, 256, 384, 512, 1024, 1152
_CQ, _CK, _CV, _DQ, _DK, _DV = 1280, 1536, 1664, 1792, 2048, 2304
IN_W = 2560

_NT = (((1,), (1,)), ((), ()))


def _sigmoid(x):
    return 1.0 / (1.0 + jnp.exp(-x))


def _rms(x, g):
    return x * lax.rsqrt(jnp.mean(x * x, axis=-1, keepdims=True) + EPS) * g


def _dot(a, b):
    return jnp.dot(a, b, preferred_element_type=F32)


def _dot_nt(a, b):
    return lax.dot_general(a, b, _NT, preferred_element_type=F32)


def _ada_kernel(c_ref, w_ref, b_ref, o_ref):
    c = c_ref[...]
    a = (c * _sigmoid(c)).astype(BF16)
    o_ref[...] = _dot(a, w_ref[...].astype(BF16)) + b_ref[...]


def _ada(cvec, ada_w, ada_b):
    depth = ada_w.shape[0]
    n = ada_w.shape[2]
    tn = 1536
    return pl.pallas_call(
        _ada_kernel,
        grid=(depth, n // tn),
        in_specs=[
            pl.BlockSpec((16, D_MODEL), lambda l, j: (0, 0)),
            pl.BlockSpec((None, D_MODEL, tn), lambda l, j: (l, 0, j)),
            pl.BlockSpec((None, 1, tn), lambda l, j: (l, 0, j)),
        ],
        out_specs=pl.BlockSpec((None, 16, tn), lambda l, j: (l, 0, j)),
        out_shape=jax.ShapeDtypeStruct((depth, 16, n), F32),
        name="ada",
    )(cvec, ada_w, ada_b.reshape(depth, 1, n))


def _rope(x, cos, sin, chunk):
    outs = []
    for c in range(x.shape[1] // LANES):
        sl = slice(c * LANES, (c + 1) * LANES)
        xs = x[:, sl]
        lane = lax.broadcasted_iota(jnp.int32, xs.shape, 1)
        first = (lane & chunk) == 0
        rot = jnp.where(first, pltpu.roll(xs, LANES - chunk, 1), pltpu.roll(xs, chunk, 1))
        outs.append(xs * cos[:, sl] + rot * sin[:, sl])
    return outs[0] if len(outs) == 1 else jnp.concatenate(outs, axis=1)


def _head_norm(x, avg):
    sq = x * x
    hi = sq.astype(BF16)
    lo = (sq - hi.astype(F32)).astype(BF16)
    ms = _dot(hi, avg) + _dot(lo, avg)
    return x * lax.rsqrt(ms + EPS)


def _inproj_kernel(x_ref, mod_ref, g_ref, w_ref, qg_ref, kg_ref, kvg_ref, wkv_ref, avg_ref,
                   cosh_ref, sinh_ref, cosb_ref, sinb_ref, cosr_ref, sinr_ref,
                   qa_ref, ka_ref, va_ref, qb_ref, kb_ref, vb_ref,
                   qc_ref, kc_ref, vc_ref, qd_ref, kd_ref, vd_ref):
    d = D_MODEL
    h = _rms(x_ref[...], g_ref[...]) * (1.0 + mod_ref[:, d:2 * d]) + mod_ref[:, 0:d]
    p = _dot(h.astype(BF16), w_ref[...])
    cosh, sinh = cosh_ref[...], sinh_ref[...]
    avg = avg_ref[...]

    qa = _head_norm(p[:, _AQ:_AQ + 256], avg) * qg_ref[...]
    qa_ref[...] = _rope(qa, cosh, sinh, 16).astype(BF16)
    ka = _head_norm(p[:, _AK:_AK + 128], avg[0:128, 0:128]) * kg_ref[...]
    ka_ref[...] = _rope(ka, cosh[:, 0:128], sinh[:, 0:128], 16).astype(BF16)
    va_ref[...] = p[:, _AV:_AV + 128].astype(BF16)

    qb_ref[...] = _rope(p[:, _BQ:_BQ + 512], cosb_ref[...], sinb_ref[...], 8).astype(BF16)
    cn = _rms(p[:, _BC:_BC + 128], kvg_ref[...])
    kv = _dot(cn.astype(BF16), wkv_ref[...])
    kr = _rope(p[:, _BR:_BR + 128], cosr_ref[...], sinr_ref[...], 8)
    kr = pltpu.roll(kr, 64, 1)
    kb_ref[...] = jnp.concatenate(
        [kv[:, hh * 128:(hh + 1) * 128] + kr for hh in range(B_HEADS)], axis=1).astype(BF16)
    vb_ref[...] = kv[:, 512:768].astype(BF16)

    qc_ref[...] = _rope(p[:, _CQ:_CQ + 256] * 0.125, cosh, sinh, 16).astype(BF16)
    kc_ref[...] = _rope(p[:, _CK:_CK + 128], cosh[:, 0:128], sinh[:, 0:128], 16).astype(BF16)
    vc_ref[...] = p[:, _CV:_CV + 128].astype(BF16)

    qd_ref[...] = (p[:, _DQ:_DQ + 256] * 0.125).astype(BF16)
    kd_ref[...] = p[:, _DK:_DK + 256].astype(BF16)
    vd_ref[...] = p[:, _DV:_DV + 256].astype(BF16)


def _mod_row(j, b, n_lat_blocks):
    return jnp.where(j < n_lat_blocks, b, 8)


def _inproj(xall, mod3, g, w, qg, kg, kvg, wkv, avg, tabs):
    bsz = xall.shape[0]
    nb = TOK // ROW_BLK
    n_lat = SEQ // ROW_BLK
    full = lambda shape: pl.BlockSpec(shape, lambda b, j: (0,) * len(shape))
    row = lambda w_: pl.BlockSpec((ROW_BLK, w_), lambda b, j: (j, 0))
    out = lambda w_: pl.BlockSpec((None, ROW_BLK, w_), lambda b, j: (b, j, 0))
    widths = (256, 128, 128, 512, 512, 256, 256, 128, 128, 256, 256, 256)
    return pl.pallas_call(
        _inproj_kernel,
        grid=(bsz, nb),
        in_specs=[
            pl.BlockSpec((None, ROW_BLK, D_MODEL), lambda b, j: (b, j, 0)),
            pl.BlockSpec((None, 1, 6 * D_MODEL), lambda b, j: (_mod_row(j, b, n_lat), 0, 0)),
            full((1, D_MODEL)), full((D_MODEL, IN_W)),
            full((1, 256)), full((1, 128)), full((1, 128)), full((128, 768)), full((256, 256)),
            row(256), row(256), row(512), row(512), row(128), row(128),
        ],
        out_specs=[out(w_) for w_ in widths],
        out_shape=[jax.ShapeDtypeStruct((bsz, TOK, w_), BF16) for w_ in widths],
        name="inproj",
    )(xall, mod3, g, w, qg, kg, kvg, wkv, avg, *tabs)


def _softmax_pv(segs, sink=None):
    m = None
    for s, _ in segs:
        ms = jnp.max(s, axis=-1, keepdims=True)
        m = ms if m is None else jnp.maximum(m, ms)
    if sink is not None:
        m = jnp.maximum(m, sink)
    l = None
    o = None
    for s, v in segs:
        p = jnp.exp(s - m)
        ls = jnp.sum(p, axis=-1, keepdims=True)
        os_ = _dot(p.astype(BF16), v)
        l = ls if l is None else l + ls
        o = os_ if o is None else o + os_
    if sink is not None:
        l = l + jnp.exp(sink - m)
    return o / l


def _lane_lo(dtype_shape):
    return lax.broadcasted_iota(jnp.int32, dtype_shape, 1) < HEAD_DIM


def _half(qt, hh):
    lo = _lane_lo(qt.shape)
    return jnp.where(lo if hh == 0 else jnp.logical_not(lo), qt, jnp.zeros_like(qt))


def _merge_halves(o0, o1):
    return jnp.where(_lane_lo(o0.shape), o0, o1)


def _attn_a_kernel(q_ref, k_ref, v_ref, o_ref):
    i = pl.program_id(1)

    def run(k, v):
        for t in range(2):
            qt = q_ref[:, t * 128:(t + 1) * 128]
            o = [_softmax_pv([(_dot_nt(_half(qt, hh), k), v)]) for hh in range(2)]
            o_ref[:, t * 128:(t + 1) * 128] = _merge_halves(o[0], o[1]).astype(o_ref.dtype)

    @pl.when(i < NQ)
    def _():
        run(k_ref[...], v_ref[...])

    @pl.when(i >= NQ)
    def _():
        run(k_ref[SEQ:TOK, :], v_ref[SEQ:TOK, :])


def _attn_b_kernel(q_ref, k_ref, v_ref, o_ref):
    i = pl.program_id(1)

    def run(lo_, hi_):
        for t in range(2):
            v = v_ref[lo_:hi_, t * 128:(t + 1) * 128]
            o = []
            for hh in range(2):
                hd = 2 * t + hh
                q = q_ref[:, hd * 128:(hd + 1) * 128]
                k = k_ref[lo_:hi_, hd * 128:(hd + 1) * 128]
                o.append(_softmax_pv([(_dot_nt(q, k), v)]))
            o_ref[:, t * 128:(t + 1) * 128] = _merge_halves(o[0], o[1]).astype(o_ref.dtype)

    @pl.when(i < NQ)
    def _():
        run(0, TOK)

    @pl.when(i >= NQ)
    def _():
        run(SEQ, TOK)


def _attn_c_kernel(q_ref, k_ref, v_ref, sink_ref, o_ref):
    i = pl.program_id(1)
    band = 3 * QBLK
    kc, vc = k_ref[SEQ:TOK, :], v_ref[SEQ:TOK, :]

    def run(latent):
        if latent:
            start = pl.multiple_of(jnp.clip((i - 1) * QBLK, 0, SEQ - band), QBLK)
            kb, vb = k_ref[pl.ds(start, band), :], v_ref[pl.ds(start, band), :]
            qpos = i * QBLK + lax.broadcasted_iota(jnp.int32, (QBLK, band), 0)
            kpos = start + lax.broadcasted_iota(jnp.int32, (QBLK, band), 1)
            valid = jnp.abs(qpos - kpos) <= C_WINDOW
        for t in range(2):
            qt = q_ref[:, t * 128:(t + 1) * 128]
            o = []
            for hh in range(2):
                qm = _half(qt, hh)
                segs = [(_dot_nt(qm, kc), vc)]
                if latent:
                    segs.insert(0, (jnp.where(valid, _dot_nt(qm, kb), NEG), vb))
                j = 2 * t + hh
                o.append(_softmax_pv(segs, sink=sink_ref[j:j + 1, 0:1]))
            o_ref[:, t * 128:(t + 1) * 128] = _merge_halves(o[0], o[1]).astype(o_ref.dtype)

    @pl.when(i < NQ)
    def _():
        run(True)

    @pl.when(i >= NQ)
    def _():
        run(False)


def _attn_d_kernel(q_ref, k_ref, v_ref, bias_ref, o_ref):
    i = pl.program_id(1)

    def run(latent):
        if latent:
            start = pl.multiple_of(jnp.clip((i - 2) * QBLK, 0, SEQ - NA_WIN), QBLK)
        for t in range(2):
            sl = slice(t * 128, (t + 1) * 128)
            qt = q_ref[:, sl]
            kc, vc = k_ref[SEQ:TOK, sl], v_ref[SEQ:TOK, sl]
            if latent:
                kw, vw = k_ref[pl.ds(start, NA_WIN), sl], v_ref[pl.ds(start, NA_WIN), sl]
            o = []
            for hh in range(2):
                qm = _half(qt, hh)
                segs = [(_dot_nt(qm, kc), vc)]
                if latent:
                    segs.insert(0, (_dot_nt(qm, kw) + bias_ref[2 * t + hh], vw))
                o.append(_softmax_pv(segs))
            o_ref[:, sl] = _merge_halves(o[0], o[1]).astype(o_ref.dtype)

    @pl.when(i < NQ)
    def _():
        run(True)

    @pl.when(i >= NQ)
    def _():
        run(False)


def _attn_call(kernel, name, q, k, v, extra, extra_specs, n_blocks):
    bsz = q.shape[0]
    return pl.pallas_call(
        kernel,
        grid=(bsz, n_blocks),
        in_specs=[
            pl.BlockSpec((None, QBLK, q.shape[2]), lambda b, i: (b, i, 0)),
            pl.BlockSpec((None, TOK, k.shape[2]), lambda b, i: (b, 0, 0)),
            pl.BlockSpec((None, TOK, v.shape[2]), lambda b, i: (b, 0, 0)),
        ] + extra_specs,
        out_specs=pl.BlockSpec((None, QBLK, 256), lambda b, i: (b, i, 0)),
        out_shape=jax.ShapeDtypeStruct((bsz, n_blocks * QBLK, 256), BF16),
        name=name,
    )(q, k, v, *extra)


def _na_pattern(i):
    return jnp.where(i < 2, i, jnp.where(i < NQ - 2, 2, jnp.minimum(i, NQ - 1) - (NQ - 5)))


def _na_bias(rpb):
    rows = SEQ // GRID_W
    out = []
    for i in (0, 1, 2, NQ - 2, NQ - 1):
        start_row = min(max(2 * i - 4, 0), rows - 10)
        q = np.arange(QBLK)
        qr, qc = 2 * i + q // GRID_W, q % GRID_W
        kk = np.arange(NA_WIN)
        kr, kc = start_row + kk // GRID_W, kk % GRID_W
        rs = np.clip(qr - NA_ROWS // 2, 0, rows - NA_ROWS)[:, None]
        cs = np.clip(qc - NA_COLS // 2, 0, GRID_W - NA_COLS)[:, None]
        valid = ((kr[None] >= rs) & (kr[None] < rs + NA_ROWS)
                 & (kc[None] >= cs) & (kc[None] < cs + NA_COLS))
        dr = np.clip(kr[None] - qr[:, None] + NA_ROWS - 1, 0, 2 * NA_ROWS - 2)
        dc = np.clip(kc[None] - qc[:, None] + NA_COLS - 1, 0, 2 * NA_COLS - 2)
        out.append(jnp.where(valid[None], rpb[:, dr, dc].astype(F32), NEG))
    return jnp.stack(out)


def _outproj_kernel(ma_ref, mb_ref, mc_ref, md_ref, w_ref, x_ref, mod_ref, g1_ref, g2_ref,
                    wr_ref, rb_ref, xn_ref, h2_ref, idx_ref, wt_ref):
    d = D_MODEL
    mix = jnp.concatenate([ma_ref[...], mb_ref[...], mc_ref[...], md_ref[...]], axis=1)
    y = _rms(_dot(mix, w_ref[...]), g1_ref[...])
    xn = x_ref[...] + mod_ref[:, 2 * d:3 * d] * y
    xn_ref[...] = xn
    h2 = _rms(xn, g2_ref[...]) * (1.0 + mod_ref[:, 4 * d:5 * d]) + mod_ref[:, 3 * d:4 * d]
    h2_ref[...] = h2

    n = h2.shape[0]
    s = _sigmoid(_dot_nt(wr_ref[...], h2.astype(BF16)))
    sb = s + rb_ref[...]
    per = N_EXPERTS // N_GROUPS
    slab = [sb[j * 8:(j + 1) * 8, :] for j in range(per)]
    sraw = [s[j * 8:(j + 1) * 8, :] for j in range(per)]
    giota = lax.broadcasted_iota(jnp.int32, (N_GROUPS, n), 0)
    eid = [giota * per + j for j in range(per)]
    ninf = -jnp.inf

    m1 = functools.reduce(jnp.maximum, slab)
    j1 = functools.reduce(jnp.minimum, [jnp.where(slab[j] == m1, j, per) for j in range(per)])
    m2 = functools.reduce(jnp.maximum, [jnp.where(j1 == j, ninf, slab[j]) for j in range(per)])
    gs = m1 + m2

    gsel = jnp.zeros((N_GROUPS, n), jnp.bool_)
    for _ in range(TOPK_GROUPS):
        m = jnp.max(gs, axis=0, keepdims=True)
        gi = jnp.min(jnp.where(gs == m, giota, N_GROUPS), axis=0, keepdims=True)
        hit = giota == gi
        gsel = jnp.logical_or(gsel, hit)
        gs = jnp.where(hit, ninf, gs)
    slab = [jnp.where(gsel, sj, ninf) for sj in slab]

    kiota = lax.broadcasted_iota(jnp.int32, (TOP_K, n), 0)
    idx = jnp.zeros((TOP_K, n), jnp.int32)
    wts = jnp.zeros((TOP_K, n), F32)
    for k in range(TOP_K):
        m = jnp.max(functools.reduce(jnp.maximum, slab), axis=0, keepdims=True)
        cand = functools.reduce(jnp.minimum,
                                [jnp.where(slab[j] == m, eid[j], N_EXPERTS) for j in range(per)])
        ei = jnp.min(cand, axis=0, keepdims=True)
        hits = [eid[j] == ei for j in range(per)]
        wk = jnp.sum(functools.reduce(jnp.add, [jnp.where(hits[j], sraw[j], 0.0) for j in range(per)]),
                     axis=0, keepdims=True)
        slab = [jnp.where(hits[j], ninf, slab[j]) for j in range(per)]
        idx = jnp.where(kiota == k, ei, idx)
        wts = jnp.where(kiota == k, wk, wts)
    idx_ref[...] = idx
    wt_ref[...] = wts / jnp.sum(wts, axis=0, keepdims=True) * ROUTED_SCALE


def _outproj(mixes, w, xall, mod3, g1, g2, wr, rb, nb):
    bsz = xall.shape[0]
    n_lat = SEQ // ROW_BLK
    full = lambda shape: pl.BlockSpec(shape, lambda b, j: (0,) * len(shape))
    rows = lambda w_: pl.BlockSpec((None, ROW_BLK, w_), lambda b, j: (b, j, 0))
    route = pl.BlockSpec((None, None, TOP_K, ROW_BLK), lambda b, j: (b, j, 0, 0))
    return pl.pallas_call(
        _outproj_kernel,
        grid=(bsz, nb),
        in_specs=[rows(256)] * 4 + [
            full((D_MODEL, D_MODEL)), rows(D_MODEL),
            pl.BlockSpec((None, 1, 6 * D_MODEL), lambda b, j: (_mod_row(j, b, n_lat), 0, 0)),
            full((1, D_MODEL)), full((1, D_MODEL)),
            full((N_EXPERTS, D_MODEL)), full((N_EXPERTS, ROW_BLK)),
        ],
        out_specs=[rows(D_MODEL), rows(D_MODEL), route, route],
        out_shape=[
            jax.ShapeDtypeStruct((bsz, nb * ROW_BLK, D_MODEL), F32),
            jax.ShapeDtypeStruct((bsz, nb * ROW_BLK, D_MODEL), F32),
            jax.ShapeDtypeStruct((bsz, nb, TOP_K, ROW_BLK), jnp.int32),
            jax.ShapeDtypeStruct((bsz, nb, TOP_K, ROW_BLK), F32),
        ],
        name="outproj",
    )(*mixes, w, xall, mod3, g1, g2, wr, rb)


def _start_row_gather(idx_ref, n_rows, src_hbm, dst, sem):
    def body(r, carry):
        pltpu.make_async_copy(src_hbm.at[pl.ds(idx_ref[0, r], 1), :],
                              dst.at[pl.ds(r, 1), :], sem).start()
        return carry
    lax.fori_loop(0, n_rows, body, 0, unroll=8)


def _experts_kernel(be_ref, nu_ref, cur_ref, nxt_ref, h_hbm, wg_ref, wu_ref, wd_ref, y_ref,
                    xbuf, sem, wgb, wub, wdb):
    i = pl.program_id(0)
    n_used = nu_ref[0]
    slot = i % 2

    @pl.when(i == 0)
    def _():
        _start_row_gather(cur_ref, MOE_BLK, h_hbm, xbuf.at[0], sem.at[0])

    @pl.when(i + 1 < n_used)
    def _():
        _start_row_gather(nxt_ref, MOE_BLK, h_hbm, xbuf.at[1 - slot], sem.at[1 - slot])

    @pl.when(i < n_used)
    def _():
        pltpu.make_async_copy(h_hbm.at[pl.ds(0, MOE_BLK), :], xbuf.at[slot], sem.at[slot]).wait()

        @pl.when(jnp.logical_or(i == 0, be_ref[i] != be_ref[jnp.maximum(i - 1, 0)]))
        def _():
            wgb[...] = wg_ref[...].astype(BF16)
            wub[...] = wu_ref[...].astype(BF16)
            wdb[...] = wd_ref[...].astype(BF16)

        x = xbuf[slot].astype(BF16)
        g = _dot(x, wgb[...])
        u = _dot(x, wub[...])
        a = (g * _sigmoid(g) * u).astype(BF16)
        y_ref[...] = _dot(a, wdb[...])

    @pl.when(i >= n_used)
    def _():
        y_ref[...] = jnp.zeros_like(y_ref)


def _experts(block_e, n_used, slot_tok, h_rows, wg, wu, wd, layer, n_blocks):
    last = n_blocks - 1
    wspec = lambda shape: pl.BlockSpec((None, None) + shape, lambda i, be, nu: (layer, be[i], 0, 0))
    grid_spec = pltpu.PrefetchScalarGridSpec(
        num_scalar_prefetch=2,
        grid=(n_blocks,),
        in_specs=[
            pl.BlockSpec((None, 1, MOE_BLK), lambda i, be, nu: (i, 0, 0), memory_space=pltpu.SMEM),
            pl.BlockSpec((None, 1, MOE_BLK), lambda i, be, nu: (jnp.minimum(i + 1, last), 0, 0),
                         memory_space=pltpu.SMEM),
            pl.BlockSpec(memory_space=pl.ANY),
            wspec((D_MODEL, EXPERT_DIM)), wspec((D_MODEL, EXPERT_DIM)), wspec((EXPERT_DIM, D_MODEL)),
        ],
        out_specs=pl.BlockSpec((MOE_BLK, D_MODEL), lambda i, be, nu: (i, 0)),
        scratch_shapes=[
            pltpu.VMEM((2, MOE_BLK, D_MODEL), F32),
            pltpu.SemaphoreType.DMA((2,)),
            pltpu.VMEM((D_MODEL, EXPERT_DIM), BF16),
            pltpu.VMEM((D_MODEL, EXPERT_DIM), BF16),
            pltpu.VMEM((EXPERT_DIM, D_MODEL), BF16),
        ],
    )
    slots3 = slot_tok.reshape(n_blocks, 1, MOE_BLK)
    return pl.pallas_call(
        _experts_kernel,
        grid_spec=grid_spec,
        out_shape=jax.ShapeDtypeStruct((n_blocks * MOE_BLK, D_MODEL), F32),
        compiler_params=pltpu.CompilerParams(dimension_semantics=("arbitrary",)),
        name="experts",
    )(block_e, n_used, slots3, slots3, h_rows, wg, wu, wd)


def _combine_kernel(cur_ref, nxt_ref, y_hbm, wcol_ref, h2_ref, xn_ref, mod_ref, g_ref,
                    swg_ref, swu_ref, swd_ref, o_ref, ybuf, sem, *, n_steps):
    i = pl.program_id(0)
    slot = i % 2
    d = D_MODEL
    rows = TOP_K * CMB_BLK

    @pl.when(i == 0)
    def _():
        _start_row_gather(cur_ref, rows, y_hbm, ybuf.at[0], sem.at[0])

    @pl.when(i + 1 < n_steps)
    def _():
        _start_row_gather(nxt_ref, rows, y_hbm, ybuf.at[1 - slot], sem.at[1 - slot])

    pltpu.make_async_copy(y_hbm.at[pl.ds(0, rows), :], ybuf.at[slot], sem.at[slot]).wait()

    hb = h2_ref[...].astype(BF16)
    g = _dot(hb, swg_ref[...])
    u = _dot(hb, swu_ref[...])
    f = _dot((g * _sigmoid(g) * u).astype(BF16), swd_ref[...])
    w = wcol_ref[...]
    for k in range(TOP_K):
        f = f + ybuf[slot, k * CMB_BLK:(k + 1) * CMB_BLK, :] * w[:, k:k + 1]
    o_ref[...] = xn_ref[...] + mod_ref[:, 5 * d:6 * d] * _rms(f, g_ref[...])


def _combine(pos, y_sorted, wcol, h2_rows, xn_rows, mod3, g, swg, swu, swd, n_steps, per_batch):
    last = n_steps - 1
    full = lambda shape: pl.BlockSpec(shape, lambda n: (0,) * len(shape))
    rows = lambda w_: pl.BlockSpec((CMB_BLK, w_), lambda n: (n, 0))
    n_lat = SEQ // CMB_BLK
    return pl.pallas_call(
        functools.partial(_combine_kernel, n_steps=n_steps),
        grid=(n_steps,),
        in_specs=[
            pl.BlockSpec((None, 1, TOP_K * CMB_BLK), lambda n: (n, 0, 0), memory_space=pltpu.SMEM),
            pl.BlockSpec((None, 1, TOP_K * CMB_BLK), lambda n: (jnp.minimum(n + 1, last), 0, 0),
                         memory_space=pltpu.SMEM),
            pl.BlockSpec(memory_space=pl.ANY),
            pl.BlockSpec((CMB_BLK, TOP_K), lambda n: (n, 0)),
            rows(D_MODEL), rows(D_MODEL),
            pl.BlockSpec((None, 1, 6 * D_MODEL),
                         lambda n: (_mod_row(n % per_batch, n // per_batch, n_lat), 0, 0)),
            full((1, D_MODEL)),
            full((D_MODEL, SHARED_DIM)), full((D_MODEL, SHARED_DIM)), full((SHARED_DIM, D_MODEL)),
        ],
        out_specs=rows(D_MODEL),
        out_shape=jax.ShapeDtypeStruct(xn_rows.shape, F32),
        scratch_shapes=[
            pltpu.VMEM((2, TOP_K * CMB_BLK, D_MODEL), F32),
            pltpu.SemaphoreType.DMA((2,)),
        ],
        compiler_params=pltpu.CompilerParams(dimension_semantics=("arbitrary",)),
        name="combine",
    )(pos, pos, y_sorted, wcol, h2_rows, xn_rows, mod3, g, swg, swu, swd)


def _dispatch_plan(idx, tok_rows, n_blocks):
    onehot = (idx[:, :, None] == jnp.arange(N_EXPERTS, dtype=jnp.int32)).astype(jnp.int32).sum(axis=1)
    rank = jnp.cumsum(onehot, axis=0) - onehot
    counts = onehot.sum(axis=0)
    padded = (counts + MOE_BLK - 1) // MOE_BLK * MOE_BLK
    pad_end = jnp.cumsum(padded)
    pad_start = pad_end - padded
    slot = pad_start[idx] + jnp.take_along_axis(rank, idx, axis=1)
    slot_tok = jnp.zeros((n_blocks * MOE_BLK,), jnp.int32).at[slot.reshape(-1)].set(
        jnp.repeat(tok_rows, TOP_K))
    block_e = jnp.minimum(
        jnp.searchsorted(pad_end, jnp.arange(n_blocks, dtype=jnp.int32) * MOE_BLK, side="right"),
        N_EXPERTS - 1).astype(jnp.int32)
    n_used = (pad_end[-1] // MOE_BLK).astype(jnp.int32).reshape(1)
    return block_e, n_used, slot_tok, slot.astype(jnp.int32)


def _axial_tables(dim):
    t = jnp.arange(SEQ)
    row = (t // GRID_W).astype(F32)
    col = (t % GRID_W).astype(F32)
    quarter = dim // 4
    freqs = ROPE_THETA ** (-jnp.arange(quarter, dtype=F32) / quarter)
    ar, ac = row[:, None] * freqs, col[:, None] * freqs
    ang = jnp.concatenate([ar, ar, ac, ac], axis=-1)
    sign = jnp.where((jnp.arange(dim) // quarter) % 2 == 0, -1.0, 1.0).astype(F32)
    cos = jnp.concatenate([jnp.cos(ang), jnp.ones((CTX_LEN, dim), F32)], axis=0)
    sin = jnp.concatenate([jnp.sin(ang) * sign, jnp.zeros((CTX_LEN, dim), F32)], axis=0)
    return cos, sin


def _rope_tables():
    cos_h, sin_h = _axial_tables(HEAD_DIM)
    cos_r, sin_r = _axial_tables(B_ROPE)
    cosh, sinh = jnp.tile(cos_h, (1, 4)), jnp.tile(sin_h, (1, 4))
    scale_b = (B_NOPE + B_ROPE) ** -0.5
    ones, zeros = jnp.ones((TOK, 64), F32), jnp.zeros((TOK, 64), F32)
    pad1, pad0 = jnp.ones((TOK, 32), F32), jnp.zeros((TOK, 32), F32)
    cosb = jnp.tile(jnp.concatenate([ones, cos_r, pad1], axis=1) * scale_b, (1, 4))
    sinb = jnp.tile(jnp.concatenate([zeros, sin_r, pad0], axis=1) * scale_b, (1, 4))
    cosr = jnp.concatenate([cos_r, jnp.ones((TOK, 96), F32)], axis=1)
    sinr = jnp.concatenate([sin_r, jnp.zeros((TOK, 96), F32)], axis=1)
    return cosh, sinh, cosb, sinb, cosr, sinr


_GQA_ORDER = (0, 2, 1, 3)


def _head_cols(base, order, width=HEAD_DIM):
    return np.concatenate([base + np.arange(h * width, (h + 1) * width) for h in order])


def _w_in_layout(w_in):
    zeros = lambda n: jnp.zeros((D_MODEL, n), w_in.dtype)
    a_q = w_in[:, _head_cols(0, _GQA_ORDER)]
    a_kv = w_in[:, 256:512]
    b_q = jnp.concatenate(
        [jnp.concatenate([w_in[:, 512 + h * 96:512 + (h + 1) * 96], zeros(32)], axis=1)
         for h in range(B_HEADS)], axis=1)
    b_c = w_in[:, 896:1024]
    b_r = jnp.concatenate([w_in[:, 1024:1056], zeros(96)], axis=1)
    c_q = w_in[:, _head_cols(1056, _GQA_ORDER)]
    rest = w_in[:, 1312:2336]
    return jnp.concatenate([a_q, a_kv, b_q, b_c, b_r, c_q, rest], axis=1).astype(BF16)


def _w_kv_layout(w_kv_up):
    zeros = jnp.zeros((B_KV_RANK, 64), w_kv_up.dtype)
    nope = [jnp.concatenate([w_kv_up[:, h * 128:h * 128 + 64], zeros], axis=1) for h in range(B_HEADS)]
    val = [w_kv_up[:, h * 128 + 64:(h + 1) * 128] for h in range(B_HEADS)]
    return jnp.concatenate(nope + val, axis=1).astype(BF16)


def _w_out_layout(w_out):
    rows = np.concatenate([_head_cols(0, _GQA_ORDER), np.arange(256, 512),
                           _head_cols(512, _GQA_ORDER), np.arange(768, 1024)])
    return w_out[rows].astype(BF16)


_ROUTER_ROWS = np.array([g * 8 + j for j in range(8) for g in range(8)])


def kernel(x, c, ctx, c_ctx, ada_w, ada_b, pre_mix_g, post_mix_g, pre_ffn_g, post_ffn_g, w_in, a_q_norm, a_k_norm, b_kv_norm, b_w_kv_up, c_sink, d_rpb, w_out, router_w, router_b, exp_w_gate, exp_w_up, exp_w_down, sh_w_gate, sh_w_up, sh_w_down):
    bsz = x.shape[0]
    depth = ada_w.shape[0]
    xall = jnp.concatenate([x, ctx], axis=1)

    cvec = jnp.zeros((16, D_MODEL), F32).at[:bsz].set(c).at[8].set(c_ctx)
    mod = _ada(cvec, ada_w, ada_b)
    tabs = _rope_tables()
    avg = jnp.kron(jnp.eye(4, dtype=F32), jnp.full((64, 64), 1.0 / 64, F32)).astype(BF16)
    row2 = lambda v: v.reshape(1, -1).astype(F32)

    for l in range(depth):
        with_ctx = l < depth - 1
        mod3 = mod[l].reshape(16, 1, 6 * D_MODEL)

        proj = _inproj(
            xall, mod3, row2(pre_mix_g[l]), _w_in_layout(w_in[l]),
            row2(jnp.tile(a_q_norm[l], 4) * HEAD_DIM ** -0.5), row2(jnp.tile(a_k_norm[l], 2)),
            row2(b_kv_norm[l]), _w_kv_layout(b_w_kv_up[l]), avg, tabs)
        qa, ka, va, qb, kb, vb, qc, kc, vc, qd, kd, vd = proj

        nq = NQ + (CTX_LEN // QBLK if with_ctx else 0)
        full = lambda shape: pl.BlockSpec(shape, lambda b, i: (0,) * len(shape))
        mix_a = _attn_call(_attn_a_kernel, "attn_a", qa, ka, va, [], [], nq)
        mix_b = _attn_call(_attn_b_kernel, "attn_b", qb, kb, vb, [], [], nq)
        sink = jnp.zeros((8, LANES), F32).at[:4].set(
            jnp.broadcast_to(c_sink[l][np.array(_GQA_ORDER)][:, None], (4, LANES)))
        mix_c = _attn_call(_attn_c_kernel, "attn_c", qc, kc, vc, [sink], [full((8, LANES))], nq)
        bias = _na_bias(d_rpb[l])
        mix_d = _attn_call(
            _attn_d_kernel, "attn_d", qd, kd, vd, [bias],
            [pl.BlockSpec((None, D_HEADS, QBLK, NA_WIN), lambda b, i: (_na_pattern(i), 0, 0, 0))], nq)

        nb = (TOK if with_ctx else SEQ) // ROW_BLK
        rb = jnp.broadcast_to(router_b[l][_ROUTER_ROWS][:, None], (N_EXPERTS, ROW_BLK)).astype(F32)
        xn, h2, idx_t, wt_t = _outproj(
            (mix_a, mix_b, mix_c, mix_d), _w_out_layout(w_out[l]), xall, mod3,
            row2(post_mix_g[l]), row2(pre_ffn_g[l]),
            router_w[l].T[_ROUTER_ROWS].astype(BF16), rb, nb)

        t_used = nb * ROW_BLK
        n_tok = bsz * t_used
        idx = idx_t.transpose(0, 1, 3, 2).reshape(n_tok, TOP_K)
        wts = wt_t.transpose(0, 1, 3, 2).reshape(n_tok, TOP_K)
        tok_rows = jnp.arange(n_tok, dtype=jnp.int32)
        n_blocks = -(-(n_tok * TOP_K + N_EXPERTS * (MOE_BLK - 1)) // MOE_BLK)
        block_e, n_used, slot_tok, slot = _dispatch_plan(idx, tok_rows, n_blocks)

        h2_rows = h2.reshape(n_tok, D_MODEL)
        y_sorted = _experts(block_e, n_used, slot_tok, h2_rows,
                            exp_w_gate, exp_w_up, exp_w_down, l, n_blocks)

        per_batch = t_used // CMB_BLK
        n_steps = bsz * per_batch
        pos = slot.reshape(n_steps, CMB_BLK, TOP_K).transpose(0, 2, 1).reshape(n_steps, 1, TOP_K * CMB_BLK)
        xall = _combine(
            pos, y_sorted, wts, h2_rows, xn.reshape(n_tok, D_MODEL), mod3, row2(post_ffn_g[l]),
            sh_w_gate[l].astype(BF16), sh_w_up[l].astype(BF16), sh_w_down[l].astype(BF16),
            n_steps, per_batch).reshape(bsz, t_used, D_MODEL)

    return xall[:, :SEQ]
```

```python
import functools

import numpy as np
import jax
import jax.numpy as jnp
from jax import lax
from jax.experimental import pallas as pl
from jax.experimental.pallas import tpu as pltpu

F32 = jnp.float32
BF16 = jnp.bfloat16

D_MODEL = 1024
SEQ = 2048
CTX_LEN = 256
TOK = SEQ + CTX_LEN
GRID_W = 64
HEAD_DIM = 64
ROPE_THETA = 10000.0
EPS = 1e-6
NEG = -1e30

A_HEADS, A_KV_HEADS = 4, 2
B_HEADS, B_NOPE, B_ROPE, B_V, B_KV_RANK = 4, 64, 32, 64, 128
C_HEADS, C_KV_HEADS, C_WINDOW = 4, 2, 128
D_HEADS, NA_ROWS, NA_COLS = 4, 8, 16

N_EXPERTS, TOP_K, N_GROUPS, TOPK_GROUPS = 64, 8, 8, 4
EXPERT_DIM, SHARED_DIM = 256, 256
ROUTED_SCALE = 2.5

LANES = 128
QBLK = 128
ROW_BLK = 256
MOE_BLK = 256
CMB_BLK = 128
NQ = SEQ // QBLK
NA_WIN = 10 * GRID_W

_AQ, _AK, _AV, _BQ, _BC, _BR = 0, 256, 384, 512, 1024, 1152
_CQ, _CK, _CV, _DQ, _DK, _DV = 1280, 1536, 1664, 1792, 2048, 2304
IN_W = 2560

_NT = (((1,), (1,)), ((), ()))


def _sigmoid(x):
    return 1.0 / (1.0 + jnp.exp(-x))


def _rms(x, g):
    return x * lax.rsqrt(jnp.mean(x * x, axis=-1, keepdims=True) + EPS) * g


def _dot(a, b):
    return jnp.dot(a, b, preferred_element_type=F32)


def _dot_nt(a, b):
    return lax.dot_general(a, b, _NT, preferred_element_type=F32)


def _ada_kernel(c_ref, w_ref, b_ref, o_ref):
    c = c_ref[...]
    a = (c * _sigmoid(c)).astype(BF16)
    o_ref[...] = _dot(a, w_ref[...].astype(BF16)) + b_ref[...]


def _ada(cvec, ada_w, ada_b):
    depth = ada_w.shape[0]
    n = ada_w.shape[2]
    tn = 1536
    return pl.pallas_call(
        _ada_kernel,
        grid=(depth, n // tn),
        in_specs=[
            pl.BlockSpec((16, D_MODEL), lambda l, j: (0, 0)),
            pl.BlockSpec((None, D_MODEL, tn), lambda l, j: (l, 0, j)),
            pl.BlockSpec((None, 1, tn), lambda l, j: (l, 0, j)),
        ],
        out_specs=pl.BlockSpec((None, 16, tn), lambda l, j: (l, 0, j)),
        out_shape=jax.ShapeDtypeStruct((depth, 16, n), F32),
        name="ada",
    )(cvec, ada_w, ada_b.reshape(depth, 1, n))


def _rope(x, cos, sin, chunk):
    outs = []
    for c in range(x.shape[1] // LANES):
        sl = slice(c * LANES, (c + 1) * LANES)
        xs = x[:, sl]
        lane = lax.broadcasted_iota(jnp.int32, xs.shape, 1)
        first = (lane & chunk) == 0
        rot = jnp.where(first, pltpu.roll(xs, LANES - chunk, 1), pltpu.roll(xs, chunk, 1))
        outs.append(xs * cos[:, sl] + rot * sin[:, sl])
    return outs[0] if len(outs) == 1 else jnp.concatenate(outs, axis=1)


def _head_norm(x, avg):
    sq = x * x
    hi = sq.astype(BF16)
    lo = (sq - hi.astype(F32)).astype(BF16)
    ms = _dot(hi, avg) + _dot(lo, avg)
    return x * lax.rsqrt(ms + EPS)


def _inproj_kernel(x_ref, mod_ref, g_ref, w_ref, qg_ref, kg_ref, kvg_ref, wkv_ref, avg_ref,
                   cosh_ref, sinh_ref, cosb_ref, sinb_ref, cosr_ref, sinr_ref,
                   qa_ref, ka_ref, va_ref, qb_ref, kb_ref, vb_ref,
                   qc_ref, kc_ref, vc_ref, qd_ref, kd_ref, vd_ref):
    d = D_MODEL
    h = _rms(x_ref[...], g_ref[...]) * (1.0 + mod_ref[:, d:2 * d]) + mod_ref[:, 0:d]
    p = _dot(h.astype(BF16), w_ref[...])
    cosh, sinh = cosh_ref[...], sinh_ref[...]
    avg = avg_ref[...]

    qa = _head_norm(p[:, _AQ:_AQ + 256], avg) * qg_ref[...]
    qa_ref[...] = _rope(qa, cosh, sinh, 16).astype(BF16)
    ka = _head_norm(p[:, _AK:_AK + 128], avg[0:128, 0:128]) * kg_ref[...]
    ka_ref[...] = _rope(ka, cosh[:, 0:128], sinh[:, 0:128], 16).astype(BF16)
    va_ref[...] = p[:, _AV:_AV + 128].astype(BF16)

    qb_ref[...] = _rope(p[:, _BQ:_BQ + 512], cosb_ref[...], sinb_ref[...], 8).astype(BF16)
    cn = _rms(p[:, _BC:_BC + 128], kvg_ref[...])
    kv = _dot(cn.astype(BF16), wkv_ref[...])
    kr = _rope(p[:, _BR:_BR + 128], cosr_ref[...], sinr_ref[...], 8)
    kr = pltpu.roll(kr, 64, 1)
    kb_ref[...] = jnp.concatenate(
        [kv[:, hh * 128:(hh + 1) * 128] + kr for hh in range(B_HEADS)], axis=1).astype(BF16)
    vb_ref[...] = kv[:, 512:768].astype(BF16)

    qc_ref[...] = _rope(p[:, _CQ:_CQ + 256] * 0.125, cosh, sinh, 16).astype(BF16)
    kc_ref[...] = _rope(p[:, _CK:_CK + 128], cosh[:, 0:128], sinh[:, 0:128], 16).astype(BF16)
    vc_ref[...] = p[:, _CV:_CV + 128].astype(BF16)

    qd_ref[...] = (p[:, _DQ:_DQ + 256] * 0.125).astype(BF16)
    kd_ref[...] = p[:, _DK:_DK + 256].astype(BF16)
    vd_ref[...] = p[:, _DV:_DV + 256].astype(BF16)


def _mod_row(j, b, n_lat_blocks):
    return jnp.where(j < n_lat_blocks, b, 8)


def _inproj(xall, mod3, g, w, qg, kg, kvg, wkv, avg, tabs):
    bsz = xall.shape[0]
    nb = TOK // ROW_BLK
    n_lat = SEQ // ROW_BLK
    full = lambda shape: pl.BlockSpec(shape, lambda b, j: (0,) * len(shape))
    row = lambda w_: pl.BlockSpec((ROW_BLK, w_), lambda b, j: (j, 0))
    out = lambda w_: pl.BlockSpec((None, ROW_BLK, w_), lambda b, j: (b, j, 0))
    widths = (256, 128, 128, 512, 512, 256, 256, 128, 128, 256, 256, 256)
    return pl.pallas_call(
        _inproj_kernel,
        grid=(bsz, nb),
        in_specs=[
            pl.BlockSpec((None, ROW_BLK, D_MODEL), lambda b, j: (b, j, 0)),
            pl.BlockSpec((None, 1, 6 * D_MODEL), lambda b, j: (_mod_row(j, b, n_lat), 0, 0)),
            full((1, D_MODEL)), full((D_MODEL, IN_W)),
            full((1, 256)), full((1, 128)), full((1, 128)), full((128, 768)), full((256, 256)),
            row(256), row(256), row(512), row(512), row(128), row(128),
        ],
        out_specs=[out(w_) for w_ in widths],
        out_shape=[jax.ShapeDtypeStruct((bsz, TOK, w_), BF16) for w_ in widths],
        name="inproj",
    )(xall, mod3, g, w, qg, kg, kvg, wkv, avg, *tabs)


def _softmax_pv(segs, sink=None):
    m = None
    for s, _ in segs:
        ms = jnp.max(s, axis=-1, keepdims=True)
        m = ms if m is None else jnp.maximum(m, ms)
    if sink is not None:
        m = jnp.maximum(m, sink)
    l = None
    o = None
    for s, v in segs:
        p = jnp.exp(s - m)
        ls = jnp.sum(p, axis=-1, keepdims=True)
        os_ = _dot(p.astype(BF16), v)
        l = ls if l is None else l + ls
        o = os_ if o is None else o + os_
    if sink is not None:
        l = l + jnp.exp(sink - m)
    return o / l


def _lane_lo(dtype_shape):
    return lax.broadcasted_iota(jnp.int32, dtype_shape, 1) < HEAD_DIM


def _half(qt, hh):
    lo = _lane_lo(qt.shape)
    return jnp.where(lo if hh == 0 else jnp.logical_not(lo), qt, jnp.zeros_like(qt))


def _merge_halves(o0, o1):
    return jnp.where(_lane_lo(o0.shape), o0, o1)


def _attn_a_kernel(q_ref, k_ref, v_ref, o_ref):
    i = pl.program_id(1)

    def run(k, v):
        for t in range(2):
            qt = q_ref[:, t * 128:(t + 1) * 128]
            o = [_softmax_pv([(_dot_nt(_half(qt, hh), k), v)]) for hh in range(2)]
            o_ref[:, t * 128:(t + 1) * 128] = _merge_halves(o[0], o[1]).astype(o_ref.dtype)

    @pl.when(i < NQ)
    def _():
        run(k_ref[...], v_ref[...])

    @pl.when(i >= NQ)
    def _():
        run(k_ref[SEQ:TOK, :], v_ref[SEQ:TOK, :])


def _attn_b_kernel(q_ref, k_ref, v_ref, o_ref):
    i = pl.program_id(1)

    def run(lo_, hi_):
        for t in range(2):
            v = v_ref[lo_:hi_, t * 128:(t + 1) * 128]
            o = []
            for hh in range(2):
                hd = 2 * t + hh
                q = q_ref[:, hd * 128:(hd + 1) * 128]
                k = k_ref[lo_:hi_, hd * 128:(hd + 1) * 128]
                o.append(_softmax_pv([(_dot_nt(q, k), v)]))
            o_ref[:, t * 128:(t + 1) * 128] = _merge_halves(o[0], o[1]).astype(o_ref.dtype)

    @pl.when(i < NQ)
    def _():
        run(0, TOK)

    @pl.when(i >= NQ)
    def _():
        run(SEQ, TOK)


def _attn_c_kernel(q_ref, k_ref, v_ref, sink_ref, o_ref):
    i = pl.program_id(1)
    band = 3 * QBLK
    kc, vc = k_ref[SEQ:TOK, :], v_ref[SEQ:TOK, :]

    def run(latent):
        if latent:
            start = pl.multiple_of(jnp.clip((i - 1) * QBLK, 0, SEQ - band), QBLK)
            kb, vb = k_ref[pl.ds(start, band), :], v_ref[pl.ds(start, band), :]
            qpos = i * QBLK + lax.broadcasted_iota(jnp.int32, (QBLK, band), 0)
            kpos = start + lax.broadcasted_iota(jnp.int32, (QBLK, band), 1)
            valid = jnp.abs(qpos - kpos) <= C_WINDOW
        for t in range(2):
            qt = q_ref[:, t * 128:(t + 1) * 128]
            o = []
            for hh in range(2):
                qm = _half(qt, hh)
                segs = [(_dot_nt(qm, kc), vc)]
                if latent:
                    segs.insert(0, (jnp.where(valid, _dot_nt(qm, kb), NEG), vb))
                j = 2 * t + hh
                o.append(_softmax_pv(segs, sink=sink_ref[j:j + 1, 0:1]))
            o_ref[:, t * 128:(t + 1) * 128] = _merge_halves(o[0], o[1]).astype(o_ref.dtype)

    @pl.when(i < NQ)
    def _():
        run(True)

    @pl.when(i >= NQ)
    def _():
        run(False)


def _attn_d_kernel(q_ref, k_ref, v_ref, bias_ref, o_ref):
    i = pl.program_id(1)

    def run(latent):
        if latent:
            start = pl.multiple_of(jnp.clip((i - 2) * QBLK, 0, SEQ - NA_WIN), QBLK)
        for t in range(2):
            sl = slice(t * 128, (t + 1) * 128)
            qt = q_ref[:, sl]
            kc, vc = k_ref[SEQ:TOK, sl], v_ref[SEQ:TOK, sl]
            if latent:
                kw, vw = k_ref[pl.ds(start, NA_WIN), sl], v_ref[pl.ds(start, NA_WIN), sl]
            o = []
            for hh in range(2):
                qm = _half(qt, hh)
                segs = [(_dot_nt(qm, kc), vc)]
                if latent:
                    segs.insert(0, (_dot_nt(qm, kw) + bias_ref[2 * t + hh], vw))
                o.append(_softmax_pv(segs))
            o_ref[:, sl] = _merge_halves(o[0], o[1]).astype(o_ref.dtype)

    @pl.when(i < NQ)
    def _():
        run(True)

    @pl.when(i >= NQ)
    def _():
        run(False)


def _attn_call(kernel, name, q, k, v, extra, extra_specs, n_blocks):
    bsz = q.shape[0]
    return pl.pallas_call(
        kernel,
        grid=(bsz, n_blocks),
        in_specs=[
            pl.BlockSpec((None, QBLK, q.shape[2]), lambda b, i: (b, i, 0)),
            pl.BlockSpec((None, TOK, k.shape[2]), lambda b, i: (b, 0, 0)),
            pl.BlockSpec((None, TOK, v.shape[2]), lambda b, i: (b, 0, 0)),
        ] + extra_specs,
        out_specs=pl.BlockSpec((None, QBLK, 256), lambda b, i: (b, i, 0)),
        out_shape=jax.ShapeDtypeStruct((bsz, n_blocks * QBLK, 256), BF16),
        name=name,
    )(q, k, v, *extra)


def _na_pattern(i):
    return jnp.where(i < 2, i, jnp.where(i < NQ - 2, 2, jnp.minimum(i, NQ - 1) - (NQ - 5)))


def _na_bias(rpb):
    rows = SEQ // GRID_W
    n_dr, n_dc = 2 * NA_ROWS - 1, 2 * NA_COLS - 1
    qc = np.arange(GRID_W)
    dc = np.clip(qc[None, :] - qc[:, None] + NA_COLS - 1, 0, n_dc - 1)
    sel_c = jnp.asarray(np.eye(n_dc, dtype=np.float32)[dc])
    cs = np.clip(qc - NA_COLS // 2, 0, GRID_W - NA_COLS)[:, None]
    col_ok = (qc[None, :] >= cs) & (qc[None, :] < cs + NA_COLS)
    out = []
    for i in (0, 1, 2, NQ - 2, NQ - 1):
        start_row = min(max(2 * i - 4, 0), rows - 10)
        qr = 2 * i + np.arange(2)
        kr = start_row + np.arange(10)
        rs = np.clip(qr - NA_ROWS // 2, 0, rows - NA_ROWS)[:, None]
        row_ok = (kr[None] >= rs) & (kr[None] < rs + NA_ROWS)
        dr = np.clip(kr[None] - qr[:, None] + NA_ROWS - 1, 0, n_dr - 1)
        sel_r = jnp.asarray(np.eye(n_dr, dtype=np.float32)[dr])
        b = jnp.einsum("hrc,abr,qkc->haqbk", rpb.astype(F32), sel_r, sel_c,
                       precision=lax.Precision.HIGHEST)
        valid = row_ok[:, None, :, None] & col_ok[None, :, None, :]
        out.append(jnp.where(valid[None], b, NEG).reshape(D_HEADS, QBLK, NA_WIN))
    return jnp.stack(out)


def _outproj_kernel(ma_ref, mb_ref, mc_ref, md_ref, w_ref, x_ref, mod_ref, g1_ref, g2_ref,
                    wr_ref, rb_ref, xn_ref, h2_ref, idx_ref, wt_ref):
    d = D_MODEL
    mix = jnp.concatenate([ma_ref[...], mb_ref[...], mc_ref[...], md_ref[...]], axis=1)
    y = _rms(_dot(mix, w_ref[...]), g1_ref[...])
    xn = x_ref[...] + mod_ref[:, 2 * d:3 * d] * y
    xn_ref[...] = xn
    h2 = _rms(xn, g2_ref[...]) * (1.0 + mod_ref[:, 4 * d:5 * d]) + mod_ref[:, 3 * d:4 * d]
    h2_ref[...] = h2

    n = h2.shape[0]
    s = _sigmoid(_dot_nt(wr_ref[...], h2.astype(BF16)))
    sb = s + rb_ref[...]
    per = N_EXPERTS // N_GROUPS
    slab = [sb[j * 8:(j + 1) * 8, :] for j in range(per)]
    sraw = [s[j * 8:(j + 1) * 8, :] for j in range(per)]
    giota = lax.broadcasted_iota(jnp.int32, (N_GROUPS, n), 0)
    eid = [giota * per + j for j in range(per)]
    ninf = -jnp.inf

    m1 = functools.reduce(jnp.maximum, slab)
    j1 = functools.reduce(jnp.minimum, [jnp.where(slab[j] == m1, j, per) for j in range(per)])
    m2 = functools.reduce(jnp.maximum, [jnp.where(j1 == j, ninf, slab[j]) for j in range(per)])
    gs = m1 + m2

    gsel = jnp.zeros((N_GROUPS, n), jnp.bool_)
    for _ in range(TOPK_GROUPS):
        m = jnp.max(gs, axis=0, keepdims=True)
        gi = jnp.min(jnp.where(gs == m, giota, N_GROUPS), axis=0, keepdims=True)
        hit = giota == gi
        gsel = jnp.logical_or(gsel, hit)
        gs = jnp.where(hit, ninf, gs)
    slab = [jnp.where(gsel, sj, ninf) for sj in slab]

    kiota = lax.broadcasted_iota(jnp.int32, (TOP_K, n), 0)
    idx = jnp.zeros((TOP_K, n), jnp.int32)
    wts = jnp.zeros((TOP_K, n), F32)
    for k in range(TOP_K):
        m = jnp.max(functools.reduce(jnp.maximum, slab), axis=0, keepdims=True)
        cand = functools.reduce(jnp.minimum,
                                [jnp.where(slab[j] == m, eid[j], N_EXPERTS) for j in range(per)])
        ei = jnp.min(cand, axis=0, keepdims=True)
        hits = [eid[j] == ei for j in range(per)]
        wk = jnp.sum(functools.reduce(jnp.add, [jnp.where(hits[j], sraw[j], 0.0) for j in range(per)]),
                     axis=0, keepdims=True)
        slab = [jnp.where(hits[j], ninf, slab[j]) for j in range(per)]
        idx = jnp.where(kiota == k, ei, idx)
        wts = jnp.where(kiota == k, wk, wts)
    idx_ref[...] = idx
    wt_ref[...] = wts / jnp.sum(wts, axis=0, keepdims=True) * ROUTED_SCALE


def _outproj(mixes, w, xall, mod3, g1, g2, wr, rb, nb):
    bsz = xall.shape[0]
    n_lat = SEQ // ROW_BLK
    full = lambda shape: pl.BlockSpec(shape, lambda b, j: (0,) * len(shape))
    rows = lambda w_: pl.BlockSpec((None, ROW_BLK, w_), lambda b, j: (b, j, 0))
    route = pl.BlockSpec((None, None, TOP_K, ROW_BLK), lambda b, j: (b, j, 0, 0))
    return pl.pallas_call(
        _outproj_kernel,
        grid=(bsz, nb),
        in_specs=[rows(256)] * 4 + [
            full((D_MODEL, D_MODEL)), rows(D_MODEL),
            pl.BlockSpec((None, 1, 6 * D_MODEL), lambda b, j: (_mod_row(j, b, n_lat), 0, 0)),
            full((1, D_MODEL)), full((1, D_MODEL)),
            full((N_EXPERTS, D_MODEL)), full((N_EXPERTS, ROW_BLK)),
        ],
        out_specs=[rows(D_MODEL), rows(D_MODEL), route, route],
        out_shape=[
            jax.ShapeDtypeStruct((bsz, nb * ROW_BLK, D_MODEL), F32),
            jax.ShapeDtypeStruct((bsz, nb * ROW_BLK, D_MODEL), F32),
            jax.ShapeDtypeStruct((bsz, nb, TOP_K, ROW_BLK), jnp.int32),
            jax.ShapeDtypeStruct((bsz, nb, TOP_K, ROW_BLK), F32),
        ],
        name="outproj",
    )(*mixes, w, xall, mod3, g1, g2, wr, rb)


def _start_row_gather(idx_ref, n_rows, src_hbm, dst, sem):
    def body(r, carry):
        pltpu.make_async_copy(src_hbm.at[pl.ds(idx_ref[0, r], 1), :],
                              dst.at[pl.ds(r, 1), :], sem).start()
        return carry
    lax.fori_loop(0, n_rows, body, 0, unroll=8)


def _experts_kernel(be_ref, nu_ref, cur_ref, nxt_ref, h_hbm, wg_ref, wu_ref, wd_ref, y_ref,
                    xbuf, sem, wgb, wub, wdb):
    i = pl.program_id(0)
    n_used = nu_ref[0]
    slot = i % 2

    @pl.when(i == 0)
    def _():
        _start_row_gather(cur_ref, MOE_BLK, h_hbm, xbuf.at[0], sem.at[0])

    @pl.when(i + 1 < n_used)
    def _():
        _start_row_gather(nxt_ref, MOE_BLK, h_hbm, xbuf.at[1 - slot], sem.at[1 - slot])

    @pl.when(i < n_used)
    def _():
        pltpu.make_async_copy(h_hbm.at[pl.ds(0, MOE_BLK), :], xbuf.at[slot], sem.at[slot]).wait()

        @pl.when(jnp.logical_or(i == 0, be_ref[i] != be_ref[jnp.maximum(i - 1, 0)]))
        def _():
            wgb[...] = wg_ref[...].astype(BF16)
            wub[...] = wu_ref[...].astype(BF16)
            wdb[...] = wd_ref[...].astype(BF16)

        x = xbuf[slot].astype(BF16)
        g = _dot(x, wgb[...])
        u = _dot(x, wub[...])
        a = (g * _sigmoid(g) * u).astype(BF16)
        y_ref[...] = _dot(a, wdb[...])

    @pl.when(i >= n_used)
    def _():
        y_ref[...] = jnp.zeros_like(y_ref)


def _experts(block_e, n_used, slot_tok, h_rows, wg, wu, wd, layer, n_blocks):
    last = n_blocks - 1
    wspec = lambda shape: pl.BlockSpec((None, None) + shape, lambda i, be, nu: (layer, be[i], 0, 0))
    grid_spec = pltpu.PrefetchScalarGridSpec(
        num_scalar_prefetch=2,
        grid=(n_blocks,),
        in_specs=[
            pl.BlockSpec((None, 1, MOE_BLK), lambda i, be, nu: (i, 0, 0), memory_space=pltpu.SMEM),
            pl.BlockSpec((None, 1, MOE_BLK), lambda i, be, nu: (jnp.minimum(i + 1, last), 0, 0),
                         memory_space=pltpu.SMEM),
            pl.BlockSpec(memory_space=pl.ANY),
            wspec((D_MODEL, EXPERT_DIM)), wspec((D_MODEL, EXPERT_DIM)), wspec((EXPERT_DIM, D_MODEL)),
        ],
        out_specs=pl.BlockSpec((MOE_BLK, D_MODEL), lambda i, be, nu: (i, 0)),
        scratch_shapes=[
            pltpu.VMEM((2, MOE_BLK, D_MODEL), F32),
            pltpu.SemaphoreType.DMA((2,)),
            pltpu.VMEM((D_MODEL, EXPERT_DIM), BF16),
            pltpu.VMEM((D_MODEL, EXPERT_DIM), BF16),
            pltpu.VMEM((EXPERT_DIM, D_MODEL), BF16),
        ],
    )
    slots3 = slot_tok.reshape(n_blocks, 1, MOE_BLK)
    return pl.pallas_call(
        _experts_kernel,
        grid_spec=grid_spec,
        out_shape=jax.ShapeDtypeStruct((n_blocks * MOE_BLK, D_MODEL), F32),
        compiler_params=pltpu.CompilerParams(dimension_semantics=("arbitrary",)),
        name="experts",
    )(block_e, n_used, slots3, slots3, h_rows, wg, wu, wd)


def _combine_kernel(cur_ref, nxt_ref, y_hbm, wcol_ref, h2_ref, xn_ref, mod_ref, g_ref,
                    swg_ref, swu_ref, swd_ref, o_ref, ybuf, sem, *, n_steps):
    i = pl.program_id(0)
    slot = i % 2
    d = D_MODEL
    rows = TOP_K * CMB_BLK

    @pl.when(i == 0)
    def _():
        _start_row_gather(cur_ref, rows, y_hbm, ybuf.at[0], sem.at[0])

    @pl.when(i + 1 < n_steps)
    def _():
        _start_row_gather(nxt_ref, rows, y_hbm, ybuf.at[1 - slot], sem.at[1 - slot])

    pltpu.make_async_copy(y_hbm.at[pl.ds(0, rows), :], ybuf.at[slot], sem.at[slot]).wait()

    hb = h2_ref[...].astype(BF16)
    g = _dot(hb, swg_ref[...])
    u = _dot(hb, swu_ref[...])
    f = _dot((g * _sigmoid(g) * u).astype(BF16), swd_ref[...])
    w = wcol_ref[...]
    for k in range(TOP_K):
        f = f + ybuf[slot, k * CMB_BLK:(k + 1) * CMB_BLK, :] * w[:, k:k + 1]
    o_ref[...] = xn_ref[...] + mod_ref[:, 5 * d:6 * d] * _rms(f, g_ref[...])


def _combine(pos, y_sorted, wcol, h2_rows, xn_rows, mod3, g, swg, swu, swd, n_steps, per_batch):
    last = n_steps - 1
    full = lambda shape: pl.BlockSpec(shape, lambda n: (0,) * len(shape))
    rows = lambda w_: pl.BlockSpec((CMB_BLK, w_), lambda n: (n, 0))
    n_lat = SEQ // CMB_BLK
    return pl.pallas_call(
        functools.partial(_combine_kernel, n_steps=n_steps),
        grid=(n_steps,),
        in_specs=[
            pl.BlockSpec((None, 1, TOP_K * CMB_BLK), lambda n: (n, 0, 0), memory_space=pltpu.SMEM),
            pl.BlockSpec((None, 1, TOP_K * CMB_BLK), lambda n: (jnp.minimum(n + 1, last), 0, 0),
                         memory_space=pltpu.SMEM),
            pl.BlockSpec(memory_space=pl.ANY),
            pl.BlockSpec((CMB_BLK, TOP_K), lambda n: (n, 0)),
            rows(D_MODEL), rows(D_MODEL),
            pl.BlockSpec((None, 1, 6 * D_MODEL),
                         lambda n: (_mod_row(n % per_batch, n // per_batch, n_lat), 0, 0)),
            full((1, D_MODEL)),
            full((D_MODEL, SHARED_DIM)), full((D_MODEL, SHARED_DIM)), full((SHARED_DIM, D_MODEL)),
        ],
        out_specs=rows(D_MODEL),
        out_shape=jax.ShapeDtypeStruct(xn_rows.shape, F32),
        scratch_shapes=[
            pltpu.VMEM((2, TOP_K * CMB_BLK, D_MODEL), F32),
            pltpu.SemaphoreType.DMA((2,)),
        ],
        compiler_params=pltpu.CompilerParams(dimension_semantics=("arbitrary",)),
        name="combine",
    )(pos, pos, y_sorted, wcol, h2_rows, xn_rows, mod3, g, swg, swu, swd)


def _dispatch_plan(idx, tok_rows, n_blocks):
    onehot = (idx[:, :, None] == jnp.arange(N_EXPERTS, dtype=jnp.int32)).astype(jnp.int32).sum(axis=1)
    rank = jnp.cumsum(onehot, axis=0) - onehot
    counts = onehot.sum(axis=0)
    padded = (counts + MOE_BLK - 1) // MOE_BLK * MOE_BLK
    pad_end = jnp.cumsum(padded)
    pad_start = pad_end - padded
    slot = pad_start[idx] + jnp.take_along_axis(rank, idx, axis=1)
    slot_tok = jnp.zeros((n_blocks * MOE_BLK,), jnp.int32).at[slot.reshape(-1)].set(
        jnp.repeat(tok_rows, TOP_K))
    block_e = jnp.minimum(
        jnp.searchsorted(pad_end, jnp.arange(n_blocks, dtype=jnp.int32) * MOE_BLK, side="right"),
        N_EXPERTS - 1).astype(jnp.int32)
    n_used = (pad_end[-1] // MOE_BLK).astype(jnp.int32).reshape(1)
    return block_e, n_used, slot_tok, slot.astype(jnp.int32)


def _axial_tables(dim):
    t = jnp.arange(SEQ)
    row = (t // GRID_W).astype(F32)
    col = (t % GRID_W).astype(F32)
    quarter = dim // 4
    freqs = ROPE_THETA ** (-jnp.arange(quarter, dtype=F32) / quarter)
    ar, ac = row[:, None] * freqs, col[:, None] * freqs
    ang = jnp.concatenate([ar, ar, ac, ac], axis=-1)
    sign = jnp.where((jnp.arange(dim) // quarter) % 2 == 0, -1.0, 1.0).astype(F32)
    cos = jnp.concatenate([jnp.cos(ang), jnp.ones((CTX_LEN, dim), F32)], axis=0)
    sin = jnp.concatenate([jnp.sin(ang) * sign, jnp.zeros((CTX_LEN, dim), F32)], axis=0)
    return cos, sin


def _rope_tables():
    cos_h, sin_h = _axial_tables(HEAD_DIM)
    cos_r, sin_r = _axial_tables(B_ROPE)
    cosh, sinh = jnp.tile(cos_h, (1, 4)), jnp.tile(sin_h, (1, 4))
    scale_b = (B_NOPE + B_ROPE) ** -0.5
    ones, zeros = jnp.ones((TOK, 64), F32), jnp.zeros((TOK, 64), F32)
    pad1, pad0 = jnp.ones((TOK, 32), F32), jnp.zeros((TOK, 32), F32)
    cosb = jnp.tile(jnp.concatenate([ones, cos_r, pad1], axis=1) * scale_b, (1, 4))
    sinb = jnp.tile(jnp.concatenate([zeros, sin_r, pad0], axis=1) * scale_b, (1, 4))
    cosr = jnp.concatenate([cos_r, jnp.ones((TOK, 96), F32)], axis=1)
    sinr = jnp.concatenate([sin_r, jnp.zeros((TOK, 96), F32)], axis=1)
    return cosh, sinh, cosb, sinb, cosr, sinr


_GQA_ORDER = (0, 2, 1, 3)


def _head_cols(base, order, width=HEAD_DIM):
    return np.concatenate([base + np.arange(h * width, (h + 1) * width) for h in order])


def _w_in_layout(w_in):
    zeros = lambda n: jnp.zeros((D_MODEL, n), w_in.dtype)
    a_q = w_in[:, _head_cols(0, _GQA_ORDER)]
    a_kv = w_in[:, 256:512]
    b_q = jnp.concatenate(
        [jnp.concatenate([w_in[:, 512 + h * 96:512 + (h + 1) * 96], zeros(32)], axis=1)
         for h in range(B_HEADS)], axis=1)
    b_c = w_in[:, 896:1024]
    b_r = jnp.concatenate([w_in[:, 1024:1056], zeros(96)], axis=1)
    c_q = w_in[:, _head_cols(1056, _GQA_ORDER)]
    rest = w_in[:, 1312:2336]
    return jnp.concatenate([a_q, a_kv, b_q, b_c, b_r, c_q, rest], axis=1).astype(BF16)


def _w_kv_layout(w_kv_up):
    zeros = jnp.zeros((B_KV_RANK, 64), w_kv_up.dtype)
    nope = [jnp.concatenate([w_kv_up[:, h * 128:h * 128 + 64], zeros], axis=1) for h in range(B_HEADS)]
    val = [w_kv_up[:, h * 128 + 64:(h + 1) * 128] for h in range(B_HEADS)]
    return jnp.concatenate(nope + val, axis=1).astype(BF16)


def _w_out_layout(w_out):
    rows = np.concatenate([_head_cols(0, _GQA_ORDER), np.arange(256, 512),
                           _head_cols(512, _GQA_ORDER), np.arange(768, 1024)])
    return w_out[rows].astype(BF16)


_ROUTER_ROWS = np.array([g * 8 + j for j in range(8) for g in range(8)])


def kernel(x, c, ctx, c_ctx, ada_w, ada_b, pre_mix_g, post_mix_g, pre_ffn_g, post_ffn_g, w_in, a_q_norm, a_k_norm, b_kv_norm, b_w_kv_up, c_sink, d_rpb, w_out, router_w, router_b, exp_w_gate, exp_w_up, exp_w_down, sh_w_gate, sh_w_up, sh_w_down):
    bsz = x.shape[0]
    depth = ada_w.shape[0]
    xall = jnp.concatenate([x, ctx], axis=1)

    cvec = jnp.zeros((16, D_MODEL), F32).at[:bsz].set(c).at[8].set(c_ctx)
    mod = _ada(cvec, ada_w, ada_b)
    tabs = _rope_tables()
    avg = jnp.kron(jnp.eye(4, dtype=F32), jnp.full((64, 64), 1.0 / 64, F32)).astype(BF16)
    row2 = lambda v: v.reshape(1, -1).astype(F32)

    for l in range(depth):
        with_ctx = l < depth - 1
        mod3 = mod[l].reshape(16, 1, 6 * D_MODEL)

        proj = _inproj(
            xall, mod3, row2(pre_mix_g[l]), _w_in_layout(w_in[l]),
            row2(jnp.tile(a_q_norm[l], 4) * HEAD_DIM ** -0.5), row2(jnp.tile(a_k_norm[l], 2)),
            row2(b_kv_norm[l]), _w_kv_layout(b_w_kv_up[l]), avg, tabs)
        qa, ka, va, qb, kb, vb, qc, kc, vc, qd, kd, vd = proj

        nq = NQ + (CTX_LEN // QBLK if with_ctx else 0)
        full = lambda shape: pl.BlockSpec(shape, lambda b, i: (0,) * len(shape))
        mix_a = _attn_call(_attn_a_kernel, "attn_a", qa, ka, va, [], [], nq)
        mix_b = _attn_call(_attn_b_kernel, "attn_b", qb, kb, vb, [], [], nq)
        sink = jnp.zeros((8, LANES), F32).at[:4].set(
            jnp.broadcast_to(c_sink[l][np.array(_GQA_ORDER)][:, None], (4, LANES)))
        mix_c = _attn_call(_attn_c_kernel, "attn_c", qc, kc, vc, [sink], [full((8, LANES))], nq)
        bias = _na_bias(d_rpb[l])
        mix_d = _attn_call(
            _attn_d_kernel, "attn_d", qd, kd, vd, [bias],
            [pl.BlockSpec((None, D_HEADS, QBLK, NA_WIN), lambda b, i: (_na_pattern(i), 0, 0, 0))], nq)

        nb = (TOK if with_ctx else SEQ) // ROW_BLK
        rb = jnp.broadcast_to(router_b[l][_ROUTER_ROWS][:, None], (N_EXPERTS, ROW_BLK)).astype(F32)
        xn, h2, idx_t, wt_t = _outproj(
            (mix_a, mix_b, mix_c, mix_d), _w_out_layout(w_out[l]), xall, mod3,
            row2(post_mix_g[l]), row2(pre_ffn_g[l]),
            router_w[l].T[_ROUTER_ROWS].astype(BF16), rb, nb)

        t_used = nb * ROW_BLK
        n_tok = bsz * t_used
        idx = idx_t.transpose(0, 1, 3, 2).reshape(n_tok, TOP_K)
        wts = wt_t.transpose(0, 1, 3, 2).reshape(n_tok, TOP_K)
        tok_rows = jnp.arange(n_tok, dtype=jnp.int32)
        n_blocks = -(-(n_tok * TOP_K + N_EXPERTS * (MOE_BLK - 1)) // MOE_BLK)
        block_e, n_used, slot_tok, slot = _dispatch_plan(idx, tok_rows, n_blocks)

        h2_rows = h2.reshape(n_tok, D_MODEL)
        y_sorted = _experts(block_e, n_used, slot_tok, h2_rows,
                            exp_w_gate, exp_w_up, exp_w_down, l, n_blocks)

        per_batch = t_used // CMB_BLK
        n_steps = bsz * per_batch
        pos = slot.reshape(n_steps, CMB_BLK, TOP_K).transpose(0, 2, 1).reshape(n_steps, 1, TOP_K * CMB_BLK)
        xall = _combine(
            pos, y_sorted, wts, h2_rows, xn.reshape(n_tok, D_MODEL), mod3, row2(post_ffn_g[l]),
            sh_w_gate[l].astype(BF16), sh_w_up[l].astype(BF16), sh_w_down[l].astype(BF16),
            n_steps, per_batch).reshape(bsz, t_used, D_MODEL)

    return xall[:, :SEQ]
```

```python
import functools

import numpy as np
import jax
import jax.numpy as jnp
from jax import lax
from jax.experimental import pallas as pl
from jax.experimental.pallas import tpu as pltpu

F32 = jnp.float32
BF16 = jnp.bfloat16

D_MODEL = 1024
SEQ = 2048
CTX_LEN = 256
TOK = SEQ + CTX_LEN
GRID_W = 64
HEAD_DIM = 64
ROPE_THETA = 10000.0
EPS = 1e-6
NEG = -1e30

A_HEADS, A_KV_HEADS = 4, 2
B_HEADS, B_NOPE, B_ROPE, B_V, B_KV_RANK = 4, 64, 32, 64, 128
C_HEADS, C_KV_HEADS, C_WINDOW = 4, 2, 128
D_HEADS, NA_ROWS, NA_COLS = 4, 8, 16

N_EXPERTS, TOP_K, N_GROUPS, TOPK_GROUPS = 64, 8, 8, 4
EXPERT_DIM, SHARED_DIM = 256, 256
ROUTED_SCALE = 2.5

LANES = 128
QBLK = 128
ROW_BLK = 256
MOE_BLK = 256
TILE_ROWS = D_MODEL // LANES
PIECE = 8
PIECE_ROWS = PIECE * TILE_ROWS
PIECES_MAX = 320
STG_ROWS = PIECES_MAX * PIECE_ROWS
TAIL_MAX = 2048
VMEM_LIMIT = 48 * 1024 * 1024
NQ = SEQ // QBLK
NA_WIN = 10 * GRID_W

_AQ, _AK, _AV, _BQ, _BC, _BR = 0, 256, 384, 512, 1024, 1152
_CQ, _CK, _CV, _DQ, _DK, _DV = 1280, 1536, 1664, 1792, 2048, 2304
IN_W = 2560

_NT = (((1,), (1,)), ((), ()))


def _sigmoid(x):
    return 1.0 / (1.0 + jnp.exp(-x))


def _rms(x, g):
    return x * lax.rsqrt(jnp.mean(x * x, axis=-1, keepdims=True) + EPS) * g


def _dot(a, b):
    return jnp.dot(a, b, preferred_element_type=F32)


def _dot_nt(a, b):
    return lax.dot_general(a, b, _NT, preferred_element_type=F32)


def _ada_kernel(c_ref, w_ref, b_ref, o_ref):
    c = c_ref[...]
    a = (c * _sigmoid(c)).astype(BF16)
    o_ref[...] = _dot(a, w_ref[...].astype(BF16)) + b_ref[...]


def _ada(cvec, ada_w, ada_b):
    depth = ada_w.shape[0]
    n = ada_w.shape[2]
    tn = 1536
    return pl.pallas_call(
        _ada_kernel,
        grid=(depth, n // tn),
        in_specs=[
            pl.BlockSpec((16, D_MODEL), lambda l, j: (0, 0)),
            pl.BlockSpec((None, D_MODEL, tn), lambda l, j: (l, 0, j)),
            pl.BlockSpec((None, 1, tn), lambda l, j: (l, 0, j)),
        ],
        out_specs=pl.BlockSpec((None, 16, tn), lambda l, j: (l, 0, j)),
        out_shape=jax.ShapeDtypeStruct((depth, 16, n), F32),
        name="ada",
    )(cvec, ada_w, ada_b.reshape(depth, 1, n))


def _rope(x, cos, sin, chunk):
    outs = []
    for c in range(x.shape[1] // LANES):
        sl = slice(c * LANES, (c + 1) * LANES)
        xs = x[:, sl]
        lane = lax.broadcasted_iota(jnp.int32, xs.shape, 1)
        first = (lane & chunk) == 0
        rot = jnp.where(first, pltpu.roll(xs, LANES - chunk, 1), pltpu.roll(xs, chunk, 1))
        outs.append(xs * cos[:, sl] + rot * sin[:, sl])
    return outs[0] if len(outs) == 1 else jnp.concatenate(outs, axis=1)


def _head_norm(x, avg):
    sq = x * x
    hi = sq.astype(BF16)
    lo = (sq - hi.astype(F32)).astype(BF16)
    ms = _dot(hi, avg) + _dot(lo, avg)
    return x * lax.rsqrt(ms + EPS)


def _inproj_kernel(x_ref, mod_ref, g_ref, w_ref, qg_ref, kg_ref, kvg_ref, wkv_ref, avg_ref,
                   cosh_ref, sinh_ref, cosb_ref, sinb_ref, cosr_ref, sinr_ref,
                   qa_ref, ka_ref, va_ref, qb_ref, kb_ref, vb_ref,
                   qc_ref, kc_ref, vc_ref, qd_ref, kd_ref, vd_ref):
    d = D_MODEL
    h = _rms(x_ref[...], g_ref[...]) * (1.0 + mod_ref[:, d:2 * d]) + mod_ref[:, 0:d]
    p = _dot(h.astype(BF16), w_ref[...])
    cosh, sinh = cosh_ref[...], sinh_ref[...]
    avg = avg_ref[...]

    qa = _head_norm(p[:, _AQ:_AQ + 256], avg) * qg_ref[...]
    qa_ref[...] = _rope(qa, cosh, sinh, 16).astype(BF16)
    ka = _head_norm(p[:, _AK:_AK + 128], avg[0:128, 0:128]) * kg_ref[...]
    ka_ref[...] = _rope(ka, cosh[:, 0:128], sinh[:, 0:128], 16).astype(BF16)
    va_ref[...] = p[:, _AV:_AV + 128].astype(BF16)

    qb_ref[...] = _rope(p[:, _BQ:_BQ + 512], cosb_ref[...], sinb_ref[...], 8).astype(BF16)
    cn = _rms(p[:, _BC:_BC + 128], kvg_ref[...])
    kv = _dot(cn.astype(BF16), wkv_ref[...])
    kr = _rope(p[:, _BR:_BR + 128], cosr_ref[...], sinr_ref[...], 8)
    kr = pltpu.roll(kr, 64, 1)
    kb_ref[...] = jnp.concatenate(
        [kv[:, hh * 128:(hh + 1) * 128] + kr for hh in range(B_HEADS)], axis=1).astype(BF16)
    vb_ref[...] = kv[:, 512:768].astype(BF16)

    qc_ref[...] = _rope(p[:, _CQ:_CQ + 256] * 0.125, cosh, sinh, 16).astype(BF16)
    kc_ref[...] = _rope(p[:, _CK:_CK + 128], cosh[:, 0:128], sinh[:, 0:128], 16).astype(BF16)
    vc_ref[...] = p[:, _CV:_CV + 128].astype(BF16)

    qd_ref[...] = (p[:, _DQ:_DQ + 256] * 0.125).astype(BF16)
    kd_ref[...] = p[:, _DK:_DK + 256].astype(BF16)
    vd_ref[...] = p[:, _DV:_DV + 256].astype(BF16)


def _mod_row(j, b, n_lat_blocks):
    return jnp.where(j < n_lat_blocks, b, 8)


def _inproj(xall, mod3, g, w, qg, kg, kvg, wkv, avg, tabs):
    bsz = xall.shape[0]
    nb = TOK // ROW_BLK
    n_lat = SEQ // ROW_BLK
    full = lambda shape: pl.BlockSpec(shape, lambda b, j: (0,) * len(shape))
    row = lambda w_: pl.BlockSpec((ROW_BLK, w_), lambda b, j: (j, 0))
    out = lambda w_: pl.BlockSpec((None, ROW_BLK, w_), lambda b, j: (b, j, 0))
    widths = (256, 128, 128, 512, 512, 256, 256, 128, 128, 256, 256, 256)
    return pl.pallas_call(
        _inproj_kernel,
        grid=(bsz, nb),
        in_specs=[
            pl.BlockSpec((None, ROW_BLK, D_MODEL), lambda b, j: (b, j, 0)),
            pl.BlockSpec((None, 1, 6 * D_MODEL), lambda b, j: (_mod_row(j, b, n_lat), 0, 0)),
            full((1, D_MODEL)), full((D_MODEL, IN_W)),
            full((1, 256)), full((1, 128)), full((1, 128)), full((128, 768)), full((256, 256)),
            row(256), row(256), row(512), row(512), row(128), row(128),
        ],
        out_specs=[out(w_) for w_ in widths],
        out_shape=[jax.ShapeDtypeStruct((bsz, TOK, w_), BF16) for w_ in widths],
        name="inproj",
    )(xall, mod3, g, w, qg, kg, kvg, wkv, avg, *tabs)


def _softmax_pv(segs, sink=None):
    m = None
    for s, _ in segs:
        ms = jnp.max(s, axis=-1, keepdims=True)
        m = ms if m is None else jnp.maximum(m, ms)
    if sink is not None:
        m = jnp.maximum(m, sink)
    l = None
    o = None
    for s, v in segs:
        p = jnp.exp(s - m)
        ls = jnp.sum(p, axis=-1, keepdims=True)
        os_ = _dot(p.astype(BF16), v)
        l = ls if l is None else l + ls
        o = os_ if o is None else o + os_
    if sink is not None:
        l = l + jnp.exp(sink - m)
    return o / l


def _lane_lo(dtype_shape):
    return lax.broadcasted_iota(jnp.int32, dtype_shape, 1) < HEAD_DIM


def _half(qt, hh):
    lo = _lane_lo(qt.shape)
    return jnp.where(lo if hh == 0 else jnp.logical_not(lo), qt, jnp.zeros_like(qt))


def _merge_halves(o0, o1):
    return jnp.where(_lane_lo(o0.shape), o0, o1)


def _attn_a_kernel(q_ref, k_ref, v_ref, o_ref):
    i = pl.program_id(1)

    def run(k, v):
        for t in range(2):
            qt = q_ref[:, t * 128:(t + 1) * 128]
            o = [_softmax_pv([(_dot_nt(_half(qt, hh), k), v)]) for hh in range(2)]
            o_ref[:, t * 128:(t + 1) * 128] = _merge_halves(o[0], o[1]).astype(o_ref.dtype)

    @pl.when(i < NQ)
    def _():
        run(k_ref[...], v_ref[...])

    @pl.when(i >= NQ)
    def _():
        run(k_ref[SEQ:TOK, :], v_ref[SEQ:TOK, :])


def _attn_b_kernel(q_ref, k_ref, v_ref, o_ref):
    i = pl.program_id(1)

    def run(lo_, hi_):
        for t in range(2):
            v = v_ref[lo_:hi_, t * 128:(t + 1) * 128]
            o = []
            for hh in range(2):
                hd = 2 * t + hh
                q = q_ref[:, hd * 128:(hd + 1) * 128]
                k = k_ref[lo_:hi_, hd * 128:(hd + 1) * 128]
                o.append(_softmax_pv([(_dot_nt(q, k), v)]))
            o_ref[:, t * 128:(t + 1) * 128] = _merge_halves(o[0], o[1]).astype(o_ref.dtype)

    @pl.when(i < NQ)
    def _():
        run(0, TOK)

    @pl.when(i >= NQ)
    def _():
        run(SEQ, TOK)


def _attn_c_kernel(q_ref, k_ref, v_ref, sink_ref, o_ref):
    i = pl.program_id(1)
    band = 3 * QBLK
    kc, vc = k_ref[SEQ:TOK, :], v_ref[SEQ:TOK, :]

    def run(latent):
        if latent:
            start = pl.multiple_of(jnp.clip((i - 1) * QBLK, 0, SEQ - band), QBLK)
            kb, vb = k_ref[pl.ds(start, band), :], v_ref[pl.ds(start, band), :]
            qpos = i * QBLK + lax.broadcasted_iota(jnp.int32, (QBLK, band), 0)
            kpos = start + lax.broadcasted_iota(jnp.int32, (QBLK, band), 1)
            valid = jnp.abs(qpos - kpos) <= C_WINDOW
        for t in range(2):
            qt = q_ref[:, t * 128:(t + 1) * 128]
            o = []
            for hh in range(2):
                qm = _half(qt, hh)
                segs = [(_dot_nt(qm, kc), vc)]
                if latent:
                    segs.insert(0, (jnp.where(valid, _dot_nt(qm, kb), NEG), vb))
                j = 2 * t + hh
                o.append(_softmax_pv(segs, sink=sink_ref[j:j + 1, 0:1]))
            o_ref[:, t * 128:(t + 1) * 128] = _merge_halves(o[0], o[1]).astype(o_ref.dtype)

    @pl.when(i < NQ)
    def _():
        run(True)

    @pl.when(i >= NQ)
    def _():
        run(False)


def _attn_d_kernel(q_ref, k_ref, v_ref, bias_ref, o_ref):
    i = pl.program_id(1)

    def run(latent):
        if latent:
            start = pl.multiple_of(jnp.clip((i - 2) * QBLK, 0, SEQ - NA_WIN), QBLK)
        for t in range(2):
            sl = slice(t * 128, (t + 1) * 128)
            qt = q_ref[:, sl]
            kc, vc = k_ref[SEQ:TOK, sl], v_ref[SEQ:TOK, sl]
            if latent:
                kw, vw = k_ref[pl.ds(start, NA_WIN), sl], v_ref[pl.ds(start, NA_WIN), sl]
            o = []
            for hh in range(2):
                qm = _half(qt, hh)
                segs = [(_dot_nt(qm, kc), vc)]
                if latent:
                    segs.insert(0, (_dot_nt(qm, kw) + bias_ref[2 * t + hh], vw))
                o.append(_softmax_pv(segs))
            o_ref[:, sl] = _merge_halves(o[0], o[1]).astype(o_ref.dtype)

    @pl.when(i < NQ)
    def _():
        run(True)

    @pl.when(i >= NQ)
    def _():
        run(False)


def _attn_call(kernel, name, q, k, v, extra, extra_specs, n_blocks):
    bsz = q.shape[0]
    return pl.pallas_call(
        kernel,
        grid=(bsz, n_blocks),
        in_specs=[
            pl.BlockSpec((None, QBLK, q.shape[2]), lambda b, i: (b, i, 0)),
            pl.BlockSpec((None, TOK, k.shape[2]), lambda b, i: (b, 0, 0)),
            pl.BlockSpec((None, TOK, v.shape[2]), lambda b, i: (b, 0, 0)),
        ] + extra_specs,
        out_specs=pl.BlockSpec((None, QBLK, 256), lambda b, i: (b, i, 0)),
        out_shape=jax.ShapeDtypeStruct((bsz, n_blocks * QBLK, 256), BF16),
        name=name,
    )(q, k, v, *extra)


def _na_pattern(i):
    return jnp.where(i < 2, i, jnp.where(i < NQ - 2, 2, jnp.minimum(i, NQ - 1) - (NQ - 5)))


def _na_bias(rpb):
    rows = SEQ // GRID_W
    n_dr, n_dc = 2 * NA_ROWS - 1, 2 * NA_COLS - 1
    qc = np.arange(GRID_W)
    dc = np.clip(qc[None, :] - qc[:, None] + NA_COLS - 1, 0, n_dc - 1)
    sel_c = jnp.asarray(np.eye(n_dc, dtype=np.float32)[dc])
    cs = np.clip(qc - NA_COLS // 2, 0, GRID_W - NA_COLS)[:, None]
    col_ok = (qc[None, :] >= cs) & (qc[None, :] < cs + NA_COLS)
    out = []
    for i in (0, 1, 2, NQ - 2, NQ - 1):
        start_row = min(max(2 * i - 4, 0), rows - 10)
        qr = 2 * i + np.arange(2)
        kr = start_row + np.arange(10)
        rs = np.clip(qr - NA_ROWS // 2, 0, rows - NA_ROWS)[:, None]
        row_ok = (kr[None] >= rs) & (kr[None] < rs + NA_ROWS)
        dr = np.clip(kr[None] - qr[:, None] + NA_ROWS - 1, 0, n_dr - 1)
        sel_r = jnp.asarray(np.eye(n_dr, dtype=np.float32)[dr])
        b = jnp.einsum("hrc,abr,qkc->haqbk", rpb.astype(F32), sel_r, sel_c,
                       precision=lax.Precision.HIGHEST)
        valid = row_ok[:, None, :, None] & col_ok[None, :, None, :]
        out.append(jnp.where(valid[None], b, NEG).reshape(D_HEADS, QBLK, NA_WIN))
    return jnp.stack(out)


def _outproj_kernel(ma_ref, mb_ref, mc_ref, md_ref, w_ref, x_ref, mod_ref, g1_ref, g2_ref,
                    wr_ref, rb_ref, xn_ref, h2b_ref, h2t_ref, wt_ref, ls_ref, cnt_ref):
    d = D_MODEL
    mix = jnp.concatenate([ma_ref[...], mb_ref[...], mc_ref[...], md_ref[...]], axis=1)
    y = _rms(_dot(mix, w_ref[...]), g1_ref[...])
    xn = x_ref[...] + mod_ref[:, 2 * d:3 * d] * y
    xn_ref[...] = xn
    h2 = _rms(xn, g2_ref[...]) * (1.0 + mod_ref[:, 4 * d:5 * d]) + mod_ref[:, 3 * d:4 * d]
    n = h2.shape[0]
    h2b_ref[...] = h2.astype(BF16)
    for s in range(TILE_ROWS):
        h2t_ref[pl.ds(s, n, stride=TILE_ROWS), :] = h2[:, s * LANES:(s + 1) * LANES]

    s = _sigmoid(_dot_nt(wr_ref[...], h2.astype(BF16)))
    sb = s + rb_ref[...]
    per = N_EXPERTS // N_GROUPS
    slab = [sb[j * 8:(j + 1) * 8, :] for j in range(per)]
    sraw = [s[j * 8:(j + 1) * 8, :] for j in range(per)]
    giota = lax.broadcasted_iota(jnp.int32, (N_GROUPS, n), 0)
    eid = [giota * per + j for j in range(per)]
    ninf = -jnp.inf

    m1 = functools.reduce(jnp.maximum, slab)
    j1 = functools.reduce(jnp.minimum, [jnp.where(slab[j] == m1, j, per) for j in range(per)])
    m2 = functools.reduce(jnp.maximum, [jnp.where(j1 == j, ninf, slab[j]) for j in range(per)])
    gs = m1 + m2

    gsel = jnp.zeros((N_GROUPS, n), jnp.bool_)
    for _ in range(TOPK_GROUPS):
        m = jnp.max(gs, axis=0, keepdims=True)
        gi = jnp.min(jnp.where(gs == m, giota, N_GROUPS), axis=0, keepdims=True)
        hit = giota == gi
        gsel = jnp.logical_or(gsel, hit)
        gs = jnp.where(hit, ninf, gs)
    slab = [jnp.where(gsel, sj, ninf) for sj in slab]

    def pick(hits, vals):
        return jnp.sum(functools.reduce(jnp.add, [jnp.where(hits[j], vals[j], 0.0) for j in range(per)]),
                       axis=0, keepdims=True)

    kiota = lax.broadcasted_iota(jnp.int32, (TOP_K, n), 0)
    wts = jnp.zeros((TOP_K, n), F32)
    chosen = []
    for k in range(TOP_K):
        m = jnp.max(functools.reduce(jnp.maximum, slab), axis=0, keepdims=True)
        cand = functools.reduce(jnp.minimum,
                                [jnp.where(slab[j] == m, eid[j], N_EXPERTS) for j in range(per)])
        ei = jnp.min(cand, axis=0, keepdims=True)
        hits = [eid[j] == ei for j in range(per)]
        slab = [jnp.where(hits[j], ninf, slab[j]) for j in range(per)]
        wts = jnp.where(kiota == k, pick(hits, sraw), wts)
        chosen.append(ei)
    wt_ref[...] = wts / jnp.sum(wts, axis=0, keepdims=True) * ROUTED_SCALE

    hits_k = [[eid[j] == chosen[k] for j in range(per)] for k in range(TOP_K)]
    sel = jnp.concatenate(
        [functools.reduce(jnp.add, [jnp.where(hits_k[k][j], 1.0, 0.0) for k in range(TOP_K)])
         for j in range(per)], axis=0)
    tri = jnp.where(lax.broadcasted_iota(jnp.int32, (n, n), 0) < lax.broadcasted_iota(jnp.int32, (n, n), 1),
                    1.0, 0.0).astype(BF16)
    before = _dot(sel.astype(BF16), tri)
    cnt = jnp.sum(sel, axis=1, keepdims=True)
    run = jnp.floor((cnt + (PIECE - 1.0)) * (1.0 / PIECE)) * PIECE
    low = jnp.where(lax.broadcasted_iota(jnp.int32, (N_EXPERTS, N_EXPERTS), 1)
                    < lax.broadcasted_iota(jnp.int32, (N_EXPERTS, N_EXPERTS), 0), 1.0, 0.0).astype(BF16)
    start = _dot(low, jnp.broadcast_to(run, (N_EXPERTS, LANES)).astype(BF16))[:, 0:1]
    base = before + start
    base = [base[j * 8:(j + 1) * 8, :] for j in range(per)]
    lsl = jnp.zeros((TOP_K, n), F32)
    for k in range(TOP_K):
        lsl = jnp.where(kiota == k, pick(hits_k[k], base), lsl)
    ls_ref[...] = lsl.astype(jnp.int32)
    cnt_ref[...] = jnp.broadcast_to(cnt, (N_EXPERTS, LANES)).astype(jnp.int32)


def _outproj(mixes, w, xall, mod3, g1, g2, wr, rb, nb):
    bsz = xall.shape[0]
    n_lat = SEQ // ROW_BLK
    full = lambda shape: pl.BlockSpec(shape, lambda b, j: (0,) * len(shape))
    rows = lambda w_: pl.BlockSpec((None, ROW_BLK, w_), lambda b, j: (b, j, 0))
    route = pl.BlockSpec((None, None, TOP_K, ROW_BLK), lambda b, j: (b, j, 0, 0))
    return pl.pallas_call(
        _outproj_kernel,
        grid=(bsz, nb),
        in_specs=[rows(256)] * 4 + [
            full((D_MODEL, D_MODEL)), rows(D_MODEL),
            pl.BlockSpec((None, 1, 6 * D_MODEL), lambda b, j: (_mod_row(j, b, n_lat), 0, 0)),
            full((1, D_MODEL)), full((1, D_MODEL)),
            full((N_EXPERTS, D_MODEL)), full((N_EXPERTS, ROW_BLK)),
        ],
        out_specs=[
            rows(D_MODEL), rows(D_MODEL),
            pl.BlockSpec((None, ROW_BLK * TILE_ROWS, LANES), lambda b, j: (b, j, 0)),
            route, route,
            pl.BlockSpec((None, None, N_EXPERTS, LANES), lambda b, j: (b, j, 0, 0)),
        ],
        out_shape=[
            jax.ShapeDtypeStruct((bsz, nb * ROW_BLK, D_MODEL), F32),
            jax.ShapeDtypeStruct((bsz, nb * ROW_BLK, D_MODEL), BF16),
            jax.ShapeDtypeStruct((bsz, nb * ROW_BLK * TILE_ROWS, LANES), F32),
            jax.ShapeDtypeStruct((bsz, nb, TOP_K, ROW_BLK), F32),
            jax.ShapeDtypeStruct((bsz, nb, TOP_K, ROW_BLK), jnp.int32),
            jax.ShapeDtypeStruct((bsz, nb, N_EXPERTS, LANES), jnp.int32),
        ],
        name="outproj",
    )(*mixes, w, xall, mod3, g1, g2, wr, rb)


def _tile(t):
    return pl.ds(pl.multiple_of(t * TILE_ROWS, TILE_ROWS), TILE_ROWS)


def _piece(p):
    return pl.ds(pl.multiple_of(p * PIECE_ROWS, PIECE_ROWS), PIECE_ROWS)


def _dispatch_kernel(np_ref, nu_ref, nt_ref, ls_ref, pd_ref, td_ref, h_ref, xs_hbm,
                     stg, zblk, sem, zsem, *, n_steps, n_blocks):
    i = pl.program_id(0)
    slot = i % 2

    def piece_copy(s, p, d):
        return pltpu.make_async_copy(stg.at[s, _piece(p), :], xs_hbm.at[_piece(d), :], sem.at[s])

    def wait_pieces(s, n):
        def body(p, carry):
            piece_copy(s, 0, 0).wait()
            return carry
        lax.fori_loop(0, n, body, 0)

    def zero_piece_copy(d):
        return pltpu.make_async_copy(zblk.at[pl.ds(0, PIECE_ROWS), :], xs_hbm.at[_piece(d), :], zsem.at[0])

    def zero_block_copy(b):
        rows = pl.ds(pl.multiple_of(b * (MOE_BLK * TILE_ROWS), MOE_BLK * TILE_ROWS), MOE_BLK * TILE_ROWS)
        return pltpu.make_async_copy(zblk, xs_hbm.at[rows, :], zsem.at[1])

    @pl.when(i == 0)
    def _():
        zblk[...] = jnp.zeros_like(zblk)
        for s in range(2):
            def zero(c, carry, s=s):
                stg[s, pl.ds(pl.multiple_of(c * 2048, 2048), 2048), :] = jnp.zeros((2048, LANES), F32)
                return carry
            lax.fori_loop(0, STG_ROWS // 2048, zero, 0)

    @pl.when(i >= 2)
    def _():
        wait_pieces(slot, np_ref[jnp.maximum(i - 2, 0)])

    def scatter_token(t, carry):
        tile = h_ref[_tile(t), :]
        for k in range(TOP_K):
            stg[slot, _tile(ls_ref[0, k * ROW_BLK + t]), :] = tile
        return carry
    lax.fori_loop(0, ROW_BLK, scatter_token, 0, unroll=2)

    def send(p, carry):
        piece_copy(slot, p, pd_ref[0, p]).start()
        return carry
    lax.fori_loop(0, np_ref[i], send, 0)

    @pl.when(i == n_steps - 1)
    def _():
        def tail(p, carry):
            zero_piece_copy(td_ref[0, p]).start()
            return carry
        lax.fori_loop(0, nt_ref[0], tail, 0)

        def blk(b, carry):
            zero_block_copy(b).start()
            return carry
        lax.fori_loop(nu_ref[0], n_blocks, blk, 0)

        if n_steps > 1:
            wait_pieces(1 - slot, np_ref[jnp.maximum(i - 1, 0)])
        wait_pieces(slot, np_ref[i])

        def tail_wait(p, carry):
            zero_piece_copy(0).wait()
            return carry
        lax.fori_loop(0, nt_ref[0], tail_wait, 0)

        def blk_wait(b, carry):
            zero_block_copy(0).wait()
            return carry
        lax.fori_loop(nu_ref[0], n_blocks, blk_wait, 0)


def _dispatch(npieces, n_used, ntail, lslot, piece_dst, tail_dst, h2t, n_steps, n_blocks):
    smem = lambda n, imap: pl.BlockSpec((None, 1, n), imap, memory_space=pltpu.SMEM)
    grid_spec = pltpu.PrefetchScalarGridSpec(
        num_scalar_prefetch=3,
        grid=(n_steps,),
        in_specs=[
            smem(TOP_K * ROW_BLK, lambda i, *_: (i, 0, 0)),
            smem(PIECES_MAX, lambda i, *_: (i, 0, 0)),
            smem(TAIL_MAX, lambda i, *_: (0, 0, 0)),
            pl.BlockSpec((ROW_BLK * TILE_ROWS, LANES), lambda i, *_: (i, 0)),
        ],
        out_specs=pl.BlockSpec(memory_space=pl.ANY),
        scratch_shapes=[
            pltpu.VMEM((2, STG_ROWS, LANES), F32),
            pltpu.VMEM((MOE_BLK * TILE_ROWS, LANES), F32),
            pltpu.SemaphoreType.DMA((2,)),
            pltpu.SemaphoreType.DMA((2,)),
        ],
    )
    return pl.pallas_call(
        functools.partial(_dispatch_kernel, n_steps=n_steps, n_blocks=n_blocks),
        grid_spec=grid_spec,
        out_shape=jax.ShapeDtypeStruct((n_blocks * MOE_BLK * TILE_ROWS, LANES), F32),
        compiler_params=pltpu.CompilerParams(dimension_semantics=("arbitrary",),
                                             vmem_limit_bytes=VMEM_LIMIT),
        name="dispatch",
    )(npieces, n_used, ntail, lslot, piece_dst, tail_dst, h2t)


def _from_tiles(ref, n):
    return jnp.concatenate([ref[pl.ds(s, n, stride=TILE_ROWS), :] for s in range(TILE_ROWS)], axis=1)


def _experts_kernel(be_ref, nu_ref, x_ref, wg_ref, wu_ref, wd_ref, y_ref, wgb, wub, wdb):
    i = pl.program_id(0)

    @pl.when(i < nu_ref[0])
    def _():
        @pl.when(jnp.logical_or(i == 0, be_ref[i] != be_ref[jnp.maximum(i - 1, 0)]))
        def _():
            wgb[...] = wg_ref[...].astype(BF16)
            wub[...] = wu_ref[...].astype(BF16)
            wdb[...] = wd_ref[...].astype(BF16)

        x = _from_tiles(x_ref, MOE_BLK).astype(BF16)
        g = _dot(x, wgb[...])
        u = _dot(x, wub[...])
        y = _dot((g * _sigmoid(g) * u).astype(BF16), wdb[...])
        for s in range(TILE_ROWS):
            y_ref[pl.ds(s, MOE_BLK, stride=TILE_ROWS), :] = y[:, s * LANES:(s + 1) * LANES]

    @pl.when(i >= nu_ref[0])
    def _():
        y_ref[...] = jnp.zeros_like(y_ref)


def _experts(block_e, n_used, x_sorted, wg, wu, wd, layer, n_blocks):
    wspec = lambda shape: pl.BlockSpec((None, None) + shape, lambda i, be, nu: (layer, be[i], 0, 0))
    rows = MOE_BLK * TILE_ROWS
    grid_spec = pltpu.PrefetchScalarGridSpec(
        num_scalar_prefetch=2,
        grid=(n_blocks,),
        in_specs=[
            pl.BlockSpec((rows, LANES), lambda i, be, nu: (jnp.minimum(i, nu[0] - 1), 0)),
            wspec((D_MODEL, EXPERT_DIM)), wspec((D_MODEL, EXPERT_DIM)), wspec((EXPERT_DIM, D_MODEL)),
        ],
        out_specs=pl.BlockSpec((rows, LANES), lambda i, be, nu: (i, 0)),
        scratch_shapes=[
            pltpu.VMEM((D_MODEL, EXPERT_DIM), BF16),
            pltpu.VMEM((D_MODEL, EXPERT_DIM), BF16),
            pltpu.VMEM((EXPERT_DIM, D_MODEL), BF16),
        ],
    )
    return pl.pallas_call(
        _experts_kernel,
        grid_spec=grid_spec,
        out_shape=jax.ShapeDtypeStruct((n_blocks * rows, LANES), F32),
        compiler_params=pltpu.CompilerParams(dimension_semantics=("arbitrary",)),
        name="experts",
    )(block_e, n_used, x_sorted, wg, wu, wd)


def _combine_kernel(np_ref, ls_ref, wt_ref, cur_ref, nxt_ref, y_hbm, h2b_ref, xn_ref, mod_ref, g_ref,
                    swg_ref, swu_ref, swd_ref, o_ref, stg, acc, sem, *, n_steps):
    i = pl.program_id(0)
    slot = i % 2
    d = D_MODEL

    def piece_copy(s, p, src):
        return pltpu.make_async_copy(y_hbm.at[_piece(src), :], stg.at[s, _piece(p), :], sem.at[s])

    def fetch(s, n, pd_ref):
        def body(p, carry):
            piece_copy(s, p, pd_ref[0, p]).start()
            return carry
        lax.fori_loop(0, n, body, 0)

    @pl.when(i == 0)
    def _():
        fetch(0, np_ref[0], cur_ref)

    @pl.when(i + 1 < n_steps)
    def _():
        fetch(1 - slot, np_ref[jnp.minimum(i + 1, n_steps - 1)], nxt_ref)

    def wait(p, carry):
        piece_copy(slot, 0, 0).wait()
        return carry
    lax.fori_loop(0, np_ref[i], wait, 0)

    def gather_token(t, carry):
        a = None
        for k in range(TOP_K):
            v = stg[slot, _tile(ls_ref[0, k * ROW_BLK + t]), :] * wt_ref[0, k * ROW_BLK + t]
            a = v if a is None else a + v
        acc[_tile(t), :] = a
        return carry
    lax.fori_loop(0, ROW_BLK, gather_token, 0, unroll=2)

    hb = h2b_ref[...]
    g = _dot(hb, swg_ref[...])
    u = _dot(hb, swu_ref[...])
    f = _dot((g * _sigmoid(g) * u).astype(BF16), swd_ref[...]) + _from_tiles(acc, ROW_BLK)
    o_ref[...] = xn_ref[...] + mod_ref[:, 5 * d:6 * d] * _rms(f, g_ref[...])


def _combine(npieces, lslot, wts, piece_src, y_sorted, h2b_rows, xn_rows, mod3, g, swg, swu, swd,
             n_steps, per_batch):
    last = n_steps - 1
    full = lambda shape: pl.BlockSpec(shape, lambda n, *_: (0,) * len(shape))
    rows = lambda w_: pl.BlockSpec((ROW_BLK, w_), lambda n, *_: (n, 0))
    smem = lambda w_, imap: pl.BlockSpec((None, 1, w_), imap, memory_space=pltpu.SMEM)
    n_lat = SEQ // ROW_BLK
    grid_spec = pltpu.PrefetchScalarGridSpec(
        num_scalar_prefetch=1,
        grid=(n_steps,),
        in_specs=[
            smem(TOP_K * ROW_BLK, lambda n, *_: (n, 0, 0)),
            smem(TOP_K * ROW_BLK, lambda n, *_: (n, 0, 0)),
            smem(PIECES_MAX, lambda n, *_: (n, 0, 0)),
            smem(PIECES_MAX, lambda n, *_: (jnp.minimum(n + 1, last), 0, 0)),
            pl.BlockSpec(memory_space=pl.ANY),
            rows(D_MODEL), rows(D_MODEL),
            pl.BlockSpec((None, 1, 6 * D_MODEL),
                         lambda n, *_: (_mod_row(n % per_batch, n // per_batch, n_lat), 0, 0)),
            full((1, D_MODEL)),
            full((D_MODEL, SHARED_DIM)), full((D_MODEL, SHARED_DIM)), full((SHARED_DIM, D_MODEL)),
        ],
        out_specs=rows(D_MODEL),
        scratch_shapes=[
            pltpu.VMEM((2, STG_ROWS, LANES), F32),
            pltpu.VMEM((ROW_BLK * TILE_ROWS, LANES), F32),
            pltpu.SemaphoreType.DMA((2,)),
        ],
    )
    return pl.pallas_call(
        functools.partial(_combine_kernel, n_steps=n_steps),
        grid_spec=grid_spec,
        out_shape=jax.ShapeDtypeStruct(xn_rows.shape, F32),
        compiler_params=pltpu.CompilerParams(dimension_semantics=("arbitrary",),
                                             vmem_limit_bytes=VMEM_LIMIT),
        name="combine",
    )(npieces, lslot, wts, piece_src, piece_src, y_sorted, h2b_rows, xn_rows, mod3, g, swg, swu, swd)


def _moe_plan(cnt_rows, n_blocks):
    i32 = jnp.int32
    per_blk = MOE_BLK // PIECE
    to_expert = lambda v: v.reshape(v.shape[:-1] + (8, 8)).swapaxes(-1, -2).reshape(v.shape)
    runs = (cnt_rows + PIECE - 1) // PIECE
    loc_end = jnp.cumsum(runs, axis=1)
    loc = loc_end - runs
    npieces = loc_end[:, -1].astype(i32)
    reg = to_expert(runs.sum(axis=0))
    padded = (reg + per_blk - 1) // per_blk * per_blk
    pad_end = jnp.cumsum(padded)
    pad_start = pad_end - padded
    off = to_expert(pad_start)[None, :] + jnp.cumsum(runs, axis=0) - runs
    p = jnp.arange(PIECES_MAX, dtype=i32)[None, :, None]
    mine = (loc[:, None, :] <= p) & (p < loc_end[:, None, :])
    piece_dst = jnp.sum(jnp.where(mine, (off - loc)[:, None, :], 0), axis=-1) + p[:, :, 0]
    tail_n = padded - reg
    t_end = jnp.cumsum(tail_n)
    t_beg = t_end - tail_n
    q = jnp.arange(TAIL_MAX, dtype=i32)[:, None]
    tmine = (t_beg[None, :] <= q) & (q < t_end[None, :])
    tail_dst = jnp.sum(jnp.where(tmine, (pad_start + reg - t_beg)[None, :], 0), axis=-1) + q[:, 0]
    first = jnp.arange(n_blocks, dtype=i32)[:, None] * per_blk
    block_e = jnp.minimum(jnp.sum((pad_end[None, :] <= first).astype(i32), axis=-1), N_EXPERTS - 1)
    n_used = (pad_end[-1] // per_blk).astype(i32).reshape(1)
    return (npieces, piece_dst.astype(i32).reshape(-1, 1, PIECES_MAX),
            tail_dst.astype(i32).reshape(1, 1, TAIL_MAX), t_end[-1].astype(i32).reshape(1),
            block_e.astype(i32), n_used)


def _axial_tables(dim):
    t = jnp.arange(SEQ)
    row = (t // GRID_W).astype(F32)
    col = (t % GRID_W).astype(F32)
    quarter = dim // 4
    freqs = ROPE_THETA ** (-jnp.arange(quarter, dtype=F32) / quarter)
    ar, ac = row[:, None] * freqs, col[:, None] * freqs
    ang = jnp.concatenate([ar, ar, ac, ac], axis=-1)
    sign = jnp.where((jnp.arange(dim) // quarter) % 2 == 0, -1.0, 1.0).astype(F32)
    cos = jnp.concatenate([jnp.cos(ang), jnp.ones((CTX_LEN, dim), F32)], axis=0)
    sin = jnp.concatenate([jnp.sin(ang) * sign, jnp.zeros((CTX_LEN, dim), F32)], axis=0)
    return cos, sin


def _rope_tables():
    cos_h, sin_h = _axial_tables(HEAD_DIM)
    cos_r, sin_r = _axial_tables(B_ROPE)
    cosh, sinh = jnp.tile(cos_h, (1, 4)), jnp.tile(sin_h, (1, 4))
    scale_b = (B_NOPE + B_ROPE) ** -0.5
    ones, zeros = jnp.ones((TOK, 64), F32), jnp.zeros((TOK, 64), F32)
    pad1, pad0 = jnp.ones((TOK, 32), F32), jnp.zeros((TOK, 32), F32)
    cosb = jnp.tile(jnp.concatenate([ones, cos_r, pad1], axis=1) * scale_b, (1, 4))
    sinb = jnp.tile(jnp.concatenate([zeros, sin_r, pad0], axis=1) * scale_b, (1, 4))
    cosr = jnp.concatenate([cos_r, jnp.ones((TOK, 96), F32)], axis=1)
    sinr = jnp.concatenate([sin_r, jnp.zeros((TOK, 96), F32)], axis=1)
    return cosh, sinh, cosb, sinb, cosr, sinr


_GQA_ORDER = (0, 2, 1, 3)


def _head_cols(base, order, width=HEAD_DIM):
    return np.concatenate([base + np.arange(h * width, (h + 1) * width) for h in order])


def _w_in_layout(w_in):
    zeros = lambda n: jnp.zeros((D_MODEL, n), w_in.dtype)
    a_q = w_in[:, _head_cols(0, _GQA_ORDER)]
    a_kv = w_in[:, 256:512]
    b_q = jnp.concatenate(
        [jnp.concatenate([w_in[:, 512 + h * 96:512 + (h + 1) * 96], zeros(32)], axis=1)
         for h in range(B_HEADS)], axis=1)
    b_c = w_in[:, 896:1024]
    b_r = jnp.concatenate([w_in[:, 1024:1056], zeros(96)], axis=1)
    c_q = w_in[:, _head_cols(1056, _GQA_ORDER)]
    rest = w_in[:, 1312:2336]
    return jnp.concatenate([a_q, a_kv, b_q, b_c, b_r, c_q, rest], axis=1).astype(BF16)


def _w_kv_layout(w_kv_up):
    zeros = jnp.zeros((B_KV_RANK, 64), w_kv_up.dtype)
    nope = [jnp.concatenate([w_kv_up[:, h * 128:h * 128 + 64], zeros], axis=1) for h in range(B_HEADS)]
    val = [w_kv_up[:, h * 128 + 64:(h + 1) * 128] for h in range(B_HEADS)]
    return jnp.concatenate(nope + val, axis=1).astype(BF16)


def _w_out_layout(w_out):
    rows = np.concatenate([_head_cols(0, _GQA_ORDER), np.arange(256, 512),
                           _head_cols(512, _GQA_ORDER), np.arange(768, 1024)])
    return w_out[rows].astype(BF16)


_ROUTER_ROWS = np.array([g * 8 + j for j in range(8) for g in range(8)])


def kernel(x, c, ctx, c_ctx, ada_w, ada_b, pre_mix_g, post_mix_g, pre_ffn_g, post_ffn_g, w_in, a_q_norm, a_k_norm, b_kv_norm, b_w_kv_up, c_sink, d_rpb, w_out, router_w, router_b, exp_w_gate, exp_w_up, exp_w_down, sh_w_gate, sh_w_up, sh_w_down):
    bsz = x.shape[0]
    depth = ada_w.shape[0]
    xall = jnp.concatenate([x, ctx], axis=1)

    cvec = jnp.zeros((16, D_MODEL), F32).at[:bsz].set(c).at[8].set(c_ctx)
    mod = _ada(cvec, ada_w, ada_b)
    tabs = _rope_tables()
    avg = jnp.kron(jnp.eye(4, dtype=F32), jnp.full((64, 64), 1.0 / 64, F32)).astype(BF16)
    row2 = lambda v: v.reshape(1, -1).astype(F32)

    for l in range(depth):
        with_ctx = l < depth - 1
        mod3 = mod[l].reshape(16, 1, 6 * D_MODEL)

        proj = _inproj(
            xall, mod3, row2(pre_mix_g[l]), _w_in_layout(w_in[l]),
            row2(jnp.tile(a_q_norm[l], 4) * HEAD_DIM ** -0.5), row2(jnp.tile(a_k_norm[l], 2)),
            row2(b_kv_norm[l]), _w_kv_layout(b_w_kv_up[l]), avg, tabs)
        qa, ka, va, qb, kb, vb, qc, kc, vc, qd, kd, vd = proj

        nq = NQ + (CTX_LEN // QBLK if with_ctx else 0)
        full = lambda shape: pl.BlockSpec(shape, lambda b, i: (0,) * len(shape))
        mix_a = _attn_call(_attn_a_kernel, "attn_a", qa, ka, va, [], [], nq)
        mix_b = _attn_call(_attn_b_kernel, "attn_b", qb, kb, vb, [], [], nq)
        sink = jnp.zeros((8, LANES), F32).at[:4].set(
            jnp.broadcast_to(c_sink[l][np.array(_GQA_ORDER)][:, None], (4, LANES)))
        mix_c = _attn_call(_attn_c_kernel, "attn_c", qc, kc, vc, [sink], [full((8, LANES))], nq)
        bias = _na_bias(d_rpb[l])
        mix_d = _attn_call(
            _attn_d_kernel, "attn_d", qd, kd, vd, [bias],
            [pl.BlockSpec((None, D_HEADS, QBLK, NA_WIN), lambda b, i: (_na_pattern(i), 0, 0, 0))], nq)

        nb = (TOK if with_ctx else SEQ) // ROW_BLK
        rb = jnp.broadcast_to(router_b[l][_ROUTER_ROWS][:, None], (N_EXPERTS, ROW_BLK)).astype(F32)
        xn, h2b, h2t, wt_t, ls_t, cnt_t = _outproj(
            (mix_a, mix_b, mix_c, mix_d), _w_out_layout(w_out[l]), xall, mod3,
            row2(post_mix_g[l]), row2(pre_ffn_g[l]),
            router_w[l].T[_ROUTER_ROWS].astype(BF16), rb, nb)

        t_used = nb * ROW_BLK
        n_tok = bsz * t_used
        n_steps = bsz * nb
        n_blocks = -(-(n_tok * TOP_K + n_steps * N_EXPERTS * (PIECE - 1) + N_EXPERTS * (MOE_BLK - 1))
                     // MOE_BLK)
        npieces, piece_dst, tail_dst, ntail, block_e, n_used = _moe_plan(
            cnt_t[..., 0].reshape(n_steps, N_EXPERTS), n_blocks)
        lslot = ls_t.reshape(n_steps, 1, TOP_K * ROW_BLK)
        wts = wt_t.reshape(n_steps, 1, TOP_K * ROW_BLK)

        x_sorted = _dispatch(npieces, n_used, ntail, lslot, piece_dst, tail_dst,
                             h2t.reshape(n_tok * TILE_ROWS, LANES), n_steps, n_blocks)
        y_sorted = _experts(block_e, n_used, x_sorted, exp_w_gate, exp_w_up, exp_w_down, l, n_blocks)
        xall = _combine(
            npieces, lslot, wts, piece_dst, y_sorted, h2b.reshape(n_tok, D_MODEL),
            xn.reshape(n_tok, D_MODEL), mod3, row2(post_ffn_g[l]),
            sh_w_gate[l].astype(BF16), sh_w_up[l].astype(BF16), sh_w_down[l].astype(BF16),
            n_steps, nb).reshape(bsz, t_used, D_MODEL)

    return xall[:, :SEQ]
```

```python
import functools

import numpy as np
import jax
import jax.numpy as jnp
from jax import lax
from jax.experimental import pallas as pl
from jax.experimental.pallas import tpu as pltpu

F32 = jnp.float32
BF16 = jnp.bfloat16

D_MODEL = 1024
SEQ = 2048
CTX_LEN = 256
TOK = SEQ + CTX_LEN
GRID_W = 64
HEAD_DIM = 64
ROPE_THETA = 10000.0
EPS = 1e-6
NEG = -1e30

A_HEADS, A_KV_HEADS = 4, 2
B_HEADS, B_NOPE, B_ROPE, B_V, B_KV_RANK = 4, 64, 32, 64, 128
C_HEADS, C_KV_HEADS, C_WINDOW = 4, 2, 128
D_HEADS, NA_ROWS, NA_COLS = 4, 8, 16

N_EXPERTS, TOP_K, N_GROUPS, TOPK_GROUPS = 64, 8, 8, 4
EXPERT_DIM, SHARED_DIM = 256, 256
ROUTED_SCALE = 2.5

LANES = 128
QBLK = 128
ROW_BLK = 256
MOE_BLK = 256
TILE_ROWS = D_MODEL // LANES
PIECE = 8
PIECE_ROWS = PIECE * TILE_ROWS
PIECES_MAX = 320
STG_ROWS = PIECES_MAX * PIECE_ROWS
TAIL_MAX = 2048
VMEM_LIMIT = 48 * 1024 * 1024
NQ = SEQ // QBLK
NA_WIN = 10 * GRID_W

_AQ, _AK, _AV, _BQ, _BC, _BR = 0, 256, 384, 512, 1024, 1152
_CQ, _CK, _CV, _DQ, _DK, _DV = 1280, 1536, 1664, 1792, 2048, 2304
IN_W = 2560

_NT = (((1,), (1,)), ((), ()))


def _sigmoid(x):
    return 1.0 / (1.0 + jnp.exp(-x))


def _rms(x, g):
    return x * lax.rsqrt(jnp.mean(x * x, axis=-1, keepdims=True) + EPS) * g


def _dot(a, b):
    return jnp.dot(a, b, preferred_element_type=F32)


def _dot_nt(a, b):
    return lax.dot_general(a, b, _NT, preferred_element_type=F32)


def _ada_kernel(c_ref, w_ref, b_ref, o_ref):
    c = c_ref[...]
    a = (c * _sigmoid(c)).astype(BF16)
    o_ref[...] = _dot(a, w_ref[...].astype(BF16)) + b_ref[...]


def _ada(cvec, ada_w, ada_b):
    depth = ada_w.shape[0]
    n = ada_w.shape[2]
    tn = 1536
    return pl.pallas_call(
        _ada_kernel,
        grid=(depth, n // tn),
        in_specs=[
            pl.BlockSpec((16, D_MODEL), lambda l, j: (0, 0)),
            pl.BlockSpec((None, D_MODEL, tn), lambda l, j: (l, 0, j)),
            pl.BlockSpec((None, 1, tn), lambda l, j: (l, 0, j)),
        ],
        out_specs=pl.BlockSpec((None, 16, tn), lambda l, j: (l, 0, j)),
        out_shape=jax.ShapeDtypeStruct((depth, 16, n), F32),
        name="ada",
    )(cvec, ada_w, ada_b.reshape(depth, 1, n))


def _rope(x, cos, sin, chunk):
    outs = []
    for c in range(x.shape[1] // LANES):
        sl = slice(c * LANES, (c + 1) * LANES)
        xs = x[:, sl]
        lane = lax.broadcasted_iota(jnp.int32, xs.shape, 1)
        first = (lane & chunk) == 0
        rot = jnp.where(first, pltpu.roll(xs, LANES - chunk, 1), pltpu.roll(xs, chunk, 1))
        outs.append(xs * cos[:, sl] + rot * sin[:, sl])
    return outs[0] if len(outs) == 1 else jnp.concatenate(outs, axis=1)


def _head_norm(x, avg):
    sq = x * x
    hi = sq.astype(BF16)
    lo = (sq - hi.astype(F32)).astype(BF16)
    ms = _dot(hi, avg) + _dot(lo, avg)
    return x * lax.rsqrt(ms + EPS)


def _inproj_kernel(x_ref, mod_ref, g_ref, w_ref, qg_ref, kg_ref, kvg_ref, wkv_ref, avg_ref,
                   cosh_ref, sinh_ref, cosb_ref, sinb_ref, cosr_ref, sinr_ref,
                   qa_ref, ka_ref, va_ref, qb_ref, kb_ref, vb_ref,
                   qc_ref, kc_ref, vc_ref, qd_ref, kd_ref, vd_ref):
    d = D_MODEL
    h = _rms(x_ref[...], g_ref[...]) * (1.0 + mod_ref[:, d:2 * d]) + mod_ref[:, 0:d]
    p = _dot(h.astype(BF16), w_ref[...])
    cosh, sinh = cosh_ref[...], sinh_ref[...]
    avg = avg_ref[...]

    qa = _head_norm(p[:, _AQ:_AQ + 256], avg) * qg_ref[...]
    qa_ref[...] = _rope(qa, cosh, sinh, 16).astype(BF16)
    ka = _head_norm(p[:, _AK:_AK + 128], avg[0:128, 0:128]) * kg_ref[...]
    ka_ref[...] = _rope(ka, cosh[:, 0:128], sinh[:, 0:128], 16).astype(BF16)
    ones = jnp.ones((p.shape[0], LANES), BF16)
    va_ref[...] = jnp.concatenate([p[:, _AV:_AV + 128].astype(BF16), ones], axis=1)

    qb_ref[...] = _rope(p[:, _BQ:_BQ + 512], cosb_ref[...], sinb_ref[...], 8).astype(BF16)
    cn = _rms(p[:, _BC:_BC + 128], kvg_ref[...])
    kv = _dot(cn.astype(BF16), wkv_ref[...])
    kr = _rope(p[:, _BR:_BR + 128], cosr_ref[...], sinr_ref[...], 8)
    kr = pltpu.roll(kr, 64, 1)
    kb_ref[...] = jnp.concatenate(
        [kv[:, hh * 128:(hh + 1) * 128] + kr for hh in range(B_HEADS)], axis=1).astype(BF16)
    vb_ref[...] = jnp.concatenate(
        [kv[:, 512:640].astype(BF16), ones, kv[:, 640:768].astype(BF16), ones], axis=1)

    qc_ref[...] = _rope(p[:, _CQ:_CQ + 256] * 0.125, cosh, sinh, 16).astype(BF16)
    kc_ref[...] = _rope(p[:, _CK:_CK + 128], cosh[:, 0:128], sinh[:, 0:128], 16).astype(BF16)
    vc_ref[...] = jnp.concatenate([p[:, _CV:_CV + 128].astype(BF16), ones], axis=1)

    qd_ref[...] = (p[:, _DQ:_DQ + 256] * 0.125).astype(BF16)
    kd_ref[...] = p[:, _DK:_DK + 256].astype(BF16)
    vd_ref[...] = jnp.concatenate(
        [p[:, _DV:_DV + 128].astype(BF16), ones, p[:, _DV + 128:_DV + 256].astype(BF16), ones], axis=1)


def _mod_row(j, b, n_lat_blocks):
    return jnp.where(j < n_lat_blocks, b, 8)


def _inproj(xall, mod3, g, w, qg, kg, kvg, wkv, avg, tabs):
    bsz = xall.shape[0]
    nb = TOK // ROW_BLK
    n_lat = SEQ // ROW_BLK
    full = lambda shape: pl.BlockSpec(shape, lambda b, j: (0,) * len(shape))
    row = lambda w_: pl.BlockSpec((ROW_BLK, w_), lambda b, j: (j, 0))
    out = lambda w_: pl.BlockSpec((None, ROW_BLK, w_), lambda b, j: (b, j, 0))
    widths = (256, 128, 256, 512, 512, 512, 256, 128, 256, 256, 256, 512)
    return pl.pallas_call(
        _inproj_kernel,
        grid=(bsz, nb),
        in_specs=[
            pl.BlockSpec((None, ROW_BLK, D_MODEL), lambda b, j: (b, j, 0)),
            pl.BlockSpec((None, 1, 6 * D_MODEL), lambda b, j: (_mod_row(j, b, n_lat), 0, 0)),
            full((1, D_MODEL)), full((D_MODEL, IN_W)),
            full((1, 256)), full((1, 128)), full((1, 128)), full((128, 768)), full((256, 256)),
            row(256), row(256), row(512), row(512), row(128), row(128),
        ],
        out_specs=[out(w_) for w_ in widths],
        out_shape=[jax.ShapeDtypeStruct((bsz, TOK, w_), BF16) for w_ in widths],
        name="inproj",
    )(xall, mod3, g, w, qg, kg, kvg, wkv, avg, *tabs)


def _softmax_pv(segs, sink=None):
    m = None
    for s, _ in segs:
        ms = jnp.max(s, axis=-1, keepdims=True)
        m = ms if m is None else jnp.maximum(m, ms)
    if sink is not None:
        m = jnp.maximum(m, sink)
    acc = None
    for s, v1 in segs:
        r = _dot(jnp.exp((s - m).astype(BF16)), v1)
        acc = r if acc is None else acc + r
    l = acc[:, LANES:LANES + 1]
    if sink is not None:
        l = l + jnp.exp(sink - m)
    return acc[:, 0:LANES] / l


def _lane_lo(dtype_shape):
    return lax.broadcasted_iota(jnp.int32, dtype_shape, 1) < HEAD_DIM


def _half(qt, hh):
    lo = _lane_lo(qt.shape)
    return jnp.where(lo if hh == 0 else jnp.logical_not(lo), qt, jnp.zeros_like(qt))


def _merge_halves(o0, o1):
    return jnp.where(_lane_lo(o0.shape), o0, o1)


def _stack_heads(q_ref, tiles):
    return jnp.concatenate(
        [_half(q_ref[:, t * LANES:(t + 1) * LANES], hh) for t in tiles for hh in range(2)], axis=0)


def _store_heads(o_ref, o, tiles):
    n = o_ref.shape[0]
    for ti, t in enumerate(tiles):
        o0, o1 = o[2 * ti * n:(2 * ti + 1) * n], o[(2 * ti + 1) * n:(2 * ti + 2) * n]
        o_ref[:, t * LANES:(t + 1) * LANES] = _merge_halves(o0, o1).astype(o_ref.dtype)


def _attn_a_kernel(q_ref, k_ref, v_ref, o_ref, *, n_lat):
    def run(k, v1):
        for t in range(2):
            o = _softmax_pv([(_dot_nt(_stack_heads(q_ref, (t,)), k), v1)])
            _store_heads(o_ref, o, (t,))

    @pl.when(pl.program_id(1) < n_lat)
    def _():
        run(k_ref[...], v_ref[...])

    @pl.when(pl.program_id(1) >= n_lat)
    def _():
        run(k_ref[SEQ:TOK, :], v_ref[SEQ:TOK, :])


def _attn_b_kernel(q_ref, k_ref, v_ref, o_ref, *, n_lat):
    def run(lo_, hi_):
        for t in range(2):
            v1 = v_ref[lo_:hi_, t * 256:(t + 1) * 256]
            o = []
            for hh in range(2):
                hd = 2 * t + hh
                q = q_ref[:, hd * LANES:(hd + 1) * LANES]
                k = k_ref[lo_:hi_, hd * LANES:(hd + 1) * LANES]
                o.append(_softmax_pv([(_dot_nt(q, k), v1)]))
            o_ref[:, t * LANES:(t + 1) * LANES] = _merge_halves(o[0], o[1]).astype(o_ref.dtype)

    @pl.when(pl.program_id(1) < n_lat)
    def _():
        run(0, TOK)

    @pl.when(pl.program_id(1) >= n_lat)
    def _():
        run(SEQ, TOK)


def _attn_c_kernel(q_ref, k_ref, v_ref, sink_ref, o_ref, *, n_lat):
    i = pl.program_id(1)
    qblk = o_ref.shape[0]
    band = 3 * qblk

    def run(latent):
        if latent:
            start = pl.multiple_of(jnp.clip((i - 1) * qblk, 0, SEQ - band), qblk)
            shape = (2 * qblk, band)
            qpos = i * qblk + (lax.broadcasted_iota(jnp.int32, shape, 0) & (qblk - 1))
            kpos = start + lax.broadcasted_iota(jnp.int32, shape, 1)
            valid = jnp.abs(qpos - kpos) <= C_WINDOW
        for t in range(2):
            q = _stack_heads(q_ref, (t,))
            segs = [(_dot_nt(q, k_ref[SEQ:TOK, :]), v_ref[SEQ:TOK, :])]
            if latent:
                s = jnp.where(valid, _dot_nt(q, k_ref[pl.ds(start, band), :]), NEG)
                segs.insert(0, (s, v_ref[pl.ds(start, band), :]))
            sink = jnp.concatenate(
                [jnp.broadcast_to(sink_ref[j:j + 1, 0:1], (qblk, 1)) for j in (2 * t, 2 * t + 1)], axis=0)
            _store_heads(o_ref, _softmax_pv(segs, sink=sink), (t,))

    @pl.when(i < n_lat)
    def _():
        run(True)

    @pl.when(i >= n_lat)
    def _():
        run(False)


def _attn_d_kernel(q_ref, k_ref, v_ref, bias_ref, o_ref, *, n_lat):
    i = pl.program_id(1)
    qblk = o_ref.shape[0]

    def run(latent):
        if latent:
            start = pl.multiple_of(jnp.clip((i - 2) * qblk, 0, SEQ - NA_WIN), qblk)
        for t in range(2):
            ksl, vsl = slice(t * LANES, (t + 1) * LANES), slice(t * 256, (t + 1) * 256)
            q = _stack_heads(q_ref, (t,))
            segs = [(_dot_nt(q, k_ref[SEQ:TOK, ksl]), v_ref[SEQ:TOK, vsl])]
            if latent:
                bias = bias_ref[2 * t:2 * t + 2].reshape(2 * qblk, NA_WIN)
                segs.insert(0, (_dot_nt(q, k_ref[pl.ds(start, NA_WIN), ksl]) + bias,
                                v_ref[pl.ds(start, NA_WIN), vsl]))
            _store_heads(o_ref, _softmax_pv(segs), (t,))

    @pl.when(i < n_lat)
    def _():
        run(True)

    @pl.when(i >= n_lat)
    def _():
        run(False)


def _attn_call(kernel, name, q, k, v, extra, extra_specs, n_blocks, qblk=QBLK):
    bsz = q.shape[0]
    return pl.pallas_call(
        functools.partial(kernel, n_lat=SEQ // qblk),
        grid=(bsz, n_blocks),
        in_specs=[
            pl.BlockSpec((None, qblk, q.shape[2]), lambda b, i: (b, i, 0)),
            pl.BlockSpec((None, TOK, k.shape[2]), lambda b, i: (b, 0, 0)),
            pl.BlockSpec((None, TOK, v.shape[2]), lambda b, i: (b, 0, 0)),
        ] + extra_specs,
        out_specs=pl.BlockSpec((None, qblk, 256), lambda b, i: (b, i, 0)),
        out_shape=jax.ShapeDtypeStruct((bsz, n_blocks * qblk, 256), BF16),
        name=name,
    )(q, k, v, *extra)


def _na_pattern(i):
    return jnp.where(i < 2, i, jnp.where(i < NQ - 2, 2, jnp.minimum(i, NQ - 1) - (NQ - 5)))


def _na_bias(rpb):
    rows = SEQ // GRID_W
    n_dr, n_dc = 2 * NA_ROWS - 1, 2 * NA_COLS - 1
    qc = np.arange(GRID_W)
    dc = np.clip(qc[None, :] - qc[:, None] + NA_COLS - 1, 0, n_dc - 1)
    sel_c = jnp.asarray(np.eye(n_dc, dtype=np.float32)[dc])
    cs = np.clip(qc - NA_COLS // 2, 0, GRID_W - NA_COLS)[:, None]
    col_ok = (qc[None, :] >= cs) & (qc[None, :] < cs + NA_COLS)
    out = []
    for i in (0, 1, 2, NQ - 2, NQ - 1):
        start_row = min(max(2 * i - 4, 0), rows - 10)
        qr = 2 * i + np.arange(2)
        kr = start_row + np.arange(10)
        rs = np.clip(qr - NA_ROWS // 2, 0, rows - NA_ROWS)[:, None]
        row_ok = (kr[None] >= rs) & (kr[None] < rs + NA_ROWS)
        dr = np.clip(kr[None] - qr[:, None] + NA_ROWS - 1, 0, n_dr - 1)
        sel_r = jnp.asarray(np.eye(n_dr, dtype=np.float32)[dr])
        b = jnp.einsum("hrc,abr,qkc->haqbk", rpb.astype(F32), sel_r, sel_c,
                       precision=lax.Precision.HIGHEST)
        valid = row_ok[:, None, :, None] & col_ok[None, :, None, :]
        out.append(jnp.where(valid[None], b, NEG).reshape(D_HEADS, QBLK, NA_WIN))
    return jnp.stack(out)


def _outproj_kernel(ma_ref, mb_ref, mc_ref, md_ref, w_ref, x_ref, mod_ref, g1_ref, g2_ref,
                    wr_ref, rb_ref, xn_ref, h2b_ref, h2t_ref, wt_ref, ls_ref, cnt_ref):
    d = D_MODEL
    mix = jnp.concatenate([ma_ref[...], mb_ref[...], mc_ref[...], md_ref[...]], axis=1)
    y = _rms(_dot(mix, w_ref[...]), g1_ref[...])
    xn = x_ref[...] + mod_ref[:, 2 * d:3 * d] * y
    xn_ref[...] = xn
    h2 = _rms(xn, g2_ref[...]) * (1.0 + mod_ref[:, 4 * d:5 * d]) + mod_ref[:, 3 * d:4 * d]
    n = h2.shape[0]
    h2b_ref[...] = h2.astype(BF16)
    for s in range(TILE_ROWS):
        h2t_ref[pl.ds(s, n, stride=TILE_ROWS), :] = h2[:, s * LANES:(s + 1) * LANES]

    s = _sigmoid(_dot_nt(wr_ref[...], h2.astype(BF16)))
    sb = s + rb_ref[...]
    per = N_EXPERTS // N_GROUPS
    slab = [sb[j * 8:(j + 1) * 8, :] for j in range(per)]
    sraw = [s[j * 8:(j + 1) * 8, :] for j in range(per)]
    giota = lax.broadcasted_iota(jnp.int32, (N_GROUPS, n), 0)
    eid = [giota * per + j for j in range(per)]
    ninf = -jnp.inf

    m1 = functools.reduce(jnp.maximum, slab)
    j1 = functools.reduce(jnp.minimum, [jnp.where(slab[j] == m1, j, per) for j in range(per)])
    m2 = functools.reduce(jnp.maximum, [jnp.where(j1 == j, ninf, slab[j]) for j in range(per)])
    gs = m1 + m2

    gsel = jnp.zeros((N_GROUPS, n), jnp.bool_)
    for _ in range(TOPK_GROUPS):
        m = jnp.max(gs, axis=0, keepdims=True)
        gi = jnp.min(jnp.where(gs == m, giota, N_GROUPS), axis=0, keepdims=True)
        hit = giota == gi
        gsel = jnp.logical_or(gsel, hit)
        gs = jnp.where(hit, ninf, gs)
    slab = [jnp.where(gsel, sj, ninf) for sj in slab]

    def pick(hits, vals):
        return jnp.sum(functools.reduce(jnp.add, [jnp.where(hits[j], vals[j], 0.0) for j in range(per)]),
                       axis=0, keepdims=True)

    kiota = lax.broadcasted_iota(jnp.int32, (TOP_K, n), 0)
    wts = jnp.zeros((TOP_K, n), F32)
    chosen = []
    for k in range(TOP_K):
        m = jnp.max(functools.reduce(jnp.maximum, slab), axis=0, keepdims=True)
        cand = functools.reduce(jnp.minimum,
                                [jnp.where(slab[j] == m, eid[j], N_EXPERTS) for j in range(per)])
        ei = jnp.min(cand, axis=0, keepdims=True)
        hits = [eid[j] == ei for j in range(per)]
        slab = [jnp.where(hits[j], ninf, slab[j]) for j in range(per)]
        wts = jnp.where(kiota == k, pick(hits, sraw), wts)
        chosen.append(ei)
    wt_ref[...] = wts / jnp.sum(wts, axis=0, keepdims=True) * ROUTED_SCALE

    hits_k = [[eid[j] == chosen[k] for j in range(per)] for k in range(TOP_K)]
    sel = jnp.concatenate(
        [functools.reduce(jnp.add, [jnp.where(hits_k[k][j], 1.0, 0.0) for k in range(TOP_K)])
         for j in range(per)], axis=0)
    tri = jnp.where(lax.broadcasted_iota(jnp.int32, (n, n), 0) < lax.broadcasted_iota(jnp.int32, (n, n), 1),
                    1.0, 0.0).astype(BF16)
    before = _dot(sel.astype(BF16), tri)
    cnt = jnp.sum(sel, axis=1, keepdims=True)
    run = jnp.floor((cnt + (PIECE - 1.0)) * (1.0 / PIECE)) * PIECE
    low = jnp.where(lax.broadcasted_iota(jnp.int32, (N_EXPERTS, N_EXPERTS), 1)
                    < lax.broadcasted_iota(jnp.int32, (N_EXPERTS, N_EXPERTS), 0), 1.0, 0.0).astype(BF16)
    start = _dot(low, jnp.broadcast_to(run, (N_EXPERTS, LANES)).astype(BF16))[:, 0:1]
    base = before + start
    base = [base[j * 8:(j + 1) * 8, :] for j in range(per)]
    lsl = jnp.zeros((TOP_K, n), F32)
    for k in range(TOP_K):
        lsl = jnp.where(kiota == k, pick(hits_k[k], base), lsl)
    ls_ref[...] = lsl.astype(jnp.int32)
    cnt_ref[...] = jnp.broadcast_to(cnt, (N_EXPERTS, LANES)).astype(jnp.int32)


def _outproj(mixes, w, xall, mod3, g1, g2, wr, rb, nb):
    bsz = xall.shape[0]
    n_lat = SEQ // ROW_BLK
    full = lambda shape: pl.BlockSpec(shape, lambda b, j: (0,) * len(shape))
    rows = lambda w_: pl.BlockSpec((None, ROW_BLK, w_), lambda b, j: (b, j, 0))
    route = pl.BlockSpec((None, None, TOP_K, ROW_BLK), lambda b, j: (b, j, 0, 0))
    return pl.pallas_call(
        _outproj_kernel,
        grid=(bsz, nb),
        in_specs=[rows(256)] * 4 + [
            full((D_MODEL, D_MODEL)), rows(D_MODEL),
            pl.BlockSpec((None, 1, 6 * D_MODEL), lambda b, j: (_mod_row(j, b, n_lat), 0, 0)),
            full((1, D_MODEL)), full((1, D_MODEL)),
            full((N_EXPERTS, D_MODEL)), full((N_EXPERTS, ROW_BLK)),
        ],
        out_specs=[
            rows(D_MODEL), rows(D_MODEL),
            pl.BlockSpec((None, ROW_BLK * TILE_ROWS, LANES), lambda b, j: (b, j, 0)),
            route, route,
            pl.BlockSpec((None, None, N_EXPERTS, LANES), lambda b, j: (b, j, 0, 0)),
        ],
        out_shape=[
            jax.ShapeDtypeStruct((bsz, nb * ROW_BLK, D_MODEL), F32),
            jax.ShapeDtypeStruct((bsz, nb * ROW_BLK, D_MODEL), BF16),
            jax.ShapeDtypeStruct((bsz, nb * ROW_BLK * TILE_ROWS, LANES), F32),
            jax.ShapeDtypeStruct((bsz, nb, TOP_K, ROW_BLK), F32),
            jax.ShapeDtypeStruct((bsz, nb, TOP_K, ROW_BLK), jnp.int32),
            jax.ShapeDtypeStruct((bsz, nb, N_EXPERTS, LANES), jnp.int32),
        ],
        name="outproj",
    )(*mixes, w, xall, mod3, g1, g2, wr, rb)


def _tile(t):
    return pl.ds(pl.multiple_of(t * TILE_ROWS, TILE_ROWS), TILE_ROWS)


def _piece(p):
    return pl.ds(pl.multiple_of(p * PIECE_ROWS, PIECE_ROWS), PIECE_ROWS)


def _start_pieces(n, start):
    def pair(q, carry):
        start(2 * q, 0)
        start(2 * q + 1, 1)
        return carry
    lax.fori_loop(0, lax.shift_right_logical(n, 1), pair, 0)

    @pl.when((n & 1) == 1)
    def _():
        start(n - 1, 0)


def _dispatch_kernel(np_ref, nu_ref, nt_ref, ls_ref, pd_ref, td_ref, h_ref, xs_hbm,
                     stg, zblk, sem, zsem, *, n_steps, n_blocks):
    i = pl.program_id(0)
    slot = i % 2

    def piece_copy(s, p, d):
        return pltpu.make_async_copy(stg.at[s, _piece(p), :], xs_hbm.at[_piece(d), :], sem.at[s])

    def wait_pieces(s, n):
        def body(p, carry):
            piece_copy(s, 0, 0).wait()
            return carry
        lax.fori_loop(0, n, body, 0)

    def zero_piece_copy(d):
        return pltpu.make_async_copy(zblk.at[pl.ds(0, PIECE_ROWS), :], xs_hbm.at[_piece(d), :], zsem.at[0])

    def zero_block_copy(b):
        rows = pl.ds(pl.multiple_of(b * (MOE_BLK * TILE_ROWS), MOE_BLK * TILE_ROWS), MOE_BLK * TILE_ROWS)
        return pltpu.make_async_copy(zblk, xs_hbm.at[rows, :], zsem.at[1])

    @pl.when(i == 0)
    def _():
        zblk[...] = jnp.zeros_like(zblk)
        for s in range(2):
            def zero(c, carry, s=s):
                stg[s, pl.ds(pl.multiple_of(c * 2048, 2048), 2048), :] = jnp.zeros((2048, LANES), F32)
                return carry
            lax.fori_loop(0, STG_ROWS // 2048, zero, 0)

    @pl.when(i >= 2)
    def _():
        wait_pieces(slot, np_ref[jnp.maximum(i - 2, 0)])

    def scatter_token(t, carry):
        tile = h_ref[_tile(t), :]
        for k in range(TOP_K):
            stg[slot, _tile(ls_ref[0, k * ROW_BLK + t]), :] = tile
        return carry
    lax.fori_loop(0, ROW_BLK, scatter_token, 0, unroll=2)

    _start_pieces(np_ref[i], lambda p, prio: piece_copy(slot, p, pd_ref[0, p]).start(priority=prio))

    @pl.when(i == n_steps - 1)
    def _():
        def tail(p, carry):
            zero_piece_copy(td_ref[0, p]).start()
            return carry
        lax.fori_loop(0, nt_ref[0], tail, 0)

        def blk(b, carry):
            zero_block_copy(b).start()
            return carry
        lax.fori_loop(nu_ref[0], n_blocks, blk, 0)

        if n_steps > 1:
            wait_pieces(1 - slot, np_ref[jnp.maximum(i - 1, 0)])
        wait_pieces(slot, np_ref[i])

        def tail_wait(p, carry):
            zero_piece_copy(0).wait()
            return carry
        lax.fori_loop(0, nt_ref[0], tail_wait, 0)

        def blk_wait(b, carry):
            zero_block_copy(0).wait()
            return carry
        lax.fori_loop(nu_ref[0], n_blocks, blk_wait, 0)


def _dispatch(npieces, n_used, ntail, lslot, piece_dst, tail_dst, h2t, n_steps, n_blocks):
    smem = lambda n, imap: pl.BlockSpec((None, 1, n), imap, memory_space=pltpu.SMEM)
    grid_spec = pltpu.PrefetchScalarGridSpec(
        num_scalar_prefetch=3,
        grid=(n_steps,),
        in_specs=[
            smem(TOP_K * ROW_BLK, lambda i, *_: (i, 0, 0)),
            smem(PIECES_MAX, lambda i, *_: (i, 0, 0)),
            smem(TAIL_MAX, lambda i, *_: (0, 0, 0)),
            pl.BlockSpec((ROW_BLK * TILE_ROWS, LANES), lambda i, *_: (i, 0)),
        ],
        out_specs=pl.BlockSpec(memory_space=pl.ANY),
        scratch_shapes=[
            pltpu.VMEM((2, STG_ROWS, LANES), F32),
            pltpu.VMEM((MOE_BLK * TILE_ROWS, LANES), F32),
            pltpu.SemaphoreType.DMA((2,)),
            pltpu.SemaphoreType.DMA((2,)),
        ],
    )
    return pl.pallas_call(
        functools.partial(_dispatch_kernel, n_steps=n_steps, n_blocks=n_blocks),
        grid_spec=grid_spec,
        out_shape=jax.ShapeDtypeStruct((n_blocks * MOE_BLK * TILE_ROWS, LANES), F32),
        compiler_params=pltpu.CompilerParams(dimension_semantics=("arbitrary",),
                                             vmem_limit_bytes=VMEM_LIMIT),
        name="dispatch",
    )(npieces, n_used, ntail, lslot, piece_dst, tail_dst, h2t)


def _from_tiles(ref, n):
    return jnp.concatenate([ref[pl.ds(s, n, stride=TILE_ROWS), :] for s in range(TILE_ROWS)], axis=1)


def _experts_kernel(be_ref, nu_ref, xs_hbm, wg_ref, wu_ref, wd_ref, ys_hbm,
                    xbuf, ybuf, wgb, wub, wdb, xsem, ysem, *, n_blocks):
    i = pl.program_id(0)
    n_used = nu_ref[0]
    rows = MOE_BLK * TILE_ROWS
    half = rows // 2

    def halves(b):
        return [pl.ds(pl.multiple_of(b * rows + h * half, half), half) for h in range(2)]

    def x_copies(b, s):
        return [pltpu.make_async_copy(xs_hbm.at[src, :], xbuf.at[s, pl.ds(h * half, half), :], xsem.at[s])
                for h, src in enumerate(halves(b))]

    def y_copies(b, s):
        return [pltpu.make_async_copy(ybuf.at[s, pl.ds(h * half, half), :], ys_hbm.at[dst, :], ysem.at[s])
                for h, dst in enumerate(halves(b))]

    def start(copies):
        for h, c in enumerate(copies):
            c.start(priority=h)

    def wait(copies):
        for c in copies:
            c.wait()

    @pl.when(i == 0)
    def _():
        start(x_copies(0, 0))

        @pl.when(n_used > 1)
        def _():
            start(x_copies(1, 1))

    @pl.when(i + 2 < n_used)
    def _():
        start(x_copies(i + 2, (i + 2) % 3))

    @pl.when(i < n_used)
    def _():
        wait(x_copies(i, i % 3))

        @pl.when(i >= 2)
        def _():
            wait(y_copies(i - 2, i % 2))

        @pl.when(jnp.logical_or(i == 0, be_ref[i] != be_ref[jnp.maximum(i - 1, 0)]))
        def _():
            wgb[...] = wg_ref[...].astype(BF16)
            wub[...] = wu_ref[...].astype(BF16)
            wdb[...] = wd_ref[...].astype(BF16)

        x = _from_tiles(xbuf.at[i % 3], MOE_BLK).astype(BF16)
        g = _dot(x, wgb[...])
        u = _dot(x, wub[...])
        y = _dot((g * _sigmoid(g) * u).astype(BF16), wdb[...])
        yb = ybuf.at[i % 2]
        for s in range(TILE_ROWS):
            yb[pl.ds(s, MOE_BLK, stride=TILE_ROWS), :] = y[:, s * LANES:(s + 1) * LANES]
        start(y_copies(i, i % 2))

        @pl.when(i == n_used - 1)
        def _():
            @pl.when(i >= 1)
            def _():
                wait(y_copies(i - 1, (i + 1) % 2))
            wait(y_copies(i, i % 2))
            ybuf[0] = jnp.zeros((rows, LANES), F32)

            def fill(b, carry):
                start(y_copies(b, 0))
                return carry
            lax.fori_loop(n_used, n_blocks, fill, 0)

            def fill_wait(b, carry):
                wait(y_copies(0, 0))
                return carry
            lax.fori_loop(n_used, n_blocks, fill_wait, 0)


def _experts(block_e, n_used, x_sorted, wg, wu, wd, layer, n_blocks):
    wspec = lambda shape: pl.BlockSpec((None, None) + shape, lambda i, be, nu: (layer, be[i], 0, 0))
    rows = MOE_BLK * TILE_ROWS
    grid_spec = pltpu.PrefetchScalarGridSpec(
        num_scalar_prefetch=2,
        grid=(n_blocks,),
        in_specs=[
            pl.BlockSpec(memory_space=pl.ANY),
            wspec((D_MODEL, EXPERT_DIM)), wspec((D_MODEL, EXPERT_DIM)), wspec((EXPERT_DIM, D_MODEL)),
        ],
        out_specs=pl.BlockSpec(memory_space=pl.ANY),
        scratch_shapes=[
            pltpu.VMEM((3, rows, LANES), F32),
            pltpu.VMEM((2, rows, LANES), F32),
            pltpu.VMEM((D_MODEL, EXPERT_DIM), BF16),
            pltpu.VMEM((D_MODEL, EXPERT_DIM), BF16),
            pltpu.VMEM((EXPERT_DIM, D_MODEL), BF16),
            pltpu.SemaphoreType.DMA((3,)),
            pltpu.SemaphoreType.DMA((2,)),
        ],
    )
    return pl.pallas_call(
        functools.partial(_experts_kernel, n_blocks=n_blocks),
        grid_spec=grid_spec,
        out_shape=jax.ShapeDtypeStruct((n_blocks * rows, LANES), F32),
        compiler_params=pltpu.CompilerParams(dimension_semantics=("arbitrary",)),
        name="experts",
    )(block_e, n_used, x_sorted, wg, wu, wd)


def _combine_kernel(np_ref, ls_ref, wt_ref, cur_ref, nxt_ref, y_hbm, h2b_ref, xn_ref, mod_ref, g_ref,
                    swg_ref, swu_ref, swd_ref, o_ref, stg, acc, sem, *, n_steps):
    i = pl.program_id(0)
    slot = i % 2
    d = D_MODEL

    def piece_copy(s, p, src):
        return pltpu.make_async_copy(y_hbm.at[_piece(src), :], stg.at[s, _piece(p), :], sem.at[s])

    def fetch(s, n, pd_ref):
        _start_pieces(n, lambda p, prio: piece_copy(s, p, pd_ref[0, p]).start(priority=prio))

    @pl.when(i == 0)
    def _():
        fetch(0, np_ref[0], cur_ref)

    @pl.when(i + 1 < n_steps)
    def _():
        fetch(1 - slot, np_ref[jnp.minimum(i + 1, n_steps - 1)], nxt_ref)

    def wait(p, carry):
        piece_copy(slot, 0, 0).wait()
        return carry
    lax.fori_loop(0, np_ref[i], wait, 0)

    def gather_token(t, carry):
        a = None
        for k in range(TOP_K):
            v = stg[slot, _tile(ls_ref[0, k * ROW_BLK + t]), :] * wt_ref[0, k * ROW_BLK + t]
            a = v if a is None else a + v
        acc[_tile(t), :] = a
        return carry
    lax.fori_loop(0, ROW_BLK, gather_token, 0, unroll=2)

    hb = h2b_ref[...]
    g = _dot(hb, swg_ref[...])
    u = _dot(hb, swu_ref[...])
    f = _dot((g * _sigmoid(g) * u).astype(BF16), swd_ref[...]) + _from_tiles(acc, ROW_BLK)
    o_ref[...] = xn_ref[...] + mod_ref[:, 5 * d:6 * d] * _rms(f, g_ref[...])


def _combine(npieces, lslot, wts, piece_src, y_sorted, h2b_rows, xn_rows, mod3, g, swg, swu, swd,
             n_steps, per_batch):
    last = n_steps - 1
    full = lambda shape: pl.BlockSpec(shape, lambda n, *_: (0,) * len(shape))
    rows = lambda w_: pl.BlockSpec((ROW_BLK, w_), lambda n, *_: (n, 0))
    smem = lambda w_, imap: pl.BlockSpec((None, 1, w_), imap, memory_space=pltpu.SMEM)
    n_lat = SEQ // ROW_BLK
    grid_spec = pltpu.PrefetchScalarGridSpec(
        num_scalar_prefetch=1,
        grid=(n_steps,),
        in_specs=[
            smem(TOP_K * ROW_BLK, lambda n, *_: (n, 0, 0)),
            smem(TOP_K * ROW_BLK, lambda n, *_: (n, 0, 0)),
            smem(PIECES_MAX, lambda n, *_: (n, 0, 0)),
            smem(PIECES_MAX, lambda n, *_: (jnp.minimum(n + 1, last), 0, 0)),
            pl.BlockSpec(memory_space=pl.ANY),
            rows(D_MODEL), rows(D_MODEL),
            pl.BlockSpec((None, 1, 6 * D_MODEL),
                         lambda n, *_: (_mod_row(n % per_batch, n // per_batch, n_lat), 0, 0)),
            full((1, D_MODEL)),
            full((D_MODEL, SHARED_DIM)), full((D_MODEL, SHARED_DIM)), full((SHARED_DIM, D_MODEL)),
        ],
        out_specs=rows(D_MODEL),
        scratch_shapes=[
            pltpu.VMEM((2, STG_ROWS, LANES), F32),
            pltpu.VMEM((ROW_BLK * TILE_ROWS, LANES), F32),
            pltpu.SemaphoreType.DMA((2,)),
        ],
    )
    return pl.pallas_call(
        functools.partial(_combine_kernel, n_steps=n_steps),
        grid_spec=grid_spec,
        out_shape=jax.ShapeDtypeStruct(xn_rows.shape, F32),
        compiler_params=pltpu.CompilerParams(dimension_semantics=("arbitrary",),
                                             vmem_limit_bytes=VMEM_LIMIT),
        name="combine",
    )(npieces, lslot, wts, piece_src, piece_src, y_sorted, h2b_rows, xn_rows, mod3, g, swg, swu, swd)


def _moe_plan(cnt_rows, n_blocks):
    i32 = jnp.int32
    per_blk = MOE_BLK // PIECE
    to_expert = lambda v: v.reshape(v.shape[:-1] + (8, 8)).swapaxes(-1, -2).reshape(v.shape)
    runs = (cnt_rows + PIECE - 1) // PIECE
    loc_end = jnp.cumsum(runs, axis=1)
    loc = loc_end - runs
    npieces = loc_end[:, -1].astype(i32)
    reg = to_expert(runs.sum(axis=0))
    padded = (reg + per_blk - 1) // per_blk * per_blk
    pad_end = jnp.cumsum(padded)
    pad_start = pad_end - padded
    off = to_expert(pad_start)[None, :] + jnp.cumsum(runs, axis=0) - runs
    p = jnp.arange(PIECES_MAX, dtype=i32)[None, :, None]
    mine = (loc[:, None, :] <= p) & (p < loc_end[:, None, :])
    piece_dst = jnp.sum(jnp.where(mine, (off - loc)[:, None, :], 0), axis=-1) + p[:, :, 0]
    tail_n = padded - reg
    t_end = jnp.cumsum(tail_n)
    t_beg = t_end - tail_n
    q = jnp.arange(TAIL_MAX, dtype=i32)[:, None]
    tmine = (t_beg[None, :] <= q) & (q < t_end[None, :])
    tail_dst = jnp.sum(jnp.where(tmine, (pad_start + reg - t_beg)[None, :], 0), axis=-1) + q[:, 0]
    first = jnp.arange(n_blocks, dtype=i32)[:, None] * per_blk
    block_e = jnp.minimum(jnp.sum((pad_end[None, :] <= first).astype(i32), axis=-1), N_EXPERTS - 1)
    n_used = (pad_end[-1] // per_blk).astype(i32).reshape(1)
    return (npieces, piece_dst.astype(i32).reshape(-1, 1, PIECES_MAX),
            tail_dst.astype(i32).reshape(1, 1, TAIL_MAX), t_end[-1].astype(i32).reshape(1),
            block_e.astype(i32), n_used)


def _axial_tables(dim):
    t = jnp.arange(SEQ)
    row = (t // GRID_W).astype(F32)
    col = (t % GRID_W).astype(F32)
    quarter = dim // 4
    freqs = ROPE_THETA ** (-jnp.arange(quarter, dtype=F32) / quarter)
    ar, ac = row[:, None] * freqs, col[:, None] * freqs
    ang = jnp.concatenate([ar, ar, ac, ac], axis=-1)
    sign = jnp.where((jnp.arange(dim) // quarter) % 2 == 0, -1.0, 1.0).astype(F32)
    cos = jnp.concatenate([jnp.cos(ang), jnp.ones((CTX_LEN, dim), F32)], axis=0)
    sin = jnp.concatenate([jnp.sin(ang) * sign, jnp.zeros((CTX_LEN, dim), F32)], axis=0)
    return cos, sin


def _rope_tables():
    cos_h, sin_h = _axial_tables(HEAD_DIM)
    cos_r, sin_r = _axial_tables(B_ROPE)
    cosh, sinh = jnp.tile(cos_h, (1, 4)), jnp.tile(sin_h, (1, 4))
    scale_b = (B_NOPE + B_ROPE) ** -0.5
    ones, zeros = jnp.ones((TOK, 64), F32), jnp.zeros((TOK, 64), F32)
    pad1, pad0 = jnp.ones((TOK, 32), F32), jnp.zeros((TOK, 32), F32)
    cosb = jnp.tile(jnp.concatenate([ones, cos_r, pad1], axis=1) * scale_b, (1, 4))
    sinb = jnp.tile(jnp.concatenate([zeros, sin_r, pad0], axis=1) * scale_b, (1, 4))
    cosr = jnp.concatenate([cos_r, jnp.ones((TOK, 96), F32)], axis=1)
    sinr = jnp.concatenate([sin_r, jnp.zeros((TOK, 96), F32)], axis=1)
    return cosh, sinh, cosb, sinb, cosr, sinr


_GQA_ORDER = (0, 2, 1, 3)


def _head_cols(base, order, width=HEAD_DIM):
    return np.concatenate([base + np.arange(h * width, (h + 1) * width) for h in order])


def _w_in_layout(w_in):
    zeros = lambda n: jnp.zeros((D_MODEL, n), w_in.dtype)
    a_q = w_in[:, _head_cols(0, _GQA_ORDER)]
    a_kv = w_in[:, 256:512]
    b_q = jnp.concatenate(
        [jnp.concatenate([w_in[:, 512 + h * 96:512 + (h + 1) * 96], zeros(32)], axis=1)
         for h in range(B_HEADS)], axis=1)
    b_c = w_in[:, 896:1024]
    b_r = jnp.concatenate([w_in[:, 1024:1056], zeros(96)], axis=1)
    c_q = w_in[:, _head_cols(1056, _GQA_ORDER)]
    rest = w_in[:, 1312:2336]
    return jnp.concatenate([a_q, a_kv, b_q, b_c, b_r, c_q, rest], axis=1).astype(BF16)


def _w_kv_layout(w_kv_up):
    zeros = jnp.zeros((B_KV_RANK, 64), w_kv_up.dtype)
    nope = [jnp.concatenate([w_kv_up[:, h * 128:h * 128 + 64], zeros], axis=1) for h in range(B_HEADS)]
    val = [w_kv_up[:, h * 128 + 64:(h + 1) * 128] for h in range(B_HEADS)]
    return jnp.concatenate(nope + val, axis=1).astype(BF16)


def _w_out_layout(w_out):
    rows = np.concatenate([_head_cols(0, _GQA_ORDER), np.arange(256, 512),
                           _head_cols(512, _GQA_ORDER), np.arange(768, 1024)])
    return w_out[rows].astype(BF16)


_ROUTER_ROWS = np.array([g * 8 + j for j in range(8) for g in range(8)])


def kernel(x, c, ctx, c_ctx, ada_w, ada_b, pre_mix_g, post_mix_g, pre_ffn_g, post_ffn_g, w_in, a_q_norm, a_k_norm, b_kv_norm, b_w_kv_up, c_sink, d_rpb, w_out, router_w, router_b, exp_w_gate, exp_w_up, exp_w_down, sh_w_gate, sh_w_up, sh_w_down):
    bsz = x.shape[0]
    depth = ada_w.shape[0]
    xall = jnp.concatenate([x, ctx], axis=1)

    cvec = jnp.zeros((16, D_MODEL), F32).at[:bsz].set(c).at[8].set(c_ctx)
    mod = _ada(cvec, ada_w, ada_b)
    tabs = _rope_tables()
    avg = jnp.kron(jnp.eye(4, dtype=F32), jnp.full((64, 64), 1.0 / 64, F32)).astype(BF16)
    row2 = lambda v: v.reshape(1, -1).astype(F32)

    for l in range(depth):
        with_ctx = l < depth - 1
        mod3 = mod[l].reshape(16, 1, 6 * D_MODEL)

        proj = _inproj(
            xall, mod3, row2(pre_mix_g[l]), _w_in_layout(w_in[l]),
            row2(jnp.tile(a_q_norm[l], 4) * HEAD_DIM ** -0.5), row2(jnp.tile(a_k_norm[l], 2)),
            row2(b_kv_norm[l]), _w_kv_layout(b_w_kv_up[l]), avg, tabs)
        qa, ka, va, qb, kb, vb, qc, kc, vc, qd, kd, vd = proj

        nq = NQ + (CTX_LEN // QBLK if with_ctx else 0)
        full = lambda shape: pl.BlockSpec(shape, lambda b, i: (0,) * len(shape))
        mix_a = _attn_call(_attn_a_kernel, "attn_a", qa, ka, va, [], [], nq)
        mix_b = _attn_call(_attn_b_kernel, "attn_b", qb, kb, vb, [], [], nq // 2, qblk=2 * QBLK)
        sink = jnp.zeros((8, LANES), F32).at[:4].set(
            jnp.broadcast_to(c_sink[l][np.array(_GQA_ORDER)][:, None], (4, LANES)))
        mix_c = _attn_call(_attn_c_kernel, "attn_c", qc, kc, vc, [sink], [full((8, LANES))], nq)
        bias = _na_bias(d_rpb[l])
        mix_d = _attn_call(
            _attn_d_kernel, "attn_d", qd, kd, vd, [bias],
            [pl.BlockSpec((None, D_HEADS, QBLK, NA_WIN), lambda b, i: (_na_pattern(i), 0, 0, 0))], nq)

        nb = (TOK if with_ctx else SEQ) // ROW_BLK
        rb = jnp.broadcast_to(router_b[l][_ROUTER_ROWS][:, None], (N_EXPERTS, ROW_BLK)).astype(F32)
        xn, h2b, h2t, wt_t, ls_t, cnt_t = _outproj(
            (mix_a, mix_b, mix_c, mix_d), _w_out_layout(w_out[l]), xall, mod3,
            row2(post_mix_g[l]), row2(pre_ffn_g[l]),
            router_w[l].T[_ROUTER_ROWS].astype(BF16), rb, nb)

        t_used = nb * ROW_BLK
        n_tok = bsz * t_used
        n_steps = bsz * nb
        n_blocks = -(-(n_tok * TOP_K + n_steps * N_EXPERTS * (PIECE - 1) + N_EXPERTS * (MOE_BLK - 1))
                     // MOE_BLK)
        npieces, piece_dst, tail_dst, ntail, block_e, n_used = _moe_plan(
            cnt_t[..., 0].reshape(n_steps, N_EXPERTS), n_blocks)
        lslot = ls_t.reshape(n_steps, 1, TOP_K * ROW_BLK)
        wts = wt_t.reshape(n_steps, 1, TOP_K * ROW_BLK)

        x_sorted = _dispatch(npieces, n_used, ntail, lslot, piece_dst, tail_dst,
                             h2t.reshape(n_tok * TILE_ROWS, LANES), n_steps, n_blocks)
        y_sorted = _experts(block_e, n_used, x_sorted, exp_w_gate, exp_w_up, exp_w_down, l, n_blocks)
        xall = _combine(
            npieces, lslot, wts, piece_dst, y_sorted, h2b.reshape(n_tok, D_MODEL),
            xn.reshape(n_tok, D_MODEL), mod3, row2(post_ffn_g[l]),
            sh_w_gate[l].astype(BF16), sh_w_up[l].astype(BF16), sh_w_down[l].astype(BF16),
            n_steps, nb).reshape(bsz, t_used, D_MODEL)

    return xall[:, :SEQ]
```

```python
import functools

import numpy as np
import jax
import jax.numpy as jnp
from jax import lax
from jax.experimental import pallas as pl
from jax.experimental.pallas import tpu as pltpu

F32 = jnp.float32
BF16 = jnp.bfloat16

D_MODEL = 1024
SEQ = 2048
CTX_LEN = 256
TOK = SEQ + CTX_LEN
GRID_W = 64
HEAD_DIM = 64
ROPE_THETA = 10000.0
EPS = 1e-6
NEG = -1e30

A_HEADS, A_KV_HEADS = 4, 2
B_HEADS, B_NOPE, B_ROPE, B_V, B_KV_RANK = 4, 64, 32, 64, 128
C_HEADS, C_KV_HEADS, C_WINDOW = 4, 2, 128
D_HEADS, NA_ROWS, NA_COLS = 4, 8, 16

N_EXPERTS, TOP_K, N_GROUPS, TOPK_GROUPS = 64, 8, 8, 4
EXPERT_DIM, SHARED_DIM = 256, 256
ROUTED_SCALE = 2.5

LANES = 128
QBLK = 128
ROW_BLK = 256
MOE_BLK = 256
TILE_ROWS = D_MODEL // LANES
PIECE = 8
PIECE_ROWS = PIECE * TILE_ROWS
PIECES_MAX = 320
STG_ROWS = PIECES_MAX * PIECE_ROWS
TAIL_MAX = 2048
LOG_CHUNKS = 4
N_CHUNKS = 1 << LOG_CHUNKS
VMEM_LIMIT = 56 * 1024 * 1024
NQ = SEQ // QBLK
NA_WIN = 10 * GRID_W

_AQ, _AK, _AV, _BQ, _BC, _BR = 0, 256, 384, 512, 1024, 1152
_CQ, _CK, _CV, _DQ, _DK, _DV = 1280, 1536, 1664, 1792, 2048, 2304
IN_W = 2560

_NT = (((1,), (1,)), ((), ()))


def _sigmoid(x):
    return 1.0 / (1.0 + jnp.exp(-x))


def _rms(x, g):
    return x * lax.rsqrt(jnp.mean(x * x, axis=-1, keepdims=True) + EPS) * g


def _dot(a, b):
    return jnp.dot(a, b, preferred_element_type=F32)


def _dot_nt(a, b):
    return lax.dot_general(a, b, _NT, preferred_element_type=F32)


def _ada_kernel(c_ref, w_ref, b_ref, o_ref):
    c = c_ref[...]
    a = (c * _sigmoid(c)).astype(BF16)
    o_ref[...] = _dot(a, w_ref[...].astype(BF16)) + b_ref[...]


def _ada(cvec, ada_w, ada_b):
    depth = ada_w.shape[0]
    n = ada_w.shape[2]
    tn = 1536
    return pl.pallas_call(
        _ada_kernel,
        grid=(depth, n // tn),
        in_specs=[
            pl.BlockSpec((16, D_MODEL), lambda l, j: (0, 0)),
            pl.BlockSpec((None, D_MODEL, tn), lambda l, j: (l, 0, j)),
            pl.BlockSpec((None, 1, tn), lambda l, j: (l, 0, j)),
        ],
        out_specs=pl.BlockSpec((None, 16, tn), lambda l, j: (l, 0, j)),
        out_shape=jax.ShapeDtypeStruct((depth, 16, n), F32),
        name="ada",
    )(cvec, ada_w, ada_b.reshape(depth, 1, n))


def _rope(x, cos, sin, chunk):
    outs = []
    for c in range(x.shape[1] // LANES):
        sl = slice(c * LANES, (c + 1) * LANES)
        xs = x[:, sl]
        lane = lax.broadcasted_iota(jnp.int32, xs.shape, 1)
        first = (lane & chunk) == 0
        rot = jnp.where(first, pltpu.roll(xs, LANES - chunk, 1), pltpu.roll(xs, chunk, 1))
        outs.append(xs * cos[:, sl] + rot * sin[:, sl])
    return outs[0] if len(outs) == 1 else jnp.concatenate(outs, axis=1)


def _head_norm(x, avg):
    sq = x * x
    hi = sq.astype(BF16)
    lo = (sq - hi.astype(F32)).astype(BF16)
    ms = _dot(hi, avg) + _dot(lo, avg)
    return x * lax.rsqrt(ms + EPS)


def _inproj_kernel(x_ref, mod_ref, g_ref, w_ref, qg_ref, kg_ref, kvg_ref, wkv_ref, avg_ref,
                   cosh_ref, sinh_ref, cosb_ref, sinb_ref, cosr_ref, sinr_ref,
                   qa_ref, ka_ref, va_ref, qb_ref, kb_ref, vb_ref,
                   qc_ref, kc_ref, vc_ref, qd_ref, kd_ref, vd_ref):
    d = D_MODEL
    h = _rms(x_ref[...], g_ref[...]) * (1.0 + mod_ref[:, d:2 * d]) + mod_ref[:, 0:d]
    p = _dot(h.astype(BF16), w_ref[...])
    cosh, sinh = cosh_ref[...], sinh_ref[...]
    avg = avg_ref[...]

    qa = _head_norm(p[:, _AQ:_AQ + 256], avg) * qg_ref[...]
    qa_ref[...] = _rope(qa, cosh, sinh, 16).astype(BF16)
    ka = _head_norm(p[:, _AK:_AK + 128], avg[0:128, 0:128]) * kg_ref[...]
    ka_ref[...] = _rope(ka, cosh[:, 0:128], sinh[:, 0:128], 16).astype(BF16)
    ones = jnp.ones((p.shape[0], LANES), BF16)
    va_ref[...] = jnp.concatenate([p[:, _AV:_AV + 128].astype(BF16), ones], axis=1)

    qb_ref[...] = _rope(p[:, _BQ:_BQ + 512], cosb_ref[...], sinb_ref[...], 8).astype(BF16)
    cn = _rms(p[:, _BC:_BC + 128], kvg_ref[...])
    kv = _dot(cn.astype(BF16), wkv_ref[...])
    kr = _rope(p[:, _BR:_BR + 128], cosr_ref[...], sinr_ref[...], 8)
    kr = pltpu.roll(kr, 64, 1)
    kb_ref[...] = jnp.concatenate(
        [kv[:, hh * 128:(hh + 1) * 128] + kr for hh in range(B_HEADS)], axis=1).astype(BF16)
    vb_ref[...] = jnp.concatenate(
        [kv[:, 512:640].astype(BF16), ones, kv[:, 640:768].astype(BF16), ones], axis=1)

    qc_ref[...] = _rope(p[:, _CQ:_CQ + 256] * 0.125, cosh, sinh, 16).astype(BF16)
    kc_ref[...] = _rope(p[:, _CK:_CK + 128], cosh[:, 0:128], sinh[:, 0:128], 16).astype(BF16)
    vc_ref[...] = jnp.concatenate([p[:, _CV:_CV + 128].astype(BF16), ones], axis=1)

    qd_ref[...] = (p[:, _DQ:_DQ + 256] * 0.125).astype(BF16)
    kd_ref[...] = p[:, _DK:_DK + 256].astype(BF16)
    vd_ref[...] = jnp.concatenate(
        [p[:, _DV:_DV + 128].astype(BF16), ones, p[:, _DV + 128:_DV + 256].astype(BF16), ones], axis=1)


def _mod_row(j, b, n_lat_blocks):
    return jnp.where(j < n_lat_blocks, b, 8)


def _inproj(xall, mod3, g, w, qg, kg, kvg, wkv, avg, tabs):
    bsz = xall.shape[0]
    nb = TOK // ROW_BLK
    n_lat = SEQ // ROW_BLK
    full = lambda shape: pl.BlockSpec(shape, lambda b, j: (0,) * len(shape))
    row = lambda w_: pl.BlockSpec((ROW_BLK, w_), lambda b, j: (j, 0))
    out = lambda w_: pl.BlockSpec((None, ROW_BLK, w_), lambda b, j: (b, j, 0))
    widths = (256, 128, 256, 512, 512, 512, 256, 128, 256, 256, 256, 512)
    return pl.pallas_call(
        _inproj_kernel,
        grid=(bsz, nb),
        in_specs=[
            pl.BlockSpec((None, ROW_BLK, D_MODEL), lambda b, j: (b, j, 0)),
            pl.BlockSpec((None, 1, 6 * D_MODEL), lambda b, j: (_mod_row(j, b, n_lat), 0, 0)),
            full((1, D_MODEL)), full((D_MODEL, IN_W)),
            full((1, 256)), full((1, 128)), full((1, 128)), full((128, 768)), full((256, 256)),
            row(256), row(256), row(512), row(512), row(128), row(128),
        ],
        out_specs=[out(w_) for w_ in widths],
        out_shape=[jax.ShapeDtypeStruct((bsz, TOK, w_), BF16) for w_ in widths],
        name="inproj",
    )(xall, mod3, g, w, qg, kg, kvg, wkv, avg, *tabs)


def _softmax_pv(segs, sink=None):
    m = None
    for s, _ in segs:
        ms = jnp.max(s, axis=-1, keepdims=True)
        m = ms if m is None else jnp.maximum(m, ms)
    if sink is not None:
        m = jnp.maximum(m, sink)
    acc = None
    for s, v1 in segs:
        r = _dot(jnp.exp((s - m).astype(BF16)), v1)
        acc = r if acc is None else acc + r
    l = acc[:, LANES:LANES + 1]
    if sink is not None:
        l = l + jnp.exp(sink - m)
    return acc[:, 0:LANES] / l


def _lane_lo(dtype_shape):
    return lax.broadcasted_iota(jnp.int32, dtype_shape, 1) < HEAD_DIM


def _half(qt, hh):
    lo = _lane_lo(qt.shape)
    return jnp.where(lo if hh == 0 else jnp.logical_not(lo), qt, jnp.zeros_like(qt))


def _merge_halves(o0, o1):
    return jnp.where(_lane_lo(o0.shape), o0, o1)


def _stack_heads(q_ref, tiles):
    return jnp.concatenate(
        [_half(q_ref[:, t * LANES:(t + 1) * LANES], hh) for t in tiles for hh in range(2)], axis=0)


def _store_heads(o_ref, o, tiles):
    n = o_ref.shape[0]
    for ti, t in enumerate(tiles):
        o0, o1 = o[2 * ti * n:(2 * ti + 1) * n], o[(2 * ti + 1) * n:(2 * ti + 2) * n]
        o_ref[:, t * LANES:(t + 1) * LANES] = _merge_halves(o0, o1).astype(o_ref.dtype)


def _attn_a_kernel(q_ref, k_ref, v_ref, o_ref, *, n_lat):
    def run(k, v1):
        for t in range(2):
            o = _softmax_pv([(_dot_nt(_stack_heads(q_ref, (t,)), k), v1)])
            _store_heads(o_ref, o, (t,))

    @pl.when(pl.program_id(1) < n_lat)
    def _():
        run(k_ref[...], v_ref[...])

    @pl.when(pl.program_id(1) >= n_lat)
    def _():
        run(k_ref[SEQ:TOK, :], v_ref[SEQ:TOK, :])


def _attn_b_kernel(q_ref, k_ref, v_ref, o_ref, *, n_lat):
    def run(lo_, hi_):
        for t in range(2):
            v1 = v_ref[lo_:hi_, t * 256:(t + 1) * 256]
            o = []
            for hh in range(2):
                hd = 2 * t + hh
                q = q_ref[:, hd * LANES:(hd + 1) * LANES]
                k = k_ref[lo_:hi_, hd * LANES:(hd + 1) * LANES]
                o.append(_softmax_pv([(_dot_nt(q, k), v1)]))
            o_ref[:, t * LANES:(t + 1) * LANES] = _merge_halves(o[0], o[1]).astype(o_ref.dtype)

    @pl.when(pl.program_id(1) < n_lat)
    def _():
        run(0, TOK)

    @pl.when(pl.program_id(1) >= n_lat)
    def _():
        run(SEQ, TOK)


def _attn_c_kernel(q_ref, k_ref, v_ref, sink_ref, o_ref, *, n_lat):
    i = pl.program_id(1)
    qblk = o_ref.shape[0]
    band = 3 * qblk

    def run(latent):
        if latent:
            start = pl.multiple_of(jnp.clip((i - 1) * qblk, 0, SEQ - band), qblk)
            shape = (2 * qblk, band)
            qpos = i * qblk + (lax.broadcasted_iota(jnp.int32, shape, 0) & (qblk - 1))
            kpos = start + lax.broadcasted_iota(jnp.int32, shape, 1)
            valid = jnp.abs(qpos - kpos) <= C_WINDOW
        for t in range(2):
            q = _stack_heads(q_ref, (t,))
            segs = [(_dot_nt(q, k_ref[SEQ:TOK, :]), v_ref[SEQ:TOK, :])]
            if latent:
                s = jnp.where(valid, _dot_nt(q, k_ref[pl.ds(start, band), :]), NEG)
                segs.insert(0, (s, v_ref[pl.ds(start, band), :]))
            sink = jnp.concatenate(
                [jnp.broadcast_to(sink_ref[j:j + 1, 0:1], (qblk, 1)) for j in (2 * t, 2 * t + 1)], axis=0)
            _store_heads(o_ref, _softmax_pv(segs, sink=sink), (t,))

    @pl.when(i < n_lat)
    def _():
        run(True)

    @pl.when(i >= n_lat)
    def _():
        run(False)


def _attn_d_kernel(q_ref, k_ref, v_ref, bias_ref, o_ref, *, n_lat):
    i = pl.program_id(1)
    qblk = o_ref.shape[0]

    def run(latent):
        if latent:
            start = pl.multiple_of(jnp.clip((i - 2) * qblk, 0, SEQ - NA_WIN), qblk)
        for t in range(2):
            ksl, vsl = slice(t * LANES, (t + 1) * LANES), slice(t * 256, (t + 1) * 256)
            q = _stack_heads(q_ref, (t,))
            segs = [(_dot_nt(q, k_ref[SEQ:TOK, ksl]), v_ref[SEQ:TOK, vsl])]
            if latent:
                bias = bias_ref[2 * t:2 * t + 2].reshape(2 * qblk, NA_WIN)
                segs.insert(0, (_dot_nt(q, k_ref[pl.ds(start, NA_WIN), ksl]) + bias,
                                v_ref[pl.ds(start, NA_WIN), vsl]))
            _store_heads(o_ref, _softmax_pv(segs), (t,))

    @pl.when(i < n_lat)
    def _():
        run(True)

    @pl.when(i >= n_lat)
    def _():
        run(False)


def _attn_call(kernel, name, q, k, v, extra, extra_specs, n_blocks, qblk=QBLK):
    bsz = q.shape[0]
    return pl.pallas_call(
        functools.partial(kernel, n_lat=SEQ // qblk),
        grid=(bsz, n_blocks),
        in_specs=[
            pl.BlockSpec((None, qblk, q.shape[2]), lambda b, i: (b, i, 0)),
            pl.BlockSpec((None, TOK, k.shape[2]), lambda b, i: (b, 0, 0)),
            pl.BlockSpec((None, TOK, v.shape[2]), lambda b, i: (b, 0, 0)),
        ] + extra_specs,
        out_specs=pl.BlockSpec((None, qblk, 256), lambda b, i: (b, i, 0)),
        out_shape=jax.ShapeDtypeStruct((bsz, n_blocks * qblk, 256), BF16),
        name=name,
    )(q, k, v, *extra)


def _na_pattern(i):
    return jnp.where(i < 2, i, jnp.where(i < NQ - 2, 2, jnp.minimum(i, NQ - 1) - (NQ - 5)))


def _na_bias(rpb):
    rows = SEQ // GRID_W
    n_dr, n_dc = 2 * NA_ROWS - 1, 2 * NA_COLS - 1
    qc = np.arange(GRID_W)
    dc = np.clip(qc[None, :] - qc[:, None] + NA_COLS - 1, 0, n_dc - 1)
    sel_c = jnp.asarray(np.eye(n_dc, dtype=np.float32)[dc])
    cs = np.clip(qc - NA_COLS // 2, 0, GRID_W - NA_COLS)[:, None]
    col_ok = (qc[None, :] >= cs) & (qc[None, :] < cs + NA_COLS)
    out = []
    for i in (0, 1, 2, NQ - 2, NQ - 1):
        start_row = min(max(2 * i - 4, 0), rows - 10)
        qr = 2 * i + np.arange(2)
        kr = start_row + np.arange(10)
        rs = np.clip(qr - NA_ROWS // 2, 0, rows - NA_ROWS)[:, None]
        row_ok = (kr[None] >= rs) & (kr[None] < rs + NA_ROWS)
        dr = np.clip(kr[None] - qr[:, None] + NA_ROWS - 1, 0, n_dr - 1)
        sel_r = jnp.asarray(np.eye(n_dr, dtype=np.float32)[dr])
        b = jnp.einsum("hrc,abr,qkc->haqbk", rpb.astype(F32), sel_r, sel_c,
                       precision=lax.Precision.HIGHEST)
        valid = row_ok[:, None, :, None] & col_ok[None, :, None, :]
        out.append(jnp.where(valid[None], b, NEG).reshape(D_HEADS, QBLK, NA_WIN))
    return jnp.stack(out)


def _outproj_kernel(ma_ref, mb_ref, mc_ref, md_ref, w_ref, x_ref, mod_ref, g1_ref, g2_ref,
                    wr_ref, rb_ref, xn_ref, h2b_ref, h2t_ref, wt_ref, ls_ref, cnt_ref):
    d = D_MODEL
    mix = jnp.concatenate([ma_ref[...], mb_ref[...], mc_ref[...], md_ref[...]], axis=1)
    y = _rms(_dot(mix, w_ref[...]), g1_ref[...])
    xn = x_ref[...] + mod_ref[:, 2 * d:3 * d] * y
    xn_ref[...] = xn
    h2 = _rms(xn, g2_ref[...]) * (1.0 + mod_ref[:, 4 * d:5 * d]) + mod_ref[:, 3 * d:4 * d]
    n = h2.shape[0]
    h2b_ref[...] = h2.astype(BF16)
    for s in range(TILE_ROWS):
        h2t_ref[pl.ds(s, n, stride=TILE_ROWS), :] = h2[:, s * LANES:(s + 1) * LANES]

    s = _sigmoid(_dot_nt(wr_ref[...], h2.astype(BF16)))
    sb = s + rb_ref[...]
    per = N_EXPERTS // N_GROUPS
    slab = [sb[j * 8:(j + 1) * 8, :] for j in range(per)]
    sraw = [s[j * 8:(j + 1) * 8, :] for j in range(per)]
    giota = lax.broadcasted_iota(jnp.int32, (N_GROUPS, n), 0)
    eid = [giota * per + j for j in range(per)]
    ninf = -jnp.inf

    m1 = functools.reduce(jnp.maximum, slab)
    j1 = functools.reduce(jnp.minimum, [jnp.where(slab[j] == m1, j, per) for j in range(per)])
    m2 = functools.reduce(jnp.maximum, [jnp.where(j1 == j, ninf, slab[j]) for j in range(per)])
    gs = m1 + m2

    gsel = jnp.zeros((N_GROUPS, n), jnp.bool_)
    for _ in range(TOPK_GROUPS):
        m = jnp.max(gs, axis=0, keepdims=True)
        gi = jnp.min(jnp.where(gs == m, giota, N_GROUPS), axis=0, keepdims=True)
        hit = giota == gi
        gsel = jnp.logical_or(gsel, hit)
        gs = jnp.where(hit, ninf, gs)
    slab = [jnp.where(gsel, sj, ninf) for sj in slab]

    def pick(hits, vals):
        return jnp.sum(functools.reduce(jnp.add, [jnp.where(hits[j], vals[j], 0.0) for j in range(per)]),
                       axis=0, keepdims=True)

    kiota = lax.broadcasted_iota(jnp.int32, (TOP_K, n), 0)
    wts = jnp.zeros((TOP_K, n), F32)
    chosen = []
    for k in range(TOP_K):
        m = jnp.max(functools.reduce(jnp.maximum, slab), axis=0, keepdims=True)
        cand = functools.reduce(jnp.minimum,
                                [jnp.where(slab[j] == m, eid[j], N_EXPERTS) for j in range(per)])
        ei = jnp.min(cand, axis=0, keepdims=True)
        hits = [eid[j] == ei for j in range(per)]
        slab = [jnp.where(hits[j], ninf, slab[j]) for j in range(per)]
        wts = jnp.where(kiota == k, pick(hits, sraw), wts)
        chosen.append(ei)
    wt_ref[...] = wts / jnp.sum(wts, axis=0, keepdims=True) * ROUTED_SCALE

    hits_k = [[eid[j] == chosen[k] for j in range(per)] for k in range(TOP_K)]
    sel = jnp.concatenate(
        [functools.reduce(jnp.add, [jnp.where(hits_k[k][j], 1.0, 0.0) for k in range(TOP_K)])
         for j in range(per)], axis=0)
    tri = jnp.where(lax.broadcasted_iota(jnp.int32, (n, n), 0) < lax.broadcasted_iota(jnp.int32, (n, n), 1),
                    1.0, 0.0).astype(BF16)
    before = _dot(sel.astype(BF16), tri)
    cnt = jnp.sum(sel, axis=1, keepdims=True)
    run = jnp.floor((cnt + (PIECE - 1.0)) * (1.0 / PIECE)) * PIECE
    low = jnp.where(lax.broadcasted_iota(jnp.int32, (N_EXPERTS, N_EXPERTS), 1)
                    < lax.broadcasted_iota(jnp.int32, (N_EXPERTS, N_EXPERTS), 0), 1.0, 0.0).astype(BF16)
    start = _dot(low, jnp.broadcast_to(run, (N_EXPERTS, LANES)).astype(BF16))[:, 0:1]
    base = before + start
    base = [base[j * 8:(j + 1) * 8, :] for j in range(per)]
    lsl = jnp.zeros((TOP_K, n), F32)
    for k in range(TOP_K):
        lsl = jnp.where(kiota == k, pick(hits_k[k], base), lsl)
    ls_ref[...] = lsl.astype(jnp.int32)
    cnt_ref[...] = jnp.broadcast_to(cnt, (N_EXPERTS, LANES)).astype(jnp.int32)


def _outproj(mixes, w, xall, mod3, g1, g2, wr, rb, nb):
    bsz = xall.shape[0]
    n_lat = SEQ // ROW_BLK
    full = lambda shape: pl.BlockSpec(shape, lambda b, j: (0,) * len(shape))
    rows = lambda w_: pl.BlockSpec((None, ROW_BLK, w_), lambda b, j: (b, j, 0))
    route = pl.BlockSpec((None, None, TOP_K, ROW_BLK), lambda b, j: (b, j, 0, 0))
    return pl.pallas_call(
        _outproj_kernel,
        grid=(bsz, nb),
        in_specs=[rows(256)] * 4 + [
            full((D_MODEL, D_MODEL)), rows(D_MODEL),
            pl.BlockSpec((None, 1, 6 * D_MODEL), lambda b, j: (_mod_row(j, b, n_lat), 0, 0)),
            full((1, D_MODEL)), full((1, D_MODEL)),
            full((N_EXPERTS, D_MODEL)), full((N_EXPERTS, ROW_BLK)),
        ],
        out_specs=[
            rows(D_MODEL), rows(D_MODEL),
            pl.BlockSpec((None, ROW_BLK * TILE_ROWS, LANES), lambda b, j: (b, j, 0)),
            route, route,
            pl.BlockSpec((None, None, N_EXPERTS, LANES), lambda b, j: (b, j, 0, 0)),
        ],
        out_shape=[
            jax.ShapeDtypeStruct((bsz, nb * ROW_BLK, D_MODEL), F32),
            jax.ShapeDtypeStruct((bsz, nb * ROW_BLK, D_MODEL), BF16),
            jax.ShapeDtypeStruct((bsz, nb * ROW_BLK * TILE_ROWS, LANES), F32),
            jax.ShapeDtypeStruct((bsz, nb, TOP_K, ROW_BLK), F32),
            jax.ShapeDtypeStruct((bsz, nb, TOP_K, ROW_BLK), jnp.int32),
            jax.ShapeDtypeStruct((bsz, nb, N_EXPERTS, LANES), jnp.int32),
        ],
        name="outproj",
    )(*mixes, w, xall, mod3, g1, g2, wr, rb)


def _tile(t):
    return pl.ds(pl.multiple_of(t * TILE_ROWS, TILE_ROWS), TILE_ROWS)


def _piece(p):
    return pl.ds(pl.multiple_of(p * PIECE_ROWS, PIECE_ROWS), PIECE_ROWS)


def _start_pieces(lo, hi, start):
    n = hi - lo

    def pair(q, carry):
        start(lo + 2 * q, 0)
        start(lo + 2 * q + 1, 1)
        return carry
    lax.fori_loop(0, lax.shift_right_logical(n, 1), pair, 0)

    @pl.when((n & 1) == 1)
    def _():
        start(hi - 1, 0)


def _interleave(body, n_pieces, start):
    per = ROW_BLK // N_CHUNKS

    def chunk(c, carry):
        for tt in range(per):
            body(c * per + tt)
        _start_pieces(lax.shift_right_logical(c * n_pieces, LOG_CHUNKS),
                      lax.shift_right_logical((c + 1) * n_pieces, LOG_CHUNKS), start)
        return carry
    lax.fori_loop(0, N_CHUNKS, chunk, 0)


def _dispatch_kernel(np_ref, nu_ref, nt_ref, ls_ref, pd_ref, pdp_ref, td_ref, h_ref, xs_hbm,
                     stg, zblk, sem, zsem, *, n_steps, n_blocks):
    i = pl.program_id(0)
    slot = i % 3
    prev = (i + 2) % 3

    def piece_copy(s, p, d):
        return pltpu.make_async_copy(stg.at[s, _piece(p), :], xs_hbm.at[_piece(d), :], sem.at[s])

    def wait_pieces(s, n):
        def body(p, carry):
            piece_copy(s, 0, 0).wait()
            return carry
        lax.fori_loop(0, n, body, 0)

    def zero_piece_copy(d):
        return pltpu.make_async_copy(zblk.at[pl.ds(0, PIECE_ROWS), :], xs_hbm.at[_piece(d), :], zsem.at[0])

    def zero_block_copy(b):
        rows = pl.ds(pl.multiple_of(b * (MOE_BLK * TILE_ROWS), MOE_BLK * TILE_ROWS), MOE_BLK * TILE_ROWS)
        return pltpu.make_async_copy(zblk, xs_hbm.at[rows, :], zsem.at[1])

    @pl.when(i == 0)
    def _():
        zblk[...] = jnp.zeros_like(zblk)
        for s in range(3):
            def zero(c, carry, s=s):
                stg[s, pl.ds(pl.multiple_of(c * 2048, 2048), 2048), :] = jnp.zeros((2048, LANES), F32)
                return carry
            lax.fori_loop(0, STG_ROWS // 2048, zero, 0)

    @pl.when(i >= 3)
    def _():
        wait_pieces(slot, np_ref[jnp.maximum(i - 3, 0)])

    def scatter_token(t):
        tile = h_ref[_tile(t), :]
        for k in range(TOP_K):
            stg[slot, _tile(ls_ref[0, k * ROW_BLK + t]), :] = tile

    np_prev = jnp.where(i >= 1, np_ref[jnp.maximum(i - 1, 0)], 0)
    _interleave(scatter_token, np_prev,
                lambda p, prio: piece_copy(prev, p, pdp_ref[0, p]).start(priority=prio))

    @pl.when(i == n_steps - 1)
    def _():
        _start_pieces(0, np_ref[i], lambda p, prio: piece_copy(slot, p, pd_ref[0, p]).start(priority=prio))

        def tail(p, carry):
            zero_piece_copy(td_ref[0, p]).start()
            return carry
        lax.fori_loop(0, nt_ref[0], tail, 0)

        def blk(b, carry):
            zero_block_copy(b).start()
            return carry
        lax.fori_loop(nu_ref[0], n_blocks, blk, 0)

        wait_pieces((i + 1) % 3, np_ref[jnp.maximum(i - 2, 0)])
        wait_pieces(prev, np_ref[jnp.maximum(i - 1, 0)])
        wait_pieces(slot, np_ref[i])

        def tail_wait(p, carry):
            zero_piece_copy(0).wait()
            return carry
        lax.fori_loop(0, nt_ref[0], tail_wait, 0)

        def blk_wait(b, carry):
            zero_block_copy(0).wait()
            return carry
        lax.fori_loop(nu_ref[0], n_blocks, blk_wait, 0)


def _dispatch(npieces, n_used, ntail, lslot, piece_dst, tail_dst, h2t, n_steps, n_blocks):
    assert n_steps >= 3
    smem = lambda n, imap: pl.BlockSpec((None, 1, n), imap, memory_space=pltpu.SMEM)
    grid_spec = pltpu.PrefetchScalarGridSpec(
        num_scalar_prefetch=3,
        grid=(n_steps,),
        in_specs=[
            smem(TOP_K * ROW_BLK, lambda i, *_: (i, 0, 0)),
            smem(PIECES_MAX, lambda i, *_: (i, 0, 0)),
            smem(PIECES_MAX, lambda i, *_: (jnp.maximum(i - 1, 0), 0, 0)),
            smem(TAIL_MAX, lambda i, *_: (0, 0, 0)),
            pl.BlockSpec((ROW_BLK * TILE_ROWS, LANES), lambda i, *_: (i, 0)),
        ],
        out_specs=pl.BlockSpec(memory_space=pl.ANY),
        scratch_shapes=[
            pltpu.VMEM((3, STG_ROWS, LANES), F32),
            pltpu.VMEM((MOE_BLK * TILE_ROWS, LANES), F32),
            pltpu.SemaphoreType.DMA((3,)),
            pltpu.SemaphoreType.DMA((2,)),
        ],
    )
    return pl.pallas_call(
        functools.partial(_dispatch_kernel, n_steps=n_steps, n_blocks=n_blocks),
        grid_spec=grid_spec,
        out_shape=jax.ShapeDtypeStruct((n_blocks * MOE_BLK * TILE_ROWS, LANES), F32),
        compiler_params=pltpu.CompilerParams(dimension_semantics=("arbitrary",),
                                             vmem_limit_bytes=VMEM_LIMIT),
        name="dispatch",
    )(npieces, n_used, ntail, lslot, piece_dst, piece_dst, tail_dst, h2t)


def _from_tiles(ref, n):
    return jnp.concatenate([ref[pl.ds(s, n, stride=TILE_ROWS), :] for s in range(TILE_ROWS)], axis=1)


def _experts_kernel(be_ref, nu_ref, xs_hbm, wg_ref, wu_ref, wd_ref, ys_hbm,
                    xbuf, ybuf, wgb, wub, wdb, xsem, ysem, *, n_blocks):
    i = pl.program_id(0)
    n_used = nu_ref[0]
    rows = MOE_BLK * TILE_ROWS
    half = rows // 2

    def halves(b):
        return [pl.ds(pl.multiple_of(b * rows + h * half, half), half) for h in range(2)]

    def x_copies(b, s):
        return [pltpu.make_async_copy(xs_hbm.at[src, :], xbuf.at[s, pl.ds(h * half, half), :], xsem.at[s])
                for h, src in enumerate(halves(b))]

    def y_copies(b, s):
        return [pltpu.make_async_copy(ybuf.at[s, pl.ds(h * half, half), :], ys_hbm.at[dst, :], ysem.at[s])
                for h, dst in enumerate(halves(b))]

    def start(copies):
        for h, c in enumerate(copies):
            c.start(priority=h)

    def wait(copies):
        for c in copies:
            c.wait()

    @pl.when(i == 0)
    def _():
        start(x_copies(0, 0))

        @pl.when(n_used > 1)
        def _():
            start(x_copies(1, 1))

    @pl.when(i + 2 < n_used)
    def _():
        start(x_copies(i + 2, (i + 2) % 3))

    @pl.when(i < n_used)
    def _():
        wait(x_copies(i, i % 3))

        @pl.when(i >= 2)
        def _():
            wait(y_copies(i - 2, i % 2))

        @pl.when(jnp.logical_or(i == 0, be_ref[i] != be_ref[jnp.maximum(i - 1, 0)]))
        def _():
            wgb[...] = wg_ref[...].astype(BF16)
            wub[...] = wu_ref[...].astype(BF16)
            wdb[...] = wd_ref[...].astype(BF16)

        x = _from_tiles(xbuf.at[i % 3], MOE_BLK).astype(BF16)
        g = _dot(x, wgb[...])
        u = _dot(x, wub[...])
        y = _dot((g * _sigmoid(g) * u).astype(BF16), wdb[...])
        yb = ybuf.at[i % 2]
        for s in range(TILE_ROWS):
            yb[pl.ds(s, MOE_BLK, stride=TILE_ROWS), :] = y[:, s * LANES:(s + 1) * LANES]
        start(y_copies(i, i % 2))

        @pl.when(i == n_used - 1)
        def _():
            @pl.when(i >= 1)
            def _():
                wait(y_copies(i - 1, (i + 1) % 2))
            wait(y_copies(i, i % 2))
            ybuf[0] = jnp.zeros((rows, LANES), F32)

            def fill(b, carry):
                start(y_copies(b, 0))
                return carry
            lax.fori_loop(n_used, n_blocks, fill, 0)

            def fill_wait(b, carry):
                wait(y_copies(0, 0))
                return carry
            lax.fori_loop(n_used, n_blocks, fill_wait, 0)


def _experts(block_e, n_used, x_sorted, wg, wu, wd, layer, n_blocks):
    wspec = lambda shape: pl.BlockSpec((None, None) + shape, lambda i, be, nu: (layer, be[i], 0, 0))
    rows = MOE_BLK * TILE_ROWS
    grid_spec = pltpu.PrefetchScalarGridSpec(
        num_scalar_prefetch=2,
        grid=(n_blocks,),
        in_specs=[
            pl.BlockSpec(memory_space=pl.ANY),
            wspec((D_MODEL, EXPERT_DIM)), wspec((D_MODEL, EXPERT_DIM)), wspec((EXPERT_DIM, D_MODEL)),
        ],
        out_specs=pl.BlockSpec(memory_space=pl.ANY),
        scratch_shapes=[
            pltpu.VMEM((3, rows, LANES), F32),
            pltpu.VMEM((2, rows, LANES), F32),
            pltpu.VMEM((D_MODEL, EXPERT_DIM), BF16),
            pltpu.VMEM((D_MODEL, EXPERT_DIM), BF16),
            pltpu.VMEM((EXPERT_DIM, D_MODEL), BF16),
            pltpu.SemaphoreType.DMA((3,)),
            pltpu.SemaphoreType.DMA((2,)),
        ],
    )
    return pl.pallas_call(
        functools.partial(_experts_kernel, n_blocks=n_blocks),
        grid_spec=grid_spec,
        out_shape=jax.ShapeDtypeStruct((n_blocks * rows, LANES), F32),
        compiler_params=pltpu.CompilerParams(dimension_semantics=("arbitrary",)),
        name="experts",
    )(block_e, n_used, x_sorted, wg, wu, wd)


def _combine_kernel(np_ref, ls_ref, wt_ref, cur_ref, nx1_ref, nx2_ref, y_hbm, h2b_ref, xn_ref, mod_ref,
                    g_ref, swg_ref, swu_ref, swd_ref, o_ref, stg, acc, sem, *, n_steps):
    i = pl.program_id(0)
    slot = i % 3
    d = D_MODEL

    def piece_copy(s, p, src):
        return pltpu.make_async_copy(y_hbm.at[_piece(src), :], stg.at[s, _piece(p), :], sem.at[s])

    def fetcher(s, pd_ref):
        return lambda p, prio: piece_copy(s, p, pd_ref[0, p]).start(priority=prio)

    @pl.when(i == 0)
    def _():
        _start_pieces(0, np_ref[0], fetcher(0, cur_ref))
        _start_pieces(0, np_ref[1], fetcher(1, nx1_ref))

    def wait(p, carry):
        piece_copy(slot, 0, 0).wait()
        return carry
    lax.fori_loop(0, np_ref[i], wait, 0)

    def gather_token(t):
        a = None
        for k in range(TOP_K):
            v = stg[slot, _tile(ls_ref[0, k * ROW_BLK + t]), :] * wt_ref[0, k * ROW_BLK + t]
            a = v if a is None else a + v
        acc[_tile(t), :] = a

    np_next = jnp.where(i + 2 < n_steps, np_ref[jnp.minimum(i + 2, n_steps - 1)], 0)
    _interleave(gather_token, np_next, fetcher((i + 2) % 3, nx2_ref))

    hb = h2b_ref[...]
    g = _dot(hb, swg_ref[...])
    u = _dot(hb, swu_ref[...])
    f = _dot((g * _sigmoid(g) * u).astype(BF16), swd_ref[...]) + _from_tiles(acc, ROW_BLK)
    o_ref[...] = xn_ref[...] + mod_ref[:, 5 * d:6 * d] * _rms(f, g_ref[...])


def _combine(npieces, lslot, wts, piece_src, y_sorted, h2b_rows, xn_rows, mod3, g, swg, swu, swd,
             n_steps, per_batch):
    assert n_steps >= 3
    last = n_steps - 1
    full = lambda shape: pl.BlockSpec(shape, lambda n, *_: (0,) * len(shape))
    rows = lambda w_: pl.BlockSpec((ROW_BLK, w_), lambda n, *_: (n, 0))
    smem = lambda w_, imap: pl.BlockSpec((None, 1, w_), imap, memory_space=pltpu.SMEM)
    n_lat = SEQ // ROW_BLK
    grid_spec = pltpu.PrefetchScalarGridSpec(
        num_scalar_prefetch=1,
        grid=(n_steps,),
        in_specs=[
            smem(TOP_K * ROW_BLK, lambda n, *_: (n, 0, 0)),
            smem(TOP_K * ROW_BLK, lambda n, *_: (n, 0, 0)),
            smem(PIECES_MAX, lambda n, *_: (n, 0, 0)),
            smem(PIECES_MAX, lambda n, *_: (jnp.minimum(n + 1, last), 0, 0)),
            smem(PIECES_MAX, lambda n, *_: (jnp.minimum(n + 2, last), 0, 0)),
            pl.BlockSpec(memory_space=pl.ANY),
            rows(D_MODEL), rows(D_MODEL),
            pl.BlockSpec((None, 1, 6 * D_MODEL),
                         lambda n, *_: (_mod_row(n % per_batch, n // per_batch, n_lat), 0, 0)),
            full((1, D_MODEL)),
            full((D_MODEL, SHARED_DIM)), full((D_MODEL, SHARED_DIM)), full((SHARED_DIM, D_MODEL)),
        ],
        out_specs=rows(D_MODEL),
        scratch_shapes=[
            pltpu.VMEM((3, STG_ROWS, LANES), F32),
            pltpu.VMEM((ROW_BLK * TILE_ROWS, LANES), F32),
            pltpu.SemaphoreType.DMA((3,)),
        ],
    )
    return pl.pallas_call(
        functools.partial(_combine_kernel, n_steps=n_steps),
        grid_spec=grid_spec,
        out_shape=jax.ShapeDtypeStruct(xn_rows.shape, F32),
        compiler_params=pltpu.CompilerParams(dimension_semantics=("arbitrary",),
                                             vmem_limit_bytes=VMEM_LIMIT),
        name="combine",
    )(npieces, lslot, wts, piece_src, piece_src, piece_src, y_sorted, h2b_rows, xn_rows, mod3, g,
      swg, swu, swd)


def _moe_plan(cnt_rows, n_blocks):
    i32 = jnp.int32
    per_blk = MOE_BLK // PIECE
    to_expert = lambda v: v.reshape(v.shape[:-1] + (8, 8)).swapaxes(-1, -2).reshape(v.shape)
    runs = (cnt_rows + PIECE - 1) // PIECE
    loc_end = jnp.cumsum(runs, axis=1)
    loc = loc_end - runs
    npieces = loc_end[:, -1].astype(i32)
    reg = to_expert(runs.sum(axis=0))
    padded = (reg + per_blk - 1) // per_blk * per_blk
    pad_end = jnp.cumsum(padded)
    pad_start = pad_end - padded
    off = to_expert(pad_start)[None, :] + jnp.cumsum(runs, axis=0) - runs
    p = jnp.arange(PIECES_MAX, dtype=i32)[None, :, None]
    mine = (loc[:, None, :] <= p) & (p < loc_end[:, None, :])
    piece_dst = jnp.sum(jnp.where(mine, (off - loc)[:, None, :], 0), axis=-1) + p[:, :, 0]
    tail_n = padded - reg
    t_end = jnp.cumsum(tail_n)
    t_beg = t_end - tail_n
    q = jnp.arange(TAIL_MAX, dtype=i32)[:, None]
    tmine = (t_beg[None, :] <= q) & (q < t_end[None, :])
    tail_dst = jnp.sum(jnp.where(tmine, (pad_start + reg - t_beg)[None, :], 0), axis=-1) + q[:, 0]
    first = jnp.arange(n_blocks, dtype=i32)[:, None] * per_blk
    block_e = jnp.minimum(jnp.sum((pad_end[None, :] <= first).astype(i32), axis=-1), N_EXPERTS - 1)
    n_used = (pad_end[-1] // per_blk).astype(i32).reshape(1)
    return (npieces, piece_dst.astype(i32).reshape(-1, 1, PIECES_MAX),
            tail_dst.astype(i32).reshape(1, 1, TAIL_MAX), t_end[-1].astype(i32).reshape(1),
            block_e.astype(i32), n_used)


def _axial_tables(dim):
    t = jnp.arange(SEQ)
    row = (t // GRID_W).astype(F32)
    col = (t % GRID_W).astype(F32)
    quarter = dim // 4
    freqs = ROPE_THETA ** (-jnp.arange(quarter, dtype=F32) / quarter)
    ar, ac = row[:, None] * freqs, col[:, None] * freqs
    ang = jnp.concatenate([ar, ar, ac, ac], axis=-1)
    sign = jnp.where((jnp.arange(dim) // quarter) % 2 == 0, -1.0, 1.0).astype(F32)
    cos = jnp.concatenate([jnp.cos(ang), jnp.ones((CTX_LEN, dim), F32)], axis=0)
    sin = jnp.concatenate([jnp.sin(ang) * sign, jnp.zeros((CTX_LEN, dim), F32)], axis=0)
    return cos, sin


def _rope_tables():
    cos_h, sin_h = _axial_tables(HEAD_DIM)
    cos_r, sin_r = _axial_tables(B_ROPE)
    cosh, sinh = jnp.tile(cos_h, (1, 4)), jnp.tile(sin_h, (1, 4))
    scale_b = (B_NOPE + B_ROPE) ** -0.5
    ones, zeros = jnp.ones((TOK, 64), F32), jnp.zeros((TOK, 64), F32)
    pad1, pad0 = jnp.ones((TOK, 32), F32), jnp.zeros((TOK, 32), F32)
    cosb = jnp.tile(jnp.concatenate([ones, cos_r, pad1], axis=1) * scale_b, (1, 4))
    sinb = jnp.tile(jnp.concatenate([zeros, sin_r, pad0], axis=1) * scale_b, (1, 4))
    cosr = jnp.concatenate([cos_r, jnp.ones((TOK, 96), F32)], axis=1)
    sinr = jnp.concatenate([sin_r, jnp.zeros((TOK, 96), F32)], axis=1)
    return cosh, sinh, cosb, sinb, cosr, sinr


_GQA_ORDER = (0, 2, 1, 3)


def _head_cols(base, order, width=HEAD_DIM):
    return np.concatenate([base + np.arange(h * width, (h + 1) * width) for h in order])


def _w_in_layout(w_in):
    zeros = lambda n: jnp.zeros((D_MODEL, n), w_in.dtype)
    a_q = w_in[:, _head_cols(0, _GQA_ORDER)]
    a_kv = w_in[:, 256:512]
    b_q = jnp.concatenate(
        [jnp.concatenate([w_in[:, 512 + h * 96:512 + (h + 1) * 96], zeros(32)], axis=1)
         for h in range(B_HEADS)], axis=1)
    b_c = w_in[:, 896:1024]
    b_r = jnp.concatenate([w_in[:, 1024:1056], zeros(96)], axis=1)
    c_q = w_in[:, _head_cols(1056, _GQA_ORDER)]
    rest = w_in[:, 1312:2336]
    return jnp.concatenate([a_q, a_kv, b_q, b_c, b_r, c_q, rest], axis=1).astype(BF16)


def _w_kv_layout(w_kv_up):
    zeros = jnp.zeros((B_KV_RANK, 64), w_kv_up.dtype)
    nope = [jnp.concatenate([w_kv_up[:, h * 128:h * 128 + 64], zeros], axis=1) for h in range(B_HEADS)]
    val = [w_kv_up[:, h * 128 + 64:(h + 1) * 128] for h in range(B_HEADS)]
    return jnp.concatenate(nope + val, axis=1).astype(BF16)


def _w_out_layout(w_out):
    rows = np.concatenate([_head_cols(0, _GQA_ORDER), np.arange(256, 512),
                           _head_cols(512, _GQA_ORDER), np.arange(768, 1024)])
    return w_out[rows].astype(BF16)


_ROUTER_ROWS = np.array([g * 8 + j for j in range(8) for g in range(8)])


def kernel(x, c, ctx, c_ctx, ada_w, ada_b, pre_mix_g, post_mix_g, pre_ffn_g, post_ffn_g, w_in, a_q_norm, a_k_norm, b_kv_norm, b_w_kv_up, c_sink, d_rpb, w_out, router_w, router_b, exp_w_gate, exp_w_up, exp_w_down, sh_w_gate, sh_w_up, sh_w_down):
    bsz = x.shape[0]
    depth = ada_w.shape[0]
    xall = jnp.concatenate([x, ctx], axis=1)

    cvec = jnp.zeros((16, D_MODEL), F32).at[:bsz].set(c).at[8].set(c_ctx)
    mod = _ada(cvec, ada_w, ada_b)
    tabs = _rope_tables()
    avg = jnp.kron(jnp.eye(4, dtype=F32), jnp.full((64, 64), 1.0 / 64, F32)).astype(BF16)
    row2 = lambda v: v.reshape(1, -1).astype(F32)

    for l in range(depth):
        with_ctx = l < depth - 1
        mod3 = mod[l].reshape(16, 1, 6 * D_MODEL)

        proj = _inproj(
            xall, mod3, row2(pre_mix_g[l]), _w_in_layout(w_in[l]),
            row2(jnp.tile(a_q_norm[l], 4) * HEAD_DIM ** -0.5), row2(jnp.tile(a_k_norm[l], 2)),
            row2(b_kv_norm[l]), _w_kv_layout(b_w_kv_up[l]), avg, tabs)
        qa, ka, va, qb, kb, vb, qc, kc, vc, qd, kd, vd = proj

        nq = NQ + (CTX_LEN // QBLK if with_ctx else 0)
        full = lambda shape: pl.BlockSpec(shape, lambda b, i: (0,) * len(shape))
        mix_a = _attn_call(_attn_a_kernel, "attn_a", qa, ka, va, [], [], nq)
        mix_b = _attn_call(_attn_b_kernel, "attn_b", qb, kb, vb, [], [], nq // 2, qblk=2 * QBLK)
        sink = jnp.zeros((8, LANES), F32).at[:4].set(
            jnp.broadcast_to(c_sink[l][np.array(_GQA_ORDER)][:, None], (4, LANES)))
        mix_c = _attn_call(_attn_c_kernel, "attn_c", qc, kc, vc, [sink], [full((8, LANES))], nq)
        bias = _na_bias(d_rpb[l])
        mix_d = _attn_call(
            _attn_d_kernel, "attn_d", qd, kd, vd, [bias],
            [pl.BlockSpec((None, D_HEADS, QBLK, NA_WIN), lambda b, i: (_na_pattern(i), 0, 0, 0))], nq)

        nb = (TOK if with_ctx else SEQ) // ROW_BLK
        rb = jnp.broadcast_to(router_b[l][_ROUTER_ROWS][:, None], (N_EXPERTS, ROW_BLK)).astype(F32)
        xn, h2b, h2t, wt_t, ls_t, cnt_t = _outproj(
            (mix_a, mix_b, mix_c, mix_d), _w_out_layout(w_out[l]), xall, mod3,
            row2(post_mix_g[l]), row2(pre_ffn_g[l]),
            router_w[l].T[_ROUTER_ROWS].astype(BF16), rb, nb)

        t_used = nb * ROW_BLK
        n_tok = bsz * t_used
        n_steps = bsz * nb
        n_blocks = -(-(n_tok * TOP_K + n_steps * N_EXPERTS * (PIECE - 1) + N_EXPERTS * (MOE_BLK - 1))
                     // MOE_BLK)
        npieces, piece_dst, tail_dst, ntail, block_e, n_used = _moe_plan(
            cnt_t[..., 0].reshape(n_steps, N_EXPERTS), n_blocks)
        lslot = ls_t.reshape(n_steps, 1, TOP_K * ROW_BLK)
        wts = wt_t.reshape(n_steps, 1, TOP_K * ROW_BLK)

        x_sorted = _dispatch(npieces, n_used, ntail, lslot, piece_dst, tail_dst,
                             h2t.reshape(n_tok * TILE_ROWS, LANES), n_steps, n_blocks)
        y_sorted = _experts(block_e, n_used, x_sorted, exp_w_gate, exp_w_up, exp_w_down, l, n_blocks)
        xall = _combine(
            npieces, lslot, wts, piece_dst, y_sorted, h2b.reshape(n_tok, D_MODEL),
            xn.reshape(n_tok, D_MODEL), mod3, row2(post_ffn_g[l]),
            sh_w_gate[l].astype(BF16), sh_w_up[l].astype(BF16), sh_w_down[l].astype(BF16),
            n_steps, nb).reshape(bsz, t_used, D_MODEL)

    return xall[:, :SEQ]
```

```python
import functools

import numpy as np
import jax
import jax.numpy as jnp
from jax import lax
from jax.experimental import pallas as pl
from jax.experimental.pallas import tpu as pltpu

F32 = jnp.float32
BF16 = jnp.bfloat16
U32 = jnp.uint32

D_MODEL = 1024
SEQ = 2048
CTX_LEN = 256
TOK = SEQ + CTX_LEN
GRID_W = 64
HEAD_DIM = 64
ROPE_THETA = 10000.0
EPS = 1e-6
NEG = -1e30

A_HEADS, A_KV_HEADS = 4, 2
B_HEADS, B_NOPE, B_ROPE, B_V, B_KV_RANK = 4, 64, 32, 64, 128
C_HEADS, C_KV_HEADS, C_WINDOW = 4, 2, 128
D_HEADS, NA_ROWS, NA_COLS = 4, 8, 16

N_EXPERTS, TOP_K, N_GROUPS, TOPK_GROUPS = 64, 8, 8, 4
EXPERT_DIM, SHARED_DIM = 256, 256
ROUTED_SCALE = 2.5

LANES = 128
QBLK = 128
ROW_BLK = 256
MOE_BLK = 256
TILE_ROWS = D_MODEL // LANES
PIECE = 8
PIECE_ROWS = PIECE * TILE_ROWS
PIECES_MAX = 320
STG_ROWS = PIECES_MAX * PIECE_ROWS
PK_PIECE_ROWS = PIECE_ROWS // 2
PK_STG_ROWS = STG_ROWS // 2
PK_BLK_ROWS = MOE_BLK * TILE_ROWS // 2
TAIL_MAX = 2048
LOG_CHUNKS = 4
N_CHUNKS = 1 << LOG_CHUNKS
VMEM_LIMIT = 56 * 1024 * 1024
NQ = SEQ // QBLK
NA_WIN = 10 * GRID_W

_AQ, _AK, _AV, _BQ, _BC, _BR = 0, 256, 384, 512, 1024, 1152
_CQ, _CK, _CV, _DQ, _DK, _DV = 1280, 1536, 1664, 1792, 2048, 2304
IN_W = 2560

_NT = (((1,), (1,)), ((), ()))


def _sigmoid(x):
    return 1.0 / (1.0 + jnp.exp(-x))


def _rms(x, g):
    return x * lax.rsqrt(jnp.mean(x * x, axis=-1, keepdims=True) + EPS) * g


def _dot(a, b):
    return jnp.dot(a, b, preferred_element_type=F32)


def _dot_nt(a, b):
    return lax.dot_general(a, b, _NT, preferred_element_type=F32)


def _ada_kernel(c_ref, w_ref, b_ref, o_ref):
    c = c_ref[...]
    a = (c * _sigmoid(c)).astype(BF16)
    o_ref[...] = _dot(a, w_ref[...].astype(BF16)) + b_ref[...]


def _ada(cvec, ada_w, ada_b):
    depth = ada_w.shape[0]
    n = ada_w.shape[2]
    tn = 1536
    return pl.pallas_call(
        _ada_kernel,
        grid=(depth, n // tn),
        in_specs=[
            pl.BlockSpec((16, D_MODEL), lambda l, j: (0, 0)),
            pl.BlockSpec((None, D_MODEL, tn), lambda l, j: (l, 0, j)),
            pl.BlockSpec((None, 1, tn), lambda l, j: (l, 0, j)),
        ],
        out_specs=pl.BlockSpec((None, 16, tn), lambda l, j: (l, 0, j)),
        out_shape=jax.ShapeDtypeStruct((depth, 16, n), F32),
        name="ada",
    )(cvec, ada_w, ada_b.reshape(depth, 1, n))


def _rope(x, cos, sin, chunk):
    outs = []
    for c in range(x.shape[1] // LANES):
        sl = slice(c * LANES, (c + 1) * LANES)
        xs = x[:, sl]
        lane = lax.broadcasted_iota(jnp.int32, xs.shape, 1)
        first = (lane & chunk) == 0
        rot = jnp.where(first, pltpu.roll(xs, LANES - chunk, 1), pltpu.roll(xs, chunk, 1))
        outs.append(xs * cos[:, sl] + rot * sin[:, sl])
    return outs[0] if len(outs) == 1 else jnp.concatenate(outs, axis=1)


def _head_norm(x, avg):
    sq = x * x
    hi = sq.astype(BF16)
    lo = (sq - hi.astype(F32)).astype(BF16)
    ms = _dot(hi, avg) + _dot(lo, avg)
    return x * lax.rsqrt(ms + EPS)


def _inproj_kernel(x_ref, mod_ref, g_ref, w_ref, qg_ref, kg_ref, kvg_ref, wkv_ref, avg_ref,
                   cosh_ref, sinh_ref, cosb_ref, sinb_ref, cosr_ref, sinr_ref,
                   qa_ref, ka_ref, va_ref, qb_ref, kb_ref, vb_ref,
                   qc_ref, kc_ref, vc_ref, qd_ref, kd_ref, vd_ref):
    d = D_MODEL
    h = _rms(x_ref[...], g_ref[...]) * (1.0 + mod_ref[:, d:2 * d]) + mod_ref[:, 0:d]
    p = _dot(h.astype(BF16), w_ref[...])
    cosh, sinh = cosh_ref[...], sinh_ref[...]
    avg = avg_ref[...]

    qa = _head_norm(p[:, _AQ:_AQ + 256], avg) * qg_ref[...]
    qa_ref[...] = _rope(qa, cosh, sinh, 16).astype(BF16)
    ka = _head_norm(p[:, _AK:_AK + 128], avg[0:128, 0:128]) * kg_ref[...]
    ka_ref[...] = _rope(ka, cosh[:, 0:128], sinh[:, 0:128], 16).astype(BF16)
    ones = jnp.ones((p.shape[0], LANES), BF16)
    va_ref[...] = jnp.concatenate([p[:, _AV:_AV + 128].astype(BF16), ones], axis=1)

    qb_ref[...] = _rope(p[:, _BQ:_BQ + 512], cosb_ref[...], sinb_ref[...], 8).astype(BF16)
    cn = _rms(p[:, _BC:_BC + 128], kvg_ref[...])
    kv = _dot(cn.astype(BF16), wkv_ref[...])
    kr = _rope(p[:, _BR:_BR + 128], cosr_ref[...], sinr_ref[...], 8)
    kr = pltpu.roll(kr, 64, 1)
    kb_ref[...] = jnp.concatenate(
        [kv[:, hh * 128:(hh + 1) * 128] + kr for hh in range(B_HEADS)], axis=1).astype(BF16)
    vb_ref[...] = jnp.concatenate(
        [kv[:, 512:640].astype(BF16), ones, kv[:, 640:768].astype(BF16), ones], axis=1)

    qc_ref[...] = _rope(p[:, _CQ:_CQ + 256] * 0.125, cosh, sinh, 16).astype(BF16)
    kc_ref[...] = _rope(p[:, _CK:_CK + 128], cosh[:, 0:128], sinh[:, 0:128], 16).astype(BF16)
    vc_ref[...] = jnp.concatenate([p[:, _CV:_CV + 128].astype(BF16), ones], axis=1)

    qd_ref[...] = (p[:, _DQ:_DQ + 256] * 0.125).astype(BF16)
    kd_ref[...] = p[:, _DK:_DK + 256].astype(BF16)
    vd_ref[...] = jnp.concatenate(
        [p[:, _DV:_DV + 128].astype(BF16), ones, p[:, _DV + 128:_DV + 256].astype(BF16), ones], axis=1)


def _mod_row(j, b, n_lat_blocks):
    return jnp.where(j < n_lat_blocks, b, 8)


def _inproj(xall, mod3, g, w, qg, kg, kvg, wkv, avg, tabs):
    bsz = xall.shape[0]
    nb = TOK // ROW_BLK
    n_lat = SEQ // ROW_BLK
    full = lambda shape: pl.BlockSpec(shape, lambda b, j: (0,) * len(shape))
    row = lambda w_: pl.BlockSpec((ROW_BLK, w_), lambda b, j: (j, 0))
    out = lambda w_: pl.BlockSpec((None, ROW_BLK, w_), lambda b, j: (b, j, 0))
    widths = (256, 128, 256, 512, 512, 512, 256, 128, 256, 256, 256, 512)
    return pl.pallas_call(
        _inproj_kernel,
        grid=(bsz, nb),
        in_specs=[
            pl.BlockSpec((None, ROW_BLK, D_MODEL), lambda b, j: (b, j, 0)),
            pl.BlockSpec((None, 1, 6 * D_MODEL), lambda b, j: (_mod_row(j, b, n_lat), 0, 0)),
            full((1, D_MODEL)), full((D_MODEL, IN_W)),
            full((1, 256)), full((1, 128)), full((1, 128)), full((128, 768)), full((256, 256)),
            row(256), row(256), row(512), row(512), row(128), row(128),
        ],
        out_specs=[out(w_) for w_ in widths],
        out_shape=[jax.ShapeDtypeStruct((bsz, TOK, w_), BF16) for w_ in widths],
        name="inproj",
    )(xall, mod3, g, w, qg, kg, kvg, wkv, avg, *tabs)


def _softmax_pv(segs, sink=None):
    m = None
    for s, _ in segs:
        ms = jnp.max(s, axis=-1, keepdims=True)
        m = ms if m is None else jnp.maximum(m, ms)
    if sink is not None:
        m = jnp.maximum(m, sink)
    acc = None
    for s, v1 in segs:
        r = _dot(jnp.exp((s - m).astype(BF16)), v1)
        acc = r if acc is None else acc + r
    l = acc[:, LANES:LANES + 1]
    if sink is not None:
        l = l + jnp.exp(sink - m)
    return acc[:, 0:LANES] / l


def _lane_lo(dtype_shape):
    return lax.broadcasted_iota(jnp.int32, dtype_shape, 1) < HEAD_DIM


def _half(qt, hh):
    lo = _lane_lo(qt.shape)
    return jnp.where(lo if hh == 0 else jnp.logical_not(lo), qt, jnp.zeros_like(qt))


def _merge_halves(o0, o1):
    return jnp.where(_lane_lo(o0.shape), o0, o1)


def _stack_heads(q_ref, tiles):
    return jnp.concatenate(
        [_half(q_ref[:, t * LANES:(t + 1) * LANES], hh) for t in tiles for hh in range(2)], axis=0)


def _store_heads(o_ref, o, tiles):
    n = o_ref.shape[0]
    for ti, t in enumerate(tiles):
        o0, o1 = o[2 * ti * n:(2 * ti + 1) * n], o[(2 * ti + 1) * n:(2 * ti + 2) * n]
        o_ref[:, t * LANES:(t + 1) * LANES] = _merge_halves(o0, o1).astype(o_ref.dtype)


def _attn_a_kernel(q_ref, k_ref, v_ref, o_ref, *, n_lat):
    def run(k, v1):
        for t in range(2):
            o = _softmax_pv([(_dot_nt(_stack_heads(q_ref, (t,)), k), v1)])
            _store_heads(o_ref, o, (t,))

    @pl.when(pl.program_id(1) < n_lat)
    def _():
        run(k_ref[...], v_ref[...])

    @pl.when(pl.program_id(1) >= n_lat)
    def _():
        run(k_ref[SEQ:TOK, :], v_ref[SEQ:TOK, :])


def _attn_b_kernel(q_ref, k_ref, v_ref, o_ref, *, n_lat):
    def run(lo_, hi_):
        for t in range(2):
            v1 = v_ref[lo_:hi_, t * 256:(t + 1) * 256]
            o = []
            for hh in range(2):
                hd = 2 * t + hh
                q = q_ref[:, hd * LANES:(hd + 1) * LANES]
                k = k_ref[lo_:hi_, hd * LANES:(hd + 1) * LANES]
                o.append(_softmax_pv([(_dot_nt(q, k), v1)]))
            o_ref[:, t * LANES:(t + 1) * LANES] = _merge_halves(o[0], o[1]).astype(o_ref.dtype)

    @pl.when(pl.program_id(1) < n_lat)
    def _():
        run(0, TOK)

    @pl.when(pl.program_id(1) >= n_lat)
    def _():
        run(SEQ, TOK)


def _attn_c_kernel(q_ref, k_ref, v_ref, sink_ref, o_ref, *, n_lat):
    i = pl.program_id(1)
    qblk = o_ref.shape[0]
    band = 3 * qblk

    def run(latent):
        if latent:
            start = pl.multiple_of(jnp.clip((i - 1) * qblk, 0, SEQ - band), qblk)
            shape = (2 * qblk, band)
            qpos = i * qblk + (lax.broadcasted_iota(jnp.int32, shape, 0) & (qblk - 1))
            kpos = start + lax.broadcasted_iota(jnp.int32, shape, 1)
            valid = jnp.abs(qpos - kpos) <= C_WINDOW
        for t in range(2):
            q = _stack_heads(q_ref, (t,))
            segs = [(_dot_nt(q, k_ref[SEQ:TOK, :]), v_ref[SEQ:TOK, :])]
            if latent:
                s = jnp.where(valid, _dot_nt(q, k_ref[pl.ds(start, band), :]), NEG)
                segs.insert(0, (s, v_ref[pl.ds(start, band), :]))
            sink = jnp.concatenate(
                [jnp.broadcast_to(sink_ref[j:j + 1, 0:1], (qblk, 1)) for j in (2 * t, 2 * t + 1)], axis=0)
            _store_heads(o_ref, _softmax_pv(segs, sink=sink), (t,))

    @pl.when(i < n_lat)
    def _():
        run(True)

    @pl.when(i >= n_lat)
    def _():
        run(False)


def _attn_d_kernel(q_ref, k_ref, v_ref, bias_ref, o_ref, *, n_lat):
    i = pl.program_id(1)
    qblk = o_ref.shape[0]

    def run(latent):
        if latent:
            start = pl.multiple_of(jnp.clip((i - 2) * qblk, 0, SEQ - NA_WIN), qblk)
        for t in range(2):
            ksl, vsl = slice(t * LANES, (t + 1) * LANES), slice(t * 256, (t + 1) * 256)
            q = _stack_heads(q_ref, (t,))
            segs = [(_dot_nt(q, k_ref[SEQ:TOK, ksl]), v_ref[SEQ:TOK, vsl])]
            if latent:
                bias = bias_ref[2 * t:2 * t + 2].reshape(2 * qblk, NA_WIN)
                segs.insert(0, (_dot_nt(q, k_ref[pl.ds(start, NA_WIN), ksl]) + bias,
                                v_ref[pl.ds(start, NA_WIN), vsl]))
            _store_heads(o_ref, _softmax_pv(segs), (t,))

    @pl.when(i < n_lat)
    def _():
        run(True)

    @pl.when(i >= n_lat)
    def _():
        run(False)


def _attn_call(kernel, name, q, k, v, extra, extra_specs, n_blocks, qblk=QBLK):
    bsz = q.shape[0]
    return pl.pallas_call(
        functools.partial(kernel, n_lat=SEQ // qblk),
        grid=(bsz, n_blocks),
        in_specs=[
            pl.BlockSpec((None, qblk, q.shape[2]), lambda b, i: (b, i, 0)),
            pl.BlockSpec((None, TOK, k.shape[2]), lambda b, i: (b, 0, 0)),
            pl.BlockSpec((None, TOK, v.shape[2]), lambda b, i: (b, 0, 0)),
        ] + extra_specs,
        out_specs=pl.BlockSpec((None, qblk, 256), lambda b, i: (b, i, 0)),
        out_shape=jax.ShapeDtypeStruct((bsz, n_blocks * qblk, 256), BF16),
        name=name,
    )(q, k, v, *extra)


def _na_pattern(i):
    return jnp.where(i < 2, i, jnp.where(i < NQ - 2, 2, jnp.minimum(i, NQ - 1) - (NQ - 5)))


def _na_bias(rpb):
    rows = SEQ // GRID_W
    n_dr, n_dc = 2 * NA_ROWS - 1, 2 * NA_COLS - 1
    qc = np.arange(GRID_W)
    dc = np.clip(qc[None, :] - qc[:, None] + NA_COLS - 1, 0, n_dc - 1)
    sel_c = jnp.asarray(np.eye(n_dc, dtype=np.float32)[dc])
    cs = np.clip(qc - NA_COLS // 2, 0, GRID_W - NA_COLS)[:, None]
    col_ok = (qc[None, :] >= cs) & (qc[None, :] < cs + NA_COLS)
    out = []
    for i in (0, 1, 2, NQ - 2, NQ - 1):
        start_row = min(max(2 * i - 4, 0), rows - 10)
        qr = 2 * i + np.arange(2)
        kr = start_row + np.arange(10)
        rs = np.clip(qr - NA_ROWS // 2, 0, rows - NA_ROWS)[:, None]
        row_ok = (kr[None] >= rs) & (kr[None] < rs + NA_ROWS)
        dr = np.clip(kr[None] - qr[:, None] + NA_ROWS - 1, 0, n_dr - 1)
        sel_r = jnp.asarray(np.eye(n_dr, dtype=np.float32)[dr])
        b = jnp.einsum("hrc,abr,qkc->haqbk", rpb.astype(F32), sel_r, sel_c,
                       precision=lax.Precision.HIGHEST)
        valid = row_ok[:, None, :, None] & col_ok[None, :, None, :]
        out.append(jnp.where(valid[None], b, NEG).reshape(D_HEADS, QBLK, NA_WIN))
    return jnp.stack(out)


def _outproj_kernel(ma_ref, mb_ref, mc_ref, md_ref, w_ref, x_ref, mod_ref, g1_ref, g2_ref,
                    wr_ref, rb_ref, xn_ref, h2b_ref, h2t_ref, wt_ref, ls_ref, cnt_ref):
    d = D_MODEL
    mix = jnp.concatenate([ma_ref[...], mb_ref[...], mc_ref[...], md_ref[...]], axis=1)
    y = _rms(_dot(mix, w_ref[...]), g1_ref[...])
    xn = x_ref[...] + mod_ref[:, 2 * d:3 * d] * y
    xn_ref[...] = xn
    h2 = _rms(xn, g2_ref[...]) * (1.0 + mod_ref[:, 4 * d:5 * d]) + mod_ref[:, 3 * d:4 * d]
    n = h2.shape[0]
    h2b_ref[...] = h2.astype(BF16)
    for s in range(TILE_ROWS):
        h2t_ref[pl.ds(s, n, stride=TILE_ROWS), :] = h2[:, s * LANES:(s + 1) * LANES]

    s = _sigmoid(_dot_nt(wr_ref[...], h2.astype(BF16)))
    sb = s + rb_ref[...]
    per = N_EXPERTS // N_GROUPS
    slab = [sb[j * 8:(j + 1) * 8, :] for j in range(per)]
    sraw = [s[j * 8:(j + 1) * 8, :] for j in range(per)]
    giota = lax.broadcasted_iota(jnp.int32, (N_GROUPS, n), 0)
    eid = [giota * per + j for j in range(per)]
    ninf = -jnp.inf

    m1 = functools.reduce(jnp.maximum, slab)
    j1 = functools.reduce(jnp.minimum, [jnp.where(slab[j] == m1, j, per) for j in range(per)])
    m2 = functools.reduce(jnp.maximum, [jnp.where(j1 == j, ninf, slab[j]) for j in range(per)])
    gs = m1 + m2

    gsel = jnp.zeros((N_GROUPS, n), jnp.bool_)
    for _ in range(TOPK_GROUPS):
        m = jnp.max(gs, axis=0, keepdims=True)
        gi = jnp.min(jnp.where(gs == m, giota, N_GROUPS), axis=0, keepdims=True)
        hit = giota == gi
        gsel = jnp.logical_or(gsel, hit)
        gs = jnp.where(hit, ninf, gs)
    slab = [jnp.where(gsel, sj, ninf) for sj in slab]

    def pick(hits, vals):
        return jnp.sum(functools.reduce(jnp.add, [jnp.where(hits[j], vals[j], 0.0) for j in range(per)]),
                       axis=0, keepdims=True)

    kiota = lax.broadcasted_iota(jnp.int32, (TOP_K, n), 0)
    wts = jnp.zeros((TOP_K, n), F32)
    chosen = []
    for k in range(TOP_K):
        m = jnp.max(functools.reduce(jnp.maximum, slab), axis=0, keepdims=True)
        cand = functools.reduce(jnp.minimum,
                                [jnp.where(slab[j] == m, eid[j], N_EXPERTS) for j in range(per)])
        ei = jnp.min(cand, axis=0, keepdims=True)
        hits = [eid[j] == ei for j in range(per)]
        slab = [jnp.where(hits[j], ninf, slab[j]) for j in range(per)]
        wts = jnp.where(kiota == k, pick(hits, sraw), wts)
        chosen.append(ei)
    wt_ref[...] = wts / jnp.sum(wts, axis=0, keepdims=True) * ROUTED_SCALE

    hits_k = [[eid[j] == chosen[k] for j in range(per)] for k in range(TOP_K)]
    sel = jnp.concatenate(
        [functools.reduce(jnp.add, [jnp.where(hits_k[k][j], 1.0, 0.0) for k in range(TOP_K)])
         for j in range(per)], axis=0)
    tri = jnp.where(lax.broadcasted_iota(jnp.int32, (n, n), 0) < lax.broadcasted_iota(jnp.int32, (n, n), 1),
                    1.0, 0.0).astype(BF16)
    before = _dot(sel.astype(BF16), tri)
    cnt = jnp.sum(sel, axis=1, keepdims=True)
    run = jnp.floor((cnt + (PIECE - 1.0)) * (1.0 / PIECE)) * PIECE
    low = jnp.where(lax.broadcasted_iota(jnp.int32, (N_EXPERTS, N_EXPERTS), 1)
                    < lax.broadcasted_iota(jnp.int32, (N_EXPERTS, N_EXPERTS), 0), 1.0, 0.0).astype(BF16)
    start = _dot(low, jnp.broadcast_to(run, (N_EXPERTS, LANES)).astype(BF16))[:, 0:1]
    base = before + start
    base = [base[j * 8:(j + 1) * 8, :] for j in range(per)]
    lsl = jnp.zeros((TOP_K, n), F32)
    for k in range(TOP_K):
        lsl = jnp.where(kiota == k, pick(hits_k[k], base), lsl)
    ls_ref[...] = lsl.astype(jnp.int32)
    cnt_ref[...] = jnp.broadcast_to(cnt, (N_EXPERTS, LANES)).astype(jnp.int32)


def _outproj(mixes, w, xall, mod3, g1, g2, wr, rb, nb):
    bsz = xall.shape[0]
    n_lat = SEQ // ROW_BLK
    full = lambda shape: pl.BlockSpec(shape, lambda b, j: (0,) * len(shape))
    rows = lambda w_: pl.BlockSpec((None, ROW_BLK, w_), lambda b, j: (b, j, 0))
    route = pl.BlockSpec((None, None, TOP_K, ROW_BLK), lambda b, j: (b, j, 0, 0))
    return pl.pallas_call(
        _outproj_kernel,
        grid=(bsz, nb),
        in_specs=[rows(256)] * 4 + [
            full((D_MODEL, D_MODEL)), rows(D_MODEL),
            pl.BlockSpec((None, 1, 6 * D_MODEL), lambda b, j: (_mod_row(j, b, n_lat), 0, 0)),
            full((1, D_MODEL)), full((1, D_MODEL)),
            full((N_EXPERTS, D_MODEL)), full((N_EXPERTS, ROW_BLK)),
        ],
        out_specs=[
            rows(D_MODEL), rows(D_MODEL),
            pl.BlockSpec((None, ROW_BLK * TILE_ROWS, LANES), lambda b, j: (b, j, 0)),
            route, route,
            pl.BlockSpec((None, None, N_EXPERTS, LANES), lambda b, j: (b, j, 0, 0)),
        ],
        out_shape=[
            jax.ShapeDtypeStruct((bsz, nb * ROW_BLK, D_MODEL), F32),
            jax.ShapeDtypeStruct((bsz, nb * ROW_BLK, D_MODEL), BF16),
            jax.ShapeDtypeStruct((bsz, nb * ROW_BLK * TILE_ROWS, LANES), F32),
            jax.ShapeDtypeStruct((bsz, nb, TOP_K, ROW_BLK), F32),
            jax.ShapeDtypeStruct((bsz, nb, TOP_K, ROW_BLK), jnp.int32),
            jax.ShapeDtypeStruct((bsz, nb, N_EXPERTS, LANES), jnp.int32),
        ],
        name="outproj",
    )(*mixes, w, xall, mod3, g1, g2, wr, rb)


def _rows(i, n):
    return pl.ds(pl.multiple_of(i * n, n), n)


def _tile_at(row):
    return pl.ds(pl.multiple_of(row, TILE_ROWS), TILE_ROWS)


def _piece(p):
    return _rows(p, PK_PIECE_ROWS)


def _pack_pairs(lo, hi):
    lo_bits = pltpu.bitcast(lo.astype(BF16).astype(F32), U32)
    hi_bits = pltpu.bitcast(hi.astype(BF16).astype(F32), U32)
    return lax.shift_right_logical(lo_bits, jnp.uint32(16)) | (hi_bits & jnp.uint32(0xFFFF0000))


def _unpack_lo(u):
    return pltpu.bitcast(lax.shift_left(u, jnp.uint32(16)), F32)


def _unpack_hi(u):
    return pltpu.bitcast(u & jnp.uint32(0xFFFF0000), F32)


def _start_pieces(lo, hi, start):
    n = hi - lo

    def pair(q, carry):
        start(lo + 2 * q, 0)
        start(lo + 2 * q + 1, 1)
        return carry
    lax.fori_loop(0, lax.shift_right_logical(n, 1), pair, 0)

    @pl.when((n & 1) == 1)
    def _():
        start(hi - 1, 0)


def _interleave(body, n_pieces, start):
    per = ROW_BLK // N_CHUNKS

    def chunk(c, carry):
        for tt in range(per):
            body(c * per + tt)
        _start_pieces(lax.shift_right_logical(c * n_pieces, LOG_CHUNKS),
                      lax.shift_right_logical((c + 1) * n_pieces, LOG_CHUNKS), start)
        return carry
    lax.fori_loop(0, N_CHUNKS, chunk, 0)


def _dispatch_kernel(np_ref, nu_ref, nt_ref, ls_ref, pd_ref, pdp_ref, td_ref, h_ref, xs_hbm,
                     stg, pk, zblk, sem, zsem, *, n_steps, n_blocks):
    i = pl.program_id(0)
    slot = i % 3
    prev = (i + 2) % 3

    def piece_copy(s, p, d):
        return pltpu.make_async_copy(pk.at[s, _piece(p), :], xs_hbm.at[_piece(d), :], sem.at[s])

    def wait_pieces(s, n):
        def body(p, carry):
            piece_copy(s, 0, 0).wait()
            return carry
        lax.fori_loop(0, n, body, 0)

    def zero_piece_copy(d):
        return pltpu.make_async_copy(zblk.at[pl.ds(0, PK_PIECE_ROWS), :], xs_hbm.at[_piece(d), :], zsem.at[0])

    def zero_block_copy(b):
        return pltpu.make_async_copy(zblk, xs_hbm.at[_rows(b, PK_BLK_ROWS), :], zsem.at[1])

    @pl.when(i == 0)
    def _():
        zblk[...] = jnp.zeros_like(zblk)

        def zero(c, carry):
            stg[_rows(c, 2048), :] = jnp.zeros((2048, LANES), F32)
            return carry
        lax.fori_loop(0, STG_ROWS // 2048, zero, 0)

    @pl.when(i >= 3)
    def _():
        wait_pieces(slot, np_ref[jnp.maximum(i - 3, 0)])

    def scatter_token(t):
        tile = h_ref[_rows(t, TILE_ROWS), :]
        for k in range(TOP_K):
            stg[_tile_at(ls_ref[0, k * ROW_BLK + t]), :] = tile

    np_prev = jnp.where(i >= 1, np_ref[jnp.maximum(i - 1, 0)], 0)
    _interleave(scatter_token, np_prev,
                lambda p, prio: piece_copy(prev, p, pdp_ref[0, p]).start(priority=prio))

    def pack_piece(p, carry):
        v = stg[_rows(p, PIECE_ROWS), :].reshape(PIECE // 2, 2 * TILE_ROWS, LANES)
        packed = _pack_pairs(v[:, 0:TILE_ROWS, :], v[:, TILE_ROWS:2 * TILE_ROWS, :])
        pk[slot, _piece(p), :] = packed.reshape(PK_PIECE_ROWS, LANES)
        return carry
    lax.fori_loop(0, np_ref[i], pack_piece, 0)

    @pl.when(i == n_steps - 1)
    def _():
        _start_pieces(0, np_ref[i], lambda p, prio: piece_copy(slot, p, pd_ref[0, p]).start(priority=prio))

        def tail(p, carry):
            zero_piece_copy(td_ref[0, p]).start()
            return carry
        lax.fori_loop(0, nt_ref[0], tail, 0)

        def blk(b, carry):
            zero_block_copy(b).start()
            return carry
        lax.fori_loop(nu_ref[0], n_blocks, blk, 0)

        wait_pieces((i + 1) % 3, np_ref[jnp.maximum(i - 2, 0)])
        wait_pieces(prev, np_ref[jnp.maximum(i - 1, 0)])
        wait_pieces(slot, np_ref[i])

        def tail_wait(p, carry):
            zero_piece_copy(0).wait()
            return carry
        lax.fori_loop(0, nt_ref[0], tail_wait, 0)

        def blk_wait(b, carry):
            zero_block_copy(0).wait()
            return carry
        lax.fori_loop(nu_ref[0], n_blocks, blk_wait, 0)


def _dispatch(npieces, n_used, ntail, lslot, piece_dst, tail_dst, h2t, n_steps, n_blocks):
    assert n_steps >= 3
    smem = lambda n, imap: pl.BlockSpec((None, 1, n), imap, memory_space=pltpu.SMEM)
    grid_spec = pltpu.PrefetchScalarGridSpec(
        num_scalar_prefetch=3,
        grid=(n_steps,),
        in_specs=[
            smem(TOP_K * ROW_BLK, lambda i, *_: (i, 0, 0)),
            smem(PIECES_MAX, lambda i, *_: (i, 0, 0)),
            smem(PIECES_MAX, lambda i, *_: (jnp.maximum(i - 1, 0), 0, 0)),
            smem(TAIL_MAX, lambda i, *_: (0, 0, 0)),
            pl.BlockSpec((ROW_BLK * TILE_ROWS, LANES), lambda i, *_: (i, 0)),
        ],
        out_specs=pl.BlockSpec(memory_space=pl.ANY),
        scratch_shapes=[
            pltpu.VMEM((STG_ROWS, LANES), F32),
            pltpu.VMEM((3, PK_STG_ROWS, LANES), U32),
            pltpu.VMEM((PK_BLK_ROWS, LANES), U32),
            pltpu.SemaphoreType.DMA((3,)),
            pltpu.SemaphoreType.DMA((2,)),
        ],
    )
    return pl.pallas_call(
        functools.partial(_dispatch_kernel, n_steps=n_steps, n_blocks=n_blocks),
        grid_spec=grid_spec,
        out_shape=jax.ShapeDtypeStruct((n_blocks * PK_BLK_ROWS, LANES), U32),
        compiler_params=pltpu.CompilerParams(dimension_semantics=("arbitrary",),
                                             vmem_limit_bytes=VMEM_LIMIT),
        name="dispatch",
    )(npieces, n_used, ntail, lslot, piece_dst, piece_dst, tail_dst, h2t)


def _from_tiles(ref, n):
    return jnp.concatenate([ref[pl.ds(s, n, stride=TILE_ROWS), :] for s in range(TILE_ROWS)], axis=1)


def _experts_kernel(be_ref, nu_ref, xs_hbm, wg_ref, wu_ref, wd_ref, ys_hbm,
                    xbuf, ybuf, wgb, wub, wdb, xsem, ysem, *, n_blocks):
    i = pl.program_id(0)
    n_used = nu_ref[0]
    rows = PK_BLK_ROWS
    half = rows // 2
    pairs = MOE_BLK // 2

    def halves(b):
        return [pl.ds(pl.multiple_of(b * rows + h * half, half), half) for h in range(2)]

    def x_copies(b, s):
        return [pltpu.make_async_copy(xs_hbm.at[src, :], xbuf.at[s, pl.ds(h * half, half), :], xsem.at[s])
                for h, src in enumerate(halves(b))]

    def y_copies(b, s):
        return [pltpu.make_async_copy(ybuf.at[s, pl.ds(h * half, half), :], ys_hbm.at[dst, :], ysem.at[s])
                for h, dst in enumerate(halves(b))]

    def start(copies):
        for h, c in enumerate(copies):
            c.start(priority=h)

    def wait(copies):
        for c in copies:
            c.wait()

    @pl.when(i == 0)
    def _():
        start(x_copies(0, 0))

        @pl.when(n_used > 1)
        def _():
            start(x_copies(1, 1))

    @pl.when(i + 2 < n_used)
    def _():
        start(x_copies(i + 2, (i + 2) % 3))

    @pl.when(i < n_used)
    def _():
        wait(x_copies(i, i % 3))

        @pl.when(i >= 2)
        def _():
            wait(y_copies(i - 2, i % 2))

        @pl.when(jnp.logical_or(i == 0, be_ref[i] != be_ref[jnp.maximum(i - 1, 0)]))
        def _():
            wgb[...] = wg_ref[...].astype(BF16)
            wub[...] = wu_ref[...].astype(BF16)
            wdb[...] = wd_ref[...].astype(BF16)

        x = pltpu.bitcast(_from_tiles(xbuf.at[i % 3], pairs), BF16)
        g = _dot(x, wgb[...])
        u = _dot(x, wub[...])
        y = _dot((g * _sigmoid(g) * u).astype(BF16), wdb[...])
        yp = pltpu.bitcast(y.astype(BF16), U32)
        yb = ybuf.at[i % 2]
        for s in range(TILE_ROWS):
            yb[pl.ds(s, pairs, stride=TILE_ROWS), :] = yp[:, s * LANES:(s + 1) * LANES]
        start(y_copies(i, i % 2))

        @pl.when(i == n_used - 1)
        def _():
            @pl.when(i >= 1)
            def _():
                wait(y_copies(i - 1, (i + 1) % 2))
            wait(y_copies(i, i % 2))
            ybuf[0] = jnp.zeros((rows, LANES), U32)

            def fill(b, carry):
                start(y_copies(b, 0))
                return carry
            lax.fori_loop(n_used, n_blocks, fill, 0)

            def fill_wait(b, carry):
                wait(y_copies(0, 0))
                return carry
            lax.fori_loop(n_used, n_blocks, fill_wait, 0)


def _experts(block_e, n_used, x_sorted, wg, wu, wd, layer, n_blocks):
    wspec = lambda shape: pl.BlockSpec((None, None) + shape, lambda i, be, nu: (layer, be[i], 0, 0))
    rows = PK_BLK_ROWS
    grid_spec = pltpu.PrefetchScalarGridSpec(
        num_scalar_prefetch=2,
        grid=(n_blocks,),
        in_specs=[
            pl.BlockSpec(memory_space=pl.ANY),
            wspec((D_MODEL, EXPERT_DIM)), wspec((D_MODEL, EXPERT_DIM)), wspec((EXPERT_DIM, D_MODEL)),
        ],
        out_specs=pl.BlockSpec(memory_space=pl.ANY),
        scratch_shapes=[
            pltpu.VMEM((3, rows, LANES), U32),
            pltpu.VMEM((2, rows, LANES), U32),
            pltpu.VMEM((D_MODEL, EXPERT_DIM), BF16),
            pltpu.VMEM((D_MODEL, EXPERT_DIM), BF16),
            pltpu.VMEM((EXPERT_DIM, D_MODEL), BF16),
            pltpu.SemaphoreType.DMA((3,)),
            pltpu.SemaphoreType.DMA((2,)),
        ],
    )
    return pl.pallas_call(
        functools.partial(_experts_kernel, n_blocks=n_blocks),
        grid_spec=grid_spec,
        out_shape=jax.ShapeDtypeStruct((n_blocks * rows, LANES), U32),
        compiler_params=pltpu.CompilerParams(dimension_semantics=("arbitrary",)),
        name="experts",
    )(block_e, n_used, x_sorted, wg, wu, wd)


def _combine_kernel(np_ref, lp_ref, wl_ref, wh_ref, cur_ref, nx1_ref, nx2_ref, y_hbm, h2b_ref, xn_ref,
                    mod_ref, g_ref, swg_ref, swu_ref, swd_ref, o_ref, stg, acc, sem, *, n_steps):
    i = pl.program_id(0)
    slot = i % 3
    d = D_MODEL

    def piece_copy(s, p, src):
        return pltpu.make_async_copy(y_hbm.at[_piece(src), :], stg.at[s, _piece(p), :], sem.at[s])

    def fetcher(s, pd_ref):
        return lambda p, prio: piece_copy(s, p, pd_ref[0, p]).start(priority=prio)

    @pl.when(i == 0)
    def _():
        _start_pieces(0, np_ref[0], fetcher(0, cur_ref))
        _start_pieces(0, np_ref[1], fetcher(1, nx1_ref))

    def wait(p, carry):
        piece_copy(slot, 0, 0).wait()
        return carry
    lax.fori_loop(0, np_ref[i], wait, 0)

    def gather_token(t):
        a = None
        for k in range(TOP_K):
            j = k * ROW_BLK + t
            pair = stg[slot, _tile_at(lp_ref[0, j]), :]
            v = _unpack_lo(pair) * wl_ref[0, j] + _unpack_hi(pair) * wh_ref[0, j]
            a = v if a is None else a + v
        acc[_rows(t, TILE_ROWS), :] = a

    np_next = jnp.where(i + 2 < n_steps, np_ref[jnp.minimum(i + 2, n_steps - 1)], 0)
    _interleave(gather_token, np_next, fetcher((i + 2) % 3, nx2_ref))

    hb = h2b_ref[...]
    g = _dot(hb, swg_ref[...])
    u = _dot(hb, swu_ref[...])
    f = _dot((g * _sigmoid(g) * u).astype(BF16), swd_ref[...]) + _from_tiles(acc, ROW_BLK)
    o_ref[...] = xn_ref[...] + mod_ref[:, 5 * d:6 * d] * _rms(f, g_ref[...])


def _combine(npieces, pair_row, w_lo, w_hi, piece_src, y_sorted, h2b_rows, xn_rows, mod3, g, swg, swu, swd,
             n_steps, per_batch):
    assert n_steps >= 3
    last = n_steps - 1
    full = lambda shape: pl.BlockSpec(shape, lambda n, *_: (0,) * len(shape))
    rows = lambda w_: pl.BlockSpec((ROW_BLK, w_), lambda n, *_: (n, 0))
    smem = lambda w_, imap: pl.BlockSpec((None, 1, w_), imap, memory_space=pltpu.SMEM)
    n_lat = SEQ // ROW_BLK
    grid_spec = pltpu.PrefetchScalarGridSpec(
        num_scalar_prefetch=1,
        grid=(n_steps,),
        in_specs=[
            smem(TOP_K * ROW_BLK, lambda n, *_: (n, 0, 0)),
            smem(TOP_K * ROW_BLK, lambda n, *_: (n, 0, 0)),
            smem(TOP_K * ROW_BLK, lambda n, *_: (n, 0, 0)),
            smem(PIECES_MAX, lambda n, *_: (n, 0, 0)),
            smem(PIECES_MAX, lambda n, *_: (jnp.minimum(n + 1, last), 0, 0)),
            smem(PIECES_MAX, lambda n, *_: (jnp.minimum(n + 2, last), 0, 0)),
            pl.BlockSpec(memory_space=pl.ANY),
            rows(D_MODEL), rows(D_MODEL),
            pl.BlockSpec((None, 1, 6 * D_MODEL),
                         lambda n, *_: (_mod_row(n % per_batch, n // per_batch, n_lat), 0, 0)),
            full((1, D_MODEL)),
            full((D_MODEL, SHARED_DIM)), full((D_MODEL, SHARED_DIM)), full((SHARED_DIM, D_MODEL)),
        ],
        out_specs=rows(D_MODEL),
        scratch_shapes=[
            pltpu.VMEM((3, PK_STG_ROWS, LANES), U32),
            pltpu.VMEM((ROW_BLK * TILE_ROWS, LANES), F32),
            pltpu.SemaphoreType.DMA((3,)),
        ],
    )
    return pl.pallas_call(
        functools.partial(_combine_kernel, n_steps=n_steps),
        grid_spec=grid_spec,
        out_shape=jax.ShapeDtypeStruct(xn_rows.shape, F32),
        compiler_params=pltpu.CompilerParams(dimension_semantics=("arbitrary",),
                                             vmem_limit_bytes=VMEM_LIMIT),
        name="combine",
    )(npieces, pair_row, w_lo, w_hi, piece_src, piece_src, piece_src, y_sorted, h2b_rows, xn_rows, mod3, g,
      swg, swu, swd)


def _moe_plan(cnt_rows, n_blocks):
    i32 = jnp.int32
    per_blk = MOE_BLK // PIECE
    to_expert = lambda v: v.reshape(v.shape[:-1] + (8, 8)).swapaxes(-1, -2).reshape(v.shape)
    runs = (cnt_rows + PIECE - 1) // PIECE
    loc_end = jnp.cumsum(runs, axis=1)
    loc = loc_end - runs
    npieces = loc_end[:, -1].astype(i32)
    reg = to_expert(runs.sum(axis=0))
    padded = (reg + per_blk - 1) // per_blk * per_blk
    pad_end = jnp.cumsum(padded)
    pad_start = pad_end - padded
    off = to_expert(pad_start)[None, :] + jnp.cumsum(runs, axis=0) - runs
    p = jnp.arange(PIECES_MAX, dtype=i32)[None, :, None]
    mine = (loc[:, None, :] <= p) & (p < loc_end[:, None, :])
    piece_dst = jnp.sum(jnp.where(mine, (off - loc)[:, None, :], 0), axis=-1) + p[:, :, 0]
    tail_n = padded - reg
    t_end = jnp.cumsum(tail_n)
    t_beg = t_end - tail_n
    q = jnp.arange(TAIL_MAX, dtype=i32)[:, None]
    tmine = (t_beg[None, :] <= q) & (q < t_end[None, :])
    tail_dst = jnp.sum(jnp.where(tmine, (pad_start + reg - t_beg)[None, :], 0), axis=-1) + q[:, 0]
    first = jnp.arange(n_blocks, dtype=i32)[:, None] * per_blk
    block_e = jnp.minimum(jnp.sum((pad_end[None, :] <= first).astype(i32), axis=-1), N_EXPERTS - 1)
    n_used = (pad_end[-1] // per_blk).astype(i32).reshape(1)
    return (npieces, piece_dst.astype(i32).reshape(-1, 1, PIECES_MAX),
            tail_dst.astype(i32).reshape(1, 1, TAIL_MAX), t_end[-1].astype(i32).reshape(1),
            block_e.astype(i32), n_used)


def _axial_tables(dim):
    t = jnp.arange(SEQ)
    row = (t // GRID_W).astype(F32)
    col = (t % GRID_W).astype(F32)
    quarter = dim // 4
    freqs = ROPE_THETA ** (-jnp.arange(quarter, dtype=F32) / quarter)
    ar, ac = row[:, None] * freqs, col[:, None] * freqs
    ang = jnp.concatenate([ar, ar, ac, ac], axis=-1)
    sign = jnp.where((jnp.arange(dim) // quarter) % 2 == 0, -1.0, 1.0).astype(F32)
    cos = jnp.concatenate([jnp.cos(ang), jnp.ones((CTX_LEN, dim), F32)], axis=0)
    sin = jnp.concatenate([jnp.sin(ang) * sign, jnp.zeros((CTX_LEN, dim), F32)], axis=0)
    return cos, sin


def _rope_tables():
    cos_h, sin_h = _axial_tables(HEAD_DIM)
    cos_r, sin_r = _axial_tables(B_ROPE)
    cosh, sinh = jnp.tile(cos_h, (1, 4)), jnp.tile(sin_h, (1, 4))
    scale_b = (B_NOPE + B_ROPE) ** -0.5
    ones, zeros = jnp.ones((TOK, 64), F32), jnp.zeros((TOK, 64), F32)
    pad1, pad0 = jnp.ones((TOK, 32), F32), jnp.zeros((TOK, 32), F32)
    cosb = jnp.tile(jnp.concatenate([ones, cos_r, pad1], axis=1) * scale_b, (1, 4))
    sinb = jnp.tile(jnp.concatenate([zeros, sin_r, pad0], axis=1) * scale_b, (1, 4))
    cosr = jnp.concatenate([cos_r, jnp.ones((TOK, 96), F32)], axis=1)
    sinr = jnp.concatenate([sin_r, jnp.zeros((TOK, 96), F32)], axis=1)
    return cosh, sinh, cosb, sinb, cosr, sinr


_GQA_ORDER = (0, 2, 1, 3)


def _head_cols(base, order, width=HEAD_DIM):
    return np.concatenate([base + np.arange(h * width, (h + 1) * width) for h in order])


def _w_in_layout(w_in):
    zeros = lambda n: jnp.zeros((D_MODEL, n), w_in.dtype)
    a_q = w_in[:, _head_cols(0, _GQA_ORDER)]
    a_kv = w_in[:, 256:512]
    b_q = jnp.concatenate(
        [jnp.concatenate([w_in[:, 512 + h * 96:512 + (h + 1) * 96], zeros(32)], axis=1)
         for h in range(B_HEADS)], axis=1)
    b_c = w_in[:, 896:1024]
    b_r = jnp.concatenate([w_in[:, 1024:1056], zeros(96)], axis=1)
    c_q = w_in[:, _head_cols(1056, _GQA_ORDER)]
    rest = w_in[:, 1312:2336]
    return jnp.concatenate([a_q, a_kv, b_q, b_c, b_r, c_q, rest], axis=1).astype(BF16)


def _w_kv_layout(w_kv_up):
    zeros = jnp.zeros((B_KV_RANK, 64), w_kv_up.dtype)
    nope = [jnp.concatenate([w_kv_up[:, h * 128:h * 128 + 64], zeros], axis=1) for h in range(B_HEADS)]
    val = [w_kv_up[:, h * 128 + 64:(h + 1) * 128] for h in range(B_HEADS)]
    return jnp.concatenate(nope + val, axis=1).astype(BF16)


def _w_out_layout(w_out):
    rows = np.concatenate([_head_cols(0, _GQA_ORDER), np.arange(256, 512),
                           _head_cols(512, _GQA_ORDER), np.arange(768, 1024)])
    return w_out[rows].astype(BF16)


_ROUTER_ROWS = np.array([g * 8 + j for j in range(8) for g in range(8)])


def kernel(x, c, ctx, c_ctx, ada_w, ada_b, pre_mix_g, post_mix_g, pre_ffn_g, post_ffn_g, w_in, a_q_norm, a_k_norm, b_kv_norm, b_w_kv_up, c_sink, d_rpb, w_out, router_w, router_b, exp_w_gate, exp_w_up, exp_w_down, sh_w_gate, sh_w_up, sh_w_down):
    bsz = x.shape[0]
    depth = ada_w.shape[0]
    xall = jnp.concatenate([x, ctx], axis=1)

    cvec = jnp.zeros((16, D_MODEL), F32).at[:bsz].set(c).at[8].set(c_ctx)
    mod = _ada(cvec, ada_w, ada_b)
    tabs = _rope_tables()
    avg = jnp.kron(jnp.eye(4, dtype=F32), jnp.full((64, 64), 1.0 / 64, F32)).astype(BF16)
    row2 = lambda v: v.reshape(1, -1).astype(F32)

    for l in range(depth):
        with_ctx = l < depth - 1
        mod3 = mod[l].reshape(16, 1, 6 * D_MODEL)

        proj = _inproj(
            xall, mod3, row2(pre_mix_g[l]), _w_in_layout(w_in[l]),
            row2(jnp.tile(a_q_norm[l], 4) * HEAD_DIM ** -0.5), row2(jnp.tile(a_k_norm[l], 2)),
            row2(b_kv_norm[l]), _w_kv_layout(b_w_kv_up[l]), avg, tabs)
        qa, ka, va, qb, kb, vb, qc, kc, vc, qd, kd, vd = proj

        nq = NQ + (CTX_LEN // QBLK if with_ctx else 0)
        full = lambda shape: pl.BlockSpec(shape, lambda b, i: (0,) * len(shape))
        mix_a = _attn_call(_attn_a_kernel, "attn_a", qa, ka, va, [], [], nq)
        mix_b = _attn_call(_attn_b_kernel, "attn_b", qb, kb, vb, [], [], nq // 2, qblk=2 * QBLK)
        sink = jnp.zeros((8, LANES), F32).at[:4].set(
            jnp.broadcast_to(c_sink[l][np.array(_GQA_ORDER)][:, None], (4, LANES)))
        mix_c = _attn_call(_attn_c_kernel, "attn_c", qc, kc, vc, [sink], [full((8, LANES))], nq)
        bias = _na_bias(d_rpb[l])
        mix_d = _attn_call(
            _attn_d_kernel, "attn_d", qd, kd, vd, [bias],
            [pl.BlockSpec((None, D_HEADS, QBLK, NA_WIN), lambda b, i: (_na_pattern(i), 0, 0, 0))], nq)

        nb = (TOK if with_ctx else SEQ) // ROW_BLK
        rb = jnp.broadcast_to(router_b[l][_ROUTER_ROWS][:, None], (N_EXPERTS, ROW_BLK)).astype(F32)
        xn, h2b, h2t, wt_t, ls_t, cnt_t = _outproj(
            (mix_a, mix_b, mix_c, mix_d), _w_out_layout(w_out[l]), xall, mod3,
            row2(post_mix_g[l]), row2(pre_ffn_g[l]),
            router_w[l].T[_ROUTER_ROWS].astype(BF16), rb, nb)

        t_used = nb * ROW_BLK
        n_tok = bsz * t_used
        n_steps = bsz * nb
        n_blocks = -(-(n_tok * TOP_K + n_steps * N_EXPERTS * (PIECE - 1) + N_EXPERTS * (MOE_BLK - 1))
                     // MOE_BLK)
        npieces, piece_dst, tail_dst, ntail, block_e, n_used = _moe_plan(
            cnt_t[..., 0].reshape(n_steps, N_EXPERTS), n_blocks)
        lslot = ls_t.reshape(n_steps, 1, TOP_K * ROW_BLK)
        wts = wt_t.reshape(n_steps, 1, TOP_K * ROW_BLK)
        stg_row = lslot * TILE_ROWS
        pair_row = (lslot // 2) * TILE_ROWS
        w_lo = jnp.where(lslot % 2 == 0, wts, 0.0)
        w_hi = wts - w_lo

        x_sorted = _dispatch(npieces, n_used, ntail, stg_row, piece_dst, tail_dst,
                             h2t.reshape(n_tok * TILE_ROWS, LANES), n_steps, n_blocks)
        y_sorted = _experts(block_e, n_used, x_sorted, exp_w_gate, exp_w_up, exp_w_down, l, n_blocks)
        xall = _combine(
            npieces, pair_row, w_lo, w_hi, piece_dst, y_sorted, h2b.reshape(n_tok, D_MODEL),
            xn.reshape(n_tok, D_MODEL), mod3, row2(post_ffn_g[l]),
            sh_w_gate[l].astype(BF16), sh_w_up[l].astype(BF16), sh_w_down[l].astype(BF16),
            n_steps, nb).reshape(bsz, t_used, D_MODEL)

    return xall[:, :SEQ]
```

```python
import functools

import numpy as np
import jax
import jax.numpy as jnp
from jax import lax
from jax.experimental import pallas as pl
from jax.experimental.pallas import tpu as pltpu

F32 = jnp.float32
BF16 = jnp.bfloat16
U32 = jnp.uint32

D_MODEL = 1024
SEQ = 2048
CTX_LEN = 256
TOK = SEQ + CTX_LEN
GRID_W = 64
HEAD_DIM = 64
ROPE_THETA = 10000.0
EPS = 1e-6
NEG = -1e30

A_HEADS, A_KV_HEADS = 4, 2
B_HEADS, B_NOPE, B_ROPE, B_V, B_KV_RANK = 4, 64, 32, 64, 128
C_HEADS, C_KV_HEADS, C_WINDOW = 4, 2, 128
D_HEADS, NA_ROWS, NA_COLS = 4, 8, 16

N_EXPERTS, TOP_K, N_GROUPS, TOPK_GROUPS = 64, 8, 8, 4
EXPERT_DIM, SHARED_DIM = 256, 256
ROUTED_SCALE = 2.5

LANES = 128
QBLK = 128
ROW_BLK = 256
MOE_BLK = 512
TILE_ROWS = D_MODEL // LANES
PIECE = 8
PIECE_ROWS = PIECE * TILE_ROWS
PIECES_MAX = 320
STG_ROWS = PIECES_MAX * PIECE_ROWS
PK_PIECE_ROWS = PIECE_ROWS // 2
PK_STG_ROWS = STG_ROWS // 2
PK_BLK_ROWS = MOE_BLK * TILE_ROWS // 2
TAIL_MAX = N_EXPERTS * MOE_BLK // PIECE
LOG_WAIT_GROUP = 4
WAIT_GROUP = 1 << LOG_WAIT_GROUP
LOG_CHUNKS = 4
N_CHUNKS = 1 << LOG_CHUNKS
VMEM_LIMIT = 56 * 1024 * 1024
NQ = SEQ // QBLK
NA_WIN = 10 * GRID_W

_AQ, _AK, _AV, _BQ, _BC, _BR = 0, 256, 384, 512, 1024, 1152
_CQ, _CK, _CV, _DQ, _DK, _DV = 1280, 1536, 1664, 1792, 2048, 2304
IN_W = 2560

_NT = (((1,), (1,)), ((), ()))


def _sigmoid(x):
    return 1.0 / (1.0 + jnp.exp(-x))


def _rms(x, g):
    return x * lax.rsqrt(jnp.mean(x * x, axis=-1, keepdims=True) + EPS) * g


def _dot(a, b):
    return jnp.dot(a, b, preferred_element_type=F32)


def _dot_nt(a, b):
    return lax.dot_general(a, b, _NT, preferred_element_type=F32)


def _ada_kernel(c_ref, w_ref, b_ref, o_ref):
    c = c_ref[...]
    a = (c * _sigmoid(c)).astype(BF16)
    o_ref[...] = _dot(a, w_ref[...].astype(BF16)) + b_ref[...]


def _ada(cvec, ada_w, ada_b):
    depth = ada_w.shape[0]
    n = ada_w.shape[2]
    tn = 1536
    return pl.pallas_call(
        _ada_kernel,
        grid=(depth, n // tn),
        in_specs=[
            pl.BlockSpec((16, D_MODEL), lambda l, j: (0, 0)),
            pl.BlockSpec((None, D_MODEL, tn), lambda l, j: (l, 0, j)),
            pl.BlockSpec((None, 1, tn), lambda l, j: (l, 0, j)),
        ],
        out_specs=pl.BlockSpec((None, 16, tn), lambda l, j: (l, 0, j)),
        out_shape=jax.ShapeDtypeStruct((depth, 16, n), F32),
        name="ada",
    )(cvec, ada_w, ada_b.reshape(depth, 1, n))


def _rope(x, cos, sin, chunk):
    outs = []
    for c in range(x.shape[1] // LANES):
        sl = slice(c * LANES, (c + 1) * LANES)
        xs = x[:, sl]
        lane = lax.broadcasted_iota(jnp.int32, xs.shape, 1)
        first = (lane & chunk) == 0
        rot = jnp.where(first, pltpu.roll(xs, LANES - chunk, 1), pltpu.roll(xs, chunk, 1))
        outs.append(xs * cos[:, sl] + rot * sin[:, sl])
    return outs[0] if len(outs) == 1 else jnp.concatenate(outs, axis=1)


def _head_norm(x, avg):
    sq = x * x
    hi = sq.astype(BF16)
    lo = (sq - hi.astype(F32)).astype(BF16)
    ms = _dot(hi, avg) + _dot(lo, avg)
    return x * lax.rsqrt(ms + EPS)


def _inproj_kernel(x_ref, mod_ref, g_ref, w_ref, qg_ref, kg_ref, kvg_ref, wkv_ref, avg_ref,
                   cosh_ref, sinh_ref, cosb_ref, sinb_ref, cosr_ref, sinr_ref,
                   qa_ref, ka_ref, va_ref, qb_ref, kb_ref, vb_ref,
                   qc_ref, kc_ref, vc_ref, qd_ref, kd_ref, vd_ref):
    d = D_MODEL
    h = _rms(x_ref[...], g_ref[...]) * (1.0 + mod_ref[:, d:2 * d]) + mod_ref[:, 0:d]
    p = _dot(h.astype(BF16), w_ref[...])
    cosh, sinh = cosh_ref[...], sinh_ref[...]
    avg = avg_ref[...]

    qa = _head_norm(p[:, _AQ:_AQ + 256], avg) * qg_ref[...]
    qa_ref[...] = _rope(qa, cosh, sinh, 16).astype(BF16)
    ka = _head_norm(p[:, _AK:_AK + 128], avg[0:128, 0:128]) * kg_ref[...]
    ka_ref[...] = _rope(ka, cosh[:, 0:128], sinh[:, 0:128], 16).astype(BF16)
    ones = jnp.ones((p.shape[0], LANES), BF16)
    va_ref[...] = jnp.concatenate([p[:, _AV:_AV + 128].astype(BF16), ones], axis=1)

    qb_ref[...] = _rope(p[:, _BQ:_BQ + 512], cosb_ref[...], sinb_ref[...], 8).astype(BF16)
    cn = _rms(p[:, _BC:_BC + 128], kvg_ref[...])
    kv = _dot(cn.astype(BF16), wkv_ref[...])
    kr = _rope(p[:, _BR:_BR + 128], cosr_ref[...], sinr_ref[...], 8)
    kr = pltpu.roll(kr, 64, 1)
    kb_ref[...] = jnp.concatenate(
        [kv[:, hh * 128:(hh + 1) * 128] + kr for hh in range(B_HEADS)], axis=1).astype(BF16)
    vb_ref[...] = jnp.concatenate(
        [kv[:, 512:640].astype(BF16), ones, kv[:, 640:768].astype(BF16), ones], axis=1)

    qc_ref[...] = _rope(p[:, _CQ:_CQ + 256] * 0.125, cosh, sinh, 16).astype(BF16)
    kc_ref[...] = _rope(p[:, _CK:_CK + 128], cosh[:, 0:128], sinh[:, 0:128], 16).astype(BF16)
    vc_ref[...] = jnp.concatenate([p[:, _CV:_CV + 128].astype(BF16), ones], axis=1)

    qd_ref[...] = (p[:, _DQ:_DQ + 256] * 0.125).astype(BF16)
    kd_ref[...] = p[:, _DK:_DK + 256].astype(BF16)
    vd_ref[...] = jnp.concatenate(
        [p[:, _DV:_DV + 128].astype(BF16), ones, p[:, _DV + 128:_DV + 256].astype(BF16), ones], axis=1)


def _mod_row(j, b, n_lat_blocks):
    return jnp.where(j < n_lat_blocks, b, 8)


def _inproj(xall, mod3, g, w, qg, kg, kvg, wkv, avg, tabs):
    bsz = xall.shape[0]
    nb = TOK // ROW_BLK
    n_lat = SEQ // ROW_BLK
    full = lambda shape: pl.BlockSpec(shape, lambda b, j: (0,) * len(shape))
    row = lambda w_: pl.BlockSpec((ROW_BLK, w_), lambda b, j: (j, 0))
    out = lambda w_: pl.BlockSpec((None, ROW_BLK, w_), lambda b, j: (b, j, 0))
    widths = (256, 128, 256, 512, 512, 512, 256, 128, 256, 256, 256, 512)
    return pl.pallas_call(
        _inproj_kernel,
        grid=(bsz, nb),
        in_specs=[
            pl.BlockSpec((None, ROW_BLK, D_MODEL), lambda b, j: (b, j, 0)),
            pl.BlockSpec((None, 1, 6 * D_MODEL), lambda b, j: (_mod_row(j, b, n_lat), 0, 0)),
            full((1, D_MODEL)), full((D_MODEL, IN_W)),
            full((1, 256)), full((1, 128)), full((1, 128)), full((128, 768)), full((256, 256)),
            row(256), row(256), row(512), row(512), row(128), row(128),
        ],
        out_specs=[out(w_) for w_ in widths],
        out_shape=[jax.ShapeDtypeStruct((bsz, TOK, w_), BF16) for w_ in widths],
        name="inproj",
    )(xall, mod3, g, w, qg, kg, kvg, wkv, avg, *tabs)


def _softmax_pv(segs, sink=None):
    m = None
    for s, _ in segs:
        ms = jnp.max(s, axis=-1, keepdims=True)
        m = ms if m is None else jnp.maximum(m, ms)
    if sink is not None:
        m = jnp.maximum(m, sink)
    acc = None
    for s, v1 in segs:
        r = _dot(jnp.exp((s - m).astype(BF16)), v1)
        acc = r if acc is None else acc + r
    l = acc[:, LANES:LANES + 1]
    if sink is not None:
        l = l + jnp.exp(sink - m)
    return acc[:, 0:LANES] / l


def _lane_lo(dtype_shape):
    return lax.broadcasted_iota(jnp.int32, dtype_shape, 1) < HEAD_DIM


def _half(qt, hh):
    lo = _lane_lo(qt.shape)
    return jnp.where(lo if hh == 0 else jnp.logical_not(lo), qt, jnp.zeros_like(qt))


def _merge_halves(o0, o1):
    return jnp.where(_lane_lo(o0.shape), o0, o1)


def _stack_heads(q_ref, tiles):
    return jnp.concatenate(
        [_half(q_ref[:, t * LANES:(t + 1) * LANES], hh) for t in tiles for hh in range(2)], axis=0)


def _store_heads(o_ref, o, tiles):
    n = o_ref.shape[0]
    for ti, t in enumerate(tiles):
        o0, o1 = o[2 * ti * n:(2 * ti + 1) * n], o[(2 * ti + 1) * n:(2 * ti + 2) * n]
        o_ref[:, t * LANES:(t + 1) * LANES] = _merge_halves(o0, o1).astype(o_ref.dtype)


def _attn_a_kernel(q_ref, k_ref, v_ref, o_ref, *, n_lat):
    def run(k, v1):
        for t in range(2):
            o = _softmax_pv([(_dot_nt(_stack_heads(q_ref, (t,)), k), v1)])
            _store_heads(o_ref, o, (t,))

    @pl.when(pl.program_id(1) < n_lat)
    def _():
        run(k_ref[...], v_ref[...])

    @pl.when(pl.program_id(1) >= n_lat)
    def _():
        run(k_ref[SEQ:TOK, :], v_ref[SEQ:TOK, :])


def _attn_b_kernel(q_ref, k_ref, v_ref, o_ref, *, n_lat):
    def run(lo_, hi_):
        for t in range(2):
            v1 = v_ref[lo_:hi_, t * 256:(t + 1) * 256]
            o = []
            for hh in range(2):
                hd = 2 * t + hh
                q = q_ref[:, hd * LANES:(hd + 1) * LANES]
                k = k_ref[lo_:hi_, hd * LANES:(hd + 1) * LANES]
                o.append(_softmax_pv([(_dot_nt(q, k), v1)]))
            o_ref[:, t * LANES:(t + 1) * LANES] = _merge_halves(o[0], o[1]).astype(o_ref.dtype)

    @pl.when(pl.program_id(1) < n_lat)
    def _():
        run(0, TOK)

    @pl.when(pl.program_id(1) >= n_lat)
    def _():
        run(SEQ, TOK)


def _attn_c_kernel(q_ref, k_ref, v_ref, sink_ref, o_ref, *, n_lat):
    i = pl.program_id(1)
    qblk = o_ref.shape[0]
    band = 3 * qblk

    def run(latent):
        if latent:
            start = pl.multiple_of(jnp.clip((i - 1) * qblk, 0, SEQ - band), qblk)
            shape = (2 * qblk, band)
            qpos = i * qblk + (lax.broadcasted_iota(jnp.int32, shape, 0) & (qblk - 1))
            kpos = start + lax.broadcasted_iota(jnp.int32, shape, 1)
            valid = jnp.abs(qpos - kpos) <= C_WINDOW
        for t in range(2):
            q = _stack_heads(q_ref, (t,))
            segs = [(_dot_nt(q, k_ref[SEQ:TOK, :]), v_ref[SEQ:TOK, :])]
            if latent:
                s = jnp.where(valid, _dot_nt(q, k_ref[pl.ds(start, band), :]), NEG)
                segs.insert(0, (s, v_ref[pl.ds(start, band), :]))
            sink = jnp.concatenate(
                [jnp.broadcast_to(sink_ref[j:j + 1, 0:1], (qblk, 1)) for j in (2 * t, 2 * t + 1)], axis=0)
            _store_heads(o_ref, _softmax_pv(segs, sink=sink), (t,))

    @pl.when(i < n_lat)
    def _():
        run(True)

    @pl.when(i >= n_lat)
    def _():
        run(False)


def _attn_d_kernel(q_ref, k_ref, v_ref, bias_ref, o_ref, *, n_lat):
    i = pl.program_id(1)
    qblk = o_ref.shape[0]

    def run(latent):
        if latent:
            start = pl.multiple_of(jnp.clip((i - 2) * qblk, 0, SEQ - NA_WIN), qblk)
        for t in range(2):
            ksl, vsl = slice(t * LANES, (t + 1) * LANES), slice(t * 256, (t + 1) * 256)
            q = _stack_heads(q_ref, (t,))
            segs = [(_dot_nt(q, k_ref[SEQ:TOK, ksl]), v_ref[SEQ:TOK, vsl])]
            if latent:
                bias = bias_ref[2 * t:2 * t + 2].reshape(2 * qblk, NA_WIN)
                segs.insert(0, (_dot_nt(q, k_ref[pl.ds(start, NA_WIN), ksl]) + bias,
                                v_ref[pl.ds(start, NA_WIN), vsl]))
            _store_heads(o_ref, _softmax_pv(segs), (t,))

    @pl.when(i < n_lat)
    def _():
        run(True)

    @pl.when(i >= n_lat)
    def _():
        run(False)


def _attn_call(kernel, name, q, k, v, extra, extra_specs, n_blocks, qblk=QBLK):
    bsz = q.shape[0]
    return pl.pallas_call(
        functools.partial(kernel, n_lat=SEQ // qblk),
        grid=(bsz, n_blocks),
        in_specs=[
            pl.BlockSpec((None, qblk, q.shape[2]), lambda b, i: (b, i, 0)),
            pl.BlockSpec((None, TOK, k.shape[2]), lambda b, i: (b, 0, 0)),
            pl.BlockSpec((None, TOK, v.shape[2]), lambda b, i: (b, 0, 0)),
        ] + extra_specs,
        out_specs=pl.BlockSpec((None, qblk, 256), lambda b, i: (b, i, 0)),
        out_shape=jax.ShapeDtypeStruct((bsz, n_blocks * qblk, 256), BF16),
        name=name,
    )(q, k, v, *extra)


def _na_pattern(i):
    return jnp.where(i < 2, i, jnp.where(i < NQ - 2, 2, jnp.minimum(i, NQ - 1) - (NQ - 5)))


def _na_bias(rpb):
    rows = SEQ // GRID_W
    n_dr, n_dc = 2 * NA_ROWS - 1, 2 * NA_COLS - 1
    qc = np.arange(GRID_W)
    dc = np.clip(qc[None, :] - qc[:, None] + NA_COLS - 1, 0, n_dc - 1)
    sel_c = jnp.asarray(np.eye(n_dc, dtype=np.float32)[dc])
    cs = np.clip(qc - NA_COLS // 2, 0, GRID_W - NA_COLS)[:, None]
    col_ok = (qc[None, :] >= cs) & (qc[None, :] < cs + NA_COLS)
    out = []
    for i in (0, 1, 2, NQ - 2, NQ - 1):
        start_row = min(max(2 * i - 4, 0), rows - 10)
        qr = 2 * i + np.arange(2)
        kr = start_row + np.arange(10)
        rs = np.clip(qr - NA_ROWS // 2, 0, rows - NA_ROWS)[:, None]
        row_ok = (kr[None] >= rs) & (kr[None] < rs + NA_ROWS)
        dr = np.clip(kr[None] - qr[:, None] + NA_ROWS - 1, 0, n_dr - 1)
        sel_r = jnp.asarray(np.eye(n_dr, dtype=np.float32)[dr])
        b = jnp.einsum("hrc,abr,qkc->haqbk", rpb.astype(F32), sel_r, sel_c,
                       precision=lax.Precision.HIGHEST)
        valid = row_ok[:, None, :, None] & col_ok[None, :, None, :]
        out.append(jnp.where(valid[None], b, NEG).reshape(D_HEADS, QBLK, NA_WIN))
    return jnp.stack(out)


def _outproj_kernel(ma_ref, mb_ref, mc_ref, md_ref, w_ref, x_ref, mod_ref, g1_ref, g2_ref,
                    wr_ref, rb_ref, xn_ref, h2b_ref, h2t_ref, wt_ref, ls_ref, cnt_ref):
    d = D_MODEL
    mix = jnp.concatenate([ma_ref[...], mb_ref[...], mc_ref[...], md_ref[...]], axis=1)
    y = _rms(_dot(mix, w_ref[...]), g1_ref[...])
    xn = x_ref[...] + mod_ref[:, 2 * d:3 * d] * y
    xn_ref[...] = xn
    h2 = _rms(xn, g2_ref[...]) * (1.0 + mod_ref[:, 4 * d:5 * d]) + mod_ref[:, 3 * d:4 * d]
    n = h2.shape[0]
    h2b_ref[...] = h2.astype(BF16)
    for s in range(TILE_ROWS):
        h2t_ref[pl.ds(s, n, stride=TILE_ROWS), :] = h2[:, s * LANES:(s + 1) * LANES]

    s = _sigmoid(_dot_nt(wr_ref[...], h2.astype(BF16)))
    sb = s + rb_ref[...]
    per = N_EXPERTS // N_GROUPS
    slab = [sb[j * 8:(j + 1) * 8, :] for j in range(per)]
    sraw = [s[j * 8:(j + 1) * 8, :] for j in range(per)]
    giota = lax.broadcasted_iota(jnp.int32, (N_GROUPS, n), 0)
    eid = [giota * per + j for j in range(per)]
    ninf = -jnp.inf

    m1 = functools.reduce(jnp.maximum, slab)
    j1 = functools.reduce(jnp.minimum, [jnp.where(slab[j] == m1, j, per) for j in range(per)])
    m2 = functools.reduce(jnp.maximum, [jnp.where(j1 == j, ninf, slab[j]) for j in range(per)])
    gs = m1 + m2

    gsel = jnp.zeros((N_GROUPS, n), jnp.bool_)
    for _ in range(TOPK_GROUPS):
        m = jnp.max(gs, axis=0, keepdims=True)
        gi = jnp.min(jnp.where(gs == m, giota, N_GROUPS), axis=0, keepdims=True)
        hit = giota == gi
        gsel = jnp.logical_or(gsel, hit)
        gs = jnp.where(hit, ninf, gs)
    slab = [jnp.where(gsel, sj, ninf) for sj in slab]

    def pick(hits, vals):
        return jnp.sum(functools.reduce(jnp.add, [jnp.where(hits[j], vals[j], 0.0) for j in range(per)]),
                       axis=0, keepdims=True)

    kiota = lax.broadcasted_iota(jnp.int32, (TOP_K, n), 0)
    wts = jnp.zeros((TOP_K, n), F32)
    chosen = []
    for k in range(TOP_K):
        m = jnp.max(functools.reduce(jnp.maximum, slab), axis=0, keepdims=True)
        cand = functools.reduce(jnp.minimum,
                                [jnp.where(slab[j] == m, eid[j], N_EXPERTS) for j in range(per)])
        ei = jnp.min(cand, axis=0, keepdims=True)
        hits = [eid[j] == ei for j in range(per)]
        slab = [jnp.where(hits[j], ninf, slab[j]) for j in range(per)]
        wts = jnp.where(kiota == k, pick(hits, sraw), wts)
        chosen.append(ei)
    wt_ref[...] = wts / jnp.sum(wts, axis=0, keepdims=True) * ROUTED_SCALE

    hits_k = [[eid[j] == chosen[k] for j in range(per)] for k in range(TOP_K)]
    sel = jnp.concatenate(
        [functools.reduce(jnp.add, [jnp.where(hits_k[k][j], 1.0, 0.0) for k in range(TOP_K)])
         for j in range(per)], axis=0)
    tri = jnp.where(lax.broadcasted_iota(jnp.int32, (n, n), 0) < lax.broadcasted_iota(jnp.int32, (n, n), 1),
                    1.0, 0.0).astype(BF16)
    before = _dot(sel.astype(BF16), tri)
    cnt = jnp.sum(sel, axis=1, keepdims=True)
    run = jnp.floor((cnt + (PIECE - 1.0)) * (1.0 / PIECE)) * PIECE
    low = jnp.where(lax.broadcasted_iota(jnp.int32, (N_EXPERTS, N_EXPERTS), 1)
                    < lax.broadcasted_iota(jnp.int32, (N_EXPERTS, N_EXPERTS), 0), 1.0, 0.0).astype(BF16)
    start = _dot(low, jnp.broadcast_to(run, (N_EXPERTS, LANES)).astype(BF16))[:, 0:1]
    base = before + start
    base = [base[j * 8:(j + 1) * 8, :] for j in range(per)]
    lsl = jnp.zeros((TOP_K, n), F32)
    for k in range(TOP_K):
        lsl = jnp.where(kiota == k, pick(hits_k[k], base), lsl)
    ls_ref[...] = lsl.astype(jnp.int32)
    cnt_ref[...] = jnp.broadcast_to(cnt, (N_EXPERTS, LANES)).astype(jnp.int32)


def _outproj(mixes, w, xall, mod3, g1, g2, wr, rb, nb):
    bsz = xall.shape[0]
    n_lat = SEQ // ROW_BLK
    full = lambda shape: pl.BlockSpec(shape, lambda b, j: (0,) * len(shape))
    rows = lambda w_: pl.BlockSpec((None, ROW_BLK, w_), lambda b, j: (b, j, 0))
    route = pl.BlockSpec((None, None, TOP_K, ROW_BLK), lambda b, j: (b, j, 0, 0))
    return pl.pallas_call(
        _outproj_kernel,
        grid=(bsz, nb),
        in_specs=[rows(256)] * 4 + [
            full((D_MODEL, D_MODEL)), rows(D_MODEL),
            pl.BlockSpec((None, 1, 6 * D_MODEL), lambda b, j: (_mod_row(j, b, n_lat), 0, 0)),
            full((1, D_MODEL)), full((1, D_MODEL)),
            full((N_EXPERTS, D_MODEL)), full((N_EXPERTS, ROW_BLK)),
        ],
        out_specs=[
            rows(D_MODEL), rows(D_MODEL),
            pl.BlockSpec((None, ROW_BLK * TILE_ROWS, LANES), lambda b, j: (b, j, 0)),
            route, route,
            pl.BlockSpec((None, None, N_EXPERTS, LANES), lambda b, j: (b, j, 0, 0)),
        ],
        out_shape=[
            jax.ShapeDtypeStruct((bsz, nb * ROW_BLK, D_MODEL), F32),
            jax.ShapeDtypeStruct((bsz, nb * ROW_BLK, D_MODEL), BF16),
            jax.ShapeDtypeStruct((bsz, nb * ROW_BLK * TILE_ROWS, LANES), F32),
            jax.ShapeDtypeStruct((bsz, nb, TOP_K, ROW_BLK), F32),
            jax.ShapeDtypeStruct((bsz, nb, TOP_K, ROW_BLK), jnp.int32),
            jax.ShapeDtypeStruct((bsz, nb, N_EXPERTS, LANES), jnp.int32),
        ],
        name="outproj",
    )(*mixes, w, xall, mod3, g1, g2, wr, rb)


def _rows(i, n):
    return pl.ds(pl.multiple_of(i * n, n), n)


def _tile_at(row):
    return pl.ds(pl.multiple_of(row, TILE_ROWS), TILE_ROWS)


def _piece(p):
    return _rows(p, PK_PIECE_ROWS)


def _pack_pairs(lo, hi):
    return (lax.shift_right_logical(pltpu.bitcast(lo, U32), jnp.uint32(16))
            | (pltpu.bitcast(hi, U32) & jnp.uint32(0xFFFF0000)))


def _unpack_lo(u):
    return pltpu.bitcast(lax.shift_left(u, jnp.uint32(16)), F32)


def _unpack_hi(u):
    return pltpu.bitcast(u & jnp.uint32(0xFFFF0000), F32)


def _start_pieces(lo, hi, start):
    n = hi - lo

    def pair(q, carry):
        start(lo + 2 * q, 0)
        start(lo + 2 * q + 1, 1)
        return carry
    lax.fori_loop(0, lax.shift_right_logical(n, 1), pair, 0)

    @pl.when((n & 1) == 1)
    def _():
        start(hi - 1, 0)


def _wait_pieces(n, copy_of_rows):
    def group(p, carry):
        copy_of_rows(WAIT_GROUP * PK_PIECE_ROWS).wait()
        return carry
    lax.fori_loop(0, lax.shift_right_logical(n, LOG_WAIT_GROUP), group, 0)

    def single(p, carry):
        copy_of_rows(PK_PIECE_ROWS).wait()
        return carry
    lax.fori_loop(0, n & (WAIT_GROUP - 1), single, 0)


def _interleave(body, n_pieces, start):
    per = ROW_BLK // N_CHUNKS

    def chunk(c, carry):
        for tt in range(per):
            body(c * per + tt)
        _start_pieces(lax.shift_right_logical(c * n_pieces, LOG_CHUNKS),
                      lax.shift_right_logical((c + 1) * n_pieces, LOG_CHUNKS), start)
        return carry
    lax.fori_loop(0, N_CHUNKS, chunk, 0)


def _dispatch_kernel(np_ref, nu_ref, nt_ref, ls_ref, pd_ref, pdp_ref, td_ref, h_ref, xs_hbm,
                     stg, pk, zblk, sem, zsem, *, n_steps, n_blocks):
    i = pl.program_id(0)
    slot = i % 3
    prev = (i + 2) % 3

    def piece_copy(s, p, d):
        return pltpu.make_async_copy(pk.at[s, _piece(p), :], xs_hbm.at[_piece(d), :], sem.at[s])

    def wait_pieces(s, n):
        _wait_pieces(n, lambda rows: pltpu.make_async_copy(
            pk.at[s, pl.ds(0, rows), :], xs_hbm.at[pl.ds(0, rows), :], sem.at[s]))

    def zero_piece_copy(d):
        return pltpu.make_async_copy(zblk.at[pl.ds(0, PK_PIECE_ROWS), :], xs_hbm.at[_piece(d), :], zsem.at[0])

    def zero_block_copy(b):
        return pltpu.make_async_copy(zblk, xs_hbm.at[_rows(b, PK_BLK_ROWS), :], zsem.at[1])

    @pl.when(i == 0)
    def _():
        zblk[...] = jnp.zeros_like(zblk)

        def zero(c, carry):
            stg[_rows(c, 2048), :] = jnp.zeros((2048, LANES), F32)
            return carry
        lax.fori_loop(0, STG_ROWS // 2048, zero, 0)

    @pl.when(i >= 3)
    def _():
        wait_pieces(slot, np_ref[jnp.maximum(i - 3, 0)])

    def scatter_token(t):
        tile = h_ref[_rows(t, TILE_ROWS), :].astype(BF16).astype(F32)
        for k in range(TOP_K):
            stg[_tile_at(ls_ref[0, k * ROW_BLK + t]), :] = tile

    np_prev = jnp.where(i >= 1, np_ref[jnp.maximum(i - 1, 0)], 0)
    _interleave(scatter_token, np_prev,
                lambda p, prio: piece_copy(prev, p, pdp_ref[0, p]).start(priority=prio))

    def pack_piece(p, carry):
        v = stg[_rows(p, PIECE_ROWS), :].reshape(PIECE // 2, 2 * TILE_ROWS, LANES)
        packed = _pack_pairs(v[:, 0:TILE_ROWS, :], v[:, TILE_ROWS:2 * TILE_ROWS, :])
        pk[slot, _piece(p), :] = packed.reshape(PK_PIECE_ROWS, LANES)
        return carry
    lax.fori_loop(0, np_ref[i], pack_piece, 0)

    @pl.when(i == n_steps - 1)
    def _():
        _start_pieces(0, np_ref[i], lambda p, prio: piece_copy(slot, p, pd_ref[0, p]).start(priority=prio))

        def tail(p, carry):
            zero_piece_copy(td_ref[0, p]).start()
            return carry
        lax.fori_loop(0, nt_ref[0], tail, 0)

        def blk(b, carry):
            zero_block_copy(b).start()
            return carry
        lax.fori_loop(nu_ref[0], n_blocks, blk, 0)

        wait_pieces((i + 1) % 3, np_ref[jnp.maximum(i - 2, 0)])
        wait_pieces(prev, np_ref[jnp.maximum(i - 1, 0)])
        wait_pieces(slot, np_ref[i])

        def tail_wait(p, carry):
            zero_piece_copy(0).wait()
            return carry
        lax.fori_loop(0, nt_ref[0], tail_wait, 0)

        def blk_wait(b, carry):
            zero_block_copy(0).wait()
            return carry
        lax.fori_loop(nu_ref[0], n_blocks, blk_wait, 0)


def _dispatch(npieces, n_used, ntail, lslot, piece_dst, tail_dst, h2t, n_steps, n_blocks):
    assert n_steps >= 3
    smem = lambda n, imap: pl.BlockSpec((None, 1, n), imap, memory_space=pltpu.SMEM)
    grid_spec = pltpu.PrefetchScalarGridSpec(
        num_scalar_prefetch=3,
        grid=(n_steps,),
        in_specs=[
            smem(TOP_K * ROW_BLK, lambda i, *_: (i, 0, 0)),
            smem(PIECES_MAX, lambda i, *_: (i, 0, 0)),
            smem(PIECES_MAX, lambda i, *_: (jnp.maximum(i - 1, 0), 0, 0)),
            smem(TAIL_MAX, lambda i, *_: (0, 0, 0)),
            pl.BlockSpec((ROW_BLK * TILE_ROWS, LANES), lambda i, *_: (i, 0)),
        ],
        out_specs=pl.BlockSpec(memory_space=pl.ANY),
        scratch_shapes=[
            pltpu.VMEM((STG_ROWS, LANES), F32),
            pltpu.VMEM((3, PK_STG_ROWS, LANES), U32),
            pltpu.VMEM((PK_BLK_ROWS, LANES), U32),
            pltpu.SemaphoreType.DMA((3,)),
            pltpu.SemaphoreType.DMA((2,)),
        ],
    )
    return pl.pallas_call(
        functools.partial(_dispatch_kernel, n_steps=n_steps, n_blocks=n_blocks),
        grid_spec=grid_spec,
        out_shape=jax.ShapeDtypeStruct((n_blocks * PK_BLK_ROWS, LANES), U32),
        compiler_params=pltpu.CompilerParams(dimension_semantics=("arbitrary",),
                                             vmem_limit_bytes=VMEM_LIMIT),
        name="dispatch",
    )(npieces, n_used, ntail, lslot, piece_dst, piece_dst, tail_dst, h2t)


def _from_tiles(ref, n):
    return jnp.concatenate([ref[pl.ds(s, n, stride=TILE_ROWS), :] for s in range(TILE_ROWS)], axis=1)


def _experts_kernel(be_ref, nu_ref, xs_hbm, wg_ref, wu_ref, wd_ref, ys_hbm,
                    xbuf, ybuf, wgb, wub, wdb, xsem, ysem, *, n_blocks):
    i = pl.program_id(0)
    n_used = nu_ref[0]
    rows = PK_BLK_ROWS
    half = rows // 2
    pairs = MOE_BLK // 2

    def halves(b):
        return [pl.ds(pl.multiple_of(b * rows + h * half, half), half) for h in range(2)]

    def x_copies(b, s):
        return [pltpu.make_async_copy(xs_hbm.at[src, :], xbuf.at[s, pl.ds(h * half, half), :], xsem.at[s])
                for h, src in enumerate(halves(b))]

    def y_copies(b, s):
        return [pltpu.make_async_copy(ybuf.at[s, pl.ds(h * half, half), :], ys_hbm.at[dst, :], ysem.at[s])
                for h, dst in enumerate(halves(b))]

    def start(copies):
        for h, c in enumerate(copies):
            c.start(priority=h)

    def wait(copies):
        for c in copies:
            c.wait()

    @pl.when(i == 0)
    def _():
        start(x_copies(0, 0))

        @pl.when(n_used > 1)
        def _():
            start(x_copies(1, 1))

    @pl.when(i + 2 < n_used)
    def _():
        start(x_copies(i + 2, (i + 2) % 3))

    @pl.when(i < n_used)
    def _():
        wait(x_copies(i, i % 3))

        @pl.when(i >= 2)
        def _():
            wait(y_copies(i - 2, i % 2))

        @pl.when(jnp.logical_or(i == 0, be_ref[i] != be_ref[jnp.maximum(i - 1, 0)]))
        def _():
            wgb[...] = wg_ref[...].astype(BF16)
            wub[...] = wu_ref[...].astype(BF16)
            wdb[...] = wd_ref[...].astype(BF16)

        x = pltpu.bitcast(_from_tiles(xbuf.at[i % 3], pairs), BF16)
        g = _dot(x, wgb[...])
        u = _dot(x, wub[...])
        y = _dot((g * _sigmoid(g) * u).astype(BF16), wdb[...])
        yp = pltpu.bitcast(y.astype(BF16), U32)
        yb = ybuf.at[i % 2]
        for s in range(TILE_ROWS):
            yb[pl.ds(s, pairs, stride=TILE_ROWS), :] = yp[:, s * LANES:(s + 1) * LANES]
        start(y_copies(i, i % 2))

        @pl.when(i == n_used - 1)
        def _():
            @pl.when(i >= 1)
            def _():
                wait(y_copies(i - 1, (i + 1) % 2))
            wait(y_copies(i, i % 2))
            ybuf[0] = jnp.zeros((rows, LANES), U32)

            def fill(b, carry):
                start(y_copies(b, 0))
                return carry
            lax.fori_loop(n_used, n_blocks, fill, 0)

            def fill_wait(b, carry):
                wait(y_copies(0, 0))
                return carry
            lax.fori_loop(n_used, n_blocks, fill_wait, 0)


def _experts(block_e, n_used, x_sorted, wg, wu, wd, layer, n_blocks):
    wspec = lambda shape: pl.BlockSpec((None, None) + shape, lambda i, be, nu: (layer, be[i], 0, 0))
    rows = PK_BLK_ROWS
    grid_spec = pltpu.PrefetchScalarGridSpec(
        num_scalar_prefetch=2,
        grid=(n_blocks,),
        in_specs=[
            pl.BlockSpec(memory_space=pl.ANY),
            wspec((D_MODEL, EXPERT_DIM)), wspec((D_MODEL, EXPERT_DIM)), wspec((EXPERT_DIM, D_MODEL)),
        ],
        out_specs=pl.BlockSpec(memory_space=pl.ANY),
        scratch_shapes=[
            pltpu.VMEM((3, rows, LANES), U32),
            pltpu.VMEM((2, rows, LANES), U32),
            pltpu.VMEM((D_MODEL, EXPERT_DIM), BF16),
            pltpu.VMEM((D_MODEL, EXPERT_DIM), BF16),
            pltpu.VMEM((EXPERT_DIM, D_MODEL), BF16),
            pltpu.SemaphoreType.DMA((3,)),
            pltpu.SemaphoreType.DMA((2,)),
        ],
    )
    return pl.pallas_call(
        functools.partial(_experts_kernel, n_blocks=n_blocks),
        grid_spec=grid_spec,
        out_shape=jax.ShapeDtypeStruct((n_blocks * rows, LANES), U32),
        compiler_params=pltpu.CompilerParams(dimension_semantics=("arbitrary",)),
        name="experts",
    )(block_e, n_used, x_sorted, wg, wu, wd)


def _combine_kernel(np_ref, lp_ref, wl_ref, wh_ref, cur_ref, nx1_ref, nx2_ref, y_hbm, h2b_ref, xn_ref,
                    mod_ref, g_ref, swg_ref, swu_ref, swd_ref, o_ref, stg, acc, sem, *, n_steps):
    i = pl.program_id(0)
    slot = i % 3
    d = D_MODEL

    def piece_copy(s, p, src):
        return pltpu.make_async_copy(y_hbm.at[_piece(src), :], stg.at[s, _piece(p), :], sem.at[s])

    def fetcher(s, pd_ref):
        return lambda p, prio: piece_copy(s, p, pd_ref[0, p]).start(priority=prio)

    @pl.when(i == 0)
    def _():
        _start_pieces(0, np_ref[0], fetcher(0, cur_ref))
        _start_pieces(0, np_ref[1], fetcher(1, nx1_ref))

    _wait_pieces(np_ref[i], lambda rows: pltpu.make_async_copy(
        y_hbm.at[pl.ds(0, rows), :], stg.at[slot, pl.ds(0, rows), :], sem.at[slot]))

    def gather_token(t):
        a = None
        for k in range(TOP_K):
            j = k * ROW_BLK + t
            pair = stg[slot, _tile_at(lp_ref[0, j]), :]
            v = _unpack_lo(pair) * wl_ref[0, j] + _unpack_hi(pair) * wh_ref[0, j]
            a = v if a is None else a + v
        acc[_rows(t, TILE_ROWS), :] = a

    np_next = jnp.where(i + 2 < n_steps, np_ref[jnp.minimum(i + 2, n_steps - 1)], 0)
    _interleave(gather_token, np_next, fetcher((i + 2) % 3, nx2_ref))

    hb = h2b_ref[...]
    g = _dot(hb, swg_ref[...])
    u = _dot(hb, swu_ref[...])
    f = _dot((g * _sigmoid(g) * u).astype(BF16), swd_ref[...]) + _from_tiles(acc, ROW_BLK)
    o_ref[...] = xn_ref[...] + mod_ref[:, 5 * d:6 * d] * _rms(f, g_ref[...])


def _combine(npieces, pair_row, w_lo, w_hi, piece_src, y_sorted, h2b_rows, xn_rows, mod3, g, swg, swu, swd,
             n_steps, per_batch):
    assert n_steps >= 3
    last = n_steps - 1
    full = lambda shape: pl.BlockSpec(shape, lambda n, *_: (0,) * len(shape))
    rows = lambda w_: pl.BlockSpec((ROW_BLK, w_), lambda n, *_: (n, 0))
    smem = lambda w_, imap: pl.BlockSpec((None, 1, w_), imap, memory_space=pltpu.SMEM)
    n_lat = SEQ // ROW_BLK
    grid_spec = pltpu.PrefetchScalarGridSpec(
        num_scalar_prefetch=1,
        grid=(n_steps,),
        in_specs=[
            smem(TOP_K * ROW_BLK, lambda n, *_: (n, 0, 0)),
            smem(TOP_K * ROW_BLK, lambda n, *_: (n, 0, 0)),
            smem(TOP_K * ROW_BLK, lambda n, *_: (n, 0, 0)),
            smem(PIECES_MAX, lambda n, *_: (n, 0, 0)),
            smem(PIECES_MAX, lambda n, *_: (jnp.minimum(n + 1, last), 0, 0)),
            smem(PIECES_MAX, lambda n, *_: (jnp.minimum(n + 2, last), 0, 0)),
            pl.BlockSpec(memory_space=pl.ANY),
            rows(D_MODEL), rows(D_MODEL),
            pl.BlockSpec((None, 1, 6 * D_MODEL),
                         lambda n, *_: (_mod_row(n % per_batch, n // per_batch, n_lat), 0, 0)),
            full((1, D_MODEL)),
            full((D_MODEL, SHARED_DIM)), full((D_MODEL, SHARED_DIM)), full((SHARED_DIM, D_MODEL)),
        ],
        out_specs=rows(D_MODEL),
        scratch_shapes=[
            pltpu.VMEM((3, PK_STG_ROWS, LANES), U32),
            pltpu.VMEM((ROW_BLK * TILE_ROWS, LANES), F32),
            pltpu.SemaphoreType.DMA((3,)),
        ],
    )
    return pl.pallas_call(
        functools.partial(_combine_kernel, n_steps=n_steps),
        grid_spec=grid_spec,
        out_shape=jax.ShapeDtypeStruct(xn_rows.shape, F32),
        compiler_params=pltpu.CompilerParams(dimension_semantics=("arbitrary",),
                                             vmem_limit_bytes=VMEM_LIMIT),
        name="combine",
    )(npieces, pair_row, w_lo, w_hi, piece_src, piece_src, piece_src, y_sorted, h2b_rows, xn_rows, mod3, g,
      swg, swu, swd)


def _moe_plan(cnt_rows, n_blocks):
    i32 = jnp.int32
    per_blk = MOE_BLK // PIECE
    to_expert = lambda v: v.reshape(v.shape[:-1] + (8, 8)).swapaxes(-1, -2).reshape(v.shape)
    runs = (cnt_rows + PIECE - 1) // PIECE
    loc_end = jnp.cumsum(runs, axis=1)
    loc = loc_end - runs
    npieces = loc_end[:, -1].astype(i32)
    reg = to_expert(runs.sum(axis=0))
    padded = (reg + per_blk - 1) // per_blk * per_blk
    pad_end = jnp.cumsum(padded)
    pad_start = pad_end - padded
    off = to_expert(pad_start)[None, :] + jnp.cumsum(runs, axis=0) - runs
    p = jnp.arange(PIECES_MAX, dtype=i32)[None, :, None]
    mine = (loc[:, None, :] <= p) & (p < loc_end[:, None, :])
    piece_dst = jnp.sum(jnp.where(mine, (off - loc)[:, None, :], 0), axis=-1) + p[:, :, 0]
    tail_n = padded - reg
    t_end = jnp.cumsum(tail_n)
    t_beg = t_end - tail_n
    q = jnp.arange(TAIL_MAX, dtype=i32)[:, None]
    tmine = (t_beg[None, :] <= q) & (q < t_end[None, :])
    tail_dst = jnp.sum(jnp.where(tmine, (pad_start + reg - t_beg)[None, :], 0), axis=-1) + q[:, 0]
    first = jnp.arange(n_blocks, dtype=i32)[:, None] * per_blk
    block_e = jnp.minimum(jnp.sum((pad_end[None, :] <= first).astype(i32), axis=-1), N_EXPERTS - 1)
    n_used = (pad_end[-1] // per_blk).astype(i32).reshape(1)
    return (npieces, piece_dst.astype(i32).reshape(-1, 1, PIECES_MAX),
            tail_dst.astype(i32).reshape(1, 1, TAIL_MAX), t_end[-1].astype(i32).reshape(1),
            block_e.astype(i32), n_used)


def _axial_tables(dim):
    t = jnp.arange(SEQ)
    row = (t // GRID_W).astype(F32)
    col = (t % GRID_W).astype(F32)
    quarter = dim // 4
    freqs = ROPE_THETA ** (-jnp.arange(quarter, dtype=F32) / quarter)
    ar, ac = row[:, None] * freqs, col[:, None] * freqs
    ang = jnp.concatenate([ar, ar, ac, ac], axis=-1)
    sign = jnp.where((jnp.arange(dim) // quarter) % 2 == 0, -1.0, 1.0).astype(F32)
    cos = jnp.concatenate([jnp.cos(ang), jnp.ones((CTX_LEN, dim), F32)], axis=0)
    sin = jnp.concatenate([jnp.sin(ang) * sign, jnp.zeros((CTX_LEN, dim), F32)], axis=0)
    return cos, sin


def _rope_tables():
    cos_h, sin_h = _axial_tables(HEAD_DIM)
    cos_r, sin_r = _axial_tables(B_ROPE)
    cosh, sinh = jnp.tile(cos_h, (1, 4)), jnp.tile(sin_h, (1, 4))
    scale_b = (B_NOPE + B_ROPE) ** -0.5
    ones, zeros = jnp.ones((TOK, 64), F32), jnp.zeros((TOK, 64), F32)
    pad1, pad0 = jnp.ones((TOK, 32), F32), jnp.zeros((TOK, 32), F32)
    cosb = jnp.tile(jnp.concatenate([ones, cos_r, pad1], axis=1) * scale_b, (1, 4))
    sinb = jnp.tile(jnp.concatenate([zeros, sin_r, pad0], axis=1) * scale_b, (1, 4))
    cosr = jnp.concatenate([cos_r, jnp.ones((TOK, 96), F32)], axis=1)
    sinr = jnp.concatenate([sin_r, jnp.zeros((TOK, 96), F32)], axis=1)
    return cosh, sinh, cosb, sinb, cosr, sinr


_GQA_ORDER = (0, 2, 1, 3)


def _head_cols(base, order, width=HEAD_DIM):
    return np.concatenate([base + np.arange(h * width, (h + 1) * width) for h in order])


def _w_in_layout(w_in):
    zeros = lambda n: jnp.zeros((D_MODEL, n), w_in.dtype)
    a_q = w_in[:, _head_cols(0, _GQA_ORDER)]
    a_kv = w_in[:, 256:512]
    b_q = jnp.concatenate(
        [jnp.concatenate([w_in[:, 512 + h * 96:512 + (h + 1) * 96], zeros(32)], axis=1)
         for h in range(B_HEADS)], axis=1)
    b_c = w_in[:, 896:1024]
    b_r = jnp.concatenate([w_in[:, 1024:1056], zeros(96)], axis=1)
    c_q = w_in[:, _head_cols(1056, _GQA_ORDER)]
    rest = w_in[:, 1312:2336]
    return jnp.concatenate([a_q, a_kv, b_q, b_c, b_r, c_q, rest], axis=1).astype(BF16)


def _w_kv_layout(w_kv_up):
    zeros = jnp.zeros((B_KV_RANK, 64), w_kv_up.dtype)
    nope = [jnp.concatenate([w_kv_up[:, h * 128:h * 128 + 64], zeros], axis=1) for h in range(B_HEADS)]
    val = [w_kv_up[:, h * 128 + 64:(h + 1) * 128] for h in range(B_HEADS)]
    return jnp.concatenate(nope + val, axis=1).astype(BF16)


def _w_out_layout(w_out):
    rows = np.concatenate([_head_cols(0, _GQA_ORDER), np.arange(256, 512),
                           _head_cols(512, _GQA_ORDER), np.arange(768, 1024)])
    return w_out[rows].astype(BF16)


_ROUTER_ROWS = np.array([g * 8 + j for j in range(8) for g in range(8)])


def kernel(x, c, ctx, c_ctx, ada_w, ada_b, pre_mix_g, post_mix_g, pre_ffn_g, post_ffn_g, w_in, a_q_norm, a_k_norm, b_kv_norm, b_w_kv_up, c_sink, d_rpb, w_out, router_w, router_b, exp_w_gate, exp_w_up, exp_w_down, sh_w_gate, sh_w_up, sh_w_down):
    bsz = x.shape[0]
    depth = ada_w.shape[0]
    xall = jnp.concatenate([x, ctx], axis=1)

    cvec = jnp.zeros((16, D_MODEL), F32).at[:bsz].set(c).at[8].set(c_ctx)
    mod = _ada(cvec, ada_w, ada_b)
    tabs = _rope_tables()
    avg = jnp.kron(jnp.eye(4, dtype=F32), jnp.full((64, 64), 1.0 / 64, F32)).astype(BF16)
    row2 = lambda v: v.reshape(1, -1).astype(F32)

    for l in range(depth):
        with_ctx = l < depth - 1
        mod3 = mod[l].reshape(16, 1, 6 * D_MODEL)

        proj = _inproj(
            xall, mod3, row2(pre_mix_g[l]), _w_in_layout(w_in[l]),
            row2(jnp.tile(a_q_norm[l], 4) * HEAD_DIM ** -0.5), row2(jnp.tile(a_k_norm[l], 2)),
            row2(b_kv_norm[l]), _w_kv_layout(b_w_kv_up[l]), avg, tabs)
        qa, ka, va, qb, kb, vb, qc, kc, vc, qd, kd, vd = proj

        nq = NQ + (CTX_LEN // QBLK if with_ctx else 0)
        full = lambda shape: pl.BlockSpec(shape, lambda b, i: (0,) * len(shape))
        mix_a = _attn_call(_attn_a_kernel, "attn_a", qa, ka, va, [], [], nq)
        mix_b = _attn_call(_attn_b_kernel, "attn_b", qb, kb, vb, [], [], nq // 2, qblk=2 * QBLK)
        sink = jnp.zeros((8, LANES), F32).at[:4].set(
            jnp.broadcast_to(c_sink[l][np.array(_GQA_ORDER)][:, None], (4, LANES)))
        mix_c = _attn_call(_attn_c_kernel, "attn_c", qc, kc, vc, [sink], [full((8, LANES))], nq)
        bias = _na_bias(d_rpb[l])
        mix_d = _attn_call(
            _attn_d_kernel, "attn_d", qd, kd, vd, [bias],
            [pl.BlockSpec((None, D_HEADS, QBLK, NA_WIN), lambda b, i: (_na_pattern(i), 0, 0, 0))], nq)

        nb = (TOK if with_ctx else SEQ) // ROW_BLK
        rb = jnp.broadcast_to(router_b[l][_ROUTER_ROWS][:, None], (N_EXPERTS, ROW_BLK)).astype(F32)
        xn, h2b, h2t, wt_t, ls_t, cnt_t = _outproj(
            (mix_a, mix_b, mix_c, mix_d), _w_out_layout(w_out[l]), xall, mod3,
            row2(post_mix_g[l]), row2(pre_ffn_g[l]),
            router_w[l].T[_ROUTER_ROWS].astype(BF16), rb, nb)

        t_used = nb * ROW_BLK
        n_tok = bsz * t_used
        n_steps = bsz * nb
        n_blocks = -(-(n_tok * TOP_K + n_steps * N_EXPERTS * (PIECE - 1) + N_EXPERTS * (MOE_BLK - 1))
                     // MOE_BLK)
        npieces, piece_dst, tail_dst, ntail, block_e, n_used = _moe_plan(
            cnt_t[..., 0].reshape(n_steps, N_EXPERTS), n_blocks)
        lslot = ls_t.reshape(n_steps, 1, TOP_K * ROW_BLK)
        wts = wt_t.reshape(n_steps, 1, TOP_K * ROW_BLK)
        stg_row = lslot * TILE_ROWS
        pair_row = (lslot // 2) * TILE_ROWS
        w_lo = jnp.where(lslot % 2 == 0, wts, 0.0)
        w_hi = wts - w_lo

        x_sorted = _dispatch(npieces, n_used, ntail, stg_row, piece_dst, tail_dst,
                             h2t.reshape(n_tok * TILE_ROWS, LANES), n_steps, n_blocks)
        y_sorted = _experts(block_e, n_used, x_sorted, exp_w_gate, exp_w_up, exp_w_down, l, n_blocks)
        xall = _combine(
            npieces, pair_row, w_lo, w_hi, piece_dst, y_sorted, h2b.reshape(n_tok, D_MODEL),
            xn.reshape(n_tok, D_MODEL), mod3, row2(post_ffn_g[l]),
            sh_w_gate[l].astype(BF16), sh_w_up[l].astype(BF16), sh_w_down[l].astype(BF16),
            n_steps, nb).reshape(bsz, t_used, D_MODEL)

    return xall[:, :SEQ]
```

```python
import functools

import numpy as np
import jax
import jax.numpy as jnp
from jax import lax
from jax.experimental import pallas as pl
from jax.experimental.pallas import tpu as pltpu

F32 = jnp.float32
BF16 = jnp.bfloat16
U32 = jnp.uint32

D_MODEL = 1024
SEQ = 2048
CTX_LEN = 256
TOK = SEQ + CTX_LEN
GRID_W = 64
HEAD_DIM = 64
ROPE_THETA = 10000.0
EPS = 1e-6
NEG = -1e30

A_HEADS, A_KV_HEADS = 4, 2
B_HEADS, B_NOPE, B_ROPE, B_V, B_KV_RANK = 4, 64, 32, 64, 128
C_HEADS, C_KV_HEADS, C_WINDOW = 4, 2, 128
D_HEADS, NA_ROWS, NA_COLS = 4, 8, 16

N_EXPERTS, TOP_K, N_GROUPS, TOPK_GROUPS = 64, 8, 8, 4
EXPERT_DIM, SHARED_DIM = 256, 256
ROUTED_SCALE = 2.5

LANES = 128
QBLK = 128
ROW_BLK = 256
MOE_BLK = 512
TILE_ROWS = D_MODEL // LANES
PIECE = 8
PIECE_ROWS = PIECE * TILE_ROWS
PIECES_MAX = 320
STG_ROWS = PIECES_MAX * PIECE_ROWS
PK_PIECE_ROWS = PIECE_ROWS // 2
PK_STG_ROWS = STG_ROWS // 2
PK_BLK_ROWS = MOE_BLK * TILE_ROWS // 2
TAIL_MAX = N_EXPERTS * MOE_BLK // PIECE
LOG_PACK_GROUP = 2
PACK_GROUP = 1 << LOG_PACK_GROUP
LOG_WAIT_GROUP = 4
WAIT_GROUP = 1 << LOG_WAIT_GROUP
LOG_CHUNKS = 4
N_CHUNKS = 1 << LOG_CHUNKS
VMEM_LIMIT = 56 * 1024 * 1024
NQ = SEQ // QBLK
NA_WIN = 10 * GRID_W

_AQ, _AK, _AV, _BQ, _BC, _BR = 0, 256, 384, 512, 1024, 1152
_CQ, _CK, _CV, _DQ, _DK, _DV = 1280, 1536, 1664, 1792, 2048, 2304
IN_W = 2560

_NT = (((1,), (1,)), ((), ()))


def _sigmoid(x):
    return 1.0 / (1.0 + jnp.exp(-x))


def _rms(x, g):
    return x * lax.rsqrt(jnp.mean(x * x, axis=-1, keepdims=True) + EPS) * g


def _dot(a, b):
    return jnp.dot(a, b, preferred_element_type=F32)


def _dot_nt(a, b):
    return lax.dot_general(a, b, _NT, preferred_element_type=F32)


def _ada_kernel(c_ref, w_ref, b_ref, o_ref):
    c = c_ref[...]
    a = (c * _sigmoid(c)).astype(BF16)
    o_ref[...] = _dot(a, w_ref[...].astype(BF16)) + b_ref[...]


def _ada(cvec, ada_w, ada_b):
    depth = ada_w.shape[0]
    n = ada_w.shape[2]
    tn = 1536
    return pl.pallas_call(
        _ada_kernel,
        grid=(depth, n // tn),
        in_specs=[
            pl.BlockSpec((16, D_MODEL), lambda l, j: (0, 0)),
            pl.BlockSpec((None, D_MODEL, tn), lambda l, j: (l, 0, j)),
            pl.BlockSpec((None, 1, tn), lambda l, j: (l, 0, j)),
        ],
        out_specs=pl.BlockSpec((None, 16, tn), lambda l, j: (l, 0, j)),
        out_shape=jax.ShapeDtypeStruct((depth, 16, n), F32),
        name="ada",
    )(cvec, ada_w, ada_b.reshape(depth, 1, n))


def _rope(x, cos, sin, chunk):
    outs = []
    for c in range(x.shape[1] // LANES):
        sl = slice(c * LANES, (c + 1) * LANES)
        xs = x[:, sl]
        lane = lax.broadcasted_iota(jnp.int32, xs.shape, 1)
        first = (lane & chunk) == 0
        rot = jnp.where(first, pltpu.roll(xs, LANES - chunk, 1), pltpu.roll(xs, chunk, 1))
        outs.append(xs * cos[:, sl] + rot * sin[:, sl])
    return outs[0] if len(outs) == 1 else jnp.concatenate(outs, axis=1)


def _head_norm(x, avg):
    sq = x * x
    hi = sq.astype(BF16)
    lo = (sq - hi.astype(F32)).astype(BF16)
    ms = _dot(hi, avg) + _dot(lo, avg)
    return x * lax.rsqrt(ms + EPS)


def _inproj_kernel(x_ref, mod_ref, g_ref, w_ref, qg_ref, kg_ref, kvg_ref, wkv_ref, avg_ref,
                   cosh_ref, sinh_ref, cosb_ref, sinb_ref, cosr_ref, sinr_ref,
                   qa_ref, ka_ref, va_ref, qb_ref, kb_ref, vb_ref,
                   qc_ref, kc_ref, vc_ref, qd_ref, kd_ref, vd_ref):
    d = D_MODEL
    h = _rms(x_ref[...], g_ref[...]) * (1.0 + mod_ref[:, d:2 * d]) + mod_ref[:, 0:d]
    p = _dot(h.astype(BF16), w_ref[...])
    cosh, sinh = cosh_ref[...], sinh_ref[...]
    avg = avg_ref[...]

    qa = _head_norm(p[:, _AQ:_AQ + 256], avg) * qg_ref[...]
    qa_ref[...] = _rope(qa, cosh, sinh, 16).astype(BF16)
    ka = _head_norm(p[:, _AK:_AK + 128], avg[0:128, 0:128]) * kg_ref[...]
    ka_ref[...] = _rope(ka, cosh[:, 0:128], sinh[:, 0:128], 16).astype(BF16)
    ones = jnp.ones((p.shape[0], LANES), BF16)
    va_ref[...] = jnp.concatenate([p[:, _AV:_AV + 128].astype(BF16), ones], axis=1)

    qb_ref[...] = _rope(p[:, _BQ:_BQ + 512], cosb_ref[...], sinb_ref[...], 8).astype(BF16)
    cn = _rms(p[:, _BC:_BC + 128], kvg_ref[...])
    kv = _dot(cn.astype(BF16), wkv_ref[...])
    kr = _rope(p[:, _BR:_BR + 128], cosr_ref[...], sinr_ref[...], 8)
    kr = pltpu.roll(kr, 64, 1)
    kb_ref[...] = jnp.concatenate(
        [kv[:, hh * 128:(hh + 1) * 128] + kr for hh in range(B_HEADS)], axis=1).astype(BF16)
    vb_ref[...] = jnp.concatenate(
        [kv[:, 512:640].astype(BF16), ones, kv[:, 640:768].astype(BF16), ones], axis=1)

    qc_ref[...] = _rope(p[:, _CQ:_CQ + 256] * 0.125, cosh, sinh, 16).astype(BF16)
    kc_ref[...] = _rope(p[:, _CK:_CK + 128], cosh[:, 0:128], sinh[:, 0:128], 16).astype(BF16)
    vc_ref[...] = jnp.concatenate([p[:, _CV:_CV + 128].astype(BF16), ones], axis=1)

    qd_ref[...] = (p[:, _DQ:_DQ + 256] * 0.125).astype(BF16)
    kd_ref[...] = p[:, _DK:_DK + 256].astype(BF16)
    vd_ref[...] = jnp.concatenate(
        [p[:, _DV:_DV + 128].astype(BF16), ones, p[:, _DV + 128:_DV + 256].astype(BF16), ones], axis=1)


def _mod_row(j, b, n_lat_blocks):
    return jnp.where(j < n_lat_blocks, b, 8)


def _inproj(xall, mod3, g, w, qg, kg, kvg, wkv, avg, tabs):
    bsz = xall.shape[0]
    nb = TOK // ROW_BLK
    n_lat = SEQ // ROW_BLK
    full = lambda shape: pl.BlockSpec(shape, lambda b, j: (0,) * len(shape))
    row = lambda w_: pl.BlockSpec((ROW_BLK, w_), lambda b, j: (j, 0))
    out = lambda w_: pl.BlockSpec((None, ROW_BLK, w_), lambda b, j: (b, j, 0))
    widths = (256, 128, 256, 512, 512, 512, 256, 128, 256, 256, 256, 512)
    return pl.pallas_call(
        _inproj_kernel,
        grid=(bsz, nb),
        in_specs=[
            pl.BlockSpec((None, ROW_BLK, D_MODEL), lambda b, j: (b, j, 0)),
            pl.BlockSpec((None, 1, 6 * D_MODEL), lambda b, j: (_mod_row(j, b, n_lat), 0, 0)),
            full((1, D_MODEL)), full((D_MODEL, IN_W)),
            full((1, 256)), full((1, 128)), full((1, 128)), full((128, 768)), full((256, 256)),
            row(256), row(256), row(512), row(512), row(128), row(128),
        ],
        out_specs=[out(w_) for w_ in widths],
        out_shape=[jax.ShapeDtypeStruct((bsz, TOK, w_), BF16) for w_ in widths],
        name="inproj",
    )(xall, mod3, g, w, qg, kg, kvg, wkv, avg, *tabs)


def _softmax_pv(segs, sink=None):
    m = None
    for s, _ in segs:
        ms = jnp.max(s, axis=-1, keepdims=True)
        m = ms if m is None else jnp.maximum(m, ms)
    if sink is not None:
        m = jnp.maximum(m, sink)
    acc = None
    for s, v1 in segs:
        r = _dot(jnp.exp((s - m).astype(BF16)), v1)
        acc = r if acc is None else acc + r
    l = acc[:, LANES:LANES + 1]
    if sink is not None:
        l = l + jnp.exp(sink - m)
    return acc[:, 0:LANES] / l


def _lane_lo(dtype_shape):
    return lax.broadcasted_iota(jnp.int32, dtype_shape, 1) < HEAD_DIM


def _half(qt, hh):
    lo = _lane_lo(qt.shape)
    return jnp.where(lo if hh == 0 else jnp.logical_not(lo), qt, jnp.zeros_like(qt))


def _merge_halves(o0, o1):
    return jnp.where(_lane_lo(o0.shape), o0, o1)


def _stack_heads(q_ref, tiles):
    return jnp.concatenate(
        [_half(q_ref[:, t * LANES:(t + 1) * LANES], hh) for t in tiles for hh in range(2)], axis=0)


def _store_heads(o_ref, o, tiles):
    n = o_ref.shape[0]
    for ti, t in enumerate(tiles):
        o0, o1 = o[2 * ti * n:(2 * ti + 1) * n], o[(2 * ti + 1) * n:(2 * ti + 2) * n]
        o_ref[:, t * LANES:(t + 1) * LANES] = _merge_halves(o0, o1).astype(o_ref.dtype)


def _attn_a_kernel(q_ref, k_ref, v_ref, o_ref, *, n_lat):
    def run(k, v1):
        for t in range(2):
            o = [_softmax_pv([(_dot_nt(_half(q_ref[:, t * LANES:(t + 1) * LANES], hh), k), v1)])
                 for hh in range(2)]
            o_ref[:, t * LANES:(t + 1) * LANES] = _merge_halves(o[0], o[1]).astype(o_ref.dtype)

    @pl.when(pl.program_id(1) < n_lat)
    def _():
        run(k_ref[...], v_ref[...])

    @pl.when(pl.program_id(1) >= n_lat)
    def _():
        run(k_ref[SEQ:TOK, :], v_ref[SEQ:TOK, :])


def _attn_b_kernel(q_ref, k_ref, v_ref, o_ref, *, n_lat):
    def run(lo_, hi_):
        for t in range(2):
            v1 = v_ref[lo_:hi_, t * 256:(t + 1) * 256]
            o = []
            for hh in range(2):
                hd = 2 * t + hh
                q = q_ref[:, hd * LANES:(hd + 1) * LANES]
                k = k_ref[lo_:hi_, hd * LANES:(hd + 1) * LANES]
                o.append(_softmax_pv([(_dot_nt(q, k), v1)]))
            o_ref[:, t * LANES:(t + 1) * LANES] = _merge_halves(o[0], o[1]).astype(o_ref.dtype)

    @pl.when(pl.program_id(1) < n_lat)
    def _():
        run(0, TOK)

    @pl.when(pl.program_id(1) >= n_lat)
    def _():
        run(SEQ, TOK)


def _attn_c_kernel(q_ref, k_ref, v_ref, sink_ref, o_ref, *, n_lat):
    i = pl.program_id(1)
    qblk = o_ref.shape[0]
    band = 3 * qblk

    def run(latent):
        if latent:
            start = pl.multiple_of(jnp.clip((i - 1) * qblk, 0, SEQ - band), qblk)
            shape = (2 * qblk, band)
            qpos = i * qblk + (lax.broadcasted_iota(jnp.int32, shape, 0) & (qblk - 1))
            kpos = start + lax.broadcasted_iota(jnp.int32, shape, 1)
            valid = jnp.abs(qpos - kpos) <= C_WINDOW
        for t in range(2):
            q = _stack_heads(q_ref, (t,))
            segs = [(_dot_nt(q, k_ref[SEQ:TOK, :]), v_ref[SEQ:TOK, :])]
            if latent:
                s = jnp.where(valid, _dot_nt(q, k_ref[pl.ds(start, band), :]), NEG)
                segs.insert(0, (s, v_ref[pl.ds(start, band), :]))
            sink = jnp.concatenate(
                [jnp.broadcast_to(sink_ref[j:j + 1, 0:1], (qblk, 1)) for j in (2 * t, 2 * t + 1)], axis=0)
            _store_heads(o_ref, _softmax_pv(segs, sink=sink), (t,))

    @pl.when(i < n_lat)
    def _():
        run(True)

    @pl.when(i >= n_lat)
    def _():
        run(False)


def _attn_d_kernel(q_ref, k_ref, v_ref, bias_ref, o_ref, *, n_lat):
    i = pl.program_id(1)
    qblk = o_ref.shape[0]

    def run(latent):
        if latent:
            start = pl.multiple_of(jnp.clip((i - 2) * qblk, 0, SEQ - NA_WIN), qblk)
        for t in range(2):
            ksl, vsl = slice(t * LANES, (t + 1) * LANES), slice(t * 256, (t + 1) * 256)
            q = _stack_heads(q_ref, (t,))
            segs = [(_dot_nt(q, k_ref[SEQ:TOK, ksl]), v_ref[SEQ:TOK, vsl])]
            if latent:
                bias = bias_ref[2 * t:2 * t + 2].reshape(2 * qblk, NA_WIN)
                segs.insert(0, (_dot_nt(q, k_ref[pl.ds(start, NA_WIN), ksl]) + bias,
                                v_ref[pl.ds(start, NA_WIN), vsl]))
            _store_heads(o_ref, _softmax_pv(segs), (t,))

    @pl.when(i < n_lat)
    def _():
        run(True)

    @pl.when(i >= n_lat)
    def _():
        run(False)


def _attn_call(kernel, name, q, k, v, extra, extra_specs, n_blocks, qblk=QBLK):
    bsz = q.shape[0]
    return pl.pallas_call(
        functools.partial(kernel, n_lat=SEQ // qblk),
        grid=(bsz, n_blocks),
        in_specs=[
            pl.BlockSpec((None, qblk, q.shape[2]), lambda b, i: (b, i, 0)),
            pl.BlockSpec((None, TOK, k.shape[2]), lambda b, i: (b, 0, 0)),
            pl.BlockSpec((None, TOK, v.shape[2]), lambda b, i: (b, 0, 0)),
        ] + extra_specs,
        out_specs=pl.BlockSpec((None, qblk, 256), lambda b, i: (b, i, 0)),
        out_shape=jax.ShapeDtypeStruct((bsz, n_blocks * qblk, 256), BF16),
        name=name,
    )(q, k, v, *extra)


def _na_pattern(i):
    return jnp.where(i < 2, i, jnp.where(i < NQ - 2, 2, jnp.minimum(i, NQ - 1) - (NQ - 5)))


def _na_bias(rpb):
    rows = SEQ // GRID_W
    n_dr, n_dc = 2 * NA_ROWS - 1, 2 * NA_COLS - 1
    qc = np.arange(GRID_W)
    dc = np.clip(qc[None, :] - qc[:, None] + NA_COLS - 1, 0, n_dc - 1)
    sel_c = jnp.asarray(np.eye(n_dc, dtype=np.float32)[dc])
    cs = np.clip(qc - NA_COLS // 2, 0, GRID_W - NA_COLS)[:, None]
    col_ok = (qc[None, :] >= cs) & (qc[None, :] < cs + NA_COLS)
    out = []
    for i in (0, 1, 2, NQ - 2, NQ - 1):
        start_row = min(max(2 * i - 4, 0), rows - 10)
        qr = 2 * i + np.arange(2)
        kr = start_row + np.arange(10)
        rs = np.clip(qr - NA_ROWS // 2, 0, rows - NA_ROWS)[:, None]
        row_ok = (kr[None] >= rs) & (kr[None] < rs + NA_ROWS)
        dr = np.clip(kr[None] - qr[:, None] + NA_ROWS - 1, 0, n_dr - 1)
        sel_r = jnp.asarray(np.eye(n_dr, dtype=np.float32)[dr])
        b = jnp.einsum("hrc,abr,qkc->haqbk", rpb.astype(F32), sel_r, sel_c,
                       precision=lax.Precision.HIGHEST)
        valid = row_ok[:, None, :, None] & col_ok[None, :, None, :]
        out.append(jnp.where(valid[None], b, NEG).reshape(D_HEADS, QBLK, NA_WIN))
    return jnp.stack(out)


def _outproj_kernel(ma_ref, mb_ref, mc_ref, md_ref, w_ref, x_ref, mod_ref, g1_ref, g2_ref,
                    wr_ref, rb_ref, xn_ref, h2b_ref, h2t_ref, wt_ref, ls_ref, cnt_ref):
    d = D_MODEL
    mix = jnp.concatenate([ma_ref[...], mb_ref[...], mc_ref[...], md_ref[...]], axis=1)
    y = _rms(_dot(mix, w_ref[...]), g1_ref[...])
    xn = x_ref[...] + mod_ref[:, 2 * d:3 * d] * y
    xn_ref[...] = xn
    h2 = _rms(xn, g2_ref[...]) * (1.0 + mod_ref[:, 4 * d:5 * d]) + mod_ref[:, 3 * d:4 * d]
    n = h2.shape[0]
    h2b_ref[...] = h2.astype(BF16)
    for s in range(TILE_ROWS):
        h2t_ref[pl.ds(s, n, stride=TILE_ROWS), :] = h2[:, s * LANES:(s + 1) * LANES]

    s = _sigmoid(_dot_nt(wr_ref[...], h2.astype(BF16)))
    sb = s + rb_ref[...]
    per = N_EXPERTS // N_GROUPS
    slab = [sb[j * 8:(j + 1) * 8, :] for j in range(per)]
    sraw = [s[j * 8:(j + 1) * 8, :] for j in range(per)]
    giota = lax.broadcasted_iota(jnp.int32, (N_GROUPS, n), 0)
    eid = [giota * per + j for j in range(per)]
    ninf = -jnp.inf

    m1 = functools.reduce(jnp.maximum, slab)
    j1 = functools.reduce(jnp.minimum, [jnp.where(slab[j] == m1, j, per) for j in range(per)])
    m2 = functools.reduce(jnp.maximum, [jnp.where(j1 == j, ninf, slab[j]) for j in range(per)])
    gs = m1 + m2

    gsel = jnp.zeros((N_GROUPS, n), jnp.bool_)
    for _ in range(TOPK_GROUPS):
        m = jnp.max(gs, axis=0, keepdims=True)
        gi = jnp.min(jnp.where(gs == m, giota, N_GROUPS), axis=0, keepdims=True)
        hit = giota == gi
        gsel = jnp.logical_or(gsel, hit)
        gs = jnp.where(hit, ninf, gs)
    slab = [jnp.where(gsel, sj, ninf) for sj in slab]

    def pick(hits, vals):
        return jnp.sum(functools.reduce(jnp.add, [jnp.where(hits[j], vals[j], 0.0) for j in range(per)]),
                       axis=0, keepdims=True)

    kiota = lax.broadcasted_iota(jnp.int32, (TOP_K, n), 0)
    wts = jnp.zeros((TOP_K, n), F32)
    chosen = []
    for k in range(TOP_K):
        m = jnp.max(functools.reduce(jnp.maximum, slab), axis=0, keepdims=True)
        cand = functools.reduce(jnp.minimum,
                                [jnp.where(slab[j] == m, eid[j], N_EXPERTS) for j in range(per)])
        ei = jnp.min(cand, axis=0, keepdims=True)
        hits = [eid[j] == ei for j in range(per)]
        slab = [jnp.where(hits[j], ninf, slab[j]) for j in range(per)]
        wts = jnp.where(kiota == k, pick(hits, sraw), wts)
        chosen.append(ei)
    wt_ref[...] = wts / jnp.sum(wts, axis=0, keepdims=True) * ROUTED_SCALE

    hits_k = [[eid[j] == chosen[k] for j in range(per)] for k in range(TOP_K)]
    sel = jnp.concatenate(
        [functools.reduce(jnp.add, [jnp.where(hits_k[k][j], 1.0, 0.0) for k in range(TOP_K)])
         for j in range(per)], axis=0)
    tri = jnp.where(lax.broadcasted_iota(jnp.int32, (n, n), 0) < lax.broadcasted_iota(jnp.int32, (n, n), 1),
                    1.0, 0.0).astype(BF16)
    before = _dot(sel.astype(BF16), tri)
    cnt = jnp.sum(sel, axis=1, keepdims=True)
    run = jnp.floor((cnt + (PIECE - 1.0)) * (1.0 / PIECE)) * PIECE
    low = jnp.where(lax.broadcasted_iota(jnp.int32, (N_EXPERTS, N_EXPERTS), 1)
                    < lax.broadcasted_iota(jnp.int32, (N_EXPERTS, N_EXPERTS), 0), 1.0, 0.0).astype(BF16)
    start = _dot(low, jnp.broadcast_to(run, (N_EXPERTS, LANES)).astype(BF16))[:, 0:1]
    base = before + start
    base = [base[j * 8:(j + 1) * 8, :] for j in range(per)]
    lsl = jnp.zeros((TOP_K, n), F32)
    for k in range(TOP_K):
        lsl = jnp.where(kiota == k, pick(hits_k[k], base), lsl)
    ls_ref[...] = lsl.astype(jnp.int32)
    cnt_ref[...] = jnp.broadcast_to(cnt, (N_EXPERTS, LANES)).astype(jnp.int32)


def _outproj(mixes, w, xall, mod3, g1, g2, wr, rb, nb):
    bsz = xall.shape[0]
    n_lat = SEQ // ROW_BLK
    full = lambda shape: pl.BlockSpec(shape, lambda b, j: (0,) * len(shape))
    rows = lambda w_: pl.BlockSpec((None, ROW_BLK, w_), lambda b, j: (b, j, 0))
    route = pl.BlockSpec((None, None, TOP_K, ROW_BLK), lambda b, j: (b, j, 0, 0))
    return pl.pallas_call(
        _outproj_kernel,
        grid=(bsz, nb),
        in_specs=[rows(256)] * 4 + [
            full((D_MODEL, D_MODEL)), rows(D_MODEL),
            pl.BlockSpec((None, 1, 6 * D_MODEL), lambda b, j: (_mod_row(j, b, n_lat), 0, 0)),
            full((1, D_MODEL)), full((1, D_MODEL)),
            full((N_EXPERTS, D_MODEL)), full((N_EXPERTS, ROW_BLK)),
        ],
        out_specs=[
            rows(D_MODEL), rows(D_MODEL),
            pl.BlockSpec((None, ROW_BLK * TILE_ROWS, LANES), lambda b, j: (b, j, 0)),
            route, route,
            pl.BlockSpec((None, None, N_EXPERTS, LANES), lambda b, j: (b, j, 0, 0)),
        ],
        out_shape=[
            jax.ShapeDtypeStruct((bsz, nb * ROW_BLK, D_MODEL), F32),
            jax.ShapeDtypeStruct((bsz, nb * ROW_BLK, D_MODEL), BF16),
            jax.ShapeDtypeStruct((bsz, nb * ROW_BLK * TILE_ROWS, LANES), F32),
            jax.ShapeDtypeStruct((bsz, nb, TOP_K, ROW_BLK), F32),
            jax.ShapeDtypeStruct((bsz, nb, TOP_K, ROW_BLK), jnp.int32),
            jax.ShapeDtypeStruct((bsz, nb, N_EXPERTS, LANES), jnp.int32),
        ],
        name="outproj",
    )(*mixes, w, xall, mod3, g1, g2, wr, rb)


def _rows(i, n):
    return pl.ds(pl.multiple_of(i * n, n), n)


def _tile_at(row):
    return pl.ds(pl.multiple_of(row, TILE_ROWS), TILE_ROWS)


def _piece(p):
    return _rows(p, PK_PIECE_ROWS)


def _pack_pairs(lo, hi):
    return (lax.shift_right_logical(pltpu.bitcast(lo, U32), jnp.uint32(16))
            | (pltpu.bitcast(hi, U32) & jnp.uint32(0xFFFF0000)))


def _unpack_lo(u):
    return pltpu.bitcast(lax.shift_left(u, jnp.uint32(16)), F32)


def _unpack_hi(u):
    return pltpu.bitcast(u & jnp.uint32(0xFFFF0000), F32)


def _start_pieces(lo, hi, start):
    n = hi - lo

    def quad(q, carry):
        for r in range(4):
            start(lo + 4 * q + r, r % 2)
        return carry
    lax.fori_loop(0, lax.shift_right_logical(n, 2), quad, 0)

    def rest(r, carry):
        start(hi - 1 - r, 0)
        return carry
    lax.fori_loop(0, n & 3, rest, 0)


def _wait_pieces(n, copy_of_rows):
    def group(p, carry):
        copy_of_rows(WAIT_GROUP * PK_PIECE_ROWS).wait()
        return carry
    lax.fori_loop(0, lax.shift_right_logical(n, LOG_WAIT_GROUP), group, 0)

    def single(p, carry):
        copy_of_rows(PK_PIECE_ROWS).wait()
        return carry
    lax.fori_loop(0, n & (WAIT_GROUP - 1), single, 0)


def _interleave(body, n_pieces, start):
    per = ROW_BLK // N_CHUNKS

    def chunk(c, carry):
        for tt in range(per):
            body(c * per + tt)
        _start_pieces(lax.shift_right_logical(c * n_pieces, LOG_CHUNKS),
                      lax.shift_right_logical((c + 1) * n_pieces, LOG_CHUNKS), start)
        return carry
    lax.fori_loop(0, N_CHUNKS, chunk, 0)


def _dispatch_kernel(np_ref, nu_ref, nt_ref, ls_ref, pd_ref, pdp_ref, td_ref, h_ref, xs_hbm,
                     stg, pk, zblk, sem, zsem, *, n_steps, n_blocks):
    i = pl.program_id(0)
    slot = i % 3
    prev = (i + 2) % 3

    def piece_copy(s, p, d):
        return pltpu.make_async_copy(pk.at[s, _piece(p), :], xs_hbm.at[_piece(d), :], sem.at[s])

    def wait_pieces(s, n):
        _wait_pieces(n, lambda rows: pltpu.make_async_copy(
            pk.at[s, pl.ds(0, rows), :], xs_hbm.at[pl.ds(0, rows), :], sem.at[s]))

    def zero_piece_copy(d):
        return pltpu.make_async_copy(zblk.at[pl.ds(0, PK_PIECE_ROWS), :], xs_hbm.at[_piece(d), :], zsem.at[0])

    def zero_block_copy(b):
        return pltpu.make_async_copy(zblk, xs_hbm.at[_rows(b, PK_BLK_ROWS), :], zsem.at[1])

    @pl.when(i == 0)
    def _():
        zblk[...] = jnp.zeros_like(zblk)

        def zero(c, carry):
            stg[_rows(c, 2048), :] = jnp.zeros((2048, LANES), F32)
            return carry
        lax.fori_loop(0, STG_ROWS // 2048, zero, 0)

    @pl.when(i >= 3)
    def _():
        wait_pieces(slot, np_ref[jnp.maximum(i - 3, 0)])

    def scatter_token(t):
        tile = h_ref[_rows(t, TILE_ROWS), :].astype(BF16).astype(F32)
        for k in range(TOP_K):
            stg[_tile_at(ls_ref[0, k * ROW_BLK + t]), :] = tile

    np_prev = jnp.where(i >= 1, np_ref[jnp.maximum(i - 1, 0)], 0)
    _interleave(scatter_token, np_prev,
                lambda p, prio: piece_copy(prev, p, pdp_ref[0, p]).start(priority=prio))

    def pack_group(q, carry):
        v = stg[_rows(q, PACK_GROUP * PIECE_ROWS), :].reshape(PACK_GROUP * PIECE // 2, 2 * TILE_ROWS, LANES)
        packed = _pack_pairs(v[:, 0:TILE_ROWS, :], v[:, TILE_ROWS:2 * TILE_ROWS, :])
        pk[slot, _rows(q, PACK_GROUP * PK_PIECE_ROWS), :] = packed.reshape(PACK_GROUP * PK_PIECE_ROWS, LANES)
        return carry
    lax.fori_loop(0, lax.shift_right_logical(np_ref[i] + (PACK_GROUP - 1), LOG_PACK_GROUP), pack_group, 0)

    @pl.when(i == n_steps - 1)
    def _():
        _start_pieces(0, np_ref[i], lambda p, prio: piece_copy(slot, p, pd_ref[0, p]).start(priority=prio))

        def tail(p, carry):
            zero_piece_copy(td_ref[0, p]).start()
            return carry
        lax.fori_loop(0, nt_ref[0], tail, 0)

        def blk(b, carry):
            zero_block_copy(b).start()
            return carry
        lax.fori_loop(nu_ref[0], n_blocks, blk, 0)

        wait_pieces((i + 1) % 3, np_ref[jnp.maximum(i - 2, 0)])
        wait_pieces(prev, np_ref[jnp.maximum(i - 1, 0)])
        wait_pieces(slot, np_ref[i])

        def tail_wait(p, carry):
            zero_piece_copy(0).wait()
            return carry
        lax.fori_loop(0, nt_ref[0], tail_wait, 0)

        def blk_wait(b, carry):
            zero_block_copy(0).wait()
            return carry
        lax.fori_loop(nu_ref[0], n_blocks, blk_wait, 0)


def _dispatch(npieces, n_used, ntail, lslot, piece_dst, tail_dst, h2t, n_steps, n_blocks):
    assert n_steps >= 3
    smem = lambda n, imap: pl.BlockSpec((None, 1, n), imap, memory_space=pltpu.SMEM)
    grid_spec = pltpu.PrefetchScalarGridSpec(
        num_scalar_prefetch=3,
        grid=(n_steps,),
        in_specs=[
            smem(TOP_K * ROW_BLK, lambda i, *_: (i, 0, 0)),
            smem(PIECES_MAX, lambda i, *_: (i, 0, 0)),
            smem(PIECES_MAX, lambda i, *_: (jnp.maximum(i - 1, 0), 0, 0)),
            smem(TAIL_MAX, lambda i, *_: (0, 0, 0)),
            pl.BlockSpec((ROW_BLK * TILE_ROWS, LANES), lambda i, *_: (i, 0)),
        ],
        out_specs=pl.BlockSpec(memory_space=pl.ANY),
        scratch_shapes=[
            pltpu.VMEM((STG_ROWS, LANES), F32),
            pltpu.VMEM((3, PK_STG_ROWS, LANES), U32),
            pltpu.VMEM((PK_BLK_ROWS, LANES), U32),
            pltpu.SemaphoreType.DMA((3,)),
            pltpu.SemaphoreType.DMA((2,)),
        ],
    )
    return pl.pallas_call(
        functools.partial(_dispatch_kernel, n_steps=n_steps, n_blocks=n_blocks),
        grid_spec=grid_spec,
        out_shape=jax.ShapeDtypeStruct((n_blocks * PK_BLK_ROWS, LANES), U32),
        compiler_params=pltpu.CompilerParams(dimension_semantics=("arbitrary",),
                                             vmem_limit_bytes=VMEM_LIMIT),
        name="dispatch",
    )(npieces, n_used, ntail, lslot, piece_dst, piece_dst, tail_dst, h2t)


def _from_tiles(ref, n):
    return jnp.concatenate([ref[pl.ds(s, n, stride=TILE_ROWS), :] for s in range(TILE_ROWS)], axis=1)


def _experts_kernel(be_ref, nu_ref, xs_hbm, wg_ref, wu_ref, wd_ref, ys_hbm,
                    xbuf, ybuf, wgb, wub, wdb, xsem, ysem, *, n_blocks):
    i = pl.program_id(0)
    n_used = nu_ref[0]
    rows = PK_BLK_ROWS
    half = rows // 2
    pairs = MOE_BLK // 2

    def halves(b):
        return [pl.ds(pl.multiple_of(b * rows + h * half, half), half) for h in range(2)]

    def x_copies(b, s):
        return [pltpu.make_async_copy(xs_hbm.at[src, :], xbuf.at[s, pl.ds(h * half, half), :], xsem.at[s])
                for h, src in enumerate(halves(b))]

    def y_copies(b, s):
        return [pltpu.make_async_copy(ybuf.at[s, pl.ds(h * half, half), :], ys_hbm.at[dst, :], ysem.at[s])
                for h, dst in enumerate(halves(b))]

    def start(copies):
        for h, c in enumerate(copies):
            c.start(priority=h)

    def wait(copies):
        for c in copies:
            c.wait()

    @pl.when(i == 0)
    def _():
        start(x_copies(0, 0))

        @pl.when(n_used > 1)
        def _():
            start(x_copies(1, 1))

    @pl.when(i + 2 < n_used)
    def _():
        start(x_copies(i + 2, (i + 2) % 3))

    @pl.when(i < n_used)
    def _():
        wait(x_copies(i, i % 3))

        @pl.when(i >= 2)
        def _():
            wait(y_copies(i - 2, i % 2))

        @pl.when(jnp.logical_or(i == 0, be_ref[i] != be_ref[jnp.maximum(i - 1, 0)]))
        def _():
            wgb[...] = wg_ref[...].astype(BF16)
            wub[...] = wu_ref[...].astype(BF16)
            wdb[...] = wd_ref[...].astype(BF16)

        x = pltpu.bitcast(_from_tiles(xbuf.at[i % 3], pairs), BF16)
        g = _dot(x, wgb[...])
        u = _dot(x, wub[...])
        y = _dot((g * _sigmoid(g) * u).astype(BF16), wdb[...])
        yp = pltpu.bitcast(y.astype(BF16), U32)
        yb = ybuf.at[i % 2]
        for s in range(TILE_ROWS):
            yb[pl.ds(s, pairs, stride=TILE_ROWS), :] = yp[:, s * LANES:(s + 1) * LANES]
        start(y_copies(i, i % 2))

        @pl.when(i == n_used - 1)
        def _():
            @pl.when(i >= 1)
            def _():
                wait(y_copies(i - 1, (i + 1) % 2))
            wait(y_copies(i, i % 2))
            ybuf[0] = jnp.zeros((rows, LANES), U32)

            def fill(b, carry):
                start(y_copies(b, 0))
                return carry
            lax.fori_loop(n_used, n_blocks, fill, 0)

            def fill_wait(b, carry):
                wait(y_copies(0, 0))
                return carry
            lax.fori_loop(n_used, n_blocks, fill_wait, 0)


def _experts(block_e, n_used, x_sorted, wg, wu, wd, layer, n_blocks):
    wspec = lambda shape: pl.BlockSpec((None, None) + shape, lambda i, be, nu: (layer, be[i], 0, 0))
    rows = PK_BLK_ROWS
    grid_spec = pltpu.PrefetchScalarGridSpec(
        num_scalar_prefetch=2,
        grid=(n_blocks,),
        in_specs=[
            pl.BlockSpec(memory_space=pl.ANY),
            wspec((D_MODEL, EXPERT_DIM)), wspec((D_MODEL, EXPERT_DIM)), wspec((EXPERT_DIM, D_MODEL)),
        ],
        out_specs=pl.BlockSpec(memory_space=pl.ANY),
        scratch_shapes=[
            pltpu.VMEM((3, rows, LANES), U32),
            pltpu.VMEM((2, rows, LANES), U32),
            pltpu.VMEM((D_MODEL, EXPERT_DIM), BF16),
            pltpu.VMEM((D_MODEL, EXPERT_DIM), BF16),
            pltpu.VMEM((EXPERT_DIM, D_MODEL), BF16),
            pltpu.SemaphoreType.DMA((3,)),
            pltpu.SemaphoreType.DMA((2,)),
        ],
    )
    return pl.pallas_call(
        functools.partial(_experts_kernel, n_blocks=n_blocks),
        grid_spec=grid_spec,
        out_shape=jax.ShapeDtypeStruct((n_blocks * rows, LANES), U32),
        compiler_params=pltpu.CompilerParams(dimension_semantics=("arbitrary",)),
        name="experts",
    )(block_e, n_used, x_sorted, wg, wu, wd)


def _combine_kernel(np_ref, lp_ref, wl_ref, wh_ref, cur_ref, nx1_ref, nx2_ref, y_hbm, h2b_ref, xn_ref,
                    mod_ref, g_ref, swg_ref, swu_ref, swd_ref, o_ref, stg, acc, sem, *, n_steps):
    i = pl.program_id(0)
    slot = i % 3
    d = D_MODEL

    def piece_copy(s, p, src):
        return pltpu.make_async_copy(y_hbm.at[_piece(src), :], stg.at[s, _piece(p), :], sem.at[s])

    def fetcher(s, pd_ref):
        return lambda p, prio: piece_copy(s, p, pd_ref[0, p]).start(priority=prio)

    @pl.when(i == 0)
    def _():
        _start_pieces(0, np_ref[0], fetcher(0, cur_ref))
        _start_pieces(0, np_ref[1], fetcher(1, nx1_ref))

    _wait_pieces(np_ref[i], lambda rows: pltpu.make_async_copy(
        y_hbm.at[pl.ds(0, rows), :], stg.at[slot, pl.ds(0, rows), :], sem.at[slot]))

    def gather_token(t):
        a = None
        for k in range(TOP_K):
            j = k * ROW_BLK + t
            pair = stg[slot, _tile_at(lp_ref[0, j]), :]
            v = _unpack_lo(pair) * wl_ref[0, j] + _unpack_hi(pair) * wh_ref[0, j]
            a = v if a is None else a + v
        acc[_rows(t, TILE_ROWS), :] = a

    np_next = jnp.where(i + 2 < n_steps, np_ref[jnp.minimum(i + 2, n_steps - 1)], 0)
    _interleave(gather_token, np_next, fetcher((i + 2) % 3, nx2_ref))

    hb = h2b_ref[...]
    g = _dot(hb, swg_ref[...])
    u = _dot(hb, swu_ref[...])
    f = _dot((g * _sigmoid(g) * u).astype(BF16), swd_ref[...]) + _from_tiles(acc, ROW_BLK)
    o_ref[...] = xn_ref[...] + mod_ref[:, 5 * d:6 * d] * _rms(f, g_ref[...])


def _combine(npieces, pair_row, w_lo, w_hi, piece_src, y_sorted, h2b_rows, xn_rows, mod3, g, swg, swu, swd,
             n_steps, per_batch):
    assert n_steps >= 3
    last = n_steps - 1
    full = lambda shape: pl.BlockSpec(shape, lambda n, *_: (0,) * len(shape))
    rows = lambda w_: pl.BlockSpec((ROW_BLK, w_), lambda n, *_: (n, 0))
    smem = lambda w_, imap: pl.BlockSpec((None, 1, w_), imap, memory_space=pltpu.SMEM)
    n_lat = SEQ // ROW_BLK
    grid_spec = pltpu.PrefetchScalarGridSpec(
        num_scalar_prefetch=1,
        grid=(n_steps,),
        in_specs=[
            smem(TOP_K * ROW_BLK, lambda n, *_: (n, 0, 0)),
            smem(TOP_K * ROW_BLK, lambda n, *_: (n, 0, 0)),
            smem(TOP_K * ROW_BLK, lambda n, *_: (n, 0, 0)),
            smem(PIECES_MAX, lambda n, *_: (n, 0, 0)),
            smem(PIECES_MAX, lambda n, *_: (jnp.minimum(n + 1, last), 0, 0)),
            smem(PIECES_MAX, lambda n, *_: (jnp.minimum(n + 2, last), 0, 0)),
            pl.BlockSpec(memory_space=pl.ANY),
            rows(D_MODEL), rows(D_MODEL),
            pl.BlockSpec((None, 1, 6 * D_MODEL),
                         lambda n, *_: (_mod_row(n % per_batch, n // per_batch, n_lat), 0, 0)),
            full((1, D_MODEL)),
            full((D_MODEL, SHARED_DIM)), full((D_MODEL, SHARED_DIM)), full((SHARED_DIM, D_MODEL)),
        ],
        out_specs=rows(D_MODEL),
        scratch_shapes=[
            pltpu.VMEM((3, PK_STG_ROWS, LANES), U32),
            pltpu.VMEM((ROW_BLK * TILE_ROWS, LANES), F32),
            pltpu.SemaphoreType.DMA((3,)),
        ],
    )
    return pl.pallas_call(
        functools.partial(_combine_kernel, n_steps=n_steps),
        grid_spec=grid_spec,
        out_shape=jax.ShapeDtypeStruct(xn_rows.shape, F32),
        compiler_params=pltpu.CompilerParams(dimension_semantics=("arbitrary",),
                                             vmem_limit_bytes=VMEM_LIMIT),
        name="combine",
    )(npieces, pair_row, w_lo, w_hi, piece_src, piece_src, piece_src, y_sorted, h2b_rows, xn_rows, mod3, g,
      swg, swu, swd)


def _moe_plan(cnt_rows, n_blocks):
    i32 = jnp.int32
    per_blk = MOE_BLK // PIECE
    to_expert = lambda v: v.reshape(v.shape[:-1] + (8, 8)).swapaxes(-1, -2).reshape(v.shape)
    runs = (cnt_rows + PIECE - 1) // PIECE
    loc_end = jnp.cumsum(runs, axis=1)
    loc = loc_end - runs
    npieces = loc_end[:, -1].astype(i32)
    reg = to_expert(runs.sum(axis=0))
    padded = (reg + per_blk - 1) // per_blk * per_blk
    pad_end = jnp.cumsum(padded)
    pad_start = pad_end - padded
    off = to_expert(pad_start)[None, :] + jnp.cumsum(runs, axis=0) - runs
    p = jnp.arange(PIECES_MAX, dtype=i32)[None, :, None]
    mine = (loc[:, None, :] <= p) & (p < loc_end[:, None, :])
    piece_dst = jnp.sum(jnp.where(mine, (off - loc)[:, None, :], 0), axis=-1) + p[:, :, 0]
    tail_n = padded - reg
    t_end = jnp.cumsum(tail_n)
    t_beg = t_end - tail_n
    q = jnp.arange(TAIL_MAX, dtype=i32)[:, None]
    tmine = (t_beg[None, :] <= q) & (q < t_end[None, :])
    tail_dst = jnp.sum(jnp.where(tmine, (pad_start + reg - t_beg)[None, :], 0), axis=-1) + q[:, 0]
    first = jnp.arange(n_blocks, dtype=i32)[:, None] * per_blk
    block_e = jnp.minimum(jnp.sum((pad_end[None, :] <= first).astype(i32), axis=-1), N_EXPERTS - 1)
    n_used = (pad_end[-1] // per_blk).astype(i32).reshape(1)
    return (npieces, piece_dst.astype(i32).reshape(-1, 1, PIECES_MAX),
            tail_dst.astype(i32).reshape(1, 1, TAIL_MAX), t_end[-1].astype(i32).reshape(1),
            block_e.astype(i32), n_used)


def _axial_tables(dim):
    f32 = np.float32
    t = np.arange(SEQ)
    row = (t // GRID_W).astype(f32)
    col = (t % GRID_W).astype(f32)
    quarter = dim // 4
    freqs = (f32(ROPE_THETA) ** (-np.arange(quarter, dtype=f32) / f32(quarter))).astype(f32)
    ar, ac = row[:, None] * freqs, col[:, None] * freqs
    ang = np.concatenate([ar, ar, ac, ac], axis=-1)
    sign = np.where((np.arange(dim) // quarter) % 2 == 0, -1.0, 1.0).astype(f32)
    cos = np.concatenate([np.cos(ang), np.ones((CTX_LEN, dim), f32)], axis=0)
    sin = np.concatenate([np.sin(ang) * sign, np.zeros((CTX_LEN, dim), f32)], axis=0)
    return cos.astype(f32), sin.astype(f32)


def _rope_tables():
    f32 = np.float32
    cos_h, sin_h = _axial_tables(HEAD_DIM)
    cos_r, sin_r = _axial_tables(B_ROPE)
    cosh, sinh = np.tile(cos_h, (1, 4)), np.tile(sin_h, (1, 4))
    scale_b = f32((B_NOPE + B_ROPE) ** -0.5)
    ones, zeros = np.ones((TOK, 64), f32), np.zeros((TOK, 64), f32)
    pad1, pad0 = np.ones((TOK, 32), f32), np.zeros((TOK, 32), f32)
    cosb = np.tile(np.concatenate([ones, cos_r, pad1], axis=1) * scale_b, (1, 4))
    sinb = np.tile(np.concatenate([zeros, sin_r, pad0], axis=1) * scale_b, (1, 4))
    cosr = np.concatenate([cos_r, np.ones((TOK, 96), f32)], axis=1)
    sinr = np.concatenate([sin_r, np.zeros((TOK, 96), f32)], axis=1)
    return tuple(jnp.asarray(a, F32) for a in (cosh, sinh, cosb, sinb, cosr, sinr))


_GQA_ORDER = (0, 2, 1, 3)


def _head_cols(base, order, width=HEAD_DIM):
    return np.concatenate([base + np.arange(h * width, (h + 1) * width) for h in order])


def _w_in_layout(w_in):
    zeros = lambda n: jnp.zeros((D_MODEL, n), w_in.dtype)
    a_q = w_in[:, _head_cols(0, _GQA_ORDER)]
    a_kv = w_in[:, 256:512]
    b_q = jnp.concatenate(
        [jnp.concatenate([w_in[:, 512 + h * 96:512 + (h + 1) * 96], zeros(32)], axis=1)
         for h in range(B_HEADS)], axis=1)
    b_c = w_in[:, 896:1024]
    b_r = jnp.concatenate([w_in[:, 1024:1056], zeros(96)], axis=1)
    c_q = w_in[:, _head_cols(1056, _GQA_ORDER)]
    rest = w_in[:, 1312:2336]
    return jnp.concatenate([a_q, a_kv, b_q, b_c, b_r, c_q, rest], axis=1).astype(BF16)


def _w_kv_layout(w_kv_up):
    zeros = jnp.zeros((B_KV_RANK, 64), w_kv_up.dtype)
    nope = [jnp.concatenate([w_kv_up[:, h * 128:h * 128 + 64], zeros], axis=1) for h in range(B_HEADS)]
    val = [w_kv_up[:, h * 128 + 64:(h + 1) * 128] for h in range(B_HEADS)]
    return jnp.concatenate(nope + val, axis=1).astype(BF16)


def _w_out_layout(w_out):
    rows = np.concatenate([_head_cols(0, _GQA_ORDER), np.arange(256, 512),
                           _head_cols(512, _GQA_ORDER), np.arange(768, 1024)])
    return w_out[rows].astype(BF16)


_ROUTER_ROWS = np.array([g * 8 + j for j in range(8) for g in range(8)])


def kernel(x, c, ctx, c_ctx, ada_w, ada_b, pre_mix_g, post_mix_g, pre_ffn_g, post_ffn_g, w_in, a_q_norm, a_k_norm, b_kv_norm, b_w_kv_up, c_sink, d_rpb, w_out, router_w, router_b, exp_w_gate, exp_w_up, exp_w_down, sh_w_gate, sh_w_up, sh_w_down):
    bsz = x.shape[0]
    depth = ada_w.shape[0]
    xall = jnp.concatenate([x, ctx], axis=1)

    cvec = jnp.zeros((16, D_MODEL), F32).at[:bsz].set(c).at[8].set(c_ctx)
    mod = _ada(cvec, ada_w, ada_b)
    tabs = _rope_tables()
    avg = jnp.asarray(np.kron(np.eye(4, dtype=np.float32), np.full((64, 64), 1.0 / 64, np.float32)), BF16)
    row2 = lambda v: v.reshape(1, -1).astype(F32)

    for l in range(depth):
        with_ctx = l < depth - 1
        mod3 = mod[l].reshape(16, 1, 6 * D_MODEL)

        proj = _inproj(
            xall, mod3, row2(pre_mix_g[l]), _w_in_layout(w_in[l]),
            row2(jnp.tile(a_q_norm[l], 4) * HEAD_DIM ** -0.5), row2(jnp.tile(a_k_norm[l], 2)),
            row2(b_kv_norm[l]), _w_kv_layout(b_w_kv_up[l]), avg, tabs)
        qa, ka, va, qb, kb, vb, qc, kc, vc, qd, kd, vd = proj

        nq = NQ + (CTX_LEN // QBLK if with_ctx else 0)
        full = lambda shape: pl.BlockSpec(shape, lambda b, i: (0,) * len(shape))
        mix_a = _attn_call(_attn_a_kernel, "attn_a", qa, ka, va, [], [], nq // 2, qblk=2 * QBLK)
        mix_b = _attn_call(_attn_b_kernel, "attn_b", qb, kb, vb, [], [], nq // 2, qblk=2 * QBLK)
        sink = jnp.zeros((8, LANES), F32).at[:4].set(
            jnp.broadcast_to(c_sink[l][np.array(_GQA_ORDER)][:, None], (4, LANES)))
        mix_c = _attn_call(_attn_c_kernel, "attn_c", qc, kc, vc, [sink], [full((8, LANES))], nq)
        bias = _na_bias(d_rpb[l])
        mix_d = _attn_call(
            _attn_d_kernel, "attn_d", qd, kd, vd, [bias],
            [pl.BlockSpec((None, D_HEADS, QBLK, NA_WIN), lambda b, i: (_na_pattern(i), 0, 0, 0))], nq)

        nb = (TOK if with_ctx else SEQ) // ROW_BLK
        rb = jnp.broadcast_to(router_b[l][_ROUTER_ROWS][:, None], (N_EXPERTS, ROW_BLK)).astype(F32)
        xn, h2b, h2t, wt_t, ls_t, cnt_t = _outproj(
            (mix_a, mix_b, mix_c, mix_d), _w_out_layout(w_out[l]), xall, mod3,
            row2(post_mix_g[l]), row2(pre_ffn_g[l]),
            router_w[l].T[_ROUTER_ROWS].astype(BF16), rb, nb)

        t_used = nb * ROW_BLK
        n_tok = bsz * t_used
        n_steps = bsz * nb
        n_blocks = -(-(n_tok * TOP_K + n_steps * N_EXPERTS * (PIECE - 1) + N_EXPERTS * (MOE_BLK - 1))
                     // MOE_BLK)
        npieces, piece_dst, tail_dst, ntail, block_e, n_used = _moe_plan(
            cnt_t[..., 0].reshape(n_steps, N_EXPERTS), n_blocks)
        lslot = ls_t.reshape(n_steps, 1, TOP_K * ROW_BLK)
        wts = wt_t.reshape(n_steps, 1, TOP_K * ROW_BLK)
        stg_row = lslot * TILE_ROWS
        pair_row = (lslot // 2) * TILE_ROWS
        w_lo = jnp.where(lslot % 2 == 0, wts, 0.0)
        w_hi = wts - w_lo

        x_sorted = _dispatch(npieces, n_used, ntail, stg_row, piece_dst, tail_dst,
                             h2t.reshape(n_tok * TILE_ROWS, LANES), n_steps, n_blocks)
        y_sorted = _experts(block_e, n_used, x_sorted, exp_w_gate, exp_w_up, exp_w_down, l, n_blocks)
        xall = _combine(
            npieces, pair_row, w_lo, w_hi, piece_dst, y_sorted, h2b.reshape(n_tok, D_MODEL),
            xn.reshape(n_tok, D_MODEL), mod3, row2(post_ffn_g[l]),
            sh_w_gate[l].astype(BF16), sh_w_up[l].astype(BF16), sh_w_down[l].astype(BF16),
            n_steps, nb).reshape(bsz, t_used, D_MODEL)

    return xall[:, :SEQ]
```

```python
import functools

import numpy as np
import jax
import jax.numpy as jnp
from jax import lax
from jax.experimental import pallas as pl
from jax.experimental.pallas import tpu as pltpu

F32 = jnp.float32
BF16 = jnp.bfloat16
U32 = jnp.uint32

D_MODEL = 1024
SEQ = 2048
CTX_LEN = 256
TOK = SEQ + CTX_LEN
GRID_W = 64
HEAD_DIM = 64
ROPE_THETA = 10000.0
EPS = 1e-6
NEG = -1e30

A_HEADS, A_KV_HEADS = 4, 2
B_HEADS, B_NOPE, B_ROPE, B_V, B_KV_RANK = 4, 64, 32, 64, 128
C_HEADS, C_KV_HEADS, C_WINDOW = 4, 2, 128
D_HEADS, NA_ROWS, NA_COLS = 4, 8, 16

N_EXPERTS, TOP_K, N_GROUPS, TOPK_GROUPS = 64, 8, 8, 4
EXPERT_DIM, SHARED_DIM = 256, 256
ROUTED_SCALE = 2.5

LANES = 128
QBLK = 128
ROW_BLK = 256
MOE_BLK = 512
TILE_ROWS = D_MODEL // LANES
PIECE = 8
PIECE_ROWS = PIECE * TILE_ROWS
PIECES_MAX = 320
STG_ROWS = PIECES_MAX * PIECE_ROWS
PK_PIECE_ROWS = PIECE_ROWS // 2
PK_STG_ROWS = STG_ROWS // 2
PK_BLK_ROWS = MOE_BLK * TILE_ROWS // 2
TAIL_MAX = N_EXPERTS * MOE_BLK // PIECE
LOG_PACK_GROUP = 2
PACK_GROUP = 1 << LOG_PACK_GROUP
LOG_WAIT_GROUP = 4
WAIT_GROUP = 1 << LOG_WAIT_GROUP
LOG_CHUNKS = 4
N_CHUNKS = 1 << LOG_CHUNKS
VMEM_LIMIT = 56 * 1024 * 1024
NQ = SEQ // QBLK
NA_WIN = 10 * GRID_W

_AQ, _AK, _AV, _BQ, _BC, _BR = 0, 256, 384, 512, 1024, 1152
_CQ, _CK, _CV, _DQ, _DK, _DV = 1280, 1536, 1664, 1792, 2048, 2304
IN_W = 2560

_NT = (((1,), (1,)), ((), ()))


def _sigmoid(x):
    return 1.0 / (1.0 + jnp.exp(-x))


def _rms(x, g):
    return x * lax.rsqrt(jnp.mean(x * x, axis=-1, keepdims=True) + EPS) * g


def _dot(a, b):
    return jnp.dot(a, b, preferred_element_type=F32)


def _dot_nt(a, b):
    return lax.dot_general(a, b, _NT, preferred_element_type=F32)


def _ada_kernel(c_ref, w_ref, b_ref, o_ref):
    c = c_ref[...]
    a = (c * _sigmoid(c)).astype(BF16)
    o_ref[...] = _dot(a, w_ref[...].astype(BF16)) + b_ref[...]


def _ada(cvec, ada_w, ada_b):
    depth = ada_w.shape[0]
    n = ada_w.shape[2]
    tn = 1536
    return pl.pallas_call(
        _ada_kernel,
        grid=(depth, n // tn),
        in_specs=[
            pl.BlockSpec((16, D_MODEL), lambda l, j: (0, 0)),
            pl.BlockSpec((None, D_MODEL, tn), lambda l, j: (l, 0, j)),
            pl.BlockSpec((None, 1, tn), lambda l, j: (l, 0, j)),
        ],
        out_specs=pl.BlockSpec((None, 16, tn), lambda l, j: (l, 0, j)),
        out_shape=jax.ShapeDtypeStruct((depth, 16, n), F32),
        name="ada",
    )(cvec, ada_w, ada_b.reshape(depth, 1, n))


def _rope(x, cos, sin, chunk):
    outs = []
    for c in range(x.shape[1] // LANES):
        sl = slice(c * LANES, (c + 1) * LANES)
        xs = x[:, sl]
        lane = lax.broadcasted_iota(jnp.int32, xs.shape, 1)
        first = (lane & chunk) == 0
        rot = jnp.where(first, pltpu.roll(xs, LANES - chunk, 1), pltpu.roll(xs, chunk, 1))
        outs.append(xs * cos[:, sl] + rot * sin[:, sl])
    return outs[0] if len(outs) == 1 else jnp.concatenate(outs, axis=1)


def _head_norm(x, avg):
    sq = x * x
    hi = sq.astype(BF16)
    lo = (sq - hi.astype(F32)).astype(BF16)
    ms = _dot(hi, avg) + _dot(lo, avg)
    return x * lax.rsqrt(ms + EPS)


def _inproj_kernel(x_ref, mod_ref, g_ref, w_ref, qg_ref, kg_ref, kvg_ref, wkv_ref, avg_ref,
                   cosh_ref, sinh_ref, cosb_ref, sinb_ref, cosr_ref, sinr_ref,
                   qa_ref, ka_ref, va_ref, qb_ref, kb_ref, vb_ref,
                   qc_ref, kc_ref, vc_ref, qd_ref, kd_ref, vd_ref):
    d = D_MODEL
    h = _rms(x_ref[...], g_ref[...]) * (1.0 + mod_ref[:, d:2 * d]) + mod_ref[:, 0:d]
    p = _dot(h.astype(BF16), w_ref[...])
    cosh, sinh = cosh_ref[...], sinh_ref[...]
    avg = avg_ref[...]

    qa = _head_norm(p[:, _AQ:_AQ + 256], avg) * qg_ref[...]
    qa_ref[...] = _rope(qa, cosh, sinh, 16).astype(BF16)
    ka = _head_norm(p[:, _AK:_AK + 128], avg[0:128, 0:128]) * kg_ref[...]
    ka_ref[...] = _rope(ka, cosh[:, 0:128], sinh[:, 0:128], 16).astype(BF16)
    ones = jnp.ones((p.shape[0], LANES), BF16)
    va_ref[...] = jnp.concatenate([p[:, _AV:_AV + 128].astype(BF16), ones], axis=1)

    qb_ref[...] = _rope(p[:, _BQ:_BQ + 512], cosb_ref[...], sinb_ref[...], 8).astype(BF16)
    cn = _rms(p[:, _BC:_BC + 128], kvg_ref[...])
    kv = _dot(cn.astype(BF16), wkv_ref[...])
    kr = _rope(p[:, _BR:_BR + 128], cosr_ref[...], sinr_ref[...], 8)
    kr = pltpu.roll(kr, 64, 1)
    kb_ref[...] = jnp.concatenate(
        [kv[:, hh * 128:(hh + 1) * 128] + kr for hh in range(B_HEADS)], axis=1).astype(BF16)
    vb_ref[...] = jnp.concatenate(
        [kv[:, 512:640].astype(BF16), ones, kv[:, 640:768].astype(BF16), ones], axis=1)

    qc_ref[...] = _rope(p[:, _CQ:_CQ + 256] * 0.125, cosh, sinh, 16).astype(BF16)
    kc_ref[...] = _rope(p[:, _CK:_CK + 128], cosh[:, 0:128], sinh[:, 0:128], 16).astype(BF16)
    vc_ref[...] = jnp.concatenate([p[:, _CV:_CV + 128].astype(BF16), ones], axis=1)

    qd_ref[...] = (p[:, _DQ:_DQ + 256] * 0.125).astype(BF16)
    kd_ref[...] = p[:, _DK:_DK + 256].astype(BF16)
    vd_ref[...] = jnp.concatenate(
        [p[:, _DV:_DV + 128].astype(BF16), ones, p[:, _DV + 128:_DV + 256].astype(BF16), ones], axis=1)


def _mod_row(j, b, n_lat_blocks):
    return jnp.where(j < n_lat_blocks, b, 8)


def _inproj(xall, mod3, g, w, qg, kg, kvg, wkv, avg, tabs):
    bsz = xall.shape[0]
    nb = TOK // ROW_BLK
    n_lat = SEQ // ROW_BLK
    full = lambda shape: pl.BlockSpec(shape, lambda b, j: (0,) * len(shape))
    row = lambda w_: pl.BlockSpec((ROW_BLK, w_), lambda b, j: (j, 0))
    out = lambda w_: pl.BlockSpec((None, ROW_BLK, w_), lambda b, j: (b, j, 0))
    widths = (256, 128, 256, 512, 512, 512, 256, 128, 256, 256, 256, 512)
    return pl.pallas_call(
        _inproj_kernel,
        grid=(bsz, nb),
        in_specs=[
            pl.BlockSpec((None, ROW_BLK, D_MODEL), lambda b, j: (b, j, 0)),
            pl.BlockSpec((None, 1, 6 * D_MODEL), lambda b, j: (_mod_row(j, b, n_lat), 0, 0)),
            full((1, D_MODEL)), full((D_MODEL, IN_W)),
            full((1, 256)), full((1, 128)), full((1, 128)), full((128, 768)), full((256, 256)),
            row(256), row(256), row(512), row(512), row(128), row(128),
        ],
        out_specs=[out(w_) for w_ in widths],
        out_shape=[jax.ShapeDtypeStruct((bsz, TOK, w_), BF16) for w_ in widths],
        name="inproj",
    )(xall, mod3, g, w, qg, kg, kvg, wkv, avg, *tabs)


def _softmax_pv(segs, sink=None):
    m = None
    for s, _ in segs:
        ms = jnp.max(s, axis=-1, keepdims=True)
        m = ms if m is None else jnp.maximum(m, ms)
    if sink is not None:
        m = jnp.maximum(m, sink)
    acc = None
    for s, v1 in segs:
        r = _dot(jnp.exp((s - m).astype(BF16)), v1)
        acc = r if acc is None else acc + r
    l = acc[:, LANES:LANES + 1]
    if sink is not None:
        l = l + jnp.exp(sink - m)
    return acc[:, 0:LANES] / l


def _lane_lo(dtype_shape):
    return lax.broadcasted_iota(jnp.int32, dtype_shape, 1) < HEAD_DIM


def _half(qt, hh):
    lo = _lane_lo(qt.shape)
    return jnp.where(lo if hh == 0 else jnp.logical_not(lo), qt, jnp.zeros_like(qt))


def _merge_halves(o0, o1):
    return jnp.where(_lane_lo(o0.shape), o0, o1)


def _stack_heads(q_ref, tiles):
    return jnp.concatenate(
        [_half(q_ref[:, t * LANES:(t + 1) * LANES], hh) for t in tiles for hh in range(2)], axis=0)


def _store_heads(o_ref, o, tiles):
    n = o_ref.shape[0]
    for ti, t in enumerate(tiles):
        o0, o1 = o[2 * ti * n:(2 * ti + 1) * n], o[(2 * ti + 1) * n:(2 * ti + 2) * n]
        o_ref[:, t * LANES:(t + 1) * LANES] = _merge_halves(o0, o1).astype(o_ref.dtype)


def _attn_a_kernel(q_ref, k_ref, v_ref, o_ref, *, n_lat):
    def run(k, v1):
        for t in range(2):
            o = [_softmax_pv([(_dot_nt(_half(q_ref[:, t * LANES:(t + 1) * LANES], hh), k), v1)])
                 for hh in range(2)]
            o_ref[:, t * LANES:(t + 1) * LANES] = _merge_halves(o[0], o[1]).astype(o_ref.dtype)

    @pl.when(pl.program_id(1) < n_lat)
    def _():
        run(k_ref[...], v_ref[...])

    @pl.when(pl.program_id(1) >= n_lat)
    def _():
        run(k_ref[SEQ:TOK, :], v_ref[SEQ:TOK, :])


def _attn_b_kernel(q_ref, k_ref, v_ref, o_ref, *, n_lat):
    def run(lo_, hi_):
        for t in range(2):
            v1 = v_ref[lo_:hi_, t * 256:(t + 1) * 256]
            o = []
            for hh in range(2):
                hd = 2 * t + hh
                q = q_ref[:, hd * LANES:(hd + 1) * LANES]
                k = k_ref[lo_:hi_, hd * LANES:(hd + 1) * LANES]
                o.append(_softmax_pv([(_dot_nt(q, k), v1)]))
            o_ref[:, t * LANES:(t + 1) * LANES] = _merge_halves(o[0], o[1]).astype(o_ref.dtype)

    @pl.when(pl.program_id(1) < n_lat)
    def _():
        run(0, TOK)

    @pl.when(pl.program_id(1) >= n_lat)
    def _():
        run(SEQ, TOK)


def _attn_c_kernel(q_ref, k_ref, v_ref, sink_ref, o_ref, *, n_lat):
    i = pl.program_id(1)
    qblk = o_ref.shape[0]
    band = 3 * qblk

    def run(latent):
        if latent:
            start = pl.multiple_of(jnp.clip((i - 1) * qblk, 0, SEQ - band), qblk)
            shape = (2 * qblk, band)
            qpos = i * qblk + (lax.broadcasted_iota(jnp.int32, shape, 0) & (qblk - 1))
            kpos = start + lax.broadcasted_iota(jnp.int32, shape, 1)
            valid = jnp.abs(qpos - kpos) <= C_WINDOW
        for t in range(2):
            q = _stack_heads(q_ref, (t,))
            segs = [(_dot_nt(q, k_ref[SEQ:TOK, :]), v_ref[SEQ:TOK, :])]
            if latent:
                s = jnp.where(valid, _dot_nt(q, k_ref[pl.ds(start, band), :]), NEG)
                segs.insert(0, (s, v_ref[pl.ds(start, band), :]))
            sink = jnp.concatenate(
                [jnp.broadcast_to(sink_ref[j:j + 1, 0:1], (qblk, 1)) for j in (2 * t, 2 * t + 1)], axis=0)
            _store_heads(o_ref, _softmax_pv(segs, sink=sink), (t,))

    @pl.when(i < n_lat)
    def _():
        run(True)

    @pl.when(i >= n_lat)
    def _():
        run(False)


def _attn_d_kernel(q_ref, k_ref, v_ref, bias_ref, o_ref, *, n_lat):
    i = pl.program_id(1)
    qblk = o_ref.shape[0]

    def run(latent):
        if latent:
            start = pl.multiple_of(jnp.clip((i - 2) * qblk, 0, SEQ - NA_WIN), qblk)
        for t in range(2):
            ksl, vsl = slice(t * LANES, (t + 1) * LANES), slice(t * 256, (t + 1) * 256)
            q = _stack_heads(q_ref, (t,))
            segs = [(_dot_nt(q, k_ref[SEQ:TOK, ksl]), v_ref[SEQ:TOK, vsl])]
            if latent:
                bias = bias_ref[2 * t:2 * t + 2].reshape(2 * qblk, NA_WIN)
                segs.insert(0, (_dot_nt(q, k_ref[pl.ds(start, NA_WIN), ksl]) + bias,
                                v_ref[pl.ds(start, NA_WIN), vsl]))
            _store_heads(o_ref, _softmax_pv(segs), (t,))

    @pl.when(i < n_lat)
    def _():
        run(True)

    @pl.when(i >= n_lat)
    def _():
        run(False)


def _attn_call(kernel, name, q, k, v, extra, extra_specs, n_blocks, qblk=QBLK):
    bsz = q.shape[0]
    return pl.pallas_call(
        functools.partial(kernel, n_lat=SEQ // qblk),
        grid=(bsz, n_blocks),
        in_specs=[
            pl.BlockSpec((None, qblk, q.shape[2]), lambda b, i: (b, i, 0)),
            pl.BlockSpec((None, TOK, k.shape[2]), lambda b, i: (b, 0, 0)),
            pl.BlockSpec((None, TOK, v.shape[2]), lambda b, i: (b, 0, 0)),
        ] + extra_specs,
        out_specs=pl.BlockSpec((None, qblk, 256), lambda b, i: (b, i, 0)),
        out_shape=jax.ShapeDtypeStruct((bsz, n_blocks * qblk, 256), BF16),
        name=name,
    )(q, k, v, *extra)


def _na_pattern(i):
    return jnp.where(i < 2, i, jnp.where(i < NQ - 2, 2, jnp.minimum(i, NQ - 1) - (NQ - 5)))


def _na_bias(rpb):
    rows = SEQ // GRID_W
    n_dr, n_dc = 2 * NA_ROWS - 1, 2 * NA_COLS - 1
    qc = np.arange(GRID_W)
    dc = np.clip(qc[None, :] - qc[:, None] + NA_COLS - 1, 0, n_dc - 1)
    sel_c = jnp.asarray(np.eye(n_dc, dtype=np.float32)[dc])
    cs = np.clip(qc - NA_COLS // 2, 0, GRID_W - NA_COLS)[:, None]
    col_ok = (qc[None, :] >= cs) & (qc[None, :] < cs + NA_COLS)
    out = []
    for i in (0, 1, 2, NQ - 2, NQ - 1):
        start_row = min(max(2 * i - 4, 0), rows - 10)
        qr = 2 * i + np.arange(2)
        kr = start_row + np.arange(10)
        rs = np.clip(qr - NA_ROWS // 2, 0, rows - NA_ROWS)[:, None]
        row_ok = (kr[None] >= rs) & (kr[None] < rs + NA_ROWS)
        dr = np.clip(kr[None] - qr[:, None] + NA_ROWS - 1, 0, n_dr - 1)
        sel_r = jnp.asarray(np.eye(n_dr, dtype=np.float32)[dr])
        b = jnp.einsum("hrc,abr,qkc->haqbk", rpb.astype(F32), sel_r, sel_c,
                       precision=lax.Precision.HIGHEST)
        valid = row_ok[:, None, :, None] & col_ok[None, :, None, :]
        out.append(jnp.where(valid[None], b, NEG).reshape(D_HEADS, QBLK, NA_WIN))
    return jnp.stack(out)


def _outproj_kernel(ma_ref, mb_ref, mc_ref, md_ref, w_ref, x_ref, mod_ref, g1_ref, g2_ref,
                    wr_ref, rb_ref, xn_ref, h2b_ref, h2t_ref, wt_ref, ls_ref, cnt_ref):
    d = D_MODEL
    mix = jnp.concatenate([ma_ref[...], mb_ref[...], mc_ref[...], md_ref[...]], axis=1)
    y = _rms(_dot(mix, w_ref[...]), g1_ref[...])
    xn = x_ref[...] + mod_ref[:, 2 * d:3 * d] * y
    xn_ref[...] = xn
    h2 = _rms(xn, g2_ref[...]) * (1.0 + mod_ref[:, 4 * d:5 * d]) + mod_ref[:, 3 * d:4 * d]
    n = h2.shape[0]
    h2b_ref[...] = h2.astype(BF16)
    for s in range(TILE_ROWS):
        h2t_ref[pl.ds(s, n, stride=TILE_ROWS), :] = h2[:, s * LANES:(s + 1) * LANES]

    s = _sigmoid(_dot_nt(wr_ref[...], h2.astype(BF16)))
    sb = s + rb_ref[...]
    per = N_EXPERTS // N_GROUPS
    slab = [sb[j * 8:(j + 1) * 8, :] for j in range(per)]
    sraw = [s[j * 8:(j + 1) * 8, :] for j in range(per)]
    giota = lax.broadcasted_iota(jnp.int32, (N_GROUPS, n), 0)
    eid = [giota * per + j for j in range(per)]
    ninf = -jnp.inf

    m1 = functools.reduce(jnp.maximum, slab)
    j1 = functools.reduce(jnp.minimum, [jnp.where(slab[j] == m1, j, per) for j in range(per)])
    m2 = functools.reduce(jnp.maximum, [jnp.where(j1 == j, ninf, slab[j]) for j in range(per)])
    gs = m1 + m2

    gsel = jnp.zeros((N_GROUPS, n), jnp.bool_)
    for _ in range(TOPK_GROUPS):
        m = jnp.max(gs, axis=0, keepdims=True)
        gi = jnp.min(jnp.where(gs == m, giota, N_GROUPS), axis=0, keepdims=True)
        hit = giota == gi
        gsel = jnp.logical_or(gsel, hit)
        gs = jnp.where(hit, ninf, gs)
    slab = [jnp.where(gsel, sj, ninf) for sj in slab]

    def pick(hits, vals):
        return jnp.sum(functools.reduce(jnp.add, [jnp.where(hits[j], vals[j], 0.0) for j in range(per)]),
                       axis=0, keepdims=True)

    kiota = lax.broadcasted_iota(jnp.int32, (TOP_K, n), 0)
    wts = jnp.zeros((TOP_K, n), F32)
    chosen = []
    for k in range(TOP_K):
        m = jnp.max(functools.reduce(jnp.maximum, slab), axis=0, keepdims=True)
        cand = functools.reduce(jnp.minimum,
                                [jnp.where(slab[j] == m, eid[j], N_EXPERTS) for j in range(per)])
        ei = jnp.min(cand, axis=0, keepdims=True)
        hits = [eid[j] == ei for j in range(per)]
        slab = [jnp.where(hits[j], ninf, slab[j]) for j in range(per)]
        wts = jnp.where(kiota == k, pick(hits, sraw), wts)
        chosen.append(ei)
    wt_ref[...] = wts / jnp.sum(wts, axis=0, keepdims=True) * ROUTED_SCALE

    hits_k = [[eid[j] == chosen[k] for j in range(per)] for k in range(TOP_K)]
    sel = jnp.concatenate(
        [functools.reduce(jnp.add, [jnp.where(hits_k[k][j], 1.0, 0.0) for k in range(TOP_K)])
         for j in range(per)], axis=0)
    tri = jnp.where(lax.broadcasted_iota(jnp.int32, (n, n), 0) < lax.broadcasted_iota(jnp.int32, (n, n), 1),
                    1.0, 0.0).astype(BF16)
    before = _dot(sel.astype(BF16), tri)
    cnt = jnp.sum(sel, axis=1, keepdims=True)
    run = jnp.floor((cnt + (PIECE - 1.0)) * (1.0 / PIECE)) * PIECE
    low = jnp.where(lax.broadcasted_iota(jnp.int32, (N_EXPERTS, N_EXPERTS), 1)
                    < lax.broadcasted_iota(jnp.int32, (N_EXPERTS, N_EXPERTS), 0), 1.0, 0.0).astype(BF16)
    start = _dot(low, jnp.broadcast_to(run, (N_EXPERTS, LANES)).astype(BF16))[:, 0:1]
    base = before + start
    base = [base[j * 8:(j + 1) * 8, :] for j in range(per)]
    lsl = jnp.zeros((TOP_K, n), F32)
    for k in range(TOP_K):
        lsl = jnp.where(kiota == k, pick(hits_k[k], base), lsl)
    ls_ref[...] = lsl.astype(jnp.int32)
    cnt_ref[...] = jnp.broadcast_to(cnt, (N_EXPERTS, LANES)).astype(jnp.int32)


def _outproj(mixes, w, xall, mod3, g1, g2, wr, rb, nb):
    bsz = xall.shape[0]
    n_lat = SEQ // ROW_BLK
    full = lambda shape: pl.BlockSpec(shape, lambda b, j: (0,) * len(shape))
    rows = lambda w_: pl.BlockSpec((None, ROW_BLK, w_), lambda b, j: (b, j, 0))
    route = pl.BlockSpec((None, None, TOP_K, ROW_BLK), lambda b, j: (b, j, 0, 0))
    return pl.pallas_call(
        _outproj_kernel,
        grid=(bsz, nb),
        in_specs=[rows(256)] * 4 + [
            full((D_MODEL, D_MODEL)), rows(D_MODEL),
            pl.BlockSpec((None, 1, 6 * D_MODEL), lambda b, j: (_mod_row(j, b, n_lat), 0, 0)),
            full((1, D_MODEL)), full((1, D_MODEL)),
            full((N_EXPERTS, D_MODEL)), full((N_EXPERTS, ROW_BLK)),
        ],
        out_specs=[
            rows(D_MODEL), rows(D_MODEL),
            pl.BlockSpec((None, ROW_BLK * TILE_ROWS, LANES), lambda b, j: (b, j, 0)),
            route, route,
            pl.BlockSpec((None, None, N_EXPERTS, LANES), lambda b, j: (b, j, 0, 0)),
        ],
        out_shape=[
            jax.ShapeDtypeStruct((bsz, nb * ROW_BLK, D_MODEL), F32),
            jax.ShapeDtypeStruct((bsz, nb * ROW_BLK, D_MODEL), BF16),
            jax.ShapeDtypeStruct((bsz, nb * ROW_BLK * TILE_ROWS, LANES), F32),
            jax.ShapeDtypeStruct((bsz, nb, TOP_K, ROW_BLK), F32),
            jax.ShapeDtypeStruct((bsz, nb, TOP_K, ROW_BLK), jnp.int32),
            jax.ShapeDtypeStruct((bsz, nb, N_EXPERTS, LANES), jnp.int32),
        ],
        name="outproj",
    )(*mixes, w, xall, mod3, g1, g2, wr, rb)


def _rows(i, n):
    return pl.ds(pl.multiple_of(i * n, n), n)


def _tile_at(row):
    return pl.ds(pl.multiple_of(row, TILE_ROWS), TILE_ROWS)


def _piece(p):
    return _rows(p, PK_PIECE_ROWS)


def _pack_pairs(lo, hi):
    return (lax.shift_right_logical(pltpu.bitcast(lo, U32), jnp.uint32(16))
            | (pltpu.bitcast(hi, U32) & jnp.uint32(0xFFFF0000)))


def _unpack_lo(u):
    return pltpu.bitcast(lax.shift_left(u, jnp.uint32(16)), F32)


def _unpack_hi(u):
    return pltpu.bitcast(u & jnp.uint32(0xFFFF0000), F32)


def _start_pieces(lo, hi, start):
    n = hi - lo

    def quad(q, carry):
        for r in range(4):
            start(lo + 4 * q + r, r % 2)
        return carry
    lax.fori_loop(0, lax.shift_right_logical(n, 2), quad, 0)

    def rest(r, carry):
        start(hi - 1 - r, 0)
        return carry
    lax.fori_loop(0, n & 3, rest, 0)


def _wait_pieces(n, piece_rows, copy_of_rows):
    def group(p, carry):
        copy_of_rows(WAIT_GROUP * piece_rows).wait()
        return carry
    lax.fori_loop(0, lax.shift_right_logical(n, LOG_WAIT_GROUP), group, 0)

    def single(p, carry):
        copy_of_rows(piece_rows).wait()
        return carry
    lax.fori_loop(0, n & (WAIT_GROUP - 1), single, 0)


def _interleave(body, n_pieces, start):
    per = ROW_BLK // N_CHUNKS

    def chunk(c, carry):
        for tt in range(per):
            body(c * per + tt)
        _start_pieces(lax.shift_right_logical(c * n_pieces, LOG_CHUNKS),
                      lax.shift_right_logical((c + 1) * n_pieces, LOG_CHUNKS), start)
        return carry
    lax.fori_loop(0, N_CHUNKS, chunk, 0)


def _dispatch_kernel(np_ref, nu_ref, nt_ref, ls_ref, pd_ref, pdp_ref, td_ref, h_ref, xs_hbm,
                     stg, pk, zblk, sem, zsem, *, n_steps, n_blocks):
    i = pl.program_id(0)
    slot = i % 3
    prev = (i + 2) % 3

    def piece_copy(s, p, d):
        return pltpu.make_async_copy(pk.at[s, _piece(p), :], xs_hbm.at[_piece(d), :], sem.at[s])

    def wait_pieces(s, n):
        _wait_pieces(n, PK_PIECE_ROWS, lambda rows: pltpu.make_async_copy(
            pk.at[s, pl.ds(0, rows), :], xs_hbm.at[pl.ds(0, rows), :], sem.at[s]))

    def zero_piece_copy(d):
        return pltpu.make_async_copy(zblk.at[pl.ds(0, PK_PIECE_ROWS), :], xs_hbm.at[_piece(d), :], zsem.at[0])

    def zero_block_copy(b):
        return pltpu.make_async_copy(zblk, xs_hbm.at[_rows(b, PK_BLK_ROWS), :], zsem.at[1])

    @pl.when(i == 0)
    def _():
        zblk[...] = jnp.zeros_like(zblk)

        def zero(c, carry):
            stg[_rows(c, 2048), :] = jnp.zeros((2048, LANES), F32)
            return carry
        lax.fori_loop(0, STG_ROWS // 2048, zero, 0)

    @pl.when(i >= 3)
    def _():
        wait_pieces(slot, np_ref[jnp.maximum(i - 3, 0)])

    def scatter_token(t):
        tile = h_ref[_rows(t, TILE_ROWS), :].astype(BF16).astype(F32)
        for k in range(TOP_K):
            stg[_tile_at(ls_ref[0, k * ROW_BLK + t]), :] = tile

    np_prev = jnp.where(i >= 1, np_ref[jnp.maximum(i - 1, 0)], 0)
    _interleave(scatter_token, np_prev,
                lambda p, prio: piece_copy(prev, p, pdp_ref[0, p]).start(priority=prio))

    def pack_group(q, carry):
        v = stg[_rows(q, PACK_GROUP * PIECE_ROWS), :].reshape(PACK_GROUP * PIECE // 2, 2 * TILE_ROWS, LANES)
        packed = _pack_pairs(v[:, 0:TILE_ROWS, :], v[:, TILE_ROWS:2 * TILE_ROWS, :])
        pk[slot, _rows(q, PACK_GROUP * PK_PIECE_ROWS), :] = packed.reshape(PACK_GROUP * PK_PIECE_ROWS, LANES)
        return carry
    lax.fori_loop(0, lax.shift_right_logical(np_ref[i] + (PACK_GROUP - 1), LOG_PACK_GROUP), pack_group, 0)

    @pl.when(i == n_steps - 1)
    def _():
        _start_pieces(0, np_ref[i], lambda p, prio: piece_copy(slot, p, pd_ref[0, p]).start(priority=prio))

        def tail(p, carry):
            zero_piece_copy(td_ref[0, p]).start()
            return carry
        lax.fori_loop(0, nt_ref[0], tail, 0)

        def blk(b, carry):
            zero_block_copy(b).start()
            return carry
        lax.fori_loop(nu_ref[0], n_blocks, blk, 0)

        wait_pieces((i + 1) % 3, np_ref[jnp.maximum(i - 2, 0)])
        wait_pieces(prev, np_ref[jnp.maximum(i - 1, 0)])
        wait_pieces(slot, np_ref[i])

        def tail_wait(p, carry):
            zero_piece_copy(0).wait()
            return carry
        lax.fori_loop(0, nt_ref[0], tail_wait, 0)

        def blk_wait(b, carry):
            zero_block_copy(0).wait()
            return carry
        lax.fori_loop(nu_ref[0], n_blocks, blk_wait, 0)


def _dispatch(npieces, n_used, ntail, lslot, piece_dst, tail_dst, h2t, n_steps, n_blocks):
    assert n_steps >= 3
    smem = lambda n, imap: pl.BlockSpec((None, 1, n), imap, memory_space=pltpu.SMEM)
    grid_spec = pltpu.PrefetchScalarGridSpec(
        num_scalar_prefetch=3,
        grid=(n_steps,),
        in_specs=[
            smem(TOP_K * ROW_BLK, lambda i, *_: (i, 0, 0)),
            smem(PIECES_MAX, lambda i, *_: (i, 0, 0)),
            smem(PIECES_MAX, lambda i, *_: (jnp.maximum(i - 1, 0), 0, 0)),
            smem(TAIL_MAX, lambda i, *_: (0, 0, 0)),
            pl.BlockSpec((ROW_BLK * TILE_ROWS, LANES), lambda i, *_: (i, 0)),
        ],
        out_specs=pl.BlockSpec(memory_space=pl.ANY),
        scratch_shapes=[
            pltpu.VMEM((STG_ROWS, LANES), F32),
            pltpu.VMEM((3, PK_STG_ROWS, LANES), U32),
            pltpu.VMEM((PK_BLK_ROWS, LANES), U32),
            pltpu.SemaphoreType.DMA((3,)),
            pltpu.SemaphoreType.DMA((2,)),
        ],
    )
    return pl.pallas_call(
        functools.partial(_dispatch_kernel, n_steps=n_steps, n_blocks=n_blocks),
        grid_spec=grid_spec,
        out_shape=jax.ShapeDtypeStruct((n_blocks * PK_BLK_ROWS, LANES), U32),
        compiler_params=pltpu.CompilerParams(dimension_semantics=("arbitrary",),
                                             vmem_limit_bytes=VMEM_LIMIT),
        name="dispatch",
    )(npieces, n_used, ntail, lslot, piece_dst, piece_dst, tail_dst, h2t)


def _from_tiles(ref, n):
    return jnp.concatenate([ref[pl.ds(s, n, stride=TILE_ROWS), :] for s in range(TILE_ROWS)], axis=1)


def _experts_kernel(be_ref, nu_ref, xs_hbm, wg_ref, wu_ref, wd_ref, ys_hbm,
                    xbuf, ybuf, wgb, wub, wdb, xsem, ysem, *, n_blocks):
    i = pl.program_id(0)
    n_used = nu_ref[0]
    rows = PK_BLK_ROWS
    half = rows // 2
    pairs = MOE_BLK // 2

    def halves(b):
        return [pl.ds(pl.multiple_of(b * rows + h * half, half), half) for h in range(2)]

    def x_copies(b, s):
        return [pltpu.make_async_copy(xs_hbm.at[src, :], xbuf.at[s, pl.ds(h * half, half), :], xsem.at[s])
                for h, src in enumerate(halves(b))]

    def y_copies(b, s):
        return [pltpu.make_async_copy(ybuf.at[s, pl.ds(h * half, half), :], ys_hbm.at[dst, :], ysem.at[s])
                for h, dst in enumerate(halves(b))]

    def start(copies):
        for h, c in enumerate(copies):
            c.start(priority=h)

    def wait(copies):
        for c in copies:
            c.wait()

    @pl.when(i == 0)
    def _():
        start(x_copies(0, 0))

        @pl.when(n_used > 1)
        def _():
            start(x_copies(1, 1))

    @pl.when(i + 2 < n_used)
    def _():
        start(x_copies(i + 2, (i + 2) % 3))

    @pl.when(i < n_used)
    def _():
        wait(x_copies(i, i % 3))

        @pl.when(i >= 2)
        def _():
            wait(y_copies(i - 2, i % 2))

        @pl.when(jnp.logical_or(i == 0, be_ref[i] != be_ref[jnp.maximum(i - 1, 0)]))
        def _():
            wgb[...] = wg_ref[...].astype(BF16)
            wub[...] = wu_ref[...].astype(BF16)
            wdb[...] = wd_ref[...].astype(BF16)

        x = pltpu.bitcast(_from_tiles(xbuf.at[i % 3], pairs), BF16)
        g = _dot(x, wgb[...])
        u = _dot(x, wub[...])
        y = _dot((g * _sigmoid(g) * u).astype(BF16), wdb[...])
        yp = pltpu.bitcast(y.astype(BF16), U32)
        yb = ybuf.at[i % 2]
        for s in range(TILE_ROWS):
            yb[pl.ds(s, pairs, stride=TILE_ROWS), :] = yp[:, s * LANES:(s + 1) * LANES]
        start(y_copies(i, i % 2))

        @pl.when(i == n_used - 1)
        def _():
            @pl.when(i >= 1)
            def _():
                wait(y_copies(i - 1, (i + 1) % 2))
            wait(y_copies(i, i % 2))
            ybuf[0] = jnp.zeros((rows, LANES), U32)

            def fill(b, carry):
                start(y_copies(b, 0))
                return carry
            lax.fori_loop(n_used, n_blocks, fill, 0)

            def fill_wait(b, carry):
                wait(y_copies(0, 0))
                return carry
            lax.fori_loop(n_used, n_blocks, fill_wait, 0)


def _experts(block_e, n_used, x_sorted, wg, wu, wd, layer, n_blocks):
    wspec = lambda shape: pl.BlockSpec((None, None) + shape, lambda i, be, nu: (layer, be[i], 0, 0))
    rows = PK_BLK_ROWS
    grid_spec = pltpu.PrefetchScalarGridSpec(
        num_scalar_prefetch=2,
        grid=(n_blocks,),
        in_specs=[
            pl.BlockSpec(memory_space=pl.ANY),
            wspec((D_MODEL, EXPERT_DIM)), wspec((D_MODEL, EXPERT_DIM)), wspec((EXPERT_DIM, D_MODEL)),
        ],
        out_specs=pl.BlockSpec(memory_space=pl.ANY),
        scratch_shapes=[
            pltpu.VMEM((3, rows, LANES), U32),
            pltpu.VMEM((2, rows, LANES), U32),
            pltpu.VMEM((D_MODEL, EXPERT_DIM), BF16),
            pltpu.VMEM((D_MODEL, EXPERT_DIM), BF16),
            pltpu.VMEM((EXPERT_DIM, D_MODEL), BF16),
            pltpu.SemaphoreType.DMA((3,)),
            pltpu.SemaphoreType.DMA((2,)),
        ],
    )
    return pl.pallas_call(
        functools.partial(_experts_kernel, n_blocks=n_blocks),
        grid_spec=grid_spec,
        out_shape=jax.ShapeDtypeStruct((n_blocks * rows, LANES), U32),
        compiler_params=pltpu.CompilerParams(dimension_semantics=("arbitrary",)),
        name="experts",
    )(block_e, n_used, x_sorted, wg, wu, wd)


def _combine_kernel(np_ref, lp_ref, wl_ref, wh_ref, cur_ref, nx1_ref, nx2_ref, y_hbm, h2b_ref, xn_ref,
                    mod_ref, g_ref, swg_ref, swu_ref, swd_ref, o_ref, stg, acc, sem, *, n_steps):
    i = pl.program_id(0)
    slot = i % 3
    d = D_MODEL

    def piece_copy(s, p, src):
        return pltpu.make_async_copy(y_hbm.at[_piece(src), :], stg.at[s, _piece(p), :], sem.at[s])

    def fetcher(s, pd_ref):
        return lambda p, prio: piece_copy(s, p, pd_ref[0, p]).start(priority=prio)

    @pl.when(i == 0)
    def _():
        _start_pieces(0, np_ref[0], fetcher(0, cur_ref))
        _start_pieces(0, np_ref[1], fetcher(1, nx1_ref))

    _wait_pieces(np_ref[i], PK_PIECE_ROWS, lambda rows: pltpu.make_async_copy(
        y_hbm.at[pl.ds(0, rows), :], stg.at[slot, pl.ds(0, rows), :], sem.at[slot]))

    def gather_token(t):
        a = None
        for k in range(TOP_K):
            j = k * ROW_BLK + t
            pair = stg[slot, _tile_at(lp_ref[0, j]), :]
            v = _unpack_lo(pair) * wl_ref[0, j] + _unpack_hi(pair) * wh_ref[0, j]
            a = v if a is None else a + v
        acc[_rows(t, TILE_ROWS), :] = a

    np_next = jnp.where(i + 2 < n_steps, np_ref[jnp.minimum(i + 2, n_steps - 1)], 0)
    _interleave(gather_token, np_next, fetcher((i + 2) % 3, nx2_ref))

    hb = h2b_ref[...]
    g = _dot(hb, swg_ref[...])
    u = _dot(hb, swu_ref[...])
    f = _dot((g * _sigmoid(g) * u).astype(BF16), swd_ref[...]) + _from_tiles(acc, ROW_BLK)
    o_ref[...] = xn_ref[...] + mod_ref[:, 5 * d:6 * d] * _rms(f, g_ref[...])


def _combine(npieces, pair_row, w_lo, w_hi, piece_src, y_sorted, h2b_rows, xn_rows, mod3, g, swg, swu, swd,
             n_steps, per_batch):
    assert n_steps >= 3
    last = n_steps - 1
    full = lambda shape: pl.BlockSpec(shape, lambda n, *_: (0,) * len(shape))
    rows = lambda w_: pl.BlockSpec((ROW_BLK, w_), lambda n, *_: (n, 0))
    smem = lambda w_, imap: pl.BlockSpec((None, 1, w_), imap, memory_space=pltpu.SMEM)
    n_lat = SEQ // ROW_BLK
    grid_spec = pltpu.PrefetchScalarGridSpec(
        num_scalar_prefetch=1,
        grid=(n_steps,),
        in_specs=[
            smem(TOP_K * ROW_BLK, lambda n, *_: (n, 0, 0)),
            smem(TOP_K * ROW_BLK, lambda n, *_: (n, 0, 0)),
            smem(TOP_K * ROW_BLK, lambda n, *_: (n, 0, 0)),
            smem(PIECES_MAX, lambda n, *_: (n, 0, 0)),
            smem(PIECES_MAX, lambda n, *_: (jnp.minimum(n + 1, last), 0, 0)),
            smem(PIECES_MAX, lambda n, *_: (jnp.minimum(n + 2, last), 0, 0)),
            pl.BlockSpec(memory_space=pl.ANY),
            rows(D_MODEL), rows(D_MODEL),
            pl.BlockSpec((None, 1, 6 * D_MODEL),
                         lambda n, *_: (_mod_row(n % per_batch, n // per_batch, n_lat), 0, 0)),
            full((1, D_MODEL)),
            full((D_MODEL, SHARED_DIM)), full((D_MODEL, SHARED_DIM)), full((SHARED_DIM, D_MODEL)),
        ],
        out_specs=rows(D_MODEL),
        scratch_shapes=[
            pltpu.VMEM((3, PK_STG_ROWS, LANES), U32),
            pltpu.VMEM((ROW_BLK * TILE_ROWS, LANES), F32),
            pltpu.SemaphoreType.DMA((3,)),
        ],
    )
    return pl.pallas_call(
        functools.partial(_combine_kernel, n_steps=n_steps),
        grid_spec=grid_spec,
        out_shape=jax.ShapeDtypeStruct(xn_rows.shape, F32),
        compiler_params=pltpu.CompilerParams(dimension_semantics=("arbitrary",),
                                             vmem_limit_bytes=VMEM_LIMIT),
        name="combine",
    )(npieces, pair_row, w_lo, w_hi, piece_src, piece_src, piece_src, y_sorted, h2b_rows, xn_rows, mod3, g,
      swg, swu, swd)


def _moe_plan(cnt_rows, n_blocks):
    i32 = jnp.int32
    per_blk = MOE_BLK // PIECE
    to_expert = lambda v: v.reshape(v.shape[:-1] + (8, 8)).swapaxes(-1, -2).reshape(v.shape)
    runs = (cnt_rows + PIECE - 1) // PIECE
    loc_end = jnp.cumsum(runs, axis=1)
    loc = loc_end - runs
    npieces = loc_end[:, -1].astype(i32)
    reg = to_expert(runs.sum(axis=0))
    padded = (reg + per_blk - 1) // per_blk * per_blk
    pad_end = jnp.cumsum(padded)
    pad_start = pad_end - padded
    off = to_expert(pad_start)[None, :] + jnp.cumsum(runs, axis=0) - runs
    p = jnp.arange(PIECES_MAX, dtype=i32)[None, :, None]
    mine = (loc[:, None, :] <= p) & (p < loc_end[:, None, :])
    piece_dst = jnp.sum(jnp.where(mine, (off - loc)[:, None, :], 0), axis=-1) + p[:, :, 0]
    tail_n = padded - reg
    t_end = jnp.cumsum(tail_n)
    t_beg = t_end - tail_n
    q = jnp.arange(TAIL_MAX, dtype=i32)[:, None]
    tmine = (t_beg[None, :] <= q) & (q < t_end[None, :])
    tail_dst = jnp.sum(jnp.where(tmine, (pad_start + reg - t_beg)[None, :], 0), axis=-1) + q[:, 0]
    first = jnp.arange(n_blocks, dtype=i32)[:, None] * per_blk
    block_e = jnp.minimum(jnp.sum((pad_end[None, :] <= first).astype(i32), axis=-1), N_EXPERTS - 1)
    n_used = (pad_end[-1] // per_blk).astype(i32).reshape(1)
    return (npieces, piece_dst.astype(i32).reshape(-1, 1, PIECES_MAX),
            tail_dst.astype(i32).reshape(1, 1, TAIL_MAX), t_end[-1].astype(i32).reshape(1),
            block_e.astype(i32), n_used)


def _axial_tables(dim):
    f32 = np.float32
    t = np.arange(SEQ)
    row = (t // GRID_W).astype(f32)
    col = (t % GRID_W).astype(f32)
    quarter = dim // 4
    freqs = (f32(ROPE_THETA) ** (-np.arange(quarter, dtype=f32) / f32(quarter))).astype(f32)
    ar, ac = row[:, None] * freqs, col[:, None] * freqs
    ang = np.concatenate([ar, ar, ac, ac], axis=-1)
    sign = np.where((np.arange(dim) // quarter) % 2 == 0, -1.0, 1.0).astype(f32)
    cos = np.concatenate([np.cos(ang), np.ones((CTX_LEN, dim), f32)], axis=0)
    sin = np.concatenate([np.sin(ang) * sign, np.zeros((CTX_LEN, dim), f32)], axis=0)
    return cos.astype(f32), sin.astype(f32)


def _rope_tables():
    f32 = np.float32
    cos_h, sin_h = _axial_tables(HEAD_DIM)
    cos_r, sin_r = _axial_tables(B_ROPE)
    cosh, sinh = np.tile(cos_h, (1, 4)), np.tile(sin_h, (1, 4))
    scale_b = f32((B_NOPE + B_ROPE) ** -0.5)
    ones, zeros = np.ones((TOK, 64), f32), np.zeros((TOK, 64), f32)
    pad1, pad0 = np.ones((TOK, 32), f32), np.zeros((TOK, 32), f32)
    cosb = np.tile(np.concatenate([ones, cos_r, pad1], axis=1) * scale_b, (1, 4))
    sinb = np.tile(np.concatenate([zeros, sin_r, pad0], axis=1) * scale_b, (1, 4))
    cosr = np.concatenate([cos_r, np.ones((TOK, 96), f32)], axis=1)
    sinr = np.concatenate([sin_r, np.zeros((TOK, 96), f32)], axis=1)
    return tuple(jnp.asarray(a, F32) for a in (cosh, sinh, cosb, sinb, cosr, sinr))


_GQA_ORDER = (0, 2, 1, 3)


def _head_cols(base, order, width=HEAD_DIM):
    return np.concatenate([base + np.arange(h * width, (h + 1) * width) for h in order])


def _w_in_layout(w_in):
    zeros = lambda n: jnp.zeros((D_MODEL, n), w_in.dtype)
    a_q = w_in[:, _head_cols(0, _GQA_ORDER)]
    a_kv = w_in[:, 256:512]
    b_q = jnp.concatenate(
        [jnp.concatenate([w_in[:, 512 + h * 96:512 + (h + 1) * 96], zeros(32)], axis=1)
         for h in range(B_HEADS)], axis=1)
    b_c = w_in[:, 896:1024]
    b_r = jnp.concatenate([w_in[:, 1024:1056], zeros(96)], axis=1)
    c_q = w_in[:, _head_cols(1056, _GQA_ORDER)]
    rest = w_in[:, 1312:2336]
    return jnp.concatenate([a_q, a_kv, b_q, b_c, b_r, c_q, rest], axis=1).astype(BF16)


def _w_kv_layout(w_kv_up):
    zeros = jnp.zeros((B_KV_RANK, 64), w_kv_up.dtype)
    nope = [jnp.concatenate([w_kv_up[:, h * 128:h * 128 + 64], zeros], axis=1) for h in range(B_HEADS)]
    val = [w_kv_up[:, h * 128 + 64:(h + 1) * 128] for h in range(B_HEADS)]
    return jnp.concatenate(nope + val, axis=1).astype(BF16)


def _w_out_layout(w_out):
    rows = np.concatenate([_head_cols(0, _GQA_ORDER), np.arange(256, 512),
                           _head_cols(512, _GQA_ORDER), np.arange(768, 1024)])
    return w_out[rows].astype(BF16)


_ROUTER_ROWS = np.array([g * 8 + j for j in range(8) for g in range(8)])


def kernel(x, c, ctx, c_ctx, ada_w, ada_b, pre_mix_g, post_mix_g, pre_ffn_g, post_ffn_g, w_in, a_q_norm, a_k_norm, b_kv_norm, b_w_kv_up, c_sink, d_rpb, w_out, router_w, router_b, exp_w_gate, exp_w_up, exp_w_down, sh_w_gate, sh_w_up, sh_w_down):
    bsz = x.shape[0]
    depth = ada_w.shape[0]
    xall = jnp.concatenate([x, ctx], axis=1)

    cvec = jnp.zeros((16, D_MODEL), F32).at[:bsz].set(c).at[8].set(c_ctx)
    mod = _ada(cvec, ada_w, ada_b)
    tabs = _rope_tables()
    avg = jnp.asarray(np.kron(np.eye(4, dtype=np.float32), np.full((64, 64), 1.0 / 64, np.float32)), BF16)
    row2 = lambda v: v.reshape(1, -1).astype(F32)

    for l in range(depth):
        with_ctx = l < depth - 1
        mod3 = mod[l].reshape(16, 1, 6 * D_MODEL)

        proj = _inproj(
            xall, mod3, row2(pre_mix_g[l]), _w_in_layout(w_in[l]),
            row2(jnp.tile(a_q_norm[l], 4) * HEAD_DIM ** -0.5), row2(jnp.tile(a_k_norm[l], 2)),
            row2(b_kv_norm[l]), _w_kv_layout(b_w_kv_up[l]), avg, tabs)
        qa, ka, va, qb, kb, vb, qc, kc, vc, qd, kd, vd = proj

        nq = NQ + (CTX_LEN // QBLK if with_ctx else 0)
        full = lambda shape: pl.BlockSpec(shape, lambda b, i: (0,) * len(shape))
        mix_a = _attn_call(_attn_a_kernel, "attn_a", qa, ka, va, [], [], nq // 2, qblk=2 * QBLK)
        mix_b = _attn_call(_attn_b_kernel, "attn_b", qb, kb, vb, [], [], nq // 2, qblk=2 * QBLK)
        sink = jnp.zeros((8, LANES), F32).at[:4].set(
            jnp.broadcast_to(c_sink[l][np.array(_GQA_ORDER)][:, None], (4, LANES)))
        mix_c = _attn_call(_attn_c_kernel, "attn_c", qc, kc, vc, [sink], [full((8, LANES))], nq // 2,
                           qblk=2 * QBLK)
        bias = _na_bias(d_rpb[l])
        mix_d = _attn_call(
            _attn_d_kernel, "attn_d", qd, kd, vd, [bias],
            [pl.BlockSpec((None, D_HEADS, QBLK, NA_WIN), lambda b, i: (_na_pattern(i), 0, 0, 0))], nq)

        nb = (TOK if with_ctx else SEQ) // ROW_BLK
        rb = jnp.broadcast_to(router_b[l][_ROUTER_ROWS][:, None], (N_EXPERTS, ROW_BLK)).astype(F32)
        xn, h2b, h2t, wt_t, ls_t, cnt_t = _outproj(
            (mix_a, mix_b, mix_c, mix_d), _w_out_layout(w_out[l]), xall, mod3,
            row2(post_mix_g[l]), row2(pre_ffn_g[l]),
            router_w[l].T[_ROUTER_ROWS].astype(BF16), rb, nb)

        t_used = nb * ROW_BLK
        n_tok = bsz * t_used
        n_steps = bsz * nb
        n_blocks = -(-(n_tok * TOP_K + n_steps * N_EXPERTS * (PIECE - 1) + N_EXPERTS * (MOE_BLK - 1))
                     // MOE_BLK)
        npieces, piece_dst, tail_dst, ntail, block_e, n_used = _moe_plan(
            cnt_t[..., 0].reshape(n_steps, N_EXPERTS), n_blocks)
        lslot = ls_t.reshape(n_steps, 1, TOP_K * ROW_BLK)
        wts = wt_t.reshape(n_steps, 1, TOP_K * ROW_BLK)
        stg_row = lslot * TILE_ROWS
        pair_row = (lslot // 2) * TILE_ROWS
        w_lo = jnp.where(lslot % 2 == 0, wts, 0.0)
        w_hi = wts - w_lo

        x_sorted = _dispatch(npieces, n_used, ntail, stg_row, piece_dst, tail_dst,
                             h2t.reshape(n_tok * TILE_ROWS, LANES), n_steps, n_blocks)
        y_sorted = _experts(block_e, n_used, x_sorted, exp_w_gate, exp_w_up, exp_w_down, l, n_blocks)
        xall = _combine(
            npieces, pair_row, w_lo, w_hi, piece_dst, y_sorted, h2b.reshape(n_tok, D_MODEL),
            xn.reshape(n_tok, D_MODEL), mod3, row2(post_ffn_g[l]),
            sh_w_gate[l].astype(BF16), sh_w_up[l].astype(BF16), sh_w_down[l].astype(BF16),
            n_steps, nb).reshape(bsz, t_used, D_MODEL)

    return xall[:, :SEQ]
```

```python
import functools

import numpy as np
import jax
import jax.numpy as jnp
from jax import lax
from jax.experimental import pallas as pl
from jax.experimental.pallas import tpu as pltpu

F32 = jnp.float32
BF16 = jnp.bfloat16
U32 = jnp.uint32

D_MODEL = 1024
SEQ = 2048
CTX_LEN = 256
TOK = SEQ + CTX_LEN
GRID_W = 64
HEAD_DIM = 64
ROPE_THETA = 10000.0
EPS = 1e-6
NEG = -1e30

A_HEADS, A_KV_HEADS = 4, 2
B_HEADS, B_NOPE, B_ROPE, B_V, B_KV_RANK = 4, 64, 32, 64, 128
C_HEADS, C_KV_HEADS, C_WINDOW = 4, 2, 128
D_HEADS, NA_ROWS, NA_COLS = 4, 8, 16

N_EXPERTS, TOP_K, N_GROUPS, TOPK_GROUPS = 64, 8, 8, 4
EXPERT_DIM, SHARED_DIM = 256, 256
ROUTED_SCALE = 2.5

LANES = 128
QBLK = 128
ROW_BLK = 256
IN_BLK = 768
MOE_BLK = 512
TILE_ROWS = D_MODEL // LANES
PIECE = 8
PIECE_ROWS = PIECE * TILE_ROWS
PIECES_MAX = 320
STG_ROWS = PIECES_MAX * PIECE_ROWS
PK_PIECE_ROWS = PIECE_ROWS // 2
PK_STG_ROWS = STG_ROWS // 2
PK_BLK_ROWS = MOE_BLK * TILE_ROWS // 2
TAIL_MAX = N_EXPERTS * MOE_BLK // PIECE
LOG_PACK_GROUP = 2
PACK_GROUP = 1 << LOG_PACK_GROUP
LOG_WAIT_GROUP = 4
WAIT_GROUP = 1 << LOG_WAIT_GROUP
LOG_CHUNKS = 4
N_CHUNKS = 1 << LOG_CHUNKS
VMEM_LIMIT = 56 * 1024 * 1024
NQ = SEQ // QBLK
NA_WIN = 10 * GRID_W

_AQ, _AK, _AV, _BQ, _BC, _BR = 0, 256, 384, 512, 1024, 1152
_CQ, _CK, _CV, _DQ, _DK, _DV = 1280, 1536, 1664, 1792, 2048, 2304
IN_W = 2560

_NT = (((1,), (1,)), ((), ()))


def _sigmoid(x):
    return 1.0 / (1.0 + jnp.exp(-x))


def _rms(x, g):
    return x * lax.rsqrt(jnp.mean(x * x, axis=-1, keepdims=True) + EPS) * g


def _dot(a, b):
    return jnp.dot(a, b, preferred_element_type=F32)


def _dot_nt(a, b):
    return lax.dot_general(a, b, _NT, preferred_element_type=F32)


def _ada_kernel(c_ref, w_ref, b_ref, o_ref):
    c = c_ref[...]
    a = (c * _sigmoid(c)).astype(BF16)
    o_ref[...] = _dot(a, w_ref[...].astype(BF16)) + b_ref[...]


def _ada(cvec, ada_w, ada_b):
    depth = ada_w.shape[0]
    n = ada_w.shape[2]
    tn = 1536
    return pl.pallas_call(
        _ada_kernel,
        grid=(depth, n // tn),
        in_specs=[
            pl.BlockSpec((16, D_MODEL), lambda l, j: (0, 0)),
            pl.BlockSpec((None, D_MODEL, tn), lambda l, j: (l, 0, j)),
            pl.BlockSpec((None, 1, tn), lambda l, j: (l, 0, j)),
        ],
        out_specs=pl.BlockSpec((None, 16, tn), lambda l, j: (l, 0, j)),
        out_shape=jax.ShapeDtypeStruct((depth, 16, n), F32),
        name="ada",
    )(cvec, ada_w, ada_b.reshape(depth, 1, n))


def _rope(x, cos, sin, chunk):
    outs = []
    for c in range(x.shape[1] // LANES):
        sl = slice(c * LANES, (c + 1) * LANES)
        xs = x[:, sl]
        lane = lax.broadcasted_iota(jnp.int32, xs.shape, 1)
        first = (lane & chunk) == 0
        rot = jnp.where(first, pltpu.roll(xs, LANES - chunk, 1), pltpu.roll(xs, chunk, 1))
        outs.append(xs * cos[:, sl] + rot * sin[:, sl])
    return outs[0] if len(outs) == 1 else jnp.concatenate(outs, axis=1)


def _head_norm(x, avg):
    sq = x * x
    hi = sq.astype(BF16)
    lo = (sq - hi.astype(F32)).astype(BF16)
    ms = _dot(hi, avg) + _dot(lo, avg)
    return x * lax.rsqrt(ms + EPS)


def _inproj_kernel(x_ref, mod_ref, modc_ref, g_ref, w_ref, qg_ref, kg_ref, kvg_ref, wkv_ref, avg_ref,
                   cosh_ref, sinh_ref, cosb_ref, sinb_ref, cosr_ref, sinr_ref,
                   qa_ref, ka_ref, va_ref, qb_ref, kb_ref, vb_ref,
                   qc_ref, kc_ref, vc_ref, qd_ref, kd_ref, vd_ref):
    d = D_MODEL
    n = x_ref.shape[0]
    is_ctx = pl.program_id(1) * n + lax.broadcasted_iota(jnp.int32, (n, 1), 0) >= SEQ
    scale = jnp.where(is_ctx, modc_ref[:, d:2 * d], mod_ref[:, d:2 * d])
    shift = jnp.where(is_ctx, modc_ref[:, 0:d], mod_ref[:, 0:d])
    h = _rms(x_ref[...], g_ref[...]) * (1.0 + scale) + shift
    p = _dot(h.astype(BF16), w_ref[...])
    cosh, sinh = cosh_ref[...], sinh_ref[...]
    avg = avg_ref[...]

    qa = _head_norm(p[:, _AQ:_AQ + 256], avg) * qg_ref[...]
    qa_ref[...] = _rope(qa, cosh, sinh, 16).astype(BF16)
    ka = _head_norm(p[:, _AK:_AK + 128], avg[0:128, 0:128]) * kg_ref[...]
    ka_ref[...] = _rope(ka, cosh[:, 0:128], sinh[:, 0:128], 16).astype(BF16)
    ones = jnp.ones((p.shape[0], LANES), BF16)
    va_ref[...] = jnp.concatenate([p[:, _AV:_AV + 128].astype(BF16), ones], axis=1)

    qb_ref[...] = _rope(p[:, _BQ:_BQ + 512], cosb_ref[...], sinb_ref[...], 8).astype(BF16)
    cn = _rms(p[:, _BC:_BC + 128], kvg_ref[...])
    kv = _dot(cn.astype(BF16), wkv_ref[...])
    kr = _rope(p[:, _BR:_BR + 128], cosr_ref[...], sinr_ref[...], 8)
    kr = pltpu.roll(kr, 64, 1)
    kb_ref[...] = jnp.concatenate(
        [kv[:, hh * 128:(hh + 1) * 128] + kr for hh in range(B_HEADS)], axis=1).astype(BF16)
    vb_ref[...] = jnp.concatenate(
        [kv[:, 512:640].astype(BF16), ones, kv[:, 640:768].astype(BF16), ones], axis=1)

    qc_ref[...] = _rope(p[:, _CQ:_CQ + 256] * 0.125, cosh, sinh, 16).astype(BF16)
    kc_ref[...] = _rope(p[:, _CK:_CK + 128], cosh[:, 0:128], sinh[:, 0:128], 16).astype(BF16)
    vc_ref[...] = jnp.concatenate([p[:, _CV:_CV + 128].astype(BF16), ones], axis=1)

    qd_ref[...] = (p[:, _DQ:_DQ + 256] * 0.125).astype(BF16)
    kd_ref[...] = p[:, _DK:_DK + 256].astype(BF16)
    vd_ref[...] = jnp.concatenate(
        [p[:, _DV:_DV + 128].astype(BF16), ones, p[:, _DV + 128:_DV + 256].astype(BF16), ones], axis=1)


def _mod_row(j, b, n_lat_blocks):
    return jnp.where(j < n_lat_blocks, b, 8)


def _inproj(xall, mod3, g, w, qg, kg, kvg, wkv, avg, tabs):
    bsz = xall.shape[0]
    nb = TOK // IN_BLK
    full = lambda shape: pl.BlockSpec(shape, lambda b, j: (0,) * len(shape))
    row = lambda w_: pl.BlockSpec((IN_BLK, w_), lambda b, j: (j, 0))
    out = lambda w_: pl.BlockSpec((None, IN_BLK, w_), lambda b, j: (b, j, 0))
    widths = (256, 128, 256, 512, 512, 512, 256, 128, 256, 256, 256, 512)
    return pl.pallas_call(
        _inproj_kernel,
        grid=(bsz, nb),
        in_specs=[
            pl.BlockSpec((None, IN_BLK, D_MODEL), lambda b, j: (b, j, 0)),
            pl.BlockSpec((None, 1, 6 * D_MODEL), lambda b, j: (b, 0, 0)),
            pl.BlockSpec((None, 1, 6 * D_MODEL), lambda b, j: (8, 0, 0)),
            full((1, D_MODEL)), full((D_MODEL, IN_W)),
            full((1, 256)), full((1, 128)), full((1, 128)), full((128, 768)), full((256, 256)),
            row(256), row(256), row(512), row(512), row(128), row(128),
        ],
        out_specs=[out(w_) for w_ in widths],
        out_shape=[jax.ShapeDtypeStruct((bsz, TOK, w_), BF16) for w_ in widths],
        compiler_params=pltpu.CompilerParams(vmem_limit_bytes=VMEM_LIMIT),
        name="inproj",
    )(xall, mod3, mod3, g, w, qg, kg, kvg, wkv, avg, *tabs)


def _softmax_pv(segs, sink=None):
    m = None
    for s, _ in segs:
        ms = jnp.max(s, axis=-1, keepdims=True)
        m = ms if m is None else jnp.maximum(m, ms)
    if sink is not None:
        m = jnp.maximum(m, sink)
    acc = None
    for s, v1 in segs:
        r = _dot(jnp.exp((s - m).astype(BF16)), v1)
        acc = r if acc is None else acc + r
    l = acc[:, LANES:LANES + 1]
    if sink is not None:
        l = l + jnp.exp(sink - m)
    return acc[:, 0:LANES] / l


def _lane_lo(dtype_shape):
    return lax.broadcasted_iota(jnp.int32, dtype_shape, 1) < HEAD_DIM


def _half(qt, hh):
    lo = _lane_lo(qt.shape)
    return jnp.where(lo if hh == 0 else jnp.logical_not(lo), qt, jnp.zeros_like(qt))


def _merge_halves(o0, o1):
    return jnp.where(_lane_lo(o0.shape), o0, o1)


def _stack_heads(q_ref, tiles):
    return jnp.concatenate(
        [_half(q_ref[:, t * LANES:(t + 1) * LANES], hh) for t in tiles for hh in range(2)], axis=0)


def _store_heads(o_ref, o, tiles):
    n = o_ref.shape[0]
    for ti, t in enumerate(tiles):
        o0, o1 = o[2 * ti * n:(2 * ti + 1) * n], o[(2 * ti + 1) * n:(2 * ti + 2) * n]
        o_ref[:, t * LANES:(t + 1) * LANES] = _merge_halves(o0, o1).astype(o_ref.dtype)


def _attn_a_kernel(q_ref, k_ref, v_ref, o_ref, *, n_lat):
    def run(k, v1):
        for t in range(2):
            o = [_softmax_pv([(_dot_nt(_half(q_ref[:, t * LANES:(t + 1) * LANES], hh), k), v1)])
                 for hh in range(2)]
            o_ref[:, t * LANES:(t + 1) * LANES] = _merge_halves(o[0], o[1]).astype(o_ref.dtype)

    @pl.when(pl.program_id(1) < n_lat)
    def _():
        run(k_ref[...], v_ref[...])

    @pl.when(pl.program_id(1) >= n_lat)
    def _():
        run(k_ref[SEQ:TOK, :], v_ref[SEQ:TOK, :])


def _attn_b_kernel(q_ref, k_ref, v_ref, o_ref, *, n_lat):
    def run(lo_, hi_):
        for t in range(2):
            v1 = v_ref[lo_:hi_, t * 256:(t + 1) * 256]
            o = []
            for hh in range(2):
                hd = 2 * t + hh
                q = q_ref[:, hd * LANES:(hd + 1) * LANES]
                k = k_ref[lo_:hi_, hd * LANES:(hd + 1) * LANES]
                o.append(_softmax_pv([(_dot_nt(q, k), v1)]))
            o_ref[:, t * LANES:(t + 1) * LANES] = _merge_halves(o[0], o[1]).astype(o_ref.dtype)

    @pl.when(pl.program_id(1) < n_lat)
    def _():
        run(0, TOK)

    @pl.when(pl.program_id(1) >= n_lat)
    def _():
        run(SEQ, TOK)


def _attn_c_kernel(q_ref, k_ref, v_ref, sink_ref, o_ref, *, n_lat):
    i = pl.program_id(1)
    qblk = o_ref.shape[0]
    band = 3 * qblk

    def run(latent):
        if latent:
            start = pl.multiple_of(jnp.clip((i - 1) * qblk, 0, SEQ - band), qblk)
            shape = (2 * qblk, band)
            qpos = i * qblk + (lax.broadcasted_iota(jnp.int32, shape, 0) & (qblk - 1))
            kpos = start + lax.broadcasted_iota(jnp.int32, shape, 1)
            valid = jnp.abs(qpos - kpos) <= C_WINDOW
        for t in range(2):
            q = _stack_heads(q_ref, (t,))
            segs = [(_dot_nt(q, k_ref[SEQ:TOK, :]), v_ref[SEQ:TOK, :])]
            if latent:
                s = jnp.where(valid, _dot_nt(q, k_ref[pl.ds(start, band), :]), NEG)
                segs.insert(0, (s, v_ref[pl.ds(start, band), :]))
            sink = jnp.concatenate(
                [jnp.broadcast_to(sink_ref[j:j + 1, 0:1], (qblk, 1)) for j in (2 * t, 2 * t + 1)], axis=0)
            _store_heads(o_ref, _softmax_pv(segs, sink=sink), (t,))

    @pl.when(i < n_lat)
    def _():
        run(True)

    @pl.when(i >= n_lat)
    def _():
        run(False)


def _attn_d_kernel(q_ref, k_ref, v_ref, bias_ref, o_ref, *, n_lat):
    i = pl.program_id(1)
    qblk = o_ref.shape[0]

    def run(latent):
        if latent:
            start = pl.multiple_of(jnp.clip((i - 2) * qblk, 0, SEQ - NA_WIN), qblk)
        for t in range(2):
            ksl, vsl = slice(t * LANES, (t + 1) * LANES), slice(t * 256, (t + 1) * 256)
            q = _stack_heads(q_ref, (t,))
            segs = [(_dot_nt(q, k_ref[SEQ:TOK, ksl]), v_ref[SEQ:TOK, vsl])]
            if latent:
                bias = bias_ref[2 * t:2 * t + 2].reshape(2 * qblk, NA_WIN)
                segs.insert(0, (_dot_nt(q, k_ref[pl.ds(start, NA_WIN), ksl]) + bias,
                                v_ref[pl.ds(start, NA_WIN), vsl]))
            _store_heads(o_ref, _softmax_pv(segs), (t,))

    @pl.when(i < n_lat)
    def _():
        run(True)

    @pl.when(i >= n_lat)
    def _():
        run(False)


def _attn_call(kernel, name, q, k, v, extra, extra_specs, n_blocks, qblk=QBLK):
    bsz = q.shape[0]
    return pl.pallas_call(
        functools.partial(kernel, n_lat=SEQ // qblk),
        grid=(bsz, n_blocks),
        in_specs=[
            pl.BlockSpec((None, qblk, q.shape[2]), lambda b, i: (b, i, 0)),
            pl.BlockSpec((None, TOK, k.shape[2]), lambda b, i: (b, 0, 0)),
            pl.BlockSpec((None, TOK, v.shape[2]), lambda b, i: (b, 0, 0)),
        ] + extra_specs,
        out_specs=pl.BlockSpec((None, qblk, 256), lambda b, i: (b, i, 0)),
        out_shape=jax.ShapeDtypeStruct((bsz, n_blocks * qblk, 256), BF16),
        name=name,
    )(q, k, v, *extra)


def _na_pattern(i):
    return jnp.where(i < 2, i, jnp.where(i < NQ - 2, 2, jnp.minimum(i, NQ - 1) - (NQ - 5)))


def _na_bias(rpb):
    rows = SEQ // GRID_W
    n_dr, n_dc = 2 * NA_ROWS - 1, 2 * NA_COLS - 1
    qc = np.arange(GRID_W)
    dc = np.clip(qc[None, :] - qc[:, None] + NA_COLS - 1, 0, n_dc - 1)
    sel_c = jnp.asarray(np.eye(n_dc, dtype=np.float32)[dc])
    cs = np.clip(qc - NA_COLS // 2, 0, GRID_W - NA_COLS)[:, None]
    col_ok = (qc[None, :] >= cs) & (qc[None, :] < cs + NA_COLS)
    sel_r, row_ok = [], []
    for i in (0, 1, 2, NQ - 2, NQ - 1):
        start_row = min(max(2 * i - 4, 0), rows - 10)
        qr = 2 * i + np.arange(2)
        kr = start_row + np.arange(10)
        rs = np.clip(qr - NA_ROWS // 2, 0, rows - NA_ROWS)[:, None]
        row_ok.append((kr[None] >= rs) & (kr[None] < rs + NA_ROWS))
        dr = np.clip(kr[None] - qr[:, None] + NA_ROWS - 1, 0, n_dr - 1)
        sel_r.append(np.eye(n_dr, dtype=np.float32)[dr])
    b = jnp.einsum("hrc,pabr,qkc->phaqbk", rpb.astype(F32), jnp.asarray(np.stack(sel_r)), sel_c,
                   precision=lax.Precision.HIGHEST)
    valid = np.stack(row_ok)[:, None, :, None, :, None] & col_ok[None, None, None, :, None, :]
    return jnp.where(valid, b, NEG).reshape(5, D_HEADS, QBLK, NA_WIN)


def _outproj_kernel(ma_ref, mb_ref, mc_ref, md_ref, w_ref, x_ref, mod_ref, g1_ref, g2_ref,
                    wr_ref, rb_ref, xn_ref, h2b_ref, h2t_ref, wt_ref, ls_ref, cnt_ref):
    d = D_MODEL
    mix = jnp.concatenate([ma_ref[...], mb_ref[...], mc_ref[...], md_ref[...]], axis=1)
    y = _rms(_dot(mix, w_ref[...]), g1_ref[...])
    xn = x_ref[...] + mod_ref[:, 2 * d:3 * d] * y
    xn_ref[...] = xn
    h2 = _rms(xn, g2_ref[...]) * (1.0 + mod_ref[:, 4 * d:5 * d]) + mod_ref[:, 3 * d:4 * d]
    n = h2.shape[0]
    h2b_ref[...] = h2.astype(BF16)
    for s in range(TILE_ROWS):
        h2t_ref[pl.ds(s, n, stride=TILE_ROWS), :] = h2[:, s * LANES:(s + 1) * LANES]

    s = _sigmoid(_dot_nt(wr_ref[...], h2.astype(BF16)))
    sb = s + rb_ref[...]
    per = N_EXPERTS // N_GROUPS
    slab = [sb[j * 8:(j + 1) * 8, :] for j in range(per)]
    sraw = [s[j * 8:(j + 1) * 8, :] for j in range(per)]
    giota = lax.broadcasted_iota(jnp.int32, (N_GROUPS, n), 0)
    eid = [giota * per + j for j in range(per)]
    ninf = -jnp.inf

    m1 = functools.reduce(jnp.maximum, slab)
    j1 = functools.reduce(jnp.minimum, [jnp.where(slab[j] == m1, j, per) for j in range(per)])
    m2 = functools.reduce(jnp.maximum, [jnp.where(j1 == j, ninf, slab[j]) for j in range(per)])
    gs = m1 + m2

    gsel = jnp.zeros((N_GROUPS, n), jnp.bool_)
    for _ in range(TOPK_GROUPS):
        m = jnp.max(gs, axis=0, keepdims=True)
        gi = jnp.min(jnp.where(gs == m, giota, N_GROUPS), axis=0, keepdims=True)
        hit = giota == gi
        gsel = jnp.logical_or(gsel, hit)
        gs = jnp.where(hit, ninf, gs)
    slab = [jnp.where(gsel, sj, ninf) for sj in slab]

    def pick(hits, vals):
        return jnp.sum(functools.reduce(jnp.add, [jnp.where(hits[j], vals[j], 0.0) for j in range(per)]),
                       axis=0, keepdims=True)

    kiota = lax.broadcasted_iota(jnp.int32, (TOP_K, n), 0)
    wts = jnp.zeros((TOP_K, n), F32)
    chosen = []
    for k in range(TOP_K):
        m = jnp.max(functools.reduce(jnp.maximum, slab), axis=0, keepdims=True)
        cand = functools.reduce(jnp.minimum,
                                [jnp.where(slab[j] == m, eid[j], N_EXPERTS) for j in range(per)])
        ei = jnp.min(cand, axis=0, keepdims=True)
        hits = [eid[j] == ei for j in range(per)]
        slab = [jnp.where(hits[j], ninf, slab[j]) for j in range(per)]
        wts = jnp.where(kiota == k, pick(hits, sraw), wts)
        chosen.append(ei)
    wt_ref[...] = wts / jnp.sum(wts, axis=0, keepdims=True) * ROUTED_SCALE

    hits_k = [[eid[j] == chosen[k] for j in range(per)] for k in range(TOP_K)]
    sel = jnp.concatenate(
        [functools.reduce(jnp.add, [jnp.where(hits_k[k][j], 1.0, 0.0) for k in range(TOP_K)])
         for j in range(per)], axis=0)
    tri = jnp.where(lax.broadcasted_iota(jnp.int32, (n, n), 0) < lax.broadcasted_iota(jnp.int32, (n, n), 1),
                    1.0, 0.0).astype(BF16)
    before = _dot(sel.astype(BF16), tri)
    cnt = jnp.sum(sel, axis=1, keepdims=True)
    run = jnp.floor((cnt + (PIECE - 1.0)) * (1.0 / PIECE)) * PIECE
    low = jnp.where(lax.broadcasted_iota(jnp.int32, (N_EXPERTS, N_EXPERTS), 1)
                    < lax.broadcasted_iota(jnp.int32, (N_EXPERTS, N_EXPERTS), 0), 1.0, 0.0).astype(BF16)
    start = _dot(low, jnp.broadcast_to(run, (N_EXPERTS, LANES)).astype(BF16))[:, 0:1]
    base = before + start
    base = [base[j * 8:(j + 1) * 8, :] for j in range(per)]
    lsl = jnp.zeros((TOP_K, n), F32)
    for k in range(TOP_K):
        lsl = jnp.where(kiota == k, pick(hits_k[k], base), lsl)
    ls_ref[...] = lsl.astype(jnp.int32)
    cnt_ref[...] = jnp.broadcast_to(cnt, (N_EXPERTS, LANES)).astype(jnp.int32)


def _outproj(mixes, w, xall, mod3, g1, g2, wr, rb, nb):
    bsz = xall.shape[0]
    n_lat = SEQ // ROW_BLK
    full = lambda shape: pl.BlockSpec(shape, lambda b, j: (0,) * len(shape))
    rows = lambda w_: pl.BlockSpec((None, ROW_BLK, w_), lambda b, j: (b, j, 0))
    route = pl.BlockSpec((None, None, TOP_K, ROW_BLK), lambda b, j: (b, j, 0, 0))
    return pl.pallas_call(
        _outproj_kernel,
        grid=(bsz, nb),
        in_specs=[rows(256)] * 4 + [
            full((D_MODEL, D_MODEL)), rows(D_MODEL),
            pl.BlockSpec((None, 1, 6 * D_MODEL), lambda b, j: (_mod_row(j, b, n_lat), 0, 0)),
            full((1, D_MODEL)), full((1, D_MODEL)),
            full((N_EXPERTS, D_MODEL)), full((N_EXPERTS, ROW_BLK)),
        ],
        out_specs=[
            rows(D_MODEL), rows(D_MODEL),
            pl.BlockSpec((None, ROW_BLK * TILE_ROWS, LANES), lambda b, j: (b, j, 0)),
            route, route,
            pl.BlockSpec((None, None, N_EXPERTS, LANES), lambda b, j: (b, j, 0, 0)),
        ],
        out_shape=[
            jax.ShapeDtypeStruct((bsz, nb * ROW_BLK, D_MODEL), F32),
            jax.ShapeDtypeStruct((bsz, nb * ROW_BLK, D_MODEL), BF16),
            jax.ShapeDtypeStruct((bsz, nb * ROW_BLK * TILE_ROWS, LANES), F32),
            jax.ShapeDtypeStruct((bsz, nb, TOP_K, ROW_BLK), F32),
            jax.ShapeDtypeStruct((bsz, nb, TOP_K, ROW_BLK), jnp.int32),
            jax.ShapeDtypeStruct((bsz, nb, N_EXPERTS, LANES), jnp.int32),
        ],
        name="outproj",
    )(*mixes, w, xall, mod3, g1, g2, wr, rb)


def _rows(i, n):
    return pl.ds(pl.multiple_of(i * n, n), n)


def _tile_at(row):
    return pl.ds(pl.multiple_of(row, TILE_ROWS), TILE_ROWS)


def _piece(p):
    return _rows(p, PK_PIECE_ROWS)


def _pack_pairs(lo, hi):
    return (lax.shift_right_logical(pltpu.bitcast(lo, U32), jnp.uint32(16))
            | (pltpu.bitcast(hi, U32) & jnp.uint32(0xFFFF0000)))


def _unpack_lo(u):
    return pltpu.bitcast(lax.shift_left(u, jnp.uint32(16)), F32)


def _unpack_hi(u):
    return pltpu.bitcast(u & jnp.uint32(0xFFFF0000), F32)


def _start_pieces(lo, hi, start):
    n = hi - lo

    def quad(q, carry):
        for r in range(4):
            start(lo + 4 * q + r, r % 2)
        return carry
    lax.fori_loop(0, lax.shift_right_logical(n, 2), quad, 0)

    def rest(r, carry):
        start(hi - 1 - r, 0)
        return carry
    lax.fori_loop(0, n & 3, rest, 0)


def _wait_pieces(n, piece_rows, copy_of_rows):
    def group(p, carry):
        copy_of_rows(WAIT_GROUP * piece_rows).wait()
        return carry
    lax.fori_loop(0, lax.shift_right_logical(n, LOG_WAIT_GROUP), group, 0)

    def single(p, carry):
        copy_of_rows(piece_rows).wait()
        return carry
    lax.fori_loop(0, n & (WAIT_GROUP - 1), single, 0)


def _interleave(body, n_pieces, start):
    per = ROW_BLK // N_CHUNKS

    def chunk(c, carry):
        for tt in range(per):
            body(c * per + tt)
        _start_pieces(lax.shift_right_logical(c * n_pieces, LOG_CHUNKS),
                      lax.shift_right_logical((c + 1) * n_pieces, LOG_CHUNKS), start)
        return carry
    lax.fori_loop(0, N_CHUNKS, chunk, 0)


def _dispatch_kernel(np_ref, nu_ref, nt_ref, ls_ref, pd_ref, pdp_ref, td_ref, h_ref, xs_hbm,
                     stg, pk, zblk, sem, zsem, *, n_steps, n_blocks):
    i = pl.program_id(0)
    slot = i % 3
    prev = (i + 2) % 3

    def piece_copy(s, p, d):
        return pltpu.make_async_copy(pk.at[s, _piece(p), :], xs_hbm.at[_piece(d), :], sem.at[s])

    def wait_pieces(s, n):
        _wait_pieces(n, PK_PIECE_ROWS, lambda rows: pltpu.make_async_copy(
            pk.at[s, pl.ds(0, rows), :], xs_hbm.at[pl.ds(0, rows), :], sem.at[s]))

    def zero_piece_copy(d):
        return pltpu.make_async_copy(zblk.at[pl.ds(0, PK_PIECE_ROWS), :], xs_hbm.at[_piece(d), :], zsem.at[0])

    def zero_block_copy(b):
        return pltpu.make_async_copy(zblk, xs_hbm.at[_rows(b, PK_BLK_ROWS), :], zsem.at[1])

    @pl.when(i == 0)
    def _():
        zblk[...] = jnp.zeros_like(zblk)

        def zero(c, carry):
            stg[_rows(c, 2048), :] = jnp.zeros((2048, LANES), F32)
            return carry
        lax.fori_loop(0, STG_ROWS // 2048, zero, 0)

    @pl.when(i >= 3)
    def _():
        wait_pieces(slot, np_ref[jnp.maximum(i - 3, 0)])

    def scatter_token(t):
        tile = h_ref[_rows(t, TILE_ROWS), :].astype(BF16).astype(F32)
        for k in range(TOP_K):
            stg[_tile_at(ls_ref[0, k * ROW_BLK + t]), :] = tile

    np_prev = jnp.where(i >= 1, np_ref[jnp.maximum(i - 1, 0)], 0)
    _interleave(scatter_token, np_prev,
                lambda p, prio: piece_copy(prev, p, pdp_ref[0, p]).start(priority=prio))

    def pack_group(q, carry):
        v = stg[_rows(q, PACK_GROUP * PIECE_ROWS), :].reshape(PACK_GROUP * PIECE // 2, 2 * TILE_ROWS, LANES)
        packed = _pack_pairs(v[:, 0:TILE_ROWS, :], v[:, TILE_ROWS:2 * TILE_ROWS, :])
        pk[slot, _rows(q, PACK_GROUP * PK_PIECE_ROWS), :] = packed.reshape(PACK_GROUP * PK_PIECE_ROWS, LANES)
        return carry
    lax.fori_loop(0, lax.shift_right_logical(np_ref[i] + (PACK_GROUP - 1), LOG_PACK_GROUP), pack_group, 0)

    @pl.when(i == n_steps - 1)
    def _():
        _start_pieces(0, np_ref[i], lambda p, prio: piece_copy(slot, p, pd_ref[0, p]).start(priority=prio))

        def tail(p, carry):
            zero_piece_copy(td_ref[0, p]).start()
            return carry
        lax.fori_loop(0, nt_ref[0], tail, 0)

        def blk(b, carry):
            zero_block_copy(b).start()
            return carry
        lax.fori_loop(nu_ref[0], n_blocks, blk, 0)

        wait_pieces((i + 1) % 3, np_ref[jnp.maximum(i - 2, 0)])
        wait_pieces(prev, np_ref[jnp.maximum(i - 1, 0)])
        wait_pieces(slot, np_ref[i])

        def tail_wait(p, carry):
            zero_piece_copy(0).wait()
            return carry
        lax.fori_loop(0, nt_ref[0], tail_wait, 0)

        def blk_wait(b, carry):
            zero_block_copy(0).wait()
            return carry
        lax.fori_loop(nu_ref[0], n_blocks, blk_wait, 0)


def _dispatch(npieces, n_used, ntail, lslot, piece_dst, tail_dst, h2t, n_steps, n_blocks):
    assert n_steps >= 3
    smem = lambda n, imap: pl.BlockSpec((None, 1, n), imap, memory_space=pltpu.SMEM)
    grid_spec = pltpu.PrefetchScalarGridSpec(
        num_scalar_prefetch=3,
        grid=(n_steps,),
        in_specs=[
            smem(TOP_K * ROW_BLK, lambda i, *_: (i, 0, 0)),
            smem(PIECES_MAX, lambda i, *_: (i, 0, 0)),
            smem(PIECES_MAX, lambda i, *_: (jnp.maximum(i - 1, 0), 0, 0)),
            smem(TAIL_MAX, lambda i, *_: (0, 0, 0)),
            pl.BlockSpec((ROW_BLK * TILE_ROWS, LANES), lambda i, *_: (i, 0)),
        ],
        out_specs=pl.BlockSpec(memory_space=pl.ANY),
        scratch_shapes=[
            pltpu.VMEM((STG_ROWS, LANES), F32),
            pltpu.VMEM((3, PK_STG_ROWS, LANES), U32),
            pltpu.VMEM((PK_BLK_ROWS, LANES), U32),
            pltpu.SemaphoreType.DMA((3,)),
            pltpu.SemaphoreType.DMA((2,)),
        ],
    )
    return pl.pallas_call(
        functools.partial(_dispatch_kernel, n_steps=n_steps, n_blocks=n_blocks),
        grid_spec=grid_spec,
        out_shape=jax.ShapeDtypeStruct((n_blocks * PK_BLK_ROWS, LANES), U32),
        compiler_params=pltpu.CompilerParams(dimension_semantics=("arbitrary",),
                                             vmem_limit_bytes=VMEM_LIMIT),
        name="dispatch",
    )(npieces, n_used, ntail, lslot, piece_dst, piece_dst, tail_dst, h2t)


def _from_tiles(ref, n):
    return jnp.concatenate([ref[pl.ds(s, n, stride=TILE_ROWS), :] for s in range(TILE_ROWS)], axis=1)


def _experts_kernel(be_ref, nu_ref, xs_hbm, wg_ref, wu_ref, wd_ref, ys_hbm,
                    xbuf, ybuf, wgb, wub, wdb, xsem, ysem, *, n_blocks):
    i = pl.program_id(0)
    n_used = nu_ref[0]
    rows = PK_BLK_ROWS
    half = rows // 2
    pairs = MOE_BLK // 2

    def halves(b):
        return [pl.ds(pl.multiple_of(b * rows + h * half, half), half) for h in range(2)]

    def x_copies(b, s):
        return [pltpu.make_async_copy(xs_hbm.at[src, :], xbuf.at[s, pl.ds(h * half, half), :], xsem.at[s])
                for h, src in enumerate(halves(b))]

    def y_copies(b, s):
        return [pltpu.make_async_copy(ybuf.at[s, pl.ds(h * half, half), :], ys_hbm.at[dst, :], ysem.at[s])
                for h, dst in enumerate(halves(b))]

    def start(copies):
        for h, c in enumerate(copies):
            c.start(priority=h)

    def wait(copies):
        for c in copies:
            c.wait()

    @pl.when(i == 0)
    def _():
        start(x_copies(0, 0))

        @pl.when(n_used > 1)
        def _():
            start(x_copies(1, 1))

    @pl.when(i + 2 < n_used)
    def _():
        start(x_copies(i + 2, (i + 2) % 3))

    @pl.when(i < n_used)
    def _():
        wait(x_copies(i, i % 3))

        @pl.when(i >= 2)
        def _():
            wait(y_copies(i - 2, i % 2))

        @pl.when(jnp.logical_or(i == 0, be_ref[i] != be_ref[jnp.maximum(i - 1, 0)]))
        def _():
            wgb[...] = wg_ref[...].astype(BF16)
            wub[...] = wu_ref[...].astype(BF16)
            wdb[...] = wd_ref[...].astype(BF16)

        x = pltpu.bitcast(_from_tiles(xbuf.at[i % 3], pairs), BF16)
        g = _dot(x, wgb[...])
        u = _dot(x, wub[...])
        y = _dot((g * _sigmoid(g) * u).astype(BF16), wdb[...])
        yp = pltpu.bitcast(y.astype(BF16), U32)
        yb = ybuf.at[i % 2]
        for s in range(TILE_ROWS):
            yb[pl.ds(s, pairs, stride=TILE_ROWS), :] = yp[:, s * LANES:(s + 1) * LANES]
        start(y_copies(i, i % 2))

        @pl.when(i == n_used - 1)
        def _():
            @pl.when(i >= 1)
            def _():
                wait(y_copies(i - 1, (i + 1) % 2))
            wait(y_copies(i, i % 2))
            ybuf[0] = jnp.zeros((rows, LANES), U32)

            def fill(b, carry):
                start(y_copies(b, 0))
                return carry
            lax.fori_loop(n_used, n_blocks, fill, 0)

            def fill_wait(b, carry):
                wait(y_copies(0, 0))
                return carry
            lax.fori_loop(n_used, n_blocks, fill_wait, 0)


def _experts(block_e, n_used, x_sorted, wg, wu, wd, layer, n_blocks):
    wspec = lambda shape: pl.BlockSpec((None, None) + shape, lambda i, be, nu: (layer, be[i], 0, 0))
    rows = PK_BLK_ROWS
    grid_spec = pltpu.PrefetchScalarGridSpec(
        num_scalar_prefetch=2,
        grid=(n_blocks,),
        in_specs=[
            pl.BlockSpec(memory_space=pl.ANY),
            wspec((D_MODEL, EXPERT_DIM)), wspec((D_MODEL, EXPERT_DIM)), wspec((EXPERT_DIM, D_MODEL)),
        ],
        out_specs=pl.BlockSpec(memory_space=pl.ANY),
        scratch_shapes=[
            pltpu.VMEM((3, rows, LANES), U32),
            pltpu.VMEM((2, rows, LANES), U32),
            pltpu.VMEM((D_MODEL, EXPERT_DIM), BF16),
            pltpu.VMEM((D_MODEL, EXPERT_DIM), BF16),
            pltpu.VMEM((EXPERT_DIM, D_MODEL), BF16),
            pltpu.SemaphoreType.DMA((3,)),
            pltpu.SemaphoreType.DMA((2,)),
        ],
    )
    return pl.pallas_call(
        functools.partial(_experts_kernel, n_blocks=n_blocks),
        grid_spec=grid_spec,
        out_shape=jax.ShapeDtypeStruct((n_blocks * rows, LANES), U32),
        compiler_params=pltpu.CompilerParams(dimension_semantics=("arbitrary",)),
        name="experts",
    )(block_e, n_used, x_sorted, wg, wu, wd)


def _combine_kernel(np_ref, lp_ref, wl_ref, wh_ref, cur_ref, nx1_ref, nx2_ref, y_hbm, h2b_ref, xn_ref,
                    mod_ref, g_ref, swg_ref, swu_ref, swd_ref, o_ref, stg, acc, sem, *, n_steps):
    i = pl.program_id(0)
    slot = i % 3
    d = D_MODEL

    def piece_copy(s, p, src):
        return pltpu.make_async_copy(y_hbm.at[_piece(src), :], stg.at[s, _piece(p), :], sem.at[s])

    def fetcher(s, pd_ref):
        return lambda p, prio: piece_copy(s, p, pd_ref[0, p]).start(priority=prio)

    @pl.when(i == 0)
    def _():
        _start_pieces(0, np_ref[0], fetcher(0, cur_ref))
        _start_pieces(0, np_ref[1], fetcher(1, nx1_ref))

    _wait_pieces(np_ref[i], PK_PIECE_ROWS, lambda rows: pltpu.make_async_copy(
        y_hbm.at[pl.ds(0, rows), :], stg.at[slot, pl.ds(0, rows), :], sem.at[slot]))

    def gather_token(t):
        a = None
        for k in range(TOP_K):
            j = k * ROW_BLK + t
            pair = stg[slot, _tile_at(lp_ref[0, j]), :]
            v = _unpack_lo(pair) * wl_ref[0, j] + _unpack_hi(pair) * wh_ref[0, j]
            a = v if a is None else a + v
        acc[_rows(t, TILE_ROWS), :] = a

    np_next = jnp.where(i + 2 < n_steps, np_ref[jnp.minimum(i + 2, n_steps - 1)], 0)
    _interleave(gather_token, np_next, fetcher((i + 2) % 3, nx2_ref))

    hb = h2b_ref[...]
    g = _dot(hb, swg_ref[...])
    u = _dot(hb, swu_ref[...])
    f = _dot((g * _sigmoid(g) * u).astype(BF16), swd_ref[...]) + _from_tiles(acc, ROW_BLK)
    o_ref[...] = xn_ref[...] + mod_ref[:, 5 * d:6 * d] * _rms(f, g_ref[...])


def _combine(npieces, pair_row, w_lo, w_hi, piece_src, y_sorted, h2b_rows, xn_rows, mod3, g, swg, swu, swd,
             n_steps, per_batch):
    assert n_steps >= 3
    last = n_steps - 1
    full = lambda shape: pl.BlockSpec(shape, lambda n, *_: (0,) * len(shape))
    rows = lambda w_: pl.BlockSpec((ROW_BLK, w_), lambda n, *_: (n, 0))
    smem = lambda w_, imap: pl.BlockSpec((None, 1, w_), imap, memory_space=pltpu.SMEM)
    n_lat = SEQ // ROW_BLK
    grid_spec = pltpu.PrefetchScalarGridSpec(
        num_scalar_prefetch=1,
        grid=(n_steps,),
        in_specs=[
            smem(TOP_K * ROW_BLK, lambda n, *_: (n, 0, 0)),
            smem(TOP_K * ROW_BLK, lambda n, *_: (n, 0, 0)),
            smem(TOP_K * ROW_BLK, lambda n, *_: (n, 0, 0)),
            smem(PIECES_MAX, lambda n, *_: (n, 0, 0)),
            smem(PIECES_MAX, lambda n, *_: (jnp.minimum(n + 1, last), 0, 0)),
            smem(PIECES_MAX, lambda n, *_: (jnp.minimum(n + 2, last), 0, 0)),
            pl.BlockSpec(memory_space=pl.ANY),
            rows(D_MODEL), rows(D_MODEL),
            pl.BlockSpec((None, 1, 6 * D_MODEL),
                         lambda n, *_: (_mod_row(n % per_batch, n // per_batch, n_lat), 0, 0)),
            full((1, D_MODEL)),
            full((D_MODEL, SHARED_DIM)), full((D_MODEL, SHARED_DIM)), full((SHARED_DIM, D_MODEL)),
        ],
        out_specs=rows(D_MODEL),
        scratch_shapes=[
            pltpu.VMEM((3, PK_STG_ROWS, LANES), U32),
            pltpu.VMEM((ROW_BLK * TILE_ROWS, LANES), F32),
            pltpu.SemaphoreType.DMA((3,)),
        ],
    )
    return pl.pallas_call(
        functools.partial(_combine_kernel, n_steps=n_steps),
        grid_spec=grid_spec,
        out_shape=jax.ShapeDtypeStruct(xn_rows.shape, F32),
        compiler_params=pltpu.CompilerParams(dimension_semantics=("arbitrary",),
                                             vmem_limit_bytes=VMEM_LIMIT),
        name="combine",
    )(npieces, pair_row, w_lo, w_hi, piece_src, piece_src, piece_src, y_sorted, h2b_rows, xn_rows, mod3, g,
      swg, swu, swd)


def _moe_plan(cnt_rows, n_blocks):
    i32 = jnp.int32
    per_blk = MOE_BLK // PIECE
    to_expert = lambda v: v.reshape(v.shape[:-1] + (8, 8)).swapaxes(-1, -2).reshape(v.shape)
    runs = (cnt_rows + PIECE - 1) // PIECE
    loc_end = jnp.cumsum(runs, axis=1)
    loc = loc_end - runs
    npieces = loc_end[:, -1].astype(i32)
    reg = to_expert(runs.sum(axis=0))
    padded = (reg + per_blk - 1) // per_blk * per_blk
    pad_end = jnp.cumsum(padded)
    pad_start = pad_end - padded
    off = to_expert(pad_start)[None, :] + jnp.cumsum(runs, axis=0) - runs
    p = jnp.arange(PIECES_MAX, dtype=i32)[None, :, None]
    mine = (loc[:, None, :] <= p) & (p < loc_end[:, None, :])
    piece_dst = jnp.sum(jnp.where(mine, (off - loc)[:, None, :], 0), axis=-1) + p[:, :, 0]
    tail_n = padded - reg
    t_end = jnp.cumsum(tail_n)
    t_beg = t_end - tail_n
    q = jnp.arange(TAIL_MAX, dtype=i32)[:, None]
    tmine = (t_beg[None, :] <= q) & (q < t_end[None, :])
    tail_dst = jnp.sum(jnp.where(tmine, (pad_start + reg - t_beg)[None, :], 0), axis=-1) + q[:, 0]
    first = jnp.arange(n_blocks, dtype=i32)[:, None] * per_blk
    block_e = jnp.minimum(jnp.sum((pad_end[None, :] <= first).astype(i32), axis=-1), N_EXPERTS - 1)
    n_used = (pad_end[-1] // per_blk).astype(i32).reshape(1)
    return (npieces, piece_dst.astype(i32).reshape(-1, 1, PIECES_MAX),
            tail_dst.astype(i32).reshape(1, 1, TAIL_MAX), t_end[-1].astype(i32).reshape(1),
            block_e.astype(i32), n_used)


def _axial_tables(dim):
    f32 = np.float32
    t = np.arange(SEQ)
    row = (t // GRID_W).astype(f32)
    col = (t % GRID_W).astype(f32)
    quarter = dim // 4
    freqs = (f32(ROPE_THETA) ** (-np.arange(quarter, dtype=f32) / f32(quarter))).astype(f32)
    ar, ac = row[:, None] * freqs, col[:, None] * freqs
    ang = np.concatenate([ar, ar, ac, ac], axis=-1)
    sign = np.where((np.arange(dim) // quarter) % 2 == 0, -1.0, 1.0).astype(f32)
    cos = np.concatenate([np.cos(ang), np.ones((CTX_LEN, dim), f32)], axis=0)
    sin = np.concatenate([np.sin(ang) * sign, np.zeros((CTX_LEN, dim), f32)], axis=0)
    return cos.astype(f32), sin.astype(f32)


def _rope_tables():
    f32 = np.float32
    cos_h, sin_h = _axial_tables(HEAD_DIM)
    cos_r, sin_r = _axial_tables(B_ROPE)
    cosh, sinh = np.tile(cos_h, (1, 4)), np.tile(sin_h, (1, 4))
    scale_b = f32((B_NOPE + B_ROPE) ** -0.5)
    ones, zeros = np.ones((TOK, 64), f32), np.zeros((TOK, 64), f32)
    pad1, pad0 = np.ones((TOK, 32), f32), np.zeros((TOK, 32), f32)
    cosb = np.tile(np.concatenate([ones, cos_r, pad1], axis=1) * scale_b, (1, 4))
    sinb = np.tile(np.concatenate([zeros, sin_r, pad0], axis=1) * scale_b, (1, 4))
    cosr = np.concatenate([cos_r, np.ones((TOK, 96), f32)], axis=1)
    sinr = np.concatenate([sin_r, np.zeros((TOK, 96), f32)], axis=1)
    return tuple(jnp.asarray(a, F32) for a in (cosh, sinh, cosb, sinb, cosr, sinr))


_GQA_ORDER = (0, 2, 1, 3)


def _take_heads(w, base, order, axis):
    return jnp.concatenate(
        [lax.slice_in_dim(w, base + h * HEAD_DIM, base + (h + 1) * HEAD_DIM, axis=axis) for h in order],
        axis=axis)


def _w_in_layout(w_in):
    zeros = lambda n: jnp.zeros((D_MODEL, n), w_in.dtype)
    a_q = _take_heads(w_in, 0, _GQA_ORDER, 1)
    a_kv = w_in[:, 256:512]
    b_q = jnp.concatenate(
        [jnp.concatenate([w_in[:, 512 + h * 96:512 + (h + 1) * 96], zeros(32)], axis=1)
         for h in range(B_HEADS)], axis=1)
    b_c = w_in[:, 896:1024]
    b_r = jnp.concatenate([w_in[:, 1024:1056], zeros(96)], axis=1)
    c_q = _take_heads(w_in, 1056, _GQA_ORDER, 1)
    rest = w_in[:, 1312:2336]
    return jnp.concatenate([a_q, a_kv, b_q, b_c, b_r, c_q, rest], axis=1).astype(BF16)


def _w_kv_layout(w_kv_up):
    zeros = jnp.zeros((B_KV_RANK, 64), w_kv_up.dtype)
    nope = [jnp.concatenate([w_kv_up[:, h * 128:h * 128 + 64], zeros], axis=1) for h in range(B_HEADS)]
    val = [w_kv_up[:, h * 128 + 64:(h + 1) * 128] for h in range(B_HEADS)]
    return jnp.concatenate(nope + val, axis=1).astype(BF16)


def _w_out_layout(w_out):
    return jnp.concatenate([_take_heads(w_out, 0, _GQA_ORDER, 0), w_out[256:512],
                            _take_heads(w_out, 512, _GQA_ORDER, 0), w_out[768:1024]], axis=0).astype(BF16)


def _router_rows(v):
    return v.reshape((8, 8) + v.shape[1:]).swapaxes(0, 1).reshape(v.shape)


def kernel(x, c, ctx, c_ctx, ada_w, ada_b, pre_mix_g, post_mix_g, pre_ffn_g, post_ffn_g, w_in, a_q_norm, a_k_norm, b_kv_norm, b_w_kv_up, c_sink, d_rpb, w_out, router_w, router_b, exp_w_gate, exp_w_up, exp_w_down, sh_w_gate, sh_w_up, sh_w_down):
    bsz = x.shape[0]
    depth = ada_w.shape[0]
    xall = jnp.concatenate([x, ctx], axis=1)

    cvec = jnp.zeros((16, D_MODEL), F32).at[:bsz].set(c).at[8].set(c_ctx)
    mod = _ada(cvec, ada_w, ada_b)
    tabs = _rope_tables()
    avg = jnp.asarray(np.kron(np.eye(4, dtype=np.float32), np.full((64, 64), 1.0 / 64, np.float32)), BF16)
    row2 = lambda v: v.reshape(1, -1).astype(F32)

    for l in range(depth):
        with_ctx = l < depth - 1
        mod3 = mod[l].reshape(16, 1, 6 * D_MODEL)

        proj = _inproj(
            xall, mod3, row2(pre_mix_g[l]), _w_in_layout(w_in[l]),
            row2(jnp.tile(a_q_norm[l], 4) * HEAD_DIM ** -0.5), row2(jnp.tile(a_k_norm[l], 2)),
            row2(b_kv_norm[l]), _w_kv_layout(b_w_kv_up[l]), avg, tabs)
        qa, ka, va, qb, kb, vb, qc, kc, vc, qd, kd, vd = proj

        nq = NQ + (CTX_LEN // QBLK if with_ctx else 0)
        full = lambda shape: pl.BlockSpec(shape, lambda b, i: (0,) * len(shape))
        mix_a = _attn_call(_attn_a_kernel, "attn_a", qa, ka, va, [], [], nq // 2, qblk=2 * QBLK)
        mix_b = _attn_call(_attn_b_kernel, "attn_b", qb, kb, vb, [], [], nq // 2, qblk=2 * QBLK)
        sink = jnp.zeros((8, LANES), F32).at[:4].set(
            jnp.broadcast_to(c_sink[l][np.array(_GQA_ORDER)][:, None], (4, LANES)))
        mix_c = _attn_call(_attn_c_kernel, "attn_c", qc, kc, vc, [sink], [full((8, LANES))], nq)
        bias = _na_bias(d_rpb[l])
        mix_d = _attn_call(
            _attn_d_kernel, "attn_d", qd, kd, vd, [bias],
            [pl.BlockSpec((None, D_HEADS, QBLK, NA_WIN), lambda b, i: (_na_pattern(i), 0, 0, 0))], nq)

        nb = (TOK if with_ctx else SEQ) // ROW_BLK
        rb = jnp.broadcast_to(_router_rows(router_b[l])[:, None], (N_EXPERTS, ROW_BLK)).astype(F32)
        xn, h2b, h2t, wt_t, ls_t, cnt_t = _outproj(
            (mix_a, mix_b, mix_c, mix_d), _w_out_layout(w_out[l]), xall, mod3,
            row2(post_mix_g[l]), row2(pre_ffn_g[l]),
            _router_rows(router_w[l].T).astype(BF16), rb, nb)

        t_used = nb * ROW_BLK
        n_tok = bsz * t_used
        n_steps = bsz * nb
        n_blocks = -(-(n_tok * TOP_K + n_steps * N_EXPERTS * (PIECE - 1) + N_EXPERTS * (MOE_BLK - 1))
                     // MOE_BLK)
        npieces, piece_dst, tail_dst, ntail, block_e, n_used = _moe_plan(
            cnt_t[..., 0].reshape(n_steps, N_EXPERTS), n_blocks)
        lslot = ls_t.reshape(n_steps, 1, TOP_K * ROW_BLK)
        wts = wt_t.reshape(n_steps, 1, TOP_K * ROW_BLK)
        stg_row = lslot * TILE_ROWS
        pair_row = (lslot // 2) * TILE_ROWS
        w_lo = jnp.where(lslot % 2 == 0, wts, 0.0)
        w_hi = wts - w_lo

        x_sorted = _dispatch(npieces, n_used, ntail, stg_row, piece_dst, tail_dst,
                             h2t.reshape(n_tok * TILE_ROWS, LANES), n_steps, n_blocks)
        y_sorted = _experts(block_e, n_used, x_sorted, exp_w_gate, exp_w_up, exp_w_down, l, n_blocks)
        xall = _combine(
            npieces, pair_row, w_lo, w_hi, piece_dst, y_sorted, h2b.reshape(n_tok, D_MODEL),
            xn.reshape(n_tok, D_MODEL), mod3, row2(post_ffn_g[l]),
            sh_w_gate[l].astype(BF16), sh_w_up[l].astype(BF16), sh_w_down[l].astype(BF16),
            n_steps, nb).reshape(bsz, t_used, D_MODEL)

    return xall[:, :SEQ]
```

```python
import functools

import numpy as np
import jax
import jax.numpy as jnp
from jax import lax
from jax.experimental import pallas as pl
from jax.experimental.pallas import tpu as pltpu

F32 = jnp.float32
BF16 = jnp.bfloat16
U32 = jnp.uint32

D_MODEL = 1024
SEQ = 2048
CTX_LEN = 256
TOK = SEQ + CTX_LEN
GRID_W = 64
HEAD_DIM = 64
ROPE_THETA = 10000.0
EPS = 1e-6
NEG = -1e30

A_HEADS, A_KV_HEADS = 4, 2
B_HEADS, B_NOPE, B_ROPE, B_V, B_KV_RANK = 4, 64, 32, 64, 128
C_HEADS, C_KV_HEADS, C_WINDOW = 4, 2, 128
D_HEADS, NA_ROWS, NA_COLS = 4, 8, 16

N_EXPERTS, TOP_K, N_GROUPS, TOPK_GROUPS = 64, 8, 8, 4
EXPERT_DIM, SHARED_DIM = 256, 256
ROUTED_SCALE = 2.5

LANES = 128
QBLK = 128
ROW_BLK = 256
IN_BLK = 768
MOE_BLK = 512
TILE_ROWS = D_MODEL // LANES
PIECE = 8
PIECE_ROWS = PIECE * TILE_ROWS
PIECES_MAX = 320
STG_ROWS = PIECES_MAX * PIECE_ROWS
PK_PIECE_ROWS = PIECE_ROWS // 2
PK_STG_ROWS = STG_ROWS // 2
PK_BLK_ROWS = MOE_BLK * TILE_ROWS // 2
TAIL_MAX = N_EXPERTS * MOE_BLK // PIECE
LOG_PACK_GROUP = 2
PACK_GROUP = 1 << LOG_PACK_GROUP
LOG_WAIT_GROUP = 4
WAIT_GROUP = 1 << LOG_WAIT_GROUP
LOG_CHUNKS = 2
N_CHUNKS = 1 << LOG_CHUNKS
VMEM_LIMIT = 56 * 1024 * 1024
NQ = SEQ // QBLK
NA_WIN = 10 * GRID_W

_AQ, _AK, _AV, _BQ, _BC, _BR = 0, 256, 384, 512, 1024, 1152
_CQ, _CK, _CV, _DQ, _DK, _DV = 1280, 1536, 1664, 1792, 2048, 2304
IN_W = 2560

_NT = (((1,), (1,)), ((), ()))


def _sigmoid(x):
    return 1.0 / (1.0 + jnp.exp(-x))


def _rms(x, g):
    return x * lax.rsqrt(jnp.mean(x * x, axis=-1, keepdims=True) + EPS) * g


def _dot(a, b):
    return jnp.dot(a, b, preferred_element_type=F32)


def _dot_nt(a, b):
    return lax.dot_general(a, b, _NT, preferred_element_type=F32)


def _ada_kernel(c_ref, w_ref, b_ref, o_ref):
    c = c_ref[...]
    a = (c * _sigmoid(c)).astype(BF16)
    o_ref[...] = _dot(a, w_ref[...].astype(BF16)) + b_ref[...]


def _ada(cvec, ada_w, ada_b):
    depth = ada_w.shape[0]
    n = ada_w.shape[2]
    tn = 1536
    return pl.pallas_call(
        _ada_kernel,
        grid=(depth, n // tn),
        in_specs=[
            pl.BlockSpec((16, D_MODEL), lambda l, j: (0, 0)),
            pl.BlockSpec((None, D_MODEL, tn), lambda l, j: (l, 0, j)),
            pl.BlockSpec((None, 1, tn), lambda l, j: (l, 0, j)),
        ],
        out_specs=pl.BlockSpec((None, 16, tn), lambda l, j: (l, 0, j)),
        out_shape=jax.ShapeDtypeStruct((depth, 16, n), F32),
        name="ada",
    )(cvec, ada_w, ada_b.reshape(depth, 1, n))


def _rope(x, cos, sin, chunk):
    outs = []
    for c in range(x.shape[1] // LANES):
        sl = slice(c * LANES, (c + 1) * LANES)
        xs = x[:, sl]
        lane = lax.broadcasted_iota(jnp.int32, xs.shape, 1)
        first = (lane & chunk) == 0
        rot = jnp.where(first, pltpu.roll(xs, LANES - chunk, 1), pltpu.roll(xs, chunk, 1))
        outs.append(xs * cos[:, sl] + rot * sin[:, sl])
    return outs[0] if len(outs) == 1 else jnp.concatenate(outs, axis=1)


def _head_norm(x, avg):
    sq = x * x
    hi = sq.astype(BF16)
    lo = (sq - hi.astype(F32)).astype(BF16)
    ms = _dot(hi, avg) + _dot(lo, avg)
    return x * lax.rsqrt(ms + EPS)


def _inproj_kernel(x_ref, mod_ref, modc_ref, g_ref, w_ref, qg_ref, kg_ref, kvg_ref, wkv_ref, avg_ref,
                   cosh_ref, sinh_ref, cosb_ref, sinb_ref, cosr_ref, sinr_ref,
                   qa_ref, ka_ref, va_ref, qb_ref, kb_ref, vb_ref,
                   qc_ref, kc_ref, vc_ref, qd_ref, kd_ref, vd_ref):
    d = D_MODEL
    n = x_ref.shape[0]
    is_ctx = pl.program_id(1) * n + lax.broadcasted_iota(jnp.int32, (n, 1), 0) >= SEQ
    scale = jnp.where(is_ctx, modc_ref[:, d:2 * d], mod_ref[:, d:2 * d])
    shift = jnp.where(is_ctx, modc_ref[:, 0:d], mod_ref[:, 0:d])
    h = _rms(x_ref[...], g_ref[...]) * (1.0 + scale) + shift
    p = _dot(h.astype(BF16), w_ref[...])
    cosh, sinh = cosh_ref[...], sinh_ref[...]
    avg = avg_ref[...]

    qa = _head_norm(p[:, _AQ:_AQ + 256], avg) * qg_ref[...]
    qa_ref[...] = _rope(qa, cosh, sinh, 16).astype(BF16)
    ka = _head_norm(p[:, _AK:_AK + 128], avg[0:128, 0:128]) * kg_ref[...]
    ka_ref[...] = _rope(ka, cosh[:, 0:128], sinh[:, 0:128], 16).astype(BF16)
    ones = jnp.ones((p.shape[0], LANES), BF16)
    va_ref[...] = jnp.concatenate([p[:, _AV:_AV + 128].astype(BF16), ones], axis=1)

    qb_ref[...] = _rope(p[:, _BQ:_BQ + 512], cosb_ref[...], sinb_ref[...], 8).astype(BF16)
    cn = _rms(p[:, _BC:_BC + 128], kvg_ref[...])
    kv = _dot(cn.astype(BF16), wkv_ref[...])
    kr = _rope(p[:, _BR:_BR + 128], cosr_ref[...], sinr_ref[...], 8)
    kr = pltpu.roll(kr, 64, 1)
    kb_ref[...] = jnp.concatenate(
        [kv[:, hh * 128:(hh + 1) * 128] + kr for hh in range(B_HEADS)], axis=1).astype(BF16)
    vb_ref[...] = jnp.concatenate(
        [kv[:, 512:640].astype(BF16), ones, kv[:, 640:768].astype(BF16), ones], axis=1)

    qc_ref[...] = _rope(p[:, _CQ:_CQ + 256] * 0.125, cosh, sinh, 16).astype(BF16)
    kc_ref[...] = _rope(p[:, _CK:_CK + 128], cosh[:, 0:128], sinh[:, 0:128], 16).astype(BF16)
    vc_ref[...] = jnp.concatenate([p[:, _CV:_CV + 128].astype(BF16), ones], axis=1)

    qd_ref[...] = (p[:, _DQ:_DQ + 256] * 0.125).astype(BF16)
    kd_ref[...] = p[:, _DK:_DK + 256].astype(BF16)
    vd_ref[...] = jnp.concatenate(
        [p[:, _DV:_DV + 128].astype(BF16), ones, p[:, _DV + 128:_DV + 256].astype(BF16), ones], axis=1)


def _mod_row(j, b, n_lat_blocks):
    return jnp.where(j < n_lat_blocks, b, 8)


def _inproj(xall, mod3, g, w, qg, kg, kvg, wkv, avg, tabs):
    bsz = xall.shape[0]
    nb = TOK // IN_BLK
    full = lambda shape: pl.BlockSpec(shape, lambda b, j: (0,) * len(shape))
    row = lambda w_: pl.BlockSpec((IN_BLK, w_), lambda b, j: (j, 0))
    out = lambda w_: pl.BlockSpec((None, IN_BLK, w_), lambda b, j: (b, j, 0))
    widths = (256, 128, 256, 512, 512, 512, 256, 128, 256, 256, 256, 512)
    return pl.pallas_call(
        _inproj_kernel,
        grid=(bsz, nb),
        in_specs=[
            pl.BlockSpec((None, IN_BLK, D_MODEL), lambda b, j: (b, j, 0)),
            pl.BlockSpec((None, 1, 6 * D_MODEL), lambda b, j: (b, 0, 0)),
            pl.BlockSpec((None, 1, 6 * D_MODEL), lambda b, j: (8, 0, 0)),
            full((1, D_MODEL)), full((D_MODEL, IN_W)),
            full((1, 256)), full((1, 128)), full((1, 128)), full((128, 768)), full((256, 256)),
            row(256), row(256), row(512), row(512), row(128), row(128),
        ],
        out_specs=[out(w_) for w_ in widths],
        out_shape=[jax.ShapeDtypeStruct((bsz, TOK, w_), BF16) for w_ in widths],
        compiler_params=pltpu.CompilerParams(vmem_limit_bytes=VMEM_LIMIT),
        name="inproj",
    )(xall, mod3, mod3, g, w, qg, kg, kvg, wkv, avg, *tabs)


def _softmax_pv(segs, sink=None):
    m = None
    for s, _ in segs:
        ms = jnp.max(s, axis=-1, keepdims=True)
        m = ms if m is None else jnp.maximum(m, ms)
    if sink is not None:
        m = jnp.maximum(m, sink)
    acc = None
    for s, v1 in segs:
        r = _dot(jnp.exp((s - m).astype(BF16)), v1)
        acc = r if acc is None else acc + r
    l = acc[:, LANES:LANES + 1]
    if sink is not None:
        l = l + jnp.exp(sink - m)
    return acc[:, 0:LANES] / l


def _lane_lo(dtype_shape):
    return lax.broadcasted_iota(jnp.int32, dtype_shape, 1) < HEAD_DIM


def _half(qt, hh):
    lo = _lane_lo(qt.shape)
    return jnp.where(lo if hh == 0 else jnp.logical_not(lo), qt, jnp.zeros_like(qt))


def _merge_halves(o0, o1):
    return jnp.where(_lane_lo(o0.shape), o0, o1)


def _stack_heads(q_ref, tiles):
    return jnp.concatenate(
        [_half(q_ref[:, t * LANES:(t + 1) * LANES], hh) for t in tiles for hh in range(2)], axis=0)


def _store_heads(o_ref, o, tiles):
    n = o_ref.shape[0]
    for ti, t in enumerate(tiles):
        o0, o1 = o[2 * ti * n:(2 * ti + 1) * n], o[(2 * ti + 1) * n:(2 * ti + 2) * n]
        o_ref[:, t * LANES:(t + 1) * LANES] = _merge_halves(o0, o1).astype(o_ref.dtype)


def _attn_a_kernel(q_ref, k_ref, v_ref, o_ref, *, n_lat):
    def run(k, v1):
        for t in range(2):
            o = [_softmax_pv([(_dot_nt(_half(q_ref[:, t * LANES:(t + 1) * LANES], hh), k), v1)])
                 for hh in range(2)]
            o_ref[:, t * LANES:(t + 1) * LANES] = _merge_halves(o[0], o[1]).astype(o_ref.dtype)

    @pl.when(pl.program_id(1) < n_lat)
    def _():
        run(k_ref[...], v_ref[...])

    @pl.when(pl.program_id(1) >= n_lat)
    def _():
        run(k_ref[SEQ:TOK, :], v_ref[SEQ:TOK, :])


def _attn_b_kernel(q_ref, k_ref, v_ref, o_ref, *, n_lat):
    def run(lo_, hi_):
        for t in range(2):
            v1 = v_ref[lo_:hi_, t * 256:(t + 1) * 256]
            o = []
            for hh in range(2):
                hd = 2 * t + hh
                q = q_ref[:, hd * LANES:(hd + 1) * LANES]
                k = k_ref[lo_:hi_, hd * LANES:(hd + 1) * LANES]
                o.append(_softmax_pv([(_dot_nt(q, k), v1)]))
            o_ref[:, t * LANES:(t + 1) * LANES] = _merge_halves(o[0], o[1]).astype(o_ref.dtype)

    @pl.when(pl.program_id(1) < n_lat)
    def _():
        run(0, TOK)

    @pl.when(pl.program_id(1) >= n_lat)
    def _():
        run(SEQ, TOK)


def _attn_c_kernel(q_ref, k_ref, v_ref, sink_ref, o_ref, *, n_lat):
    i = pl.program_id(1)
    qblk = o_ref.shape[0]
    band = 3 * qblk

    def run(latent):
        if latent:
            start = pl.multiple_of(jnp.clip((i - 1) * qblk, 0, SEQ - band), qblk)
            shape = (2 * qblk, band)
            qpos = i * qblk + (lax.broadcasted_iota(jnp.int32, shape, 0) & (qblk - 1))
            kpos = start + lax.broadcasted_iota(jnp.int32, shape, 1)
            valid = jnp.abs(qpos - kpos) <= C_WINDOW
        for t in range(2):
            q = _stack_heads(q_ref, (t,))
            segs = [(_dot_nt(q, k_ref[SEQ:TOK, :]), v_ref[SEQ:TOK, :])]
            if latent:
                s = jnp.where(valid, _dot_nt(q, k_ref[pl.ds(start, band), :]), NEG)
                segs.insert(0, (s, v_ref[pl.ds(start, band), :]))
            sink = jnp.concatenate(
                [jnp.broadcast_to(sink_ref[j:j + 1, 0:1], (qblk, 1)) for j in (2 * t, 2 * t + 1)], axis=0)
            _store_heads(o_ref, _softmax_pv(segs, sink=sink), (t,))

    @pl.when(i < n_lat)
    def _():
        run(True)

    @pl.when(i >= n_lat)
    def _():
        run(False)


def _attn_d_kernel(q_ref, k_ref, v_ref, bias_ref, o_ref, *, n_lat):
    i = pl.program_id(1)
    qblk = o_ref.shape[0]

    def run(latent):
        if latent:
            start = pl.multiple_of(jnp.clip((i - 2) * qblk, 0, SEQ - NA_WIN), qblk)
        for t in range(2):
            ksl, vsl = slice(t * LANES, (t + 1) * LANES), slice(t * 256, (t + 1) * 256)
            q = _stack_heads(q_ref, (t,))
            segs = [(_dot_nt(q, k_ref[SEQ:TOK, ksl]), v_ref[SEQ:TOK, vsl])]
            if latent:
                bias = bias_ref[2 * t:2 * t + 2].reshape(2 * qblk, NA_WIN)
                segs.insert(0, (_dot_nt(q, k_ref[pl.ds(start, NA_WIN), ksl]) + bias,
                                v_ref[pl.ds(start, NA_WIN), vsl]))
            _store_heads(o_ref, _softmax_pv(segs), (t,))

    @pl.when(i < n_lat)
    def _():
        run(True)

    @pl.when(i >= n_lat)
    def _():
        run(False)


def _attn_call(kernel, name, q, k, v, extra, extra_specs, n_blocks, qblk=QBLK):
    bsz = q.shape[0]
    return pl.pallas_call(
        functools.partial(kernel, n_lat=SEQ // qblk),
        grid=(bsz, n_blocks),
        in_specs=[
            pl.BlockSpec((None, qblk, q.shape[2]), lambda b, i: (b, i, 0)),
            pl.BlockSpec((None, TOK, k.shape[2]), lambda b, i: (b, 0, 0)),
            pl.BlockSpec((None, TOK, v.shape[2]), lambda b, i: (b, 0, 0)),
        ] + extra_specs,
        out_specs=pl.BlockSpec((None, qblk, 256), lambda b, i: (b, i, 0)),
        out_shape=jax.ShapeDtypeStruct((bsz, n_blocks * qblk, 256), BF16),
        name=name,
    )(q, k, v, *extra)


def _na_pattern(i):
    return jnp.where(i < 2, i, jnp.where(i < NQ - 2, 2, jnp.minimum(i, NQ - 1) - (NQ - 5)))


def _na_bias(rpb):
    rows = SEQ // GRID_W
    n_dr, n_dc = 2 * NA_ROWS - 1, 2 * NA_COLS - 1
    qc = np.arange(GRID_W)
    dc = np.clip(qc[None, :] - qc[:, None] + NA_COLS - 1, 0, n_dc - 1)
    sel_c = jnp.asarray(np.eye(n_dc, dtype=np.float32)[dc])
    cs = np.clip(qc - NA_COLS // 2, 0, GRID_W - NA_COLS)[:, None]
    col_ok = (qc[None, :] >= cs) & (qc[None, :] < cs + NA_COLS)
    sel_r, row_ok = [], []
    for i in (0, 1, 2, NQ - 2, NQ - 1):
        start_row = min(max(2 * i - 4, 0), rows - 10)
        qr = 2 * i + np.arange(2)
        kr = start_row + np.arange(10)
        rs = np.clip(qr - NA_ROWS // 2, 0, rows - NA_ROWS)[:, None]
        row_ok.append((kr[None] >= rs) & (kr[None] < rs + NA_ROWS))
        dr = np.clip(kr[None] - qr[:, None] + NA_ROWS - 1, 0, n_dr - 1)
        sel_r.append(np.eye(n_dr, dtype=np.float32)[dr])
    b = jnp.einsum("hrc,pabr,qkc->phaqbk", rpb.astype(F32), jnp.asarray(np.stack(sel_r)), sel_c,
                   precision=lax.Precision.HIGHEST)
    valid = np.stack(row_ok)[:, None, :, None, :, None] & col_ok[None, None, None, :, None, :]
    return jnp.where(valid, b, NEG).reshape(5, D_HEADS, QBLK, NA_WIN)


def _outproj_kernel(ma_ref, mb_ref, mc_ref, md_ref, w_ref, x_ref, mod_ref, g1_ref, g2_ref,
                    wr_ref, rb_ref, xn_ref, h2b_ref, h2t_ref, wt_ref, ls_ref, cnt_ref):
    d = D_MODEL
    mix = jnp.concatenate([ma_ref[...], mb_ref[...], mc_ref[...], md_ref[...]], axis=1)
    y = _rms(_dot(mix, w_ref[...]), g1_ref[...])
    xn = x_ref[...] + mod_ref[:, 2 * d:3 * d] * y
    xn_ref[...] = xn
    h2 = _rms(xn, g2_ref[...]) * (1.0 + mod_ref[:, 4 * d:5 * d]) + mod_ref[:, 3 * d:4 * d]
    n = h2.shape[0]
    h2b_ref[...] = h2.astype(BF16)
    for s in range(TILE_ROWS):
        h2t_ref[pl.ds(s, n, stride=TILE_ROWS), :] = h2[:, s * LANES:(s + 1) * LANES]

    s = _sigmoid(_dot_nt(wr_ref[...], h2.astype(BF16)))
    sb = s + rb_ref[...]
    per = N_EXPERTS // N_GROUPS
    slab = [sb[j * 8:(j + 1) * 8, :] for j in range(per)]
    sraw = [s[j * 8:(j + 1) * 8, :] for j in range(per)]
    giota = lax.broadcasted_iota(jnp.int32, (N_GROUPS, n), 0)
    eid = [giota * per + j for j in range(per)]
    ninf = -jnp.inf

    m1 = functools.reduce(jnp.maximum, slab)
    j1 = functools.reduce(jnp.minimum, [jnp.where(slab[j] == m1, j, per) for j in range(per)])
    m2 = functools.reduce(jnp.maximum, [jnp.where(j1 == j, ninf, slab[j]) for j in range(per)])
    gs = m1 + m2

    gsel = jnp.zeros((N_GROUPS, n), jnp.bool_)
    for _ in range(TOPK_GROUPS):
        m = jnp.max(gs, axis=0, keepdims=True)
        gi = jnp.min(jnp.where(gs == m, giota, N_GROUPS), axis=0, keepdims=True)
        hit = giota == gi
        gsel = jnp.logical_or(gsel, hit)
        gs = jnp.where(hit, ninf, gs)
    slab = [jnp.where(gsel, sj, ninf) for sj in slab]

    def pick(hits, vals):
        return jnp.sum(functools.reduce(jnp.add, [jnp.where(hits[j], vals[j], 0.0) for j in range(per)]),
                       axis=0, keepdims=True)

    kiota = lax.broadcasted_iota(jnp.int32, (TOP_K, n), 0)
    wts = jnp.zeros((TOP_K, n), F32)
    chosen = []
    for k in range(TOP_K):
        m = jnp.max(functools.reduce(jnp.maximum, slab), axis=0, keepdims=True)
        cand = functools.reduce(jnp.minimum,
                                [jnp.where(slab[j] == m, eid[j], N_EXPERTS) for j in range(per)])
        ei = jnp.min(cand, axis=0, keepdims=True)
        hits = [eid[j] == ei for j in range(per)]
        slab = [jnp.where(hits[j], ninf, slab[j]) for j in range(per)]
        wts = jnp.where(kiota == k, pick(hits, sraw), wts)
        chosen.append(ei)
    wt_ref[...] = wts / jnp.sum(wts, axis=0, keepdims=True) * ROUTED_SCALE

    hits_k = [[eid[j] == chosen[k] for j in range(per)] for k in range(TOP_K)]
    sel = jnp.concatenate(
        [functools.reduce(jnp.add, [jnp.where(hits_k[k][j], 1.0, 0.0) for k in range(TOP_K)])
         for j in range(per)], axis=0)
    tri = jnp.where(lax.broadcasted_iota(jnp.int32, (n, n), 0) < lax.broadcasted_iota(jnp.int32, (n, n), 1),
                    1.0, 0.0).astype(BF16)
    before = _dot(sel.astype(BF16), tri)
    cnt = jnp.sum(sel, axis=1, keepdims=True)
    run = jnp.floor((cnt + (PIECE - 1.0)) * (1.0 / PIECE)) * PIECE
    low = jnp.where(lax.broadcasted_iota(jnp.int32, (N_EXPERTS, N_EXPERTS), 1)
                    < lax.broadcasted_iota(jnp.int32, (N_EXPERTS, N_EXPERTS), 0), 1.0, 0.0).astype(BF16)
    start = _dot(low, jnp.broadcast_to(run, (N_EXPERTS, LANES)).astype(BF16))[:, 0:1]
    base = before + start
    base = [base[j * 8:(j + 1) * 8, :] for j in range(per)]
    lsl = jnp.zeros((TOP_K, n), F32)
    for k in range(TOP_K):
        lsl = jnp.where(kiota == k, pick(hits_k[k], base), lsl)
    ls_ref[...] = lsl.astype(jnp.int32)
    cnt_ref[...] = jnp.broadcast_to(cnt, (N_EXPERTS, LANES)).astype(jnp.int32)


def _outproj(mixes, w, xall, mod3, g1, g2, wr, rb, nb):
    bsz = xall.shape[0]
    n_lat = SEQ // ROW_BLK
    full = lambda shape: pl.BlockSpec(shape, lambda b, j: (0,) * len(shape))
    rows = lambda w_: pl.BlockSpec((None, ROW_BLK, w_), lambda b, j: (b, j, 0))
    route = pl.BlockSpec((None, None, TOP_K, ROW_BLK), lambda b, j: (b, j, 0, 0))
    return pl.pallas_call(
        _outproj_kernel,
        grid=(bsz, nb),
        in_specs=[rows(256)] * 4 + [
            full((D_MODEL, D_MODEL)), rows(D_MODEL),
            pl.BlockSpec((None, 1, 6 * D_MODEL), lambda b, j: (_mod_row(j, b, n_lat), 0, 0)),
            full((1, D_MODEL)), full((1, D_MODEL)),
            full((N_EXPERTS, D_MODEL)), full((N_EXPERTS, ROW_BLK)),
        ],
        out_specs=[
            rows(D_MODEL), rows(D_MODEL),
            pl.BlockSpec((None, ROW_BLK * TILE_ROWS, LANES), lambda b, j: (b, j, 0)),
            route, route,
            pl.BlockSpec((None, None, N_EXPERTS, LANES), lambda b, j: (b, j, 0, 0)),
        ],
        out_shape=[
            jax.ShapeDtypeStruct((bsz, nb * ROW_BLK, D_MODEL), F32),
            jax.ShapeDtypeStruct((bsz, nb * ROW_BLK, D_MODEL), BF16),
            jax.ShapeDtypeStruct((bsz, nb * ROW_BLK * TILE_ROWS, LANES), F32),
            jax.ShapeDtypeStruct((bsz, nb, TOP_K, ROW_BLK), F32),
            jax.ShapeDtypeStruct((bsz, nb, TOP_K, ROW_BLK), jnp.int32),
            jax.ShapeDtypeStruct((bsz, nb, N_EXPERTS, LANES), jnp.int32),
        ],
        name="outproj",
    )(*mixes, w, xall, mod3, g1, g2, wr, rb)


def _rows(i, n):
    return pl.ds(pl.multiple_of(i * n, n), n)


def _tile_at(row):
    return pl.ds(pl.multiple_of(row, TILE_ROWS), TILE_ROWS)


def _piece(p):
    return _rows(p, PK_PIECE_ROWS)


def _pack_pairs(lo, hi):
    return (lax.shift_right_logical(pltpu.bitcast(lo, U32), jnp.uint32(16))
            | (pltpu.bitcast(hi, U32) & jnp.uint32(0xFFFF0000)))


def _unpack_lo(u):
    return pltpu.bitcast(lax.shift_left(u, jnp.uint32(16)), F32)


def _unpack_hi(u):
    return pltpu.bitcast(u & jnp.uint32(0xFFFF0000), F32)


def _start_pieces(lo, hi, start):
    n = hi - lo

    def quad(q, carry):
        for r in range(4):
            start(lo + 4 * q + r, r % 2)
        return carry
    lax.fori_loop(0, lax.shift_right_logical(n, 2), quad, 0)

    def rest(r, carry):
        start(hi - 1 - r, 0)
        return carry
    lax.fori_loop(0, n & 3, rest, 0)


def _wait_pieces(n, piece_rows, copy_of_rows):
    def group(p, carry):
        copy_of_rows(WAIT_GROUP * piece_rows).wait()
        return carry
    lax.fori_loop(0, lax.shift_right_logical(n, LOG_WAIT_GROUP), group, 0)

    def single(p, carry):
        copy_of_rows(piece_rows).wait()
        return carry
    lax.fori_loop(0, n & (WAIT_GROUP - 1), single, 0)


def _interleave(body, n_pieces, start):
    per = ROW_BLK // N_CHUNKS

    def chunk(c, carry):
        for tt in range(per):
            body(c * per + tt)
        _start_pieces(lax.shift_right_logical(c * n_pieces, LOG_CHUNKS),
                      lax.shift_right_logical((c + 1) * n_pieces, LOG_CHUNKS), start)
        return carry
    lax.fori_loop(0, N_CHUNKS, chunk, 0)


def _dispatch_kernel(np_ref, nu_ref, nt_ref, ls_ref, pd_ref, pdp_ref, td_ref, h_ref, xs_hbm,
                     stg, pk, zblk, sem, zsem, *, n_steps, n_blocks):
    i = pl.program_id(0)
    slot = i % 3
    prev = (i + 2) % 3

    def piece_copy(s, p, d):
        return pltpu.make_async_copy(pk.at[s, _piece(p), :], xs_hbm.at[_piece(d), :], sem.at[s])

    def wait_pieces(s, n):
        _wait_pieces(n, PK_PIECE_ROWS, lambda rows: pltpu.make_async_copy(
            pk.at[s, pl.ds(0, rows), :], xs_hbm.at[pl.ds(0, rows), :], sem.at[s]))

    def zero_piece_copy(d):
        return pltpu.make_async_copy(zblk.at[pl.ds(0, PK_PIECE_ROWS), :], xs_hbm.at[_piece(d), :], zsem.at[0])

    def zero_block_copy(b):
        return pltpu.make_async_copy(zblk, xs_hbm.at[_rows(b, PK_BLK_ROWS), :], zsem.at[1])

    @pl.when(i == 0)
    def _():
        zblk[...] = jnp.zeros_like(zblk)

        def zero(c, carry):
            stg[_rows(c, 2048), :] = jnp.zeros((2048, LANES), F32)
            return carry
        lax.fori_loop(0, STG_ROWS // 2048, zero, 0)

        def blk(b, carry):
            zero_block_copy(b).start()
            return carry
        lax.fori_loop(nu_ref[0], n_blocks, blk, 0)

    @pl.when(i >= 3)
    def _():
        wait_pieces(slot, np_ref[jnp.maximum(i - 3, 0)])

    def scatter_token(t):
        tile = h_ref[_rows(t, TILE_ROWS), :].astype(BF16).astype(F32)
        for k in range(TOP_K):
            stg[_tile_at(ls_ref[0, k * ROW_BLK + t]), :] = tile

    np_prev = jnp.where(i >= 1, np_ref[jnp.maximum(i - 1, 0)], 0)
    _interleave(scatter_token, np_prev,
                lambda p, prio: piece_copy(prev, p, pdp_ref[0, p]).start(priority=prio))

    def pack_group(q, carry):
        v = stg[_rows(q, PACK_GROUP * PIECE_ROWS), :].reshape(PACK_GROUP * PIECE // 2, 2 * TILE_ROWS, LANES)
        packed = _pack_pairs(v[:, 0:TILE_ROWS, :], v[:, TILE_ROWS:2 * TILE_ROWS, :])
        pk[slot, _rows(q, PACK_GROUP * PK_PIECE_ROWS), :] = packed.reshape(PACK_GROUP * PK_PIECE_ROWS, LANES)
        return carry
    lax.fori_loop(0, lax.shift_right_logical(np_ref[i] + (PACK_GROUP - 1), LOG_PACK_GROUP), pack_group, 0)

    @pl.when(i == n_steps - 1)
    def _():
        _start_pieces(0, np_ref[i], lambda p, prio: piece_copy(slot, p, pd_ref[0, p]).start(priority=prio))

        def tail(p, carry):
            zero_piece_copy(td_ref[0, p]).start()
            return carry
        lax.fori_loop(0, nt_ref[0], tail, 0)

        wait_pieces((i + 1) % 3, np_ref[jnp.maximum(i - 2, 0)])
        wait_pieces(prev, np_ref[jnp.maximum(i - 1, 0)])
        wait_pieces(slot, np_ref[i])

        def tail_wait(p, carry):
            zero_piece_copy(0).wait()
            return carry
        lax.fori_loop(0, nt_ref[0], tail_wait, 0)

        def blk_wait(b, carry):
            zero_block_copy(0).wait()
            return carry
        lax.fori_loop(nu_ref[0], n_blocks, blk_wait, 0)


def _dispatch(npieces, n_used, ntail, lslot, piece_dst, tail_dst, h2t, n_steps, n_blocks):
    assert n_steps >= 3
    smem = lambda n, imap: pl.BlockSpec((None, 1, n), imap, memory_space=pltpu.SMEM)
    grid_spec = pltpu.PrefetchScalarGridSpec(
        num_scalar_prefetch=3,
        grid=(n_steps,),
        in_specs=[
            smem(TOP_K * ROW_BLK, lambda i, *_: (i, 0, 0)),
            smem(PIECES_MAX, lambda i, *_: (i, 0, 0)),
            smem(PIECES_MAX, lambda i, *_: (jnp.maximum(i - 1, 0), 0, 0)),
            smem(TAIL_MAX, lambda i, *_: (0, 0, 0)),
            pl.BlockSpec((ROW_BLK * TILE_ROWS, LANES), lambda i, *_: (i, 0)),
        ],
        out_specs=pl.BlockSpec(memory_space=pl.ANY),
        scratch_shapes=[
            pltpu.VMEM((STG_ROWS, LANES), F32),
            pltpu.VMEM((3, PK_STG_ROWS, LANES), U32),
            pltpu.VMEM((PK_BLK_ROWS, LANES), U32),
            pltpu.SemaphoreType.DMA((3,)),
            pltpu.SemaphoreType.DMA((2,)),
        ],
    )
    return pl.pallas_call(
        functools.partial(_dispatch_kernel, n_steps=n_steps, n_blocks=n_blocks),
        grid_spec=grid_spec,
        out_shape=jax.ShapeDtypeStruct((n_blocks * PK_BLK_ROWS, LANES), U32),
        compiler_params=pltpu.CompilerParams(dimension_semantics=("arbitrary",),
                                             vmem_limit_bytes=VMEM_LIMIT),
        name="dispatch",
    )(npieces, n_used, ntail, lslot, piece_dst, piece_dst, tail_dst, h2t)


def _from_tiles(ref, n):
    return jnp.concatenate([ref[pl.ds(s, n, stride=TILE_ROWS), :] for s in range(TILE_ROWS)], axis=1)


def _experts_kernel(be_ref, nu_ref, xs_hbm, wg_ref, wu_ref, wd_ref, ys_hbm,
                    xbuf, ybuf, zbuf, wgb, wub, wdb, xsem, ysem, zsem, *, n_blocks):
    i = pl.program_id(0)
    n_used = nu_ref[0]
    rows = PK_BLK_ROWS
    half = rows // 2
    pairs = MOE_BLK // 2

    def halves(b):
        return [pl.ds(pl.multiple_of(b * rows + h * half, half), half) for h in range(2)]

    def x_copies(b, s):
        return [pltpu.make_async_copy(xs_hbm.at[src, :], xbuf.at[s, pl.ds(h * half, half), :], xsem.at[s])
                for h, src in enumerate(halves(b))]

    def y_copies(b, s):
        return [pltpu.make_async_copy(ybuf.at[s, pl.ds(h * half, half), :], ys_hbm.at[dst, :], ysem.at[s])
                for h, dst in enumerate(halves(b))]

    def start(copies):
        for h, c in enumerate(copies):
            c.start(priority=h)

    def wait(copies):
        for c in copies:
            c.wait()

    def zero_fill(b):
        return pltpu.make_async_copy(zbuf, ys_hbm.at[_rows(b, rows), :], zsem.at[0])

    @pl.when(i == 0)
    def _():
        start(x_copies(0, 0))

        @pl.when(n_used > 1)
        def _():
            start(x_copies(1, 1))

        zbuf[...] = jnp.zeros_like(zbuf)

        def fill(b, carry):
            zero_fill(b).start()
            return carry
        lax.fori_loop(n_used, n_blocks, fill, 0)

    @pl.when(i + 2 < n_used)
    def _():
        start(x_copies(i + 2, (i + 2) % 3))

    @pl.when(i < n_used)
    def _():
        wait(x_copies(i, i % 3))

        @pl.when(i >= 2)
        def _():
            wait(y_copies(i - 2, i % 2))

        @pl.when(jnp.logical_or(i == 0, be_ref[i] != be_ref[jnp.maximum(i - 1, 0)]))
        def _():
            wgb[...] = wg_ref[...].astype(BF16)
            wub[...] = wu_ref[...].astype(BF16)
            wdb[...] = wd_ref[...].astype(BF16)

        x = pltpu.bitcast(_from_tiles(xbuf.at[i % 3], pairs), BF16)
        g = _dot(x, wgb[...])
        u = _dot(x, wub[...])
        y = _dot((g * _sigmoid(g) * u).astype(BF16), wdb[...])
        yp = pltpu.bitcast(y.astype(BF16), U32)
        yb = ybuf.at[i % 2]
        for s in range(TILE_ROWS):
            yb[pl.ds(s, pairs, stride=TILE_ROWS), :] = yp[:, s * LANES:(s + 1) * LANES]
        start(y_copies(i, i % 2))

        @pl.when(i == n_used - 1)
        def _():
            @pl.when(i >= 1)
            def _():
                wait(y_copies(i - 1, (i + 1) % 2))
            wait(y_copies(i, i % 2))

            def fill_wait(b, carry):
                zero_fill(0).wait()
                return carry
            lax.fori_loop(n_used, n_blocks, fill_wait, 0)


def _experts(block_e, n_used, x_sorted, wg, wu, wd, layer, n_blocks):
    wspec = lambda shape: pl.BlockSpec((None, None) + shape, lambda i, be, nu: (layer, be[i], 0, 0))
    rows = PK_BLK_ROWS
    grid_spec = pltpu.PrefetchScalarGridSpec(
        num_scalar_prefetch=2,
        grid=(n_blocks,),
        in_specs=[
            pl.BlockSpec(memory_space=pl.ANY),
            wspec((D_MODEL, EXPERT_DIM)), wspec((D_MODEL, EXPERT_DIM)), wspec((EXPERT_DIM, D_MODEL)),
        ],
        out_specs=pl.BlockSpec(memory_space=pl.ANY),
        scratch_shapes=[
            pltpu.VMEM((3, rows, LANES), U32),
            pltpu.VMEM((2, rows, LANES), U32),
            pltpu.VMEM((rows, LANES), U32),
            pltpu.VMEM((D_MODEL, EXPERT_DIM), BF16),
            pltpu.VMEM((D_MODEL, EXPERT_DIM), BF16),
            pltpu.VMEM((EXPERT_DIM, D_MODEL), BF16),
            pltpu.SemaphoreType.DMA((3,)),
            pltpu.SemaphoreType.DMA((2,)),
            pltpu.SemaphoreType.DMA((1,)),
        ],
    )
    return pl.pallas_call(
        functools.partial(_experts_kernel, n_blocks=n_blocks),
        grid_spec=grid_spec,
        out_shape=jax.ShapeDtypeStruct((n_blocks * rows, LANES), U32),
        compiler_params=pltpu.CompilerParams(dimension_semantics=("arbitrary",)),
        name="experts",
    )(block_e, n_used, x_sorted, wg, wu, wd)


def _combine_kernel(np_ref, lp_ref, wl_ref, wh_ref, cur_ref, nx1_ref, nx2_ref, y_hbm, h2b_ref, xn_ref,
                    mod_ref, g_ref, swg_ref, swu_ref, swd_ref, o_ref, stg, acc, sem, *, n_steps):
    i = pl.program_id(0)
    slot = i % 3
    d = D_MODEL

    def piece_copy(s, p, src):
        return pltpu.make_async_copy(y_hbm.at[_piece(src), :], stg.at[s, _piece(p), :], sem.at[s])

    def fetcher(s, pd_ref):
        return lambda p, prio: piece_copy(s, p, pd_ref[0, p]).start(priority=prio)

    @pl.when(i == 0)
    def _():
        _start_pieces(0, np_ref[0], fetcher(0, cur_ref))
        _start_pieces(0, np_ref[1], fetcher(1, nx1_ref))

    _wait_pieces(np_ref[i], PK_PIECE_ROWS, lambda rows: pltpu.make_async_copy(
        y_hbm.at[pl.ds(0, rows), :], stg.at[slot, pl.ds(0, rows), :], sem.at[slot]))

    def gather_token(t):
        a = None
        for k in range(TOP_K):
            j = k * ROW_BLK + t
            pair = stg[slot, _tile_at(lp_ref[0, j]), :]
            v = _unpack_lo(pair) * wl_ref[0, j] + _unpack_hi(pair) * wh_ref[0, j]
            a = v if a is None else a + v
        acc[_rows(t, TILE_ROWS), :] = a

    np_next = jnp.where(i + 2 < n_steps, np_ref[jnp.minimum(i + 2, n_steps - 1)], 0)
    _interleave(gather_token, np_next, fetcher((i + 2) % 3, nx2_ref))

    hb = h2b_ref[...]
    g = _dot(hb, swg_ref[...])
    u = _dot(hb, swu_ref[...])
    f = _dot((g * _sigmoid(g) * u).astype(BF16), swd_ref[...]) + _from_tiles(acc, ROW_BLK)
    o_ref[...] = xn_ref[...] + mod_ref[:, 5 * d:6 * d] * _rms(f, g_ref[...])


def _combine(npieces, pair_row, w_lo, w_hi, piece_src, y_sorted, h2b_rows, xn_rows, mod3, g, swg, swu, swd,
             n_steps, per_batch):
    assert n_steps >= 3
    last = n_steps - 1
    full = lambda shape: pl.BlockSpec(shape, lambda n, *_: (0,) * len(shape))
    rows = lambda w_: pl.BlockSpec((ROW_BLK, w_), lambda n, *_: (n, 0))
    smem = lambda w_, imap: pl.BlockSpec((None, 1, w_), imap, memory_space=pltpu.SMEM)
    n_lat = SEQ // ROW_BLK
    grid_spec = pltpu.PrefetchScalarGridSpec(
        num_scalar_prefetch=1,
        grid=(n_steps,),
        in_specs=[
            smem(TOP_K * ROW_BLK, lambda n, *_: (n, 0, 0)),
            smem(TOP_K * ROW_BLK, lambda n, *_: (n, 0, 0)),
            smem(TOP_K * ROW_BLK, lambda n, *_: (n, 0, 0)),
            smem(PIECES_MAX, lambda n, *_: (n, 0, 0)),
            smem(PIECES_MAX, lambda n, *_: (jnp.minimum(n + 1, last), 0, 0)),
            smem(PIECES_MAX, lambda n, *_: (jnp.minimum(n + 2, last), 0, 0)),
            pl.BlockSpec(memory_space=pl.ANY),
            rows(D_MODEL), rows(D_MODEL),
            pl.BlockSpec((None, 1, 6 * D_MODEL),
                         lambda n, *_: (_mod_row(n % per_batch, n // per_batch, n_lat), 0, 0)),
            full((1, D_MODEL)),
            full((D_MODEL, SHARED_DIM)), full((D_MODEL, SHARED_DIM)), full((SHARED_DIM, D_MODEL)),
        ],
        out_specs=rows(D_MODEL),
        scratch_shapes=[
            pltpu.VMEM((3, PK_STG_ROWS, LANES), U32),
            pltpu.VMEM((ROW_BLK * TILE_ROWS, LANES), F32),
            pltpu.SemaphoreType.DMA((3,)),
        ],
    )
    return pl.pallas_call(
        functools.partial(_combine_kernel, n_steps=n_steps),
        grid_spec=grid_spec,
        out_shape=jax.ShapeDtypeStruct(xn_rows.shape, F32),
        compiler_params=pltpu.CompilerParams(dimension_semantics=("arbitrary",),
                                             vmem_limit_bytes=VMEM_LIMIT),
        name="combine",
    )(npieces, pair_row, w_lo, w_hi, piece_src, piece_src, piece_src, y_sorted, h2b_rows, xn_rows, mod3, g,
      swg, swu, swd)


def _moe_plan(cnt_rows, n_blocks):
    i32 = jnp.int32
    per_blk = MOE_BLK // PIECE
    to_expert = lambda v: v.reshape(v.shape[:-1] + (8, 8)).swapaxes(-1, -2).reshape(v.shape)
    runs = (cnt_rows + PIECE - 1) // PIECE
    loc_end = jnp.cumsum(runs, axis=1)
    loc = loc_end - runs
    npieces = loc_end[:, -1].astype(i32)
    reg = to_expert(runs.sum(axis=0))
    padded = (reg + per_blk - 1) // per_blk * per_blk
    pad_end = jnp.cumsum(padded)
    pad_start = pad_end - padded
    off = to_expert(pad_start)[None, :] + jnp.cumsum(runs, axis=0) - runs
    p = jnp.arange(PIECES_MAX, dtype=i32)[None, :, None]
    mine = (loc[:, None, :] <= p) & (p < loc_end[:, None, :])
    piece_dst = jnp.sum(jnp.where(mine, (off - loc)[:, None, :], 0), axis=-1) + p[:, :, 0]
    tail_n = padded - reg
    t_end = jnp.cumsum(tail_n)
    t_beg = t_end - tail_n
    q = jnp.arange(TAIL_MAX, dtype=i32)[:, None]
    tmine = (t_beg[None, :] <= q) & (q < t_end[None, :])
    tail_dst = jnp.sum(jnp.where(tmine, (pad_start + reg - t_beg)[None, :], 0), axis=-1) + q[:, 0]
    first = jnp.arange(n_blocks, dtype=i32)[:, None] * per_blk
    block_e = jnp.minimum(jnp.sum((pad_end[None, :] <= first).astype(i32), axis=-1), N_EXPERTS - 1)
    n_used = (pad_end[-1] // per_blk).astype(i32).reshape(1)
    return (npieces, piece_dst.astype(i32).reshape(-1, 1, PIECES_MAX),
            tail_dst.astype(i32).reshape(1, 1, TAIL_MAX), t_end[-1].astype(i32).reshape(1),
            block_e.astype(i32), n_used)


def _axial_tables(dim):
    f32 = np.float32
    t = np.arange(SEQ)
    row = (t // GRID_W).astype(f32)
    col = (t % GRID_W).astype(f32)
    quarter = dim // 4
    freqs = (f32(ROPE_THETA) ** (-np.arange(quarter, dtype=f32) / f32(quarter))).astype(f32)
    ar, ac = row[:, None] * freqs, col[:, None] * freqs
    ang = np.concatenate([ar, ar, ac, ac], axis=-1)
    sign = np.where((np.arange(dim) // quarter) % 2 == 0, -1.0, 1.0).astype(f32)
    cos = np.concatenate([np.cos(ang), np.ones((CTX_LEN, dim), f32)], axis=0)
    sin = np.concatenate([np.sin(ang) * sign, np.zeros((CTX_LEN, dim), f32)], axis=0)
    return cos.astype(f32), sin.astype(f32)


def _rope_tables():
    f32 = np.float32
    cos_h, sin_h = _axial_tables(HEAD_DIM)
    cos_r, sin_r = _axial_tables(B_ROPE)
    cosh, sinh = np.tile(cos_h, (1, 4)), np.tile(sin_h, (1, 4))
    scale_b = f32((B_NOPE + B_ROPE) ** -0.5)
    ones, zeros = np.ones((TOK, 64), f32), np.zeros((TOK, 64), f32)
    pad1, pad0 = np.ones((TOK, 32), f32), np.zeros((TOK, 32), f32)
    cosb = np.tile(np.concatenate([ones, cos_r, pad1], axis=1) * scale_b, (1, 4))
    sinb = np.tile(np.concatenate([zeros, sin_r, pad0], axis=1) * scale_b, (1, 4))
    cosr = np.concatenate([cos_r, np.ones((TOK, 96), f32)], axis=1)
    sinr = np.concatenate([sin_r, np.zeros((TOK, 96), f32)], axis=1)
    return tuple(jnp.asarray(a, F32) for a in (cosh, sinh, cosb, sinb, cosr, sinr))


_GQA_ORDER = (0, 2, 1, 3)


def _take_heads(w, base, order, axis):
    return jnp.concatenate(
        [lax.slice_in_dim(w, base + h * HEAD_DIM, base + (h + 1) * HEAD_DIM, axis=axis) for h in order],
        axis=axis)


def _w_in_layout(w_in):
    zeros = lambda n: jnp.zeros((D_MODEL, n), w_in.dtype)
    a_q = _take_heads(w_in, 0, _GQA_ORDER, 1)
    a_kv = w_in[:, 256:512]
    b_q = jnp.concatenate(
        [jnp.concatenate([w_in[:, 512 + h * 96:512 + (h + 1) * 96], zeros(32)], axis=1)
         for h in range(B_HEADS)], axis=1)
    b_c = w_in[:, 896:1024]
    b_r = jnp.concatenate([w_in[:, 1024:1056], zeros(96)], axis=1)
    c_q = _take_heads(w_in, 1056, _GQA_ORDER, 1)
    rest = w_in[:, 1312:2336]
    return jnp.concatenate([a_q, a_kv, b_q, b_c, b_r, c_q, rest], axis=1).astype(BF16)


def _w_kv_layout(w_kv_up):
    zeros = jnp.zeros((B_KV_RANK, 64), w_kv_up.dtype)
    nope = [jnp.concatenate([w_kv_up[:, h * 128:h * 128 + 64], zeros], axis=1) for h in range(B_HEADS)]
    val = [w_kv_up[:, h * 128 + 64:(h + 1) * 128] for h in range(B_HEADS)]
    return jnp.concatenate(nope + val, axis=1).astype(BF16)


def _w_out_layout(w_out):
    return jnp.concatenate([_take_heads(w_out, 0, _GQA_ORDER, 0), w_out[256:512],
                            _take_heads(w_out, 512, _GQA_ORDER, 0), w_out[768:1024]], axis=0).astype(BF16)


def _router_rows(v):
    return v.reshape((8, 8) + v.shape[1:]).swapaxes(0, 1).reshape(v.shape)


def kernel(x, c, ctx, c_ctx, ada_w, ada_b, pre_mix_g, post_mix_g, pre_ffn_g, post_ffn_g, w_in, a_q_norm, a_k_norm, b_kv_norm, b_w_kv_up, c_sink, d_rpb, w_out, router_w, router_b, exp_w_gate, exp_w_up, exp_w_down, sh_w_gate, sh_w_up, sh_w_down):
    bsz = x.shape[0]
    depth = ada_w.shape[0]
    xall = jnp.concatenate([x, ctx], axis=1)

    cvec = jnp.zeros((16, D_MODEL), F32).at[:bsz].set(c).at[8].set(c_ctx)
    mod = _ada(cvec, ada_w, ada_b)
    tabs = _rope_tables()
    avg = jnp.asarray(np.kron(np.eye(4, dtype=np.float32), np.full((64, 64), 1.0 / 64, np.float32)), BF16)
    row2 = lambda v: v.reshape(1, -1).astype(F32)

    for l in range(depth):
        with_ctx = l < depth - 1
        mod3 = mod[l].reshape(16, 1, 6 * D_MODEL)

        proj = _inproj(
            xall, mod3, row2(pre_mix_g[l]), _w_in_layout(w_in[l]),
            row2(jnp.tile(a_q_norm[l], 4) * HEAD_DIM ** -0.5), row2(jnp.tile(a_k_norm[l], 2)),
            row2(b_kv_norm[l]), _w_kv_layout(b_w_kv_up[l]), avg, tabs)
        qa, ka, va, qb, kb, vb, qc, kc, vc, qd, kd, vd = proj

        nq = NQ + (CTX_LEN // QBLK if with_ctx else 0)
        full = lambda shape: pl.BlockSpec(shape, lambda b, i: (0,) * len(shape))
        mix_a = _attn_call(_attn_a_kernel, "attn_a", qa, ka, va, [], [], nq // 2, qblk=2 * QBLK)
        mix_b = _attn_call(_attn_b_kernel, "attn_b", qb, kb, vb, [], [], nq // 2, qblk=2 * QBLK)
        sink = jnp.zeros((8, LANES), F32).at[:4].set(
            jnp.broadcast_to(c_sink[l][np.array(_GQA_ORDER)][:, None], (4, LANES)))
        mix_c = _attn_call(_attn_c_kernel, "attn_c", qc, kc, vc, [sink], [full((8, LANES))], nq)
        bias = _na_bias(d_rpb[l])
        mix_d = _attn_call(
            _attn_d_kernel, "attn_d", qd, kd, vd, [bias],
            [pl.BlockSpec((None, D_HEADS, QBLK, NA_WIN), lambda b, i: (_na_pattern(i), 0, 0, 0))], nq)

        nb = (TOK if with_ctx else SEQ) // ROW_BLK
        rb = jnp.broadcast_to(_router_rows(router_b[l])[:, None], (N_EXPERTS, ROW_BLK)).astype(F32)
        xn, h2b, h2t, wt_t, ls_t, cnt_t = _outproj(
            (mix_a, mix_b, mix_c, mix_d), _w_out_layout(w_out[l]), xall, mod3,
            row2(post_mix_g[l]), row2(pre_ffn_g[l]),
            _router_rows(router_w[l].T).astype(BF16), rb, nb)

        t_used = nb * ROW_BLK
        n_tok = bsz * t_used
        n_steps = bsz * nb
        n_blocks = -(-(n_tok * TOP_K + n_steps * N_EXPERTS * (PIECE - 1) + N_EXPERTS * (MOE_BLK - 1))
                     // MOE_BLK)
        npieces, piece_dst, tail_dst, ntail, block_e, n_used = _moe_plan(
            cnt_t[..., 0].reshape(n_steps, N_EXPERTS), n_blocks)
        lslot = ls_t.reshape(n_steps, 1, TOP_K * ROW_BLK)
        wts = wt_t.reshape(n_steps, 1, TOP_K * ROW_BLK)
        stg_row = lslot * TILE_ROWS
        pair_row = (lslot // 2) * TILE_ROWS
        w_lo = jnp.where(lslot % 2 == 0, wts, 0.0)
        w_hi = wts - w_lo

        x_sorted = _dispatch(npieces, n_used, ntail, stg_row, piece_dst, tail_dst,
                             h2t.reshape(n_tok * TILE_ROWS, LANES), n_steps, n_blocks)
        y_sorted = _experts(block_e, n_used, x_sorted, exp_w_gate, exp_w_up, exp_w_down, l, n_blocks)
        xall = _combine(
            npieces, pair_row, w_lo, w_hi, piece_dst, y_sorted, h2b.reshape(n_tok, D_MODEL),
            xn.reshape(n_tok, D_MODEL), mod3, row2(post_ffn_g[l]),
            sh_w_gate[l].astype(BF16), sh_w_up[l].astype(BF16), sh_w_down[l].astype(BF16),
            n_steps, nb).reshape(bsz, t_used, D_MODEL)

    return xall[:, :SEQ]
```

```python
import functools

import numpy as np
import jax
import jax.numpy as jnp
from jax import lax
from jax.experimental import pallas as pl
from jax.experimental.pallas import tpu as pltpu

F32 = jnp.float32
BF16 = jnp.bfloat16
U32 = jnp.uint32

D_MODEL = 1024
SEQ = 2048
CTX_LEN = 256
TOK = SEQ + CTX_LEN
GRID_W = 64
HEAD_DIM = 64
ROPE_THETA = 10000.0
EPS = 1e-6
NEG = -1e30

A_HEADS, A_KV_HEADS = 4, 2
B_HEADS, B_NOPE, B_ROPE, B_V, B_KV_RANK = 4, 64, 32, 64, 128
C_HEADS, C_KV_HEADS, C_WINDOW = 4, 2, 128
D_HEADS, NA_ROWS, NA_COLS = 4, 8, 16

N_EXPERTS, TOP_K, N_GROUPS, TOPK_GROUPS = 64, 8, 8, 4
EXPERT_DIM, SHARED_DIM = 256, 256
ROUTED_SCALE = 2.5

LANES = 128
QBLK = 128
ROW_BLK = 256
IN_BLK = 768
MOE_BLK = 512
TILE_ROWS = D_MODEL // LANES
PIECE = 8
PIECE_ROWS = PIECE * TILE_ROWS
PIECES_MAX = 320
STG_ROWS = PIECES_MAX * PIECE_ROWS
PK_PIECE_ROWS = PIECE_ROWS // 2
PK_STG_ROWS = STG_ROWS // 2
PK_BLK_ROWS = MOE_BLK * TILE_ROWS // 2
TAIL_MAX = N_EXPERTS * MOE_BLK // PIECE
LOG_PACK_GROUP = 2
PACK_GROUP = 1 << LOG_PACK_GROUP
LOG_WAIT_GROUP = 4
WAIT_GROUP = 1 << LOG_WAIT_GROUP
LOG_CHUNKS = 2
N_CHUNKS = 1 << LOG_CHUNKS
VMEM_LIMIT = 56 * 1024 * 1024
NQ = SEQ // QBLK
NA_WIN = 10 * GRID_W

_AQ, _AK, _AV, _BQ, _BC, _BR = 0, 256, 384, 512, 1024, 1152
_CQ, _CK, _CV, _DQ, _DK, _DV = 1280, 1536, 1664, 1792, 2048, 2304
IN_W = 2560

_NT = (((1,), (1,)), ((), ()))


def _sigmoid(x):
    return 1.0 / (1.0 + jnp.exp(-x))


def _rms(x, g):
    return x * lax.rsqrt(jnp.mean(x * x, axis=-1, keepdims=True) + EPS) * g


def _dot(a, b):
    return jnp.dot(a, b, preferred_element_type=F32)


def _dot_nt(a, b):
    return lax.dot_general(a, b, _NT, preferred_element_type=F32)


def _ada_kernel(c_ref, w_ref, b_ref, o_ref):
    c = c_ref[...]
    a = (c * _sigmoid(c)).astype(BF16)
    o_ref[...] = _dot(a, w_ref[...].astype(BF16)) + b_ref[...]


def _ada(cvec, ada_w, ada_b):
    depth = ada_w.shape[0]
    n = ada_w.shape[2]
    tn = 1536
    return pl.pallas_call(
        _ada_kernel,
        grid=(depth, n // tn),
        in_specs=[
            pl.BlockSpec((16, D_MODEL), lambda l, j: (0, 0)),
            pl.BlockSpec((None, D_MODEL, tn), lambda l, j: (l, 0, j)),
            pl.BlockSpec((None, 1, tn), lambda l, j: (l, 0, j)),
        ],
        out_specs=pl.BlockSpec((None, 16, tn), lambda l, j: (l, 0, j)),
        out_shape=jax.ShapeDtypeStruct((depth, 16, n), F32),
        name="ada",
    )(cvec, ada_w, ada_b.reshape(depth, 1, n))


def _rope(x, cos, sin, chunk):
    outs = []
    for c in range(x.shape[1] // LANES):
        sl = slice(c * LANES, (c + 1) * LANES)
        xs = x[:, sl]
        lane = lax.broadcasted_iota(jnp.int32, xs.shape, 1)
        first = (lane & chunk) == 0
        rot = jnp.where(first, pltpu.roll(xs, LANES - chunk, 1), pltpu.roll(xs, chunk, 1))
        outs.append(xs * cos[:, sl] + rot * sin[:, sl])
    return outs[0] if len(outs) == 1 else jnp.concatenate(outs, axis=1)


def _head_norm(x, avg):
    sq = x * x
    hi = sq.astype(BF16)
    lo = (sq - hi.astype(F32)).astype(BF16)
    ms = _dot(hi, avg) + _dot(lo, avg)
    return x * lax.rsqrt(ms + EPS)


def _inproj_kernel(x_ref, mod_ref, modc_ref, g_ref, w_ref, qg_ref, kg_ref, kvg_ref, wkv_ref, avg_ref,
                   cosh_ref, sinh_ref, cosb_ref, sinb_ref, cosr_ref, sinr_ref,
                   qa_ref, ka_ref, va_ref, qb_ref, kb_ref, vb_ref,
                   qc_ref, kc_ref, vc_ref, qd_ref, kd_ref, vd_ref):
    d = D_MODEL
    n = x_ref.shape[0]
    is_ctx = pl.program_id(1) * n + lax.broadcasted_iota(jnp.int32, (n, 1), 0) >= SEQ
    scale = jnp.where(is_ctx, modc_ref[:, d:2 * d], mod_ref[:, d:2 * d])
    shift = jnp.where(is_ctx, modc_ref[:, 0:d], mod_ref[:, 0:d])
    h = _rms(x_ref[...], g_ref[...]) * (1.0 + scale) + shift
    p = _dot(h.astype(BF16), w_ref[...])
    cosh, sinh = cosh_ref[...], sinh_ref[...]
    avg = avg_ref[...]

    qa = _head_norm(p[:, _AQ:_AQ + 256], avg) * qg_ref[...]
    qa_ref[...] = _rope(qa, cosh, sinh, 16).astype(BF16)
    ka = _head_norm(p[:, _AK:_AK + 128], avg[0:128, 0:128]) * kg_ref[...]
    ka_ref[...] = _rope(ka, cosh[:, 0:128], sinh[:, 0:128], 16).astype(BF16)
    ones = jnp.ones((p.shape[0], LANES), BF16)
    va_ref[...] = jnp.concatenate([p[:, _AV:_AV + 128].astype(BF16), ones], axis=1)

    qb_ref[...] = _rope(p[:, _BQ:_BQ + 512], cosb_ref[...], sinb_ref[...], 8).astype(BF16)
    cn = _rms(p[:, _BC:_BC + 128], kvg_ref[...])
    kv = _dot(cn.astype(BF16), wkv_ref[...])
    kr = _rope(p[:, _BR:_BR + 128], cosr_ref[...], sinr_ref[...], 8)
    kr = pltpu.roll(kr, 64, 1)
    kb_ref[...] = jnp.concatenate(
        [kv[:, hh * 128:(hh + 1) * 128] + kr for hh in range(B_HEADS)], axis=1).astype(BF16)
    vb_ref[...] = jnp.concatenate(
        [kv[:, 512:640].astype(BF16), ones, kv[:, 640:768].astype(BF16), ones], axis=1)

    qc_ref[...] = _rope(p[:, _CQ:_CQ + 256] * 0.125, cosh, sinh, 16).astype(BF16)
    kc_ref[...] = _rope(p[:, _CK:_CK + 128], cosh[:, 0:128], sinh[:, 0:128], 16).astype(BF16)
    vc_ref[...] = jnp.concatenate([p[:, _CV:_CV + 128].astype(BF16), ones], axis=1)

    qd_ref[...] = (p[:, _DQ:_DQ + 256] * 0.125).astype(BF16)
    kd_ref[...] = p[:, _DK:_DK + 256].astype(BF16)
    vd_ref[...] = jnp.concatenate(
        [p[:, _DV:_DV + 128].astype(BF16), ones, p[:, _DV + 128:_DV + 256].astype(BF16), ones], axis=1)


def _mod_row(j, b, n_lat_blocks):
    return jnp.where(j < n_lat_blocks, b, 8)


def _inproj(xall, mod3, g, w, qg, kg, kvg, wkv, avg, tabs):
    bsz = xall.shape[0]
    nb = TOK // IN_BLK
    full = lambda shape: pl.BlockSpec(shape, lambda b, j: (0,) * len(shape))
    row = lambda w_: pl.BlockSpec((IN_BLK, w_), lambda b, j: (j, 0))
    out = lambda w_: pl.BlockSpec((None, IN_BLK, w_), lambda b, j: (b, j, 0))
    widths = (256, 128, 256, 512, 512, 512, 256, 128, 256, 256, 256, 512)
    return pl.pallas_call(
        _inproj_kernel,
        grid=(bsz, nb),
        in_specs=[
            pl.BlockSpec((None, IN_BLK, D_MODEL), lambda b, j: (b, j, 0)),
            pl.BlockSpec((None, 1, 6 * D_MODEL), lambda b, j: (b, 0, 0)),
            pl.BlockSpec((None, 1, 6 * D_MODEL), lambda b, j: (8, 0, 0)),
            full((1, D_MODEL)), full((D_MODEL, IN_W)),
            full((1, 256)), full((1, 128)), full((1, 128)), full((128, 768)), full((256, 256)),
            row(256), row(256), row(512), row(512), row(128), row(128),
        ],
        out_specs=[out(w_) for w_ in widths],
        out_shape=[jax.ShapeDtypeStruct((bsz, TOK, w_), BF16) for w_ in widths],
        compiler_params=pltpu.CompilerParams(vmem_limit_bytes=VMEM_LIMIT),
        name="inproj",
    )(xall, mod3, mod3, g, w, qg, kg, kvg, wkv, avg, *tabs)


def _softmax_pv(segs, sink=None):
    m = None
    for s, _ in segs:
        ms = jnp.max(s, axis=-1, keepdims=True)
        m = ms if m is None else jnp.maximum(m, ms)
    if sink is not None:
        m = jnp.maximum(m, sink)
    acc = None
    for s, v1 in segs:
        r = _dot(jnp.exp((s - m).astype(BF16)), v1)
        acc = r if acc is None else acc + r
    l = acc[:, LANES:LANES + 1]
    if sink is not None:
        l = l + jnp.exp(sink - m)
    return acc[:, 0:LANES] / l


def _lane_lo(dtype_shape):
    return lax.broadcasted_iota(jnp.int32, dtype_shape, 1) < HEAD_DIM


def _half(qt, hh):
    lo = _lane_lo(qt.shape)
    return jnp.where(lo if hh == 0 else jnp.logical_not(lo), qt, jnp.zeros_like(qt))


def _merge_halves(o0, o1):
    return jnp.where(_lane_lo(o0.shape), o0, o1)


def _stack_heads(q_ref, tiles):
    return jnp.concatenate(
        [_half(q_ref[:, t * LANES:(t + 1) * LANES], hh) for t in tiles for hh in range(2)], axis=0)


def _store_heads(o_ref, o, tiles):
    n = o_ref.shape[0]
    for ti, t in enumerate(tiles):
        o0, o1 = o[2 * ti * n:(2 * ti + 1) * n], o[(2 * ti + 1) * n:(2 * ti + 2) * n]
        o_ref[:, t * LANES:(t + 1) * LANES] = _merge_halves(o0, o1).astype(o_ref.dtype)


def _attn_a_kernel(q_ref, k_ref, v_ref, o_ref, *, n_lat):
    def run(k, v1):
        for t in range(2):
            o = [_softmax_pv([(_dot_nt(_half(q_ref[:, t * LANES:(t + 1) * LANES], hh), k), v1)])
                 for hh in range(2)]
            o_ref[:, t * LANES:(t + 1) * LANES] = _merge_halves(o[0], o[1]).astype(o_ref.dtype)

    @pl.when(pl.program_id(1) < n_lat)
    def _():
        run(k_ref[...], v_ref[...])

    @pl.when(pl.program_id(1) >= n_lat)
    def _():
        run(k_ref[SEQ:TOK, :], v_ref[SEQ:TOK, :])


def _attn_b_kernel(q_ref, k_ref, v_ref, o_ref, *, n_lat):
    def run(lo_, hi_):
        for t in range(2):
            v1 = v_ref[lo_:hi_, t * 256:(t + 1) * 256]
            o = []
            for hh in range(2):
                hd = 2 * t + hh
                q = q_ref[:, hd * LANES:(hd + 1) * LANES]
                k = k_ref[lo_:hi_, hd * LANES:(hd + 1) * LANES]
                o.append(_softmax_pv([(_dot_nt(q, k), v1)]))
            o_ref[:, t * LANES:(t + 1) * LANES] = _merge_halves(o[0], o[1]).astype(o_ref.dtype)

    @pl.when(pl.program_id(1) < n_lat)
    def _():
        run(0, TOK)

    @pl.when(pl.program_id(1) >= n_lat)
    def _():
        run(SEQ, TOK)


def _attn_c_kernel(q_ref, k_ref, v_ref, sink_ref, o_ref, *, n_lat):
    i = pl.program_id(1)
    qblk = o_ref.shape[0]
    band = 3 * qblk

    def run(latent):
        if latent:
            start = pl.multiple_of(jnp.clip((i - 1) * qblk, 0, SEQ - band), qblk)
            shape = (2 * qblk, band)
            qpos = i * qblk + (lax.broadcasted_iota(jnp.int32, shape, 0) & (qblk - 1))
            kpos = start + lax.broadcasted_iota(jnp.int32, shape, 1)
            valid = jnp.abs(qpos - kpos) <= C_WINDOW
        for t in range(2):
            q = _stack_heads(q_ref, (t,))
            segs = [(_dot_nt(q, k_ref[SEQ:TOK, :]), v_ref[SEQ:TOK, :])]
            if latent:
                s = jnp.where(valid, _dot_nt(q, k_ref[pl.ds(start, band), :]), NEG)
                segs.insert(0, (s, v_ref[pl.ds(start, band), :]))
            sink = jnp.concatenate(
                [jnp.broadcast_to(sink_ref[j:j + 1, 0:1], (qblk, 1)) for j in (2 * t, 2 * t + 1)], axis=0)
            _store_heads(o_ref, _softmax_pv(segs, sink=sink), (t,))

    @pl.when(i < n_lat)
    def _():
        run(True)

    @pl.when(i >= n_lat)
    def _():
        run(False)


def _attn_d_kernel(q_ref, k_ref, v_ref, bias_ref, o_ref, *, n_lat):
    i = pl.program_id(1)
    qblk = o_ref.shape[0]

    def run(latent):
        if latent:
            start = pl.multiple_of(jnp.clip((i - 2) * qblk, 0, SEQ - NA_WIN), qblk)
        for t in range(2):
            ksl, vsl = slice(t * LANES, (t + 1) * LANES), slice(t * 256, (t + 1) * 256)
            q = _stack_heads(q_ref, (t,))
            segs = [(_dot_nt(q, k_ref[SEQ:TOK, ksl]), v_ref[SEQ:TOK, vsl])]
            if latent:
                bias = bias_ref[2 * t:2 * t + 2].reshape(2 * qblk, NA_WIN)
                segs.insert(0, (_dot_nt(q, k_ref[pl.ds(start, NA_WIN), ksl]) + bias,
                                v_ref[pl.ds(start, NA_WIN), vsl]))
            _store_heads(o_ref, _softmax_pv(segs), (t,))

    @pl.when(i < n_lat)
    def _():
        run(True)

    @pl.when(i >= n_lat)
    def _():
        run(False)


def _attn_call(kernel, name, q, k, v, extra, extra_specs, n_blocks, qblk=QBLK):
    bsz = q.shape[0]
    return pl.pallas_call(
        functools.partial(kernel, n_lat=SEQ // qblk),
        grid=(bsz, n_blocks),
        in_specs=[
            pl.BlockSpec((None, qblk, q.shape[2]), lambda b, i: (b, i, 0)),
            pl.BlockSpec((None, TOK, k.shape[2]), lambda b, i: (b, 0, 0)),
            pl.BlockSpec((None, TOK, v.shape[2]), lambda b, i: (b, 0, 0)),
        ] + extra_specs,
        out_specs=pl.BlockSpec((None, qblk, 256), lambda b, i: (b, i, 0)),
        out_shape=jax.ShapeDtypeStruct((bsz, n_blocks * qblk, 256), BF16),
        name=name,
    )(q, k, v, *extra)


def _na_pattern(i):
    return jnp.where(i < 2, i, jnp.where(i < NQ - 2, 2, jnp.minimum(i, NQ - 1) - (NQ - 5)))


def _na_bias(rpb):
    rows = SEQ // GRID_W
    n_dr, n_dc = 2 * NA_ROWS - 1, 2 * NA_COLS - 1
    qc = np.arange(GRID_W)
    dc = np.clip(qc[None, :] - qc[:, None] + NA_COLS - 1, 0, n_dc - 1)
    sel_c = jnp.asarray(np.eye(n_dc, dtype=np.float32)[dc])
    cs = np.clip(qc - NA_COLS // 2, 0, GRID_W - NA_COLS)[:, None]
    col_ok = (qc[None, :] >= cs) & (qc[None, :] < cs + NA_COLS)
    sel_r, row_ok = [], []
    for i in (0, 1, 2, NQ - 2, NQ - 1):
        start_row = min(max(2 * i - 4, 0), rows - 10)
        qr = 2 * i + np.arange(2)
        kr = start_row + np.arange(10)
        rs = np.clip(qr - NA_ROWS // 2, 0, rows - NA_ROWS)[:, None]
        row_ok.append((kr[None] >= rs) & (kr[None] < rs + NA_ROWS))
        dr = np.clip(kr[None] - qr[:, None] + NA_ROWS - 1, 0, n_dr - 1)
        sel_r.append(np.eye(n_dr, dtype=np.float32)[dr])
    b = jnp.einsum("hrc,pabr,qkc->phaqbk", rpb.astype(F32), jnp.asarray(np.stack(sel_r)), sel_c,
                   precision=lax.Precision.HIGHEST)
    valid = np.stack(row_ok)[:, None, :, None, :, None] & col_ok[None, None, None, :, None, :]
    return jnp.where(valid, b, NEG).reshape(5, D_HEADS, QBLK, NA_WIN)


def _outproj_kernel(ma_ref, mb_ref, mc_ref, md_ref, w_ref, x_ref, mod_ref, g1_ref, g2_ref,
                    wr_ref, rb_ref, xn_ref, h2b_ref, h2t_ref, wt_ref, ls_ref, cnt_ref):
    d = D_MODEL
    mix = jnp.concatenate([ma_ref[...], mb_ref[...], mc_ref[...], md_ref[...]], axis=1)
    y = _rms(_dot(mix, w_ref[...]), g1_ref[...])
    xn = x_ref[...] + mod_ref[:, 2 * d:3 * d] * y
    xn_ref[...] = xn
    h2 = _rms(xn, g2_ref[...]) * (1.0 + mod_ref[:, 4 * d:5 * d]) + mod_ref[:, 3 * d:4 * d]
    n = h2.shape[0]
    h2b_ref[...] = h2.astype(BF16)
    for s in range(TILE_ROWS):
        h2t_ref[pl.ds(s, n, stride=TILE_ROWS), :] = h2[:, s * LANES:(s + 1) * LANES]

    s = _sigmoid(_dot_nt(wr_ref[...], h2.astype(BF16)))
    sb = s + rb_ref[...]
    per = N_EXPERTS // N_GROUPS
    slab = [sb[j * 8:(j + 1) * 8, :] for j in range(per)]
    sraw = [s[j * 8:(j + 1) * 8, :] for j in range(per)]
    giota = lax.broadcasted_iota(jnp.int32, (N_GROUPS, n), 0)
    eid = [giota * per + j for j in range(per)]
    ninf = -jnp.inf

    m1 = functools.reduce(jnp.maximum, slab)
    j1 = functools.reduce(jnp.minimum, [jnp.where(slab[j] == m1, j, per) for j in range(per)])
    m2 = functools.reduce(jnp.maximum, [jnp.where(j1 == j, ninf, slab[j]) for j in range(per)])
    gs = m1 + m2

    gsel = jnp.zeros((N_GROUPS, n), jnp.bool_)
    for _ in range(TOPK_GROUPS):
        m = jnp.max(gs, axis=0, keepdims=True)
        gi = jnp.min(jnp.where(gs == m, giota, N_GROUPS), axis=0, keepdims=True)
        hit = giota == gi
        gsel = jnp.logical_or(gsel, hit)
        gs = jnp.where(hit, ninf, gs)
    slab = [jnp.where(gsel, sj, ninf) for sj in slab]

    def pick(hits, vals):
        return jnp.sum(functools.reduce(jnp.add, [jnp.where(hits[j], vals[j], 0.0) for j in range(per)]),
                       axis=0, keepdims=True)

    kiota = lax.broadcasted_iota(jnp.int32, (TOP_K, n), 0)
    wts = jnp.zeros((TOP_K, n), F32)
    chosen = []
    for k in range(TOP_K):
        m = jnp.max(functools.reduce(jnp.maximum, slab), axis=0, keepdims=True)
        cand = functools.reduce(jnp.minimum,
                                [jnp.where(slab[j] == m, eid[j], N_EXPERTS) for j in range(per)])
        ei = jnp.min(cand, axis=0, keepdims=True)
        hits = [eid[j] == ei for j in range(per)]
        slab = [jnp.where(hits[j], ninf, slab[j]) for j in range(per)]
        wts = jnp.where(kiota == k, pick(hits, sraw), wts)
        chosen.append(ei)
    wt_ref[...] = wts / jnp.sum(wts, axis=0, keepdims=True) * ROUTED_SCALE

    hits_k = [[eid[j] == chosen[k] for j in range(per)] for k in range(TOP_K)]
    sel = jnp.concatenate(
        [functools.reduce(jnp.add, [jnp.where(hits_k[k][j], 1.0, 0.0) for k in range(TOP_K)])
         for j in range(per)], axis=0)
    tri = jnp.where(lax.broadcasted_iota(jnp.int32, (n, n), 0) < lax.broadcasted_iota(jnp.int32, (n, n), 1),
                    1.0, 0.0).astype(BF16)
    before = _dot(sel.astype(BF16), tri)
    cnt = jnp.sum(sel, axis=1, keepdims=True)
    run = jnp.floor((cnt + (PIECE - 1.0)) * (1.0 / PIECE)) * PIECE
    low = jnp.where(lax.broadcasted_iota(jnp.int32, (N_EXPERTS, N_EXPERTS), 1)
                    < lax.broadcasted_iota(jnp.int32, (N_EXPERTS, N_EXPERTS), 0), 1.0, 0.0).astype(BF16)
    start = _dot(low, jnp.broadcast_to(run, (N_EXPERTS, LANES)).astype(BF16))[:, 0:1]
    base = before + start
    base = [base[j * 8:(j + 1) * 8, :] for j in range(per)]
    lsl = jnp.zeros((TOP_K, n), F32)
    for k in range(TOP_K):
        lsl = jnp.where(kiota == k, pick(hits_k[k], base), lsl)
    ls_ref[...] = lsl.astype(jnp.int32)
    cnt_ref[...] = jnp.broadcast_to(cnt, (N_EXPERTS, LANES)).astype(jnp.int32)


def _outproj(mixes, w, xall, mod3, g1, g2, wr, rb, nb):
    bsz = xall.shape[0]
    n_lat = SEQ // ROW_BLK
    full = lambda shape: pl.BlockSpec(shape, lambda b, j: (0,) * len(shape))
    rows = lambda w_: pl.BlockSpec((None, ROW_BLK, w_), lambda b, j: (b, j, 0))
    route = pl.BlockSpec((None, None, TOP_K, ROW_BLK), lambda b, j: (b, j, 0, 0))
    return pl.pallas_call(
        _outproj_kernel,
        grid=(bsz, nb),
        in_specs=[rows(256)] * 4 + [
            full((D_MODEL, D_MODEL)), rows(D_MODEL),
            pl.BlockSpec((None, 1, 6 * D_MODEL), lambda b, j: (_mod_row(j, b, n_lat), 0, 0)),
            full((1, D_MODEL)), full((1, D_MODEL)),
            full((N_EXPERTS, D_MODEL)), full((N_EXPERTS, ROW_BLK)),
        ],
        out_specs=[
            rows(D_MODEL), rows(D_MODEL),
            pl.BlockSpec((None, ROW_BLK * TILE_ROWS, LANES), lambda b, j: (b, j, 0)),
            route, route,
            pl.BlockSpec((None, None, N_EXPERTS, LANES), lambda b, j: (b, j, 0, 0)),
        ],
        out_shape=[
            jax.ShapeDtypeStruct((bsz, nb * ROW_BLK, D_MODEL), F32),
            jax.ShapeDtypeStruct((bsz, nb * ROW_BLK, D_MODEL), BF16),
            jax.ShapeDtypeStruct((bsz, nb * ROW_BLK * TILE_ROWS, LANES), F32),
            jax.ShapeDtypeStruct((bsz, nb, TOP_K, ROW_BLK), F32),
            jax.ShapeDtypeStruct((bsz, nb, TOP_K, ROW_BLK), jnp.int32),
            jax.ShapeDtypeStruct((bsz, nb, N_EXPERTS, LANES), jnp.int32),
        ],
        name="outproj",
    )(*mixes, w, xall, mod3, g1, g2, wr, rb)


def _rows(i, n):
    return pl.ds(pl.multiple_of(i * n, n), n)


def _tile_at(row):
    return pl.ds(pl.multiple_of(row, TILE_ROWS), TILE_ROWS)


def _piece(p):
    return _rows(p, PK_PIECE_ROWS)


def _pack_pairs(lo, hi):
    return (lax.shift_right_logical(pltpu.bitcast(lo, U32), jnp.uint32(16))
            | (pltpu.bitcast(hi, U32) & jnp.uint32(0xFFFF0000)))


def _unpack_lo(u):
    return pltpu.bitcast(lax.shift_left(u, jnp.uint32(16)), F32)


def _unpack_hi(u):
    return pltpu.bitcast(u & jnp.uint32(0xFFFF0000), F32)


def _start_pieces(lo, hi, start):
    n = hi - lo

    def quad(q, carry):
        for r in range(4):
            start(lo + 4 * q + r, r % 2)
        return carry
    lax.fori_loop(0, lax.shift_right_logical(n, 2), quad, 0)

    def rest(r, carry):
        start(hi - 1 - r, 0)
        return carry
    lax.fori_loop(0, n & 3, rest, 0)


def _wait_pieces(n, piece_rows, copy_of_rows):
    def group(p, carry):
        copy_of_rows(WAIT_GROUP * piece_rows).wait()
        return carry
    lax.fori_loop(0, lax.shift_right_logical(n, LOG_WAIT_GROUP), group, 0)

    def single(p, carry):
        copy_of_rows(piece_rows).wait()
        return carry
    lax.fori_loop(0, n & (WAIT_GROUP - 1), single, 0)


def _interleave(body, n_pieces, start):
    per = ROW_BLK // N_CHUNKS

    def chunk(c, carry):
        for tt in range(per):
            body(c * per + tt)
        _start_pieces(lax.shift_right_logical(c * n_pieces, LOG_CHUNKS),
                      lax.shift_right_logical((c + 1) * n_pieces, LOG_CHUNKS), start)
        return carry
    lax.fori_loop(0, N_CHUNKS, chunk, 0)


def _dispatch_kernel(np_ref, nu_ref, nt_ref, ls_ref, pd_ref, pdp_ref, td_ref, h_ref, xs_hbm,
                     stg, pk, zblk, sem, zsem, *, n_steps, n_blocks):
    i = pl.program_id(0)
    slot = i % 3
    prev = (i + 2) % 3

    def piece_copy(s, p, d):
        return pltpu.make_async_copy(pk.at[s, _piece(p), :], xs_hbm.at[_piece(d), :], sem.at[s])

    def wait_pieces(s, n):
        _wait_pieces(n, PK_PIECE_ROWS, lambda rows: pltpu.make_async_copy(
            pk.at[s, pl.ds(0, rows), :], xs_hbm.at[pl.ds(0, rows), :], sem.at[s]))

    def zero_piece_copy(d):
        return pltpu.make_async_copy(zblk.at[pl.ds(0, PK_PIECE_ROWS), :], xs_hbm.at[_piece(d), :], zsem.at[0])

    def zero_block_copy(b):
        return pltpu.make_async_copy(zblk, xs_hbm.at[_rows(b, PK_BLK_ROWS), :], zsem.at[1])

    @pl.when(i == 0)
    def _():
        zblk[...] = jnp.zeros_like(zblk)

        def zero(c, carry):
            stg[_rows(c, 2048), :] = jnp.zeros((2048, LANES), F32)
            return carry
        lax.fori_loop(0, STG_ROWS // 2048, zero, 0)

        def blk(b, carry):
            zero_block_copy(b).start()
            return carry
        lax.fori_loop(nu_ref[0], n_blocks, blk, 0)

    @pl.when(i >= 3)
    def _():
        wait_pieces(slot, np_ref[jnp.maximum(i - 3, 0)])

    def scatter_token(t):
        tile = h_ref[_rows(t, TILE_ROWS), :].astype(BF16).astype(F32)
        for k in range(TOP_K):
            stg[_tile_at(ls_ref[0, k * ROW_BLK + t]), :] = tile

    np_prev = jnp.where(i >= 1, np_ref[jnp.maximum(i - 1, 0)], 0)
    _interleave(scatter_token, np_prev,
                lambda p, prio: piece_copy(prev, p, pdp_ref[0, p]).start(priority=prio))

    def pack_group(q, carry):
        v = stg[_rows(q, PACK_GROUP * PIECE_ROWS), :].reshape(PACK_GROUP * PIECE // 2, 2 * TILE_ROWS, LANES)
        packed = _pack_pairs(v[:, 0:TILE_ROWS, :], v[:, TILE_ROWS:2 * TILE_ROWS, :])
        pk[slot, _rows(q, PACK_GROUP * PK_PIECE_ROWS), :] = packed.reshape(PACK_GROUP * PK_PIECE_ROWS, LANES)
        return carry
    lax.fori_loop(0, lax.shift_right_logical(np_ref[i] + (PACK_GROUP - 1), LOG_PACK_GROUP), pack_group, 0)

    @pl.when(i == n_steps - 1)
    def _():
        _start_pieces(0, np_ref[i], lambda p, prio: piece_copy(slot, p, pd_ref[0, p]).start(priority=prio))

        def tail(p, carry):
            zero_piece_copy(td_ref[0, p]).start()
            return carry
        lax.fori_loop(0, nt_ref[0], tail, 0)

        wait_pieces((i + 1) % 3, np_ref[jnp.maximum(i - 2, 0)])
        wait_pieces(prev, np_ref[jnp.maximum(i - 1, 0)])
        wait_pieces(slot, np_ref[i])

        def tail_wait(p, carry):
            zero_piece_copy(0).wait()
            return carry
        lax.fori_loop(0, nt_ref[0], tail_wait, 0)

        def blk_wait(b, carry):
            zero_block_copy(0).wait()
            return carry
        lax.fori_loop(nu_ref[0], n_blocks, blk_wait, 0)


def _dispatch(npieces, n_used, ntail, lslot, piece_dst, tail_dst, h2t, n_steps, n_blocks):
    assert n_steps >= 3
    smem = lambda n, imap: pl.BlockSpec((None, 1, n), imap, memory_space=pltpu.SMEM)
    grid_spec = pltpu.PrefetchScalarGridSpec(
        num_scalar_prefetch=3,
        grid=(n_steps,),
        in_specs=[
            smem(TOP_K * ROW_BLK, lambda i, *_: (i, 0, 0)),
            smem(PIECES_MAX, lambda i, *_: (i, 0, 0)),
            smem(PIECES_MAX, lambda i, *_: (jnp.maximum(i - 1, 0), 0, 0)),
            smem(TAIL_MAX, lambda i, *_: (0, 0, 0)),
            pl.BlockSpec((ROW_BLK * TILE_ROWS, LANES), lambda i, *_: (i, 0)),
        ],
        out_specs=pl.BlockSpec(memory_space=pl.ANY),
        scratch_shapes=[
            pltpu.VMEM((STG_ROWS, LANES), F32),
            pltpu.VMEM((3, PK_STG_ROWS, LANES), U32),
            pltpu.VMEM((PK_BLK_ROWS, LANES), U32),
            pltpu.SemaphoreType.DMA((3,)),
            pltpu.SemaphoreType.DMA((2,)),
        ],
    )
    return pl.pallas_call(
        functools.partial(_dispatch_kernel, n_steps=n_steps, n_blocks=n_blocks),
        grid_spec=grid_spec,
        out_shape=jax.ShapeDtypeStruct((n_blocks * PK_BLK_ROWS, LANES), U32),
        compiler_params=pltpu.CompilerParams(dimension_semantics=("arbitrary",),
                                             vmem_limit_bytes=VMEM_LIMIT),
        name="dispatch",
    )(npieces, n_used, ntail, lslot, piece_dst, piece_dst, tail_dst, h2t)


def _from_tiles(ref, n):
    return jnp.concatenate([ref[pl.ds(s, n, stride=TILE_ROWS), :] for s in range(TILE_ROWS)], axis=1)


def _experts_kernel(be_ref, nu_ref, xs_hbm, wg_ref, wu_ref, wd_ref, ys_hbm,
                    xbuf, ybuf, zbuf, wgb, wub, wdb, xsem, ysem, zsem, *, n_blocks):
    i = pl.program_id(0)
    n_used = nu_ref[0]
    rows = PK_BLK_ROWS
    half = rows // 2
    pairs = MOE_BLK // 2

    def halves(b):
        return [pl.ds(pl.multiple_of(b * rows + h * half, half), half) for h in range(2)]

    def x_copies(b, s):
        return [pltpu.make_async_copy(xs_hbm.at[src, :], xbuf.at[s, pl.ds(h * half, half), :], xsem.at[s])
                for h, src in enumerate(halves(b))]

    def y_copies(b, s):
        return [pltpu.make_async_copy(ybuf.at[s, pl.ds(h * half, half), :], ys_hbm.at[dst, :], ysem.at[s])
                for h, dst in enumerate(halves(b))]

    def start(copies):
        for h, c in enumerate(copies):
            c.start(priority=h)

    def wait(copies):
        for c in copies:
            c.wait()

    def zero_fill(b):
        return pltpu.make_async_copy(zbuf, ys_hbm.at[_rows(b, rows), :], zsem.at[0])

    @pl.when(i == 0)
    def _():
        start(x_copies(0, 0))

        @pl.when(n_used > 1)
        def _():
            start(x_copies(1, 1))

        zbuf[...] = jnp.zeros_like(zbuf)

        def fill(b, carry):
            zero_fill(b).start()
            return carry
        lax.fori_loop(n_used, n_blocks, fill, 0)

    @pl.when(i + 2 < n_used)
    def _():
        start(x_copies(i + 2, (i + 2) % 3))

    @pl.when(i < n_used)
    def _():
        wait(x_copies(i, i % 3))

        @pl.when(i >= 2)
        def _():
            wait(y_copies(i - 2, i % 2))

        @pl.when(jnp.logical_or(i == 0, be_ref[i] != be_ref[jnp.maximum(i - 1, 0)]))
        def _():
            wgb[...] = wg_ref[...].astype(BF16)
            wub[...] = wu_ref[...].astype(BF16)
            wdb[...] = wd_ref[...].astype(BF16)

        x = pltpu.bitcast(_from_tiles(xbuf.at[i % 3], pairs), BF16)
        g = _dot(x, wgb[...])
        u = _dot(x, wub[...])
        y = _dot((g * _sigmoid(g) * u).astype(BF16), wdb[...])
        yp = pltpu.bitcast(y.astype(BF16), U32)
        yb = ybuf.at[i % 2]
        for s in range(TILE_ROWS):
            yb[pl.ds(s, pairs, stride=TILE_ROWS), :] = yp[:, s * LANES:(s + 1) * LANES]
        start(y_copies(i, i % 2))

        @pl.when(i == n_used - 1)
        def _():
            @pl.when(i >= 1)
            def _():
                wait(y_copies(i - 1, (i + 1) % 2))
            wait(y_copies(i, i % 2))

            def fill_wait(b, carry):
                zero_fill(0).wait()
                return carry
            lax.fori_loop(n_used, n_blocks, fill_wait, 0)


def _experts(block_e, n_used, x_sorted, wg, wu, wd, layer, n_blocks):
    wspec = lambda shape: pl.BlockSpec((None, None) + shape, lambda i, be, nu: (layer, be[i], 0, 0))
    rows = PK_BLK_ROWS
    grid_spec = pltpu.PrefetchScalarGridSpec(
        num_scalar_prefetch=2,
        grid=(n_blocks,),
        in_specs=[
            pl.BlockSpec(memory_space=pl.ANY),
            wspec((D_MODEL, EXPERT_DIM)), wspec((D_MODEL, EXPERT_DIM)), wspec((EXPERT_DIM, D_MODEL)),
        ],
        out_specs=pl.BlockSpec(memory_space=pl.ANY),
        scratch_shapes=[
            pltpu.VMEM((3, rows, LANES), U32),
            pltpu.VMEM((2, rows, LANES), U32),
            pltpu.VMEM((rows, LANES), U32),
            pltpu.VMEM((D_MODEL, EXPERT_DIM), BF16),
            pltpu.VMEM((D_MODEL, EXPERT_DIM), BF16),
            pltpu.VMEM((EXPERT_DIM, D_MODEL), BF16),
            pltpu.SemaphoreType.DMA((3,)),
            pltpu.SemaphoreType.DMA((2,)),
            pltpu.SemaphoreType.DMA((1,)),
        ],
    )
    return pl.pallas_call(
        functools.partial(_experts_kernel, n_blocks=n_blocks),
        grid_spec=grid_spec,
        out_shape=jax.ShapeDtypeStruct((n_blocks * rows, LANES), U32),
        compiler_params=pltpu.CompilerParams(dimension_semantics=("arbitrary",)),
        name="experts",
    )(block_e, n_used, x_sorted, wg, wu, wd)


def _combine_kernel(np_ref, lp_ref, ws_ref, cur_ref, nx1_ref, nx2_ref, y_hbm, h2b_ref, xn_ref,
                    mod_ref, g_ref, swg_ref, swu_ref, swd_ref, o_ref, stg, acc, sem, *, n_steps):
    i = pl.program_id(0)
    slot = i % 3
    d = D_MODEL

    def piece_copy(s, p, src):
        return pltpu.make_async_copy(y_hbm.at[_piece(src), :], stg.at[s, _piece(p), :], sem.at[s])

    def fetcher(s, pd_ref):
        return lambda p, prio: piece_copy(s, p, pd_ref[0, p]).start(priority=prio)

    @pl.when(i == 0)
    def _():
        _start_pieces(0, np_ref[0], fetcher(0, cur_ref))
        _start_pieces(0, np_ref[1], fetcher(1, nx1_ref))

    _wait_pieces(np_ref[i], PK_PIECE_ROWS, lambda rows: pltpu.make_async_copy(
        y_hbm.at[pl.ds(0, rows), :], stg.at[slot, pl.ds(0, rows), :], sem.at[slot]))

    def gather_token(t):
        a = None
        for k in range(TOP_K):
            j = k * ROW_BLK + t
            pair = stg[slot, _tile_at(lp_ref[0, j]), :]
            w = jnp.full((TILE_ROWS, LANES), ws_ref[0, j], F32)
            v = jnp.where(w < 0.0, _unpack_hi(pair), _unpack_lo(pair)) * jnp.abs(w)
            a = v if a is None else a + v
        acc[_rows(t, TILE_ROWS), :] = a

    np_next = jnp.where(i + 2 < n_steps, np_ref[jnp.minimum(i + 2, n_steps - 1)], 0)
    _interleave(gather_token, np_next, fetcher((i + 2) % 3, nx2_ref))

    hb = h2b_ref[...]
    g = _dot(hb, swg_ref[...])
    u = _dot(hb, swu_ref[...])
    f = _dot((g * _sigmoid(g) * u).astype(BF16), swd_ref[...]) + _from_tiles(acc, ROW_BLK)
    o_ref[...] = xn_ref[...] + mod_ref[:, 5 * d:6 * d] * _rms(f, g_ref[...])


def _combine(npieces, pair_row, w_signed, piece_src, y_sorted, h2b_rows, xn_rows, mod3, g, swg, swu, swd,
             n_steps, per_batch):
    assert n_steps >= 3
    last = n_steps - 1
    full = lambda shape: pl.BlockSpec(shape, lambda n, *_: (0,) * len(shape))
    rows = lambda w_: pl.BlockSpec((ROW_BLK, w_), lambda n, *_: (n, 0))
    smem = lambda w_, imap: pl.BlockSpec((None, 1, w_), imap, memory_space=pltpu.SMEM)
    n_lat = SEQ // ROW_BLK
    grid_spec = pltpu.PrefetchScalarGridSpec(
        num_scalar_prefetch=1,
        grid=(n_steps,),
        in_specs=[
            smem(TOP_K * ROW_BLK, lambda n, *_: (n, 0, 0)),
            smem(TOP_K * ROW_BLK, lambda n, *_: (n, 0, 0)),
            smem(PIECES_MAX, lambda n, *_: (n, 0, 0)),
            smem(PIECES_MAX, lambda n, *_: (jnp.minimum(n + 1, last), 0, 0)),
            smem(PIECES_MAX, lambda n, *_: (jnp.minimum(n + 2, last), 0, 0)),
            pl.BlockSpec(memory_space=pl.ANY),
            rows(D_MODEL), rows(D_MODEL),
            pl.BlockSpec((None, 1, 6 * D_MODEL),
                         lambda n, *_: (_mod_row(n % per_batch, n // per_batch, n_lat), 0, 0)),
            full((1, D_MODEL)),
            full((D_MODEL, SHARED_DIM)), full((D_MODEL, SHARED_DIM)), full((SHARED_DIM, D_MODEL)),
        ],
        out_specs=rows(D_MODEL),
        scratch_shapes=[
            pltpu.VMEM((3, PK_STG_ROWS, LANES), U32),
            pltpu.VMEM((ROW_BLK * TILE_ROWS, LANES), F32),
            pltpu.SemaphoreType.DMA((3,)),
        ],
    )
    return pl.pallas_call(
        functools.partial(_combine_kernel, n_steps=n_steps),
        grid_spec=grid_spec,
        out_shape=jax.ShapeDtypeStruct(xn_rows.shape, F32),
        compiler_params=pltpu.CompilerParams(dimension_semantics=("arbitrary",),
                                             vmem_limit_bytes=VMEM_LIMIT),
        name="combine",
    )(npieces, pair_row, w_signed, piece_src, piece_src, piece_src, y_sorted, h2b_rows, xn_rows, mod3, g,
      swg, swu, swd)


def _moe_plan(cnt_rows, n_blocks):
    i32 = jnp.int32
    per_blk = MOE_BLK // PIECE
    to_expert = lambda v: v.reshape(v.shape[:-1] + (8, 8)).swapaxes(-1, -2).reshape(v.shape)
    runs = (cnt_rows + PIECE - 1) // PIECE
    loc_end = jnp.cumsum(runs, axis=1)
    loc = loc_end - runs
    npieces = loc_end[:, -1].astype(i32)
    reg = to_expert(runs.sum(axis=0))
    padded = (reg + per_blk - 1) // per_blk * per_blk
    pad_end = jnp.cumsum(padded)
    pad_start = pad_end - padded
    off = to_expert(pad_start)[None, :] + jnp.cumsum(runs, axis=0) - runs
    p = jnp.arange(PIECES_MAX, dtype=i32)[None, :, None]
    mine = (loc[:, None, :] <= p) & (p < loc_end[:, None, :])
    piece_dst = jnp.sum(jnp.where(mine, (off - loc)[:, None, :], 0), axis=-1) + p[:, :, 0]
    tail_n = padded - reg
    t_end = jnp.cumsum(tail_n)
    t_beg = t_end - tail_n
    q = jnp.arange(TAIL_MAX, dtype=i32)[:, None]
    tmine = (t_beg[None, :] <= q) & (q < t_end[None, :])
    tail_dst = jnp.sum(jnp.where(tmine, (pad_start + reg - t_beg)[None, :], 0), axis=-1) + q[:, 0]
    first = jnp.arange(n_blocks, dtype=i32)[:, None] * per_blk
    block_e = jnp.minimum(jnp.sum((pad_end[None, :] <= first).astype(i32), axis=-1), N_EXPERTS - 1)
    n_used = (pad_end[-1] // per_blk).astype(i32).reshape(1)
    return (npieces, piece_dst.astype(i32).reshape(-1, 1, PIECES_MAX),
            tail_dst.astype(i32).reshape(1, 1, TAIL_MAX), t_end[-1].astype(i32).reshape(1),
            block_e.astype(i32), n_used)


def _axial_tables(dim):
    f32 = np.float32
    t = np.arange(SEQ)
    row = (t // GRID_W).astype(f32)
    col = (t % GRID_W).astype(f32)
    quarter = dim // 4
    freqs = (f32(ROPE_THETA) ** (-np.arange(quarter, dtype=f32) / f32(quarter))).astype(f32)
    ar, ac = row[:, None] * freqs, col[:, None] * freqs
    ang = np.concatenate([ar, ar, ac, ac], axis=-1)
    sign = np.where((np.arange(dim) // quarter) % 2 == 0, -1.0, 1.0).astype(f32)
    cos = np.concatenate([np.cos(ang), np.ones((CTX_LEN, dim), f32)], axis=0)
    sin = np.concatenate([np.sin(ang) * sign, np.zeros((CTX_LEN, dim), f32)], axis=0)
    return cos.astype(f32), sin.astype(f32)


def _rope_tables():
    f32 = np.float32
    cos_h, sin_h = _axial_tables(HEAD_DIM)
    cos_r, sin_r = _axial_tables(B_ROPE)
    cosh, sinh = np.tile(cos_h, (1, 4)), np.tile(sin_h, (1, 4))
    scale_b = f32((B_NOPE + B_ROPE) ** -0.5)
    ones, zeros = np.ones((TOK, 64), f32), np.zeros((TOK, 64), f32)
    pad1, pad0 = np.ones((TOK, 32), f32), np.zeros((TOK, 32), f32)
    cosb = np.tile(np.concatenate([ones, cos_r, pad1], axis=1) * scale_b, (1, 4))
    sinb = np.tile(np.concatenate([zeros, sin_r, pad0], axis=1) * scale_b, (1, 4))
    cosr = np.concatenate([cos_r, np.ones((TOK, 96), f32)], axis=1)
    sinr = np.concatenate([sin_r, np.zeros((TOK, 96), f32)], axis=1)
    return tuple(jnp.asarray(a, F32) for a in (cosh, sinh, cosb, sinb, cosr, sinr))


_GQA_ORDER = (0, 2, 1, 3)


def _take_heads(w, base, order, axis):
    return jnp.concatenate(
        [lax.slice_in_dim(w, base + h * HEAD_DIM, base + (h + 1) * HEAD_DIM, axis=axis) for h in order],
        axis=axis)


def _w_in_layout(w_in):
    zeros = lambda n: jnp.zeros((D_MODEL, n), w_in.dtype)
    a_q = _take_heads(w_in, 0, _GQA_ORDER, 1)
    a_kv = w_in[:, 256:512]
    b_q = jnp.concatenate(
        [jnp.concatenate([w_in[:, 512 + h * 96:512 + (h + 1) * 96], zeros(32)], axis=1)
         for h in range(B_HEADS)], axis=1)
    b_c = w_in[:, 896:1024]
    b_r = jnp.concatenate([w_in[:, 1024:1056], zeros(96)], axis=1)
    c_q = _take_heads(w_in, 1056, _GQA_ORDER, 1)
    rest = w_in[:, 1312:2336]
    return jnp.concatenate([a_q, a_kv, b_q, b_c, b_r, c_q, rest], axis=1).astype(BF16)


def _w_kv_layout(w_kv_up):
    zeros = jnp.zeros((B_KV_RANK, 64), w_kv_up.dtype)
    nope = [jnp.concatenate([w_kv_up[:, h * 128:h * 128 + 64], zeros], axis=1) for h in range(B_HEADS)]
    val = [w_kv_up[:, h * 128 + 64:(h + 1) * 128] for h in range(B_HEADS)]
    return jnp.concatenate(nope + val, axis=1).astype(BF16)


def _w_out_layout(w_out):
    return jnp.concatenate([_take_heads(w_out, 0, _GQA_ORDER, 0), w_out[256:512],
                            _take_heads(w_out, 512, _GQA_ORDER, 0), w_out[768:1024]], axis=0).astype(BF16)


def _router_rows(v):
    return v.reshape((8, 8) + v.shape[1:]).swapaxes(0, 1).reshape(v.shape)


def kernel(x, c, ctx, c_ctx, ada_w, ada_b, pre_mix_g, post_mix_g, pre_ffn_g, post_ffn_g, w_in, a_q_norm, a_k_norm, b_kv_norm, b_w_kv_up, c_sink, d_rpb, w_out, router_w, router_b, exp_w_gate, exp_w_up, exp_w_down, sh_w_gate, sh_w_up, sh_w_down):
    bsz = x.shape[0]
    depth = ada_w.shape[0]
    xall = jnp.concatenate([x, ctx], axis=1)

    cvec = jnp.zeros((16, D_MODEL), F32).at[:bsz].set(c).at[8].set(c_ctx)
    mod = _ada(cvec, ada_w, ada_b)
    tabs = _rope_tables()
    avg = jnp.asarray(np.kron(np.eye(4, dtype=np.float32), np.full((64, 64), 1.0 / 64, np.float32)), BF16)
    row2 = lambda v: v.reshape(1, -1).astype(F32)

    for l in range(depth):
        with_ctx = l < depth - 1
        mod3 = mod[l].reshape(16, 1, 6 * D_MODEL)

        proj = _inproj(
            xall, mod3, row2(pre_mix_g[l]), _w_in_layout(w_in[l]),
            row2(jnp.tile(a_q_norm[l], 4) * HEAD_DIM ** -0.5), row2(jnp.tile(a_k_norm[l], 2)),
            row2(b_kv_norm[l]), _w_kv_layout(b_w_kv_up[l]), avg, tabs)
        qa, ka, va, qb, kb, vb, qc, kc, vc, qd, kd, vd = proj

        nq = NQ + (CTX_LEN // QBLK if with_ctx else 0)
        full = lambda shape: pl.BlockSpec(shape, lambda b, i: (0,) * len(shape))
        mix_a = _attn_call(_attn_a_kernel, "attn_a", qa, ka, va, [], [], nq // 2, qblk=2 * QBLK)
        mix_b = _attn_call(_attn_b_kernel, "attn_b", qb, kb, vb, [], [], nq // 2, qblk=2 * QBLK)
        sink = jnp.zeros((8, LANES), F32).at[:4].set(
            jnp.broadcast_to(c_sink[l][np.array(_GQA_ORDER)][:, None], (4, LANES)))
        mix_c = _attn_call(_attn_c_kernel, "attn_c", qc, kc, vc, [sink], [full((8, LANES))], nq)
        bias = _na_bias(d_rpb[l])
        mix_d = _attn_call(
            _attn_d_kernel, "attn_d", qd, kd, vd, [bias],
            [pl.BlockSpec((None, D_HEADS, QBLK, NA_WIN), lambda b, i: (_na_pattern(i), 0, 0, 0))], nq)

        nb = (TOK if with_ctx else SEQ) // ROW_BLK
        rb = jnp.broadcast_to(_router_rows(router_b[l])[:, None], (N_EXPERTS, ROW_BLK)).astype(F32)
        xn, h2b, h2t, wt_t, ls_t, cnt_t = _outproj(
            (mix_a, mix_b, mix_c, mix_d), _w_out_layout(w_out[l]), xall, mod3,
            row2(post_mix_g[l]), row2(pre_ffn_g[l]),
            _router_rows(router_w[l].T).astype(BF16), rb, nb)

        t_used = nb * ROW_BLK
        n_tok = bsz * t_used
        n_steps = bsz * nb
        n_blocks = -(-(n_tok * TOP_K + n_steps * N_EXPERTS * (PIECE - 1) + N_EXPERTS * (MOE_BLK - 1))
                     // MOE_BLK)
        npieces, piece_dst, tail_dst, ntail, block_e, n_used = _moe_plan(
            cnt_t[..., 0].reshape(n_steps, N_EXPERTS), n_blocks)
        lslot = ls_t.reshape(n_steps, 1, TOP_K * ROW_BLK)
        wts = wt_t.reshape(n_steps, 1, TOP_K * ROW_BLK)
        stg_row = lslot * TILE_ROWS
        pair_row = (lslot // 2) * TILE_ROWS
        w_signed = jnp.where(lslot % 2 == 0, wts, -wts)

        x_sorted = _dispatch(npieces, n_used, ntail, stg_row, piece_dst, tail_dst,
                             h2t.reshape(n_tok * TILE_ROWS, LANES), n_steps, n_blocks)
        y_sorted = _experts(block_e, n_used, x_sorted, exp_w_gate, exp_w_up, exp_w_down, l, n_blocks)
        xall = _combine(
            npieces, pair_row, w_signed, piece_dst, y_sorted, h2b.reshape(n_tok, D_MODEL),
            xn.reshape(n_tok, D_MODEL), mod3, row2(post_ffn_g[l]),
            sh_w_gate[l].astype(BF16), sh_w_up[l].astype(BF16), sh_w_down[l].astype(BF16),
            n_steps, nb).reshape(bsz, t_used, D_MODEL)

    return xall[:, :SEQ]
```

```python
import functools

import numpy as np
import jax
import jax.numpy as jnp
from jax import lax
from jax.experimental import pallas as pl
from jax.experimental.pallas import tpu as pltpu

F32 = jnp.float32
BF16 = jnp.bfloat16
U32 = jnp.uint32

D_MODEL = 1024
SEQ = 2048
CTX_LEN = 256
TOK = SEQ + CTX_LEN
GRID_W = 64
HEAD_DIM = 64
ROPE_THETA = 10000.0
EPS = 1e-6
NEG = -1e30

A_HEADS, A_KV_HEADS = 4, 2
B_HEADS, B_NOPE, B_ROPE, B_V, B_KV_RANK = 4, 64, 32, 64, 128
C_HEADS, C_KV_HEADS, C_WINDOW = 4, 2, 128
D_HEADS, NA_ROWS, NA_COLS = 4, 8, 16

N_EXPERTS, TOP_K, N_GROUPS, TOPK_GROUPS = 64, 8, 8, 4
EXPERT_DIM, SHARED_DIM = 256, 256
ROUTED_SCALE = 2.5

LANES = 128
QBLK = 128
ROW_BLK = 256
IN_BLK = 768
MOE_BLK = 512
TILE_ROWS = D_MODEL // LANES
PIECE = 8
PIECE_ROWS = PIECE * TILE_ROWS
PIECES_MAX = 320
STG_ROWS = PIECES_MAX * PIECE_ROWS
PK_PIECE_ROWS = PIECE_ROWS // 2
PK_STG_ROWS = STG_ROWS // 2
PK_BLK_ROWS = MOE_BLK * TILE_ROWS // 2
TAIL_MAX = N_EXPERTS * MOE_BLK // PIECE
LOG_PACK_GROUP = 2
PACK_GROUP = 1 << LOG_PACK_GROUP
LOG_WAIT_GROUP = 4
WAIT_GROUP = 1 << LOG_WAIT_GROUP
LOG_CHUNKS = 2
N_CHUNKS = 1 << LOG_CHUNKS
VMEM_LIMIT = 56 * 1024 * 1024
NQ = SEQ // QBLK
NA_WIN = 10 * GRID_W

_AQ, _AK, _AV, _BQ, _BC, _BR = 0, 256, 384, 512, 1024, 1152
_CQ, _CK, _CV, _DQ, _DK, _DV = 1280, 1536, 1664, 1792, 2048, 2304
IN_W = 2560

_NT = (((1,), (1,)), ((), ()))


def _sigmoid(x):
    return 1.0 / (1.0 + jnp.exp(-x))


def _rms(x, g):
    return x * lax.rsqrt(jnp.mean(x * x, axis=-1, keepdims=True) + EPS) * g


def _dot(a, b):
    return jnp.dot(a, b, preferred_element_type=F32)


def _dot_nt(a, b):
    return lax.dot_general(a, b, _NT, preferred_element_type=F32)


def _ada_kernel(c_ref, w_ref, b_ref, o_ref):
    c = c_ref[...]
    a = (c * _sigmoid(c)).astype(BF16)
    o_ref[...] = _dot(a, w_ref[...].astype(BF16)) + b_ref[...]


def _ada(cvec, ada_w, ada_b):
    depth = ada_w.shape[0]
    n = ada_w.shape[2]
    tn = 1536
    return pl.pallas_call(
        _ada_kernel,
        grid=(depth, n // tn),
        in_specs=[
            pl.BlockSpec((16, D_MODEL), lambda l, j: (0, 0)),
            pl.BlockSpec((None, D_MODEL, tn), lambda l, j: (l, 0, j)),
            pl.BlockSpec((None, 1, tn), lambda l, j: (l, 0, j)),
        ],
        out_specs=pl.BlockSpec((None, 16, tn), lambda l, j: (l, 0, j)),
        out_shape=jax.ShapeDtypeStruct((depth, 16, n), F32),
        name="ada",
    )(cvec, ada_w, ada_b.reshape(depth, 1, n))


def _rope(x, cos, sin, chunk):
    outs = []
    for c in range(x.shape[1] // LANES):
        sl = slice(c * LANES, (c + 1) * LANES)
        xs = x[:, sl]
        lane = lax.broadcasted_iota(jnp.int32, xs.shape, 1)
        first = (lane & chunk) == 0
        rot = jnp.where(first, pltpu.roll(xs, LANES - chunk, 1), pltpu.roll(xs, chunk, 1))
        outs.append(xs * cos[:, sl] + rot * sin[:, sl])
    return outs[0] if len(outs) == 1 else jnp.concatenate(outs, axis=1)


def _head_norm(x, avg):
    sq = x * x
    hi = sq.astype(BF16)
    lo = (sq - hi.astype(F32)).astype(BF16)
    ms = _dot(hi, avg) + _dot(lo, avg)
    return x * lax.rsqrt(ms + EPS)


def _inproj_kernel(x_ref, mod_ref, modc_ref, g_ref, w_ref, qg_ref, kg_ref, kvg_ref, wkv_ref, avg_ref,
                   cosh_ref, sinh_ref, cosb_ref, sinb_ref, cosr_ref, sinr_ref,
                   qa_ref, ka_ref, va_ref, qb_ref, kb_ref, vb_ref,
                   qc_ref, kc_ref, vc_ref, qd_ref, kd_ref, vd_ref):
    d = D_MODEL
    n = x_ref.shape[0]
    is_ctx = pl.program_id(1) * n + lax.broadcasted_iota(jnp.int32, (n, 1), 0) >= SEQ
    scale = jnp.where(is_ctx, modc_ref[:, d:2 * d], mod_ref[:, d:2 * d])
    shift = jnp.where(is_ctx, modc_ref[:, 0:d], mod_ref[:, 0:d])
    h = _rms(x_ref[...], g_ref[...]) * (1.0 + scale) + shift
    p = _dot(h.astype(BF16), w_ref[...])
    cosh, sinh = cosh_ref[...], sinh_ref[...]
    avg = avg_ref[...]

    qa = _head_norm(p[:, _AQ:_AQ + 256], avg) * qg_ref[...]
    qa_ref[...] = _rope(qa, cosh, sinh, 16).astype(BF16)
    ka = _head_norm(p[:, _AK:_AK + 128], avg[0:128, 0:128]) * kg_ref[...]
    ka_ref[...] = _rope(ka, cosh[:, 0:128], sinh[:, 0:128], 16).astype(BF16)
    ones = jnp.ones((p.shape[0], LANES), BF16)
    va_ref[...] = jnp.concatenate([p[:, _AV:_AV + 128].astype(BF16), ones], axis=1)

    qb_ref[...] = _rope(p[:, _BQ:_BQ + 512], cosb_ref[...], sinb_ref[...], 8).astype(BF16)
    cn = _rms(p[:, _BC:_BC + 128], kvg_ref[...])
    kv = _dot(cn.astype(BF16), wkv_ref[...])
    kr = _rope(p[:, _BR:_BR + 128], cosr_ref[...], sinr_ref[...], 8)
    kr = pltpu.roll(kr, 64, 1)
    kb_ref[...] = jnp.concatenate(
        [kv[:, hh * 128:(hh + 1) * 128] + kr for hh in range(B_HEADS)], axis=1).astype(BF16)
    vb_ref[...] = jnp.concatenate(
        [kv[:, 512:640].astype(BF16), ones, kv[:, 640:768].astype(BF16), ones], axis=1)

    qc_ref[...] = _rope(p[:, _CQ:_CQ + 256] * 0.125, cosh, sinh, 16).astype(BF16)
    kc_ref[...] = _rope(p[:, _CK:_CK + 128], cosh[:, 0:128], sinh[:, 0:128], 16).astype(BF16)
    vc_ref[...] = jnp.concatenate([p[:, _CV:_CV + 128].astype(BF16), ones], axis=1)

    qd_ref[...] = (p[:, _DQ:_DQ + 256] * 0.125).astype(BF16)
    kd_ref[...] = p[:, _DK:_DK + 256].astype(BF16)
    vd_ref[...] = jnp.concatenate(
        [p[:, _DV:_DV + 128].astype(BF16), ones, p[:, _DV + 128:_DV + 256].astype(BF16), ones], axis=1)


def _mod_row(j, b, n_lat_blocks):
    return jnp.where(j < n_lat_blocks, b, 8)


def _inproj(xall, mod3, g, w, qg, kg, kvg, wkv, avg, tabs):
    bsz = xall.shape[0]
    nb = TOK // IN_BLK
    full = lambda shape: pl.BlockSpec(shape, lambda b, j: (0,) * len(shape))
    row = lambda w_: pl.BlockSpec((IN_BLK, w_), lambda b, j: (j, 0))
    out = lambda w_: pl.BlockSpec((None, IN_BLK, w_), lambda b, j: (b, j, 0))
    widths = (256, 128, 256, 512, 512, 512, 256, 128, 256, 256, 256, 512)
    return pl.pallas_call(
        _inproj_kernel,
        grid=(bsz, nb),
        in_specs=[
            pl.BlockSpec((None, IN_BLK, D_MODEL), lambda b, j: (b, j, 0)),
            pl.BlockSpec((None, 1, 6 * D_MODEL), lambda b, j: (b, 0, 0)),
            pl.BlockSpec((None, 1, 6 * D_MODEL), lambda b, j: (8, 0, 0)),
            full((1, D_MODEL)), full((D_MODEL, IN_W)),
            full((1, 256)), full((1, 128)), full((1, 128)), full((128, 768)), full((256, 256)),
            row(256), row(256), row(512), row(512), row(128), row(128),
        ],
        out_specs=[out(w_) for w_ in widths],
        out_shape=[jax.ShapeDtypeStruct((bsz, TOK, w_), BF16) for w_ in widths],
        compiler_params=pltpu.CompilerParams(vmem_limit_bytes=VMEM_LIMIT),
        name="inproj",
    )(xall, mod3, mod3, g, w, qg, kg, kvg, wkv, avg, *tabs)


def _softmax_pv(segs, sink=None):
    m = None
    for s, _ in segs:
        ms = jnp.max(s, axis=-1, keepdims=True)
        m = ms if m is None else jnp.maximum(m, ms)
    if sink is not None:
        m = jnp.maximum(m, sink)
    acc = None
    for s, v1 in segs:
        r = _dot(jnp.exp((s - m).astype(BF16)), v1)
        acc = r if acc is None else acc + r
    l = acc[:, LANES:LANES + 1]
    if sink is not None:
        l = l + jnp.exp(sink - m)
    return acc[:, 0:LANES] / l


def _lane_lo(dtype_shape):
    return lax.broadcasted_iota(jnp.int32, dtype_shape, 1) < HEAD_DIM


def _half(qt, hh):
    lo = _lane_lo(qt.shape)
    return jnp.where(lo if hh == 0 else jnp.logical_not(lo), qt, jnp.zeros_like(qt))


def _merge_halves(o0, o1):
    return jnp.where(_lane_lo(o0.shape), o0, o1)


def _stack_heads(q_ref, tiles):
    return jnp.concatenate(
        [_half(q_ref[:, t * LANES:(t + 1) * LANES], hh) for t in tiles for hh in range(2)], axis=0)


def _store_heads(o_ref, o, tiles):
    n = o_ref.shape[0]
    for ti, t in enumerate(tiles):
        o0, o1 = o[2 * ti * n:(2 * ti + 1) * n], o[(2 * ti + 1) * n:(2 * ti + 2) * n]
        o_ref[:, t * LANES:(t + 1) * LANES] = _merge_halves(o0, o1).astype(o_ref.dtype)


def _attn_a_kernel(q_ref, k_ref, v_ref, o_ref, *, n_lat):
    def run(k, v1):
        for t in range(2):
            o = [_softmax_pv([(_dot_nt(_half(q_ref[:, t * LANES:(t + 1) * LANES], hh), k), v1)])
                 for hh in range(2)]
            o_ref[:, t * LANES:(t + 1) * LANES] = _merge_halves(o[0], o[1]).astype(o_ref.dtype)

    @pl.when(pl.program_id(1) < n_lat)
    def _():
        run(k_ref[...], v_ref[...])

    @pl.when(pl.program_id(1) >= n_lat)
    def _():
        run(k_ref[SEQ:TOK, :], v_ref[SEQ:TOK, :])


def _attn_b_kernel(q_ref, k_ref, v_ref, o_ref, *, n_lat):
    def run(lo_, hi_):
        for t in range(2):
            v1 = v_ref[lo_:hi_, t * 256:(t + 1) * 256]
            o = []
            for hh in range(2):
                hd = 2 * t + hh
                q = q_ref[:, hd * LANES:(hd + 1) * LANES]
                k = k_ref[lo_:hi_, hd * LANES:(hd + 1) * LANES]
                o.append(_softmax_pv([(_dot_nt(q, k), v1)]))
            o_ref[:, t * LANES:(t + 1) * LANES] = _merge_halves(o[0], o[1]).astype(o_ref.dtype)

    @pl.when(pl.program_id(1) < n_lat)
    def _():
        run(0, TOK)

    @pl.when(pl.program_id(1) >= n_lat)
    def _():
        run(SEQ, TOK)


def _attn_c_kernel(q_ref, k_ref, v_ref, sink_ref, o_ref, *, n_lat):
    i = pl.program_id(1)
    qblk = o_ref.shape[0]
    band = 3 * qblk

    def run(latent):
        if latent:
            start = pl.multiple_of(jnp.clip((i - 1) * qblk, 0, SEQ - band), qblk)
            shape = (2 * qblk, band)
            qpos = i * qblk + (lax.broadcasted_iota(jnp.int32, shape, 0) & (qblk - 1))
            kpos = start + lax.broadcasted_iota(jnp.int32, shape, 1)
            valid = jnp.abs(qpos - kpos) <= C_WINDOW
        for t in range(2):
            q = _stack_heads(q_ref, (t,))
            segs = [(_dot_nt(q, k_ref[SEQ:TOK, :]), v_ref[SEQ:TOK, :])]
            if latent:
                s = jnp.where(valid, _dot_nt(q, k_ref[pl.ds(start, band), :]), NEG)
                segs.insert(0, (s, v_ref[pl.ds(start, band), :]))
            sink = jnp.concatenate(
                [jnp.broadcast_to(sink_ref[j:j + 1, 0:1], (qblk, 1)) for j in (2 * t, 2 * t + 1)], axis=0)
            _store_heads(o_ref, _softmax_pv(segs, sink=sink), (t,))

    @pl.when(i < n_lat)
    def _():
        run(True)

    @pl.when(i >= n_lat)
    def _():
        run(False)


def _attn_d_kernel(q_ref, k_ref, v_ref, bias_ref, o_ref, *, n_lat):
    i = pl.program_id(1)
    qblk = o_ref.shape[0]

    def run(latent):
        if latent:
            start = pl.multiple_of(jnp.clip((i - 2) * qblk, 0, SEQ - NA_WIN), qblk)
        for t in range(2):
            ksl, vsl = slice(t * LANES, (t + 1) * LANES), slice(t * 256, (t + 1) * 256)
            q = _stack_heads(q_ref, (t,))
            segs = [(_dot_nt(q, k_ref[SEQ:TOK, ksl]), v_ref[SEQ:TOK, vsl])]
            if latent:
                bias = bias_ref[2 * t:2 * t + 2].reshape(2 * qblk, NA_WIN)
                segs.insert(0, (_dot_nt(q, k_ref[pl.ds(start, NA_WIN), ksl]) + bias,
                                v_ref[pl.ds(start, NA_WIN), vsl]))
            _store_heads(o_ref, _softmax_pv(segs), (t,))

    @pl.when(i < n_lat)
    def _():
        run(True)

    @pl.when(i >= n_lat)
    def _():
        run(False)


def _attn_call(kernel, name, q, k, v, extra, extra_specs, n_blocks, qblk=QBLK):
    bsz = q.shape[0]
    return pl.pallas_call(
        functools.partial(kernel, n_lat=SEQ // qblk),
        grid=(bsz, n_blocks),
        in_specs=[
            pl.BlockSpec((None, qblk, q.shape[2]), lambda b, i: (b, i, 0)),
            pl.BlockSpec((None, TOK, k.shape[2]), lambda b, i: (b, 0, 0)),
            pl.BlockSpec((None, TOK, v.shape[2]), lambda b, i: (b, 0, 0)),
        ] + extra_specs,
        out_specs=pl.BlockSpec((None, qblk, 256), lambda b, i: (b, i, 0)),
        out_shape=jax.ShapeDtypeStruct((bsz, n_blocks * qblk, 256), BF16),
        name=name,
    )(q, k, v, *extra)


def _na_pattern(i):
    return jnp.where(i < 2, i, jnp.where(i < NQ - 2, 2, jnp.minimum(i, NQ - 1) - (NQ - 5)))


def _na_bias(rpb):
    rows = SEQ // GRID_W
    n_dr, n_dc = 2 * NA_ROWS - 1, 2 * NA_COLS - 1
    qc = np.arange(GRID_W)
    dc = np.clip(qc[None, :] - qc[:, None] + NA_COLS - 1, 0, n_dc - 1)
    sel_c = jnp.asarray(np.eye(n_dc, dtype=np.float32)[dc])
    cs = np.clip(qc - NA_COLS // 2, 0, GRID_W - NA_COLS)[:, None]
    col_ok = (qc[None, :] >= cs) & (qc[None, :] < cs + NA_COLS)
    sel_r, row_ok = [], []
    for i in (0, 1, 2, NQ - 2, NQ - 1):
        start_row = min(max(2 * i - 4, 0), rows - 10)
        qr = 2 * i + np.arange(2)
        kr = start_row + np.arange(10)
        rs = np.clip(qr - NA_ROWS // 2, 0, rows - NA_ROWS)[:, None]
        row_ok.append((kr[None] >= rs) & (kr[None] < rs + NA_ROWS))
        dr = np.clip(kr[None] - qr[:, None] + NA_ROWS - 1, 0, n_dr - 1)
        sel_r.append(np.eye(n_dr, dtype=np.float32)[dr])
    b = jnp.einsum("hrc,pabr,qkc->phaqbk", rpb.astype(F32), jnp.asarray(np.stack(sel_r)), sel_c,
                   precision=lax.Precision.HIGHEST)
    valid = np.stack(row_ok)[:, None, :, None, :, None] & col_ok[None, None, None, :, None, :]
    return jnp.where(valid, b, NEG).reshape(5, D_HEADS, QBLK, NA_WIN)


def _outproj_kernel(ma_ref, mb_ref, mc_ref, md_ref, w_ref, x_ref, mod_ref, g1_ref, g2_ref,
                    wr_ref, rb_ref, xn_ref, h2b_ref, h2t_ref, wt_ref, ls_ref, cnt_ref):
    d = D_MODEL
    mix = jnp.concatenate([ma_ref[...], mb_ref[...], mc_ref[...], md_ref[...]], axis=1)
    y = _rms(_dot(mix, w_ref[...]), g1_ref[...])
    xn = x_ref[...] + mod_ref[:, 2 * d:3 * d] * y
    xn_ref[...] = xn
    h2 = _rms(xn, g2_ref[...]) * (1.0 + mod_ref[:, 4 * d:5 * d]) + mod_ref[:, 3 * d:4 * d]
    n = h2.shape[0]
    h2b_ref[...] = h2.astype(BF16)
    for s in range(TILE_ROWS):
        h2t_ref[pl.ds(s, n, stride=TILE_ROWS), :] = h2[:, s * LANES:(s + 1) * LANES]

    s = _sigmoid(_dot_nt(wr_ref[...], h2.astype(BF16)))
    sb = s + rb_ref[...]
    per = N_EXPERTS // N_GROUPS
    slab = [sb[j * 8:(j + 1) * 8, :] for j in range(per)]
    sraw = [s[j * 8:(j + 1) * 8, :] for j in range(per)]
    giota = lax.broadcasted_iota(jnp.int32, (N_GROUPS, n), 0)
    eid = [giota * per + j for j in range(per)]
    ninf = -jnp.inf

    m1 = functools.reduce(jnp.maximum, slab)
    j1 = functools.reduce(jnp.minimum, [jnp.where(slab[j] == m1, j, per) for j in range(per)])
    m2 = functools.reduce(jnp.maximum, [jnp.where(j1 == j, ninf, slab[j]) for j in range(per)])
    gs = m1 + m2

    gsel = jnp.zeros((N_GROUPS, n), jnp.bool_)
    for _ in range(TOPK_GROUPS):
        m = jnp.max(gs, axis=0, keepdims=True)
        gi = jnp.min(jnp.where(gs == m, giota, N_GROUPS), axis=0, keepdims=True)
        hit = giota == gi
        gsel = jnp.logical_or(gsel, hit)
        gs = jnp.where(hit, ninf, gs)
    slab = [jnp.where(gsel, sj, ninf) for sj in slab]

    def pick(hits, vals):
        return jnp.sum(functools.reduce(jnp.add, [jnp.where(hits[j], vals[j], 0.0) for j in range(per)]),
                       axis=0, keepdims=True)

    kiota = lax.broadcasted_iota(jnp.int32, (TOP_K, n), 0)
    wts = jnp.zeros((TOP_K, n), F32)
    chosen = []
    for k in range(TOP_K):
        m = jnp.max(functools.reduce(jnp.maximum, slab), axis=0, keepdims=True)
        cand = functools.reduce(jnp.minimum,
                                [jnp.where(slab[j] == m, eid[j], N_EXPERTS) for j in range(per)])
        ei = jnp.min(cand, axis=0, keepdims=True)
        hits = [eid[j] == ei for j in range(per)]
        slab = [jnp.where(hits[j], ninf, slab[j]) for j in range(per)]
        wts = jnp.where(kiota == k, pick(hits, sraw), wts)
        chosen.append(ei)
    wt_ref[...] = wts / jnp.sum(wts, axis=0, keepdims=True) * ROUTED_SCALE

    hits_k = [[eid[j] == chosen[k] for j in range(per)] for k in range(TOP_K)]
    sel = jnp.concatenate(
        [functools.reduce(jnp.add, [jnp.where(hits_k[k][j], 1.0, 0.0) for k in range(TOP_K)])
         for j in range(per)], axis=0)
    tri = jnp.where(lax.broadcasted_iota(jnp.int32, (n, n), 0) < lax.broadcasted_iota(jnp.int32, (n, n), 1),
                    1.0, 0.0).astype(BF16)
    before = _dot(sel.astype(BF16), tri)
    cnt = jnp.sum(sel, axis=1, keepdims=True)
    run = jnp.floor((cnt + (PIECE - 1.0)) * (1.0 / PIECE)) * PIECE
    low = jnp.where(lax.broadcasted_iota(jnp.int32, (N_EXPERTS, N_EXPERTS), 1)
                    < lax.broadcasted_iota(jnp.int32, (N_EXPERTS, N_EXPERTS), 0), 1.0, 0.0).astype(BF16)
    start = _dot(low, jnp.broadcast_to(run, (N_EXPERTS, LANES)).astype(BF16))[:, 0:1]
    base = before + start
    base = [base[j * 8:(j + 1) * 8, :] for j in range(per)]
    lsl = jnp.zeros((TOP_K, n), F32)
    for k in range(TOP_K):
        lsl = jnp.where(kiota == k, pick(hits_k[k], base), lsl)
    ls_ref[...] = lsl.astype(jnp.int32)
    cnt_ref[...] = jnp.broadcast_to(cnt, (N_EXPERTS, LANES)).astype(jnp.int32)


def _outproj(mixes, w, xall, mod3, g1, g2, wr, rb, nb):
    bsz = xall.shape[0]
    n_lat = SEQ // ROW_BLK
    full = lambda shape: pl.BlockSpec(shape, lambda b, j: (0,) * len(shape))
    rows = lambda w_: pl.BlockSpec((None, ROW_BLK, w_), lambda b, j: (b, j, 0))
    route = pl.BlockSpec((None, None, TOP_K, ROW_BLK), lambda b, j: (b, j, 0, 0))
    return pl.pallas_call(
        _outproj_kernel,
        grid=(bsz, nb),
        in_specs=[rows(256)] * 4 + [
            full((D_MODEL, D_MODEL)), rows(D_MODEL),
            pl.BlockSpec((None, 1, 6 * D_MODEL), lambda b, j: (_mod_row(j, b, n_lat), 0, 0)),
            full((1, D_MODEL)), full((1, D_MODEL)),
            full((N_EXPERTS, D_MODEL)), full((N_EXPERTS, ROW_BLK)),
        ],
        out_specs=[
            rows(D_MODEL), rows(D_MODEL),
            pl.BlockSpec((None, ROW_BLK * TILE_ROWS, LANES), lambda b, j: (b, j, 0)),
            route, route,
            pl.BlockSpec((None, None, N_EXPERTS, LANES), lambda b, j: (b, j, 0, 0)),
        ],
        out_shape=[
            jax.ShapeDtypeStruct((bsz, nb * ROW_BLK, D_MODEL), F32),
            jax.ShapeDtypeStruct((bsz, nb * ROW_BLK, D_MODEL), BF16),
            jax.ShapeDtypeStruct((bsz, nb * ROW_BLK * TILE_ROWS, LANES), F32),
            jax.ShapeDtypeStruct((bsz, nb, TOP_K, ROW_BLK), F32),
            jax.ShapeDtypeStruct((bsz, nb, TOP_K, ROW_BLK), jnp.int32),
            jax.ShapeDtypeStruct((bsz, nb, N_EXPERTS, LANES), jnp.int32),
        ],
        name="outproj",
    )(*mixes, w, xall, mod3, g1, g2, wr, rb)


def _rows(i, n):
    return pl.ds(pl.multiple_of(i * n, n), n)


def _tile_at(row):
    return pl.ds(pl.multiple_of(row, TILE_ROWS), TILE_ROWS)


def _piece(p):
    return _rows(p, PK_PIECE_ROWS)


def _pack_pairs(lo, hi):
    return (lax.shift_right_logical(pltpu.bitcast(lo, U32), jnp.uint32(16))
            | (pltpu.bitcast(hi, U32) & jnp.uint32(0xFFFF0000)))


def _unpack_lo(u):
    return pltpu.bitcast(lax.shift_left(u, jnp.uint32(16)), F32)


def _unpack_hi(u):
    return pltpu.bitcast(u & jnp.uint32(0xFFFF0000), F32)


def _start_pieces(lo, hi, start):
    n = hi - lo

    def quad(q, carry):
        for r in range(4):
            start(lo + 4 * q + r, r % 2)
        return carry
    lax.fori_loop(0, lax.shift_right_logical(n, 2), quad, 0)

    def rest(r, carry):
        start(hi - 1 - r, 0)
        return carry
    lax.fori_loop(0, n & 3, rest, 0)


def _wait_pieces(n, piece_rows, copy_of_rows):
    def group(p, carry):
        copy_of_rows(WAIT_GROUP * piece_rows).wait()
        return carry
    lax.fori_loop(0, lax.shift_right_logical(n, LOG_WAIT_GROUP), group, 0)

    def single(p, carry):
        copy_of_rows(piece_rows).wait()
        return carry
    lax.fori_loop(0, n & (WAIT_GROUP - 1), single, 0)


def _interleave(body, n_pieces, start):
    per = ROW_BLK // N_CHUNKS

    def chunk(c, carry):
        for tt in range(per):
            body(c * per + tt)
        _start_pieces(lax.shift_right_logical(c * n_pieces, LOG_CHUNKS),
                      lax.shift_right_logical((c + 1) * n_pieces, LOG_CHUNKS), start)
        return carry
    lax.fori_loop(0, N_CHUNKS, chunk, 0)


def _dispatch_kernel(np_ref, nu_ref, nt_ref, ls_ref, pd_ref, pdp_ref, td_ref, h_ref, xs_hbm,
                     stg, pk, zblk, sem, zsem, *, n_steps, n_blocks):
    i = pl.program_id(0)
    slot = i % 3
    prev = (i + 2) % 3

    def piece_copy(s, p, d):
        return pltpu.make_async_copy(pk.at[s, _piece(p), :], xs_hbm.at[_piece(d), :], sem.at[s])

    def wait_pieces(s, n):
        _wait_pieces(n, PK_PIECE_ROWS, lambda rows: pltpu.make_async_copy(
            pk.at[s, pl.ds(0, rows), :], xs_hbm.at[pl.ds(0, rows), :], sem.at[s]))

    def zero_piece_copy(d):
        return pltpu.make_async_copy(zblk.at[pl.ds(0, PK_PIECE_ROWS), :], xs_hbm.at[_piece(d), :], zsem.at[0])

    def zero_block_copy(b):
        return pltpu.make_async_copy(zblk, xs_hbm.at[_rows(b, PK_BLK_ROWS), :], zsem.at[1])

    @pl.when(i == 0)
    def _():
        zblk[...] = jnp.zeros_like(zblk)

        def zero(c, carry):
            stg[_rows(c, 2048), :] = jnp.zeros((2048, LANES), F32)
            return carry
        lax.fori_loop(0, STG_ROWS // 2048, zero, 0)

        def blk(b, carry):
            zero_block_copy(b).start()
            return carry
        lax.fori_loop(nu_ref[0], n_blocks, blk, 0)

    @pl.when(i >= 3)
    def _():
        wait_pieces(slot, np_ref[jnp.maximum(i - 3, 0)])

    def scatter_token(t):
        tile = h_ref[_rows(t, TILE_ROWS), :].astype(BF16).astype(F32)
        for k in range(TOP_K):
            stg[_tile_at(ls_ref[0, k * ROW_BLK + t]), :] = tile

    np_prev = jnp.where(i >= 1, np_ref[jnp.maximum(i - 1, 0)], 0)
    _interleave(scatter_token, np_prev,
                lambda p, prio: piece_copy(prev, p, pdp_ref[0, p]).start(priority=prio))

    def pack_group(q, carry):
        v = stg[_rows(q, PACK_GROUP * PIECE_ROWS), :].reshape(PACK_GROUP * PIECE // 2, 2 * TILE_ROWS, LANES)
        packed = _pack_pairs(v[:, 0:TILE_ROWS, :], v[:, TILE_ROWS:2 * TILE_ROWS, :])
        pk[slot, _rows(q, PACK_GROUP * PK_PIECE_ROWS), :] = packed.reshape(PACK_GROUP * PK_PIECE_ROWS, LANES)
        return carry
    lax.fori_loop(0, lax.shift_right_logical(np_ref[i] + (PACK_GROUP - 1), LOG_PACK_GROUP), pack_group, 0)

    @pl.when(i == n_steps - 1)
    def _():
        _start_pieces(0, np_ref[i], lambda p, prio: piece_copy(slot, p, pd_ref[0, p]).start(priority=prio))

        def tail(p, carry):
            zero_piece_copy(td_ref[0, p]).start()
            return carry
        lax.fori_loop(0, nt_ref[0], tail, 0)

        wait_pieces((i + 1) % 3, np_ref[jnp.maximum(i - 2, 0)])
        wait_pieces(prev, np_ref[jnp.maximum(i - 1, 0)])
        wait_pieces(slot, np_ref[i])

        def tail_wait(p, carry):
            zero_piece_copy(0).wait()
            return carry
        lax.fori_loop(0, nt_ref[0], tail_wait, 0)

        def blk_wait(b, carry):
            zero_block_copy(0).wait()
            return carry
        lax.fori_loop(nu_ref[0], n_blocks, blk_wait, 0)


def _dispatch(npieces, n_used, ntail, lslot, piece_dst, tail_dst, h2t, n_steps, n_blocks):
    assert n_steps >= 3
    smem = lambda n, imap: pl.BlockSpec((None, 1, n), imap, memory_space=pltpu.SMEM)
    grid_spec = pltpu.PrefetchScalarGridSpec(
        num_scalar_prefetch=3,
        grid=(n_steps,),
        in_specs=[
            smem(TOP_K * ROW_BLK, lambda i, *_: (i, 0, 0)),
            smem(PIECES_MAX, lambda i, *_: (i, 0, 0)),
            smem(PIECES_MAX, lambda i, *_: (jnp.maximum(i - 1, 0), 0, 0)),
            smem(TAIL_MAX, lambda i, *_: (0, 0, 0)),
            pl.BlockSpec((ROW_BLK * TILE_ROWS, LANES), lambda i, *_: (i, 0)),
        ],
        out_specs=pl.BlockSpec(memory_space=pl.ANY),
        scratch_shapes=[
            pltpu.VMEM((STG_ROWS, LANES), F32),
            pltpu.VMEM((3, PK_STG_ROWS, LANES), U32),
            pltpu.VMEM((PK_BLK_ROWS, LANES), U32),
            pltpu.SemaphoreType.DMA((3,)),
            pltpu.SemaphoreType.DMA((2,)),
        ],
    )
    return pl.pallas_call(
        functools.partial(_dispatch_kernel, n_steps=n_steps, n_blocks=n_blocks),
        grid_spec=grid_spec,
        out_shape=jax.ShapeDtypeStruct((n_blocks * PK_BLK_ROWS, LANES), U32),
        compiler_params=pltpu.CompilerParams(dimension_semantics=("arbitrary",),
                                             vmem_limit_bytes=VMEM_LIMIT),
        name="dispatch",
    )(npieces, n_used, ntail, lslot, piece_dst, piece_dst, tail_dst, h2t)


def _from_tiles(ref, n):
    return jnp.concatenate([ref[pl.ds(s, n, stride=TILE_ROWS), :] for s in range(TILE_ROWS)], axis=1)


def _experts_kernel(be_ref, nu_ref, xs_hbm, wg_ref, wu_ref, wd_ref, ys_hbm,
                    xbuf, ybuf, zbuf, wgb, wub, wdb, xsem, ysem, zsem, *, n_blocks):
    i = pl.program_id(0)
    n_used = nu_ref[0]
    rows = PK_BLK_ROWS
    half = rows // 2
    pairs = MOE_BLK // 2

    def halves(b):
        return [pl.ds(pl.multiple_of(b * rows + h * half, half), half) for h in range(2)]

    def x_copies(b, s):
        return [pltpu.make_async_copy(xs_hbm.at[src, :], xbuf.at[s, pl.ds(h * half, half), :], xsem.at[s])
                for h, src in enumerate(halves(b))]

    def y_copies(b, s):
        return [pltpu.make_async_copy(ybuf.at[s, pl.ds(h * half, half), :], ys_hbm.at[dst, :], ysem.at[s])
                for h, dst in enumerate(halves(b))]

    def start(copies):
        for h, c in enumerate(copies):
            c.start(priority=h)

    def wait(copies):
        for c in copies:
            c.wait()

    def zero_fill(b):
        return pltpu.make_async_copy(zbuf, ys_hbm.at[_rows(b, rows), :], zsem.at[0])

    @pl.when(i == 0)
    def _():
        start(x_copies(0, 0))

        @pl.when(n_used > 1)
        def _():
            start(x_copies(1, 1))

        zbuf[...] = jnp.zeros_like(zbuf)

        def fill(b, carry):
            zero_fill(b).start()
            return carry
        lax.fori_loop(n_used, n_blocks, fill, 0)

    @pl.when(i + 2 < n_used)
    def _():
        start(x_copies(i + 2, (i + 2) % 3))

    @pl.when(i < n_used)
    def _():
        wait(x_copies(i, i % 3))

        @pl.when(i >= 2)
        def _():
            wait(y_copies(i - 2, i % 2))

        @pl.when(jnp.logical_or(i == 0, be_ref[i] != be_ref[jnp.maximum(i - 1, 0)]))
        def _():
            wgb[...] = wg_ref[...].astype(BF16)
            wub[...] = wu_ref[...].astype(BF16)
            wdb[...] = wd_ref[...].astype(BF16)

        x = pltpu.bitcast(_from_tiles(xbuf.at[i % 3], pairs), BF16)
        g = _dot(x, wgb[...])
        u = _dot(x, wub[...])
        y = _dot((g * _sigmoid(g) * u).astype(BF16), wdb[...])
        yp = pltpu.bitcast(y.astype(BF16), U32)
        yb = ybuf.at[i % 2]
        for s in range(TILE_ROWS):
            yb[pl.ds(s, pairs, stride=TILE_ROWS), :] = yp[:, s * LANES:(s + 1) * LANES]
        start(y_copies(i, i % 2))

        @pl.when(i == n_used - 1)
        def _():
            @pl.when(i >= 1)
            def _():
                wait(y_copies(i - 1, (i + 1) % 2))
            wait(y_copies(i, i % 2))

            def fill_wait(b, carry):
                zero_fill(0).wait()
                return carry
            lax.fori_loop(n_used, n_blocks, fill_wait, 0)


def _experts(block_e, n_used, x_sorted, wg, wu, wd, layer, n_blocks):
    wspec = lambda shape: pl.BlockSpec((None, None) + shape, lambda i, be, nu: (layer, be[i], 0, 0))
    rows = PK_BLK_ROWS
    grid_spec = pltpu.PrefetchScalarGridSpec(
        num_scalar_prefetch=2,
        grid=(n_blocks,),
        in_specs=[
            pl.BlockSpec(memory_space=pl.ANY),
            wspec((D_MODEL, EXPERT_DIM)), wspec((D_MODEL, EXPERT_DIM)), wspec((EXPERT_DIM, D_MODEL)),
        ],
        out_specs=pl.BlockSpec(memory_space=pl.ANY),
        scratch_shapes=[
            pltpu.VMEM((3, rows, LANES), U32),
            pltpu.VMEM((2, rows, LANES), U32),
            pltpu.VMEM((rows, LANES), U32),
            pltpu.VMEM((D_MODEL, EXPERT_DIM), BF16),
            pltpu.VMEM((D_MODEL, EXPERT_DIM), BF16),
            pltpu.VMEM((EXPERT_DIM, D_MODEL), BF16),
            pltpu.SemaphoreType.DMA((3,)),
            pltpu.SemaphoreType.DMA((2,)),
            pltpu.SemaphoreType.DMA((1,)),
        ],
    )
    return pl.pallas_call(
        functools.partial(_experts_kernel, n_blocks=n_blocks),
        grid_spec=grid_spec,
        out_shape=jax.ShapeDtypeStruct((n_blocks * rows, LANES), U32),
        compiler_params=pltpu.CompilerParams(dimension_semantics=("arbitrary",)),
        name="experts",
    )(block_e, n_used, x_sorted, wg, wu, wd)


def _combine_kernel(np_ref, lp_ref, ws_ref, cur_ref, nx1_ref, nx2_ref, y_hbm, h2b_ref, xn_ref,
                    mod_ref, g_ref, swg_ref, swu_ref, swd_ref, o_ref, stg, acc, sem, *, n_steps):
    i = pl.program_id(0)
    slot = i % 3
    d = D_MODEL

    def piece_copy(s, p, src):
        dst = pl.ds(pl.multiple_of(s * PK_STG_ROWS + p * PK_PIECE_ROWS, PK_PIECE_ROWS), PK_PIECE_ROWS)
        return pltpu.make_async_copy(y_hbm.at[_piece(src), :], stg.at[dst, :], sem.at[s])

    def fetcher(s, pd_ref):
        return lambda p, prio: piece_copy(s, p, pd_ref[0, p]).start(priority=prio)

    @pl.when(i == 0)
    def _():
        _start_pieces(0, np_ref[0], fetcher(0, cur_ref))
        _start_pieces(0, np_ref[1], fetcher(1, nx1_ref))

    _wait_pieces(np_ref[i], PK_PIECE_ROWS, lambda rows: pltpu.make_async_copy(
        y_hbm.at[pl.ds(0, rows), :], stg.at[pl.ds(0, rows), :], sem.at[slot]))

    def gather_token(t):
        a = None
        for k in range(TOP_K):
            j = k * ROW_BLK + t
            pair = stg[_tile_at(lp_ref[0, j]), :]
            w = jnp.full((TILE_ROWS, LANES), ws_ref[0, j], F32)
            v = jnp.where(w < 0.0, _unpack_hi(pair), _unpack_lo(pair)) * jnp.abs(w)
            a = v if a is None else a + v
        acc[_rows(t, TILE_ROWS), :] = a

    np_next = jnp.where(i + 2 < n_steps, np_ref[jnp.minimum(i + 2, n_steps - 1)], 0)
    _interleave(gather_token, np_next, fetcher((i + 2) % 3, nx2_ref))

    hb = h2b_ref[...]
    g = _dot(hb, swg_ref[...])
    u = _dot(hb, swu_ref[...])
    f = _dot((g * _sigmoid(g) * u).astype(BF16), swd_ref[...]) + _from_tiles(acc, ROW_BLK)
    o_ref[...] = xn_ref[...] + mod_ref[:, 5 * d:6 * d] * _rms(f, g_ref[...])


def _combine(npieces, pair_row, w_signed, piece_src, y_sorted, h2b_rows, xn_rows, mod3, g, swg, swu, swd,
             n_steps, per_batch):
    assert n_steps >= 3
    last = n_steps - 1
    full = lambda shape: pl.BlockSpec(shape, lambda n, *_: (0,) * len(shape))
    rows = lambda w_: pl.BlockSpec((ROW_BLK, w_), lambda n, *_: (n, 0))
    smem = lambda w_, imap: pl.BlockSpec((None, 1, w_), imap, memory_space=pltpu.SMEM)
    n_lat = SEQ // ROW_BLK
    grid_spec = pltpu.PrefetchScalarGridSpec(
        num_scalar_prefetch=1,
        grid=(n_steps,),
        in_specs=[
            smem(TOP_K * ROW_BLK, lambda n, *_: (n, 0, 0)),
            smem(TOP_K * ROW_BLK, lambda n, *_: (n, 0, 0)),
            smem(PIECES_MAX, lambda n, *_: (n, 0, 0)),
            smem(PIECES_MAX, lambda n, *_: (jnp.minimum(n + 1, last), 0, 0)),
            smem(PIECES_MAX, lambda n, *_: (jnp.minimum(n + 2, last), 0, 0)),
            pl.BlockSpec(memory_space=pl.ANY),
            rows(D_MODEL), rows(D_MODEL),
            pl.BlockSpec((None, 1, 6 * D_MODEL),
                         lambda n, *_: (_mod_row(n % per_batch, n // per_batch, n_lat), 0, 0)),
            full((1, D_MODEL)),
            full((D_MODEL, SHARED_DIM)), full((D_MODEL, SHARED_DIM)), full((SHARED_DIM, D_MODEL)),
        ],
        out_specs=rows(D_MODEL),
        scratch_shapes=[
            pltpu.VMEM((3 * PK_STG_ROWS, LANES), U32),
            pltpu.VMEM((ROW_BLK * TILE_ROWS, LANES), F32),
            pltpu.SemaphoreType.DMA((3,)),
        ],
    )
    return pl.pallas_call(
        functools.partial(_combine_kernel, n_steps=n_steps),
        grid_spec=grid_spec,
        out_shape=jax.ShapeDtypeStruct(xn_rows.shape, F32),
        compiler_params=pltpu.CompilerParams(dimension_semantics=("arbitrary",),
                                             vmem_limit_bytes=VMEM_LIMIT),
        name="combine",
    )(npieces, pair_row, w_signed, piece_src, piece_src, piece_src, y_sorted, h2b_rows, xn_rows, mod3, g,
      swg, swu, swd)


def _moe_plan(cnt_rows, n_blocks):
    i32 = jnp.int32
    per_blk = MOE_BLK // PIECE
    to_expert = lambda v: v.reshape(v.shape[:-1] + (8, 8)).swapaxes(-1, -2).reshape(v.shape)
    runs = (cnt_rows + PIECE - 1) // PIECE
    loc_end = jnp.cumsum(runs, axis=1)
    loc = loc_end - runs
    npieces = loc_end[:, -1].astype(i32)
    reg = to_expert(runs.sum(axis=0))
    padded = (reg + per_blk - 1) // per_blk * per_blk
    pad_end = jnp.cumsum(padded)
    pad_start = pad_end - padded
    off = to_expert(pad_start)[None, :] + jnp.cumsum(runs, axis=0) - runs
    p = jnp.arange(PIECES_MAX, dtype=i32)[None, :, None]
    mine = (loc[:, None, :] <= p) & (p < loc_end[:, None, :])
    piece_dst = jnp.sum(jnp.where(mine, (off - loc)[:, None, :], 0), axis=-1) + p[:, :, 0]
    tail_n = padded - reg
    t_end = jnp.cumsum(tail_n)
    t_beg = t_end - tail_n
    q = jnp.arange(TAIL_MAX, dtype=i32)[:, None]
    tmine = (t_beg[None, :] <= q) & (q < t_end[None, :])
    tail_dst = jnp.sum(jnp.where(tmine, (pad_start + reg - t_beg)[None, :], 0), axis=-1) + q[:, 0]
    first = jnp.arange(n_blocks, dtype=i32)[:, None] * per_blk
    block_e = jnp.minimum(jnp.sum((pad_end[None, :] <= first).astype(i32), axis=-1), N_EXPERTS - 1)
    n_used = (pad_end[-1] // per_blk).astype(i32).reshape(1)
    return (npieces, piece_dst.astype(i32).reshape(-1, 1, PIECES_MAX),
            tail_dst.astype(i32).reshape(1, 1, TAIL_MAX), t_end[-1].astype(i32).reshape(1),
            block_e.astype(i32), n_used)


def _axial_tables(dim):
    f32 = np.float32
    t = np.arange(SEQ)
    row = (t // GRID_W).astype(f32)
    col = (t % GRID_W).astype(f32)
    quarter = dim // 4
    freqs = (f32(ROPE_THETA) ** (-np.arange(quarter, dtype=f32) / f32(quarter))).astype(f32)
    ar, ac = row[:, None] * freqs, col[:, None] * freqs
    ang = np.concatenate([ar, ar, ac, ac], axis=-1)
    sign = np.where((np.arange(dim) // quarter) % 2 == 0, -1.0, 1.0).astype(f32)
    cos = np.concatenate([np.cos(ang), np.ones((CTX_LEN, dim), f32)], axis=0)
    sin = np.concatenate([np.sin(ang) * sign, np.zeros((CTX_LEN, dim), f32)], axis=0)
    return cos.astype(f32), sin.astype(f32)


def _rope_tables():
    f32 = np.float32
    cos_h, sin_h = _axial_tables(HEAD_DIM)
    cos_r, sin_r = _axial_tables(B_ROPE)
    cosh, sinh = np.tile(cos_h, (1, 4)), np.tile(sin_h, (1, 4))
    scale_b = f32((B_NOPE + B_ROPE) ** -0.5)
    ones, zeros = np.ones((TOK, 64), f32), np.zeros((TOK, 64), f32)
    pad1, pad0 = np.ones((TOK, 32), f32), np.zeros((TOK, 32), f32)
    cosb = np.tile(np.concatenate([ones, cos_r, pad1], axis=1) * scale_b, (1, 4))
    sinb = np.tile(np.concatenate([zeros, sin_r, pad0], axis=1) * scale_b, (1, 4))
    cosr = np.concatenate([cos_r, np.ones((TOK, 96), f32)], axis=1)
    sinr = np.concatenate([sin_r, np.zeros((TOK, 96), f32)], axis=1)
    return tuple(jnp.asarray(a, F32) for a in (cosh, sinh, cosb, sinb, cosr, sinr))


_GQA_ORDER = (0, 2, 1, 3)


def _take_heads(w, base, order, axis):
    return jnp.concatenate(
        [lax.slice_in_dim(w, base + h * HEAD_DIM, base + (h + 1) * HEAD_DIM, axis=axis) for h in order],
        axis=axis)


def _w_in_layout(w_in):
    zeros = lambda n: jnp.zeros((D_MODEL, n), w_in.dtype)
    a_q = _take_heads(w_in, 0, _GQA_ORDER, 1)
    a_kv = w_in[:, 256:512]
    b_q = jnp.concatenate(
        [jnp.concatenate([w_in[:, 512 + h * 96:512 + (h + 1) * 96], zeros(32)], axis=1)
         for h in range(B_HEADS)], axis=1)
    b_c = w_in[:, 896:1024]
    b_r = jnp.concatenate([w_in[:, 1024:1056], zeros(96)], axis=1)
    c_q = _take_heads(w_in, 1056, _GQA_ORDER, 1)
    rest = w_in[:, 1312:2336]
    return jnp.concatenate([a_q, a_kv, b_q, b_c, b_r, c_q, rest], axis=1).astype(BF16)


def _w_kv_layout(w_kv_up):
    zeros = jnp.zeros((B_KV_RANK, 64), w_kv_up.dtype)
    nope = [jnp.concatenate([w_kv_up[:, h * 128:h * 128 + 64], zeros], axis=1) for h in range(B_HEADS)]
    val = [w_kv_up[:, h * 128 + 64:(h + 1) * 128] for h in range(B_HEADS)]
    return jnp.concatenate(nope + val, axis=1).astype(BF16)


def _w_out_layout(w_out):
    return jnp.concatenate([_take_heads(w_out, 0, _GQA_ORDER, 0), w_out[256:512],
                            _take_heads(w_out, 512, _GQA_ORDER, 0), w_out[768:1024]], axis=0).astype(BF16)


def _router_rows(v):
    return v.reshape((8, 8) + v.shape[1:]).swapaxes(0, 1).reshape(v.shape)


def kernel(x, c, ctx, c_ctx, ada_w, ada_b, pre_mix_g, post_mix_g, pre_ffn_g, post_ffn_g, w_in, a_q_norm, a_k_norm, b_kv_norm, b_w_kv_up, c_sink, d_rpb, w_out, router_w, router_b, exp_w_gate, exp_w_up, exp_w_down, sh_w_gate, sh_w_up, sh_w_down):
    bsz = x.shape[0]
    depth = ada_w.shape[0]
    xall = jnp.concatenate([x, ctx], axis=1)

    cvec = jnp.zeros((16, D_MODEL), F32).at[:bsz].set(c).at[8].set(c_ctx)
    mod = _ada(cvec, ada_w, ada_b)
    tabs = _rope_tables()
    avg = jnp.asarray(np.kron(np.eye(4, dtype=np.float32), np.full((64, 64), 1.0 / 64, np.float32)), BF16)
    row2 = lambda v: v.reshape(1, -1).astype(F32)

    for l in range(depth):
        with_ctx = l < depth - 1
        mod3 = mod[l].reshape(16, 1, 6 * D_MODEL)

        proj = _inproj(
            xall, mod3, row2(pre_mix_g[l]), _w_in_layout(w_in[l]),
            row2(jnp.tile(a_q_norm[l], 4) * HEAD_DIM ** -0.5), row2(jnp.tile(a_k_norm[l], 2)),
            row2(b_kv_norm[l]), _w_kv_layout(b_w_kv_up[l]), avg, tabs)
        qa, ka, va, qb, kb, vb, qc, kc, vc, qd, kd, vd = proj

        nq = NQ + (CTX_LEN // QBLK if with_ctx else 0)
        full = lambda shape: pl.BlockSpec(shape, lambda b, i: (0,) * len(shape))
        mix_a = _attn_call(_attn_a_kernel, "attn_a", qa, ka, va, [], [], nq // 2, qblk=2 * QBLK)
        mix_b = _attn_call(_attn_b_kernel, "attn_b", qb, kb, vb, [], [], nq // 2, qblk=2 * QBLK)
        sink = jnp.zeros((8, LANES), F32).at[:4].set(
            jnp.broadcast_to(c_sink[l][np.array(_GQA_ORDER)][:, None], (4, LANES)))
        mix_c = _attn_call(_attn_c_kernel, "attn_c", qc, kc, vc, [sink], [full((8, LANES))], nq)
        bias = _na_bias(d_rpb[l])
        mix_d = _attn_call(
            _attn_d_kernel, "attn_d", qd, kd, vd, [bias],
            [pl.BlockSpec((None, D_HEADS, QBLK, NA_WIN), lambda b, i: (_na_pattern(i), 0, 0, 0))], nq)

        nb = (TOK if with_ctx else SEQ) // ROW_BLK
        rb = jnp.broadcast_to(_router_rows(router_b[l])[:, None], (N_EXPERTS, ROW_BLK)).astype(F32)
        xn, h2b, h2t, wt_t, ls_t, cnt_t = _outproj(
            (mix_a, mix_b, mix_c, mix_d), _w_out_layout(w_out[l]), xall, mod3,
            row2(post_mix_g[l]), row2(pre_ffn_g[l]),
            _router_rows(router_w[l].T).astype(BF16), rb, nb)

        t_used = nb * ROW_BLK
        n_tok = bsz * t_used
        n_steps = bsz * nb
        n_blocks = -(-(n_tok * TOP_K + n_steps * N_EXPERTS * (PIECE - 1) + N_EXPERTS * (MOE_BLK - 1))
                     // MOE_BLK)
        npieces, piece_dst, tail_dst, ntail, block_e, n_used = _moe_plan(
            cnt_t[..., 0].reshape(n_steps, N_EXPERTS), n_blocks)
        lslot = ls_t.reshape(n_steps, 1, TOP_K * ROW_BLK)
        wts = wt_t.reshape(n_steps, 1, TOP_K * ROW_BLK)
        stg_row = lslot * TILE_ROWS
        ring = (jnp.arange(n_steps, dtype=jnp.int32) % 3)[:, None, None] * PK_STG_ROWS
        pair_row = (lslot // 2) * TILE_ROWS + ring
        w_signed = jnp.where(lslot % 2 == 0, wts, -wts)

        x_sorted = _dispatch(npieces, n_used, ntail, stg_row, piece_dst, tail_dst,
                             h2t.reshape(n_tok * TILE_ROWS, LANES), n_steps, n_blocks)
        y_sorted = _experts(block_e, n_used, x_sorted, exp_w_gate, exp_w_up, exp_w_down, l, n_blocks)
        xall = _combine(
            npieces, pair_row, w_signed, piece_dst, y_sorted, h2b.reshape(n_tok, D_MODEL),
            xn.reshape(n_tok, D_MODEL), mod3, row2(post_ffn_g[l]),
            sh_w_gate[l].astype(BF16), sh_w_up[l].astype(BF16), sh_w_down[l].astype(BF16),
            n_steps, nb).reshape(bsz, t_used, D_MODEL)

    return xall[:, :SEQ]
```

```python
import functools

import numpy as np
import jax
import jax.numpy as jnp
from jax import lax
from jax.experimental import pallas as pl
from jax.experimental.pallas import tpu as pltpu

F32 = jnp.float32
BF16 = jnp.bfloat16
U32 = jnp.uint32

D_MODEL = 1024
SEQ = 2048
CTX_LEN = 256
TOK = SEQ + CTX_LEN
GRID_W = 64
HEAD_DIM = 64
ROPE_THETA = 10000.0
EPS = 1e-6
NEG = -1e30

A_HEADS, A_KV_HEADS = 4, 2
B_HEADS, B_NOPE, B_ROPE, B_V, B_KV_RANK = 4, 64, 32, 64, 128
C_HEADS, C_KV_HEADS, C_WINDOW = 4, 2, 128
D_HEADS, NA_ROWS, NA_COLS = 4, 8, 16

N_EXPERTS, TOP_K, N_GROUPS, TOPK_GROUPS = 64, 8, 8, 4
EXPERT_DIM, SHARED_DIM = 256, 256
ROUTED_SCALE = 2.5

LANES = 128
QBLK = 128
ROW_BLK = 256
IN_BLK = 768
MOE_BLK = 512
TILE_ROWS = D_MODEL // LANES
PIECE = 8
PIECE_ROWS = PIECE * TILE_ROWS
PIECES_MAX = 320
STG_ROWS = PIECES_MAX * PIECE_ROWS
PK_PIECE_ROWS = PIECE_ROWS // 2
PK_STG_ROWS = STG_ROWS // 2
PK_BLK_ROWS = MOE_BLK * TILE_ROWS // 2
TAIL_MAX = N_EXPERTS * MOE_BLK // PIECE
LOG_PACK_GROUP = 2
PACK_GROUP = 1 << LOG_PACK_GROUP
LOG_WAIT_GROUP = 4
WAIT_GROUP = 1 << LOG_WAIT_GROUP
LOG_CHUNKS = 2
N_CHUNKS = 1 << LOG_CHUNKS
VMEM_LIMIT = 56 * 1024 * 1024
NQ = SEQ // QBLK
NA_WIN = 10 * GRID_W

_AQ, _AK, _AV, _BQ, _BC, _BR = 0, 256, 384, 512, 1024, 1152
_CQ, _CK, _CV, _DQ, _DK, _DV = 1280, 1536, 1664, 1792, 2048, 2304
IN_W = 2560

_NT = (((1,), (1,)), ((), ()))


def _sigmoid(x):
    return 1.0 / (1.0 + jnp.exp(-x))


def _rms(x, g):
    return x * lax.rsqrt(jnp.mean(x * x, axis=-1, keepdims=True) + EPS) * g


def _dot(a, b):
    return jnp.dot(a, b, preferred_element_type=F32)


def _dot_nt(a, b):
    return lax.dot_general(a, b, _NT, preferred_element_type=F32)


def _ada_kernel(c_ref, w_ref, b_ref, o_ref):
    c = c_ref[...]
    a = (c * _sigmoid(c)).astype(BF16)
    o_ref[...] = _dot(a, w_ref[...].astype(BF16)) + b_ref[...]


def _ada(cvec, ada_w, ada_b):
    depth = ada_w.shape[0]
    n = ada_w.shape[2]
    tn = 1536
    return pl.pallas_call(
        _ada_kernel,
        grid=(depth, n // tn),
        in_specs=[
            pl.BlockSpec((16, D_MODEL), lambda l, j: (0, 0)),
            pl.BlockSpec((None, D_MODEL, tn), lambda l, j: (l, 0, j)),
            pl.BlockSpec((None, 1, tn), lambda l, j: (l, 0, j)),
        ],
        out_specs=pl.BlockSpec((None, 16, tn), lambda l, j: (l, 0, j)),
        out_shape=jax.ShapeDtypeStruct((depth, 16, n), F32),
        name="ada",
    )(cvec, ada_w, ada_b.reshape(depth, 1, n))


def _rope(x, cos, sin, chunk):
    outs = []
    for c in range(x.shape[1] // LANES):
        sl = slice(c * LANES, (c + 1) * LANES)
        xs = x[:, sl]
        lane = lax.broadcasted_iota(jnp.int32, xs.shape, 1)
        first = (lane & chunk) == 0
        rot = jnp.where(first, pltpu.roll(xs, LANES - chunk, 1), pltpu.roll(xs, chunk, 1))
        outs.append(xs * cos[:, sl] + rot * sin[:, sl])
    return outs[0] if len(outs) == 1 else jnp.concatenate(outs, axis=1)


def _head_norm(x, avg):
    sq = x * x
    hi = sq.astype(BF16)
    lo = (sq - hi.astype(F32)).astype(BF16)
    ms = _dot(hi, avg) + _dot(lo, avg)
    return x * lax.rsqrt(ms + EPS)


def _inproj_kernel(x_ref, mod_ref, modc_ref, g_ref, w_ref, qg_ref, kg_ref, kvg_ref, wkv_ref, avg_ref,
                   cosh_ref, sinh_ref, cosb_ref, sinb_ref, cosr_ref, sinr_ref,
                   qa_ref, ka_ref, va_ref, qb_ref, kb_ref, vb_ref,
                   qc_ref, kc_ref, vc_ref, qd_ref, kd_ref, vd_ref):
    d = D_MODEL
    n = x_ref.shape[0]
    is_ctx = pl.program_id(1) * n + lax.broadcasted_iota(jnp.int32, (n, 1), 0) >= SEQ
    scale = jnp.where(is_ctx, modc_ref[:, d:2 * d], mod_ref[:, d:2 * d])
    shift = jnp.where(is_ctx, modc_ref[:, 0:d], mod_ref[:, 0:d])
    h = _rms(x_ref[...], g_ref[...]) * (1.0 + scale) + shift
    p = _dot(h.astype(BF16), w_ref[...])
    cosh, sinh = cosh_ref[...], sinh_ref[...]
    avg = avg_ref[...]

    qa = _head_norm(p[:, _AQ:_AQ + 256], avg) * qg_ref[...]
    qa_ref[...] = _rope(qa, cosh, sinh, 16).astype(BF16)
    ka = _head_norm(p[:, _AK:_AK + 128], avg[0:128, 0:128]) * kg_ref[...]
    ka_ref[...] = _rope(ka, cosh[:, 0:128], sinh[:, 0:128], 16).astype(BF16)
    ones = jnp.ones((p.shape[0], LANES), BF16)
    va_ref[...] = jnp.concatenate([p[:, _AV:_AV + 128].astype(BF16), ones], axis=1)

    qb_ref[...] = _rope(p[:, _BQ:_BQ + 512], cosb_ref[...], sinb_ref[...], 8).astype(BF16)
    cn = _rms(p[:, _BC:_BC + 128], kvg_ref[...])
    kv = _dot(cn.astype(BF16), wkv_ref[...])
    kr = _rope(p[:, _BR:_BR + 128], cosr_ref[...], sinr_ref[...], 8)
    kr = pltpu.roll(kr, 64, 1)
    kb_ref[...] = jnp.concatenate(
        [kv[:, hh * 128:(hh + 1) * 128] + kr for hh in range(B_HEADS)], axis=1).astype(BF16)
    vb_ref[...] = jnp.concatenate(
        [kv[:, 512:640].astype(BF16), ones, kv[:, 640:768].astype(BF16), ones], axis=1)

    qc_ref[...] = _rope(p[:, _CQ:_CQ + 256] * 0.125, cosh, sinh, 16).astype(BF16)
    kc_ref[...] = _rope(p[:, _CK:_CK + 128], cosh[:, 0:128], sinh[:, 0:128], 16).astype(BF16)
    vc_ref[...] = jnp.concatenate([p[:, _CV:_CV + 128].astype(BF16), ones], axis=1)

    qd_ref[...] = (p[:, _DQ:_DQ + 256] * 0.125).astype(BF16)
    kd_ref[...] = p[:, _DK:_DK + 256].astype(BF16)
    vd_ref[...] = jnp.concatenate(
        [p[:, _DV:_DV + 128].astype(BF16), ones, p[:, _DV + 128:_DV + 256].astype(BF16), ones], axis=1)


def _mod_row(j, b, n_lat_blocks):
    return jnp.where(j < n_lat_blocks, b, 8)


def _inproj(xall, mod3, g, w, qg, kg, kvg, wkv, avg, tabs):
    bsz = xall.shape[0]
    nb = TOK // IN_BLK
    full = lambda shape: pl.BlockSpec(shape, lambda b, j: (0,) * len(shape))
    row = lambda w_: pl.BlockSpec((IN_BLK, w_), lambda b, j: (j, 0))
    out = lambda w_: pl.BlockSpec((None, IN_BLK, w_), lambda b, j: (b, j, 0))
    widths = (256, 128, 256, 512, 512, 512, 256, 128, 256, 256, 256, 512)
    return pl.pallas_call(
        _inproj_kernel,
        grid=(bsz, nb),
        in_specs=[
            pl.BlockSpec((None, IN_BLK, D_MODEL), lambda b, j: (b, j, 0)),
            pl.BlockSpec((None, 1, 6 * D_MODEL), lambda b, j: (b, 0, 0)),
            pl.BlockSpec((None, 1, 6 * D_MODEL), lambda b, j: (8, 0, 0)),
            full((1, D_MODEL)), full((D_MODEL, IN_W)),
            full((1, 256)), full((1, 128)), full((1, 128)), full((128, 768)), full((256, 256)),
            row(256), row(256), row(512), row(512), row(128), row(128),
        ],
        out_specs=[out(w_) for w_ in widths],
        out_shape=[jax.ShapeDtypeStruct((bsz, TOK, w_), BF16) for w_ in widths],
        compiler_params=pltpu.CompilerParams(vmem_limit_bytes=VMEM_LIMIT),
        name="inproj",
    )(xall, mod3, mod3, g, w, qg, kg, kvg, wkv, avg, *tabs)


def _softmax_pv(segs, sink=None):
    m = None
    for s, _ in segs:
        ms = jnp.max(s, axis=-1, keepdims=True)
        m = ms if m is None else jnp.maximum(m, ms)
    if sink is not None:
        m = jnp.maximum(m, sink)
    acc = None
    for s, v1 in segs:
        r = _dot(jnp.exp((s - m).astype(BF16)), v1)
        acc = r if acc is None else acc + r
    l = acc[:, LANES:LANES + 1]
    if sink is not None:
        l = l + jnp.exp(sink - m)
    return acc[:, 0:LANES] / l


def _lane_lo(dtype_shape):
    return lax.broadcasted_iota(jnp.int32, dtype_shape, 1) < HEAD_DIM


def _half(qt, hh):
    lo = _lane_lo(qt.shape)
    return jnp.where(lo if hh == 0 else jnp.logical_not(lo), qt, jnp.zeros_like(qt))


def _merge_halves(o0, o1):
    return jnp.where(_lane_lo(o0.shape), o0, o1)


def _stack_heads(q_ref, tiles):
    return jnp.concatenate(
        [_half(q_ref[:, t * LANES:(t + 1) * LANES], hh) for t in tiles for hh in range(2)], axis=0)


def _store_heads(o_ref, o, tiles):
    n = o_ref.shape[0]
    for ti, t in enumerate(tiles):
        o0, o1 = o[2 * ti * n:(2 * ti + 1) * n], o[(2 * ti + 1) * n:(2 * ti + 2) * n]
        o_ref[:, t * LANES:(t + 1) * LANES] = _merge_halves(o0, o1).astype(o_ref.dtype)


def _attn_a_kernel(q_ref, k_ref, v_ref, o_ref, *, n_lat):
    def run(k, v1):
        for t in range(2):
            o = [_softmax_pv([(_dot_nt(_half(q_ref[:, t * LANES:(t + 1) * LANES], hh), k), v1)])
                 for hh in range(2)]
            o_ref[:, t * LANES:(t + 1) * LANES] = _merge_halves(o[0], o[1]).astype(o_ref.dtype)

    @pl.when(pl.program_id(1) < n_lat)
    def _():
        run(k_ref[...], v_ref[...])

    @pl.when(pl.program_id(1) >= n_lat)
    def _():
        run(k_ref[SEQ:TOK, :], v_ref[SEQ:TOK, :])


def _attn_b_kernel(q_ref, k_ref, v_ref, o_ref, *, n_lat):
    def run(lo_, hi_):
        for t in range(2):
            v1 = v_ref[lo_:hi_, t * 256:(t + 1) * 256]
            o = []
            for hh in range(2):
                hd = 2 * t + hh
                q = q_ref[:, hd * LANES:(hd + 1) * LANES]
                k = k_ref[lo_:hi_, hd * LANES:(hd + 1) * LANES]
                o.append(_softmax_pv([(_dot_nt(q, k), v1)]))
            o_ref[:, t * LANES:(t + 1) * LANES] = _merge_halves(o[0], o[1]).astype(o_ref.dtype)

    @pl.when(pl.program_id(1) < n_lat)
    def _():
        run(0, TOK)

    @pl.when(pl.program_id(1) >= n_lat)
    def _():
        run(SEQ, TOK)


def _attn_c_kernel(q_ref, k_ref, v_ref, sink_ref, o_ref, *, n_lat):
    i = pl.program_id(1)
    qblk = o_ref.shape[0]
    band = 3 * qblk

    def run(latent):
        if latent:
            start = pl.multiple_of(jnp.clip((i - 1) * qblk, 0, SEQ - band), qblk)
            shape = (2 * qblk, band)
            qpos = i * qblk + (lax.broadcasted_iota(jnp.int32, shape, 0) & (qblk - 1))
            kpos = start + lax.broadcasted_iota(jnp.int32, shape, 1)
            valid = jnp.abs(qpos - kpos) <= C_WINDOW
        for t in range(2):
            q = _stack_heads(q_ref, (t,))
            segs = [(_dot_nt(q, k_ref[SEQ:TOK, :]), v_ref[SEQ:TOK, :])]
            if latent:
                s = jnp.where(valid, _dot_nt(q, k_ref[pl.ds(start, band), :]), NEG)
                segs.insert(0, (s, v_ref[pl.ds(start, band), :]))
            sink = jnp.concatenate(
                [jnp.broadcast_to(sink_ref[j:j + 1, 0:1], (qblk, 1)) for j in (2 * t, 2 * t + 1)], axis=0)
            _store_heads(o_ref, _softmax_pv(segs, sink=sink), (t,))

    @pl.when(i < n_lat)
    def _():
        run(True)

    @pl.when(i >= n_lat)
    def _():
        run(False)


def _attn_d_kernel(q_ref, k_ref, v_ref, bias_ref, o_ref, *, n_lat):
    i = pl.program_id(1)
    qblk = o_ref.shape[0]

    def run(latent):
        if latent:
            start = pl.multiple_of(jnp.clip((i - 2) * qblk, 0, SEQ - NA_WIN), qblk)
        for t in range(2):
            ksl, vsl = slice(t * LANES, (t + 1) * LANES), slice(t * 256, (t + 1) * 256)
            q = _stack_heads(q_ref, (t,))
            segs = [(_dot_nt(q, k_ref[SEQ:TOK, ksl]), v_ref[SEQ:TOK, vsl])]
            if latent:
                bias = bias_ref[2 * t:2 * t + 2].reshape(2 * qblk, NA_WIN)
                segs.insert(0, (_dot_nt(q, k_ref[pl.ds(start, NA_WIN), ksl]) + bias,
                                v_ref[pl.ds(start, NA_WIN), vsl]))
            _store_heads(o_ref, _softmax_pv(segs), (t,))

    @pl.when(i < n_lat)
    def _():
        run(True)

    @pl.when(i >= n_lat)
    def _():
        run(False)


def _attn_call(kernel, name, q, k, v, extra, extra_specs, n_blocks, qblk=QBLK):
    bsz = q.shape[0]
    return pl.pallas_call(
        functools.partial(kernel, n_lat=SEQ // qblk),
        grid=(bsz, n_blocks),
        in_specs=[
            pl.BlockSpec((None, qblk, q.shape[2]), lambda b, i: (b, i, 0)),
            pl.BlockSpec((None, TOK, k.shape[2]), lambda b, i: (b, 0, 0)),
            pl.BlockSpec((None, TOK, v.shape[2]), lambda b, i: (b, 0, 0)),
        ] + extra_specs,
        out_specs=pl.BlockSpec((None, qblk, 256), lambda b, i: (b, i, 0)),
        out_shape=jax.ShapeDtypeStruct((bsz, n_blocks * qblk, 256), BF16),
        name=name,
    )(q, k, v, *extra)


def _na_pattern(i):
    return jnp.where(i < 2, i, jnp.where(i < NQ - 2, 2, jnp.minimum(i, NQ - 1) - (NQ - 5)))


def _na_bias(rpb):
    rows = SEQ // GRID_W
    n_dr, n_dc = 2 * NA_ROWS - 1, 2 * NA_COLS - 1
    qc = np.arange(GRID_W)
    dc = np.clip(qc[None, :] - qc[:, None] + NA_COLS - 1, 0, n_dc - 1)
    sel_c = jnp.asarray(np.eye(n_dc, dtype=np.float32)[dc])
    cs = np.clip(qc - NA_COLS // 2, 0, GRID_W - NA_COLS)[:, None]
    col_ok = (qc[None, :] >= cs) & (qc[None, :] < cs + NA_COLS)
    sel_r, row_ok = [], []
    for i in (0, 1, 2, NQ - 2, NQ - 1):
        start_row = min(max(2 * i - 4, 0), rows - 10)
        qr = 2 * i + np.arange(2)
        kr = start_row + np.arange(10)
        rs = np.clip(qr - NA_ROWS // 2, 0, rows - NA_ROWS)[:, None]
        row_ok.append((kr[None] >= rs) & (kr[None] < rs + NA_ROWS))
        dr = np.clip(kr[None] - qr[:, None] + NA_ROWS - 1, 0, n_dr - 1)
        sel_r.append(np.eye(n_dr, dtype=np.float32)[dr])
    b = jnp.einsum("hrc,pabr,qkc->phaqbk", rpb.astype(F32), jnp.asarray(np.stack(sel_r)), sel_c,
                   precision=lax.Precision.HIGHEST)
    valid = np.stack(row_ok)[:, None, :, None, :, None] & col_ok[None, None, None, :, None, :]
    return jnp.where(valid, b, NEG).reshape(5, D_HEADS, QBLK, NA_WIN)


def _outproj_kernel(ma_ref, mb_ref, mc_ref, md_ref, w_ref, x_ref, mod_ref, g1_ref, g2_ref,
                    wr_ref, rb_ref, xn_ref, h2b_ref, h2t_ref, wt_ref, ls_ref, cnt_ref):
    d = D_MODEL
    mix = jnp.concatenate([ma_ref[...], mb_ref[...], mc_ref[...], md_ref[...]], axis=1)
    y = _rms(_dot(mix, w_ref[...]), g1_ref[...])
    xn = x_ref[...] + mod_ref[:, 2 * d:3 * d] * y
    xn_ref[...] = xn
    h2 = _rms(xn, g2_ref[...]) * (1.0 + mod_ref[:, 4 * d:5 * d]) + mod_ref[:, 3 * d:4 * d]
    n = h2.shape[0]
    h2b_ref[...] = h2.astype(BF16)
    for s in range(TILE_ROWS):
        h2t_ref[pl.ds(s, n, stride=TILE_ROWS), :] = h2[:, s * LANES:(s + 1) * LANES]

    s = _sigmoid(_dot_nt(wr_ref[...], h2.astype(BF16)))
    sb = s + rb_ref[...]
    per = N_EXPERTS // N_GROUPS
    slab = [sb[j * 8:(j + 1) * 8, :] for j in range(per)]
    sraw = [s[j * 8:(j + 1) * 8, :] for j in range(per)]
    giota = lax.broadcasted_iota(jnp.int32, (N_GROUPS, n), 0)
    eid = [giota * per + j for j in range(per)]
    ninf = -jnp.inf

    m1 = functools.reduce(jnp.maximum, slab)
    j1 = functools.reduce(jnp.minimum, [jnp.where(slab[j] == m1, j, per) for j in range(per)])
    m2 = functools.reduce(jnp.maximum, [jnp.where(j1 == j, ninf, slab[j]) for j in range(per)])
    gs = m1 + m2

    gsel = jnp.zeros((N_GROUPS, n), jnp.bool_)
    for _ in range(TOPK_GROUPS):
        m = jnp.max(gs, axis=0, keepdims=True)
        gi = jnp.min(jnp.where(gs == m, giota, N_GROUPS), axis=0, keepdims=True)
        hit = giota == gi
        gsel = jnp.logical_or(gsel, hit)
        gs = jnp.where(hit, ninf, gs)
    slab = [jnp.where(gsel, sj, ninf) for sj in slab]

    def pick(hits, vals):
        return jnp.sum(functools.reduce(jnp.add, [jnp.where(hits[j], vals[j], 0.0) for j in range(per)]),
                       axis=0, keepdims=True)

    kiota = lax.broadcasted_iota(jnp.int32, (TOP_K, n), 0)
    wts = jnp.zeros((TOP_K, n), F32)
    chosen = []
    for k in range(TOP_K):
        m = jnp.max(functools.reduce(jnp.maximum, slab), axis=0, keepdims=True)
        cand = functools.reduce(jnp.minimum,
                                [jnp.where(slab[j] == m, eid[j], N_EXPERTS) for j in range(per)])
        ei = jnp.min(cand, axis=0, keepdims=True)
        hits = [eid[j] == ei for j in range(per)]
        slab = [jnp.where(hits[j], ninf, slab[j]) for j in range(per)]
        wts = jnp.where(kiota == k, pick(hits, sraw), wts)
        chosen.append(ei)
    wt_ref[...] = wts / jnp.sum(wts, axis=0, keepdims=True) * ROUTED_SCALE

    hits_k = [[eid[j] == chosen[k] for j in range(per)] for k in range(TOP_K)]
    sel = jnp.concatenate(
        [functools.reduce(jnp.add, [jnp.where(hits_k[k][j], 1.0, 0.0) for k in range(TOP_K)])
         for j in range(per)], axis=0)
    tri = jnp.where(lax.broadcasted_iota(jnp.int32, (n, n), 0) < lax.broadcasted_iota(jnp.int32, (n, n), 1),
                    1.0, 0.0).astype(BF16)
    before = _dot(sel.astype(BF16), tri)
    cnt = jnp.sum(sel, axis=1, keepdims=True)
    run = jnp.floor((cnt + (PIECE - 1.0)) * (1.0 / PIECE)) * PIECE
    low = jnp.where(lax.broadcasted_iota(jnp.int32, (N_EXPERTS, N_EXPERTS), 1)
                    < lax.broadcasted_iota(jnp.int32, (N_EXPERTS, N_EXPERTS), 0), 1.0, 0.0).astype(BF16)
    start = _dot(low, jnp.broadcast_to(run, (N_EXPERTS, LANES)).astype(BF16))[:, 0:1]
    base = before + start
    base = [base[j * 8:(j + 1) * 8, :] for j in range(per)]
    lsl = jnp.zeros((TOP_K, n), F32)
    for k in range(TOP_K):
        lsl = jnp.where(kiota == k, pick(hits_k[k], base), lsl)
    ls_ref[...] = lsl.astype(jnp.int32)
    cnt_ref[...] = jnp.broadcast_to(cnt, (N_EXPERTS, LANES)).astype(jnp.int32)


def _outproj(mixes, w, xall, mod3, g1, g2, wr, rb, nb):
    bsz = xall.shape[0]
    n_lat = SEQ // ROW_BLK
    full = lambda shape: pl.BlockSpec(shape, lambda b, j: (0,) * len(shape))
    rows = lambda w_: pl.BlockSpec((None, ROW_BLK, w_), lambda b, j: (b, j, 0))
    route = pl.BlockSpec((None, None, TOP_K, ROW_BLK), lambda b, j: (b, j, 0, 0))
    return pl.pallas_call(
        _outproj_kernel,
        grid=(bsz, nb),
        in_specs=[rows(256)] * 4 + [
            full((D_MODEL, D_MODEL)), rows(D_MODEL),
            pl.BlockSpec((None, 1, 6 * D_MODEL), lambda b, j: (_mod_row(j, b, n_lat), 0, 0)),
            full((1, D_MODEL)), full((1, D_MODEL)),
            full((N_EXPERTS, D_MODEL)), full((N_EXPERTS, ROW_BLK)),
        ],
        out_specs=[
            rows(D_MODEL), rows(D_MODEL),
            pl.BlockSpec((None, ROW_BLK * TILE_ROWS, LANES), lambda b, j: (b, j, 0)),
            route, route,
            pl.BlockSpec((None, None, N_EXPERTS, LANES), lambda b, j: (b, j, 0, 0)),
        ],
        out_shape=[
            jax.ShapeDtypeStruct((bsz, nb * ROW_BLK, D_MODEL), F32),
            jax.ShapeDtypeStruct((bsz, nb * ROW_BLK, D_MODEL), BF16),
            jax.ShapeDtypeStruct((bsz, nb * ROW_BLK * TILE_ROWS, LANES), F32),
            jax.ShapeDtypeStruct((bsz, nb, TOP_K, ROW_BLK), F32),
            jax.ShapeDtypeStruct((bsz, nb, TOP_K, ROW_BLK), jnp.int32),
            jax.ShapeDtypeStruct((bsz, nb, N_EXPERTS, LANES), jnp.int32),
        ],
        name="outproj",
    )(*mixes, w, xall, mod3, g1, g2, wr, rb)


def _rows(i, n):
    return pl.ds(pl.multiple_of(i * n, n), n)


def _tile_at(row):
    return pl.ds(pl.multiple_of(row, TILE_ROWS), TILE_ROWS)


def _piece(p):
    return _rows(p, PK_PIECE_ROWS)


def _pack_pairs(lo, hi):
    return (lax.shift_right_logical(pltpu.bitcast(lo, U32), jnp.uint32(16))
            | (pltpu.bitcast(hi, U32) & jnp.uint32(0xFFFF0000)))


def _unpack_lo(u):
    return pltpu.bitcast(lax.shift_left(u, jnp.uint32(16)), F32)


def _unpack_hi(u):
    return pltpu.bitcast(u & jnp.uint32(0xFFFF0000), F32)


def _start_pieces(lo, hi, start):
    n = hi - lo

    def quad(q, carry):
        for r in range(4):
            start(lo + 4 * q + r, r % 2)
        return carry
    lax.fori_loop(0, lax.shift_right_logical(n, 2), quad, 0)

    def rest(r, carry):
        start(hi - 1 - r, 0)
        return carry
    lax.fori_loop(0, n & 3, rest, 0)


def _wait_pieces(n, piece_rows, copy_of_rows):
    def group(p, carry):
        copy_of_rows(WAIT_GROUP * piece_rows).wait()
        return carry
    lax.fori_loop(0, lax.shift_right_logical(n, LOG_WAIT_GROUP), group, 0)

    def single(p, carry):
        copy_of_rows(piece_rows).wait()
        return carry
    lax.fori_loop(0, n & (WAIT_GROUP - 1), single, 0)


def _interleave(body, n_pieces, start):
    per = ROW_BLK // N_CHUNKS

    def chunk(c, carry):
        for tt in range(per):
            body(c * per + tt)
        _start_pieces(lax.shift_right_logical(c * n_pieces, LOG_CHUNKS),
                      lax.shift_right_logical((c + 1) * n_pieces, LOG_CHUNKS), start)
        return carry
    lax.fori_loop(0, N_CHUNKS, chunk, 0)


def _dispatch_kernel(np_ref, nu_ref, nt_ref, ls_ref, pd_ref, pdp_ref, td_ref, h_ref, xs_hbm,
                     stg, pk, zblk, sem, zsem, *, n_steps, n_blocks):
    i = pl.program_id(0)
    slot = i % 3
    prev = (i + 2) % 3

    def piece_copy(s, p, d):
        return pltpu.make_async_copy(pk.at[s, _piece(p), :], xs_hbm.at[_piece(d), :], sem.at[s])

    def wait_pieces(s, n):
        _wait_pieces(n, PK_PIECE_ROWS, lambda rows: pltpu.make_async_copy(
            pk.at[s, pl.ds(0, rows), :], xs_hbm.at[pl.ds(0, rows), :], sem.at[s]))

    def zero_piece_copy(d):
        return pltpu.make_async_copy(zblk.at[pl.ds(0, PK_PIECE_ROWS), :], xs_hbm.at[_piece(d), :], zsem.at[0])

    def zero_block_copy(b):
        return pltpu.make_async_copy(zblk, xs_hbm.at[_rows(b, PK_BLK_ROWS), :], zsem.at[1])

    @pl.when(i == 0)
    def _():
        zblk[...] = jnp.zeros_like(zblk)

        def zero(c, carry):
            stg[_rows(c, 2048), :] = jnp.zeros((2048, LANES), F32)
            return carry
        lax.fori_loop(0, STG_ROWS // 2048, zero, 0)

        def blk(b, carry):
            zero_block_copy(b).start()
            return carry
        lax.fori_loop(nu_ref[0], n_blocks, blk, 0)

    @pl.when(i >= 3)
    def _():
        wait_pieces(slot, np_ref[jnp.maximum(i - 3, 0)])

    def scatter_token(t):
        tile = h_ref[_rows(t, TILE_ROWS), :].astype(BF16).astype(F32)
        for k in range(TOP_K):
            stg[_tile_at(ls_ref[0, k * ROW_BLK + t]), :] = tile

    np_prev = jnp.where(i >= 1, np_ref[jnp.maximum(i - 1, 0)], 0)
    _interleave(scatter_token, np_prev,
                lambda p, prio: piece_copy(prev, p, pdp_ref[0, p]).start(priority=prio))

    def pack_group(q, carry):
        v = stg[_rows(q, PACK_GROUP * PIECE_ROWS), :].reshape(PACK_GROUP * PIECE // 2, 2 * TILE_ROWS, LANES)
        packed = _pack_pairs(v[:, 0:TILE_ROWS, :], v[:, TILE_ROWS:2 * TILE_ROWS, :])
        pk[slot, _rows(q, PACK_GROUP * PK_PIECE_ROWS), :] = packed.reshape(PACK_GROUP * PK_PIECE_ROWS, LANES)
        return carry
    lax.fori_loop(0, lax.shift_right_logical(np_ref[i] + (PACK_GROUP - 1), LOG_PACK_GROUP), pack_group, 0)

    @pl.when(i == n_steps - 1)
    def _():
        _start_pieces(0, np_ref[i], lambda p, prio: piece_copy(slot, p, pd_ref[0, p]).start(priority=prio))

        def tail(p, carry):
            zero_piece_copy(td_ref[0, p]).start()
            return carry
        lax.fori_loop(0, nt_ref[0], tail, 0)

        wait_pieces((i + 1) % 3, np_ref[jnp.maximum(i - 2, 0)])
        wait_pieces(prev, np_ref[jnp.maximum(i - 1, 0)])
        wait_pieces(slot, np_ref[i])

        def tail_wait(p, carry):
            zero_piece_copy(0).wait()
            return carry
        lax.fori_loop(0, nt_ref[0], tail_wait, 0)

        def blk_wait(b, carry):
            zero_block_copy(0).wait()
            return carry
        lax.fori_loop(nu_ref[0], n_blocks, blk_wait, 0)


def _dispatch(npieces, n_used, ntail, lslot, piece_dst, tail_dst, h2t, n_steps, n_blocks):
    assert n_steps >= 3
    smem = lambda n, imap: pl.BlockSpec((None, 1, n), imap, memory_space=pltpu.SMEM)
    grid_spec = pltpu.PrefetchScalarGridSpec(
        num_scalar_prefetch=3,
        grid=(n_steps,),
        in_specs=[
            smem(TOP_K * ROW_BLK, lambda i, *_: (i, 0, 0)),
            smem(PIECES_MAX, lambda i, *_: (i, 0, 0)),
            smem(PIECES_MAX, lambda i, *_: (jnp.maximum(i - 1, 0), 0, 0)),
            smem(TAIL_MAX, lambda i, *_: (0, 0, 0)),
            pl.BlockSpec((ROW_BLK * TILE_ROWS, LANES), lambda i, *_: (i, 0)),
        ],
        out_specs=pl.BlockSpec(memory_space=pl.ANY),
        scratch_shapes=[
            pltpu.VMEM((STG_ROWS, LANES), F32),
            pltpu.VMEM((3, PK_STG_ROWS, LANES), U32),
            pltpu.VMEM((PK_BLK_ROWS, LANES), U32),
            pltpu.SemaphoreType.DMA((3,)),
            pltpu.SemaphoreType.DMA((2,)),
        ],
    )
    return pl.pallas_call(
        functools.partial(_dispatch_kernel, n_steps=n_steps, n_blocks=n_blocks),
        grid_spec=grid_spec,
        out_shape=jax.ShapeDtypeStruct((n_blocks * PK_BLK_ROWS, LANES), U32),
        compiler_params=pltpu.CompilerParams(dimension_semantics=("arbitrary",),
                                             vmem_limit_bytes=VMEM_LIMIT),
        name="dispatch",
    )(npieces, n_used, ntail, lslot, piece_dst, piece_dst, tail_dst, h2t)


def _from_tiles(ref, n):
    return jnp.concatenate([ref[pl.ds(s, n, stride=TILE_ROWS), :] for s in range(TILE_ROWS)], axis=1)


def _experts_kernel(be_ref, nu_ref, xs_hbm, wg_ref, wu_ref, wd_ref, ys_hbm,
                    xbuf, ybuf, zbuf, yhold, wgb, wub, wdb, xsem, ysem, zsem, *, n_blocks):
    i = pl.program_id(0)
    n_used = nu_ref[0]
    rows = PK_BLK_ROWS
    half = rows // 2
    pairs = MOE_BLK // 2

    def halves(b):
        return [pl.ds(pl.multiple_of(b * rows + h * half, half), half) for h in range(2)]

    def x_copies(b, s):
        return [pltpu.make_async_copy(xs_hbm.at[src, :], xbuf.at[s, pl.ds(h * half, half), :], xsem.at[s])
                for h, src in enumerate(halves(b))]

    def y_copies(b, s):
        return [pltpu.make_async_copy(ybuf.at[s, pl.ds(h * half, half), :], ys_hbm.at[dst, :], ysem.at[s])
                for h, dst in enumerate(halves(b))]

    def start(copies):
        for h, c in enumerate(copies):
            c.start(priority=h)

    def wait(copies):
        for c in copies:
            c.wait()

    def zero_fill(b):
        return pltpu.make_async_copy(zbuf, ys_hbm.at[_rows(b, rows), :], zsem.at[0])

    @pl.when(i == 0)
    def _():
        start(x_copies(0, 0))

        @pl.when(n_used > 1)
        def _():
            start(x_copies(1, 1))

        zbuf[...] = jnp.zeros_like(zbuf)
        yhold[...] = jnp.zeros_like(yhold)

        def fill(b, carry):
            zero_fill(b).start()
            return carry
        lax.fori_loop(n_used, n_blocks, fill, 0)

    @pl.when(i + 2 < n_used)
    def _():
        start(x_copies(i + 2, (i + 2) % 3))

    @pl.when(i < n_used)
    def _():
        wait(x_copies(i, i % 3))

        @pl.when(i >= 2)
        def _():
            wait(y_copies(i - 2, i % 2))

        @pl.when(jnp.logical_or(i == 0, be_ref[i] != be_ref[jnp.maximum(i - 1, 0)]))
        def _():
            wgb[...] = wg_ref[...].astype(BF16)
            wub[...] = wu_ref[...].astype(BF16)
            wdb[...] = wd_ref[...].astype(BF16)

        def send_held(b, s):
            held = yhold[...]
            yb = ybuf.at[s]
            for c in range(TILE_ROWS):
                yb[pl.ds(c, pairs, stride=TILE_ROWS), :] = held[:, c * LANES:(c + 1) * LANES]
            start(y_copies(b, s))

        send_held(jnp.where(i == 0, n_blocks, i - 1), i % 2)

        x = pltpu.bitcast(_from_tiles(xbuf.at[i % 3], pairs), BF16)
        g = _dot(x, wgb[...])
        u = _dot(x, wub[...])
        y = _dot((g * _sigmoid(g) * u).astype(BF16), wdb[...])
        yhold[...] = pltpu.bitcast(y.astype(BF16), U32)

        @pl.when(i == n_used - 1)
        def _():
            @pl.when(i >= 1)
            def _():
                wait(y_copies(0, (i + 1) % 2))
            send_held(i, (i + 1) % 2)
            wait(y_copies(0, (i + 1) % 2))
            wait(y_copies(0, i % 2))

            def fill_wait(b, carry):
                zero_fill(0).wait()
                return carry
            lax.fori_loop(n_used, n_blocks, fill_wait, 0)


def _experts(block_e, n_used, x_sorted, wg, wu, wd, layer, n_blocks):
    wspec = lambda shape: pl.BlockSpec((None, None) + shape, lambda i, be, nu: (layer, be[i], 0, 0))
    rows = PK_BLK_ROWS
    grid_spec = pltpu.PrefetchScalarGridSpec(
        num_scalar_prefetch=2,
        grid=(n_blocks,),
        in_specs=[
            pl.BlockSpec(memory_space=pl.ANY),
            wspec((D_MODEL, EXPERT_DIM)), wspec((D_MODEL, EXPERT_DIM)), wspec((EXPERT_DIM, D_MODEL)),
        ],
        out_specs=pl.BlockSpec(memory_space=pl.ANY),
        scratch_shapes=[
            pltpu.VMEM((3, rows, LANES), U32),
            pltpu.VMEM((2, rows, LANES), U32),
            pltpu.VMEM((rows, LANES), U32),
            pltpu.VMEM((MOE_BLK // 2, D_MODEL), U32),
            pltpu.VMEM((D_MODEL, EXPERT_DIM), BF16),
            pltpu.VMEM((D_MODEL, EXPERT_DIM), BF16),
            pltpu.VMEM((EXPERT_DIM, D_MODEL), BF16),
            pltpu.SemaphoreType.DMA((3,)),
            pltpu.SemaphoreType.DMA((2,)),
            pltpu.SemaphoreType.DMA((1,)),
        ],
    )
    return pl.pallas_call(
        functools.partial(_experts_kernel, n_blocks=n_blocks),
        grid_spec=grid_spec,
        out_shape=jax.ShapeDtypeStruct(((n_blocks + 1) * rows, LANES), U32),
        compiler_params=pltpu.CompilerParams(dimension_semantics=("arbitrary",)),
        name="experts",
    )(block_e, n_used, x_sorted, wg, wu, wd)


def _combine_kernel(np_ref, lp_ref, ws_ref, cur_ref, nx1_ref, nx2_ref, y_hbm, h2b_ref, xn_ref,
                    mod_ref, g_ref, swg_ref, swu_ref, swd_ref, o_ref, stg, acc, sem, *, n_steps):
    i = pl.program_id(0)
    slot = i % 3
    d = D_MODEL

    def piece_copy(s, p, src):
        dst = pl.ds(pl.multiple_of(s * PK_STG_ROWS + p * PK_PIECE_ROWS, PK_PIECE_ROWS), PK_PIECE_ROWS)
        return pltpu.make_async_copy(y_hbm.at[_piece(src), :], stg.at[dst, :], sem.at[s])

    def fetcher(s, pd_ref):
        return lambda p, prio: piece_copy(s, p, pd_ref[0, p]).start(priority=prio)

    @pl.when(i == 0)
    def _():
        _start_pieces(0, np_ref[0], fetcher(0, cur_ref))
        _start_pieces(0, np_ref[1], fetcher(1, nx1_ref))

    _wait_pieces(np_ref[i], PK_PIECE_ROWS, lambda rows: pltpu.make_async_copy(
        y_hbm.at[pl.ds(0, rows), :], stg.at[pl.ds(0, rows), :], sem.at[slot]))

    def gather_token(t):
        a = None
        for k in range(TOP_K):
            j = k * ROW_BLK + t
            pair = stg[_tile_at(lp_ref[0, j]), :]
            w = jnp.full((TILE_ROWS, LANES), ws_ref[0, j], F32)
            v = jnp.where(w < 0.0, _unpack_hi(pair), _unpack_lo(pair)) * jnp.abs(w)
            a = v if a is None else a + v
        acc[_rows(t, TILE_ROWS), :] = a

    np_next = jnp.where(i + 2 < n_steps, np_ref[jnp.minimum(i + 2, n_steps - 1)], 0)
    _interleave(gather_token, np_next, fetcher((i + 2) % 3, nx2_ref))

    hb = h2b_ref[...]
    g = _dot(hb, swg_ref[...])
    u = _dot(hb, swu_ref[...])
    f = _dot((g * _sigmoid(g) * u).astype(BF16), swd_ref[...]) + _from_tiles(acc, ROW_BLK)
    o_ref[...] = xn_ref[...] + mod_ref[:, 5 * d:6 * d] * _rms(f, g_ref[...])


def _combine(npieces, pair_row, w_signed, piece_src, y_sorted, h2b_rows, xn_rows, mod3, g, swg, swu, swd,
             n_steps, per_batch):
    assert n_steps >= 3
    last = n_steps - 1
    full = lambda shape: pl.BlockSpec(shape, lambda n, *_: (0,) * len(shape))
    rows = lambda w_: pl.BlockSpec((ROW_BLK, w_), lambda n, *_: (n, 0))
    smem = lambda w_, imap: pl.BlockSpec((None, 1, w_), imap, memory_space=pltpu.SMEM)
    n_lat = SEQ // ROW_BLK
    grid_spec = pltpu.PrefetchScalarGridSpec(
        num_scalar_prefetch=1,
        grid=(n_steps,),
        in_specs=[
            smem(TOP_K * ROW_BLK, lambda n, *_: (n, 0, 0)),
            smem(TOP_K * ROW_BLK, lambda n, *_: (n, 0, 0)),
            smem(PIECES_MAX, lambda n, *_: (n, 0, 0)),
            smem(PIECES_MAX, lambda n, *_: (jnp.minimum(n + 1, last), 0, 0)),
            smem(PIECES_MAX, lambda n, *_: (jnp.minimum(n + 2, last), 0, 0)),
            pl.BlockSpec(memory_space=pl.ANY),
            rows(D_MODEL), rows(D_MODEL),
            pl.BlockSpec((None, 1, 6 * D_MODEL),
                         lambda n, *_: (_mod_row(n % per_batch, n // per_batch, n_lat), 0, 0)),
            full((1, D_MODEL)),
            full((D_MODEL, SHARED_DIM)), full((D_MODEL, SHARED_DIM)), full((SHARED_DIM, D_MODEL)),
        ],
        out_specs=rows(D_MODEL),
        scratch_shapes=[
            pltpu.VMEM((3 * PK_STG_ROWS, LANES), U32),
            pltpu.VMEM((ROW_BLK * TILE_ROWS, LANES), F32),
            pltpu.SemaphoreType.DMA((3,)),
        ],
    )
    return pl.pallas_call(
        functools.partial(_combine_kernel, n_steps=n_steps),
        grid_spec=grid_spec,
        out_shape=jax.ShapeDtypeStruct(xn_rows.shape, F32),
        compiler_params=pltpu.CompilerParams(dimension_semantics=("arbitrary",),
                                             vmem_limit_bytes=VMEM_LIMIT),
        name="combine",
    )(npieces, pair_row, w_signed, piece_src, piece_src, piece_src, y_sorted, h2b_rows, xn_rows, mod3, g,
      swg, swu, swd)


def _moe_plan(cnt_rows, n_blocks):
    i32 = jnp.int32
    per_blk = MOE_BLK // PIECE
    to_expert = lambda v: v.reshape(v.shape[:-1] + (8, 8)).swapaxes(-1, -2).reshape(v.shape)
    runs = (cnt_rows + PIECE - 1) // PIECE
    loc_end = jnp.cumsum(runs, axis=1)
    loc = loc_end - runs
    npieces = loc_end[:, -1].astype(i32)
    reg = to_expert(runs.sum(axis=0))
    padded = (reg + per_blk - 1) // per_blk * per_blk
    pad_end = jnp.cumsum(padded)
    pad_start = pad_end - padded
    off = to_expert(pad_start)[None, :] + jnp.cumsum(runs, axis=0) - runs
    p = jnp.arange(PIECES_MAX, dtype=i32)[None, :, None]
    mine = (loc[:, None, :] <= p) & (p < loc_end[:, None, :])
    piece_dst = jnp.sum(jnp.where(mine, (off - loc)[:, None, :], 0), axis=-1) + p[:, :, 0]
    tail_n = padded - reg
    t_end = jnp.cumsum(tail_n)
    t_beg = t_end - tail_n
    q = jnp.arange(TAIL_MAX, dtype=i32)[:, None]
    tmine = (t_beg[None, :] <= q) & (q < t_end[None, :])
    tail_dst = jnp.sum(jnp.where(tmine, (pad_start + reg - t_beg)[None, :], 0), axis=-1) + q[:, 0]
    first = jnp.arange(n_blocks, dtype=i32)[:, None] * per_blk
    block_e = jnp.minimum(jnp.sum((pad_end[None, :] <= first).astype(i32), axis=-1), N_EXPERTS - 1)
    n_used = (pad_end[-1] // per_blk).astype(i32).reshape(1)
    return (npieces, piece_dst.astype(i32).reshape(-1, 1, PIECES_MAX),
            tail_dst.astype(i32).reshape(1, 1, TAIL_MAX), t_end[-1].astype(i32).reshape(1),
            block_e.astype(i32), n_used)


def _axial_tables(dim):
    f32 = np.float32
    t = np.arange(SEQ)
    row = (t // GRID_W).astype(f32)
    col = (t % GRID_W).astype(f32)
    quarter = dim // 4
    freqs = (f32(ROPE_THETA) ** (-np.arange(quarter, dtype=f32) / f32(quarter))).astype(f32)
    ar, ac = row[:, None] * freqs, col[:, None] * freqs
    ang = np.concatenate([ar, ar, ac, ac], axis=-1)
    sign = np.where((np.arange(dim) // quarter) % 2 == 0, -1.0, 1.0).astype(f32)
    cos = np.concatenate([np.cos(ang), np.ones((CTX_LEN, dim), f32)], axis=0)
    sin = np.concatenate([np.sin(ang) * sign, np.zeros((CTX_LEN, dim), f32)], axis=0)
    return cos.astype(f32), sin.astype(f32)


def _rope_tables():
    f32 = np.float32
    cos_h, sin_h = _axial_tables(HEAD_DIM)
    cos_r, sin_r = _axial_tables(B_ROPE)
    cosh, sinh = np.tile(cos_h, (1, 4)), np.tile(sin_h, (1, 4))
    scale_b = f32((B_NOPE + B_ROPE) ** -0.5)
    ones, zeros = np.ones((TOK, 64), f32), np.zeros((TOK, 64), f32)
    pad1, pad0 = np.ones((TOK, 32), f32), np.zeros((TOK, 32), f32)
    cosb = np.tile(np.concatenate([ones, cos_r, pad1], axis=1) * scale_b, (1, 4))
    sinb = np.tile(np.concatenate([zeros, sin_r, pad0], axis=1) * scale_b, (1, 4))
    cosr = np.concatenate([cos_r, np.ones((TOK, 96), f32)], axis=1)
    sinr = np.concatenate([sin_r, np.zeros((TOK, 96), f32)], axis=1)
    return tuple(jnp.asarray(a, F32) for a in (cosh, sinh, cosb, sinb, cosr, sinr))


_GQA_ORDER = (0, 2, 1, 3)


def _take_heads(w, base, order, axis):
    return jnp.concatenate(
        [lax.slice_in_dim(w, base + h * HEAD_DIM, base + (h + 1) * HEAD_DIM, axis=axis) for h in order],
        axis=axis)


def _w_in_layout(w_in):
    zeros = lambda n: jnp.zeros((D_MODEL, n), w_in.dtype)
    a_q = _take_heads(w_in, 0, _GQA_ORDER, 1)
    a_kv = w_in[:, 256:512]
    b_q = jnp.concatenate(
        [jnp.concatenate([w_in[:, 512 + h * 96:512 + (h + 1) * 96], zeros(32)], axis=1)
         for h in range(B_HEADS)], axis=1)
    b_c = w_in[:, 896:1024]
    b_r = jnp.concatenate([w_in[:, 1024:1056], zeros(96)], axis=1)
    c_q = _take_heads(w_in, 1056, _GQA_ORDER, 1)
    rest = w_in[:, 1312:2336]
    return jnp.concatenate([a_q, a_kv, b_q, b_c, b_r, c_q, rest], axis=1).astype(BF16)


def _w_kv_layout(w_kv_up):
    zeros = jnp.zeros((B_KV_RANK, 64), w_kv_up.dtype)
    nope = [jnp.concatenate([w_kv_up[:, h * 128:h * 128 + 64], zeros], axis=1) for h in range(B_HEADS)]
    val = [w_kv_up[:, h * 128 + 64:(h + 1) * 128] for h in range(B_HEADS)]
    return jnp.concatenate(nope + val, axis=1).astype(BF16)


def _w_out_layout(w_out):
    return jnp.concatenate([_take_heads(w_out, 0, _GQA_ORDER, 0), w_out[256:512],
                            _take_heads(w_out, 512, _GQA_ORDER, 0), w_out[768:1024]], axis=0).astype(BF16)


def _router_rows(v):
    return v.reshape((8, 8) + v.shape[1:]).swapaxes(0, 1).reshape(v.shape)


def kernel(x, c, ctx, c_ctx, ada_w, ada_b, pre_mix_g, post_mix_g, pre_ffn_g, post_ffn_g, w_in, a_q_norm, a_k_norm, b_kv_norm, b_w_kv_up, c_sink, d_rpb, w_out, router_w, router_b, exp_w_gate, exp_w_up, exp_w_down, sh_w_gate, sh_w_up, sh_w_down):
    bsz = x.shape[0]
    depth = ada_w.shape[0]
    xall = jnp.concatenate([x, ctx], axis=1)

    cvec = jnp.zeros((16, D_MODEL), F32).at[:bsz].set(c).at[8].set(c_ctx)
    mod = _ada(cvec, ada_w, ada_b)
    tabs = _rope_tables()
    avg = jnp.asarray(np.kron(np.eye(4, dtype=np.float32), np.full((64, 64), 1.0 / 64, np.float32)), BF16)
    row2 = lambda v: v.reshape(1, -1).astype(F32)

    for l in range(depth):
        with_ctx = l < depth - 1
        mod3 = mod[l].reshape(16, 1, 6 * D_MODEL)

        proj = _inproj(
            xall, mod3, row2(pre_mix_g[l]), _w_in_layout(w_in[l]),
            row2(jnp.tile(a_q_norm[l], 4) * HEAD_DIM ** -0.5), row2(jnp.tile(a_k_norm[l], 2)),
            row2(b_kv_norm[l]), _w_kv_layout(b_w_kv_up[l]), avg, tabs)
        qa, ka, va, qb, kb, vb, qc, kc, vc, qd, kd, vd = proj

        nq = NQ + (CTX_LEN // QBLK if with_ctx else 0)
        full = lambda shape: pl.BlockSpec(shape, lambda b, i: (0,) * len(shape))
        mix_a = _attn_call(_attn_a_kernel, "attn_a", qa, ka, va, [], [], nq // 2, qblk=2 * QBLK)
        mix_b = _attn_call(_attn_b_kernel, "attn_b", qb, kb, vb, [], [], nq // 2, qblk=2 * QBLK)
        sink = jnp.zeros((8, LANES), F32).at[:4].set(
            jnp.broadcast_to(c_sink[l][np.array(_GQA_ORDER)][:, None], (4, LANES)))
        mix_c = _attn_call(_attn_c_kernel, "attn_c", qc, kc, vc, [sink], [full((8, LANES))], nq)
        bias = _na_bias(d_rpb[l])
        mix_d = _attn_call(
            _attn_d_kernel, "attn_d", qd, kd, vd, [bias],
            [pl.BlockSpec((None, D_HEADS, QBLK, NA_WIN), lambda b, i: (_na_pattern(i), 0, 0, 0))], nq)

        nb = (TOK if with_ctx else SEQ) // ROW_BLK
        rb = jnp.broadcast_to(_router_rows(router_b[l])[:, None], (N_EXPERTS, ROW_BLK)).astype(F32)
        xn, h2b, h2t, wt_t, ls_t, cnt_t = _outproj(
            (mix_a, mix_b, mix_c, mix_d), _w_out_layout(w_out[l]), xall, mod3,
            row2(post_mix_g[l]), row2(pre_ffn_g[l]),
            _router_rows(router_w[l].T).astype(BF16), rb, nb)

        t_used = nb * ROW_BLK
        n_tok = bsz * t_used
        n_steps = bsz * nb
        n_blocks = -(-(n_tok * TOP_K + n_steps * N_EXPERTS * (PIECE - 1) + N_EXPERTS * (MOE_BLK - 1))
                     // MOE_BLK)
        npieces, piece_dst, tail_dst, ntail, block_e, n_used = _moe_plan(
            cnt_t[..., 0].reshape(n_steps, N_EXPERTS), n_blocks)
        lslot = ls_t.reshape(n_steps, 1, TOP_K * ROW_BLK)
        wts = wt_t.reshape(n_steps, 1, TOP_K * ROW_BLK)
        stg_row = lslot * TILE_ROWS
        ring = (jnp.arange(n_steps, dtype=jnp.int32) % 3)[:, None, None] * PK_STG_ROWS
        pair_row = (lslot // 2) * TILE_ROWS + ring
        w_signed = jnp.where(lslot % 2 == 0, wts, -wts)

        x_sorted = _dispatch(npieces, n_used, ntail, stg_row, piece_dst, tail_dst,
                             h2t.reshape(n_tok * TILE_ROWS, LANES), n_steps, n_blocks)
        y_sorted = _experts(block_e, n_used, x_sorted, exp_w_gate, exp_w_up, exp_w_down, l, n_blocks)
        xall = _combine(
            npieces, pair_row, w_signed, piece_dst, y_sorted, h2b.reshape(n_tok, D_MODEL),
            xn.reshape(n_tok, D_MODEL), mod3, row2(post_ffn_g[l]),
            sh_w_gate[l].astype(BF16), sh_w_up[l].astype(BF16), sh_w_down[l].astype(BF16),
            n_steps, nb).reshape(bsz, t_used, D_MODEL)

    return xall[:, :SEQ]
```

```python
import functools

import numpy as np
import jax
import jax.numpy as jnp
from jax import lax
from jax.experimental import pallas as pl
from jax.experimental.pallas import tpu as pltpu

F32 = jnp.float32
BF16 = jnp.bfloat16
U32 = jnp.uint32

D_MODEL = 1024
SEQ = 2048
CTX_LEN = 256
TOK = SEQ + CTX_LEN
GRID_W = 64
HEAD_DIM = 64
ROPE_THETA = 10000.0
EPS = 1e-6
NEG = -1e30

A_HEADS, A_KV_HEADS = 4, 2
B_HEADS, B_NOPE, B_ROPE, B_V, B_KV_RANK = 4, 64, 32, 64, 128
C_HEADS, C_KV_HEADS, C_WINDOW = 4, 2, 128
D_HEADS, NA_ROWS, NA_COLS = 4, 8, 16

N_EXPERTS, TOP_K, N_GROUPS, TOPK_GROUPS = 64, 8, 8, 4
EXPERT_DIM, SHARED_DIM = 256, 256
ROUTED_SCALE = 2.5

LANES = 128
QBLK = 128
ROW_BLK = 256
IN_BLK = 768
MOE_BLK = 384
TILE_ROWS = D_MODEL // LANES
PIECE = 8
PIECE_ROWS = PIECE * TILE_ROWS
PIECES_MAX = 320
STG_ROWS = PIECES_MAX * PIECE_ROWS
PK_PIECE_ROWS = PIECE_ROWS // 2
PK_STG_ROWS = STG_ROWS // 2
PK_BLK_ROWS = MOE_BLK * TILE_ROWS // 2
TAIL_MAX = N_EXPERTS * MOE_BLK // PIECE
LOG_PACK_GROUP = 2
PACK_GROUP = 1 << LOG_PACK_GROUP
LOG_WAIT_GROUP = 4
WAIT_GROUP = 1 << LOG_WAIT_GROUP
LOG_CHUNKS = 2
N_CHUNKS = 1 << LOG_CHUNKS
VMEM_LIMIT = 56 * 1024 * 1024
NQ = SEQ // QBLK
NA_WIN = 10 * GRID_W

_AQ, _AK, _AV, _BQ, _BC, _BR = 0, 256, 384, 512, 1024, 1152
_CQ, _CK, _CV, _DQ, _DK, _DV = 1280, 1536, 1664, 1792, 2048, 2304
IN_W = 2560

_NT = (((1,), (1,)), ((), ()))


def _sigmoid(x):
    return 1.0 / (1.0 + jnp.exp(-x))


def _rms(x, g):
    return x * lax.rsqrt(jnp.mean(x * x, axis=-1, keepdims=True) + EPS) * g


def _dot(a, b):
    return jnp.dot(a, b, preferred_element_type=F32)


def _dot_nt(a, b):
    return lax.dot_general(a, b, _NT, preferred_element_type=F32)


def _ada_kernel(c_ref, w_ref, b_ref, o_ref):
    c = c_ref[...]
    a = (c * _sigmoid(c)).astype(BF16)
    o_ref[...] = _dot(a, w_ref[...].astype(BF16)) + b_ref[...]


def _ada(cvec, ada_w, ada_b):
    depth = ada_w.shape[0]
    n = ada_w.shape[2]
    tn = 1536
    return pl.pallas_call(
        _ada_kernel,
        grid=(depth, n // tn),
        in_specs=[
            pl.BlockSpec((16, D_MODEL), lambda l, j: (0, 0)),
            pl.BlockSpec((None, D_MODEL, tn), lambda l, j: (l, 0, j)),
            pl.BlockSpec((None, 1, tn), lambda l, j: (l, 0, j)),
        ],
        out_specs=pl.BlockSpec((None, 16, tn), lambda l, j: (l, 0, j)),
        out_shape=jax.ShapeDtypeStruct((depth, 16, n), F32),
        name="ada",
    )(cvec, ada_w, ada_b.reshape(depth, 1, n))


def _rope(x, cos, sin, chunk):
    outs = []
    for c in range(x.shape[1] // LANES):
        sl = slice(c * LANES, (c + 1) * LANES)
        xs = x[:, sl]
        lane = lax.broadcasted_iota(jnp.int32, xs.shape, 1)
        first = (lane & chunk) == 0
        rot = jnp.where(first, pltpu.roll(xs, LANES - chunk, 1), pltpu.roll(xs, chunk, 1))
        outs.append(xs * cos[:, sl] + rot * sin[:, sl])
    return outs[0] if len(outs) == 1 else jnp.concatenate(outs, axis=1)


def _head_norm(x, avg):
    sq = x * x
    hi = sq.astype(BF16)
    lo = (sq - hi.astype(F32)).astype(BF16)
    ms = _dot(hi, avg) + _dot(lo, avg)
    return x * lax.rsqrt(ms + EPS)


def _inproj_kernel(x_ref, mod_ref, modc_ref, g_ref, w_ref, qg_ref, kg_ref, kvg_ref, wkv_ref, avg_ref,
                   cosh_ref, sinh_ref, cosb_ref, sinb_ref, cosr_ref, sinr_ref,
                   qa_ref, ka_ref, va_ref, qb_ref, kb_ref, vb_ref,
                   qc_ref, kc_ref, vc_ref, qd_ref, kd_ref, vd_ref):
    d = D_MODEL
    n = x_ref.shape[0]
    is_ctx = pl.program_id(1) * n + lax.broadcasted_iota(jnp.int32, (n, 1), 0) >= SEQ
    scale = jnp.where(is_ctx, modc_ref[:, d:2 * d], mod_ref[:, d:2 * d])
    shift = jnp.where(is_ctx, modc_ref[:, 0:d], mod_ref[:, 0:d])
    h = _rms(x_ref[...], g_ref[...]) * (1.0 + scale) + shift
    p = _dot(h.astype(BF16), w_ref[...])
    cosh, sinh = cosh_ref[...], sinh_ref[...]
    avg = avg_ref[...]

    qa = _head_norm(p[:, _AQ:_AQ + 256], avg) * qg_ref[...]
    qa_ref[...] = _rope(qa, cosh, sinh, 16).astype(BF16)
    ka = _head_norm(p[:, _AK:_AK + 128], avg[0:128, 0:128]) * kg_ref[...]
    ka_ref[...] = _rope(ka, cosh[:, 0:128], sinh[:, 0:128], 16).astype(BF16)
    ones = jnp.ones((p.shape[0], LANES), BF16)
    va_ref[...] = jnp.concatenate([p[:, _AV:_AV + 128].astype(BF16), ones], axis=1)

    qb_ref[...] = _rope(p[:, _BQ:_BQ + 512], cosb_ref[...], sinb_ref[...], 8).astype(BF16)
    cn = _rms(p[:, _BC:_BC + 128], kvg_ref[...])
    kv = _dot(cn.astype(BF16), wkv_ref[...])
    kr = _rope(p[:, _BR:_BR + 128], cosr_ref[...], sinr_ref[...], 8)
    kr = pltpu.roll(kr, 64, 1)
    kb_ref[...] = jnp.concatenate(
        [kv[:, hh * 128:(hh + 1) * 128] + kr for hh in range(B_HEADS)], axis=1).astype(BF16)
    vb_ref[...] = jnp.concatenate(
        [kv[:, 512:640].astype(BF16), ones, kv[:, 640:768].astype(BF16), ones], axis=1)

    qc_ref[...] = _rope(p[:, _CQ:_CQ + 256] * 0.125, cosh, sinh, 16).astype(BF16)
    kc_ref[...] = _rope(p[:, _CK:_CK + 128], cosh[:, 0:128], sinh[:, 0:128], 16).astype(BF16)
    vc_ref[...] = jnp.concatenate([p[:, _CV:_CV + 128].astype(BF16), ones], axis=1)

    qd_ref[...] = (p[:, _DQ:_DQ + 256] * 0.125).astype(BF16)
    kd_ref[...] = p[:, _DK:_DK + 256].astype(BF16)
    vd_ref[...] = jnp.concatenate(
        [p[:, _DV:_DV + 128].astype(BF16), ones, p[:, _DV + 128:_DV + 256].astype(BF16), ones], axis=1)


def _mod_row(j, b, n_lat_blocks):
    return jnp.where(j < n_lat_blocks, b, 8)


def _inproj(xall, mod3, g, w, qg, kg, kvg, wkv, avg, tabs):
    bsz = xall.shape[0]
    nb = TOK // IN_BLK
    full = lambda shape: pl.BlockSpec(shape, lambda b, j: (0,) * len(shape))
    row = lambda w_: pl.BlockSpec((IN_BLK, w_), lambda b, j: (j, 0))
    out = lambda w_: pl.BlockSpec((None, IN_BLK, w_), lambda b, j: (b, j, 0))
    widths = (256, 128, 256, 512, 512, 512, 256, 128, 256, 256, 256, 512)
    return pl.pallas_call(
        _inproj_kernel,
        grid=(bsz, nb),
        in_specs=[
            pl.BlockSpec((None, IN_BLK, D_MODEL), lambda b, j: (b, j, 0)),
            pl.BlockSpec((None, 1, 6 * D_MODEL), lambda b, j: (b, 0, 0)),
            pl.BlockSpec((None, 1, 6 * D_MODEL), lambda b, j: (8, 0, 0)),
            full((1, D_MODEL)), full((D_MODEL, IN_W)),
            full((1, 256)), full((1, 128)), full((1, 128)), full((128, 768)), full((256, 256)),
            row(256), row(256), row(512), row(512), row(128), row(128),
        ],
        out_specs=[out(w_) for w_ in widths],
        out_shape=[jax.ShapeDtypeStruct((bsz, TOK, w_), BF16) for w_ in widths],
        compiler_params=pltpu.CompilerParams(vmem_limit_bytes=VMEM_LIMIT),
        name="inproj",
    )(xall, mod3, mod3, g, w, qg, kg, kvg, wkv, avg, *tabs)


def _softmax_pv(segs, sink=None):
    m = None
    for s, _ in segs:
        ms = jnp.max(s, axis=-1, keepdims=True)
        m = ms if m is None else jnp.maximum(m, ms)
    if sink is not None:
        m = jnp.maximum(m, sink)
    acc = None
    for s, v1 in segs:
        r = _dot(jnp.exp((s - m).astype(BF16)), v1)
        acc = r if acc is None else acc + r
    l = acc[:, LANES:LANES + 1]
    if sink is not None:
        l = l + jnp.exp(sink - m)
    return acc[:, 0:LANES] / l


def _lane_lo(dtype_shape):
    return lax.broadcasted_iota(jnp.int32, dtype_shape, 1) < HEAD_DIM


def _half(qt, hh):
    lo = _lane_lo(qt.shape)
    return jnp.where(lo if hh == 0 else jnp.logical_not(lo), qt, jnp.zeros_like(qt))


def _merge_halves(o0, o1):
    return jnp.where(_lane_lo(o0.shape), o0, o1)


def _stack_heads(q_ref, tiles):
    return jnp.concatenate(
        [_half(q_ref[:, t * LANES:(t + 1) * LANES], hh) for t in tiles for hh in range(2)], axis=0)


def _store_heads(o_ref, o, tiles):
    n = o_ref.shape[0]
    for ti, t in enumerate(tiles):
        o0, o1 = o[2 * ti * n:(2 * ti + 1) * n], o[(2 * ti + 1) * n:(2 * ti + 2) * n]
        o_ref[:, t * LANES:(t + 1) * LANES] = _merge_halves(o0, o1).astype(o_ref.dtype)


def _attn_a_kernel(q_ref, k_ref, v_ref, o_ref, *, n_lat):
    def run(k, v1):
        for t in range(2):
            o = [_softmax_pv([(_dot_nt(_half(q_ref[:, t * LANES:(t + 1) * LANES], hh), k), v1)])
                 for hh in range(2)]
            o_ref[:, t * LANES:(t + 1) * LANES] = _merge_halves(o[0], o[1]).astype(o_ref.dtype)

    @pl.when(pl.program_id(1) < n_lat)
    def _():
        run(k_ref[...], v_ref[...])

    @pl.when(pl.program_id(1) >= n_lat)
    def _():
        run(k_ref[SEQ:TOK, :], v_ref[SEQ:TOK, :])


def _attn_b_kernel(q_ref, k_ref, v_ref, o_ref, *, n_lat):
    def run(lo_, hi_):
        for t in range(2):
            v1 = v_ref[lo_:hi_, t * 256:(t + 1) * 256]
            o = []
            for hh in range(2):
                hd = 2 * t + hh
                q = q_ref[:, hd * LANES:(hd + 1) * LANES]
                k = k_ref[lo_:hi_, hd * LANES:(hd + 1) * LANES]
                o.append(_softmax_pv([(_dot_nt(q, k), v1)]))
            o_ref[:, t * LANES:(t + 1) * LANES] = _merge_halves(o[0], o[1]).astype(o_ref.dtype)

    @pl.when(pl.program_id(1) < n_lat)
    def _():
        run(0, TOK)

    @pl.when(pl.program_id(1) >= n_lat)
    def _():
        run(SEQ, TOK)


def _attn_c_kernel(q_ref, k_ref, v_ref, sink_ref, o_ref, *, n_lat):
    i = pl.program_id(1)
    qblk = o_ref.shape[0]
    band = 3 * qblk

    def run(latent):
        if latent:
            start = pl.multiple_of(jnp.clip((i - 1) * qblk, 0, SEQ - band), qblk)
            shape = (2 * qblk, band)
            qpos = i * qblk + (lax.broadcasted_iota(jnp.int32, shape, 0) & (qblk - 1))
            kpos = start + lax.broadcasted_iota(jnp.int32, shape, 1)
            valid = jnp.abs(qpos - kpos) <= C_WINDOW
        for t in range(2):
            q = _stack_heads(q_ref, (t,))
            segs = [(_dot_nt(q, k_ref[SEQ:TOK, :]), v_ref[SEQ:TOK, :])]
            if latent:
                s = jnp.where(valid, _dot_nt(q, k_ref[pl.ds(start, band), :]), NEG)
                segs.insert(0, (s, v_ref[pl.ds(start, band), :]))
            sink = jnp.concatenate(
                [jnp.broadcast_to(sink_ref[j:j + 1, 0:1], (qblk, 1)) for j in (2 * t, 2 * t + 1)], axis=0)
            _store_heads(o_ref, _softmax_pv(segs, sink=sink), (t,))

    @pl.when(i < n_lat)
    def _():
        run(True)

    @pl.when(i >= n_lat)
    def _():
        run(False)


def _attn_d_kernel(q_ref, k_ref, v_ref, bias_ref, o_ref, *, n_lat):
    i = pl.program_id(1)
    qblk = o_ref.shape[0]

    def run(latent):
        if latent:
            start = pl.multiple_of(jnp.clip((i - 2) * qblk, 0, SEQ - NA_WIN), qblk)
        for t in range(2):
            ksl, vsl = slice(t * LANES, (t + 1) * LANES), slice(t * 256, (t + 1) * 256)
            q = _stack_heads(q_ref, (t,))
            segs = [(_dot_nt(q, k_ref[SEQ:TOK, ksl]), v_ref[SEQ:TOK, vsl])]
            if latent:
                bias = bias_ref[2 * t:2 * t + 2].reshape(2 * qblk, NA_WIN)
                segs.insert(0, (_dot_nt(q, k_ref[pl.ds(start, NA_WIN), ksl]) + bias,
                                v_ref[pl.ds(start, NA_WIN), vsl]))
            _store_heads(o_ref, _softmax_pv(segs), (t,))

    @pl.when(i < n_lat)
    def _():
        run(True)

    @pl.when(i >= n_lat)
    def _():
        run(False)


def _attn_call(kernel, name, q, k, v, extra, extra_specs, n_blocks, qblk=QBLK):
    bsz = q.shape[0]
    return pl.pallas_call(
        functools.partial(kernel, n_lat=SEQ // qblk),
        grid=(bsz, n_blocks),
        in_specs=[
            pl.BlockSpec((None, qblk, q.shape[2]), lambda b, i: (b, i, 0)),
            pl.BlockSpec((None, TOK, k.shape[2]), lambda b, i: (b, 0, 0)),
            pl.BlockSpec((None, TOK, v.shape[2]), lambda b, i: (b, 0, 0)),
        ] + extra_specs,
        out_specs=pl.BlockSpec((None, qblk, 256), lambda b, i: (b, i, 0)),
        out_shape=jax.ShapeDtypeStruct((bsz, n_blocks * qblk, 256), BF16),
        name=name,
    )(q, k, v, *extra)


def _na_pattern(i):
    return jnp.where(i < 2, i, jnp.where(i < NQ - 2, 2, jnp.minimum(i, NQ - 1) - (NQ - 5)))


def _na_bias(rpb):
    rows = SEQ // GRID_W
    n_dr, n_dc = 2 * NA_ROWS - 1, 2 * NA_COLS - 1
    qc = np.arange(GRID_W)
    dc = np.clip(qc[None, :] - qc[:, None] + NA_COLS - 1, 0, n_dc - 1)
    sel_c = jnp.asarray(np.eye(n_dc, dtype=np.float32)[dc])
    cs = np.clip(qc - NA_COLS // 2, 0, GRID_W - NA_COLS)[:, None]
    col_ok = (qc[None, :] >= cs) & (qc[None, :] < cs + NA_COLS)
    sel_r, row_ok = [], []
    for i in (0, 1, 2, NQ - 2, NQ - 1):
        start_row = min(max(2 * i - 4, 0), rows - 10)
        qr = 2 * i + np.arange(2)
        kr = start_row + np.arange(10)
        rs = np.clip(qr - NA_ROWS // 2, 0, rows - NA_ROWS)[:, None]
        row_ok.append((kr[None] >= rs) & (kr[None] < rs + NA_ROWS))
        dr = np.clip(kr[None] - qr[:, None] + NA_ROWS - 1, 0, n_dr - 1)
        sel_r.append(np.eye(n_dr, dtype=np.float32)[dr])
    b = jnp.einsum("hrc,pabr,qkc->phaqbk", rpb.astype(F32), jnp.asarray(np.stack(sel_r)), sel_c,
                   precision=lax.Precision.HIGHEST)
    valid = np.stack(row_ok)[:, None, :, None, :, None] & col_ok[None, None, None, :, None, :]
    return jnp.where(valid, b, NEG).reshape(5, D_HEADS, QBLK, NA_WIN)


def _outproj_kernel(ma_ref, mb_ref, mc_ref, md_ref, w_ref, x_ref, mod_ref, g1_ref, g2_ref,
                    wr_ref, rb_ref, xn_ref, h2b_ref, h2t_ref, wt_ref, ls_ref, cnt_ref):
    d = D_MODEL
    mix = jnp.concatenate([ma_ref[...], mb_ref[...], mc_ref[...], md_ref[...]], axis=1)
    y = _rms(_dot(mix, w_ref[...]), g1_ref[...])
    xn = x_ref[...] + mod_ref[:, 2 * d:3 * d] * y
    xn_ref[...] = xn
    h2 = _rms(xn, g2_ref[...]) * (1.0 + mod_ref[:, 4 * d:5 * d]) + mod_ref[:, 3 * d:4 * d]
    n = h2.shape[0]
    h2b_ref[...] = h2.astype(BF16)
    for s in range(TILE_ROWS):
        h2t_ref[pl.ds(s, n, stride=TILE_ROWS), :] = h2[:, s * LANES:(s + 1) * LANES]

    s = _sigmoid(_dot_nt(wr_ref[...], h2.astype(BF16)))
    sb = s + rb_ref[...]
    per = N_EXPERTS // N_GROUPS
    slab = [sb[j * 8:(j + 1) * 8, :] for j in range(per)]
    sraw = [s[j * 8:(j + 1) * 8, :] for j in range(per)]
    giota = lax.broadcasted_iota(jnp.int32, (N_GROUPS, n), 0)
    eid = [giota * per + j for j in range(per)]
    ninf = -jnp.inf

    m1 = functools.reduce(jnp.maximum, slab)
    j1 = functools.reduce(jnp.minimum, [jnp.where(slab[j] == m1, j, per) for j in range(per)])
    m2 = functools.reduce(jnp.maximum, [jnp.where(j1 == j, ninf, slab[j]) for j in range(per)])
    gs = m1 + m2

    gsel = jnp.zeros((N_GROUPS, n), jnp.bool_)
    for _ in range(TOPK_GROUPS):
        m = jnp.max(gs, axis=0, keepdims=True)
        gi = jnp.min(jnp.where(gs == m, giota, N_GROUPS), axis=0, keepdims=True)
        hit = giota == gi
        gsel = jnp.logical_or(gsel, hit)
        gs = jnp.where(hit, ninf, gs)
    slab = [jnp.where(gsel, sj, ninf) for sj in slab]

    def pick(hits, vals):
        return jnp.sum(functools.reduce(jnp.add, [jnp.where(hits[j], vals[j], 0.0) for j in range(per)]),
                       axis=0, keepdims=True)

    kiota = lax.broadcasted_iota(jnp.int32, (TOP_K, n), 0)
    wts = jnp.zeros((TOP_K, n), F32)
    chosen = []
    for k in range(TOP_K):
        m = jnp.max(functools.reduce(jnp.maximum, slab), axis=0, keepdims=True)
        cand = functools.reduce(jnp.minimum,
                                [jnp.where(slab[j] == m, eid[j], N_EXPERTS) for j in range(per)])
        ei = jnp.min(cand, axis=0, keepdims=True)
        hits = [eid[j] == ei for j in range(per)]
        slab = [jnp.where(hits[j], ninf, slab[j]) for j in range(per)]
        wts = jnp.where(kiota == k, pick(hits, sraw), wts)
        chosen.append(ei)
    wt_ref[...] = wts / jnp.sum(wts, axis=0, keepdims=True) * ROUTED_SCALE

    hits_k = [[eid[j] == chosen[k] for j in range(per)] for k in range(TOP_K)]
    sel = jnp.concatenate(
        [functools.reduce(jnp.add, [jnp.where(hits_k[k][j], 1.0, 0.0) for k in range(TOP_K)])
         for j in range(per)], axis=0)
    tri = jnp.where(lax.broadcasted_iota(jnp.int32, (n, n), 0) < lax.broadcasted_iota(jnp.int32, (n, n), 1),
                    1.0, 0.0).astype(BF16)
    before = _dot(sel.astype(BF16), tri)
    cnt = jnp.sum(sel, axis=1, keepdims=True)
    run = jnp.floor((cnt + (PIECE - 1.0)) * (1.0 / PIECE)) * PIECE
    low = jnp.where(lax.broadcasted_iota(jnp.int32, (N_EXPERTS, N_EXPERTS), 1)
                    < lax.broadcasted_iota(jnp.int32, (N_EXPERTS, N_EXPERTS), 0), 1.0, 0.0).astype(BF16)
    start = _dot(low, jnp.broadcast_to(run, (N_EXPERTS, LANES)).astype(BF16))[:, 0:1]
    base = before + start
    base = [base[j * 8:(j + 1) * 8, :] for j in range(per)]
    lsl = jnp.zeros((TOP_K, n), F32)
    for k in range(TOP_K):
        lsl = jnp.where(kiota == k, pick(hits_k[k], base), lsl)
    ls_ref[...] = lsl.astype(jnp.int32)
    cnt_ref[...] = jnp.broadcast_to(cnt, (N_EXPERTS, LANES)).astype(jnp.int32)


def _outproj(mixes, w, xall, mod3, g1, g2, wr, rb, nb):
    bsz = xall.shape[0]
    n_lat = SEQ // ROW_BLK
    full = lambda shape: pl.BlockSpec(shape, lambda b, j: (0,) * len(shape))
    rows = lambda w_: pl.BlockSpec((None, ROW_BLK, w_), lambda b, j: (b, j, 0))
    route = pl.BlockSpec((None, None, TOP_K, ROW_BLK), lambda b, j: (b, j, 0, 0))
    return pl.pallas_call(
        _outproj_kernel,
        grid=(bsz, nb),
        in_specs=[rows(256)] * 4 + [
            full((D_MODEL, D_MODEL)), rows(D_MODEL),
            pl.BlockSpec((None, 1, 6 * D_MODEL), lambda b, j: (_mod_row(j, b, n_lat), 0, 0)),
            full((1, D_MODEL)), full((1, D_MODEL)),
            full((N_EXPERTS, D_MODEL)), full((N_EXPERTS, ROW_BLK)),
        ],
        out_specs=[
            rows(D_MODEL), rows(D_MODEL),
            pl.BlockSpec((None, ROW_BLK * TILE_ROWS, LANES), lambda b, j: (b, j, 0)),
            route, route,
            pl.BlockSpec((None, None, N_EXPERTS, LANES), lambda b, j: (b, j, 0, 0)),
        ],
        out_shape=[
            jax.ShapeDtypeStruct((bsz, nb * ROW_BLK, D_MODEL), F32),
            jax.ShapeDtypeStruct((bsz, nb * ROW_BLK, D_MODEL), BF16),
            jax.ShapeDtypeStruct((bsz, nb * ROW_BLK * TILE_ROWS, LANES), F32),
            jax.ShapeDtypeStruct((bsz, nb, TOP_K, ROW_BLK), F32),
            jax.ShapeDtypeStruct((bsz, nb, TOP_K, ROW_BLK), jnp.int32),
            jax.ShapeDtypeStruct((bsz, nb, N_EXPERTS, LANES), jnp.int32),
        ],
        name="outproj",
    )(*mixes, w, xall, mod3, g1, g2, wr, rb)


def _rows(i, n):
    return pl.ds(pl.multiple_of(i * n, n), n)


def _tile_at(row):
    return pl.ds(pl.multiple_of(row, TILE_ROWS), TILE_ROWS)


def _piece(p):
    return _rows(p, PK_PIECE_ROWS)


def _pack_pairs(lo, hi):
    return (lax.shift_right_logical(pltpu.bitcast(lo, U32), jnp.uint32(16))
            | (pltpu.bitcast(hi, U32) & jnp.uint32(0xFFFF0000)))


def _unpack_lo(u):
    return pltpu.bitcast(lax.shift_left(u, jnp.uint32(16)), F32)


def _unpack_hi(u):
    return pltpu.bitcast(u & jnp.uint32(0xFFFF0000), F32)


def _start_pieces(lo, hi, start):
    n = hi - lo

    def quad(q, carry):
        for r in range(4):
            start(lo + 4 * q + r, r % 2)
        return carry
    lax.fori_loop(0, lax.shift_right_logical(n, 2), quad, 0)

    def rest(r, carry):
        start(hi - 1 - r, 0)
        return carry
    lax.fori_loop(0, n & 3, rest, 0)


def _wait_pieces(n, piece_rows, copy_of_rows):
    def group(p, carry):
        copy_of_rows(WAIT_GROUP * piece_rows).wait()
        return carry
    lax.fori_loop(0, lax.shift_right_logical(n, LOG_WAIT_GROUP), group, 0)

    def single(p, carry):
        copy_of_rows(piece_rows).wait()
        return carry
    lax.fori_loop(0, n & (WAIT_GROUP - 1), single, 0)


def _interleave(body, n_pieces, start):
    per = ROW_BLK // N_CHUNKS

    def chunk(c, carry):
        for tt in range(per):
            body(c * per + tt)
        _start_pieces(lax.shift_right_logical(c * n_pieces, LOG_CHUNKS),
                      lax.shift_right_logical((c + 1) * n_pieces, LOG_CHUNKS), start)
        return carry
    lax.fori_loop(0, N_CHUNKS, chunk, 0)


def _dispatch_kernel(np_ref, nu_ref, nt_ref, ls_ref, pd_ref, pdp_ref, td_ref, h_ref, xs_hbm,
                     stg, pk, zblk, sem, zsem, *, n_steps, n_blocks):
    i = pl.program_id(0)
    slot = i % 3
    prev = (i + 2) % 3

    def piece_copy(s, p, d):
        return pltpu.make_async_copy(pk.at[s, _piece(p), :], xs_hbm.at[_piece(d), :], sem.at[s])

    def wait_pieces(s, n):
        _wait_pieces(n, PK_PIECE_ROWS, lambda rows: pltpu.make_async_copy(
            pk.at[s, pl.ds(0, rows), :], xs_hbm.at[pl.ds(0, rows), :], sem.at[s]))

    def zero_piece_copy(d):
        return pltpu.make_async_copy(zblk.at[pl.ds(0, PK_PIECE_ROWS), :], xs_hbm.at[_piece(d), :], zsem.at[0])

    def zero_block_copy(b):
        return pltpu.make_async_copy(zblk, xs_hbm.at[_rows(b, PK_BLK_ROWS), :], zsem.at[1])

    @pl.when(i == 0)
    def _():
        zblk[...] = jnp.zeros_like(zblk)

        def zero(c, carry):
            stg[_rows(c, 2048), :] = jnp.zeros((2048, LANES), F32)
            return carry
        lax.fori_loop(0, STG_ROWS // 2048, zero, 0)

        def blk(b, carry):
            zero_block_copy(b).start()
            return carry
        lax.fori_loop(nu_ref[0], n_blocks, blk, 0)

    @pl.when(i >= 3)
    def _():
        wait_pieces(slot, np_ref[jnp.maximum(i - 3, 0)])

    def scatter_token(t):
        tile = h_ref[_rows(t, TILE_ROWS), :].astype(BF16).astype(F32)
        for k in range(TOP_K):
            stg[_tile_at(ls_ref[0, k * ROW_BLK + t]), :] = tile

    np_prev = jnp.where(i >= 1, np_ref[jnp.maximum(i - 1, 0)], 0)
    _interleave(scatter_token, np_prev,
                lambda p, prio: piece_copy(prev, p, pdp_ref[0, p]).start(priority=prio))

    def pack_group(q, carry):
        v = stg[_rows(q, PACK_GROUP * PIECE_ROWS), :].reshape(PACK_GROUP * PIECE // 2, 2 * TILE_ROWS, LANES)
        packed = _pack_pairs(v[:, 0:TILE_ROWS, :], v[:, TILE_ROWS:2 * TILE_ROWS, :])
        pk[slot, _rows(q, PACK_GROUP * PK_PIECE_ROWS), :] = packed.reshape(PACK_GROUP * PK_PIECE_ROWS, LANES)
        return carry
    lax.fori_loop(0, lax.shift_right_logical(np_ref[i] + (PACK_GROUP - 1), LOG_PACK_GROUP), pack_group, 0)

    @pl.when(i == n_steps - 1)
    def _():
        _start_pieces(0, np_ref[i], lambda p, prio: piece_copy(slot, p, pd_ref[0, p]).start(priority=prio))

        def tail(p, carry):
            zero_piece_copy(td_ref[0, p]).start()
            return carry
        lax.fori_loop(0, nt_ref[0], tail, 0)

        wait_pieces((i + 1) % 3, np_ref[jnp.maximum(i - 2, 0)])
        wait_pieces(prev, np_ref[jnp.maximum(i - 1, 0)])
        wait_pieces(slot, np_ref[i])

        def tail_wait(p, carry):
            zero_piece_copy(0).wait()
            return carry
        lax.fori_loop(0, nt_ref[0], tail_wait, 0)

        def blk_wait(b, carry):
            zero_block_copy(0).wait()
            return carry
        lax.fori_loop(nu_ref[0], n_blocks, blk_wait, 0)


def _dispatch(npieces, n_used, ntail, lslot, piece_dst, tail_dst, h2t, n_steps, n_blocks):
    assert n_steps >= 3
    smem = lambda n, imap: pl.BlockSpec((None, 1, n), imap, memory_space=pltpu.SMEM)
    grid_spec = pltpu.PrefetchScalarGridSpec(
        num_scalar_prefetch=3,
        grid=(n_steps,),
        in_specs=[
            smem(TOP_K * ROW_BLK, lambda i, *_: (i, 0, 0)),
            smem(PIECES_MAX, lambda i, *_: (i, 0, 0)),
            smem(PIECES_MAX, lambda i, *_: (jnp.maximum(i - 1, 0), 0, 0)),
            smem(TAIL_MAX, lambda i, *_: (0, 0, 0)),
            pl.BlockSpec((ROW_BLK * TILE_ROWS, LANES), lambda i, *_: (i, 0)),
        ],
        out_specs=pl.BlockSpec(memory_space=pl.ANY),
        scratch_shapes=[
            pltpu.VMEM((STG_ROWS, LANES), F32),
            pltpu.VMEM((3, PK_STG_ROWS, LANES), U32),
            pltpu.VMEM((PK_BLK_ROWS, LANES), U32),
            pltpu.SemaphoreType.DMA((3,)),
            pltpu.SemaphoreType.DMA((2,)),
        ],
    )
    return pl.pallas_call(
        functools.partial(_dispatch_kernel, n_steps=n_steps, n_blocks=n_blocks),
        grid_spec=grid_spec,
        out_shape=jax.ShapeDtypeStruct((n_blocks * PK_BLK_ROWS, LANES), U32),
        compiler_params=pltpu.CompilerParams(dimension_semantics=("arbitrary",),
                                             vmem_limit_bytes=VMEM_LIMIT),
        name="dispatch",
    )(npieces, n_used, ntail, lslot, piece_dst, piece_dst, tail_dst, h2t)


def _from_tiles(ref, n):
    return jnp.concatenate([ref[pl.ds(s, n, stride=TILE_ROWS), :] for s in range(TILE_ROWS)], axis=1)


def _experts_kernel(be_ref, nu_ref, xs_hbm, wg_ref, wu_ref, wd_ref, ys_hbm,
                    xbuf, ybuf, zbuf, wgb, wub, wdb, xsem, ysem, zsem, *, n_blocks):
    i = pl.program_id(0)
    n_used = nu_ref[0]
    rows = PK_BLK_ROWS
    half = rows // 2
    pairs = MOE_BLK // 2

    def halves(b):
        return [pl.ds(pl.multiple_of(b * rows + h * half, half), half) for h in range(2)]

    def x_copies(b, s):
        return [pltpu.make_async_copy(xs_hbm.at[src, :], xbuf.at[s, pl.ds(h * half, half), :], xsem.at[s])
                for h, src in enumerate(halves(b))]

    def y_copies(b, s):
        return [pltpu.make_async_copy(ybuf.at[s, pl.ds(h * half, half), :], ys_hbm.at[dst, :], ysem.at[s])
                for h, dst in enumerate(halves(b))]

    def start(copies):
        for h, c in enumerate(copies):
            c.start(priority=h)

    def wait(copies):
        for c in copies:
            c.wait()

    def zero_fill(b):
        return pltpu.make_async_copy(zbuf, ys_hbm.at[_rows(b, rows), :], zsem.at[0])

    @pl.when(i == 0)
    def _():
        start(x_copies(0, 0))

        @pl.when(n_used > 1)
        def _():
            start(x_copies(1, 1))

        zbuf[...] = jnp.zeros_like(zbuf)

        def fill(b, carry):
            zero_fill(b).start()
            return carry
        lax.fori_loop(n_used, n_blocks, fill, 0)

    @pl.when(i + 2 < n_used)
    def _():
        start(x_copies(i + 2, (i + 2) % 3))

    @pl.when(i < n_used)
    def _():
        wait(x_copies(i, i % 3))

        @pl.when(i >= 2)
        def _():
            wait(y_copies(i - 2, i % 2))

        @pl.when(jnp.logical_or(i == 0, be_ref[i] != be_ref[jnp.maximum(i - 1, 0)]))
        def _():
            wgb[...] = wg_ref[...].astype(BF16)
            wub[...] = wu_ref[...].astype(BF16)
            wdb[...] = wd_ref[...].astype(BF16)

        x = pltpu.bitcast(_from_tiles(xbuf.at[i % 3], pairs), BF16)
        g = _dot(x, wgb[...])
        u = _dot(x, wub[...])
        y = _dot((g * _sigmoid(g) * u).astype(BF16), wdb[...])
        yp = pltpu.bitcast(y.astype(BF16), U32)
        yb = ybuf.at[i % 2]
        for s in range(TILE_ROWS):
            yb[pl.ds(s, pairs, stride=TILE_ROWS), :] = yp[:, s * LANES:(s + 1) * LANES]
        start(y_copies(i, i % 2))

        @pl.when(i == n_used - 1)
        def _():
            @pl.when(i >= 1)
            def _():
                wait(y_copies(i - 1, (i + 1) % 2))
            wait(y_copies(i, i % 2))

            def fill_wait(b, carry):
                zero_fill(0).wait()
                return carry
            lax.fori_loop(n_used, n_blocks, fill_wait, 0)


def _experts(block_e, n_used, x_sorted, wg, wu, wd, layer, n_blocks):
    wspec = lambda shape: pl.BlockSpec((None, None) + shape, lambda i, be, nu: (layer, be[i], 0, 0))
    rows = PK_BLK_ROWS
    grid_spec = pltpu.PrefetchScalarGridSpec(
        num_scalar_prefetch=2,
        grid=(n_blocks,),
        in_specs=[
            pl.BlockSpec(memory_space=pl.ANY),
            wspec((D_MODEL, EXPERT_DIM)), wspec((D_MODEL, EXPERT_DIM)), wspec((EXPERT_DIM, D_MODEL)),
        ],
        out_specs=pl.BlockSpec(memory_space=pl.ANY),
        scratch_shapes=[
            pltpu.VMEM((3, rows, LANES), U32),
            pltpu.VMEM((2, rows, LANES), U32),
            pltpu.VMEM((rows, LANES), U32),
            pltpu.VMEM((D_MODEL, EXPERT_DIM), BF16),
            pltpu.VMEM((D_MODEL, EXPERT_DIM), BF16),
            pltpu.VMEM((EXPERT_DIM, D_MODEL), BF16),
            pltpu.SemaphoreType.DMA((3,)),
            pltpu.SemaphoreType.DMA((2,)),
            pltpu.SemaphoreType.DMA((1,)),
        ],
    )
    return pl.pallas_call(
        functools.partial(_experts_kernel, n_blocks=n_blocks),
        grid_spec=grid_spec,
        out_shape=jax.ShapeDtypeStruct((n_blocks * rows, LANES), U32),
        compiler_params=pltpu.CompilerParams(dimension_semantics=("arbitrary",)),
        name="experts",
    )(block_e, n_used, x_sorted, wg, wu, wd)


def _combine_kernel(np_ref, lp_ref, ws_ref, cur_ref, nx1_ref, nx2_ref, y_hbm, h2b_ref, xn_ref,
                    mod_ref, g_ref, swg_ref, swu_ref, swd_ref, o_ref, stg, acc, sem, *, n_steps):
    i = pl.program_id(0)
    slot = i % 3
    d = D_MODEL

    def piece_copy(s, p, src):
        dst = pl.ds(pl.multiple_of(s * PK_STG_ROWS + p * PK_PIECE_ROWS, PK_PIECE_ROWS), PK_PIECE_ROWS)
        return pltpu.make_async_copy(y_hbm.at[_piece(src), :], stg.at[dst, :], sem.at[s])

    def fetcher(s, pd_ref):
        return lambda p, prio: piece_copy(s, p, pd_ref[0, p]).start(priority=prio)

    @pl.when(i == 0)
    def _():
        _start_pieces(0, np_ref[0], fetcher(0, cur_ref))
        _start_pieces(0, np_ref[1], fetcher(1, nx1_ref))

    _wait_pieces(np_ref[i], PK_PIECE_ROWS, lambda rows: pltpu.make_async_copy(
        y_hbm.at[pl.ds(0, rows), :], stg.at[pl.ds(0, rows), :], sem.at[slot]))

    def gather_token(t):
        a = None
        for k in range(TOP_K):
            j = k * ROW_BLK + t
            pair = stg[_tile_at(lp_ref[0, j]), :]
            w = jnp.full((TILE_ROWS, LANES), ws_ref[0, j], F32)
            v = jnp.where(w < 0.0, _unpack_hi(pair), _unpack_lo(pair)) * jnp.abs(w)
            a = v if a is None else a + v
        acc[_rows(t, TILE_ROWS), :] = a

    np_next = jnp.where(i + 2 < n_steps, np_ref[jnp.minimum(i + 2, n_steps - 1)], 0)
    _interleave(gather_token, np_next, fetcher((i + 2) % 3, nx2_ref))

    hb = h2b_ref[...]
    g = _dot(hb, swg_ref[...])
    u = _dot(hb, swu_ref[...])
    f = _dot((g * _sigmoid(g) * u).astype(BF16), swd_ref[...]) + _from_tiles(acc, ROW_BLK)
    o_ref[...] = xn_ref[...] + mod_ref[:, 5 * d:6 * d] * _rms(f, g_ref[...])


def _combine(npieces, pair_row, w_signed, piece_src, y_sorted, h2b_rows, xn_rows, mod3, g, swg, swu, swd,
             n_steps, per_batch):
    assert n_steps >= 3
    last = n_steps - 1
    full = lambda shape: pl.BlockSpec(shape, lambda n, *_: (0,) * len(shape))
    rows = lambda w_: pl.BlockSpec((ROW_BLK, w_), lambda n, *_: (n, 0))
    smem = lambda w_, imap: pl.BlockSpec((None, 1, w_), imap, memory_space=pltpu.SMEM)
    n_lat = SEQ // ROW_BLK
    grid_spec = pltpu.PrefetchScalarGridSpec(
        num_scalar_prefetch=1,
        grid=(n_steps,),
        in_specs=[
            smem(TOP_K * ROW_BLK, lambda n, *_: (n, 0, 0)),
            smem(TOP_K * ROW_BLK, lambda n, *_: (n, 0, 0)),
            smem(PIECES_MAX, lambda n, *_: (n, 0, 0)),
            smem(PIECES_MAX, lambda n, *_: (jnp.minimum(n + 1, last), 0, 0)),
            smem(PIECES_MAX, lambda n, *_: (jnp.minimum(n + 2, last), 0, 0)),
            pl.BlockSpec(memory_space=pl.ANY),
            rows(D_MODEL), rows(D_MODEL),
            pl.BlockSpec((None, 1, 6 * D_MODEL),
                         lambda n, *_: (_mod_row(n % per_batch, n // per_batch, n_lat), 0, 0)),
            full((1, D_MODEL)),
            full((D_MODEL, SHARED_DIM)), full((D_MODEL, SHARED_DIM)), full((SHARED_DIM, D_MODEL)),
        ],
        out_specs=rows(D_MODEL),
        scratch_shapes=[
            pltpu.VMEM((3 * PK_STG_ROWS, LANES), U32),
            pltpu.VMEM((ROW_BLK * TILE_ROWS, LANES), F32),
            pltpu.SemaphoreType.DMA((3,)),
        ],
    )
    return pl.pallas_call(
        functools.partial(_combine_kernel, n_steps=n_steps),
        grid_spec=grid_spec,
        out_shape=jax.ShapeDtypeStruct(xn_rows.shape, F32),
        compiler_params=pltpu.CompilerParams(dimension_semantics=("arbitrary",),
                                             vmem_limit_bytes=VMEM_LIMIT),
        name="combine",
    )(npieces, pair_row, w_signed, piece_src, piece_src, piece_src, y_sorted, h2b_rows, xn_rows, mod3, g,
      swg, swu, swd)


def _moe_plan(cnt_rows, n_blocks):
    i32 = jnp.int32
    per_blk = MOE_BLK // PIECE
    to_expert = lambda v: v.reshape(v.shape[:-1] + (8, 8)).swapaxes(-1, -2).reshape(v.shape)
    runs = (cnt_rows + PIECE - 1) // PIECE
    loc_end = jnp.cumsum(runs, axis=1)
    loc = loc_end - runs
    npieces = loc_end[:, -1].astype(i32)
    reg = to_expert(runs.sum(axis=0))
    padded = (reg + per_blk - 1) // per_blk * per_blk
    pad_end = jnp.cumsum(padded)
    pad_start = pad_end - padded
    off = to_expert(pad_start)[None, :] + jnp.cumsum(runs, axis=0) - runs
    p = jnp.arange(PIECES_MAX, dtype=i32)[None, :, None]
    mine = (loc[:, None, :] <= p) & (p < loc_end[:, None, :])
    piece_dst = jnp.sum(jnp.where(mine, (off - loc)[:, None, :], 0), axis=-1) + p[:, :, 0]
    tail_n = padded - reg
    t_end = jnp.cumsum(tail_n)
    t_beg = t_end - tail_n
    q = jnp.arange(TAIL_MAX, dtype=i32)[:, None]
    tmine = (t_beg[None, :] <= q) & (q < t_end[None, :])
    tail_dst = jnp.sum(jnp.where(tmine, (pad_start + reg - t_beg)[None, :], 0), axis=-1) + q[:, 0]
    first = jnp.arange(n_blocks, dtype=i32)[:, None] * per_blk
    block_e = jnp.minimum(jnp.sum((pad_end[None, :] <= first).astype(i32), axis=-1), N_EXPERTS - 1)
    n_used = (pad_end[-1] // per_blk).astype(i32).reshape(1)
    return (npieces, piece_dst.astype(i32).reshape(-1, 1, PIECES_MAX),
            tail_dst.astype(i32).reshape(1, 1, TAIL_MAX), t_end[-1].astype(i32).reshape(1),
            block_e.astype(i32), n_used)


def _axial_tables(dim):
    f32 = np.float32
    t = np.arange(SEQ)
    row = (t // GRID_W).astype(f32)
    col = (t % GRID_W).astype(f32)
    quarter = dim // 4
    freqs = (f32(ROPE_THETA) ** (-np.arange(quarter, dtype=f32) / f32(quarter))).astype(f32)
    ar, ac = row[:, None] * freqs, col[:, None] * freqs
    ang = np.concatenate([ar, ar, ac, ac], axis=-1)
    sign = np.where((np.arange(dim) // quarter) % 2 == 0, -1.0, 1.0).astype(f32)
    cos = np.concatenate([np.cos(ang), np.ones((CTX_LEN, dim), f32)], axis=0)
    sin = np.concatenate([np.sin(ang) * sign, np.zeros((CTX_LEN, dim), f32)], axis=0)
    return cos.astype(f32), sin.astype(f32)


def _rope_tables():
    f32 = np.float32
    cos_h, sin_h = _axial_tables(HEAD_DIM)
    cos_r, sin_r = _axial_tables(B_ROPE)
    cosh, sinh = np.tile(cos_h, (1, 4)), np.tile(sin_h, (1, 4))
    scale_b = f32((B_NOPE + B_ROPE) ** -0.5)
    ones, zeros = np.ones((TOK, 64), f32), np.zeros((TOK, 64), f32)
    pad1, pad0 = np.ones((TOK, 32), f32), np.zeros((TOK, 32), f32)
    cosb = np.tile(np.concatenate([ones, cos_r, pad1], axis=1) * scale_b, (1, 4))
    sinb = np.tile(np.concatenate([zeros, sin_r, pad0], axis=1) * scale_b, (1, 4))
    cosr = np.concatenate([cos_r, np.ones((TOK, 96), f32)], axis=1)
    sinr = np.concatenate([sin_r, np.zeros((TOK, 96), f32)], axis=1)
    return tuple(jnp.asarray(a, F32) for a in (cosh, sinh, cosb, sinb, cosr, sinr))


_GQA_ORDER = (0, 2, 1, 3)


def _take_heads(w, base, order, axis):
    return jnp.concatenate(
        [lax.slice_in_dim(w, base + h * HEAD_DIM, base + (h + 1) * HEAD_DIM, axis=axis) for h in order],
        axis=axis)


def _w_in_layout(w_in):
    zeros = lambda n: jnp.zeros((D_MODEL, n), w_in.dtype)
    a_q = _take_heads(w_in, 0, _GQA_ORDER, 1)
    a_kv = w_in[:, 256:512]
    b_q = jnp.concatenate(
        [jnp.concatenate([w_in[:, 512 + h * 96:512 + (h + 1) * 96], zeros(32)], axis=1)
         for h in range(B_HEADS)], axis=1)
    b_c = w_in[:, 896:1024]
    b_r = jnp.concatenate([w_in[:, 1024:1056], zeros(96)], axis=1)
    c_q = _take_heads(w_in, 1056, _GQA_ORDER, 1)
    rest = w_in[:, 1312:2336]
    return jnp.concatenate([a_q, a_kv, b_q, b_c, b_r, c_q, rest], axis=1).astype(BF16)


def _w_kv_layout(w_kv_up):
    zeros = jnp.zeros((B_KV_RANK, 64), w_kv_up.dtype)
    nope = [jnp.concatenate([w_kv_up[:, h * 128:h * 128 + 64], zeros], axis=1) for h in range(B_HEADS)]
    val = [w_kv_up[:, h * 128 + 64:(h + 1) * 128] for h in range(B_HEADS)]
    return jnp.concatenate(nope + val, axis=1).astype(BF16)


def _w_out_layout(w_out):
    return jnp.concatenate([_take_heads(w_out, 0, _GQA_ORDER, 0), w_out[256:512],
                            _take_heads(w_out, 512, _GQA_ORDER, 0), w_out[768:1024]], axis=0).astype(BF16)


def _router_rows(v):
    return v.reshape((8, 8) + v.shape[1:]).swapaxes(0, 1).reshape(v.shape)


def kernel(x, c, ctx, c_ctx, ada_w, ada_b, pre_mix_g, post_mix_g, pre_ffn_g, post_ffn_g, w_in, a_q_norm, a_k_norm, b_kv_norm, b_w_kv_up, c_sink, d_rpb, w_out, router_w, router_b, exp_w_gate, exp_w_up, exp_w_down, sh_w_gate, sh_w_up, sh_w_down):
    bsz = x.shape[0]
    depth = ada_w.shape[0]
    xall = jnp.concatenate([x, ctx], axis=1)

    cvec = jnp.zeros((16, D_MODEL), F32).at[:bsz].set(c).at[8].set(c_ctx)
    mod = _ada(cvec, ada_w, ada_b)
    tabs = _rope_tables()
    avg = jnp.asarray(np.kron(np.eye(4, dtype=np.float32), np.full((64, 64), 1.0 / 64, np.float32)), BF16)
    row2 = lambda v: v.reshape(1, -1).astype(F32)

    for l in range(depth):
        with_ctx = l < depth - 1
        mod3 = mod[l].reshape(16, 1, 6 * D_MODEL)

        proj = _inproj(
            xall, mod3, row2(pre_mix_g[l]), _w_in_layout(w_in[l]),
            row2(jnp.tile(a_q_norm[l], 4) * HEAD_DIM ** -0.5), row2(jnp.tile(a_k_norm[l], 2)),
            row2(b_kv_norm[l]), _w_kv_layout(b_w_kv_up[l]), avg, tabs)
        qa, ka, va, qb, kb, vb, qc, kc, vc, qd, kd, vd = proj

        nq = NQ + (CTX_LEN // QBLK if with_ctx else 0)
        full = lambda shape: pl.BlockSpec(shape, lambda b, i: (0,) * len(shape))
        mix_a = _attn_call(_attn_a_kernel, "attn_a", qa, ka, va, [], [], nq // 2, qblk=2 * QBLK)
        mix_b = _attn_call(_attn_b_kernel, "attn_b", qb, kb, vb, [], [], nq // 2, qblk=2 * QBLK)
        sink = jnp.zeros((8, LANES), F32).at[:4].set(
            jnp.broadcast_to(c_sink[l][np.array(_GQA_ORDER)][:, None], (4, LANES)))
        mix_c = _attn_call(_attn_c_kernel, "attn_c", qc, kc, vc, [sink], [full((8, LANES))], nq)
        bias = _na_bias(d_rpb[l])
        mix_d = _attn_call(
            _attn_d_kernel, "attn_d", qd, kd, vd, [bias],
            [pl.BlockSpec((None, D_HEADS, QBLK, NA_WIN), lambda b, i: (_na_pattern(i), 0, 0, 0))], nq)

        nb = (TOK if with_ctx else SEQ) // ROW_BLK
        rb = jnp.broadcast_to(_router_rows(router_b[l])[:, None], (N_EXPERTS, ROW_BLK)).astype(F32)
        xn, h2b, h2t, wt_t, ls_t, cnt_t = _outproj(
            (mix_a, mix_b, mix_c, mix_d), _w_out_layout(w_out[l]), xall, mod3,
            row2(post_mix_g[l]), row2(pre_ffn_g[l]),
            _router_rows(router_w[l].T).astype(BF16), rb, nb)

        t_used = nb * ROW_BLK
        n_tok = bsz * t_used
        n_steps = bsz * nb
        n_blocks = -(-(n_tok * TOP_K + n_steps * N_EXPERTS * (PIECE - 1) + N_EXPERTS * (MOE_BLK - 1))
                     // MOE_BLK)
        npieces, piece_dst, tail_dst, ntail, block_e, n_used = _moe_plan(
            cnt_t[..., 0].reshape(n_steps, N_EXPERTS), n_blocks)
        lslot = ls_t.reshape(n_steps, 1, TOP_K * ROW_BLK)
        wts = wt_t.reshape(n_steps, 1, TOP_K * ROW_BLK)
        stg_row = lslot * TILE_ROWS
        ring = (jnp.arange(n_steps, dtype=jnp.int32) % 3)[:, None, None] * PK_STG_ROWS
        pair_row = (lslot // 2) * TILE_ROWS + ring
        w_signed = jnp.where(lslot % 2 == 0, wts, -wts)

        x_sorted = _dispatch(npieces, n_used, ntail, stg_row, piece_dst, tail_dst,
                             h2t.reshape(n_tok * TILE_ROWS, LANES), n_steps, n_blocks)
        y_sorted = _experts(block_e, n_used, x_sorted, exp_w_gate, exp_w_up, exp_w_down, l, n_blocks)
        xall = _combine(
            npieces, pair_row, w_signed, piece_dst, y_sorted, h2b.reshape(n_tok, D_MODEL),
            xn.reshape(n_tok, D_MODEL), mod3, row2(post_ffn_g[l]),
            sh_w_gate[l].astype(BF16), sh_w_up[l].astype(BF16), sh_w_down[l].astype(BF16),
            n_steps, nb).reshape(bsz, t_used, D_MODEL)

    return xall[:, :SEQ]
```

```python
import functools

import numpy as np
import jax
import jax.numpy as jnp
from jax import lax
from jax.experimental import pallas as pl
from jax.experimental.pallas import tpu as pltpu

F32 = jnp.float32
BF16 = jnp.bfloat16
U32 = jnp.uint32

D_MODEL = 1024
SEQ = 2048
CTX_LEN = 256
TOK = SEQ + CTX_LEN
GRID_W = 64
HEAD_DIM = 64
ROPE_THETA = 10000.0
EPS = 1e-6
NEG = -1e30

A_HEADS, A_KV_HEADS = 4, 2
B_HEADS, B_NOPE, B_ROPE, B_V, B_KV_RANK = 4, 64, 32, 64, 128
C_HEADS, C_KV_HEADS, C_WINDOW = 4, 2, 128
D_HEADS, NA_ROWS, NA_COLS = 4, 8, 16

N_EXPERTS, TOP_K, N_GROUPS, TOPK_GROUPS = 64, 8, 8, 4
EXPERT_DIM, SHARED_DIM = 256, 256
ROUTED_SCALE = 2.5

LANES = 128
QBLK = 128
ROW_BLK = 256
IN_BLK = 768
MOE_BLK = 768
TILE_ROWS = D_MODEL // LANES
PIECE = 8
PIECE_ROWS = PIECE * TILE_ROWS
PIECES_MAX = 320
STG_ROWS = PIECES_MAX * PIECE_ROWS
PK_PIECE_ROWS = PIECE_ROWS // 2
PK_STG_ROWS = STG_ROWS // 2
PK_BLK_ROWS = MOE_BLK * TILE_ROWS // 2
TAIL_MAX = N_EXPERTS * MOE_BLK // PIECE
LOG_PACK_GROUP = 2
PACK_GROUP = 1 << LOG_PACK_GROUP
LOG_WAIT_GROUP = 4
WAIT_GROUP = 1 << LOG_WAIT_GROUP
LOG_CHUNKS = 2
N_CHUNKS = 1 << LOG_CHUNKS
VMEM_LIMIT = 56 * 1024 * 1024
NQ = SEQ // QBLK
NA_WIN = 10 * GRID_W

_AQ, _AK, _AV, _BQ, _BC, _BR = 0, 256, 384, 512, 1024, 1152
_CQ, _CK, _CV, _DQ, _DK, _DV = 1280, 1536, 1664, 1792, 2048, 2304
IN_W = 2560

_NT = (((1,), (1,)), ((), ()))


def _sigmoid(x):
    return 1.0 / (1.0 + jnp.exp(-x))


def _rms(x, g):
    return x * lax.rsqrt(jnp.mean(x * x, axis=-1, keepdims=True) + EPS) * g


def _dot(a, b):
    return jnp.dot(a, b, preferred_element_type=F32)


def _dot_nt(a, b):
    return lax.dot_general(a, b, _NT, preferred_element_type=F32)


def _ada_kernel(c_ref, w_ref, b_ref, o_ref):
    c = c_ref[...]
    a = (c * _sigmoid(c)).astype(BF16)
    o_ref[...] = _dot(a, w_ref[...].astype(BF16)) + b_ref[...]


def _ada(cvec, ada_w, ada_b):
    depth = ada_w.shape[0]
    n = ada_w.shape[2]
    tn = 1536
    return pl.pallas_call(
        _ada_kernel,
        grid=(depth, n // tn),
        in_specs=[
            pl.BlockSpec((16, D_MODEL), lambda l, j: (0, 0)),
            pl.BlockSpec((None, D_MODEL, tn), lambda l, j: (l, 0, j)),
            pl.BlockSpec((None, 1, tn), lambda l, j: (l, 0, j)),
        ],
        out_specs=pl.BlockSpec((None, 16, tn), lambda l, j: (l, 0, j)),
        out_shape=jax.ShapeDtypeStruct((depth, 16, n), F32),
        name="ada",
    )(cvec, ada_w, ada_b.reshape(depth, 1, n))


def _rope(x, cos, sin, chunk):
    outs = []
    for c in range(x.shape[1] // LANES):
        sl = slice(c * LANES, (c + 1) * LANES)
        xs = x[:, sl]
        lane = lax.broadcasted_iota(jnp.int32, xs.shape, 1)
        first = (lane & chunk) == 0
        rot = jnp.where(first, pltpu.roll(xs, LANES - chunk, 1), pltpu.roll(xs, chunk, 1))
        outs.append(xs * cos[:, sl] + rot * sin[:, sl])
    return outs[0] if len(outs) == 1 else jnp.concatenate(outs, axis=1)


def _head_norm(x, avg):
    sq = x * x
    hi = sq.astype(BF16)
    lo = (sq - hi.astype(F32)).astype(BF16)
    ms = _dot(hi, avg) + _dot(lo, avg)
    return x * lax.rsqrt(ms + EPS)


def _inproj_kernel(x_ref, mod_ref, modc_ref, g_ref, w_ref, qg_ref, kg_ref, kvg_ref, wkv_ref, avg_ref,
                   cosh_ref, sinh_ref, cosb_ref, sinb_ref, cosr_ref, sinr_ref,
                   qa_ref, ka_ref, va_ref, qb_ref, kb_ref, vb_ref,
                   qc_ref, kc_ref, vc_ref, qd_ref, kd_ref, vd_ref):
    d = D_MODEL
    n = x_ref.shape[0]
    is_ctx = pl.program_id(1) * n + lax.broadcasted_iota(jnp.int32, (n, 1), 0) >= SEQ
    scale = jnp.where(is_ctx, modc_ref[:, d:2 * d], mod_ref[:, d:2 * d])
    shift = jnp.where(is_ctx, modc_ref[:, 0:d], mod_ref[:, 0:d])
    h = _rms(x_ref[...], g_ref[...]) * (1.0 + scale) + shift
    p = _dot(h.astype(BF16), w_ref[...])
    cosh, sinh = cosh_ref[...], sinh_ref[...]
    avg = avg_ref[...]

    qa = _head_norm(p[:, _AQ:_AQ + 256], avg) * qg_ref[...]
    qa_ref[...] = _rope(qa, cosh, sinh, 16).astype(BF16)
    ka = _head_norm(p[:, _AK:_AK + 128], avg[0:128, 0:128]) * kg_ref[...]
    ka_ref[...] = _rope(ka, cosh[:, 0:128], sinh[:, 0:128], 16).astype(BF16)
    ones = jnp.ones((p.shape[0], LANES), BF16)
    va_ref[...] = jnp.concatenate([p[:, _AV:_AV + 128].astype(BF16), ones], axis=1)

    qb_ref[...] = _rope(p[:, _BQ:_BQ + 512], cosb_ref[...], sinb_ref[...], 8).astype(BF16)
    cn = _rms(p[:, _BC:_BC + 128], kvg_ref[...])
    kv = _dot(cn.astype(BF16), wkv_ref[...])
    kr = _rope(p[:, _BR:_BR + 128], cosr_ref[...], sinr_ref[...], 8)
    kr = pltpu.roll(kr, 64, 1)
    kb_ref[...] = jnp.concatenate(
        [kv[:, hh * 128:(hh + 1) * 128] + kr for hh in range(B_HEADS)], axis=1).astype(BF16)
    vb_ref[...] = jnp.concatenate(
        [kv[:, 512:640].astype(BF16), ones, kv[:, 640:768].astype(BF16), ones], axis=1)

    qc_ref[...] = _rope(p[:, _CQ:_CQ + 256] * 0.125, cosh, sinh, 16).astype(BF16)
    kc_ref[...] = _rope(p[:, _CK:_CK + 128], cosh[:, 0:128], sinh[:, 0:128], 16).astype(BF16)
    vc_ref[...] = jnp.concatenate([p[:, _CV:_CV + 128].astype(BF16), ones], axis=1)

    qd_ref[...] = (p[:, _DQ:_DQ + 256] * 0.125).astype(BF16)
    kd_ref[...] = p[:, _DK:_DK + 256].astype(BF16)
    vd_ref[...] = jnp.concatenate(
        [p[:, _DV:_DV + 128].astype(BF16), ones, p[:, _DV + 128:_DV + 256].astype(BF16), ones], axis=1)


def _mod_row(j, b, n_lat_blocks):
    return jnp.where(j < n_lat_blocks, b, 8)


def _inproj(xall, mod3, g, w, qg, kg, kvg, wkv, avg, tabs):
    bsz = xall.shape[0]
    nb = TOK // IN_BLK
    full = lambda shape: pl.BlockSpec(shape, lambda b, j: (0,) * len(shape))
    row = lambda w_: pl.BlockSpec((IN_BLK, w_), lambda b, j: (j, 0))
    out = lambda w_: pl.BlockSpec((None, IN_BLK, w_), lambda b, j: (b, j, 0))
    widths = (256, 128, 256, 512, 512, 512, 256, 128, 256, 256, 256, 512)
    return pl.pallas_call(
        _inproj_kernel,
        grid=(bsz, nb),
        in_specs=[
            pl.BlockSpec((None, IN_BLK, D_MODEL), lambda b, j: (b, j, 0)),
            pl.BlockSpec((None, 1, 6 * D_MODEL), lambda b, j: (b, 0, 0)),
            pl.BlockSpec((None, 1, 6 * D_MODEL), lambda b, j: (8, 0, 0)),
            full((1, D_MODEL)), full((D_MODEL, IN_W)),
            full((1, 256)), full((1, 128)), full((1, 128)), full((128, 768)), full((256, 256)),
            row(256), row(256), row(512), row(512), row(128), row(128),
        ],
        out_specs=[out(w_) for w_ in widths],
        out_shape=[jax.ShapeDtypeStruct((bsz, TOK, w_), BF16) for w_ in widths],
        compiler_params=pltpu.CompilerParams(vmem_limit_bytes=VMEM_LIMIT),
        name="inproj",
    )(xall, mod3, mod3, g, w, qg, kg, kvg, wkv, avg, *tabs)


def _softmax_pv(segs, sink=None):
    m = None
    for s, _ in segs:
        ms = jnp.max(s, axis=-1, keepdims=True)
        m = ms if m is None else jnp.maximum(m, ms)
    if sink is not None:
        m = jnp.maximum(m, sink)
    acc = None
    for s, v1 in segs:
        r = _dot(jnp.exp((s - m).astype(BF16)), v1)
        acc = r if acc is None else acc + r
    l = acc[:, LANES:LANES + 1]
    if sink is not None:
        l = l + jnp.exp(sink - m)
    return acc[:, 0:LANES] / l


def _lane_lo(dtype_shape):
    return lax.broadcasted_iota(jnp.int32, dtype_shape, 1) < HEAD_DIM


def _half(qt, hh):
    lo = _lane_lo(qt.shape)
    return jnp.where(lo if hh == 0 else jnp.logical_not(lo), qt, jnp.zeros_like(qt))


def _merge_halves(o0, o1):
    return jnp.where(_lane_lo(o0.shape), o0, o1)


def _stack_heads(q_ref, tiles):
    return jnp.concatenate(
        [_half(q_ref[:, t * LANES:(t + 1) * LANES], hh) for t in tiles for hh in range(2)], axis=0)


def _store_heads(o_ref, o, tiles):
    n = o_ref.shape[0]
    for ti, t in enumerate(tiles):
        o0, o1 = o[2 * ti * n:(2 * ti + 1) * n], o[(2 * ti + 1) * n:(2 * ti + 2) * n]
        o_ref[:, t * LANES:(t + 1) * LANES] = _merge_halves(o0, o1).astype(o_ref.dtype)


def _attn_a_kernel(q_ref, k_ref, v_ref, o_ref, *, n_lat):
    def run(k, v1):
        for t in range(2):
            o = [_softmax_pv([(_dot_nt(_half(q_ref[:, t * LANES:(t + 1) * LANES], hh), k), v1)])
                 for hh in range(2)]
            o_ref[:, t * LANES:(t + 1) * LANES] = _merge_halves(o[0], o[1]).astype(o_ref.dtype)

    @pl.when(pl.program_id(1) < n_lat)
    def _():
        run(k_ref[...], v_ref[...])

    @pl.when(pl.program_id(1) >= n_lat)
    def _():
        run(k_ref[SEQ:TOK, :], v_ref[SEQ:TOK, :])


def _attn_b_kernel(q_ref, k_ref, v_ref, o_ref, *, n_lat):
    def run(lo_, hi_):
        for t in range(2):
            v1 = v_ref[lo_:hi_, t * 256:(t + 1) * 256]
            o = []
            for hh in range(2):
                hd = 2 * t + hh
                q = q_ref[:, hd * LANES:(hd + 1) * LANES]
                k = k_ref[lo_:hi_, hd * LANES:(hd + 1) * LANES]
                o.append(_softmax_pv([(_dot_nt(q, k), v1)]))
            o_ref[:, t * LANES:(t + 1) * LANES] = _merge_halves(o[0], o[1]).astype(o_ref.dtype)

    @pl.when(pl.program_id(1) < n_lat)
    def _():
        run(0, TOK)

    @pl.when(pl.program_id(1) >= n_lat)
    def _():
        run(SEQ, TOK)


def _attn_c_kernel(q_ref, k_ref, v_ref, sink_ref, o_ref, *, n_lat):
    i = pl.program_id(1)
    qblk = o_ref.shape[0]
    band = 3 * qblk

    def run(latent):
        if latent:
            start = pl.multiple_of(jnp.clip((i - 1) * qblk, 0, SEQ - band), qblk)
            shape = (2 * qblk, band)
            qpos = i * qblk + (lax.broadcasted_iota(jnp.int32, shape, 0) & (qblk - 1))
            kpos = start + lax.broadcasted_iota(jnp.int32, shape, 1)
            valid = jnp.abs(qpos - kpos) <= C_WINDOW
        for t in range(2):
            q = _stack_heads(q_ref, (t,))
            segs = [(_dot_nt(q, k_ref[SEQ:TOK, :]), v_ref[SEQ:TOK, :])]
            if latent:
                s = jnp.where(valid, _dot_nt(q, k_ref[pl.ds(start, band), :]), NEG)
                segs.insert(0, (s, v_ref[pl.ds(start, band), :]))
            sink = jnp.concatenate(
                [jnp.broadcast_to(sink_ref[j:j + 1, 0:1], (qblk, 1)) for j in (2 * t, 2 * t + 1)], axis=0)
            _store_heads(o_ref, _softmax_pv(segs, sink=sink), (t,))

    @pl.when(i < n_lat)
    def _():
        run(True)

    @pl.when(i >= n_lat)
    def _():
        run(False)


def _attn_d_kernel(q_ref, k_ref, v_ref, bias_ref, o_ref, *, n_lat):
    i = pl.program_id(1)
    qblk = o_ref.shape[0]

    def run(latent):
        if latent:
            start = pl.multiple_of(jnp.clip((i - 2) * qblk, 0, SEQ - NA_WIN), qblk)
        for t in range(2):
            ksl, vsl = slice(t * LANES, (t + 1) * LANES), slice(t * 256, (t + 1) * 256)
            q = _stack_heads(q_ref, (t,))
            segs = [(_dot_nt(q, k_ref[SEQ:TOK, ksl]), v_ref[SEQ:TOK, vsl])]
            if latent:
                bias = bias_ref[2 * t:2 * t + 2].reshape(2 * qblk, NA_WIN)
                segs.insert(0, (_dot_nt(q, k_ref[pl.ds(start, NA_WIN), ksl]) + bias,
                                v_ref[pl.ds(start, NA_WIN), vsl]))
            _store_heads(o_ref, _softmax_pv(segs), (t,))

    @pl.when(i < n_lat)
    def _():
        run(True)

    @pl.when(i >= n_lat)
    def _():
        run(False)


def _attn_call(kernel, name, q, k, v, extra, extra_specs, n_blocks, qblk=QBLK):
    bsz = q.shape[0]
    return pl.pallas_call(
        functools.partial(kernel, n_lat=SEQ // qblk),
        grid=(bsz, n_blocks),
        in_specs=[
            pl.BlockSpec((None, qblk, q.shape[2]), lambda b, i: (b, i, 0)),
            pl.BlockSpec((None, TOK, k.shape[2]), lambda b, i: (b, 0, 0)),
            pl.BlockSpec((None, TOK, v.shape[2]), lambda b, i: (b, 0, 0)),
        ] + extra_specs,
        out_specs=pl.BlockSpec((None, qblk, 256), lambda b, i: (b, i, 0)),
        out_shape=jax.ShapeDtypeStruct((bsz, n_blocks * qblk, 256), BF16),
        name=name,
    )(q, k, v, *extra)


def _na_pattern(i):
    return jnp.where(i < 2, i, jnp.where(i < NQ - 2, 2, jnp.minimum(i, NQ - 1) - (NQ - 5)))


def _na_bias(rpb):
    rows = SEQ // GRID_W
    n_dr, n_dc = 2 * NA_ROWS - 1, 2 * NA_COLS - 1
    qc = np.arange(GRID_W)
    dc = np.clip(qc[None, :] - qc[:, None] + NA_COLS - 1, 0, n_dc - 1)
    sel_c = jnp.asarray(np.eye(n_dc, dtype=np.float32)[dc])
    cs = np.clip(qc - NA_COLS // 2, 0, GRID_W - NA_COLS)[:, None]
    col_ok = (qc[None, :] >= cs) & (qc[None, :] < cs + NA_COLS)
    sel_r, row_ok = [], []
    for i in (0, 1, 2, NQ - 2, NQ - 1):
        start_row = min(max(2 * i - 4, 0), rows - 10)
        qr = 2 * i + np.arange(2)
        kr = start_row + np.arange(10)
        rs = np.clip(qr - NA_ROWS // 2, 0, rows - NA_ROWS)[:, None]
        row_ok.append((kr[None] >= rs) & (kr[None] < rs + NA_ROWS))
        dr = np.clip(kr[None] - qr[:, None] + NA_ROWS - 1, 0, n_dr - 1)
        sel_r.append(np.eye(n_dr, dtype=np.float32)[dr])
    b = jnp.einsum("hrc,pabr,qkc->phaqbk", rpb.astype(F32), jnp.asarray(np.stack(sel_r)), sel_c,
                   precision=lax.Precision.HIGHEST)
    valid = np.stack(row_ok)[:, None, :, None, :, None] & col_ok[None, None, None, :, None, :]
    return jnp.where(valid, b, NEG).reshape(5, D_HEADS, QBLK, NA_WIN)


def _outproj_kernel(ma_ref, mb_ref, mc_ref, md_ref, w_ref, x_ref, mod_ref, g1_ref, g2_ref,
                    wr_ref, rb_ref, xn_ref, h2b_ref, h2t_ref, wt_ref, ls_ref, cnt_ref):
    d = D_MODEL
    mix = jnp.concatenate([ma_ref[...], mb_ref[...], mc_ref[...], md_ref[...]], axis=1)
    y = _rms(_dot(mix, w_ref[...]), g1_ref[...])
    xn = x_ref[...] + mod_ref[:, 2 * d:3 * d] * y
    xn_ref[...] = xn
    h2 = _rms(xn, g2_ref[...]) * (1.0 + mod_ref[:, 4 * d:5 * d]) + mod_ref[:, 3 * d:4 * d]
    n = h2.shape[0]
    h2b_ref[...] = h2.astype(BF16)
    for s in range(TILE_ROWS):
        h2t_ref[pl.ds(s, n, stride=TILE_ROWS), :] = h2[:, s * LANES:(s + 1) * LANES]

    s = _sigmoid(_dot_nt(wr_ref[...], h2.astype(BF16)))
    sb = s + rb_ref[...]
    per = N_EXPERTS // N_GROUPS
    slab = [sb[j * 8:(j + 1) * 8, :] for j in range(per)]
    sraw = [s[j * 8:(j + 1) * 8, :] for j in range(per)]
    giota = lax.broadcasted_iota(jnp.int32, (N_GROUPS, n), 0)
    eid = [giota * per + j for j in range(per)]
    ninf = -jnp.inf

    m1 = functools.reduce(jnp.maximum, slab)
    j1 = functools.reduce(jnp.minimum, [jnp.where(slab[j] == m1, j, per) for j in range(per)])
    m2 = functools.reduce(jnp.maximum, [jnp.where(j1 == j, ninf, slab[j]) for j in range(per)])
    gs = m1 + m2

    gsel = jnp.zeros((N_GROUPS, n), jnp.bool_)
    for _ in range(TOPK_GROUPS):
        m = jnp.max(gs, axis=0, keepdims=True)
        gi = jnp.min(jnp.where(gs == m, giota, N_GROUPS), axis=0, keepdims=True)
        hit = giota == gi
        gsel = jnp.logical_or(gsel, hit)
        gs = jnp.where(hit, ninf, gs)
    slab = [jnp.where(gsel, sj, ninf) for sj in slab]

    def pick(hits, vals):
        return jnp.sum(functools.reduce(jnp.add, [jnp.where(hits[j], vals[j], 0.0) for j in range(per)]),
                       axis=0, keepdims=True)

    kiota = lax.broadcasted_iota(jnp.int32, (TOP_K, n), 0)
    wts = jnp.zeros((TOP_K, n), F32)
    chosen = []
    for k in range(TOP_K):
        m = jnp.max(functools.reduce(jnp.maximum, slab), axis=0, keepdims=True)
        cand = functools.reduce(jnp.minimum,
                                [jnp.where(slab[j] == m, eid[j], N_EXPERTS) for j in range(per)])
        ei = jnp.min(cand, axis=0, keepdims=True)
        hits = [eid[j] == ei for j in range(per)]
        slab = [jnp.where(hits[j], ninf, slab[j]) for j in range(per)]
        wts = jnp.where(kiota == k, pick(hits, sraw), wts)
        chosen.append(ei)
    wt_ref[...] = wts / jnp.sum(wts, axis=0, keepdims=True) * ROUTED_SCALE

    hits_k = [[eid[j] == chosen[k] for j in range(per)] for k in range(TOP_K)]
    sel = jnp.concatenate(
        [functools.reduce(jnp.add, [jnp.where(hits_k[k][j], 1.0, 0.0) for k in range(TOP_K)])
         for j in range(per)], axis=0)
    tri = jnp.where(lax.broadcasted_iota(jnp.int32, (n, n), 0) < lax.broadcasted_iota(jnp.int32, (n, n), 1),
                    1.0, 0.0).astype(BF16)
    before = _dot(sel.astype(BF16), tri)
    cnt = jnp.sum(sel, axis=1, keepdims=True)
    run = jnp.floor((cnt + (PIECE - 1.0)) * (1.0 / PIECE)) * PIECE
    low = jnp.where(lax.broadcasted_iota(jnp.int32, (N_EXPERTS, N_EXPERTS), 1)
                    < lax.broadcasted_iota(jnp.int32, (N_EXPERTS, N_EXPERTS), 0), 1.0, 0.0).astype(BF16)
    start = _dot(low, jnp.broadcast_to(run, (N_EXPERTS, LANES)).astype(BF16))[:, 0:1]
    base = before + start
    base = [base[j * 8:(j + 1) * 8, :] for j in range(per)]
    lsl = jnp.zeros((TOP_K, n), F32)
    for k in range(TOP_K):
        lsl = jnp.where(kiota == k, pick(hits_k[k], base), lsl)
    ls_ref[...] = lsl.astype(jnp.int32)
    cnt_ref[...] = jnp.broadcast_to(cnt, (N_EXPERTS, LANES)).astype(jnp.int32)


def _outproj(mixes, w, xall, mod3, g1, g2, wr, rb, nb):
    bsz = xall.shape[0]
    n_lat = SEQ // ROW_BLK
    full = lambda shape: pl.BlockSpec(shape, lambda b, j: (0,) * len(shape))
    rows = lambda w_: pl.BlockSpec((None, ROW_BLK, w_), lambda b, j: (b, j, 0))
    route = pl.BlockSpec((None, None, TOP_K, ROW_BLK), lambda b, j: (b, j, 0, 0))
    return pl.pallas_call(
        _outproj_kernel,
        grid=(bsz, nb),
        in_specs=[rows(256)] * 4 + [
            full((D_MODEL, D_MODEL)), rows(D_MODEL),
            pl.BlockSpec((None, 1, 6 * D_MODEL), lambda b, j: (_mod_row(j, b, n_lat), 0, 0)),
            full((1, D_MODEL)), full((1, D_MODEL)),
            full((N_EXPERTS, D_MODEL)), full((N_EXPERTS, ROW_BLK)),
        ],
        out_specs=[
            rows(D_MODEL), rows(D_MODEL),
            pl.BlockSpec((None, ROW_BLK * TILE_ROWS, LANES), lambda b, j: (b, j, 0)),
            route, route,
            pl.BlockSpec((None, None, N_EXPERTS, LANES), lambda b, j: (b, j, 0, 0)),
        ],
        out_shape=[
            jax.ShapeDtypeStruct((bsz, nb * ROW_BLK, D_MODEL), F32),
            jax.ShapeDtypeStruct((bsz, nb * ROW_BLK, D_MODEL), BF16),
            jax.ShapeDtypeStruct((bsz, nb * ROW_BLK * TILE_ROWS, LANES), F32),
            jax.ShapeDtypeStruct((bsz, nb, TOP_K, ROW_BLK), F32),
            jax.ShapeDtypeStruct((bsz, nb, TOP_K, ROW_BLK), jnp.int32),
            jax.ShapeDtypeStruct((bsz, nb, N_EXPERTS, LANES), jnp.int32),
        ],
        name="outproj",
    )(*mixes, w, xall, mod3, g1, g2, wr, rb)


def _rows(i, n):
    return pl.ds(pl.multiple_of(i * n, n), n)


def _tile_at(row):
    return pl.ds(pl.multiple_of(row, TILE_ROWS), TILE_ROWS)


def _piece(p):
    return _rows(p, PK_PIECE_ROWS)


def _pack_pairs(lo, hi):
    return (lax.shift_right_logical(pltpu.bitcast(lo, U32), jnp.uint32(16))
            | (pltpu.bitcast(hi, U32) & jnp.uint32(0xFFFF0000)))


def _unpack_lo(u):
    return pltpu.bitcast(lax.shift_left(u, jnp.uint32(16)), F32)


def _unpack_hi(u):
    return pltpu.bitcast(u & jnp.uint32(0xFFFF0000), F32)


def _start_pieces(lo, hi, start):
    n = hi - lo

    def quad(q, carry):
        for r in range(4):
            start(lo + 4 * q + r, r % 2)
        return carry
    lax.fori_loop(0, lax.shift_right_logical(n, 2), quad, 0)

    def rest(r, carry):
        start(hi - 1 - r, 0)
        return carry
    lax.fori_loop(0, n & 3, rest, 0)


def _wait_pieces(n, piece_rows, copy_of_rows):
    def group(p, carry):
        copy_of_rows(WAIT_GROUP * piece_rows).wait()
        return carry
    lax.fori_loop(0, lax.shift_right_logical(n, LOG_WAIT_GROUP), group, 0)

    def single(p, carry):
        copy_of_rows(piece_rows).wait()
        return carry
    lax.fori_loop(0, n & (WAIT_GROUP - 1), single, 0)


def _interleave(body, n_pieces, start):
    per = ROW_BLK // N_CHUNKS

    def chunk(c, carry):
        for tt in range(per):
            body(c * per + tt)
        _start_pieces(lax.shift_right_logical(c * n_pieces, LOG_CHUNKS),
                      lax.shift_right_logical((c + 1) * n_pieces, LOG_CHUNKS), start)
        return carry
    lax.fori_loop(0, N_CHUNKS, chunk, 0)


def _dispatch_kernel(np_ref, nu_ref, nt_ref, ls_ref, pd_ref, pdp_ref, td_ref, h_ref, xs_hbm,
                     stg, pk, zblk, sem, zsem, *, n_steps, n_blocks):
    i = pl.program_id(0)
    slot = i % 3
    prev = (i + 2) % 3

    def piece_copy(s, p, d):
        return pltpu.make_async_copy(pk.at[s, _piece(p), :], xs_hbm.at[_piece(d), :], sem.at[s])

    def wait_pieces(s, n):
        _wait_pieces(n, PK_PIECE_ROWS, lambda rows: pltpu.make_async_copy(
            pk.at[s, pl.ds(0, rows), :], xs_hbm.at[pl.ds(0, rows), :], sem.at[s]))

    def zero_piece_copy(d):
        return pltpu.make_async_copy(zblk.at[pl.ds(0, PK_PIECE_ROWS), :], xs_hbm.at[_piece(d), :], zsem.at[0])

    def zero_block_copy(b):
        return pltpu.make_async_copy(zblk, xs_hbm.at[_rows(b, PK_BLK_ROWS), :], zsem.at[1])

    @pl.when(i == 0)
    def _():
        zblk[...] = jnp.zeros_like(zblk)

        def zero(c, carry):
            stg[_rows(c, 2048), :] = jnp.zeros((2048, LANES), F32)
            return carry
        lax.fori_loop(0, STG_ROWS // 2048, zero, 0)

        def blk(b, carry):
            zero_block_copy(b).start()
            return carry
        lax.fori_loop(nu_ref[0], n_blocks, blk, 0)

    @pl.when(i >= 3)
    def _():
        wait_pieces(slot, np_ref[jnp.maximum(i - 3, 0)])

    def scatter_token(t):
        tile = h_ref[_rows(t, TILE_ROWS), :].astype(BF16).astype(F32)
        for k in range(TOP_K):
            stg[_tile_at(ls_ref[0, k * ROW_BLK + t]), :] = tile

    np_prev = jnp.where(i >= 1, np_ref[jnp.maximum(i - 1, 0)], 0)
    _interleave(scatter_token, np_prev,
                lambda p, prio: piece_copy(prev, p, pdp_ref[0, p]).start(priority=prio))

    def pack_group(q, carry):
        v = stg[_rows(q, PACK_GROUP * PIECE_ROWS), :].reshape(PACK_GROUP * PIECE // 2, 2 * TILE_ROWS, LANES)
        packed = _pack_pairs(v[:, 0:TILE_ROWS, :], v[:, TILE_ROWS:2 * TILE_ROWS, :])
        pk[slot, _rows(q, PACK_GROUP * PK_PIECE_ROWS), :] = packed.reshape(PACK_GROUP * PK_PIECE_ROWS, LANES)
        return carry
    lax.fori_loop(0, lax.shift_right_logical(np_ref[i] + (PACK_GROUP - 1), LOG_PACK_GROUP), pack_group, 0)

    @pl.when(i == n_steps - 1)
    def _():
        _start_pieces(0, np_ref[i], lambda p, prio: piece_copy(slot, p, pd_ref[0, p]).start(priority=prio))

        def tail(p, carry):
            zero_piece_copy(td_ref[0, p]).start()
            return carry
        lax.fori_loop(0, nt_ref[0], tail, 0)

        wait_pieces((i + 1) % 3, np_ref[jnp.maximum(i - 2, 0)])
        wait_pieces(prev, np_ref[jnp.maximum(i - 1, 0)])
        wait_pieces(slot, np_ref[i])

        def tail_wait(p, carry):
            zero_piece_copy(0).wait()
            return carry
        lax.fori_loop(0, nt_ref[0], tail_wait, 0)

        def blk_wait(b, carry):
            zero_block_copy(0).wait()
            return carry
        lax.fori_loop(nu_ref[0], n_blocks, blk_wait, 0)


def _dispatch(npieces, n_used, ntail, lslot, piece_dst, tail_dst, h2t, n_steps, n_blocks):
    assert n_steps >= 3
    smem = lambda n, imap: pl.BlockSpec((None, 1, n), imap, memory_space=pltpu.SMEM)
    grid_spec = pltpu.PrefetchScalarGridSpec(
        num_scalar_prefetch=3,
        grid=(n_steps,),
        in_specs=[
            smem(TOP_K * ROW_BLK, lambda i, *_: (i, 0, 0)),
            smem(PIECES_MAX, lambda i, *_: (i, 0, 0)),
            smem(PIECES_MAX, lambda i, *_: (jnp.maximum(i - 1, 0), 0, 0)),
            smem(TAIL_MAX, lambda i, *_: (0, 0, 0)),
            pl.BlockSpec((ROW_BLK * TILE_ROWS, LANES), lambda i, *_: (i, 0)),
        ],
        out_specs=pl.BlockSpec(memory_space=pl.ANY),
        scratch_shapes=[
            pltpu.VMEM((STG_ROWS, LANES), F32),
            pltpu.VMEM((3, PK_STG_ROWS, LANES), U32),
            pltpu.VMEM((PK_BLK_ROWS, LANES), U32),
            pltpu.SemaphoreType.DMA((3,)),
            pltpu.SemaphoreType.DMA((2,)),
        ],
    )
    return pl.pallas_call(
        functools.partial(_dispatch_kernel, n_steps=n_steps, n_blocks=n_blocks),
        grid_spec=grid_spec,
        out_shape=jax.ShapeDtypeStruct((n_blocks * PK_BLK_ROWS, LANES), U32),
        compiler_params=pltpu.CompilerParams(dimension_semantics=("arbitrary",),
                                             vmem_limit_bytes=VMEM_LIMIT),
        name="dispatch",
    )(npieces, n_used, ntail, lslot, piece_dst, piece_dst, tail_dst, h2t)


def _from_tiles(ref, n):
    return jnp.concatenate([ref[pl.ds(s, n, stride=TILE_ROWS), :] for s in range(TILE_ROWS)], axis=1)


def _experts_kernel(be_ref, nu_ref, xs_hbm, wg_ref, wu_ref, wd_ref, ys_hbm,
                    xbuf, ybuf, zbuf, wgb, wub, wdb, xsem, ysem, zsem, *, n_blocks):
    i = pl.program_id(0)
    n_used = nu_ref[0]
    rows = PK_BLK_ROWS
    half = rows // 2
    pairs = MOE_BLK // 2

    def halves(b):
        return [pl.ds(pl.multiple_of(b * rows + h * half, half), half) for h in range(2)]

    def x_copies(b, s):
        return [pltpu.make_async_copy(xs_hbm.at[src, :], xbuf.at[s, pl.ds(h * half, half), :], xsem.at[s])
                for h, src in enumerate(halves(b))]

    def y_copies(b, s):
        return [pltpu.make_async_copy(ybuf.at[s, pl.ds(h * half, half), :], ys_hbm.at[dst, :], ysem.at[s])
                for h, dst in enumerate(halves(b))]

    def start(copies):
        for h, c in enumerate(copies):
            c.start(priority=h)

    def wait(copies):
        for c in copies:
            c.wait()

    def zero_fill(b):
        return pltpu.make_async_copy(zbuf, ys_hbm.at[_rows(b, rows), :], zsem.at[0])

    @pl.when(i == 0)
    def _():
        start(x_copies(0, 0))

        @pl.when(n_used > 1)
        def _():
            start(x_copies(1, 1))

        zbuf[...] = jnp.zeros_like(zbuf)

        def fill(b, carry):
            zero_fill(b).start()
            return carry
        lax.fori_loop(n_used, n_blocks, fill, 0)

    @pl.when(i + 2 < n_used)
    def _():
        start(x_copies(i + 2, (i + 2) % 3))

    @pl.when(i < n_used)
    def _():
        wait(x_copies(i, i % 3))

        @pl.when(i >= 2)
        def _():
            wait(y_copies(i - 2, i % 2))

        @pl.when(jnp.logical_or(i == 0, be_ref[i] != be_ref[jnp.maximum(i - 1, 0)]))
        def _():
            wgb[...] = wg_ref[...].astype(BF16)
            wub[...] = wu_ref[...].astype(BF16)
            wdb[...] = wd_ref[...].astype(BF16)

        x = pltpu.bitcast(_from_tiles(xbuf.at[i % 3], pairs), BF16)
        g = _dot(x, wgb[...])
        u = _dot(x, wub[...])
        y = _dot((g * _sigmoid(g) * u).astype(BF16), wdb[...])
        yp = pltpu.bitcast(y.astype(BF16), U32)
        yb = ybuf.at[i % 2]
        for s in range(TILE_ROWS):
            yb[pl.ds(s, pairs, stride=TILE_ROWS), :] = yp[:, s * LANES:(s + 1) * LANES]
        start(y_copies(i, i % 2))

        @pl.when(i == n_used - 1)
        def _():
            @pl.when(i >= 1)
            def _():
                wait(y_copies(i - 1, (i + 1) % 2))
            wait(y_copies(i, i % 2))

            def fill_wait(b, carry):
                zero_fill(0).wait()
                return carry
            lax.fori_loop(n_used, n_blocks, fill_wait, 0)


def _experts(block_e, n_used, x_sorted, wg, wu, wd, layer, n_blocks):
    wspec = lambda shape: pl.BlockSpec((None, None) + shape, lambda i, be, nu: (layer, be[i], 0, 0))
    rows = PK_BLK_ROWS
    grid_spec = pltpu.PrefetchScalarGridSpec(
        num_scalar_prefetch=2,
        grid=(n_blocks,),
        in_specs=[
            pl.BlockSpec(memory_space=pl.ANY),
            wspec((D_MODEL, EXPERT_DIM)), wspec((D_MODEL, EXPERT_DIM)), wspec((EXPERT_DIM, D_MODEL)),
        ],
        out_specs=pl.BlockSpec(memory_space=pl.ANY),
        scratch_shapes=[
            pltpu.VMEM((3, rows, LANES), U32),
            pltpu.VMEM((2, rows, LANES), U32),
            pltpu.VMEM((rows, LANES), U32),
            pltpu.VMEM((D_MODEL, EXPERT_DIM), BF16),
            pltpu.VMEM((D_MODEL, EXPERT_DIM), BF16),
            pltpu.VMEM((EXPERT_DIM, D_MODEL), BF16),
            pltpu.SemaphoreType.DMA((3,)),
            pltpu.SemaphoreType.DMA((2,)),
            pltpu.SemaphoreType.DMA((1,)),
        ],
    )
    return pl.pallas_call(
        functools.partial(_experts_kernel, n_blocks=n_blocks),
        grid_spec=grid_spec,
        out_shape=jax.ShapeDtypeStruct((n_blocks * rows, LANES), U32),
        compiler_params=pltpu.CompilerParams(dimension_semantics=("arbitrary",)),
        name="experts",
    )(block_e, n_used, x_sorted, wg, wu, wd)


def _combine_kernel(np_ref, lp_ref, ws_ref, cur_ref, nx1_ref, nx2_ref, y_hbm, h2b_ref, xn_ref,
                    mod_ref, g_ref, swg_ref, swu_ref, swd_ref, o_ref, stg, acc, sem, *, n_steps):
    i = pl.program_id(0)
    slot = i % 3
    d = D_MODEL

    def piece_copy(s, p, src):
        dst = pl.ds(pl.multiple_of(s * PK_STG_ROWS + p * PK_PIECE_ROWS, PK_PIECE_ROWS), PK_PIECE_ROWS)
        return pltpu.make_async_copy(y_hbm.at[_piece(src), :], stg.at[dst, :], sem.at[s])

    def fetcher(s, pd_ref):
        return lambda p, prio: piece_copy(s, p, pd_ref[0, p]).start(priority=prio)

    @pl.when(i == 0)
    def _():
        _start_pieces(0, np_ref[0], fetcher(0, cur_ref))
        _start_pieces(0, np_ref[1], fetcher(1, nx1_ref))

    _wait_pieces(np_ref[i], PK_PIECE_ROWS, lambda rows: pltpu.make_async_copy(
        y_hbm.at[pl.ds(0, rows), :], stg.at[pl.ds(0, rows), :], sem.at[slot]))

    def gather_token(t):
        a = None
        for k in range(TOP_K):
            j = k * ROW_BLK + t
            pair = stg[_tile_at(lp_ref[0, j]), :]
            w = jnp.full((TILE_ROWS, LANES), ws_ref[0, j], F32)
            v = jnp.where(w < 0.0, _unpack_hi(pair), _unpack_lo(pair)) * jnp.abs(w)
            a = v if a is None else a + v
        acc[_rows(t, TILE_ROWS), :] = a

    np_next = jnp.where(i + 2 < n_steps, np_ref[jnp.minimum(i + 2, n_steps - 1)], 0)
    _interleave(gather_token, np_next, fetcher((i + 2) % 3, nx2_ref))

    hb = h2b_ref[...]
    g = _dot(hb, swg_ref[...])
    u = _dot(hb, swu_ref[...])
    f = _dot((g * _sigmoid(g) * u).astype(BF16), swd_ref[...]) + _from_tiles(acc, ROW_BLK)
    o_ref[...] = xn_ref[...] + mod_ref[:, 5 * d:6 * d] * _rms(f, g_ref[...])


def _combine(npieces, pair_row, w_signed, piece_src, y_sorted, h2b_rows, xn_rows, mod3, g, swg, swu, swd,
             n_steps, per_batch):
    assert n_steps >= 3
    last = n_steps - 1
    full = lambda shape: pl.BlockSpec(shape, lambda n, *_: (0,) * len(shape))
    rows = lambda w_: pl.BlockSpec((ROW_BLK, w_), lambda n, *_: (n, 0))
    smem = lambda w_, imap: pl.BlockSpec((None, 1, w_), imap, memory_space=pltpu.SMEM)
    n_lat = SEQ // ROW_BLK
    grid_spec = pltpu.PrefetchScalarGridSpec(
        num_scalar_prefetch=1,
        grid=(n_steps,),
        in_specs=[
            smem(TOP_K * ROW_BLK, lambda n, *_: (n, 0, 0)),
            smem(TOP_K * ROW_BLK, lambda n, *_: (n, 0, 0)),
            smem(PIECES_MAX, lambda n, *_: (n, 0, 0)),
            smem(PIECES_MAX, lambda n, *_: (jnp.minimum(n + 1, last), 0, 0)),
            smem(PIECES_MAX, lambda n, *_: (jnp.minimum(n + 2, last), 0, 0)),
            pl.BlockSpec(memory_space=pl.ANY),
            rows(D_MODEL), rows(D_MODEL),
            pl.BlockSpec((None, 1, 6 * D_MODEL),
                         lambda n, *_: (_mod_row(n % per_batch, n // per_batch, n_lat), 0, 0)),
            full((1, D_MODEL)),
            full((D_MODEL, SHARED_DIM)), full((D_MODEL, SHARED_DIM)), full((SHARED_DIM, D_MODEL)),
        ],
        out_specs=rows(D_MODEL),
        scratch_shapes=[
            pltpu.VMEM((3 * PK_STG_ROWS, LANES), U32),
            pltpu.VMEM((ROW_BLK * TILE_ROWS, LANES), F32),
            pltpu.SemaphoreType.DMA((3,)),
        ],
    )
    return pl.pallas_call(
        functools.partial(_combine_kernel, n_steps=n_steps),
        grid_spec=grid_spec,
        out_shape=jax.ShapeDtypeStruct(xn_rows.shape, F32),
        compiler_params=pltpu.CompilerParams(dimension_semantics=("arbitrary",),
                                             vmem_limit_bytes=VMEM_LIMIT),
        name="combine",
    )(npieces, pair_row, w_signed, piece_src, piece_src, piece_src, y_sorted, h2b_rows, xn_rows, mod3, g,
      swg, swu, swd)


def _moe_plan(cnt_rows, n_blocks):
    i32 = jnp.int32
    per_blk = MOE_BLK // PIECE
    to_expert = lambda v: v.reshape(v.shape[:-1] + (8, 8)).swapaxes(-1, -2).reshape(v.shape)
    runs = (cnt_rows + PIECE - 1) // PIECE
    loc_end = jnp.cumsum(runs, axis=1)
    loc = loc_end - runs
    npieces = loc_end[:, -1].astype(i32)
    reg = to_expert(runs.sum(axis=0))
    padded = (reg + per_blk - 1) // per_blk * per_blk
    pad_end = jnp.cumsum(padded)
    pad_start = pad_end - padded
    off = to_expert(pad_start)[None, :] + jnp.cumsum(runs, axis=0) - runs
    p = jnp.arange(PIECES_MAX, dtype=i32)[None, :, None]
    mine = (loc[:, None, :] <= p) & (p < loc_end[:, None, :])
    piece_dst = jnp.sum(jnp.where(mine, (off - loc)[:, None, :], 0), axis=-1) + p[:, :, 0]
    tail_n = padded - reg
    t_end = jnp.cumsum(tail_n)
    t_beg = t_end - tail_n
    q = jnp.arange(TAIL_MAX, dtype=i32)[:, None]
    tmine = (t_beg[None, :] <= q) & (q < t_end[None, :])
    tail_dst = jnp.sum(jnp.where(tmine, (pad_start + reg - t_beg)[None, :], 0), axis=-1) + q[:, 0]
    first = jnp.arange(n_blocks, dtype=i32)[:, None] * per_blk
    block_e = jnp.minimum(jnp.sum((pad_end[None, :] <= first).astype(i32), axis=-1), N_EXPERTS - 1)
    n_used = (pad_end[-1] // per_blk).astype(i32).reshape(1)
    return (npieces, piece_dst.astype(i32).reshape(-1, 1, PIECES_MAX),
            tail_dst.astype(i32).reshape(1, 1, TAIL_MAX), t_end[-1].astype(i32).reshape(1),
            block_e.astype(i32), n_used)


def _axial_tables(dim):
    f32 = np.float32
    t = np.arange(SEQ)
    row = (t // GRID_W).astype(f32)
    col = (t % GRID_W).astype(f32)
    quarter = dim // 4
    freqs = (f32(ROPE_THETA) ** (-np.arange(quarter, dtype=f32) / f32(quarter))).astype(f32)
    ar, ac = row[:, None] * freqs, col[:, None] * freqs
    ang = np.concatenate([ar, ar, ac, ac], axis=-1)
    sign = np.where((np.arange(dim) // quarter) % 2 == 0, -1.0, 1.0).astype(f32)
    cos = np.concatenate([np.cos(ang), np.ones((CTX_LEN, dim), f32)], axis=0)
    sin = np.concatenate([np.sin(ang) * sign, np.zeros((CTX_LEN, dim), f32)], axis=0)
    return cos.astype(f32), sin.astype(f32)


def _rope_tables():
    f32 = np.float32
    cos_h, sin_h = _axial_tables(HEAD_DIM)
    cos_r, sin_r = _axial_tables(B_ROPE)
    cosh, sinh = np.tile(cos_h, (1, 4)), np.tile(sin_h, (1, 4))
    scale_b = f32((B_NOPE + B_ROPE) ** -0.5)
    ones, zeros = np.ones((TOK, 64), f32), np.zeros((TOK, 64), f32)
    pad1, pad0 = np.ones((TOK, 32), f32), np.zeros((TOK, 32), f32)
    cosb = np.tile(np.concatenate([ones, cos_r, pad1], axis=1) * scale_b, (1, 4))
    sinb = np.tile(np.concatenate([zeros, sin_r, pad0], axis=1) * scale_b, (1, 4))
    cosr = np.concatenate([cos_r, np.ones((TOK, 96), f32)], axis=1)
    sinr = np.concatenate([sin_r, np.zeros((TOK, 96), f32)], axis=1)
    return tuple(jnp.asarray(a, F32) for a in (cosh, sinh, cosb, sinb, cosr, sinr))


_GQA_ORDER = (0, 2, 1, 3)


def _take_heads(w, base, order, axis):
    return jnp.concatenate(
        [lax.slice_in_dim(w, base + h * HEAD_DIM, base + (h + 1) * HEAD_DIM, axis=axis) for h in order],
        axis=axis)


def _w_in_layout(w_in):
    zeros = lambda n: jnp.zeros((D_MODEL, n), w_in.dtype)
    a_q = _take_heads(w_in, 0, _GQA_ORDER, 1)
    a_kv = w_in[:, 256:512]
    b_q = jnp.concatenate(
        [jnp.concatenate([w_in[:, 512 + h * 96:512 + (h + 1) * 96], zeros(32)], axis=1)
         for h in range(B_HEADS)], axis=1)
    b_c = w_in[:, 896:1024]
    b_r = jnp.concatenate([w_in[:, 1024:1056], zeros(96)], axis=1)
    c_q = _take_heads(w_in, 1056, _GQA_ORDER, 1)
    rest = w_in[:, 1312:2336]
    return jnp.concatenate([a_q, a_kv, b_q, b_c, b_r, c_q, rest], axis=1).astype(BF16)


def _w_kv_layout(w_kv_up):
    zeros = jnp.zeros((B_KV_RANK, 64), w_kv_up.dtype)
    nope = [jnp.concatenate([w_kv_up[:, h * 128:h * 128 + 64], zeros], axis=1) for h in range(B_HEADS)]
    val = [w_kv_up[:, h * 128 + 64:(h + 1) * 128] for h in range(B_HEADS)]
    return jnp.concatenate(nope + val, axis=1).astype(BF16)


def _w_out_layout(w_out):
    return jnp.concatenate([_take_heads(w_out, 0, _GQA_ORDER, 0), w_out[256:512],
                            _take_heads(w_out, 512, _GQA_ORDER, 0), w_out[768:1024]], axis=0).astype(BF16)


def _router_rows(v):
    return v.reshape((8, 8) + v.shape[1:]).swapaxes(0, 1).reshape(v.shape)


def kernel(x, c, ctx, c_ctx, ada_w, ada_b, pre_mix_g, post_mix_g, pre_ffn_g, post_ffn_g, w_in, a_q_norm, a_k_norm, b_kv_norm, b_w_kv_up, c_sink, d_rpb, w_out, router_w, router_b, exp_w_gate, exp_w_up, exp_w_down, sh_w_gate, sh_w_up, sh_w_down):
    bsz = x.shape[0]
    depth = ada_w.shape[0]
    xall = jnp.concatenate([x, ctx], axis=1)

    cvec = jnp.zeros((16, D_MODEL), F32).at[:bsz].set(c).at[8].set(c_ctx)
    mod = _ada(cvec, ada_w, ada_b)
    tabs = _rope_tables()
    avg = jnp.asarray(np.kron(np.eye(4, dtype=np.float32), np.full((64, 64), 1.0 / 64, np.float32)), BF16)
    row2 = lambda v: v.reshape(1, -1).astype(F32)

    for l in range(depth):
        with_ctx = l < depth - 1
        mod3 = mod[l].reshape(16, 1, 6 * D_MODEL)

        proj = _inproj(
            xall, mod3, row2(pre_mix_g[l]), _w_in_layout(w_in[l]),
            row2(jnp.tile(a_q_norm[l], 4) * HEAD_DIM ** -0.5), row2(jnp.tile(a_k_norm[l], 2)),
            row2(b_kv_norm[l]), _w_kv_layout(b_w_kv_up[l]), avg, tabs)
        qa, ka, va, qb, kb, vb, qc, kc, vc, qd, kd, vd = proj

        nq = NQ + (CTX_LEN // QBLK if with_ctx else 0)
        full = lambda shape: pl.BlockSpec(shape, lambda b, i: (0,) * len(shape))
        mix_a = _attn_call(_attn_a_kernel, "attn_a", qa, ka, va, [], [], nq // 2, qblk=2 * QBLK)
        mix_b = _attn_call(_attn_b_kernel, "attn_b", qb, kb, vb, [], [], nq // 2, qblk=2 * QBLK)
        sink = jnp.zeros((8, LANES), F32).at[:4].set(
            jnp.broadcast_to(c_sink[l][np.array(_GQA_ORDER)][:, None], (4, LANES)))
        mix_c = _attn_call(_attn_c_kernel, "attn_c", qc, kc, vc, [sink], [full((8, LANES))], nq)
        bias = _na_bias(d_rpb[l])
        mix_d = _attn_call(
            _attn_d_kernel, "attn_d", qd, kd, vd, [bias],
            [pl.BlockSpec((None, D_HEADS, QBLK, NA_WIN), lambda b, i: (_na_pattern(i), 0, 0, 0))], nq)

        nb = (TOK if with_ctx else SEQ) // ROW_BLK
        rb = jnp.broadcast_to(_router_rows(router_b[l])[:, None], (N_EXPERTS, ROW_BLK)).astype(F32)
        xn, h2b, h2t, wt_t, ls_t, cnt_t = _outproj(
            (mix_a, mix_b, mix_c, mix_d), _w_out_layout(w_out[l]), xall, mod3,
            row2(post_mix_g[l]), row2(pre_ffn_g[l]),
            _router_rows(router_w[l].T).astype(BF16), rb, nb)

        t_used = nb * ROW_BLK
        n_tok = bsz * t_used
        n_steps = bsz * nb
        n_blocks = -(-(n_tok * TOP_K + n_steps * N_EXPERTS * (PIECE - 1) + N_EXPERTS * (MOE_BLK - 1))
                     // MOE_BLK)
        npieces, piece_dst, tail_dst, ntail, block_e, n_used = _moe_plan(
            cnt_t[..., 0].reshape(n_steps, N_EXPERTS), n_blocks)
        lslot = ls_t.reshape(n_steps, 1, TOP_K * ROW_BLK)
        wts = wt_t.reshape(n_steps, 1, TOP_K * ROW_BLK)
        stg_row = lslot * TILE_ROWS
        ring = (jnp.arange(n_steps, dtype=jnp.int32) % 3)[:, None, None] * PK_STG_ROWS
        pair_row = (lslot // 2) * TILE_ROWS + ring
        w_signed = jnp.where(lslot % 2 == 0, wts, -wts)

        x_sorted = _dispatch(npieces, n_used, ntail, stg_row, piece_dst, tail_dst,
                             h2t.reshape(n_tok * TILE_ROWS, LANES), n_steps, n_blocks)
        y_sorted = _experts(block_e, n_used, x_sorted, exp_w_gate, exp_w_up, exp_w_down, l, n_blocks)
        xall = _combine(
            npieces, pair_row, w_signed, piece_dst, y_sorted, h2b.reshape(n_tok, D_MODEL),
            xn.reshape(n_tok, D_MODEL), mod3, row2(post_ffn_g[l]),
            sh_w_gate[l].astype(BF16), sh_w_up[l].astype(BF16), sh_w_down[l].astype(BF16),
            n_steps, nb).reshape(bsz, t_used, D_MODEL)

    return xall[:, :SEQ]
```

```python
import functools

import numpy as np
import jax
import jax.numpy as jnp
from jax import lax
from jax.experimental import pallas as pl
from jax.experimental.pallas import tpu as pltpu

F32 = jnp.float32
BF16 = jnp.bfloat16
U32 = jnp.uint32

D_MODEL = 1024
SEQ = 2048
CTX_LEN = 256
TOK = SEQ + CTX_LEN
GRID_W = 64
HEAD_DIM = 64
ROPE_THETA = 10000.0
EPS = 1e-6
NEG = -1e30

A_HEADS, A_KV_HEADS = 4, 2
B_HEADS, B_NOPE, B_ROPE, B_V, B_KV_RANK = 4, 64, 32, 64, 128
C_HEADS, C_KV_HEADS, C_WINDOW = 4, 2, 128
D_HEADS, NA_ROWS, NA_COLS = 4, 8, 16

N_EXPERTS, TOP_K, N_GROUPS, TOPK_GROUPS = 64, 8, 8, 4
EXPERT_DIM, SHARED_DIM = 256, 256
ROUTED_SCALE = 2.5

LANES = 128
QBLK = 128
ROW_BLK = 256
IN_BLK = 768
MOE_BLK = 768
TILE_ROWS = D_MODEL // LANES
PIECE = 8
PIECE_ROWS = PIECE * TILE_ROWS
PIECES_MAX = 320
STG_ROWS = PIECES_MAX * PIECE_ROWS
PK_PIECE_ROWS = PIECE_ROWS // 2
PK_STG_ROWS = STG_ROWS // 2
PK_BLK_ROWS = MOE_BLK * TILE_ROWS // 2
TAIL_MAX = N_EXPERTS * MOE_BLK // PIECE
LOG_PACK_GROUP = 2
PACK_GROUP = 1 << LOG_PACK_GROUP
LOG_WAIT_GROUP = 4
WAIT_GROUP = 1 << LOG_WAIT_GROUP
LOG_CHUNKS = 2
N_CHUNKS = 1 << LOG_CHUNKS
VMEM_LIMIT = 56 * 1024 * 1024
NQ = SEQ // QBLK
NA_WIN = 10 * GRID_W

_AQ, _AK, _AV, _BQ, _BC, _BR = 0, 256, 384, 512, 1024, 1152
_CQ, _CK, _CV, _DQ, _DK, _DV = 1280, 1536, 1664, 1792, 2048, 2304
IN_W = 2560

_NT = (((1,), (1,)), ((), ()))


def _sigmoid(x):
    return 1.0 / (1.0 + jnp.exp(-x))


def _rms(x, g):
    return x * lax.rsqrt(jnp.mean(x * x, axis=-1, keepdims=True) + EPS) * g


def _dot(a, b):
    return jnp.dot(a, b, preferred_element_type=F32)


def _dot_nt(a, b):
    return lax.dot_general(a, b, _NT, preferred_element_type=F32)


def _ada_kernel(c_ref, w_ref, b_ref, o_ref):
    c = c_ref[...]
    a = (c * _sigmoid(c)).astype(BF16)
    o_ref[...] = _dot(a, w_ref[...].astype(BF16)) + b_ref[...]


def _ada(cvec, ada_w, ada_b):
    depth = ada_w.shape[0]
    n = ada_w.shape[2]
    tn = 1536
    return pl.pallas_call(
        _ada_kernel,
        grid=(depth, n // tn),
        in_specs=[
            pl.BlockSpec((16, D_MODEL), lambda l, j: (0, 0)),
            pl.BlockSpec((None, D_MODEL, tn), lambda l, j: (l, 0, j)),
            pl.BlockSpec((None, 1, tn), lambda l, j: (l, 0, j)),
        ],
        out_specs=pl.BlockSpec((None, 16, tn), lambda l, j: (l, 0, j)),
        out_shape=jax.ShapeDtypeStruct((depth, 16, n), F32),
        name="ada",
    )(cvec, ada_w, ada_b.reshape(depth, 1, n))


def _rope(x, cos, sin, chunk):
    outs = []
    for c in range(x.shape[1] // LANES):
        sl = slice(c * LANES, (c + 1) * LANES)
        xs = x[:, sl]
        lane = lax.broadcasted_iota(jnp.int32, xs.shape, 1)
        first = (lane & chunk) == 0
        rot = jnp.where(first, pltpu.roll(xs, LANES - chunk, 1), pltpu.roll(xs, chunk, 1))
        outs.append(xs * cos[:, sl] + rot * sin[:, sl])
    return outs[0] if len(outs) == 1 else jnp.concatenate(outs, axis=1)


def _head_norm(x, avg):
    sq = x * x
    hi = sq.astype(BF16)
    lo = (sq - hi.astype(F32)).astype(BF16)
    ms = _dot(hi, avg) + _dot(lo, avg)
    return x * lax.rsqrt(ms + EPS)


def _inproj_kernel(x_ref, mod_ref, modc_ref, g_ref, w_ref, qg_ref, kg_ref, kvg_ref, wkv_ref, avg_ref,
                   cosh_ref, sinh_ref, cosb_ref, sinb_ref, cosr_ref, sinr_ref,
                   qa_ref, ka_ref, va_ref, qb_ref, kb_ref, vb_ref,
                   qc_ref, kc_ref, vc_ref, qd_ref, kd_ref, vd_ref):
    d = D_MODEL
    n = x_ref.shape[0]
    is_ctx = pl.program_id(1) * n + lax.broadcasted_iota(jnp.int32, (n, 1), 0) >= SEQ
    scale = jnp.where(is_ctx, modc_ref[:, d:2 * d], mod_ref[:, d:2 * d])
    shift = jnp.where(is_ctx, modc_ref[:, 0:d], mod_ref[:, 0:d])
    h = _rms(x_ref[...], g_ref[...]) * (1.0 + scale) + shift
    p = _dot(h.astype(BF16), w_ref[...])
    cosh, sinh = cosh_ref[...], sinh_ref[...]
    avg = avg_ref[...]

    qa = _head_norm(p[:, _AQ:_AQ + 256], avg) * qg_ref[...]
    qa_ref[...] = _rope(qa, cosh, sinh, 16).astype(BF16)
    ka = _head_norm(p[:, _AK:_AK + 128], avg[0:128, 0:128]) * kg_ref[...]
    ka_ref[...] = _rope(ka, cosh[:, 0:128], sinh[:, 0:128], 16).astype(BF16)
    ones = jnp.ones((p.shape[0], LANES), BF16)
    va_ref[...] = jnp.concatenate([p[:, _AV:_AV + 128].astype(BF16), ones], axis=1)

    qb_ref[...] = _rope(p[:, _BQ:_BQ + 512], cosb_ref[...], sinb_ref[...], 8).astype(BF16)
    cn = _rms(p[:, _BC:_BC + 128], kvg_ref[...])
    kv = _dot(cn.astype(BF16), wkv_ref[...])
    kr = _rope(p[:, _BR:_BR + 128], cosr_ref[...], sinr_ref[...], 8)
    kr = pltpu.roll(kr, 64, 1)
    kb_ref[...] = jnp.concatenate(
        [kv[:, hh * 128:(hh + 1) * 128] + kr for hh in range(B_HEADS)], axis=1).astype(BF16)
    vb_ref[...] = jnp.concatenate(
        [kv[:, 512:640].astype(BF16), ones, kv[:, 640:768].astype(BF16), ones], axis=1)

    qc_ref[...] = _rope(p[:, _CQ:_CQ + 256] * 0.125, cosh, sinh, 16).astype(BF16)
    kc_ref[...] = _rope(p[:, _CK:_CK + 128], cosh[:, 0:128], sinh[:, 0:128], 16).astype(BF16)
    vc_ref[...] = jnp.concatenate([p[:, _CV:_CV + 128].astype(BF16), ones], axis=1)

    qd_ref[...] = (p[:, _DQ:_DQ + 256] * 0.125).astype(BF16)
    kd_ref[...] = p[:, _DK:_DK + 256].astype(BF16)
    vd_ref[...] = jnp.concatenate(
        [p[:, _DV:_DV + 128].astype(BF16), ones, p[:, _DV + 128:_DV + 256].astype(BF16), ones], axis=1)


def _mod_row(j, b, n_lat_blocks):
    return jnp.where(j < n_lat_blocks, b, 8)


def _inproj(xall, mod3, g, w, qg, kg, kvg, wkv, avg, tabs):
    bsz = xall.shape[0]
    nb = TOK // IN_BLK
    full = lambda shape: pl.BlockSpec(shape, lambda b, j: (0,) * len(shape))
    row = lambda w_: pl.BlockSpec((IN_BLK, w_), lambda b, j: (j, 0))
    out = lambda w_: pl.BlockSpec((None, IN_BLK, w_), lambda b, j: (b, j, 0))
    widths = (256, 128, 256, 512, 512, 512, 256, 128, 256, 256, 256, 512)
    return pl.pallas_call(
        _inproj_kernel,
        grid=(bsz, nb),
        in_specs=[
            pl.BlockSpec((None, IN_BLK, D_MODEL), lambda b, j: (b, j, 0)),
            pl.BlockSpec((None, 1, 6 * D_MODEL), lambda b, j: (b, 0, 0)),
            pl.BlockSpec((None, 1, 6 * D_MODEL), lambda b, j: (8, 0, 0)),
            full((1, D_MODEL)), full((D_MODEL, IN_W)),
            full((1, 256)), full((1, 128)), full((1, 128)), full((128, 768)), full((256, 256)),
            row(256), row(256), row(512), row(512), row(128), row(128),
        ],
        out_specs=[out(w_) for w_ in widths],
        out_shape=[jax.ShapeDtypeStruct((bsz, TOK, w_), BF16) for w_ in widths],
        compiler_params=pltpu.CompilerParams(vmem_limit_bytes=VMEM_LIMIT),
        name="inproj",
    )(xall, mod3, mod3, g, w, qg, kg, kvg, wkv, avg, *tabs)


def _softmax_pv(segs, sink=None):
    m = None
    for s, _ in segs:
        ms = jnp.max(s, axis=-1, keepdims=True)
        m = ms if m is None else jnp.maximum(m, ms)
    if sink is not None:
        m = jnp.maximum(m, sink)
    acc = None
    for s, v1 in segs:
        r = _dot(jnp.exp((s - m).astype(BF16)), v1)
        acc = r if acc is None else acc + r
    l = acc[:, LANES:LANES + 1]
    if sink is not None:
        l = l + jnp.exp(sink - m)
    return acc[:, 0:LANES] / l


def _lane_lo(dtype_shape):
    return lax.broadcasted_iota(jnp.int32, dtype_shape, 1) < HEAD_DIM


def _half(qt, hh):
    lo = _lane_lo(qt.shape)
    return jnp.where(lo if hh == 0 else jnp.logical_not(lo), qt, jnp.zeros_like(qt))


def _merge_halves(o0, o1):
    return jnp.where(_lane_lo(o0.shape), o0, o1)


def _stack_heads(q_ref, tiles):
    return jnp.concatenate(
        [_half(q_ref[:, t * LANES:(t + 1) * LANES], hh) for t in tiles for hh in range(2)], axis=0)


def _store_heads(o_ref, o, tiles):
    n = o_ref.shape[0]
    for ti, t in enumerate(tiles):
        o0, o1 = o[2 * ti * n:(2 * ti + 1) * n], o[(2 * ti + 1) * n:(2 * ti + 2) * n]
        o_ref[:, t * LANES:(t + 1) * LANES] = _merge_halves(o0, o1).astype(o_ref.dtype)


def _attn_a_kernel(q_ref, k_ref, v_ref, o_ref, *, n_lat):
    def run(k, v1):
        for t in range(2):
            o = [_softmax_pv([(_dot_nt(_half(q_ref[:, t * LANES:(t + 1) * LANES], hh), k), v1)])
                 for hh in range(2)]
            o_ref[:, t * LANES:(t + 1) * LANES] = _merge_halves(o[0], o[1]).astype(o_ref.dtype)

    @pl.when(pl.program_id(1) < n_lat)
    def _():
        run(k_ref[...], v_ref[...])

    @pl.when(pl.program_id(1) >= n_lat)
    def _():
        run(k_ref[SEQ:TOK, :], v_ref[SEQ:TOK, :])


def _attn_b_kernel(q_ref, k_ref, v_ref, o_ref, *, n_lat):
    def run(lo_, hi_):
        for t in range(2):
            v1 = v_ref[lo_:hi_, t * 256:(t + 1) * 256]
            o = []
            for hh in range(2):
                hd = 2 * t + hh
                q = q_ref[:, hd * LANES:(hd + 1) * LANES]
                k = k_ref[lo_:hi_, hd * LANES:(hd + 1) * LANES]
                o.append(_softmax_pv([(_dot_nt(q, k), v1)]))
            o_ref[:, t * LANES:(t + 1) * LANES] = _merge_halves(o[0], o[1]).astype(o_ref.dtype)

    @pl.when(pl.program_id(1) < n_lat)
    def _():
        run(0, TOK)

    @pl.when(pl.program_id(1) >= n_lat)
    def _():
        run(SEQ, TOK)


def _attn_c_kernel(q_ref, k_ref, v_ref, sink_ref, o_ref, *, n_lat):
    i = pl.program_id(1)
    qblk = o_ref.shape[0]
    band = 3 * qblk

    def run(latent):
        if latent:
            start = pl.multiple_of(jnp.clip((i - 1) * qblk, 0, SEQ - band), qblk)
            shape = (2 * qblk, band)
            qpos = i * qblk + (lax.broadcasted_iota(jnp.int32, shape, 0) & (qblk - 1))
            kpos = start + lax.broadcasted_iota(jnp.int32, shape, 1)
            valid = jnp.abs(qpos - kpos) <= C_WINDOW
        for t in range(2):
            q = _stack_heads(q_ref, (t,))
            segs = [(_dot_nt(q, k_ref[SEQ:TOK, :]), v_ref[SEQ:TOK, :])]
            if latent:
                s = jnp.where(valid, _dot_nt(q, k_ref[pl.ds(start, band), :]), NEG)
                segs.insert(0, (s, v_ref[pl.ds(start, band), :]))
            sink = jnp.concatenate(
                [jnp.broadcast_to(sink_ref[j:j + 1, 0:1], (qblk, 1)) for j in (2 * t, 2 * t + 1)], axis=0)
            _store_heads(o_ref, _softmax_pv(segs, sink=sink), (t,))

    @pl.when(i < n_lat)
    def _():
        run(True)

    @pl.when(i >= n_lat)
    def _():
        run(False)


def _attn_d_kernel(q_ref, k_ref, v_ref, bias_ref, o_ref, *, n_lat):
    i = pl.program_id(1)
    qblk = o_ref.shape[0]

    def run(latent):
        if latent:
            start = pl.multiple_of(jnp.clip((i - 2) * qblk, 0, SEQ - NA_WIN), qblk)
        for t in range(2):
            ksl, vsl = slice(t * LANES, (t + 1) * LANES), slice(t * 256, (t + 1) * 256)
            q = _stack_heads(q_ref, (t,))
            segs = [(_dot_nt(q, k_ref[SEQ:TOK, ksl]), v_ref[SEQ:TOK, vsl])]
            if latent:
                bias = bias_ref[2 * t:2 * t + 2].reshape(2 * qblk, NA_WIN)
                segs.insert(0, (_dot_nt(q, k_ref[pl.ds(start, NA_WIN), ksl]) + bias,
                                v_ref[pl.ds(start, NA_WIN), vsl]))
            _store_heads(o_ref, _softmax_pv(segs), (t,))

    @pl.when(i < n_lat)
    def _():
        run(True)

    @pl.when(i >= n_lat)
    def _():
        run(False)


def _attn_call(kernel, name, q, k, v, extra, extra_specs, n_blocks, qblk=QBLK):
    bsz = q.shape[0]
    return pl.pallas_call(
        functools.partial(kernel, n_lat=SEQ // qblk),
        grid=(bsz, n_blocks),
        in_specs=[
            pl.BlockSpec((None, qblk, q.shape[2]), lambda b, i: (b, i, 0)),
            pl.BlockSpec((None, TOK, k.shape[2]), lambda b, i: (b, 0, 0)),
            pl.BlockSpec((None, TOK, v.shape[2]), lambda b, i: (b, 0, 0)),
        ] + extra_specs,
        out_specs=pl.BlockSpec((None, qblk, 256), lambda b, i: (b, i, 0)),
        out_shape=jax.ShapeDtypeStruct((bsz, n_blocks * qblk, 256), BF16),
        name=name,
    )(q, k, v, *extra)


def _na_pattern(i):
    return jnp.where(i < 2, i, jnp.where(i < NQ - 2, 2, jnp.minimum(i, NQ - 1) - (NQ - 5)))


def _na_bias(rpb):
    rows = SEQ // GRID_W
    n_dr, n_dc = 2 * NA_ROWS - 1, 2 * NA_COLS - 1
    qc = np.arange(GRID_W)
    dc = np.clip(qc[None, :] - qc[:, None] + NA_COLS - 1, 0, n_dc - 1)
    sel_c = jnp.asarray(np.eye(n_dc, dtype=np.float32)[dc])
    cs = np.clip(qc - NA_COLS // 2, 0, GRID_W - NA_COLS)[:, None]
    col_ok = (qc[None, :] >= cs) & (qc[None, :] < cs + NA_COLS)
    sel_r, row_ok = [], []
    for i in (0, 1, 2, NQ - 2, NQ - 1):
        start_row = min(max(2 * i - 4, 0), rows - 10)
        qr = 2 * i + np.arange(2)
        kr = start_row + np.arange(10)
        rs = np.clip(qr - NA_ROWS // 2, 0, rows - NA_ROWS)[:, None]
        row_ok.append((kr[None] >= rs) & (kr[None] < rs + NA_ROWS))
        dr = np.clip(kr[None] - qr[:, None] + NA_ROWS - 1, 0, n_dr - 1)
        sel_r.append(np.eye(n_dr, dtype=np.float32)[dr])
    b = jnp.einsum("hrc,pabr,qkc->phaqbk", rpb.astype(F32), jnp.asarray(np.stack(sel_r)), sel_c,
                   precision=lax.Precision.HIGHEST)
    valid = np.stack(row_ok)[:, None, :, None, :, None] & col_ok[None, None, None, :, None, :]
    return jnp.where(valid, b, NEG).reshape(5, D_HEADS, QBLK, NA_WIN)


def _outproj_kernel(ma_ref, mb_ref, mc_ref, md_ref, w_ref, x_ref, mod_ref, g1_ref, g2_ref,
                    wr_ref, rb_ref, xn_ref, h2b_ref, h2t_ref, wt_ref, ls_ref, cnt_ref):
    d = D_MODEL
    mix = jnp.concatenate([ma_ref[...], mb_ref[...], mc_ref[...], md_ref[...]], axis=1)
    y = _rms(_dot(mix, w_ref[...]), g1_ref[...])
    xn = x_ref[...] + mod_ref[:, 2 * d:3 * d] * y
    xn_ref[...] = xn
    h2 = _rms(xn, g2_ref[...]) * (1.0 + mod_ref[:, 4 * d:5 * d]) + mod_ref[:, 3 * d:4 * d]
    n = h2.shape[0]
    h2b_ref[...] = h2.astype(BF16)
    for s in range(TILE_ROWS):
        h2t_ref[pl.ds(s, n, stride=TILE_ROWS), :] = h2[:, s * LANES:(s + 1) * LANES]

    s = _sigmoid(_dot_nt(wr_ref[...], h2.astype(BF16)))
    sb = s + rb_ref[...]
    per = N_EXPERTS // N_GROUPS
    slab = [sb[j * 8:(j + 1) * 8, :] for j in range(per)]
    sraw = [s[j * 8:(j + 1) * 8, :] for j in range(per)]
    giota = lax.broadcasted_iota(jnp.int32, (N_GROUPS, n), 0)
    eid = [giota * per + j for j in range(per)]
    ninf = -jnp.inf

    m1 = functools.reduce(jnp.maximum, slab)
    j1 = functools.reduce(jnp.minimum, [jnp.where(slab[j] == m1, j, per) for j in range(per)])
    m2 = functools.reduce(jnp.maximum, [jnp.where(j1 == j, ninf, slab[j]) for j in range(per)])
    gs = m1 + m2

    gsel = jnp.zeros((N_GROUPS, n), jnp.bool_)
    for _ in range(TOPK_GROUPS):
        m = jnp.max(gs, axis=0, keepdims=True)
        gi = jnp.min(jnp.where(gs == m, giota, N_GROUPS), axis=0, keepdims=True)
        hit = giota == gi
        gsel = jnp.logical_or(gsel, hit)
        gs = jnp.where(hit, ninf, gs)
    slab = [jnp.where(gsel, sj, ninf) for sj in slab]

    def pick(hits, vals):
        return jnp.sum(functools.reduce(jnp.add, [jnp.where(hits[j], vals[j], 0.0) for j in range(per)]),
                       axis=0, keepdims=True)

    kiota = lax.broadcasted_iota(jnp.int32, (TOP_K, n), 0)
    wts = jnp.zeros((TOP_K, n), F32)
    chosen = []
    for k in range(TOP_K):
        m = jnp.max(functools.reduce(jnp.maximum, slab), axis=0, keepdims=True)
        cand = functools.reduce(jnp.minimum,
                                [jnp.where(slab[j] == m, eid[j], N_EXPERTS) for j in range(per)])
        ei = jnp.min(cand, axis=0, keepdims=True)
        hits = [eid[j] == ei for j in range(per)]
        slab = [jnp.where(hits[j], ninf, slab[j]) for j in range(per)]
        wts = jnp.where(kiota == k, pick(hits, sraw), wts)
        chosen.append(ei)
    wt_ref[...] = wts / jnp.sum(wts, axis=0, keepdims=True) * ROUTED_SCALE

    hits_k = [[eid[j] == chosen[k] for j in range(per)] for k in range(TOP_K)]
    sel = jnp.concatenate(
        [functools.reduce(jnp.add, [jnp.where(hits_k[k][j], 1.0, 0.0) for k in range(TOP_K)])
         for j in range(per)], axis=0)
    tri = jnp.where(lax.broadcasted_iota(jnp.int32, (n, n), 0) < lax.broadcasted_iota(jnp.int32, (n, n), 1),
                    1.0, 0.0).astype(BF16)
    before = _dot(sel.astype(BF16), tri)
    cnt = jnp.sum(sel, axis=1, keepdims=True)
    run = jnp.floor((cnt + (PIECE - 1.0)) * (1.0 / PIECE)) * PIECE
    low = jnp.where(lax.broadcasted_iota(jnp.int32, (N_EXPERTS, N_EXPERTS), 1)
                    < lax.broadcasted_iota(jnp.int32, (N_EXPERTS, N_EXPERTS), 0), 1.0, 0.0).astype(BF16)
    start = _dot(low, jnp.broadcast_to(run, (N_EXPERTS, LANES)).astype(BF16))[:, 0:1]
    base = before + start
    base = [base[j * 8:(j + 1) * 8, :] for j in range(per)]
    lsl = jnp.zeros((TOP_K, n), F32)
    for k in range(TOP_K):
        lsl = jnp.where(kiota == k, pick(hits_k[k], base), lsl)
    ls_ref[...] = lsl.astype(jnp.int32)
    cnt_ref[...] = jnp.broadcast_to(cnt, (N_EXPERTS, LANES)).astype(jnp.int32)


def _outproj(mixes, w, xall, mod3, g1, g2, wr, rb, nb):
    bsz = xall.shape[0]
    n_lat = SEQ // ROW_BLK
    full = lambda shape: pl.BlockSpec(shape, lambda b, j: (0,) * len(shape))
    rows = lambda w_: pl.BlockSpec((None, ROW_BLK, w_), lambda b, j: (b, j, 0))
    route = pl.BlockSpec((None, None, TOP_K, ROW_BLK), lambda b, j: (b, j, 0, 0))
    return pl.pallas_call(
        _outproj_kernel,
        grid=(bsz, nb),
        in_specs=[rows(256)] * 4 + [
            full((D_MODEL, D_MODEL)), rows(D_MODEL),
            pl.BlockSpec((None, 1, 6 * D_MODEL), lambda b, j: (_mod_row(j, b, n_lat), 0, 0)),
            full((1, D_MODEL)), full((1, D_MODEL)),
            full((N_EXPERTS, D_MODEL)), full((N_EXPERTS, ROW_BLK)),
        ],
        out_specs=[
            rows(D_MODEL), rows(D_MODEL),
            pl.BlockSpec((None, ROW_BLK * TILE_ROWS, LANES), lambda b, j: (b, j, 0)),
            route, route,
            pl.BlockSpec((None, None, N_EXPERTS, LANES), lambda b, j: (b, j, 0, 0)),
        ],
        out_shape=[
            jax.ShapeDtypeStruct((bsz, nb * ROW_BLK, D_MODEL), F32),
            jax.ShapeDtypeStruct((bsz, nb * ROW_BLK, D_MODEL), BF16),
            jax.ShapeDtypeStruct((bsz, nb * ROW_BLK * TILE_ROWS, LANES), F32),
            jax.ShapeDtypeStruct((bsz, nb, TOP_K, ROW_BLK), F32),
            jax.ShapeDtypeStruct((bsz, nb, TOP_K, ROW_BLK), jnp.int32),
            jax.ShapeDtypeStruct((bsz, nb, N_EXPERTS, LANES), jnp.int32),
        ],
        name="outproj",
    )(*mixes, w, xall, mod3, g1, g2, wr, rb)


def _rows(i, n):
    return pl.ds(pl.multiple_of(i * n, n), n)


def _tile_at(row):
    return pl.ds(pl.multiple_of(row, TILE_ROWS), TILE_ROWS)


def _piece(p):
    return _rows(p, PK_PIECE_ROWS)


def _pack_pairs(lo, hi):
    return (lax.shift_right_logical(pltpu.bitcast(lo, U32), jnp.uint32(16))
            | (pltpu.bitcast(hi, U32) & jnp.uint32(0xFFFF0000)))


def _unpack_lo(u):
    return pltpu.bitcast(lax.shift_left(u, jnp.uint32(16)), F32)


def _unpack_hi(u):
    return pltpu.bitcast(u & jnp.uint32(0xFFFF0000), F32)


def _start_pieces(lo, hi, start):
    n = hi - lo

    def group(q, carry):
        for r in range(8):
            start(lo + 8 * q + r, r % 2)
        return carry
    lax.fori_loop(0, lax.shift_right_logical(n, 3), group, 0)

    def rest(r, carry):
        start(hi - 1 - r, 0)
        return carry
    lax.fori_loop(0, n & 7, rest, 0)


def _wait_pieces(n, piece_rows, copy_of_rows):
    def group(p, carry):
        copy_of_rows(WAIT_GROUP * piece_rows).wait()
        return carry
    lax.fori_loop(0, lax.shift_right_logical(n, LOG_WAIT_GROUP), group, 0)

    def single(p, carry):
        copy_of_rows(piece_rows).wait()
        return carry
    lax.fori_loop(0, n & (WAIT_GROUP - 1), single, 0)


def _interleave(body, n_pieces, start):
    per = ROW_BLK // N_CHUNKS

    def chunk(c, carry):
        for tt in range(per):
            body(c * per + tt)
        _start_pieces(lax.shift_right_logical(c * n_pieces, LOG_CHUNKS),
                      lax.shift_right_logical((c + 1) * n_pieces, LOG_CHUNKS), start)
        return carry
    lax.fori_loop(0, N_CHUNKS, chunk, 0)


def _dispatch_kernel(np_ref, nu_ref, nt_ref, ls_ref, pd_ref, pdp_ref, td_ref, h_ref, xs_hbm,
                     stg, pk, zblk, sem, zsem, *, n_steps, n_blocks):
    i = pl.program_id(0)
    slot = i % 3
    prev = (i + 2) % 3

    def piece_copy(s, p, d):
        return pltpu.make_async_copy(pk.at[s, _piece(p), :], xs_hbm.at[_piece(d), :], sem.at[s])

    def wait_pieces(s, n):
        _wait_pieces(n, PK_PIECE_ROWS, lambda rows: pltpu.make_async_copy(
            pk.at[s, pl.ds(0, rows), :], xs_hbm.at[pl.ds(0, rows), :], sem.at[s]))

    def zero_piece_copy(d):
        return pltpu.make_async_copy(zblk.at[pl.ds(0, PK_PIECE_ROWS), :], xs_hbm.at[_piece(d), :], zsem.at[0])

    def zero_block_copy(b):
        return pltpu.make_async_copy(zblk, xs_hbm.at[_rows(b, PK_BLK_ROWS), :], zsem.at[1])

    @pl.when(i == 0)
    def _():
        zblk[...] = jnp.zeros_like(zblk)

        def zero(c, carry):
            stg[_rows(c, 2048), :] = jnp.zeros((2048, LANES), F32)
            return carry
        lax.fori_loop(0, STG_ROWS // 2048, zero, 0)

        def blk(b, carry):
            zero_block_copy(b).start()
            return carry
        lax.fori_loop(nu_ref[0], n_blocks, blk, 0)

    @pl.when(i >= 3)
    def _():
        wait_pieces(slot, np_ref[jnp.maximum(i - 3, 0)])

    def scatter_token(t):
        tile = h_ref[_rows(t, TILE_ROWS), :].astype(BF16).astype(F32)
        for k in range(TOP_K):
            stg[_tile_at(ls_ref[0, k * ROW_BLK + t]), :] = tile

    np_prev = jnp.where(i >= 1, np_ref[jnp.maximum(i - 1, 0)], 0)
    _interleave(scatter_token, np_prev,
                lambda p, prio: piece_copy(prev, p, pdp_ref[0, p]).start(priority=prio))

    def pack_group(q, carry):
        v = stg[_rows(q, PACK_GROUP * PIECE_ROWS), :].reshape(PACK_GROUP * PIECE // 2, 2 * TILE_ROWS, LANES)
        packed = _pack_pairs(v[:, 0:TILE_ROWS, :], v[:, TILE_ROWS:2 * TILE_ROWS, :])
        pk[slot, _rows(q, PACK_GROUP * PK_PIECE_ROWS), :] = packed.reshape(PACK_GROUP * PK_PIECE_ROWS, LANES)
        return carry
    lax.fori_loop(0, lax.shift_right_logical(np_ref[i] + (PACK_GROUP - 1), LOG_PACK_GROUP), pack_group, 0)

    @pl.when(i == n_steps - 1)
    def _():
        _start_pieces(0, np_ref[i], lambda p, prio: piece_copy(slot, p, pd_ref[0, p]).start(priority=prio))

        def tail(p, carry):
            zero_piece_copy(td_ref[0, p]).start()
            return carry
        lax.fori_loop(0, nt_ref[0], tail, 0)

        wait_pieces((i + 1) % 3, np_ref[jnp.maximum(i - 2, 0)])
        wait_pieces(prev, np_ref[jnp.maximum(i - 1, 0)])
        wait_pieces(slot, np_ref[i])

        _wait_pieces(nt_ref[0], PK_PIECE_ROWS, lambda rows: pltpu.make_async_copy(
            zblk.at[pl.ds(0, rows), :], xs_hbm.at[pl.ds(0, rows), :], zsem.at[0]))

        def blk_wait(b, carry):
            zero_block_copy(0).wait()
            return carry
        lax.fori_loop(nu_ref[0], n_blocks, blk_wait, 0)


def _dispatch(npieces, n_used, ntail, lslot, piece_dst, tail_dst, h2t, n_steps, n_blocks):
    assert n_steps >= 3
    smem = lambda n, imap: pl.BlockSpec((None, 1, n), imap, memory_space=pltpu.SMEM)
    grid_spec = pltpu.PrefetchScalarGridSpec(
        num_scalar_prefetch=3,
        grid=(n_steps,),
        in_specs=[
            smem(TOP_K * ROW_BLK, lambda i, *_: (i, 0, 0)),
            smem(PIECES_MAX, lambda i, *_: (i, 0, 0)),
            smem(PIECES_MAX, lambda i, *_: (jnp.maximum(i - 1, 0), 0, 0)),
            smem(TAIL_MAX, lambda i, *_: (0, 0, 0)),
            pl.BlockSpec((ROW_BLK * TILE_ROWS, LANES), lambda i, *_: (i, 0)),
        ],
        out_specs=pl.BlockSpec(memory_space=pl.ANY),
        scratch_shapes=[
            pltpu.VMEM((STG_ROWS, LANES), F32),
            pltpu.VMEM((3, PK_STG_ROWS, LANES), U32),
            pltpu.VMEM((PK_BLK_ROWS, LANES), U32),
            pltpu.SemaphoreType.DMA((3,)),
            pltpu.SemaphoreType.DMA((2,)),
        ],
    )
    return pl.pallas_call(
        functools.partial(_dispatch_kernel, n_steps=n_steps, n_blocks=n_blocks),
        grid_spec=grid_spec,
        out_shape=jax.ShapeDtypeStruct((n_blocks * PK_BLK_ROWS, LANES), U32),
        compiler_params=pltpu.CompilerParams(dimension_semantics=("arbitrary",),
                                             vmem_limit_bytes=VMEM_LIMIT),
        name="dispatch",
    )(npieces, n_used, ntail, lslot, piece_dst, piece_dst, tail_dst, h2t)


def _from_tiles(ref, n):
    return jnp.concatenate([ref[pl.ds(s, n, stride=TILE_ROWS), :] for s in range(TILE_ROWS)], axis=1)


def _experts_kernel(be_ref, nu_ref, xs_hbm, wg_ref, wu_ref, wd_ref, ys_hbm,
                    xbuf, ybuf, zbuf, wgb, wub, wdb, xsem, ysem, zsem, *, n_blocks):
    i = pl.program_id(0)
    n_used = nu_ref[0]
    rows = PK_BLK_ROWS
    half = rows // 2
    pairs = MOE_BLK // 2

    def halves(b):
        return [pl.ds(pl.multiple_of(b * rows + h * half, half), half) for h in range(2)]

    def x_copies(b, s):
        return [pltpu.make_async_copy(xs_hbm.at[src, :], xbuf.at[s, pl.ds(h * half, half), :], xsem.at[s])
                for h, src in enumerate(halves(b))]

    def y_copies(b, s):
        return [pltpu.make_async_copy(ybuf.at[s, pl.ds(h * half, half), :], ys_hbm.at[dst, :], ysem.at[s])
                for h, dst in enumerate(halves(b))]

    def start(copies):
        for h, c in enumerate(copies):
            c.start(priority=h)

    def wait(copies):
        for c in copies:
            c.wait()

    def zero_fill(b):
        return pltpu.make_async_copy(zbuf, ys_hbm.at[_rows(b, rows), :], zsem.at[0])

    @pl.when(i == 0)
    def _():
        start(x_copies(0, 0))

        @pl.when(n_used > 1)
        def _():
            start(x_copies(1, 1))

        zbuf[...] = jnp.zeros_like(zbuf)

        def fill(b, carry):
            zero_fill(b).start()
            return carry
        lax.fori_loop(n_used, n_blocks, fill, 0)

    @pl.when(i + 2 < n_used)
    def _():
        start(x_copies(i + 2, (i + 2) % 3))

    @pl.when(i < n_used)
    def _():
        wait(x_copies(i, i % 3))

        @pl.when(i >= 2)
        def _():
            wait(y_copies(i - 2, i % 2))

        @pl.when(jnp.logical_or(i == 0, be_ref[i] != be_ref[jnp.maximum(i - 1, 0)]))
        def _():
            wgb[...] = wg_ref[...].astype(BF16)
            wub[...] = wu_ref[...].astype(BF16)
            wdb[...] = wd_ref[...].astype(BF16)

        x = pltpu.bitcast(_from_tiles(xbuf.at[i % 3], pairs), BF16)
        g = _dot(x, wgb[...])
        u = _dot(x, wub[...])
        y = _dot((g * _sigmoid(g) * u).astype(BF16), wdb[...])
        yp = pltpu.bitcast(y.astype(BF16), U32)
        yb = ybuf.at[i % 2]
        for s in range(TILE_ROWS):
            yb[pl.ds(s, pairs, stride=TILE_ROWS), :] = yp[:, s * LANES:(s + 1) * LANES]
        start(y_copies(i, i % 2))

        @pl.when(i == n_used - 1)
        def _():
            @pl.when(i >= 1)
            def _():
                wait(y_copies(i - 1, (i + 1) % 2))
            wait(y_copies(i, i % 2))

            def fill_wait(b, carry):
                zero_fill(0).wait()
                return carry
            lax.fori_loop(n_used, n_blocks, fill_wait, 0)


def _experts(block_e, n_used, x_sorted, wg, wu, wd, layer, n_blocks):
    wspec = lambda shape: pl.BlockSpec((None, None) + shape, lambda i, be, nu: (layer, be[i], 0, 0))
    rows = PK_BLK_ROWS
    grid_spec = pltpu.PrefetchScalarGridSpec(
        num_scalar_prefetch=2,
        grid=(n_blocks,),
        in_specs=[
            pl.BlockSpec(memory_space=pl.ANY),
            wspec((D_MODEL, EXPERT_DIM)), wspec((D_MODEL, EXPERT_DIM)), wspec((EXPERT_DIM, D_MODEL)),
        ],
        out_specs=pl.BlockSpec(memory_space=pl.ANY),
        scratch_shapes=[
            pltpu.VMEM((3, rows, LANES), U32),
            pltpu.VMEM((2, rows, LANES), U32),
            pltpu.VMEM((rows, LANES), U32),
            pltpu.VMEM((D_MODEL, EXPERT_DIM), BF16),
            pltpu.VMEM((D_MODEL, EXPERT_DIM), BF16),
            pltpu.VMEM((EXPERT_DIM, D_MODEL), BF16),
            pltpu.SemaphoreType.DMA((3,)),
            pltpu.SemaphoreType.DMA((2,)),
            pltpu.SemaphoreType.DMA((1,)),
        ],
    )
    return pl.pallas_call(
        functools.partial(_experts_kernel, n_blocks=n_blocks),
        grid_spec=grid_spec,
        out_shape=jax.ShapeDtypeStruct((n_blocks * rows, LANES), U32),
        compiler_params=pltpu.CompilerParams(dimension_semantics=("arbitrary",)),
        name="experts",
    )(block_e, n_used, x_sorted, wg, wu, wd)


def _combine_kernel(np_ref, lp_ref, ws_ref, cur_ref, nx1_ref, nx2_ref, y_hbm, h2b_ref, xn_ref,
                    mod_ref, g_ref, swg_ref, swu_ref, swd_ref, o_ref, stg, acc, sem, *, n_steps):
    i = pl.program_id(0)
    slot = i % 3
    d = D_MODEL

    def piece_copy(s, p, src):
        dst = pl.ds(pl.multiple_of(s * PK_STG_ROWS + p * PK_PIECE_ROWS, PK_PIECE_ROWS), PK_PIECE_ROWS)
        return pltpu.make_async_copy(y_hbm.at[_piece(src), :], stg.at[dst, :], sem.at[s])

    def fetcher(s, pd_ref):
        return lambda p, prio: piece_copy(s, p, pd_ref[0, p]).start(priority=prio)

    @pl.when(i == 0)
    def _():
        _start_pieces(0, np_ref[0], fetcher(0, cur_ref))
        _start_pieces(0, np_ref[1], fetcher(1, nx1_ref))

    _wait_pieces(np_ref[i], PK_PIECE_ROWS, lambda rows: pltpu.make_async_copy(
        y_hbm.at[pl.ds(0, rows), :], stg.at[pl.ds(0, rows), :], sem.at[slot]))

    def gather_token(t):
        a = None
        for k in range(TOP_K):
            j = k * ROW_BLK + t
            pair = stg[_tile_at(lp_ref[0, j]), :]
            w = jnp.full((TILE_ROWS, LANES), ws_ref[0, j], F32)
            v = jnp.where(w < 0.0, _unpack_hi(pair), _unpack_lo(pair)) * jnp.abs(w)
            a = v if a is None else a + v
        acc[_rows(t, TILE_ROWS), :] = a

    np_next = jnp.where(i + 2 < n_steps, np_ref[jnp.minimum(i + 2, n_steps - 1)], 0)
    _interleave(gather_token, np_next, fetcher((i + 2) % 3, nx2_ref))

    hb = h2b_ref[...]
    g = _dot(hb, swg_ref[...])
    u = _dot(hb, swu_ref[...])
    f = _dot((g * _sigmoid(g) * u).astype(BF16), swd_ref[...]) + _from_tiles(acc, ROW_BLK)
    o_ref[...] = xn_ref[...] + mod_ref[:, 5 * d:6 * d] * _rms(f, g_ref[...])


def _combine(npieces, pair_row, w_signed, piece_src, y_sorted, h2b_rows, xn_rows, mod3, g, swg, swu, swd,
             n_steps, per_batch):
    assert n_steps >= 3
    last = n_steps - 1
    full = lambda shape: pl.BlockSpec(shape, lambda n, *_: (0,) * len(shape))
    rows = lambda w_: pl.BlockSpec((ROW_BLK, w_), lambda n, *_: (n, 0))
    smem = lambda w_, imap: pl.BlockSpec((None, 1, w_), imap, memory_space=pltpu.SMEM)
    n_lat = SEQ // ROW_BLK
    grid_spec = pltpu.PrefetchScalarGridSpec(
        num_scalar_prefetch=1,
        grid=(n_steps,),
        in_specs=[
            smem(TOP_K * ROW_BLK, lambda n, *_: (n, 0, 0)),
            smem(TOP_K * ROW_BLK, lambda n, *_: (n, 0, 0)),
            smem(PIECES_MAX, lambda n, *_: (n, 0, 0)),
            smem(PIECES_MAX, lambda n, *_: (jnp.minimum(n + 1, last), 0, 0)),
            smem(PIECES_MAX, lambda n, *_: (jnp.minimum(n + 2, last), 0, 0)),
            pl.BlockSpec(memory_space=pl.ANY),
            rows(D_MODEL), rows(D_MODEL),
            pl.BlockSpec((None, 1, 6 * D_MODEL),
                         lambda n, *_: (_mod_row(n % per_batch, n // per_batch, n_lat), 0, 0)),
            full((1, D_MODEL)),
            full((D_MODEL, SHARED_DIM)), full((D_MODEL, SHARED_DIM)), full((SHARED_DIM, D_MODEL)),
        ],
        out_specs=rows(D_MODEL),
        scratch_shapes=[
            pltpu.VMEM((3 * PK_STG_ROWS, LANES), U32),
            pltpu.VMEM((ROW_BLK * TILE_ROWS, LANES), F32),
            pltpu.SemaphoreType.DMA((3,)),
        ],
    )
    return pl.pallas_call(
        functools.partial(_combine_kernel, n_steps=n_steps),
        grid_spec=grid_spec,
        out_shape=jax.ShapeDtypeStruct(xn_rows.shape, F32),
        compiler_params=pltpu.CompilerParams(dimension_semantics=("arbitrary",),
                                             vmem_limit_bytes=VMEM_LIMIT),
        name="combine",
    )(npieces, pair_row, w_signed, piece_src, piece_src, piece_src, y_sorted, h2b_rows, xn_rows, mod3, g,
      swg, swu, swd)


def _moe_plan(cnt_rows, n_blocks):
    i32 = jnp.int32
    per_blk = MOE_BLK // PIECE
    to_expert = lambda v: v.reshape(v.shape[:-1] + (8, 8)).swapaxes(-1, -2).reshape(v.shape)
    runs = (cnt_rows + PIECE - 1) // PIECE
    loc_end = jnp.cumsum(runs, axis=1)
    loc = loc_end - runs
    npieces = loc_end[:, -1].astype(i32)
    reg = to_expert(runs.sum(axis=0))
    padded = (reg + per_blk - 1) // per_blk * per_blk
    pad_end = jnp.cumsum(padded)
    pad_start = pad_end - padded
    off = to_expert(pad_start)[None, :] + jnp.cumsum(runs, axis=0) - runs
    p = jnp.arange(PIECES_MAX, dtype=i32)[None, :, None]
    mine = (loc[:, None, :] <= p) & (p < loc_end[:, None, :])
    piece_dst = jnp.sum(jnp.where(mine, (off - loc)[:, None, :], 0), axis=-1) + p[:, :, 0]
    tail_n = padded - reg
    t_end = jnp.cumsum(tail_n)
    t_beg = t_end - tail_n
    q = jnp.arange(TAIL_MAX, dtype=i32)[:, None]
    tmine = (t_beg[None, :] <= q) & (q < t_end[None, :])
    tail_dst = jnp.sum(jnp.where(tmine, (pad_start + reg - t_beg)[None, :], 0), axis=-1) + q[:, 0]
    first = jnp.arange(n_blocks, dtype=i32)[:, None] * per_blk
    block_e = jnp.minimum(jnp.sum((pad_end[None, :] <= first).astype(i32), axis=-1), N_EXPERTS - 1)
    n_used = (pad_end[-1] // per_blk).astype(i32).reshape(1)
    return (npieces, piece_dst.astype(i32).reshape(-1, 1, PIECES_MAX),
            tail_dst.astype(i32).reshape(1, 1, TAIL_MAX), t_end[-1].astype(i32).reshape(1),
            block_e.astype(i32), n_used)


def _axial_tables(dim):
    f32 = np.float32
    t = np.arange(SEQ)
    row = (t // GRID_W).astype(f32)
    col = (t % GRID_W).astype(f32)
    quarter = dim // 4
    freqs = (f32(ROPE_THETA) ** (-np.arange(quarter, dtype=f32) / f32(quarter))).astype(f32)
    ar, ac = row[:, None] * freqs, col[:, None] * freqs
    ang = np.concatenate([ar, ar, ac, ac], axis=-1)
    sign = np.where((np.arange(dim) // quarter) % 2 == 0, -1.0, 1.0).astype(f32)
    cos = np.concatenate([np.cos(ang), np.ones((CTX_LEN, dim), f32)], axis=0)
    sin = np.concatenate([np.sin(ang) * sign, np.zeros((CTX_LEN, dim), f32)], axis=0)
    return cos.astype(f32), sin.astype(f32)


def _rope_tables():
    f32 = np.float32
    cos_h, sin_h = _axial_tables(HEAD_DIM)
    cos_r, sin_r = _axial_tables(B_ROPE)
    cosh, sinh = np.tile(cos_h, (1, 4)), np.tile(sin_h, (1, 4))
    scale_b = f32((B_NOPE + B_ROPE) ** -0.5)
    ones, zeros = np.ones((TOK, 64), f32), np.zeros((TOK, 64), f32)
    pad1, pad0 = np.ones((TOK, 32), f32), np.zeros((TOK, 32), f32)
    cosb = np.tile(np.concatenate([ones, cos_r, pad1], axis=1) * scale_b, (1, 4))
    sinb = np.tile(np.concatenate([zeros, sin_r, pad0], axis=1) * scale_b, (1, 4))
    cosr = np.concatenate([cos_r, np.ones((TOK, 96), f32)], axis=1)
    sinr = np.concatenate([sin_r, np.zeros((TOK, 96), f32)], axis=1)
    return tuple(jnp.asarray(a, F32) for a in (cosh, sinh, cosb, sinb, cosr, sinr))


_GQA_ORDER = (0, 2, 1, 3)


def _take_heads(w, base, order, axis):
    return jnp.concatenate(
        [lax.slice_in_dim(w, base + h * HEAD_DIM, base + (h + 1) * HEAD_DIM, axis=axis) for h in order],
        axis=axis)


def _w_in_layout(w_in):
    zeros = lambda n: jnp.zeros((D_MODEL, n), w_in.dtype)
    a_q = _take_heads(w_in, 0, _GQA_ORDER, 1)
    a_kv = w_in[:, 256:512]
    b_q = jnp.concatenate(
        [jnp.concatenate([w_in[:, 512 + h * 96:512 + (h + 1) * 96], zeros(32)], axis=1)
         for h in range(B_HEADS)], axis=1)
    b_c = w_in[:, 896:1024]
    b_r = jnp.concatenate([w_in[:, 1024:1056], zeros(96)], axis=1)
    c_q = _take_heads(w_in, 1056, _GQA_ORDER, 1)
    rest = w_in[:, 1312:2336]
    return jnp.concatenate([a_q, a_kv, b_q, b_c, b_r, c_q, rest], axis=1).astype(BF16)


def _w_kv_layout(w_kv_up):
    zeros = jnp.zeros((B_KV_RANK, 64), w_kv_up.dtype)
    nope = [jnp.concatenate([w_kv_up[:, h * 128:h * 128 + 64], zeros], axis=1) for h in range(B_HEADS)]
    val = [w_kv_up[:, h * 128 + 64:(h + 1) * 128] for h in range(B_HEADS)]
    return jnp.concatenate(nope + val, axis=1).astype(BF16)


def _w_out_layout(w_out):
    return jnp.concatenate([_take_heads(w_out, 0, _GQA_ORDER, 0), w_out[256:512],
                            _take_heads(w_out, 512, _GQA_ORDER, 0), w_out[768:1024]], axis=0).astype(BF16)


def _router_rows(v):
    return v.reshape((8, 8) + v.shape[1:]).swapaxes(0, 1).reshape(v.shape)


def kernel(x, c, ctx, c_ctx, ada_w, ada_b, pre_mix_g, post_mix_g, pre_ffn_g, post_ffn_g, w_in, a_q_norm, a_k_norm, b_kv_norm, b_w_kv_up, c_sink, d_rpb, w_out, router_w, router_b, exp_w_gate, exp_w_up, exp_w_down, sh_w_gate, sh_w_up, sh_w_down):
    bsz = x.shape[0]
    depth = ada_w.shape[0]
    xall = jnp.concatenate([x, ctx], axis=1)

    cvec = jnp.zeros((16, D_MODEL), F32).at[:bsz].set(c).at[8].set(c_ctx)
    mod = _ada(cvec, ada_w, ada_b)
    tabs = _rope_tables()
    avg = jnp.asarray(np.kron(np.eye(4, dtype=np.float32), np.full((64, 64), 1.0 / 64, np.float32)), BF16)
    row2 = lambda v: v.reshape(1, -1).astype(F32)

    for l in range(depth):
        with_ctx = l < depth - 1
        mod3 = mod[l].reshape(16, 1, 6 * D_MODEL)

        proj = _inproj(
            xall, mod3, row2(pre_mix_g[l]), _w_in_layout(w_in[l]),
            row2(jnp.tile(a_q_norm[l], 4) * HEAD_DIM ** -0.5), row2(jnp.tile(a_k_norm[l], 2)),
            row2(b_kv_norm[l]), _w_kv_layout(b_w_kv_up[l]), avg, tabs)
        qa, ka, va, qb, kb, vb, qc, kc, vc, qd, kd, vd = proj

        nq = NQ + (CTX_LEN // QBLK if with_ctx else 0)
        full = lambda shape: pl.BlockSpec(shape, lambda b, i: (0,) * len(shape))
        mix_a = _attn_call(_attn_a_kernel, "attn_a", qa, ka, va, [], [], nq // 2, qblk=2 * QBLK)
        mix_b = _attn_call(_attn_b_kernel, "attn_b", qb, kb, vb, [], [], nq // 2, qblk=2 * QBLK)
        sink = jnp.zeros((8, LANES), F32).at[:4].set(
            jnp.broadcast_to(c_sink[l][np.array(_GQA_ORDER)][:, None], (4, LANES)))
        mix_c = _attn_call(_attn_c_kernel, "attn_c", qc, kc, vc, [sink], [full((8, LANES))], nq)
        bias = _na_bias(d_rpb[l])
        mix_d = _attn_call(
            _attn_d_kernel, "attn_d", qd, kd, vd, [bias],
            [pl.BlockSpec((None, D_HEADS, QBLK, NA_WIN), lambda b, i: (_na_pattern(i), 0, 0, 0))], nq)

        nb = (TOK if with_ctx else SEQ) // ROW_BLK
        rb = jnp.broadcast_to(_router_rows(router_b[l])[:, None], (N_EXPERTS, ROW_BLK)).astype(F32)
        xn, h2b, h2t, wt_t, ls_t, cnt_t = _outproj(
            (mix_a, mix_b, mix_c, mix_d), _w_out_layout(w_out[l]), xall, mod3,
            row2(post_mix_g[l]), row2(pre_ffn_g[l]),
            _router_rows(router_w[l].T).astype(BF16), rb, nb)

        t_used = nb * ROW_BLK
        n_tok = bsz * t_used
        n_steps = bsz * nb
        n_blocks = -(-(n_tok * TOP_K + n_steps * N_EXPERTS * (PIECE - 1) + N_EXPERTS * (MOE_BLK - 1))
                     // MOE_BLK)
        npieces, piece_dst, tail_dst, ntail, block_e, n_used = _moe_plan(
            cnt_t[..., 0].reshape(n_steps, N_EXPERTS), n_blocks)
        lslot = ls_t.reshape(n_steps, 1, TOP_K * ROW_BLK)
        wts = wt_t.reshape(n_steps, 1, TOP_K * ROW_BLK)
        stg_row = lslot * TILE_ROWS
        ring = (jnp.arange(n_steps, dtype=jnp.int32) % 3)[:, None, None] * PK_STG_ROWS
        pair_row = (lslot // 2) * TILE_ROWS + ring
        w_signed = jnp.where(lslot % 2 == 0, wts, -wts)

        x_sorted = _dispatch(npieces, n_used, ntail, stg_row, piece_dst, tail_dst,
                             h2t.reshape(n_tok * TILE_ROWS, LANES), n_steps, n_blocks)
        y_sorted = _experts(block_e, n_used, x_sorted, exp_w_gate, exp_w_up, exp_w_down, l, n_blocks)
        xall = _combine(
            npieces, pair_row, w_signed, piece_dst, y_sorted, h2b.reshape(n_tok, D_MODEL),
            xn.reshape(n_tok, D_MODEL), mod3, row2(post_ffn_g[l]),
            sh_w_gate[l].astype(BF16), sh_w_up[l].astype(BF16), sh_w_down[l].astype(BF16),
            n_steps, nb).reshape(bsz, t_used, D_MODEL)

    return xall[:, :SEQ]
```

```python
import functools

import numpy as np
import jax
import jax.numpy as jnp
from jax import lax
from jax.experimental import pallas as pl
from jax.experimental.pallas import tpu as pltpu

F32 = jnp.float32
BF16 = jnp.bfloat16
U32 = jnp.uint32

D_MODEL = 1024
SEQ = 2048
CTX_LEN = 256
TOK = SEQ + CTX_LEN
GRID_W = 64
HEAD_DIM = 64
ROPE_THETA = 10000.0
EPS = 1e-6
NEG = -1e30

A_HEADS, A_KV_HEADS = 4, 2
B_HEADS, B_NOPE, B_ROPE, B_V, B_KV_RANK = 4, 64, 32, 64, 128
C_HEADS, C_KV_HEADS, C_WINDOW = 4, 2, 128
D_HEADS, NA_ROWS, NA_COLS = 4, 8, 16

N_EXPERTS, TOP_K, N_GROUPS, TOPK_GROUPS = 64, 8, 8, 4
EXPERT_DIM, SHARED_DIM = 256, 256
ROUTED_SCALE = 2.5

LANES = 128
QBLK = 128
ROW_BLK = 256
IN_BLK = 768
MOE_BLK = 768
TILE_ROWS = D_MODEL // LANES
PIECE = 8
PIECE_ROWS = PIECE * TILE_ROWS
PIECES_MAX = 320
STG_ROWS = PIECES_MAX * PIECE_ROWS
PK_PIECE_ROWS = PIECE_ROWS // 2
PK_STG_ROWS = STG_ROWS // 2
PK_BLK_ROWS = MOE_BLK * TILE_ROWS // 2
TAIL_MAX = N_EXPERTS * MOE_BLK // PIECE
LOG_PACK_GROUP = 2
PACK_GROUP = 1 << LOG_PACK_GROUP
LOG_WAIT_GROUP = 4
WAIT_GROUP = 1 << LOG_WAIT_GROUP
LOG_CHUNKS = 2
N_CHUNKS = 1 << LOG_CHUNKS
VMEM_LIMIT = 56 * 1024 * 1024
KEY_CHUNK = 768
NQ = SEQ // QBLK
NA_WIN = 10 * GRID_W

_AQ, _AK, _AV, _BQ, _BC, _BR = 0, 256, 384, 512, 1024, 1152
_CQ, _CK, _CV, _DQ, _DK, _DV = 1280, 1536, 1664, 1792, 2048, 2304
IN_W = 2560

_NT = (((1,), (1,)), ((), ()))


def _sigmoid(x):
    return 1.0 / (1.0 + jnp.exp(-x))


def _rms(x, g):
    return x * lax.rsqrt(jnp.mean(x * x, axis=-1, keepdims=True) + EPS) * g


def _dot(a, b):
    return jnp.dot(a, b, preferred_element_type=F32)


def _dot_nt(a, b):
    return lax.dot_general(a, b, _NT, preferred_element_type=F32)


def _ada_kernel(c_ref, w_ref, b_ref, o_ref):
    c = c_ref[...]
    a = (c * _sigmoid(c)).astype(BF16)
    o_ref[...] = _dot(a, w_ref[...].astype(BF16)) + b_ref[...]


def _ada(cvec, ada_w, ada_b):
    depth = ada_w.shape[0]
    n = ada_w.shape[2]
    tn = 1536
    return pl.pallas_call(
        _ada_kernel,
        grid=(depth, n // tn),
        in_specs=[
            pl.BlockSpec((16, D_MODEL), lambda l, j: (0, 0)),
            pl.BlockSpec((None, D_MODEL, tn), lambda l, j: (l, 0, j)),
            pl.BlockSpec((None, 1, tn), lambda l, j: (l, 0, j)),
        ],
        out_specs=pl.BlockSpec((None, 16, tn), lambda l, j: (l, 0, j)),
        out_shape=jax.ShapeDtypeStruct((depth, 16, n), F32),
        name="ada",
    )(cvec, ada_w, ada_b.reshape(depth, 1, n))


def _rope(x, cos, sin, chunk):
    outs = []
    for c in range(x.shape[1] // LANES):
        sl = slice(c * LANES, (c + 1) * LANES)
        xs = x[:, sl]
        lane = lax.broadcasted_iota(jnp.int32, xs.shape, 1)
        first = (lane & chunk) == 0
        rot = jnp.where(first, pltpu.roll(xs, LANES - chunk, 1), pltpu.roll(xs, chunk, 1))
        outs.append(xs * cos[:, sl] + rot * sin[:, sl])
    return outs[0] if len(outs) == 1 else jnp.concatenate(outs, axis=1)


def _head_norm(x, avg):
    sq = x * x
    hi = sq.astype(BF16)
    lo = (sq - hi.astype(F32)).astype(BF16)
    ms = _dot(hi, avg) + _dot(lo, avg)
    return x * lax.rsqrt(ms + EPS)


def _inproj_kernel(x_ref, mod_ref, modc_ref, g_ref, w_ref, qg_ref, kg_ref, kvg_ref, wkv_ref, avg_ref,
                   cosh_ref, sinh_ref, cosb_ref, sinb_ref, cosr_ref, sinr_ref,
                   qa_ref, ka_ref, va_ref, qb_ref, kb_ref, vb_ref,
                   qc_ref, kc_ref, vc_ref, qd_ref, kd_ref, vd_ref):
    d = D_MODEL
    n = x_ref.shape[0]
    is_ctx = pl.program_id(1) * n + lax.broadcasted_iota(jnp.int32, (n, 1), 0) >= SEQ
    scale = jnp.where(is_ctx, modc_ref[:, d:2 * d], mod_ref[:, d:2 * d])
    shift = jnp.where(is_ctx, modc_ref[:, 0:d], mod_ref[:, 0:d])
    h = _rms(x_ref[...], g_ref[...]) * (1.0 + scale) + shift
    p = _dot(h.astype(BF16), w_ref[...])
    cosh, sinh = cosh_ref[...], sinh_ref[...]
    avg = avg_ref[...]

    qa = _head_norm(p[:, _AQ:_AQ + 256], avg) * qg_ref[...]
    qa_ref[...] = _rope(qa, cosh, sinh, 16).astype(BF16)
    ka = _head_norm(p[:, _AK:_AK + 128], avg[0:128, 0:128]) * kg_ref[...]
    ka_ref[...] = _rope(ka, cosh[:, 0:128], sinh[:, 0:128], 16).astype(BF16)
    ones = jnp.ones((p.shape[0], LANES), BF16)
    va_ref[...] = jnp.concatenate([p[:, _AV:_AV + 128].astype(BF16), ones], axis=1)

    qb_ref[...] = _rope(p[:, _BQ:_BQ + 512], cosb_ref[...], sinb_ref[...], 8).astype(BF16)
    cn = _rms(p[:, _BC:_BC + 128], kvg_ref[...])
    kv = _dot(cn.astype(BF16), wkv_ref[...])
    kr = _rope(p[:, _BR:_BR + 128], cosr_ref[...], sinr_ref[...], 8)
    kr = pltpu.roll(kr, 64, 1)
    kb_ref[...] = jnp.concatenate(
        [kv[:, hh * 128:(hh + 1) * 128] + kr for hh in range(B_HEADS)], axis=1).astype(BF16)
    vb_ref[...] = jnp.concatenate(
        [kv[:, 512:640].astype(BF16), ones, kv[:, 640:768].astype(BF16), ones], axis=1)

    qc_ref[...] = _rope(p[:, _CQ:_CQ + 256] * 0.125, cosh, sinh, 16).astype(BF16)
    kc_ref[...] = _rope(p[:, _CK:_CK + 128], cosh[:, 0:128], sinh[:, 0:128], 16).astype(BF16)
    vc_ref[...] = jnp.concatenate([p[:, _CV:_CV + 128].astype(BF16), ones], axis=1)

    qd_ref[...] = (p[:, _DQ:_DQ + 256] * 0.125).astype(BF16)
    kd_ref[...] = p[:, _DK:_DK + 256].astype(BF16)
    vd_ref[...] = jnp.concatenate(
        [p[:, _DV:_DV + 128].astype(BF16), ones, p[:, _DV + 128:_DV + 256].astype(BF16), ones], axis=1)


def _mod_row(j, b, n_lat_blocks):
    return jnp.where(j < n_lat_blocks, b, 8)


def _inproj(xall, mod3, g, w, qg, kg, kvg, wkv, avg, tabs):
    bsz = xall.shape[0]
    nb = TOK // IN_BLK
    full = lambda shape: pl.BlockSpec(shape, lambda b, j: (0,) * len(shape))
    row = lambda w_: pl.BlockSpec((IN_BLK, w_), lambda b, j: (j, 0))
    out = lambda w_: pl.BlockSpec((None, IN_BLK, w_), lambda b, j: (b, j, 0))
    widths = (256, 128, 256, 512, 512, 512, 256, 128, 256, 256, 256, 512)
    return pl.pallas_call(
        _inproj_kernel,
        grid=(bsz, nb),
        in_specs=[
            pl.BlockSpec((None, IN_BLK, D_MODEL), lambda b, j: (b, j, 0)),
            pl.BlockSpec((None, 1, 6 * D_MODEL), lambda b, j: (b, 0, 0)),
            pl.BlockSpec((None, 1, 6 * D_MODEL), lambda b, j: (8, 0, 0)),
            full((1, D_MODEL)), full((D_MODEL, IN_W)),
            full((1, 256)), full((1, 128)), full((1, 128)), full((128, 768)), full((256, 256)),
            row(256), row(256), row(512), row(512), row(128), row(128),
        ],
        out_specs=[out(w_) for w_ in widths],
        out_shape=[jax.ShapeDtypeStruct((bsz, TOK, w_), BF16) for w_ in widths],
        compiler_params=pltpu.CompilerParams(vmem_limit_bytes=VMEM_LIMIT),
        name="inproj",
    )(xall, mod3, mod3, g, w, qg, kg, kvg, wkv, avg, *tabs)


def _softmax_pv(segs, sink=None):
    m = None
    for s, _ in segs:
        ms = jnp.max(s, axis=-1, keepdims=True)
        m = ms if m is None else jnp.maximum(m, ms)
    if sink is not None:
        m = jnp.maximum(m, sink)
    acc = None
    for s, v1 in segs:
        r = _dot(jnp.exp((s - m).astype(BF16)), v1)
        acc = r if acc is None else acc + r
    l = acc[:, LANES:LANES + 1]
    if sink is not None:
        l = l + jnp.exp(sink - m)
    return acc[:, 0:LANES] / l


def _lane_lo(dtype_shape):
    return lax.broadcasted_iota(jnp.int32, dtype_shape, 1) < HEAD_DIM


def _half(qt, hh):
    lo = _lane_lo(qt.shape)
    return jnp.where(lo if hh == 0 else jnp.logical_not(lo), qt, jnp.zeros_like(qt))


def _merge_halves(o0, o1):
    return jnp.where(_lane_lo(o0.shape), o0, o1)


def _stack_heads(q_ref, tiles):
    return jnp.concatenate(
        [_half(q_ref[:, t * LANES:(t + 1) * LANES], hh) for t in tiles for hh in range(2)], axis=0)


def _store_heads(o_ref, o, tiles):
    n = o_ref.shape[0]
    for ti, t in enumerate(tiles):
        o0, o1 = o[2 * ti * n:(2 * ti + 1) * n], o[(2 * ti + 1) * n:(2 * ti + 2) * n]
        o_ref[:, t * LANES:(t + 1) * LANES] = _merge_halves(o0, o1).astype(o_ref.dtype)


def _online_softmax_pv(q, k, v1, chunk):
    m = acc = None
    for c0 in range(0, k.shape[0], chunk):
        s = _dot_nt(q, k[c0:c0 + chunk])
        ms = jnp.max(s, axis=-1, keepdims=True)
        m_new = ms if m is None else jnp.maximum(m, ms)
        r = _dot(jnp.exp((s - m_new).astype(BF16)), v1[c0:c0 + chunk])
        acc = r if acc is None else acc * jnp.exp(m - m_new) + r
        m = m_new
    return acc[:, 0:LANES] / acc[:, LANES:LANES + 1]


def _attn_a_kernel(q_ref, k_ref, v_ref, o_ref, *, n_lat):
    def run(k, v1):
        for t in range(2):
            o = [_online_softmax_pv(_half(q_ref[:, t * LANES:(t + 1) * LANES], hh), k, v1, KEY_CHUNK)
                 for hh in range(2)]
            o_ref[:, t * LANES:(t + 1) * LANES] = _merge_halves(o[0], o[1]).astype(o_ref.dtype)

    @pl.when(pl.program_id(1) < n_lat)
    def _():
        run(k_ref[...], v_ref[...])

    @pl.when(pl.program_id(1) >= n_lat)
    def _():
        run(k_ref[SEQ:TOK, :], v_ref[SEQ:TOK, :])


def _attn_b_kernel(q_ref, k_ref, v_ref, o_ref, *, n_lat):
    def run(lo_, hi_):
        for t in range(2):
            v1 = v_ref[lo_:hi_, t * 256:(t + 1) * 256]
            o = []
            for hh in range(2):
                hd = 2 * t + hh
                q = q_ref[:, hd * LANES:(hd + 1) * LANES]
                k = k_ref[lo_:hi_, hd * LANES:(hd + 1) * LANES]
                o.append(_softmax_pv([(_dot_nt(q, k), v1)]))
            o_ref[:, t * LANES:(t + 1) * LANES] = _merge_halves(o[0], o[1]).astype(o_ref.dtype)

    @pl.when(pl.program_id(1) < n_lat)
    def _():
        run(0, TOK)

    @pl.when(pl.program_id(1) >= n_lat)
    def _():
        run(SEQ, TOK)


def _attn_c_kernel(q_ref, k_ref, v_ref, sink_ref, o_ref, *, n_lat):
    i = pl.program_id(1)
    qblk = o_ref.shape[0]
    band = 3 * qblk

    def run(latent):
        if latent:
            start = pl.multiple_of(jnp.clip((i - 1) * qblk, 0, SEQ - band), qblk)
            shape = (2 * qblk, band)
            qpos = i * qblk + (lax.broadcasted_iota(jnp.int32, shape, 0) & (qblk - 1))
            kpos = start + lax.broadcasted_iota(jnp.int32, shape, 1)
            valid = jnp.abs(qpos - kpos) <= C_WINDOW
        for t in range(2):
            q = _stack_heads(q_ref, (t,))
            segs = [(_dot_nt(q, k_ref[SEQ:TOK, :]), v_ref[SEQ:TOK, :])]
            if latent:
                s = jnp.where(valid, _dot_nt(q, k_ref[pl.ds(start, band), :]), NEG)
                segs.insert(0, (s, v_ref[pl.ds(start, band), :]))
            sink = jnp.concatenate(
                [jnp.broadcast_to(sink_ref[j:j + 1, 0:1], (qblk, 1)) for j in (2 * t, 2 * t + 1)], axis=0)
            _store_heads(o_ref, _softmax_pv(segs, sink=sink), (t,))

    @pl.when(i < n_lat)
    def _():
        run(True)

    @pl.when(i >= n_lat)
    def _():
        run(False)


def _attn_d_kernel(q_ref, k_ref, v_ref, bias_ref, o_ref, *, n_lat):
    i = pl.program_id(1)
    qblk = o_ref.shape[0]

    def run(latent):
        if latent:
            start = pl.multiple_of(jnp.clip((i - 2) * qblk, 0, SEQ - NA_WIN), qblk)
        for t in range(2):
            ksl, vsl = slice(t * LANES, (t + 1) * LANES), slice(t * 256, (t + 1) * 256)
            q = _stack_heads(q_ref, (t,))
            segs = [(_dot_nt(q, k_ref[SEQ:TOK, ksl]), v_ref[SEQ:TOK, vsl])]
            if latent:
                bias = bias_ref[2 * t:2 * t + 2].reshape(2 * qblk, NA_WIN)
                segs.insert(0, (_dot_nt(q, k_ref[pl.ds(start, NA_WIN), ksl]) + bias,
                                v_ref[pl.ds(start, NA_WIN), vsl]))
            _store_heads(o_ref, _softmax_pv(segs), (t,))

    @pl.when(i < n_lat)
    def _():
        run(True)

    @pl.when(i >= n_lat)
    def _():
        run(False)


def _attn_call(kernel, name, q, k, v, extra, extra_specs, n_blocks, qblk=QBLK):
    bsz = q.shape[0]
    return pl.pallas_call(
        functools.partial(kernel, n_lat=SEQ // qblk),
        grid=(bsz, n_blocks),
        in_specs=[
            pl.BlockSpec((None, qblk, q.shape[2]), lambda b, i: (b, i, 0)),
            pl.BlockSpec((None, TOK, k.shape[2]), lambda b, i: (b, 0, 0)),
            pl.BlockSpec((None, TOK, v.shape[2]), lambda b, i: (b, 0, 0)),
        ] + extra_specs,
        out_specs=pl.BlockSpec((None, qblk, 256), lambda b, i: (b, i, 0)),
        out_shape=jax.ShapeDtypeStruct((bsz, n_blocks * qblk, 256), BF16),
        name=name,
    )(q, k, v, *extra)


def _na_pattern(i):
    return jnp.where(i < 2, i, jnp.where(i < NQ - 2, 2, jnp.minimum(i, NQ - 1) - (NQ - 5)))


def _na_bias(rpb):
    rows = SEQ // GRID_W
    n_dr, n_dc = 2 * NA_ROWS - 1, 2 * NA_COLS - 1
    qc = np.arange(GRID_W)
    dc = np.clip(qc[None, :] - qc[:, None] + NA_COLS - 1, 0, n_dc - 1)
    sel_c = jnp.asarray(np.eye(n_dc, dtype=np.float32)[dc])
    cs = np.clip(qc - NA_COLS // 2, 0, GRID_W - NA_COLS)[:, None]
    col_ok = (qc[None, :] >= cs) & (qc[None, :] < cs + NA_COLS)
    sel_r, row_ok = [], []
    for i in (0, 1, 2, NQ - 2, NQ - 1):
        start_row = min(max(2 * i - 4, 0), rows - 10)
        qr = 2 * i + np.arange(2)
        kr = start_row + np.arange(10)
        rs = np.clip(qr - NA_ROWS // 2, 0, rows - NA_ROWS)[:, None]
        row_ok.append((kr[None] >= rs) & (kr[None] < rs + NA_ROWS))
        dr = np.clip(kr[None] - qr[:, None] + NA_ROWS - 1, 0, n_dr - 1)
        sel_r.append(np.eye(n_dr, dtype=np.float32)[dr])
    b = jnp.einsum("hrc,pabr,qkc->phaqbk", rpb.astype(F32), jnp.asarray(np.stack(sel_r)), sel_c,
                   precision=lax.Precision.HIGHEST)
    valid = np.stack(row_ok)[:, None, :, None, :, None] & col_ok[None, None, None, :, None, :]
    return jnp.where(valid, b, NEG).reshape(5, D_HEADS, QBLK, NA_WIN)


def _outproj_kernel(ma_ref, mb_ref, mc_ref, md_ref, w_ref, x_ref, mod_ref, g1_ref, g2_ref,
                    wr_ref, rb_ref, xn_ref, h2b_ref, h2t_ref, wt_ref, ls_ref, cnt_ref):
    d = D_MODEL
    mix = jnp.concatenate([ma_ref[...], mb_ref[...], mc_ref[...], md_ref[...]], axis=1)
    y = _rms(_dot(mix, w_ref[...]), g1_ref[...])
    xn = x_ref[...] + mod_ref[:, 2 * d:3 * d] * y
    xn_ref[...] = xn
    h2 = _rms(xn, g2_ref[...]) * (1.0 + mod_ref[:, 4 * d:5 * d]) + mod_ref[:, 3 * d:4 * d]
    n = h2.shape[0]
    h2b_ref[...] = h2.astype(BF16)
    for s in range(TILE_ROWS):
        h2t_ref[pl.ds(s, n, stride=TILE_ROWS), :] = h2[:, s * LANES:(s + 1) * LANES]

    s = _sigmoid(_dot_nt(wr_ref[...], h2.astype(BF16)))
    sb = s + rb_ref[...]
    per = N_EXPERTS // N_GROUPS
    slab = [sb[j * 8:(j + 1) * 8, :] for j in range(per)]
    sraw = [s[j * 8:(j + 1) * 8, :] for j in range(per)]
    giota = lax.broadcasted_iota(jnp.int32, (N_GROUPS, n), 0)
    eid = [giota * per + j for j in range(per)]
    ninf = -jnp.inf

    m1 = functools.reduce(jnp.maximum, slab)
    j1 = functools.reduce(jnp.minimum, [jnp.where(slab[j] == m1, j, per) for j in range(per)])
    m2 = functools.reduce(jnp.maximum, [jnp.where(j1 == j, ninf, slab[j]) for j in range(per)])
    gs = m1 + m2

    gsel = jnp.zeros((N_GROUPS, n), jnp.bool_)
    for _ in range(TOPK_GROUPS):
        m = jnp.max(gs, axis=0, keepdims=True)
        gi = jnp.min(jnp.where(gs == m, giota, N_GROUPS), axis=0, keepdims=True)
        hit = giota == gi
        gsel = jnp.logical_or(gsel, hit)
        gs = jnp.where(hit, ninf, gs)
    slab = [jnp.where(gsel, sj, ninf) for sj in slab]

    def pick(hits, vals):
        return jnp.sum(functools.reduce(jnp.add, [jnp.where(hits[j], vals[j], 0.0) for j in range(per)]),
                       axis=0, keepdims=True)

    kiota = lax.broadcasted_iota(jnp.int32, (TOP_K, n), 0)
    wts = jnp.zeros((TOP_K, n), F32)
    chosen = []
    for k in range(TOP_K):
        m = jnp.max(functools.reduce(jnp.maximum, slab), axis=0, keepdims=True)
        cand = functools.reduce(jnp.minimum,
                                [jnp.where(slab[j] == m, eid[j], N_EXPERTS) for j in range(per)])
        ei = jnp.min(cand, axis=0, keepdims=True)
        hits = [eid[j] == ei for j in range(per)]
        slab = [jnp.where(hits[j], ninf, slab[j]) for j in range(per)]
        wts = jnp.where(kiota == k, pick(hits, sraw), wts)
        chosen.append(ei)
    wt_ref[...] = wts / jnp.sum(wts, axis=0, keepdims=True) * ROUTED_SCALE

    hits_k = [[eid[j] == chosen[k] for j in range(per)] for k in range(TOP_K)]
    sel = jnp.concatenate(
        [functools.reduce(jnp.add, [jnp.where(hits_k[k][j], 1.0, 0.0) for k in range(TOP_K)])
         for j in range(per)], axis=0)
    tri = jnp.where(lax.broadcasted_iota(jnp.int32, (n, n), 0) < lax.broadcasted_iota(jnp.int32, (n, n), 1),
                    1.0, 0.0).astype(BF16)
    before = _dot(sel.astype(BF16), tri)
    cnt = jnp.sum(sel, axis=1, keepdims=True)
    run = jnp.floor((cnt + (PIECE - 1.0)) * (1.0 / PIECE)) * PIECE
    low = jnp.where(lax.broadcasted_iota(jnp.int32, (N_EXPERTS, N_EXPERTS), 1)
                    < lax.broadcasted_iota(jnp.int32, (N_EXPERTS, N_EXPERTS), 0), 1.0, 0.0).astype(BF16)
    start = _dot(low, jnp.broadcast_to(run, (N_EXPERTS, LANES)).astype(BF16))[:, 0:1]
    base = before + start
    base = [base[j * 8:(j + 1) * 8, :] for j in range(per)]
    lsl = jnp.zeros((TOP_K, n), F32)
    for k in range(TOP_K):
        lsl = jnp.where(kiota == k, pick(hits_k[k], base), lsl)
    ls_ref[...] = lsl.astype(jnp.int32)
    cnt_ref[...] = jnp.broadcast_to(cnt, (N_EXPERTS, LANES)).astype(jnp.int32)


def _outproj(mixes, w, xall, mod3, g1, g2, wr, rb, nb):
    bsz = xall.shape[0]
    n_lat = SEQ // ROW_BLK
    full = lambda shape: pl.BlockSpec(shape, lambda b, j: (0,) * len(shape))
    rows = lambda w_: pl.BlockSpec((None, ROW_BLK, w_), lambda b, j: (b, j, 0))
    route = pl.BlockSpec((None, None, TOP_K, ROW_BLK), lambda b, j: (b, j, 0, 0))
    return pl.pallas_call(
        _outproj_kernel,
        grid=(bsz, nb),
        in_specs=[rows(256)] * 4 + [
            full((D_MODEL, D_MODEL)), rows(D_MODEL),
            pl.BlockSpec((None, 1, 6 * D_MODEL), lambda b, j: (_mod_row(j, b, n_lat), 0, 0)),
            full((1, D_MODEL)), full((1, D_MODEL)),
            full((N_EXPERTS, D_MODEL)), full((N_EXPERTS, ROW_BLK)),
        ],
        out_specs=[
            rows(D_MODEL), rows(D_MODEL),
            pl.BlockSpec((None, ROW_BLK * TILE_ROWS, LANES), lambda b, j: (b, j, 0)),
            route, route,
            pl.BlockSpec((None, None, N_EXPERTS, LANES), lambda b, j: (b, j, 0, 0)),
        ],
        out_shape=[
            jax.ShapeDtypeStruct((bsz, nb * ROW_BLK, D_MODEL), F32),
            jax.ShapeDtypeStruct((bsz, nb * ROW_BLK, D_MODEL), BF16),
            jax.ShapeDtypeStruct((bsz, nb * ROW_BLK * TILE_ROWS, LANES), F32),
            jax.ShapeDtypeStruct((bsz, nb, TOP_K, ROW_BLK), F32),
            jax.ShapeDtypeStruct((bsz, nb, TOP_K, ROW_BLK), jnp.int32),
            jax.ShapeDtypeStruct((bsz, nb, N_EXPERTS, LANES), jnp.int32),
        ],
        name="outproj",
    )(*mixes, w, xall, mod3, g1, g2, wr, rb)


def _rows(i, n):
    return pl.ds(pl.multiple_of(i * n, n), n)


def _tile_at(row):
    return pl.ds(pl.multiple_of(row, TILE_ROWS), TILE_ROWS)


def _piece(p):
    return _rows(p, PK_PIECE_ROWS)


def _pack_pairs(lo, hi):
    return (lax.shift_right_logical(pltpu.bitcast(lo, U32), jnp.uint32(16))
            | (pltpu.bitcast(hi, U32) & jnp.uint32(0xFFFF0000)))


def _unpack_lo(u):
    return pltpu.bitcast(lax.shift_left(u, jnp.uint32(16)), F32)


def _unpack_hi(u):
    return pltpu.bitcast(u & jnp.uint32(0xFFFF0000), F32)


def _start_pieces(lo, hi, start):
    n = hi - lo

    def group(q, carry):
        for r in range(8):
            start(lo + 8 * q + r, r % 2)
        return carry
    lax.fori_loop(0, lax.shift_right_logical(n, 3), group, 0)

    def rest(r, carry):
        start(hi - 1 - r, 0)
        return carry
    lax.fori_loop(0, n & 7, rest, 0)


def _wait_pieces(n, piece_rows, copy_of_rows):
    def group(p, carry):
        copy_of_rows(WAIT_GROUP * piece_rows).wait()
        return carry
    lax.fori_loop(0, lax.shift_right_logical(n, LOG_WAIT_GROUP), group, 0)

    def single(p, carry):
        copy_of_rows(piece_rows).wait()
        return carry
    lax.fori_loop(0, n & (WAIT_GROUP - 1), single, 0)


def _interleave(body, n_pieces, start):
    per = ROW_BLK // N_CHUNKS

    def chunk(c, carry):
        for tt in range(per):
            body(c * per + tt)
        _start_pieces(lax.shift_right_logical(c * n_pieces, LOG_CHUNKS),
                      lax.shift_right_logical((c + 1) * n_pieces, LOG_CHUNKS), start)
        return carry
    lax.fori_loop(0, N_CHUNKS, chunk, 0)


def _dispatch_kernel(np_ref, nu_ref, nt_ref, ls_ref, pd_ref, pdp_ref, td_ref, h_ref, xs_hbm,
                     stg, pk, zblk, sem, zsem, *, n_steps, n_blocks):
    i = pl.program_id(0)
    slot = i % 3
    prev = (i + 2) % 3

    def piece_copy(s, p, d):
        return pltpu.make_async_copy(pk.at[s, _piece(p), :], xs_hbm.at[_piece(d), :], sem.at[s])

    def wait_pieces(s, n):
        _wait_pieces(n, PK_PIECE_ROWS, lambda rows: pltpu.make_async_copy(
            pk.at[s, pl.ds(0, rows), :], xs_hbm.at[pl.ds(0, rows), :], sem.at[s]))

    def zero_piece_copy(d):
        return pltpu.make_async_copy(zblk.at[pl.ds(0, PK_PIECE_ROWS), :], xs_hbm.at[_piece(d), :], zsem.at[0])

    def zero_block_copy(b):
        return pltpu.make_async_copy(zblk, xs_hbm.at[_rows(b, PK_BLK_ROWS), :], zsem.at[1])

    @pl.when(i == 0)
    def _():
        zblk[...] = jnp.zeros_like(zblk)

        def zero(c, carry):
            stg[_rows(c, 2048), :] = jnp.zeros((2048, LANES), F32)
            return carry
        lax.fori_loop(0, STG_ROWS // 2048, zero, 0)

        def blk(b, carry):
            zero_block_copy(b).start()
            return carry
        lax.fori_loop(nu_ref[0], n_blocks, blk, 0)

    @pl.when(i >= 3)
    def _():
        wait_pieces(slot, np_ref[jnp.maximum(i - 3, 0)])

    def scatter_token(t):
        tile = h_ref[_rows(t, TILE_ROWS), :].astype(BF16).astype(F32)
        for k in range(TOP_K):
            stg[_tile_at(ls_ref[0, k * ROW_BLK + t]), :] = tile

    np_prev = jnp.where(i >= 1, np_ref[jnp.maximum(i - 1, 0)], 0)
    _interleave(scatter_token, np_prev,
                lambda p, prio: piece_copy(prev, p, pdp_ref[0, p]).start(priority=prio))

    def pack_group(q, carry):
        v = stg[_rows(q, PACK_GROUP * PIECE_ROWS), :].reshape(PACK_GROUP * PIECE // 2, 2 * TILE_ROWS, LANES)
        packed = _pack_pairs(v[:, 0:TILE_ROWS, :], v[:, TILE_ROWS:2 * TILE_ROWS, :])
        pk[slot, _rows(q, PACK_GROUP * PK_PIECE_ROWS), :] = packed.reshape(PACK_GROUP * PK_PIECE_ROWS, LANES)
        return carry
    lax.fori_loop(0, lax.shift_right_logical(np_ref[i] + (PACK_GROUP - 1), LOG_PACK_GROUP), pack_group, 0)

    @pl.when(i == n_steps - 1)
    def _():
        _start_pieces(0, np_ref[i], lambda p, prio: piece_copy(slot, p, pd_ref[0, p]).start(priority=prio))

        def tail(p, carry):
            zero_piece_copy(td_ref[0, p]).start()
            return carry
        lax.fori_loop(0, nt_ref[0], tail, 0)

        wait_pieces((i + 1) % 3, np_ref[jnp.maximum(i - 2, 0)])
        wait_pieces(prev, np_ref[jnp.maximum(i - 1, 0)])
        wait_pieces(slot, np_ref[i])

        _wait_pieces(nt_ref[0], PK_PIECE_ROWS, lambda rows: pltpu.make_async_copy(
            zblk.at[pl.ds(0, rows), :], xs_hbm.at[pl.ds(0, rows), :], zsem.at[0]))

        def blk_wait(b, carry):
            zero_block_copy(0).wait()
            return carry
        lax.fori_loop(nu_ref[0], n_blocks, blk_wait, 0)


def _dispatch(npieces, n_used, ntail, lslot, piece_dst, tail_dst, h2t, n_steps, n_blocks):
    assert n_steps >= 3
    smem = lambda n, imap: pl.BlockSpec((None, 1, n), imap, memory_space=pltpu.SMEM)
    grid_spec = pltpu.PrefetchScalarGridSpec(
        num_scalar_prefetch=3,
        grid=(n_steps,),
        in_specs=[
            smem(TOP_K * ROW_BLK, lambda i, *_: (i, 0, 0)),
            smem(PIECES_MAX, lambda i, *_: (i, 0, 0)),
            smem(PIECES_MAX, lambda i, *_: (jnp.maximum(i - 1, 0), 0, 0)),
            smem(TAIL_MAX, lambda i, *_: (0, 0, 0)),
            pl.BlockSpec((ROW_BLK * TILE_ROWS, LANES), lambda i, *_: (i, 0)),
        ],
        out_specs=pl.BlockSpec(memory_space=pl.ANY),
        scratch_shapes=[
            pltpu.VMEM((STG_ROWS, LANES), F32),
            pltpu.VMEM((3, PK_STG_ROWS, LANES), U32),
            pltpu.VMEM((PK_BLK_ROWS, LANES), U32),
            pltpu.SemaphoreType.DMA((3,)),
            pltpu.SemaphoreType.DMA((2,)),
        ],
    )
    return pl.pallas_call(
        functools.partial(_dispatch_kernel, n_steps=n_steps, n_blocks=n_blocks),
        grid_spec=grid_spec,
        out_shape=jax.ShapeDtypeStruct((n_blocks * PK_BLK_ROWS, LANES), U32),
        compiler_params=pltpu.CompilerParams(dimension_semantics=("arbitrary",),
                                             vmem_limit_bytes=VMEM_LIMIT),
        name="dispatch",
    )(npieces, n_used, ntail, lslot, piece_dst, piece_dst, tail_dst, h2t)


def _from_tiles(ref, n):
    return jnp.concatenate([ref[pl.ds(s, n, stride=TILE_ROWS), :] for s in range(TILE_ROWS)], axis=1)


def _experts_kernel(be_ref, nu_ref, xs_hbm, wg_ref, wu_ref, wd_ref, ys_hbm,
                    xbuf, ybuf, zbuf, wgb, wub, wdb, xsem, ysem, zsem, *, n_blocks):
    i = pl.program_id(0)
    n_used = nu_ref[0]
    rows = PK_BLK_ROWS
    half = rows // 2
    pairs = MOE_BLK // 2

    def halves(b):
        return [pl.ds(pl.multiple_of(b * rows + h * half, half), half) for h in range(2)]

    def x_copies(b, s):
        return [pltpu.make_async_copy(xs_hbm.at[src, :], xbuf.at[s, pl.ds(h * half, half), :], xsem.at[s])
                for h, src in enumerate(halves(b))]

    def y_copies(b, s):
        return [pltpu.make_async_copy(ybuf.at[s, pl.ds(h * half, half), :], ys_hbm.at[dst, :], ysem.at[s])
                for h, dst in enumerate(halves(b))]

    def start(copies):
        for h, c in enumerate(copies):
            c.start(priority=h)

    def wait(copies):
        for c in copies:
            c.wait()

    def zero_fill(b):
        return pltpu.make_async_copy(zbuf, ys_hbm.at[_rows(b, rows), :], zsem.at[0])

    @pl.when(i == 0)
    def _():
        start(x_copies(0, 0))

        @pl.when(n_used > 1)
        def _():
            start(x_copies(1, 1))

        zbuf[...] = jnp.zeros_like(zbuf)

        def fill(b, carry):
            zero_fill(b).start()
            return carry
        lax.fori_loop(n_used, n_blocks, fill, 0)

    @pl.when(i + 2 < n_used)
    def _():
        start(x_copies(i + 2, (i + 2) % 3))

    @pl.when(i < n_used)
    def _():
        wait(x_copies(i, i % 3))

        @pl.when(i >= 2)
        def _():
            wait(y_copies(i - 2, i % 2))

        @pl.when(jnp.logical_or(i == 0, be_ref[i] != be_ref[jnp.maximum(i - 1, 0)]))
        def _():
            wgb[...] = wg_ref[...].astype(BF16)
            wub[...] = wu_ref[...].astype(BF16)
            wdb[...] = wd_ref[...].astype(BF16)

        x = pltpu.bitcast(_from_tiles(xbuf.at[i % 3], pairs), BF16)
        g = _dot(x, wgb[...])
        u = _dot(x, wub[...])
        y = _dot((g * _sigmoid(g) * u).astype(BF16), wdb[...])
        yp = pltpu.bitcast(y.astype(BF16), U32)
        yb = ybuf.at[i % 2]
        for s in range(TILE_ROWS):
            yb[pl.ds(s, pairs, stride=TILE_ROWS), :] = yp[:, s * LANES:(s + 1) * LANES]
        start(y_copies(i, i % 2))

        @pl.when(i == n_used - 1)
        def _():
            @pl.when(i >= 1)
            def _():
                wait(y_copies(i - 1, (i + 1) % 2))
            wait(y_copies(i, i % 2))

            def fill_wait(b, carry):
                zero_fill(0).wait()
                return carry
            lax.fori_loop(n_used, n_blocks, fill_wait, 0)


def _experts(block_e, n_used, x_sorted, wg, wu, wd, layer, n_blocks):
    wspec = lambda shape: pl.BlockSpec((None, None) + shape, lambda i, be, nu: (layer, be[i], 0, 0))
    rows = PK_BLK_ROWS
    grid_spec = pltpu.PrefetchScalarGridSpec(
        num_scalar_prefetch=2,
        grid=(n_blocks,),
        in_specs=[
            pl.BlockSpec(memory_space=pl.ANY),
            wspec((D_MODEL, EXPERT_DIM)), wspec((D_MODEL, EXPERT_DIM)), wspec((EXPERT_DIM, D_MODEL)),
        ],
        out_specs=pl.BlockSpec(memory_space=pl.ANY),
        scratch_shapes=[
            pltpu.VMEM((3, rows, LANES), U32),
            pltpu.VMEM((2, rows, LANES), U32),
            pltpu.VMEM((rows, LANES), U32),
            pltpu.VMEM((D_MODEL, EXPERT_DIM), BF16),
            pltpu.VMEM((D_MODEL, EXPERT_DIM), BF16),
            pltpu.VMEM((EXPERT_DIM, D_MODEL), BF16),
            pltpu.SemaphoreType.DMA((3,)),
            pltpu.SemaphoreType.DMA((2,)),
            pltpu.SemaphoreType.DMA((1,)),
        ],
    )
    return pl.pallas_call(
        functools.partial(_experts_kernel, n_blocks=n_blocks),
        grid_spec=grid_spec,
        out_shape=jax.ShapeDtypeStruct((n_blocks * rows, LANES), U32),
        compiler_params=pltpu.CompilerParams(dimension_semantics=("arbitrary",)),
        name="experts",
    )(block_e, n_used, x_sorted, wg, wu, wd)


def _combine_kernel(np_ref, lp_ref, ws_ref, cur_ref, nx1_ref, nx2_ref, y_hbm, h2b_ref, xn_ref,
                    mod_ref, g_ref, swg_ref, swu_ref, swd_ref, o_ref, stg, acc, sem, *, n_steps):
    i = pl.program_id(0)
    slot = i % 3
    d = D_MODEL

    def piece_copy(s, p, src):
        dst = pl.ds(pl.multiple_of(s * PK_STG_ROWS + p * PK_PIECE_ROWS, PK_PIECE_ROWS), PK_PIECE_ROWS)
        return pltpu.make_async_copy(y_hbm.at[_piece(src), :], stg.at[dst, :], sem.at[s])

    def fetcher(s, pd_ref):
        return lambda p, prio: piece_copy(s, p, pd_ref[0, p]).start(priority=prio)

    @pl.when(i == 0)
    def _():
        _start_pieces(0, np_ref[0], fetcher(0, cur_ref))
        _start_pieces(0, np_ref[1], fetcher(1, nx1_ref))

    _wait_pieces(np_ref[i], PK_PIECE_ROWS, lambda rows: pltpu.make_async_copy(
        y_hbm.at[pl.ds(0, rows), :], stg.at[pl.ds(0, rows), :], sem.at[slot]))

    def gather_token(t):
        a = None
        for k in range(TOP_K):
            j = k * ROW_BLK + t
            pair = stg[_tile_at(lp_ref[0, j]), :]
            w = jnp.full((TILE_ROWS, LANES), ws_ref[0, j], F32)
            v = jnp.where(w < 0.0, _unpack_hi(pair), _unpack_lo(pair)) * jnp.abs(w)
            a = v if a is None else a + v
        acc[_rows(t, TILE_ROWS), :] = a

    np_next = jnp.where(i + 2 < n_steps, np_ref[jnp.minimum(i + 2, n_steps - 1)], 0)
    _interleave(gather_token, np_next, fetcher((i + 2) % 3, nx2_ref))

    hb = h2b_ref[...]
    g = _dot(hb, swg_ref[...])
    u = _dot(hb, swu_ref[...])
    f = _dot((g * _sigmoid(g) * u).astype(BF16), swd_ref[...]) + _from_tiles(acc, ROW_BLK)
    o_ref[...] = xn_ref[...] + mod_ref[:, 5 * d:6 * d] * _rms(f, g_ref[...])


def _combine(npieces, pair_row, w_signed, piece_src, y_sorted, h2b_rows, xn_rows, mod3, g, swg, swu, swd,
             n_steps, per_batch):
    assert n_steps >= 3
    last = n_steps - 1
    full = lambda shape: pl.BlockSpec(shape, lambda n, *_: (0,) * len(shape))
    rows = lambda w_: pl.BlockSpec((ROW_BLK, w_), lambda n, *_: (n, 0))
    smem = lambda w_, imap: pl.BlockSpec((None, 1, w_), imap, memory_space=pltpu.SMEM)
    n_lat = SEQ // ROW_BLK
    grid_spec = pltpu.PrefetchScalarGridSpec(
        num_scalar_prefetch=1,
        grid=(n_steps,),
        in_specs=[
            smem(TOP_K * ROW_BLK, lambda n, *_: (n, 0, 0)),
            smem(TOP_K * ROW_BLK, lambda n, *_: (n, 0, 0)),
            smem(PIECES_MAX, lambda n, *_: (n, 0, 0)),
            smem(PIECES_MAX, lambda n, *_: (jnp.minimum(n + 1, last), 0, 0)),
            smem(PIECES_MAX, lambda n, *_: (jnp.minimum(n + 2, last), 0, 0)),
            pl.BlockSpec(memory_space=pl.ANY),
            rows(D_MODEL), rows(D_MODEL),
            pl.BlockSpec((None, 1, 6 * D_MODEL),
                         lambda n, *_: (_mod_row(n % per_batch, n // per_batch, n_lat), 0, 0)),
            full((1, D_MODEL)),
            full((D_MODEL, SHARED_DIM)), full((D_MODEL, SHARED_DIM)), full((SHARED_DIM, D_MODEL)),
        ],
        out_specs=rows(D_MODEL),
        scratch_shapes=[
            pltpu.VMEM((3 * PK_STG_ROWS, LANES), U32),
            pltpu.VMEM((ROW_BLK * TILE_ROWS, LANES), F32),
            pltpu.SemaphoreType.DMA((3,)),
        ],
    )
    return pl.pallas_call(
        functools.partial(_combine_kernel, n_steps=n_steps),
        grid_spec=grid_spec,
        out_shape=jax.ShapeDtypeStruct(xn_rows.shape, F32),
        compiler_params=pltpu.CompilerParams(dimension_semantics=("arbitrary",),
                                             vmem_limit_bytes=VMEM_LIMIT),
        name="combine",
    )(npieces, pair_row, w_signed, piece_src, piece_src, piece_src, y_sorted, h2b_rows, xn_rows, mod3, g,
      swg, swu, swd)


def _moe_plan(cnt_rows, n_blocks):
    i32 = jnp.int32
    per_blk = MOE_BLK // PIECE
    to_expert = lambda v: v.reshape(v.shape[:-1] + (8, 8)).swapaxes(-1, -2).reshape(v.shape)
    runs = (cnt_rows + PIECE - 1) // PIECE
    loc_end = jnp.cumsum(runs, axis=1)
    loc = loc_end - runs
    npieces = loc_end[:, -1].astype(i32)
    reg = to_expert(runs.sum(axis=0))
    padded = (reg + per_blk - 1) // per_blk * per_blk
    pad_end = jnp.cumsum(padded)
    pad_start = pad_end - padded
    off = to_expert(pad_start)[None, :] + jnp.cumsum(runs, axis=0) - runs
    p = jnp.arange(PIECES_MAX, dtype=i32)[None, :, None]
    mine = (loc[:, None, :] <= p) & (p < loc_end[:, None, :])
    piece_dst = jnp.sum(jnp.where(mine, (off - loc)[:, None, :], 0), axis=-1) + p[:, :, 0]
    tail_n = padded - reg
    t_end = jnp.cumsum(tail_n)
    t_beg = t_end - tail_n
    q = jnp.arange(TAIL_MAX, dtype=i32)[:, None]
    tmine = (t_beg[None, :] <= q) & (q < t_end[None, :])
    tail_dst = jnp.sum(jnp.where(tmine, (pad_start + reg - t_beg)[None, :], 0), axis=-1) + q[:, 0]
    first = jnp.arange(n_blocks, dtype=i32)[:, None] * per_blk
    block_e = jnp.minimum(jnp.sum((pad_end[None, :] <= first).astype(i32), axis=-1), N_EXPERTS - 1)
    n_used = (pad_end[-1] // per_blk).astype(i32).reshape(1)
    return (npieces, piece_dst.astype(i32).reshape(-1, 1, PIECES_MAX),
            tail_dst.astype(i32).reshape(1, 1, TAIL_MAX), t_end[-1].astype(i32).reshape(1),
            block_e.astype(i32), n_used)


def _axial_tables(dim):
    f32 = np.float32
    t = np.arange(SEQ)
    row = (t // GRID_W).astype(f32)
    col = (t % GRID_W).astype(f32)
    quarter = dim // 4
    freqs = (f32(ROPE_THETA) ** (-np.arange(quarter, dtype=f32) / f32(quarter))).astype(f32)
    ar, ac = row[:, None] * freqs, col[:, None] * freqs
    ang = np.concatenate([ar, ar, ac, ac], axis=-1)
    sign = np.where((np.arange(dim) // quarter) % 2 == 0, -1.0, 1.0).astype(f32)
    cos = np.concatenate([np.cos(ang), np.ones((CTX_LEN, dim), f32)], axis=0)
    sin = np.concatenate([np.sin(ang) * sign, np.zeros((CTX_LEN, dim), f32)], axis=0)
    return cos.astype(f32), sin.astype(f32)


def _rope_tables():
    f32 = np.float32
    cos_h, sin_h = _axial_tables(HEAD_DIM)
    cos_r, sin_r = _axial_tables(B_ROPE)
    cosh, sinh = np.tile(cos_h, (1, 4)), np.tile(sin_h, (1, 4))
    scale_b = f32((B_NOPE + B_ROPE) ** -0.5)
    ones, zeros = np.ones((TOK, 64), f32), np.zeros((TOK, 64), f32)
    pad1, pad0 = np.ones((TOK, 32), f32), np.zeros((TOK, 32), f32)
    cosb = np.tile(np.concatenate([ones, cos_r, pad1], axis=1) * scale_b, (1, 4))
    sinb = np.tile(np.concatenate([zeros, sin_r, pad0], axis=1) * scale_b, (1, 4))
    cosr = np.concatenate([cos_r, np.ones((TOK, 96), f32)], axis=1)
    sinr = np.concatenate([sin_r, np.zeros((TOK, 96), f32)], axis=1)
    return tuple(jnp.asarray(a, F32) for a in (cosh, sinh, cosb, sinb, cosr, sinr))


_GQA_ORDER = (0, 2, 1, 3)


def _take_heads(w, base, order, axis):
    return jnp.concatenate(
        [lax.slice_in_dim(w, base + h * HEAD_DIM, base + (h + 1) * HEAD_DIM, axis=axis) for h in order],
        axis=axis)


def _w_in_layout(w_in):
    zeros = lambda n: jnp.zeros((D_MODEL, n), w_in.dtype)
    a_q = _take_heads(w_in, 0, _GQA_ORDER, 1)
    a_kv = w_in[:, 256:512]
    b_q = jnp.concatenate(
        [jnp.concatenate([w_in[:, 512 + h * 96:512 + (h + 1) * 96], zeros(32)], axis=1)
         for h in range(B_HEADS)], axis=1)
    b_c = w_in[:, 896:1024]
    b_r = jnp.concatenate([w_in[:, 1024:1056], zeros(96)], axis=1)
    c_q = _take_heads(w_in, 1056, _GQA_ORDER, 1)
    rest = w_in[:, 1312:2336]
    return jnp.concatenate([a_q, a_kv, b_q, b_c, b_r, c_q, rest], axis=1).astype(BF16)


def _w_kv_layout(w_kv_up):
    zeros = jnp.zeros((B_KV_RANK, 64), w_kv_up.dtype)
    nope = [jnp.concatenate([w_kv_up[:, h * 128:h * 128 + 64], zeros], axis=1) for h in range(B_HEADS)]
    val = [w_kv_up[:, h * 128 + 64:(h + 1) * 128] for h in range(B_HEADS)]
    return jnp.concatenate(nope + val, axis=1).astype(BF16)


def _w_out_layout(w_out):
    return jnp.concatenate([_take_heads(w_out, 0, _GQA_ORDER, 0), w_out[256:512],
                            _take_heads(w_out, 512, _GQA_ORDER, 0), w_out[768:1024]], axis=0).astype(BF16)


def _router_rows(v):
    return v.reshape((8, 8) + v.shape[1:]).swapaxes(0, 1).reshape(v.shape)


def kernel(x, c, ctx, c_ctx, ada_w, ada_b, pre_mix_g, post_mix_g, pre_ffn_g, post_ffn_g, w_in, a_q_norm, a_k_norm, b_kv_norm, b_w_kv_up, c_sink, d_rpb, w_out, router_w, router_b, exp_w_gate, exp_w_up, exp_w_down, sh_w_gate, sh_w_up, sh_w_down):
    bsz = x.shape[0]
    depth = ada_w.shape[0]
    xall = jnp.concatenate([x, ctx], axis=1)

    cvec = jnp.zeros((16, D_MODEL), F32).at[:bsz].set(c).at[8].set(c_ctx)
    mod = _ada(cvec, ada_w, ada_b)
    tabs = _rope_tables()
    avg = jnp.asarray(np.kron(np.eye(4, dtype=np.float32), np.full((64, 64), 1.0 / 64, np.float32)), BF16)
    row2 = lambda v: v.reshape(1, -1).astype(F32)

    for l in range(depth):
        with_ctx = l < depth - 1
        mod3 = mod[l].reshape(16, 1, 6 * D_MODEL)

        proj = _inproj(
            xall, mod3, row2(pre_mix_g[l]), _w_in_layout(w_in[l]),
            row2(jnp.tile(a_q_norm[l], 4) * HEAD_DIM ** -0.5), row2(jnp.tile(a_k_norm[l], 2)),
            row2(b_kv_norm[l]), _w_kv_layout(b_w_kv_up[l]), avg, tabs)
        qa, ka, va, qb, kb, vb, qc, kc, vc, qd, kd, vd = proj

        nq = NQ + (CTX_LEN // QBLK if with_ctx else 0)
        full = lambda shape: pl.BlockSpec(shape, lambda b, i: (0,) * len(shape))
        mix_a = _attn_call(_attn_a_kernel, "attn_a", qa, ka, va, [], [], nq // 2, qblk=2 * QBLK)
        mix_b = _attn_call(_attn_b_kernel, "attn_b", qb, kb, vb, [], [], nq // 2, qblk=2 * QBLK)
        sink = jnp.zeros((8, LANES), F32).at[:4].set(
            jnp.broadcast_to(c_sink[l][np.array(_GQA_ORDER)][:, None], (4, LANES)))
        mix_c = _attn_call(_attn_c_kernel, "attn_c", qc, kc, vc, [sink], [full((8, LANES))], nq)
        bias = _na_bias(d_rpb[l])
        mix_d = _attn_call(
            _attn_d_kernel, "attn_d", qd, kd, vd, [bias],
            [pl.BlockSpec((None, D_HEADS, QBLK, NA_WIN), lambda b, i: (_na_pattern(i), 0, 0, 0))], nq)

        nb = (TOK if with_ctx else SEQ) // ROW_BLK
        rb = jnp.broadcast_to(_router_rows(router_b[l])[:, None], (N_EXPERTS, ROW_BLK)).astype(F32)
        xn, h2b, h2t, wt_t, ls_t, cnt_t = _outproj(
            (mix_a, mix_b, mix_c, mix_d), _w_out_layout(w_out[l]), xall, mod3,
            row2(post_mix_g[l]), row2(pre_ffn_g[l]),
            _router_rows(router_w[l].T).astype(BF16), rb, nb)

        t_used = nb * ROW_BLK
        n_tok = bsz * t_used
        n_steps = bsz * nb
        n_blocks = -(-(n_tok * TOP_K + n_steps * N_EXPERTS * (PIECE - 1) + N_EXPERTS * (MOE_BLK - 1))
                     // MOE_BLK)
        npieces, piece_dst, tail_dst, ntail, block_e, n_used = _moe_plan(
            cnt_t[..., 0].reshape(n_steps, N_EXPERTS), n_blocks)
        lslot = ls_t.reshape(n_steps, 1, TOP_K * ROW_BLK)
        wts = wt_t.reshape(n_steps, 1, TOP_K * ROW_BLK)
        stg_row = lslot * TILE_ROWS
        ring = (jnp.arange(n_steps, dtype=jnp.int32) % 3)[:, None, None] * PK_STG_ROWS
        pair_row = (lslot // 2) * TILE_ROWS + ring
        w_signed = jnp.where(lslot % 2 == 0, wts, -wts)

        x_sorted = _dispatch(npieces, n_used, ntail, stg_row, piece_dst, tail_dst,
                             h2t.reshape(n_tok * TILE_ROWS, LANES), n_steps, n_blocks)
        y_sorted = _experts(block_e, n_used, x_sorted, exp_w_gate, exp_w_up, exp_w_down, l, n_blocks)
        xall = _combine(
            npieces, pair_row, w_signed, piece_dst, y_sorted, h2b.reshape(n_tok, D_MODEL),
            xn.reshape(n_tok, D_MODEL), mod3, row2(post_ffn_g[l]),
            sh_w_gate[l].astype(BF16), sh_w_up[l].astype(BF16), sh_w_down[l].astype(BF16),
            n_steps, nb).reshape(bsz, t_used, D_MODEL)

    return xall[:, :SEQ]
```
